```python
import math
import jax, jax.numpy as jnp
from jax import lax
import numpy as np

D_MODEL = 1024
BATCH = 8
SEQ = 2048
DEPTH = 1
DEC_BATCH = 128
DEC_SEQ = 4
PAST_LEN = 16384
PAGE_SIZE = 128

MIX_DIM = D_MODEL
CONV_DIM = MIX_DIM // 2
CONV_HEADS = 8
CONV_WIDTH = 3
SSM_DIM = MIX_DIM - CONV_DIM
SSM_GROUP = 16
SSM_GROUPS = SSM_DIM // SSM_GROUP
SSM_STATE = 64
IN_DIM = 3 * CONV_DIM + SSM_DIM
N_EXPERT_GROUPS = 4
EXPERTS_PER_GROUP = 8
N_EXPERTS = N_EXPERT_GROUPS * EXPERTS_PER_GROUP
TOP_K = 2
D_EXPERT = D_MODEL // 4
N_MOD = 6
EPS = 1e-6
DT_MIN = 1e-3
DT_MAX = 1e-1

kernel_name = 'hymba_conv_s5_hmoe_adaln_step'


def _rmsnorm(x, g):
    xf = x.astype(jnp.float32)
    y = xf * lax.rsqrt(jnp.mean(xf * xf, axis=-1, keepdims=True) + EPS)
    return (y * g.astype(jnp.float32)).astype(x.dtype)


def _short_conv(v, buf, conv_w):
    s = v.shape[1]
    ext = jnp.concatenate([buf.astype(v.dtype), v], axis=1)
    out = ext[:, 0:s] * conv_w[0]
    for k in range(1, CONV_WIDTH):
        out = out + ext[:, k:k + s] * conv_w[k]
    return out, ext[:, s:]


def _complex_affine_combine(e1, e2):
    a1r, a1i, b1r, b1i = e1
    a2r, a2i, b2r, b2i = e2
    ar = a2r * a1r - a2i * a1i
    ai = a2r * a1i + a2i * a1r
    br = a2r * b1r - a2i * b1i + b2r
    bi = a2r * b1i + a2i * b1r + b2i
    return ar, ai, br, bi


def _s5(u, h0_re, h0_im, lambda_re, lambda_im, log_dt, b_re, b_im, c_re, c_im, d_skip, w_glu, b_glu):
    f32 = jnp.float32
    bsz, s, _ = u.shape
    uf = u.astype(f32)
    ug = uf.reshape(bsz, s, SSM_GROUPS, SSM_GROUP)
    lr = lambda_re.astype(f32)
    li = lambda_im.astype(f32)
    dt = jnp.exp(log_dt.astype(f32))[:, None]
    mag = jnp.exp(lr * dt)
    ar = mag * jnp.cos(li * dt)
    ai = mag * jnp.sin(li * dt)
    den = lr * lr + li * li
    fr = ((ar - 1.0) * lr + ai * li) / den
    fi = (ai * lr - (ar - 1.0) * li) / den
    br = b_re.astype(f32)
    bi = b_im.astype(f32)
    bbar_re = fr[..., None] * br - fi[..., None] * bi
    bbar_im = fr[..., None] * bi + fi[..., None] * br
    bu_re = jnp.einsum('bsgh,gph->bsgp', ug, bbar_re)
    bu_im = jnp.einsum('bsgh,gph->bsgp', ug, bbar_im)
    h0r = h0_re.astype(f32)
    h0i = h0_im.astype(f32)
    bu_re = bu_re.at[:, 0].add(ar * h0r - ai * h0i)
    bu_im = bu_im.at[:, 0].add(ar * h0i + ai * h0r)
    a_re = jnp.broadcast_to(ar, bu_re.shape)
    a_im = jnp.broadcast_to(ai, bu_im.shape)
    _, _, xr, xi = lax.associative_scan(_complex_affine_combine, (a_re, a_im, bu_re, bu_im), axis=1)
    y = (jnp.einsum('bsgp,ghp->bsgh', xr, c_re.astype(f32))
         - jnp.einsum('bsgp,ghp->bsgh', xi, c_im.astype(f32)))
    y = y.reshape(bsz, s, SSM_DIM) + d_skip.astype(f32) * uf
    g = jax.nn.gelu(y)
    out = g * jax.nn.sigmoid(g @ w_glu.astype(f32) + b_glu.astype(f32))
    return out.astype(u.dtype), xr[:, -1].astype(h0_re.dtype), xi[:, -1].astype(h0_im.dtype)


def _moe(h, w_rg, b_rg, w_re, b_re, w_gate, w_up, w_down):
    f32 = jnp.float32
    shp = h.shape
    t = h.reshape(-1, shp[-1])
    p_group = jax.nn.softmax((t @ w_rg + b_rg).astype(f32), axis=-1)
    g_idx = jnp.argmax(p_group, axis=-1)
    g_prob = jnp.max(p_group, axis=-1)
    g_onehot = jax.nn.one_hot(g_idx, N_EXPERT_GROUPS, dtype=f32)
    le = (jnp.einsum('td,gde->tge', t, w_re) + b_re).astype(f32)
    le_sel = jnp.einsum('tge,tg->te', le, g_onehot)
    top_v, top_i = lax.top_k(le_sel, TOP_K)
    w_k = jax.nn.softmax(top_v, axis=-1) * g_prob[:, None]
    e_idx = g_idx[:, None] * EXPERTS_PER_GROUP + top_i
    combine = jnp.einsum('tk,tke->te', w_k, jax.nn.one_hot(e_idx, N_EXPERTS, dtype=f32))
    out = jnp.zeros(t.shape, f32)
    for e in range(N_EXPERTS):
        he = jax.nn.silu(t @ w_gate[e]) * (t @ w_up[e])
        out = out + combine[:, e:e + 1] * (he @ w_down[e]).astype(f32)
    return out.astype(h.dtype).reshape(shp)


def _layer(x, c, conv_buf, s_re, s_im, w_ada, b_ada, norm1_g, norm2_g, w_in, conv_w,
           lambda_re, lambda_im, log_dt, ssm_b_re, ssm_b_im, ssm_c_re, ssm_c_im, ssm_d,
           w_glu, b_glu, out_norm_conv_g, out_norm_ssm_g, w_out, w_router_group,
           b_router_group, w_router_expert, b_router_expert, w_expert_gate, w_expert_up,
           w_expert_down):
    mod = jax.nn.silu(c) @ w_ada + b_ada
    sh1, sc1, gt1, sh2, sc2, gt2 = jnp.split(mod[:, None, :], N_MOD, axis=-1)
    h = _rmsnorm(x, norm1_g) * (1 + sc1) + sh1
    z = h @ w_in
    gate_b, gate_c, v_in, u = jnp.split(z, [CONV_DIM, 2 * CONV_DIM, 3 * CONV_DIM], axis=-1)
    conv_out, new_buf = _short_conv(gate_c * v_in, conv_buf, conv_w)
    y_conv = gate_b * conv_out
    y_ssm, new_re, new_im = _s5(u, s_re, s_im, lambda_re, lambda_im, log_dt, ssm_b_re, ssm_b_im,
                                ssm_c_re, ssm_c_im, ssm_d, w_glu, b_glu)
    merged = jnp.concatenate([_rmsnorm(y_conv, out_norm_conv_g), _rmsnorm(y_ssm, out_norm_ssm_g)], axis=-1)
    x = x + gt1 * (merged @ w_out)
    h = _rmsnorm(x, norm2_g) * (1 + sc2) + sh2
    x = x + gt2 * _moe(h, w_router_group, b_router_group, w_router_expert, b_router_expert,
                       w_expert_gate, w_expert_up, w_expert_down)
    return x, new_buf, new_re, new_im


def setup_inputs(seed: int = 0) -> dict:
    key = jax.random.key(seed)
    ks = jax.random.split(key, 40)
    f32 = jnp.float32

    def nrm(k, shape, scale):
        return jax.random.normal(k, shape, f32) * scale

    n_idx = jnp.arange(SSM_STATE, dtype=f32)
    log_dt = jax.random.uniform(ks[15], (DEPTH, SSM_GROUPS), f32, math.log(DT_MIN), math.log(DT_MAX))
    return {
        'x_prompt': nrm(ks[0], (BATCH, SEQ, D_MODEL), 1.0),
        'x_sample': nrm(ks[1], (DEC_BATCH, DEC_SEQ, D_MODEL), 1.0),
        'state_conv': nrm(ks[2], (DEPTH, DEC_BATCH, CONV_WIDTH - 1, CONV_DIM), 1.0),
        'state_ssm_re': nrm(ks[3], (DEPTH, DEC_BATCH, SSM_GROUPS, SSM_STATE), 0.1),
        'state_ssm_im': nrm(ks[4], (DEPTH, DEC_BATCH, SSM_GROUPS, SSM_STATE), 0.1),
        'c_prompt': nrm(ks[5], (BATCH, D_MODEL), 1.0),
        'c_sample': nrm(ks[6], (DEC_BATCH, D_MODEL), 1.0),
        'w_ada': nrm(ks[7], (DEPTH, D_MODEL, N_MOD * D_MODEL), 0.5 * D_MODEL ** -0.5),
        'b_ada': nrm(ks[8], (DEPTH, N_MOD * D_MODEL), 0.01),
        'norm1_g': 1.0 + nrm(ks[9], (DEPTH, D_MODEL), 0.01),
        'norm2_g': 1.0 + nrm(ks[10], (DEPTH, D_MODEL), 0.01),
        'w_in': nrm(ks[11], (DEPTH, D_MODEL, IN_DIM), D_MODEL ** -0.5),
        'conv_w': nrm(ks[12], (DEPTH, CONV_WIDTH, CONV_DIM), CONV_WIDTH ** -0.5),
        'lambda_re': -0.5 + nrm(ks[13], (DEPTH, SSM_GROUPS, SSM_STATE), 0.01),
        'lambda_im': math.pi * n_idx + nrm(ks[14], (DEPTH, SSM_GROUPS, SSM_STATE), 0.01),
        'log_dt': log_dt,
        'ssm_b_re': nrm(ks[16], (DEPTH, SSM_GROUPS, SSM_STATE, SSM_GROUP), (2 * SSM_GROUP) ** -0.5),
        'ssm_b_im': nrm(ks[17], (DEPTH, SSM_GROUPS, SSM_STATE, SSM_GROUP), (2 * SSM_GROUP) ** -0.5),
        'ssm_c_re': nrm(ks[18], (DEPTH, SSM_GROUPS, SSM_GROUP, SSM_STATE), SSM_STATE ** -0.5),
        'ssm_c_im': nrm(ks[19], (DEPTH, SSM_GROUPS, SSM_GROUP, SSM_STATE), SSM_STATE ** -0.5),
        'ssm_d': nrm(ks[20], (DEPTH, SSM_DIM), 1.0),
        'w_glu': nrm(ks[21], (DEPTH, SSM_DIM, SSM_DIM), SSM_DIM ** -0.5),
        'b_glu': nrm(ks[22], (DEPTH, SSM_DIM), 0.01),
        'out_norm_conv_g': 1.0 + nrm(ks[23], (DEPTH, CONV_DIM), 0.01),
        'out_norm_ssm_g': 1.0 + nrm(ks[24], (DEPTH, SSM_DIM), 0.01),
        'w_out': nrm(ks[25], (DEPTH, MIX_DIM, D_MODEL), MIX_DIM ** -0.5),
        'w_router_group': nrm(ks[26], (DEPTH, D_MODEL, N_EXPERT_GROUPS), D_MODEL ** -0.5),
        'b_router_group': nrm(ks[27], (DEPTH, N_EXPERT_GROUPS), 0.01),
        'w_router_expert': nrm(ks[28], (DEPTH, N_EXPERT_GROUPS, D_MODEL, EXPERTS_PER_GROUP), D_MODEL ** -0.5),
        'b_router_expert': nrm(ks[29], (DEPTH, N_EXPERT_GROUPS, EXPERTS_PER_GROUP), 0.01),
        'w_expert_gate': nrm(ks[30], (DEPTH, N_EXPERTS, D_MODEL, D_EXPERT), D_MODEL ** -0.5),
        'w_expert_up': nrm(ks[31], (DEPTH, N_EXPERTS, D_MODEL, D_EXPERT), D_MODEL ** -0.5),
        'w_expert_down': nrm(ks[32], (DEPTH, N_EXPERTS, D_EXPERT, D_MODEL), D_EXPERT ** -0.5),
        'final_norm_g': 1.0 + nrm(ks[33], (D_MODEL,), 0.01),
    }


def reference(x_prompt, x_sample, state_conv, state_ssm_re, state_ssm_im, c_prompt, c_sample,
              w_ada, b_ada, norm1_g, norm2_g, w_in, conv_w, lambda_re, lambda_im, log_dt,
              ssm_b_re, ssm_b_im, ssm_c_re, ssm_c_im, ssm_d, w_glu, b_glu, out_norm_conv_g,
              out_norm_ssm_g, w_out, w_router_group, b_router_group, w_router_expert,
              b_router_expert, w_expert_gate, w_expert_up, w_expert_down, final_norm_g):
    bp = x_prompt.shape[0]
    hp, hs = x_prompt, x_sample
    conv_p, re_p, im_p, conv_s, re_s, im_s = [], [], [], [], [], []
    for l in range(DEPTH):
        lp = (w_ada[l], b_ada[l], norm1_g[l], norm2_g[l], w_in[l], conv_w[l], lambda_re[l],
              lambda_im[l], log_dt[l], ssm_b_re[l], ssm_b_im[l], ssm_c_re[l], ssm_c_im[l], ssm_d[l],
              w_glu[l], b_glu[l], out_norm_conv_g[l], out_norm_ssm_g[l], w_out[l],
              w_router_group[l], b_router_group[l], w_router_expert[l], b_router_expert[l],
              w_expert_gate[l], w_expert_up[l], w_expert_down[l])
        zero_conv = jnp.zeros((bp, CONV_WIDTH - 1, CONV_DIM), x_prompt.dtype)
        zero_ssm = jnp.zeros((bp, SSM_GROUPS, SSM_STATE), x_prompt.dtype)
        hp, cb_p, sr_p, si_p = _layer(hp, c_prompt, zero_conv, zero_ssm, zero_ssm, *lp)
        hs, cb_s, sr_s, si_s = _layer(hs, c_sample, state_conv[l], state_ssm_re[l], state_ssm_im[l], *lp)
        conv_p.append(cb_p); re_p.append(sr_p); im_p.append(si_p)
        conv_s.append(cb_s); re_s.append(sr_s); im_s.append(si_s)
    y_prompt = _rmsnorm(hp, final_norm_g)
    y_sample = _rmsnorm(hs, final_norm_g)
    new_conv_prompt = jnp.stack(conv_p)
    new_ssm_re_prompt = jnp.stack(re_p)
    new_ssm_im_prompt = jnp.stack(im_p)
    new_conv_sample = jnp.stack(conv_s)
    new_ssm_re_sample = jnp.stack(re_s)
    new_ssm_im_sample = jnp.stack(im_s)
    return (y_prompt, y_sample, new_conv_prompt, new_ssm_re_prompt, new_ssm_im_prompt,
            new_conv_sample, new_ssm_re_sample, new_ssm_im_sample)
```

```python
import functools

import jax
import jax.numpy as jnp
from jax import lax
from jax.experimental import pallas as pl
from jax.experimental.pallas import tpu as pltpu

D_MODEL = 1024
CONV_DIM = 512
SSM_DIM = 512
SSM_GROUP = 16
SSM_GROUPS = 32
SSM_STATE = 64
N_STATE = SSM_GROUPS * SSM_STATE
IN_DIM = 2048
N_EXPERT_GROUPS = 4
EXPERTS_PER_GROUP = 8
N_EXPERTS = 32
D_EXPERT = 256
N_MOD = 6
EPS = 1e-6

SUBLANES = 8
LANES = 128
ROUTER_LANES = LANES
SCAN_COLS = 512
VMEM_LIMIT = 60 * 1024 * 1024

F32 = jnp.float32
BF16 = jnp.bfloat16


def _sigmoid(x):
    return 1.0 / (1.0 + jnp.exp(-x))


def _gelu_tanh(x):
    return 0.5 * x * (1.0 + jnp.tanh(0.7978845608028654 * (x + 0.044715 * (x * x * x))))


def _rms(x, g):
    return x * lax.rsqrt(jnp.mean(x * x, axis=-1, keepdims=True) + EPS) * g


def _bdot(a, b):
    return jnp.dot(a.astype(BF16), b.astype(BF16), preferred_element_type=F32)


def _per_seq(v, nb, scale, shift=None):
    rows, d = v.shape
    v3 = v.reshape(rows // nb, nb, d) * scale[None]
    if shift is not None:
        v3 = v3 + shift[None]
    return v3.reshape(rows, d)


def _disc_kernel(lr_ref, li_ref, ldt_ref, brt_ref, bit_ref, ar_ref, ai_ref, bbr_ref, bbi_ref):
    lr = lr_ref[...]
    li = li_ref[...]
    dt = jnp.exp(ldt_ref[...])
    mag = jnp.exp(lr * dt)
    ar = mag * jnp.cos(li * dt)
    ai = mag * jnp.sin(li * dt)
    den = lr * lr + li * li
    fr = ((ar - 1.0) * lr + ai * li) / den
    fi = (ai * lr - (ar - 1.0) * li) / den
    ar_ref[...] = ar
    ai_ref[...] = ai
    brt = brt_ref[...]
    bit = bit_ref[...]
    bbr_ref[...] = fr * brt - fi * bit
    bbi_ref[...] = fr * bit + fi * brt


def _discretise(lambda_re, lambda_im, log_dt, b_re, b_im):
    lr = lambda_re.reshape(1, N_STATE)
    li = lambda_im.reshape(1, N_STATE)
    ldt = jnp.repeat(log_dt, SSM_STATE).reshape(1, N_STATE)
    brt = b_re.transpose(2, 0, 1).reshape(SSM_GROUP, N_STATE)
    bit = b_im.transpose(2, 0, 1).reshape(SSM_GROUP, N_STATE)
    row = jax.ShapeDtypeStruct((1, N_STATE), F32)
    mat = jax.ShapeDtypeStruct((SSM_GROUP, N_STATE), F32)
    return pl.pallas_call(_disc_kernel, out_shape=(row, row, mat, mat), name="s5_discretise")(
        lr, li, ldt, brt, bit)


def _ada_kernel(c_ref, w_ref, b_ref, o_ref):
    c = c_ref[...]
    o_ref[...] = _bdot(c * _sigmoid(c), w_ref[...]) + b_ref[...]


def _adaln(c_all, w_ada, b_ada):
    n = c_all.shape[0]
    nblk = N_MOD
    return pl.pallas_call(
        _ada_kernel,
        grid=(nblk,),
        in_specs=[
            pl.BlockSpec((n, D_MODEL), lambda j: (0, 0)),
            pl.BlockSpec((D_MODEL, D_MODEL), lambda j: (0, j)),
            pl.BlockSpec((1, D_MODEL), lambda j: (0, j)),
        ],
        out_specs=pl.BlockSpec((n, D_MODEL), lambda j: (0, j)),
        out_shape=jax.ShapeDtypeStruct((n, N_MOD * D_MODEL), F32),
        compiler_params=pltpu.CompilerParams(
            dimension_semantics=("arbitrary",), vmem_limit_bytes=VMEM_LIMIT),
        name="adaln",
    )(c_all, w_ada, b_ada.reshape(1, -1))


def _mixer_kernel(nb, steps,
                  x_ref, sh_ref, sc_ref, gt_ref, h0r_ref, h0i_ref, cb0_ref,
                  n1g_ref, win_ref, cw_ref, ar_ref, ai_ref, wbr_ref, wbi_ref, wcr_ref, wci_ref,
                  dsk_ref, wglu_ref, bglu_ref, gcv_ref, gsm_ref, wout_ref,
                  x1_ref, cbo_ref, hro_ref, hio_ref,
                  z_scr, cv_scr, sr_scr, si_scr, hr_scr, hi_scr):
    rows = nb * steps
    i = pl.program_id(0)

    @pl.when(i == 0)
    def _():
        hr_scr[...] = h0r_ref[...]
        hi_scr[...] = h0i_ref[...]
        cv_scr[0:2 * nb, :] = cb0_ref[...]

    x = x_ref[...]
    h = _per_seq(_rms(x, n1g_ref[...]), nb, 1.0 + sc_ref[...], sh_ref[...])
    z_scr[...] = _bdot(h, win_ref[...])

    cv_scr[2 * nb:2 * nb + rows, :] = z_scr[:, CONV_DIM:2 * CONV_DIM] * z_scr[:, 2 * CONV_DIM:3 * CONV_DIM]
    conv = (cv_scr[0:rows, :] * cw_ref[0:1, :]
            + cv_scr[nb:nb + rows, :] * cw_ref[1:2, :]
            + cv_scr[2 * nb:2 * nb + rows, :] * cw_ref[2:3, :])
    y_conv = _rms(z_scr[:, 0:CONV_DIM] * conv, gcv_ref[...])
    carry = cv_scr[rows:rows + 2 * nb, :]
    cv_scr[0:2 * nb, :] = carry
    cbo_ref[...] = carry

    u = z_scr[:, 3 * CONV_DIM:IN_DIM]
    sr_scr[...] = _bdot(u, wbr_ref[...])
    si_scr[...] = _bdot(u, wbi_ref[...])

    for cg in range(N_STATE // SCAN_COLS):
        cols = slice(cg * SCAN_COLS, (cg + 1) * SCAN_COLS)
        a_r = jnp.broadcast_to(ar_ref[:, cols], (SUBLANES, SCAN_COLS))
        a_i = jnp.broadcast_to(ai_ref[:, cols], (SUBLANES, SCAN_COLS))

        def tile_body(bt, _, cols=cols, a_r=a_r, a_i=a_i):
            r0 = pl.multiple_of(bt * SUBLANES, SUBLANES)

            def step(l, hc):
                hr, hi = hc
                row = pl.multiple_of(l * nb + r0, SUBLANES)
                nr = a_r * hr - a_i * hi + sr_scr[pl.ds(row, SUBLANES), cols]
                ni = a_r * hi + a_i * hr + si_scr[pl.ds(row, SUBLANES), cols]
                sr_scr[pl.ds(row, SUBLANES), cols] = nr
                si_scr[pl.ds(row, SUBLANES), cols] = ni
                return nr, ni

            hr, hi = lax.fori_loop(
                0, steps, step,
                (hr_scr[pl.ds(r0, SUBLANES), cols], hi_scr[pl.ds(r0, SUBLANES), cols]),
                unroll=min(steps, 4))
            hr_scr[pl.ds(r0, SUBLANES), cols] = hr
            hi_scr[pl.ds(r0, SUBLANES), cols] = hi
            return 0

        lax.fori_loop(0, nb // SUBLANES, tile_body, 0)

    hro_ref[...] = hr_scr[...]
    hio_ref[...] = hi_scr[...]

    y = _bdot(sr_scr[...], wcr_ref[...]) - _bdot(si_scr[...], wci_ref[...]) + dsk_ref[...] * u
    g = _gelu_tanh(y)
    y_ssm = _rms(g * _sigmoid(_bdot(g, wglu_ref[...]) + bglu_ref[...]), gsm_ref[...])

    mixed = _bdot(y_conv, wout_ref[0:CONV_DIM, :]) + _bdot(y_ssm, wout_ref[CONV_DIM:2 * CONV_DIM, :])
    x1_ref[...] = x + _per_seq(mixed, nb, gt_ref[...])


def _const_spec(shape):
    return pl.BlockSpec(shape, lambda i: (0,) * len(shape), pipeline_mode=pl.Buffered(1))


def _mixer(x_tm, nb, steps_total, chunk, sh, sc, gt, h0r, h0i, cb0, weights):
    rows = nb * chunk
    nchunks = steps_total // chunk
    in_specs = [
        pl.BlockSpec((rows, D_MODEL), lambda i: (i, 0)),
        _const_spec((nb, D_MODEL)), _const_spec((nb, D_MODEL)), _const_spec((nb, D_MODEL)),
        _const_spec((nb, N_STATE)), _const_spec((nb, N_STATE)), _const_spec((2 * nb, CONV_DIM)),
    ] + [_const_spec(w.shape) for w in weights]
    out_specs = [
        pl.BlockSpec((rows, D_MODEL), lambda i: (i, 0)),
        _const_spec((2 * nb, CONV_DIM)), _const_spec((nb, N_STATE)), _const_spec((nb, N_STATE)),
    ]
    out_shape = [
        jax.ShapeDtypeStruct((steps_total * nb, D_MODEL), F32),
        jax.ShapeDtypeStruct((2 * nb, CONV_DIM), F32),
        jax.ShapeDtypeStruct((nb, N_STATE), F32),
        jax.ShapeDtypeStruct((nb, N_STATE), F32),
    ]
    scratch = [
        pltpu.VMEM((rows, IN_DIM), F32),
        pltpu.VMEM((rows + 2 * nb, CONV_DIM), F32),
        pltpu.VMEM((rows, N_STATE), F32),
        pltpu.VMEM((rows, N_STATE), F32),
        pltpu.VMEM((nb, N_STATE), F32),
        pltpu.VMEM((nb, N_STATE), F32),
    ]
    return pl.pallas_call(
        functools.partial(_mixer_kernel, nb, chunk),
        grid=(nchunks,),
        in_specs=in_specs, out_specs=out_specs, out_shape=out_shape,
        scratch_shapes=scratch,
        compiler_params=pltpu.CompilerParams(
            dimension_semantics=("arbitrary",), vmem_limit_bytes=VMEM_LIMIT),
        name="mixer",
    )(x_tm, sh, sc, gt, h0r, h0i, cb0, *weights)


def _moe_kernel(nb, x1_ref, sh_ref, sc_ref, gt_ref, n2g_ref, wr_ref, br_ref,
                wg_ref, wu_ref, wd_ref, fg_ref, y_ref, h_scr, cmb_scr, acc_scr):
    e = pl.program_id(1)

    @pl.when(e == 0)
    def _():
        h = _per_seq(_rms(x1_ref[...], n2g_ref[...]), nb, 1.0 + sc_ref[...], sh_ref[...])
        h_scr[...] = h.astype(BF16)
        lg = jnp.dot(h, wr_ref[...], preferred_element_type=F32,
                     precision=lax.Precision.HIGHEST) + br_ref[...]
        col = lax.broadcasted_iota(jnp.int32, lg.shape, 1).astype(F32)
        neg = jnp.float32(-jnp.inf)
        is_g = col < N_EXPERT_GROUPS
        gl = jnp.where(is_g, lg, neg)
        gmax = jnp.max(gl, axis=-1, keepdims=True)
        gsum = jnp.sum(jnp.where(is_g, jnp.exp(gl - gmax), 0.0), axis=-1, keepdims=True)
        g_prob = 1.0 / gsum
        g_idx = jnp.min(jnp.where(gl == gmax, col, float(ROUTER_LANES)), axis=-1, keepdims=True)
        lo = N_EXPERT_GROUPS + EXPERTS_PER_GROUP * g_idx
        el = jnp.where(col >= lo, jnp.where(col < lo + EXPERTS_PER_GROUP, lg, neg), neg)
        m1 = jnp.max(el, axis=-1, keepdims=True)
        i1 = jnp.min(jnp.where(el == m1, col, float(ROUTER_LANES)), axis=-1, keepdims=True)
        el2 = jnp.where(col == i1, neg, el)
        m2 = jnp.max(el2, axis=-1, keepdims=True)
        i2 = jnp.min(jnp.where(el2 == m2, col, float(ROUTER_LANES)), axis=-1, keepdims=True)
        e21 = jnp.exp(m2 - m1)
        w1 = g_prob / (1.0 + e21)
        w2 = g_prob * e21 / (1.0 + e21)
        cmb_scr[...] = jnp.where(col == i1, w1, 0.0) + jnp.where(col == i2, w2, 0.0)
        acc_scr[...] = jnp.zeros_like(acc_scr)

    h = h_scr[...]
    a = jnp.dot(h, wg_ref[0].astype(BF16), preferred_element_type=F32)
    b = jnp.dot(h, wu_ref[0].astype(BF16), preferred_element_type=F32)
    he = a * _sigmoid(a) * b
    o = _bdot(he, wd_ref[0])
    cmb = cmb_scr[...]
    col = lax.broadcasted_iota(jnp.int32, cmb.shape, 1)
    c_e = jnp.sum(jnp.where(col == e + N_EXPERT_GROUPS, cmb, 0.0), axis=-1, keepdims=True)
    acc_scr[...] += c_e * o

    @pl.when(e == N_EXPERTS - 1)
    def _():
        x2 = x1_ref[...] + _per_seq(acc_scr[...], nb, gt_ref[...])
        y_ref[...] = _rms(x2, fg_ref[...])


def _moe(x1_tm, nb, tb, sh, sc, gt, n2g, wr, br, wg, wu, wd, fg):
    n = x1_tm.shape[0]
    row_spec = pl.BlockSpec((tb, D_MODEL), lambda i, e: (i, 0))

    def cs(shape):
        return pl.BlockSpec(shape, lambda i, e: (0,) * len(shape))

    return pl.pallas_call(
        functools.partial(_moe_kernel, nb),
        grid=(n // tb, N_EXPERTS),
        in_specs=[
            row_spec, cs((nb, D_MODEL)), cs((nb, D_MODEL)), cs((nb, D_MODEL)),
            cs((1, D_MODEL)), cs((D_MODEL, ROUTER_LANES)), cs((1, ROUTER_LANES)),
            pl.BlockSpec((1, D_MODEL, D_EXPERT), lambda i, e: (e, 0, 0)),
            pl.BlockSpec((1, D_MODEL, D_EXPERT), lambda i, e: (e, 0, 0)),
            pl.BlockSpec((1, D_EXPERT, D_MODEL), lambda i, e: (e, 0, 0)),
            cs((1, D_MODEL)),
        ],
        out_specs=row_spec,
        out_shape=jax.ShapeDtypeStruct((n, D_MODEL), F32),
        scratch_shapes=[
            pltpu.VMEM((tb, D_MODEL), BF16),
            pltpu.VMEM((tb, ROUTER_LANES), F32),
            pltpu.VMEM((tb, D_MODEL), F32),
        ],
        compiler_params=pltpu.CompilerParams(
            dimension_semantics=("arbitrary", "arbitrary"), vmem_limit_bytes=VMEM_LIMIT),
        name="moe",
    )(x1_tm, sh, sc, gt, n2g, wr, br, wg, wu, wd, fg)


def _block_diag_in(bbt):
    eye = jnp.eye(SSM_GROUPS, dtype=bbt.dtype)
    blocks = bbt.reshape(SSM_GROUP, SSM_GROUPS, SSM_STATE)
    w = eye[:, None, :, None] * blocks.transpose(1, 0, 2)[:, :, None, :]
    return w.reshape(SSM_DIM, N_STATE)


def _block_diag_out(c):
    eye = jnp.eye(SSM_GROUPS, dtype=c.dtype)
    w = eye[:, None, :, None] * c.transpose(0, 2, 1)[:, :, None, :]
    return w.reshape(N_STATE, SSM_DIM)


def _to_time_major(x):
    b, s, d = x.shape
    return x.transpose(1, 0, 2).reshape(s * b, d)


def _from_time_major(x, b, s):
    return x.reshape(s, b, -1).transpose(1, 0, 2)


def kernel(x_prompt, x_sample, state_conv, state_ssm_re, state_ssm_im, c_prompt, c_sample, w_ada, b_ada, norm1_g, norm2_g, w_in, conv_w, lambda_re, lambda_im, log_dt, ssm_b_re, ssm_b_im, ssm_c_re, ssm_c_im, ssm_d, w_glu, b_glu, out_norm_conv_g, out_norm_ssm_g, w_out, w_router_group, b_router_group, w_router_expert, b_router_expert, w_expert_gate, w_expert_up, w_expert_down, final_norm_g):
    assert w_ada.shape[0] == 1, "single-layer step"
    bp, sp, _ = x_prompt.shape
    bs, ss, _ = x_sample.shape

    ar, ai, bbr, bbi = _discretise(lambda_re[0], lambda_im[0], log_dt[0], ssm_b_re[0], ssm_b_im[0])
    w_br = _block_diag_in(bbr).astype(BF16)
    w_bi = _block_diag_in(bbi).astype(BF16)
    w_cr = _block_diag_out(ssm_c_re[0]).astype(BF16)
    w_ci = _block_diag_out(ssm_c_im[0]).astype(BF16)

    mod = _adaln(jnp.concatenate([c_prompt, c_sample], axis=0), w_ada[0], b_ada[0])
    sh1, sc1, gt1, sh2, sc2, gt2 = [mod[:, k * D_MODEL:(k + 1) * D_MODEL] for k in range(N_MOD)]

    mixer_weights = (
        norm1_g[0].reshape(1, -1), w_in[0].astype(BF16), conv_w[0], ar, ai, w_br, w_bi, w_cr, w_ci,
        ssm_d[0].reshape(1, -1), w_glu[0].astype(BF16), b_glu[0].reshape(1, -1),
        out_norm_conv_g[0].reshape(1, -1), out_norm_ssm_g[0].reshape(1, -1), w_out[0].astype(BF16),
    )

    w_router = jnp.concatenate(
        [w_router_group[0], w_router_expert[0].transpose(1, 0, 2).reshape(D_MODEL, N_EXPERTS),
         jnp.zeros((D_MODEL, ROUTER_LANES - N_EXPERT_GROUPS - N_EXPERTS), F32)], axis=1)
    b_router = jnp.concatenate(
        [b_router_group[0], b_router_expert[0].reshape(-1),
         jnp.zeros((ROUTER_LANES - N_EXPERT_GROUPS - N_EXPERTS,), F32)]).reshape(1, -1)
    moe_weights = (norm2_g[0].reshape(1, -1), w_router, b_router,
                   w_expert_gate[0], w_expert_up[0], w_expert_down[0], final_norm_g.reshape(1, -1))

    def run(x, lo, hi, h0r, h0i, cb0, chunk, tb):
        nb, s, _ = x.shape
        x1, cbo, hro, hio = _mixer(
            _to_time_major(x), nb, s, chunk, sh1[lo:hi], sc1[lo:hi], gt1[lo:hi],
            h0r, h0i, cb0, mixer_weights)
        y = _moe(x1, nb, tb, sh2[lo:hi], sc2[lo:hi], gt2[lo:hi], *moe_weights)
        new_conv = cbo.reshape(2, nb, CONV_DIM).transpose(1, 0, 2)[None]
        new_re = hro.reshape(1, nb, SSM_GROUPS, SSM_STATE)
        new_im = hio.reshape(1, nb, SSM_GROUPS, SSM_STATE)
        return _from_time_major(y, nb, s), new_conv, new_re, new_im

    zero_state = jnp.zeros((bp, N_STATE), F32)
    zero_conv = jnp.zeros((2 * bp, CONV_DIM), F32)
    y_p, conv_p, re_p, im_p = run(x_prompt, 0, bp, zero_state, zero_state, zero_conv, 64, 1024)
    y_s, conv_s, re_s, im_s = run(
        x_sample, bp, bp + bs,
        state_ssm_re[0].reshape(bs, N_STATE), state_ssm_im[0].reshape(bs, N_STATE),
        state_conv[0].transpose(1, 0, 2).reshape(2 * bs, CONV_DIM), ss, 512)
    return (y_p, y_s, conv_p, re_p, im_p, conv_s, re_s, im_s)
```

```python
import functools

import jax
import jax.numpy as jnp
from jax import lax
from jax.experimental import pallas as pl
from jax.experimental.pallas import tpu as pltpu

D_MODEL = 1024
CONV_DIM = 512
SSM_DIM = 512
SSM_GROUP = 16
SSM_GROUPS = 32
SSM_STATE = 64
N_STATE = SSM_GROUPS * SSM_STATE
IN_DIM = 2048
N_EXPERT_GROUPS = 4
EXPERTS_PER_GROUP = 8
N_EXPERTS = 32
D_EXPERT = 256
N_MOD = 6
EPS = 1e-6

SUBLANES = 8
LANES = 128
ROUTER_LANES = LANES
SCAN_COLS = 512
VMEM_LIMIT = 60 * 1024 * 1024

F32 = jnp.float32
BF16 = jnp.bfloat16


def _sigmoid(x):
    return 1.0 / (1.0 + jnp.exp(-x))


def _gelu_tanh(x):
    return 0.5 * x * (1.0 + jnp.tanh(0.7978845608028654 * (x + 0.044715 * (x * x * x))))


def _rms(x, g):
    return x * lax.rsqrt(jnp.mean(x * x, axis=-1, keepdims=True) + EPS) * g


def _bdot(a, b):
    return jnp.dot(a.astype(BF16), b.astype(BF16), preferred_element_type=F32)


def _per_seq(v, nb, scale, shift=None):
    rows, d = v.shape
    v3 = v.reshape(rows // nb, nb, d) * scale[None]
    if shift is not None:
        v3 = v3 + shift[None]
    return v3.reshape(rows, d)


def _disc_kernel(lr_ref, li_ref, ldt_ref, brt_ref, bit_ref, ar_ref, ai_ref, bbr_ref, bbi_ref):
    lr = lr_ref[...]
    li = li_ref[...]
    dt = jnp.exp(ldt_ref[...])
    mag = jnp.exp(lr * dt)
    ar = mag * jnp.cos(li * dt)
    ai = mag * jnp.sin(li * dt)
    den = lr * lr + li * li
    fr = ((ar - 1.0) * lr + ai * li) / den
    fi = (ai * lr - (ar - 1.0) * li) / den
    ar_ref[...] = ar
    ai_ref[...] = ai
    brt = brt_ref[...]
    bit = bit_ref[...]
    bbr_ref[...] = fr * brt - fi * bit
    bbi_ref[...] = fr * bit + fi * brt


def _discretise(lambda_re, lambda_im, log_dt, b_re, b_im):
    lr = lambda_re.reshape(1, N_STATE)
    li = lambda_im.reshape(1, N_STATE)
    ldt = jnp.repeat(log_dt, SSM_STATE).reshape(1, N_STATE)
    brt = b_re.transpose(2, 0, 1).reshape(SSM_GROUP, N_STATE)
    bit = b_im.transpose(2, 0, 1).reshape(SSM_GROUP, N_STATE)
    row = jax.ShapeDtypeStruct((1, N_STATE), F32)
    mat = jax.ShapeDtypeStruct((SSM_GROUP, N_STATE), F32)
    return pl.pallas_call(_disc_kernel, out_shape=(row, row, mat, mat), name="s5_discretise")(
        lr, li, ldt, brt, bit)


def _ada_kernel(c_ref, w_ref, b_ref, o_ref):
    c = c_ref[...]
    o_ref[...] = _bdot(c * _sigmoid(c), w_ref[...]) + b_ref[...]


def _adaln(c_all, w_ada, b_ada):
    n = c_all.shape[0]
    nblk = N_MOD
    return pl.pallas_call(
        _ada_kernel,
        grid=(nblk,),
        in_specs=[
            pl.BlockSpec((n, D_MODEL), lambda j: (0, 0)),
            pl.BlockSpec((D_MODEL, D_MODEL), lambda j: (0, j)),
            pl.BlockSpec((1, D_MODEL), lambda j: (0, j)),
        ],
        out_specs=pl.BlockSpec((n, D_MODEL), lambda j: (0, j)),
        out_shape=jax.ShapeDtypeStruct((n, N_MOD * D_MODEL), F32),
        compiler_params=pltpu.CompilerParams(
            dimension_semantics=("arbitrary",), vmem_limit_bytes=VMEM_LIMIT),
        name="adaln",
    )(c_all, w_ada, b_ada.reshape(1, -1))


def _mixer_kernel(nb, steps,
                  x_ref, sh_ref, sc_ref, gt_ref, h0r_ref, h0i_ref, cb0_ref,
                  n1g_ref, win_ref, cw_ref, ar_ref, ai_ref, wbr_ref, wbi_ref, wcr_ref, wci_ref,
                  dsk_ref, wglu_ref, bglu_ref, gcv_ref, gsm_ref, wout_ref,
                  x1_ref, cbo_ref, hro_ref, hio_ref,
                  z_scr, cv_scr, sr_scr, si_scr, hr_scr, hi_scr):
    rows = nb * steps
    i = pl.program_id(0)

    @pl.when(i == 0)
    def _():
        hr_scr[...] = h0r_ref[...]
        hi_scr[...] = h0i_ref[...]
        cv_scr[0:2 * nb, :] = cb0_ref[...]

    x = x_ref[...]
    h = _per_seq(_rms(x, n1g_ref[...]), nb, 1.0 + sc_ref[...], sh_ref[...])
    z_scr[...] = _bdot(h, win_ref[...])

    cv_scr[2 * nb:2 * nb + rows, :] = z_scr[:, CONV_DIM:2 * CONV_DIM] * z_scr[:, 2 * CONV_DIM:3 * CONV_DIM]
    conv = (cv_scr[0:rows, :] * cw_ref[0:1, :]
            + cv_scr[nb:nb + rows, :] * cw_ref[1:2, :]
            + cv_scr[2 * nb:2 * nb + rows, :] * cw_ref[2:3, :])
    y_conv = _rms(z_scr[:, 0:CONV_DIM] * conv, gcv_ref[...])
    carry = cv_scr[rows:rows + 2 * nb, :]
    cv_scr[0:2 * nb, :] = carry
    cbo_ref[...] = carry

    u = z_scr[:, 3 * CONV_DIM:IN_DIM]
    sr_scr[...] = _bdot(u, wbr_ref[...])
    si_scr[...] = _bdot(u, wbi_ref[...])

    for cg in range(N_STATE // SCAN_COLS):
        cols = slice(cg * SCAN_COLS, (cg + 1) * SCAN_COLS)
        a_r = jnp.broadcast_to(ar_ref[:, cols], (SUBLANES, SCAN_COLS))
        a_i = jnp.broadcast_to(ai_ref[:, cols], (SUBLANES, SCAN_COLS))

        def tile_body(bt, _, cols=cols, a_r=a_r, a_i=a_i):
            r0 = pl.multiple_of(bt * SUBLANES, SUBLANES)

            def step(l, hc):
                hr, hi = hc
                row = pl.multiple_of(l * nb + r0, SUBLANES)
                nr = a_r * hr - a_i * hi + sr_scr[pl.ds(row, SUBLANES), cols]
                ni = a_r * hi + a_i * hr + si_scr[pl.ds(row, SUBLANES), cols]
                sr_scr[pl.ds(row, SUBLANES), cols] = nr
                si_scr[pl.ds(row, SUBLANES), cols] = ni
                return nr, ni

            hr, hi = lax.fori_loop(
                0, steps, step,
                (hr_scr[pl.ds(r0, SUBLANES), cols], hi_scr[pl.ds(r0, SUBLANES), cols]),
                unroll=min(steps, 4))
            hr_scr[pl.ds(r0, SUBLANES), cols] = hr
            hi_scr[pl.ds(r0, SUBLANES), cols] = hi
            return 0

        lax.fori_loop(0, nb // SUBLANES, tile_body, 0)

    hro_ref[...] = hr_scr[...]
    hio_ref[...] = hi_scr[...]

    y = _bdot(sr_scr[...], wcr_ref[...]) - _bdot(si_scr[...], wci_ref[...]) + dsk_ref[...] * u
    g = _gelu_tanh(y)
    y_ssm = _rms(g * _sigmoid(_bdot(g, wglu_ref[...]) + bglu_ref[...]), gsm_ref[...])

    mixed = _bdot(y_conv, wout_ref[0:CONV_DIM, :]) + _bdot(y_ssm, wout_ref[CONV_DIM:2 * CONV_DIM, :])
    x1_ref[...] = x + _per_seq(mixed, nb, gt_ref[...])


def _const_spec(shape):
    return pl.BlockSpec(shape, lambda i: (0,) * len(shape), pipeline_mode=pl.Buffered(1))


def _mixer(x_tm, nb, steps_total, chunk, sh, sc, gt, h0r, h0i, cb0, weights):
    rows = nb * chunk
    nchunks = steps_total // chunk
    in_specs = [
        pl.BlockSpec((rows, D_MODEL), lambda i: (i, 0)),
        _const_spec((nb, D_MODEL)), _const_spec((nb, D_MODEL)), _const_spec((nb, D_MODEL)),
        _const_spec((nb, N_STATE)), _const_spec((nb, N_STATE)), _const_spec((2 * nb, CONV_DIM)),
    ] + [_const_spec(w.shape) for w in weights]
    out_specs = [
        pl.BlockSpec((rows, D_MODEL), lambda i: (i, 0)),
        _const_spec((2 * nb, CONV_DIM)), _const_spec((nb, N_STATE)), _const_spec((nb, N_STATE)),
    ]
    out_shape = [
        jax.ShapeDtypeStruct((steps_total * nb, D_MODEL), F32),
        jax.ShapeDtypeStruct((2 * nb, CONV_DIM), F32),
        jax.ShapeDtypeStruct((nb, N_STATE), F32),
        jax.ShapeDtypeStruct((nb, N_STATE), F32),
    ]
    scratch = [
        pltpu.VMEM((rows, IN_DIM), F32),
        pltpu.VMEM((rows + 2 * nb, CONV_DIM), F32),
        pltpu.VMEM((rows, N_STATE), F32),
        pltpu.VMEM((rows, N_STATE), F32),
        pltpu.VMEM((nb, N_STATE), F32),
        pltpu.VMEM((nb, N_STATE), F32),
    ]
    return pl.pallas_call(
        functools.partial(_mixer_kernel, nb, chunk),
        grid=(nchunks,),
        in_specs=in_specs, out_specs=out_specs, out_shape=out_shape,
        scratch_shapes=scratch,
        compiler_params=pltpu.CompilerParams(
            dimension_semantics=("arbitrary",), vmem_limit_bytes=VMEM_LIMIT),
        name="mixer",
    )(x_tm, sh, sc, gt, h0r, h0i, cb0, *weights)


TOK_BLOCK = 512
SEG_ALIGN = 16
ROW_TILE = 256
CHUNKS_PER_TILE = ROW_TILE // SEG_ALIGN
N_SLOTS = -(-(2 * TOK_BLOCK + N_EXPERTS * (SEG_ALIGN - 1)) // ROW_TILE) * ROW_TILE


def _route(h, wr, br):
    lg = jnp.dot(h, wr, preferred_element_type=F32, precision=lax.Precision.HIGHEST) + br
    col = lax.broadcasted_iota(jnp.int32, lg.shape, 1).astype(F32)
    neg = jnp.float32(-jnp.inf)
    is_g = col < N_EXPERT_GROUPS
    gl = jnp.where(is_g, lg, neg)
    gmax = jnp.max(gl, axis=-1, keepdims=True)
    gsum = jnp.sum(jnp.where(is_g, jnp.exp(gl - gmax), 0.0), axis=-1, keepdims=True)
    g_prob = 1.0 / gsum
    g_idx = jnp.min(jnp.where(gl == gmax, col, float(ROUTER_LANES)), axis=-1, keepdims=True)
    lo = N_EXPERT_GROUPS + EXPERTS_PER_GROUP * g_idx
    el = jnp.where(col >= lo, jnp.where(col < lo + EXPERTS_PER_GROUP, lg, neg), neg)
    m1 = jnp.max(el, axis=-1, keepdims=True)
    i1 = jnp.min(jnp.where(el == m1, col, float(ROUTER_LANES)), axis=-1, keepdims=True)
    el2 = jnp.where(col == i1, neg, el)
    m2 = jnp.max(el2, axis=-1, keepdims=True)
    i2 = jnp.min(jnp.where(el2 == m2, col, float(ROUTER_LANES)), axis=-1, keepdims=True)
    e21 = jnp.exp(m2 - m1)
    w1 = g_prob / (1.0 + e21)
    w2 = g_prob * e21 / (1.0 + e21)
    return i1 - N_EXPERT_GROUPS, i2 - N_EXPERT_GROUPS, w1, w2


def _iota_f32(shape, dim):
    return lax.broadcasted_iota(jnp.int32, shape, dim).astype(F32)


def _dispatch_kernel(nbp, x1p_ref, x1s_ref, sc_ref, sh_ref, n2g_ref, wr_ref, br_ref,
                     xs_ref, meta_ref, cnt_ref):
    x1 = jnp.where(pl.program_id(0) < nbp, x1p_ref[...], x1s_ref[...])
    h = _rms(x1, n2g_ref[...]) * (1.0 + sc_ref[0]) + sh_ref[0]
    e1, e2, w1, w2 = _route(h, wr_ref[...], br_ref[...])
    t = h.shape[0]
    col = _iota_f32((t, ROUTER_LANES), 1)
    hit1 = col == e1
    hit2 = col == e2
    onehot = jnp.where(hit1, 1.0, 0.0) + jnp.where(hit2, 1.0, 0.0)
    before = jnp.where(_iota_f32((t, t), 1) < _iota_f32((t, t), 0), 1.0, 0.0)
    rank = _bdot(before, onehot)
    cnt = jnp.sum(onehot, axis=0, keepdims=True)
    seg = jnp.floor((cnt + (SEG_ALIGN - 1)) * (1.0 / SEG_ALIGN)) * SEG_ALIGN
    earlier = jnp.where(_iota_f32((ROUTER_LANES, ROUTER_LANES), 0) < _iota_f32((ROUTER_LANES, ROUTER_LANES), 1),
                        1.0, 0.0)
    segoff = _bdot(jnp.broadcast_to(seg, (SUBLANES, ROUTER_LANES)), earlier)[0:1, :]
    base = rank + segoff
    slot1 = jnp.sum(jnp.where(hit1, base, 0.0), axis=-1, keepdims=True)
    slot2 = jnp.sum(jnp.where(hit2, base, 0.0), axis=-1, keepdims=True)
    lane = _iota_f32((t, N_SLOTS), 1)
    place = jnp.where(lane == slot1, 1.0, 0.0) + jnp.where(lane == slot2, 1.0, 0.0)
    xs = lax.dot_general(place.astype(BF16), h.astype(BF16), (((0,), (0,)), ((), ())),
                         preferred_element_type=F32)
    xs_ref[...] = xs.astype(BF16)
    meta_ref[...] = jnp.where(col == 0.0, slot1, jnp.where(col == 1.0, slot2,
                              jnp.where(col == 2.0, w1, jnp.where(col == 3.0, w2, 0.0))))
    cnt_ref[0] = jnp.broadcast_to(cnt, (SUBLANES, ROUTER_LANES))


def _combine_kernel(nbp, x1p_ref, x1s_ref, gt_ref, meta_ref, ys_ref, fg_ref, y_ref):
    x1 = jnp.where(pl.program_id(0) < nbp, x1p_ref[...], x1s_ref[...])
    meta = meta_ref[...]
    slot1, slot2, w1, w2 = (meta[:, k:k + 1] for k in range(4))
    lane = _iota_f32((x1.shape[0], N_SLOTS), 1)
    pick = jnp.where(lane == slot1, w1, 0.0) + jnp.where(lane == slot2, w2, 0.0)
    moe = jnp.dot(pick.astype(BF16), ys_ref[...], preferred_element_type=F32)
    y_ref[...] = _rms(x1 + gt_ref[0] * moe, fg_ref[...])


def _expert_kernel(te_ref, tc0_ref, tnv_ref, csrc_ref, nt_ref, xs_hbm, wg_ref, wu_ref, wd_ref,
                   ys_hbm, xbuf, ybuf, in_sem, out_sem):
    t = pl.program_id(0)
    nt = nt_ref[0]
    slot = t % 2

    def chunk_rows(j):
        return pl.ds(pl.multiple_of(j * SEG_ALIGN, SEG_ALIGN), SEG_ALIGN)

    def in_copy(src, j, s):
        return pltpu.make_async_copy(
            xs_hbm.at[pl.ds(pl.multiple_of(src, SEG_ALIGN), SEG_ALIGN), :],
            xbuf.at[s, chunk_rows(j), :], in_sem.at[s])

    def out_copy(dst, j, s):
        return pltpu.make_async_copy(
            ybuf.at[s, chunk_rows(j), :],
            ys_hbm.at[pl.ds(pl.multiple_of(dst, SEG_ALIGN), SEG_ALIGN), :], out_sem.at[s])

    def for_chunks(tile, fn):
        c0 = tc0_ref[tile]

        def body(j, _):
            fn(csrc_ref[c0 + j], j)
            return 0

        lax.fori_loop(0, tnv_ref[tile], body, 0)

    def start_gather(tile, s):
        for_chunks(tile, lambda src, j: in_copy(src, j, s).start())

    def wait_scatter(tile, s):
        for_chunks(tile, lambda dst, j: out_copy(dst, j, s).wait())

    @pl.when(t == 0)
    def _():
        xbuf[...] = jnp.zeros_like(xbuf)
        start_gather(0, 0)

    @pl.when(t + 1 < nt)
    def _():
        start_gather(t + 1, 1 - slot)

    @pl.when(t < nt)
    def _():
        for_chunks(t, lambda src, j: in_copy(src, j, slot).wait())

        @pl.when(t >= 2)
        def _():
            wait_scatter(t - 2, slot)

        x = xbuf[slot]
        a = jnp.dot(x, wg_ref[0].astype(BF16), preferred_element_type=F32)
        b = jnp.dot(x, wu_ref[0].astype(BF16), preferred_element_type=F32)
        o = _bdot(a * _sigmoid(a) * b, wd_ref[0])
        ybuf[slot] = o.astype(BF16)
        for_chunks(t, lambda dst, j: out_copy(dst, j, slot).start())

        @pl.when(t == nt - 1)
        def _():
            wait_scatter(t, slot)

            @pl.when(t >= 1)
            def _():
                wait_scatter(t - 1, 1 - slot)


def _tile_plan(cnt):
    nblk = cnt.shape[0]
    max_chunks = nblk * (2 * TOK_BLOCK // SEG_ALIGN + N_EXPERTS)
    max_tiles = max_chunks // CHUNKS_PER_TILE + N_EXPERTS
    m = (cnt + SEG_ALIGN - 1) // SEG_ALIGN
    seg_start = jnp.cumsum(m, axis=1) - m
    src0 = jnp.arange(nblk, dtype=jnp.int32)[:, None] * N_SLOTS + SEG_ALIGN * seg_start
    m_e = m.T.reshape(-1)
    src0_e = src0.T.reshape(-1)
    m_cum = jnp.cumsum(m_e)
    i = jnp.arange(max_chunks, dtype=jnp.int32)
    s = jnp.minimum(jnp.sum((m_cum[None, :] <= i[:, None]).astype(jnp.int32), axis=1), m_e.size - 1)
    chunk_src = jnp.where(i < m_cum[-1], src0_e[s] + SEG_ALIGN * (i - (m_cum - m_e)[s]), 0)
    ce = jnp.sum(m, axis=0)
    c_start = jnp.cumsum(ce) - ce
    tiles = (ce + CHUNKS_PER_TILE - 1) // CHUNKS_PER_TILE
    t_cum = jnp.cumsum(tiles)
    nt = t_cum[-1]
    k = jnp.arange(max_tiles, dtype=jnp.int32)
    kk = jnp.minimum(k, nt - 1)
    ek = jnp.minimum(jnp.sum((t_cum[None, :] <= kk[:, None]).astype(jnp.int32), axis=1), N_EXPERTS - 1)
    local = kk - (t_cum - tiles)[ek]
    tile_c0 = c_start[ek] + CHUNKS_PER_TILE * local
    tile_nv = jnp.where(k < nt, jnp.clip(ce[ek] - CHUNKS_PER_TILE * local, 0, CHUNKS_PER_TILE), 0)
    return (ek.astype(jnp.int32), tile_c0.astype(jnp.int32), tile_nv.astype(jnp.int32),
            chunk_src.astype(jnp.int32), nt.reshape(1).astype(jnp.int32))


def _moe(x1p, x1s, sc, sh, gt, n2g, wr, br, wg, wu, wd, fg):
    nbp = x1p.shape[0] // TOK_BLOCK
    nblk = nbp + 1
    blk = (TOK_BLOCK, D_MODEL)
    p_spec = pl.BlockSpec(blk, lambda i: (jnp.minimum(i, nbp - 1), 0))
    s_spec = pl.BlockSpec(blk, lambda i: (0, 0))
    mod_spec = pl.BlockSpec((1,) + blk, lambda i: (i // nbp, 0, 0))
    meta_spec = pl.BlockSpec((TOK_BLOCK, ROUTER_LANES), lambda i: (i, 0))
    slots_spec = pl.BlockSpec((N_SLOTS, D_MODEL), lambda i: (i, 0))
    params = pltpu.CompilerParams(dimension_semantics=("arbitrary",), vmem_limit_bytes=VMEM_LIMIT)

    def cs(shape):
        return pl.BlockSpec(shape, lambda i: (0,) * len(shape))

    xs, meta, cnt = pl.pallas_call(
        functools.partial(_dispatch_kernel, nbp),
        grid=(nblk,),
        in_specs=[p_spec, s_spec, mod_spec, mod_spec, cs((1, D_MODEL)),
                  cs((D_MODEL, ROUTER_LANES)), cs((1, ROUTER_LANES))],
        out_specs=[slots_spec, meta_spec,
                   pl.BlockSpec((1, SUBLANES, ROUTER_LANES), lambda i: (i, 0, 0))],
        out_shape=[jax.ShapeDtypeStruct((nblk * N_SLOTS, D_MODEL), BF16),
                   jax.ShapeDtypeStruct((nblk * TOK_BLOCK, ROUTER_LANES), F32),
                   jax.ShapeDtypeStruct((nblk, SUBLANES, ROUTER_LANES), F32)],
        compiler_params=params,
        name="moe_dispatch",
    )(x1p, x1s, sc, sh, n2g, wr, br)

    plan = _tile_plan(cnt[:, 0, :N_EXPERTS].astype(jnp.int32))
    max_tiles = plan[0].shape[0]

    def w_spec(shape):
        return pl.BlockSpec((1,) + shape, lambda t, te, *_: (te[t], 0, 0))

    ys = pl.pallas_call(
        _expert_kernel,
        grid_spec=pltpu.PrefetchScalarGridSpec(
            num_scalar_prefetch=len(plan),
            grid=(max_tiles,),
            in_specs=[pl.BlockSpec(memory_space=pl.ANY),
                      w_spec((D_MODEL, D_EXPERT)), w_spec((D_MODEL, D_EXPERT)), w_spec((D_EXPERT, D_MODEL))],
            out_specs=pl.BlockSpec(memory_space=pl.ANY),
            scratch_shapes=[pltpu.VMEM((2, ROW_TILE, D_MODEL), BF16),
                            pltpu.VMEM((2, ROW_TILE, D_MODEL), BF16),
                            pltpu.SemaphoreType.DMA((2,)), pltpu.SemaphoreType.DMA((2,))]),
        out_shape=jax.ShapeDtypeStruct(xs.shape, BF16),
        input_output_aliases={len(plan): 0},
        compiler_params=params,
        name="moe_experts",
    )(*plan, xs, wg, wu, wd)

    return pl.pallas_call(
        functools.partial(_combine_kernel, nbp),
        grid=(nblk,),
        in_specs=[p_spec, s_spec, mod_spec, meta_spec, slots_spec, cs((1, D_MODEL))],
        out_specs=pl.BlockSpec(blk, lambda i: (i, 0)),
        out_shape=jax.ShapeDtypeStruct((nblk * TOK_BLOCK, D_MODEL), F32),
        compiler_params=params,
        name="moe_combine",
    )(x1p, x1s, gt, meta, ys, fg)


def _block_diag_in(bbt):
    eye = jnp.eye(SSM_GROUPS, dtype=bbt.dtype)
    blocks = bbt.reshape(SSM_GROUP, SSM_GROUPS, SSM_STATE)
    w = eye[:, None, :, None] * blocks.transpose(1, 0, 2)[:, :, None, :]
    return w.reshape(SSM_DIM, N_STATE)


def _block_diag_out(c):
    eye = jnp.eye(SSM_GROUPS, dtype=c.dtype)
    w = eye[:, None, :, None] * c.transpose(0, 2, 1)[:, :, None, :]
    return w.reshape(N_STATE, SSM_DIM)


def _to_time_major(x):
    b, s, d = x.shape
    return x.transpose(1, 0, 2).reshape(s * b, d)


def _from_time_major(x, b, s):
    return x.reshape(s, b, -1).transpose(1, 0, 2)


def kernel(x_prompt, x_sample, state_conv, state_ssm_re, state_ssm_im, c_prompt, c_sample, w_ada, b_ada, norm1_g, norm2_g, w_in, conv_w, lambda_re, lambda_im, log_dt, ssm_b_re, ssm_b_im, ssm_c_re, ssm_c_im, ssm_d, w_glu, b_glu, out_norm_conv_g, out_norm_ssm_g, w_out, w_router_group, b_router_group, w_router_expert, b_router_expert, w_expert_gate, w_expert_up, w_expert_down, final_norm_g):
    assert w_ada.shape[0] == 1, "single-layer step"
    bp, sp, _ = x_prompt.shape
    bs, ss, _ = x_sample.shape

    ar, ai, bbr, bbi = _discretise(lambda_re[0], lambda_im[0], log_dt[0], ssm_b_re[0], ssm_b_im[0])
    w_br = _block_diag_in(bbr).astype(BF16)
    w_bi = _block_diag_in(bbi).astype(BF16)
    w_cr = _block_diag_out(ssm_c_re[0]).astype(BF16)
    w_ci = _block_diag_out(ssm_c_im[0]).astype(BF16)

    mod = _adaln(jnp.concatenate([c_prompt, c_sample], axis=0), w_ada[0], b_ada[0])
    sh1, sc1, gt1, sh2, sc2, gt2 = [mod[:, k * D_MODEL:(k + 1) * D_MODEL] for k in range(N_MOD)]

    mixer_weights = (
        norm1_g[0].reshape(1, -1), w_in[0].astype(BF16), conv_w[0], ar, ai, w_br, w_bi, w_cr, w_ci,
        ssm_d[0].reshape(1, -1), w_glu[0].astype(BF16), b_glu[0].reshape(1, -1),
        out_norm_conv_g[0].reshape(1, -1), out_norm_ssm_g[0].reshape(1, -1), w_out[0].astype(BF16),
    )

    w_router = jnp.concatenate(
        [w_router_group[0], w_router_expert[0].transpose(1, 0, 2).reshape(D_MODEL, N_EXPERTS),
         jnp.zeros((D_MODEL, ROUTER_LANES - N_EXPERT_GROUPS - N_EXPERTS), F32)], axis=1)
    b_router = jnp.concatenate(
        [b_router_group[0], b_router_expert[0].reshape(-1),
         jnp.zeros((ROUTER_LANES - N_EXPERT_GROUPS - N_EXPERTS,), F32)]).reshape(1, -1)
    moe_weights = (norm2_g[0].reshape(1, -1), w_router, b_router,
                   w_expert_gate[0], w_expert_up[0], w_expert_down[0], final_norm_g.reshape(1, -1))

    def mix(x, lo, hi, h0r, h0i, cb0, chunk):
        nb, s, _ = x.shape
        x1, cbo, hro, hio = _mixer(
            _to_time_major(x), nb, s, chunk, sh1[lo:hi], sc1[lo:hi], gt1[lo:hi],
            h0r, h0i, cb0, mixer_weights)
        new_conv = cbo.reshape(2, nb, CONV_DIM).transpose(1, 0, 2)[None]
        new_re = hro.reshape(1, nb, SSM_GROUPS, SSM_STATE)
        new_im = hio.reshape(1, nb, SSM_GROUPS, SSM_STATE)
        return x1, new_conv, new_re, new_im

    zero_state = jnp.zeros((bp, N_STATE), F32)
    zero_conv = jnp.zeros((2 * bp, CONV_DIM), F32)
    x1p, conv_p, re_p, im_p = mix(x_prompt, 0, bp, zero_state, zero_state, zero_conv, 64)
    x1s, conv_s, re_s, im_s = mix(
        x_sample, bp, bp + bs,
        state_ssm_re[0].reshape(bs, N_STATE), state_ssm_im[0].reshape(bs, N_STATE),
        state_conv[0].transpose(1, 0, 2).reshape(2 * bs, CONV_DIM), ss)

    assert bs * ss == TOK_BLOCK and (bp * sp) % TOK_BLOCK == 0 and TOK_BLOCK % bp == 0

    def block_rows(m):
        return jnp.stack([jnp.tile(m[:bp], (TOK_BLOCK // bp, 1)), jnp.tile(m[bp:], (TOK_BLOCK // bs, 1))])

    y = _moe(x1p, x1s, block_rows(sc2), block_rows(sh2), block_rows(gt2), *moe_weights)
    y_p = _from_time_major(y[:bp * sp], bp, sp)
    y_s = _from_time_major(y[bp * sp:], bs, ss)
    return (y_p, y_s, conv_p, re_p, im_p, conv_s, re_s, im_s)
```

```python
import functools

import jax
import jax.numpy as jnp
from jax import lax
from jax.experimental import pallas as pl
from jax.experimental.pallas import tpu as pltpu

D_MODEL = 1024
CONV_DIM = 512
SSM_DIM = 512
SSM_GROUP = 16
SSM_GROUPS = 32
SSM_STATE = 64
N_STATE = SSM_GROUPS * SSM_STATE
IN_DIM = 2048
N_EXPERT_GROUPS = 4
EXPERTS_PER_GROUP = 8
N_EXPERTS = 32
D_EXPERT = 256
N_MOD = 6
EPS = 1e-6

SUBLANES = 8
LANES = 128
SCAN_COLS = 512
VMEM_LIMIT = 60 * 1024 * 1024

F32 = jnp.float32
BF16 = jnp.bfloat16


def _sigmoid(x):
    return 1.0 / (1.0 + jnp.exp(-x))


def _gelu_tanh(x):
    return 0.5 * x * (1.0 + jnp.tanh(0.7978845608028654 * (x + 0.044715 * (x * x * x))))


def _rms(x, g):
    return x * lax.rsqrt(jnp.mean(x * x, axis=-1, keepdims=True) + EPS) * g


def _bdot(a, b):
    return jnp.dot(a.astype(BF16), b.astype(BF16), preferred_element_type=F32)


def _per_seq(v, nb, scale, shift=None):
    rows, d = v.shape
    v3 = v.reshape(rows // nb, nb, d) * scale[None]
    if shift is not None:
        v3 = v3 + shift[None]
    return v3.reshape(rows, d)


def _disc_kernel(lr_ref, li_ref, ldt_ref, brt_ref, bit_ref, ar_ref, ai_ref, bbr_ref, bbi_ref):
    lr = lr_ref[...]
    li = li_ref[...]
    dt = jnp.exp(ldt_ref[...])
    mag = jnp.exp(lr * dt)
    ar = mag * jnp.cos(li * dt)
    ai = mag * jnp.sin(li * dt)
    den = lr * lr + li * li
    fr = ((ar - 1.0) * lr + ai * li) / den
    fi = (ai * lr - (ar - 1.0) * li) / den
    ar_ref[...] = ar
    ai_ref[...] = ai
    brt = brt_ref[...]
    bit = bit_ref[...]
    bbr_ref[...] = fr * brt - fi * bit
    bbi_ref[...] = fr * bit + fi * brt


def _discretise(lambda_re, lambda_im, log_dt, b_re, b_im):
    lr = lambda_re.reshape(1, N_STATE)
    li = lambda_im.reshape(1, N_STATE)
    ldt = jnp.repeat(log_dt, SSM_STATE).reshape(1, N_STATE)
    brt = b_re.transpose(2, 0, 1).reshape(SSM_GROUP, N_STATE)
    bit = b_im.transpose(2, 0, 1).reshape(SSM_GROUP, N_STATE)
    row = jax.ShapeDtypeStruct((1, N_STATE), F32)
    mat = jax.ShapeDtypeStruct((SSM_GROUP, N_STATE), F32)
    return pl.pallas_call(_disc_kernel, out_shape=(row, row, mat, mat), name="s5_discretise")(
        lr, li, ldt, brt, bit)


def _ada_kernel(c_ref, w_ref, b_ref, o_ref):
    c = c_ref[...]
    o_ref[...] = _bdot(c * _sigmoid(c), w_ref[...]) + b_ref[...]


def _adaln(c_all, w_ada, b_ada):
    n = c_all.shape[0]
    nblk = N_MOD
    return pl.pallas_call(
        _ada_kernel,
        grid=(nblk,),
        in_specs=[
            pl.BlockSpec((n, D_MODEL), lambda j: (0, 0)),
            pl.BlockSpec((D_MODEL, D_MODEL), lambda j: (0, j)),
            pl.BlockSpec((1, D_MODEL), lambda j: (0, j)),
        ],
        out_specs=pl.BlockSpec((n, D_MODEL), lambda j: (0, j)),
        out_shape=jax.ShapeDtypeStruct((n, N_MOD * D_MODEL), F32),
        compiler_params=pltpu.CompilerParams(
            dimension_semantics=("arbitrary",), vmem_limit_bytes=VMEM_LIMIT),
        name="adaln",
    )(c_all, w_ada, b_ada.reshape(1, -1))


def _mixer_kernel(nb, steps,
                  x_ref, sh_ref, sc_ref, gt_ref, h0r_ref, h0i_ref, cb0_ref,
                  n1g_ref, win_ref, cw_ref, ar_ref, ai_ref, wbr_ref, wbi_ref, wcr_ref, wci_ref,
                  dsk_ref, wglu_ref, bglu_ref, gcv_ref, gsm_ref, wout_ref,
                  x1_ref, cbo_ref, hro_ref, hio_ref,
                  z_scr, cv_scr, sr_scr, si_scr, hr_scr, hi_scr):
    rows = nb * steps
    i = pl.program_id(0)

    @pl.when(i == 0)
    def _():
        hr_scr[...] = h0r_ref[...]
        hi_scr[...] = h0i_ref[...]
        cv_scr[0:2 * nb, :] = cb0_ref[...]

    x = x_ref[...]
    h = _per_seq(_rms(x, n1g_ref[...]), nb, 1.0 + sc_ref[...], sh_ref[...])
    z_scr[...] = _bdot(h, win_ref[...])

    cv_scr[2 * nb:2 * nb + rows, :] = z_scr[:, CONV_DIM:2 * CONV_DIM] * z_scr[:, 2 * CONV_DIM:3 * CONV_DIM]
    conv = (cv_scr[0:rows, :] * cw_ref[0:1, :]
            + cv_scr[nb:nb + rows, :] * cw_ref[1:2, :]
            + cv_scr[2 * nb:2 * nb + rows, :] * cw_ref[2:3, :])
    y_conv = _rms(z_scr[:, 0:CONV_DIM] * conv, gcv_ref[...])
    carry = cv_scr[rows:rows + 2 * nb, :]
    cv_scr[0:2 * nb, :] = carry
    cbo_ref[...] = carry

    u = z_scr[:, 3 * CONV_DIM:IN_DIM]
    sr_scr[...] = _bdot(u, wbr_ref[...])
    si_scr[...] = _bdot(u, wbi_ref[...])

    for cg in range(N_STATE // SCAN_COLS):
        cols = slice(cg * SCAN_COLS, (cg + 1) * SCAN_COLS)
        a_r = jnp.broadcast_to(ar_ref[:, cols], (SUBLANES, SCAN_COLS))
        a_i = jnp.broadcast_to(ai_ref[:, cols], (SUBLANES, SCAN_COLS))

        def tile_body(bt, _, cols=cols, a_r=a_r, a_i=a_i):
            r0 = pl.multiple_of(bt * SUBLANES, SUBLANES)

            def step(l, hc):
                hr, hi = hc
                row = pl.multiple_of(l * nb + r0, SUBLANES)
                nr = a_r * hr - a_i * hi + sr_scr[pl.ds(row, SUBLANES), cols]
                ni = a_r * hi + a_i * hr + si_scr[pl.ds(row, SUBLANES), cols]
                sr_scr[pl.ds(row, SUBLANES), cols] = nr
                si_scr[pl.ds(row, SUBLANES), cols] = ni
                return nr, ni

            hr, hi = lax.fori_loop(
                0, steps, step,
                (hr_scr[pl.ds(r0, SUBLANES), cols], hi_scr[pl.ds(r0, SUBLANES), cols]),
                unroll=min(steps, 4))
            hr_scr[pl.ds(r0, SUBLANES), cols] = hr
            hi_scr[pl.ds(r0, SUBLANES), cols] = hi
            return 0

        lax.fori_loop(0, nb // SUBLANES, tile_body, 0)

    hro_ref[...] = hr_scr[...]
    hio_ref[...] = hi_scr[...]

    y = _bdot(sr_scr[...], wcr_ref[...]) - _bdot(si_scr[...], wci_ref[...]) + dsk_ref[...] * u
    g = _gelu_tanh(y)
    y_ssm = _rms(g * _sigmoid(_bdot(g, wglu_ref[...]) + bglu_ref[...]), gsm_ref[...])

    mixed = _bdot(y_conv, wout_ref[0:CONV_DIM, :]) + _bdot(y_ssm, wout_ref[CONV_DIM:2 * CONV_DIM, :])
    x1_ref[...] = x + _per_seq(mixed, nb, gt_ref[...])


def _const_spec(shape):
    return pl.BlockSpec(shape, lambda i: (0,) * len(shape), pipeline_mode=pl.Buffered(1))


def _mixer(x_tm, nb, steps_total, chunk, sh, sc, gt, h0r, h0i, cb0, weights):
    rows = nb * chunk
    nchunks = steps_total // chunk
    in_specs = [
        pl.BlockSpec((rows, D_MODEL), lambda i: (i, 0)),
        _const_spec((nb, D_MODEL)), _const_spec((nb, D_MODEL)), _const_spec((nb, D_MODEL)),
        _const_spec((nb, N_STATE)), _const_spec((nb, N_STATE)), _const_spec((2 * nb, CONV_DIM)),
    ] + [_const_spec(w.shape) for w in weights]
    out_specs = [
        pl.BlockSpec((rows, D_MODEL), lambda i: (i, 0)),
        _const_spec((2 * nb, CONV_DIM)), _const_spec((nb, N_STATE)), _const_spec((nb, N_STATE)),
    ]
    out_shape = [
        jax.ShapeDtypeStruct((steps_total * nb, D_MODEL), F32),
        jax.ShapeDtypeStruct((2 * nb, CONV_DIM), F32),
        jax.ShapeDtypeStruct((nb, N_STATE), F32),
        jax.ShapeDtypeStruct((nb, N_STATE), F32),
    ]
    scratch = [
        pltpu.VMEM((rows, IN_DIM), F32),
        pltpu.VMEM((rows + 2 * nb, CONV_DIM), F32),
        pltpu.VMEM((rows, N_STATE), F32),
        pltpu.VMEM((rows, N_STATE), F32),
        pltpu.VMEM((nb, N_STATE), F32),
        pltpu.VMEM((nb, N_STATE), F32),
    ]
    return pl.pallas_call(
        functools.partial(_mixer_kernel, nb, chunk),
        grid=(nchunks,),
        in_specs=in_specs, out_specs=out_specs, out_shape=out_shape,
        scratch_shapes=scratch,
        compiler_params=pltpu.CompilerParams(
            dimension_semantics=("arbitrary",), vmem_limit_bytes=VMEM_LIMIT),
        name="mixer",
    )(x_tm, sh, sc, gt, h0r, h0i, cb0, *weights)


TOK_BLOCK = 512
SEG_ALIGN = 16
ROW_TILE = 512
CHUNKS_PER_TILE = ROW_TILE // SEG_ALIGN
N_SLOTS = -(-(2 * TOK_BLOCK + N_EXPERTS * (SEG_ALIGN - 1)) // 256) * 256
ROUTER_ROWS = 48
META_ROWS = SUBLANES

_NT = (((1,), (1,)), ((), ()))
_TN = (((0,), (0,)), ((), ()))


def _iota_f32(shape, dim):
    return lax.broadcasted_iota(jnp.int32, shape, dim).astype(F32)


def _split_bf16(v):
    hi = v.astype(BF16)
    return hi, (v - hi.astype(F32)).astype(BF16)


def _route_t(lg):
    row = _iota_f32(lg.shape, 0)
    neg = jnp.float32(-jnp.inf)
    none = float(ROUTER_ROWS)
    is_g = row < N_EXPERT_GROUPS
    gl = jnp.where(is_g, lg, neg)
    gmax = jnp.max(gl, axis=0, keepdims=True)
    gsum = jnp.sum(jnp.where(is_g, jnp.exp(gl - gmax), 0.0), axis=0, keepdims=True)
    g_prob = 1.0 / gsum
    g_idx = jnp.min(jnp.where(gl == gmax, row, none), axis=0, keepdims=True)
    lo = N_EXPERT_GROUPS + EXPERTS_PER_GROUP * g_idx
    el = jnp.where(row >= lo, jnp.where(row < lo + EXPERTS_PER_GROUP, lg, neg), neg)
    m1 = jnp.max(el, axis=0, keepdims=True)
    i1 = jnp.min(jnp.where(el == m1, row, none), axis=0, keepdims=True)
    el2 = jnp.where(row == i1, neg, el)
    m2 = jnp.max(el2, axis=0, keepdims=True)
    i2 = jnp.min(jnp.where(el2 == m2, row, none), axis=0, keepdims=True)
    e21 = jnp.exp(m2 - m1)
    return i1, i2, g_prob / (1.0 + e21), g_prob * e21 / (1.0 + e21)


def _dispatch_kernel(nbp, x1p_ref, x1s_ref, sc_ref, sh_ref, n2g_ref, wr_ref, br_ref,
                     xs_ref, meta_ref, cnt_ref):
    x1 = jnp.where(pl.program_id(0) < nbp, x1p_ref[...], x1s_ref[...])
    h = _rms(x1, n2g_ref[...]) * (1.0 + sc_ref[0]) + sh_ref[0]
    t = h.shape[0]
    hb, hl = _split_bf16(h)
    wh, wl = _split_bf16(wr_ref[...])
    lg = (lax.dot_general(wh, hb, _NT, preferred_element_type=F32)
          + lax.dot_general(wh, hl, _NT, preferred_element_type=F32)
          + lax.dot_general(wl, hb, _NT, preferred_element_type=F32)) + br_ref[...]
    i1, i2, w1, w2 = _route_t(lg)

    row = _iota_f32((ROUTER_ROWS, t), 0)
    hit1 = row == i1
    hit2 = row == i2
    onehot = jnp.where(hit1, 1.0, 0.0) + jnp.where(hit2, 1.0, 0.0)
    before = jnp.where(_iota_f32((t, t), 0) < _iota_f32((t, t), 1), 1.0, 0.0)
    rank = _bdot(onehot, before)
    cnt = jnp.sum(onehot, axis=1, keepdims=True)
    seg = jnp.floor((cnt + (SEG_ALIGN - 1)) * (1.0 / SEG_ALIGN)) * SEG_ALIGN
    earlier = jnp.where(_iota_f32((ROUTER_ROWS, ROUTER_ROWS), 1) < _iota_f32((ROUTER_ROWS, ROUTER_ROWS), 0),
                        1.0, 0.0)
    base = rank + _bdot(earlier, jnp.broadcast_to(seg, (ROUTER_ROWS, t)))
    slot1 = jnp.sum(jnp.where(hit1, base, 0.0), axis=0, keepdims=True)
    slot2 = jnp.sum(jnp.where(hit2, base, 0.0), axis=0, keepdims=True)
    j = _iota_f32((N_SLOTS, t), 0)
    place = jnp.where(j == slot1, 1.0, 0.0) + jnp.where(j == slot2, 1.0, 0.0)
    xs_ref[...] = jnp.dot(place.astype(BF16), hb, preferred_element_type=F32).astype(BF16)
    r = _iota_f32((META_ROWS, t), 0)
    meta_ref[0] = jnp.where(r == 0.0, slot1, jnp.where(r == 1.0, slot2,
                            jnp.where(r == 2.0, w1, jnp.where(r == 3.0, w2, 0.0))))
    cnt_ref[0] = jnp.broadcast_to(cnt, (ROUTER_ROWS, LANES))


def _combine_kernel(nbp, x1p_ref, x1s_ref, gt_ref, meta_ref, ys_ref, fg_ref, yp_ref, ysm_ref):
    i = pl.program_id(0)
    x1 = jnp.where(i < nbp, x1p_ref[...], x1s_ref[...])
    meta = meta_ref[0]
    slot1, slot2, w1, w2 = (meta[k:k + 1, :] for k in range(4))
    j = _iota_f32((N_SLOTS, x1.shape[0]), 0)
    pick = jnp.where(j == slot1, w1, 0.0) + jnp.where(j == slot2, w2, 0.0)
    moe = lax.dot_general(pick.astype(BF16), ys_ref[...], _TN, preferred_element_type=F32)
    y = _rms(x1 + gt_ref[0] * moe, fg_ref[...])

    @pl.when(i < nbp)
    def _():
        yp_ref[...] = y

    @pl.when(i >= nbp)
    def _():
        ysm_ref[...] = y


def _expert_kernel(te_ref, tnv_ref, segm_ref, segsrc_ref, nt_ref, xs_hbm, wg_ref, wu_ref, wd_ref,
                   ys_hbm, xbuf, ybuf, wgb, wub, wdb, rows, cur, in_sem, out_sem):
    t = pl.program_id(0)
    nt = nt_ref[0]
    slot = t % 2

    def chunk_rows(j):
        return pl.ds(pl.multiple_of(j * SEG_ALIGN, SEG_ALIGN), SEG_ALIGN)

    def hbm_rows(ref, r):
        return ref.at[pl.ds(pl.multiple_of(r, SEG_ALIGN), SEG_ALIGN), :]

    def in_copy(r, j, s):
        return pltpu.make_async_copy(hbm_rows(xs_hbm, r), xbuf.at[s, chunk_rows(j), :], in_sem.at[s])

    def out_copy(r, j, s):
        return pltpu.make_async_copy(ybuf.at[s, chunk_rows(j), :], hbm_rows(ys_hbm, r), out_sem.at[s])

    def start_gather(tile, s):
        def used_up(c):
            return c[1] >= segm_ref[c[0]]

        def body(j, c):
            seg, k = lax.while_loop(used_up, lambda c: (c[0] + 1, 0), c)
            r = segsrc_ref[seg] + SEG_ALIGN * k
            rows[s, j] = r
            in_copy(r, j, s).start()
            return seg, k + 1

        seg, k = lax.fori_loop(0, tnv_ref[tile], body, (cur[0], cur[1]))
        cur[0] = seg
        cur[1] = k

    def for_chunks(tile, fn):
        def body(j, _):
            fn(j)
            return 0

        lax.fori_loop(0, tnv_ref[tile], body, 0)

    @pl.when(t == 0)
    def _():
        xbuf[...] = jnp.zeros_like(xbuf)
        cur[0] = 0
        cur[1] = 0
        start_gather(0, 0)

    @pl.when(jnp.logical_or(t == 0, te_ref[t] != te_ref[jnp.maximum(t - 1, 0)]))
    def _():
        wgb[...] = wg_ref[0].astype(BF16)
        wub[...] = wu_ref[0].astype(BF16)
        wdb[...] = wd_ref[0].astype(BF16)

    @pl.when(t + 1 < nt)
    def _():
        start_gather(t + 1, 1 - slot)

    @pl.when(t < nt)
    def _():
        for_chunks(t, lambda j: in_copy(0, j, slot).wait())

        @pl.when(t >= 2)
        def _():
            for_chunks(t - 2, lambda j: out_copy(0, j, slot).wait())

        x = xbuf[slot]
        a = jnp.dot(x, wgb[...], preferred_element_type=F32)
        b = jnp.dot(x, wub[...], preferred_element_type=F32)
        o = jnp.dot((a * _sigmoid(a) * b).astype(BF16), wdb[...], preferred_element_type=F32)
        ybuf[slot] = o.astype(BF16)
        for_chunks(t, lambda j: out_copy(rows[slot, j], j, slot).start())

        @pl.when(t == nt - 1)
        def _():
            for_chunks(t, lambda j: out_copy(0, j, slot).wait())

            @pl.when(t >= 1)
            def _():
                for_chunks(t - 1, lambda j: out_copy(0, j, 1 - slot).wait())


def _tile_plan(cnt):
    nblk = cnt.shape[0]
    max_chunks = nblk * (2 * TOK_BLOCK // SEG_ALIGN + N_EXPERTS)
    max_tiles = max_chunks // CHUNKS_PER_TILE + N_EXPERTS
    m = (cnt + SEG_ALIGN - 1) // SEG_ALIGN
    seg_start = jnp.cumsum(m, axis=1) - m
    src0 = jnp.arange(nblk, dtype=jnp.int32)[:, None] * N_SLOTS + SEG_ALIGN * seg_start
    seg_m = jnp.concatenate([m.T.reshape(-1), jnp.full((1,), max_chunks, jnp.int32)])
    seg_src = jnp.concatenate([src0.T.reshape(-1), jnp.zeros((1,), jnp.int32)])
    ce = jnp.sum(m, axis=0)
    tiles = (ce + CHUNKS_PER_TILE - 1) // CHUNKS_PER_TILE
    t_cum = jnp.cumsum(tiles)
    nt = t_cum[-1]
    k = jnp.arange(max_tiles, dtype=jnp.int32)
    kk = jnp.minimum(k, nt - 1)
    ek = jnp.minimum(jnp.sum((t_cum[None, :] <= kk[:, None]).astype(jnp.int32), axis=1), N_EXPERTS - 1)
    is_e = ek[:, None] == jnp.arange(N_EXPERTS, dtype=jnp.int32)[None, :]
    local = kk - jnp.sum(jnp.where(is_e, (t_cum - tiles)[None, :], 0), axis=1)
    left = jnp.sum(jnp.where(is_e, ce[None, :], 0), axis=1) - CHUNKS_PER_TILE * local
    tile_nv = jnp.where(k < nt, jnp.clip(left, 0, CHUNKS_PER_TILE), 0)
    return tuple(v.astype(jnp.int32) for v in (ek, tile_nv, seg_m, seg_src, nt.reshape(1)))


def _moe(x1p, x1s, sc, sh, gt, n2g, wr, br, wg, wu, wd, fg):
    nbp = x1p.shape[0] // TOK_BLOCK
    nblk = nbp + 1
    blk = (TOK_BLOCK, D_MODEL)
    p_spec = pl.BlockSpec(blk, lambda i: (jnp.minimum(i, nbp - 1), 0))
    s_spec = pl.BlockSpec(blk, lambda i: (0, 0))
    mod_spec = pl.BlockSpec((1,) + blk, lambda i: (i // nbp, 0, 0))
    meta_spec = pl.BlockSpec((1, META_ROWS, TOK_BLOCK), lambda i: (i, 0, 0))
    slots_spec = pl.BlockSpec((N_SLOTS, D_MODEL), lambda i: (i, 0))
    params = pltpu.CompilerParams(dimension_semantics=("arbitrary",), vmem_limit_bytes=VMEM_LIMIT)

    def cs(shape):
        return pl.BlockSpec(shape, lambda i: (0,) * len(shape))

    xs, meta, cnt = pl.pallas_call(
        functools.partial(_dispatch_kernel, nbp),
        grid=(nblk,),
        in_specs=[p_spec, s_spec, mod_spec, mod_spec, cs((1, D_MODEL)),
                  cs((ROUTER_ROWS, D_MODEL)), cs((ROUTER_ROWS, TOK_BLOCK))],
        out_specs=[slots_spec, meta_spec,
                   pl.BlockSpec((1, ROUTER_ROWS, LANES), lambda i: (i, 0, 0))],
        out_shape=[jax.ShapeDtypeStruct((nblk * N_SLOTS, D_MODEL), BF16),
                   jax.ShapeDtypeStruct((nblk, META_ROWS, TOK_BLOCK), F32),
                   jax.ShapeDtypeStruct((nblk, ROUTER_ROWS, LANES), F32)],
        compiler_params=params,
        name="moe_dispatch",
    )(x1p, x1s, sc, sh, n2g, wr, br)

    plan = _tile_plan(cnt[:, N_EXPERT_GROUPS:N_EXPERT_GROUPS + N_EXPERTS, 0].astype(jnp.int32))
    max_tiles = plan[0].shape[0]

    def w_spec(shape):
        return pl.BlockSpec((1,) + shape, lambda t, te, *_: (te[t], 0, 0))

    ys = pl.pallas_call(
        _expert_kernel,
        grid_spec=pltpu.PrefetchScalarGridSpec(
            num_scalar_prefetch=len(plan),
            grid=(max_tiles,),
            in_specs=[pl.BlockSpec(memory_space=pl.ANY),
                      w_spec((D_MODEL, D_EXPERT)), w_spec((D_MODEL, D_EXPERT)), w_spec((D_EXPERT, D_MODEL))],
            out_specs=pl.BlockSpec(memory_space=pl.ANY),
            scratch_shapes=[pltpu.VMEM((2, ROW_TILE, D_MODEL), BF16),
                            pltpu.VMEM((2, ROW_TILE, D_MODEL), BF16),
                            pltpu.VMEM((D_MODEL, D_EXPERT), BF16),
                            pltpu.VMEM((D_MODEL, D_EXPERT), BF16),
                            pltpu.VMEM((D_EXPERT, D_MODEL), BF16),
                            pltpu.SMEM((2, CHUNKS_PER_TILE), jnp.int32),
                            pltpu.SMEM((2,), jnp.int32),
                            pltpu.SemaphoreType.DMA((2,)), pltpu.SemaphoreType.DMA((2,))]),
        out_shape=jax.ShapeDtypeStruct(xs.shape, BF16),
        input_output_aliases={len(plan): 0},
        compiler_params=params,
        name="moe_experts",
    )(*plan, xs, wg, wu, wd)

    return pl.pallas_call(
        functools.partial(_combine_kernel, nbp),
        grid=(nblk,),
        in_specs=[p_spec, s_spec, mod_spec, meta_spec, slots_spec, cs((1, D_MODEL))],
        out_specs=[p_spec, s_spec],
        out_shape=[jax.ShapeDtypeStruct(x1p.shape, F32), jax.ShapeDtypeStruct(x1s.shape, F32)],
        compiler_params=params,
        name="moe_combine",
    )(x1p, x1s, gt, meta, ys, fg)


def _block_diag_in(bbt):
    eye = jnp.eye(SSM_GROUPS, dtype=bbt.dtype)
    blocks = bbt.reshape(SSM_GROUP, SSM_GROUPS, SSM_STATE)
    w = eye[:, None, :, None] * blocks.transpose(1, 0, 2)[:, :, None, :]
    return w.reshape(SSM_DIM, N_STATE)


def _block_diag_out(c):
    eye = jnp.eye(SSM_GROUPS, dtype=c.dtype)
    w = eye[:, None, :, None] * c.transpose(0, 2, 1)[:, :, None, :]
    return w.reshape(N_STATE, SSM_DIM)


def _to_time_major(x):
    b, s, d = x.shape
    return x.transpose(1, 0, 2).reshape(s * b, d)


def _from_time_major(x, b, s):
    return x.reshape(s, b, -1).transpose(1, 0, 2)


def kernel(x_prompt, x_sample, state_conv, state_ssm_re, state_ssm_im, c_prompt, c_sample, w_ada, b_ada, norm1_g, norm2_g, w_in, conv_w, lambda_re, lambda_im, log_dt, ssm_b_re, ssm_b_im, ssm_c_re, ssm_c_im, ssm_d, w_glu, b_glu, out_norm_conv_g, out_norm_ssm_g, w_out, w_router_group, b_router_group, w_router_expert, b_router_expert, w_expert_gate, w_expert_up, w_expert_down, final_norm_g):
    assert w_ada.shape[0] == 1, "single-layer step"
    bp, sp, _ = x_prompt.shape
    bs, ss, _ = x_sample.shape

    ar, ai, bbr, bbi = _discretise(lambda_re[0], lambda_im[0], log_dt[0], ssm_b_re[0], ssm_b_im[0])
    w_br = _block_diag_in(bbr).astype(BF16)
    w_bi = _block_diag_in(bbi).astype(BF16)
    w_cr = _block_diag_out(ssm_c_re[0]).astype(BF16)
    w_ci = _block_diag_out(ssm_c_im[0]).astype(BF16)

    mod = _adaln(jnp.concatenate([c_prompt, c_sample], axis=0), w_ada[0], b_ada[0])
    sh1, sc1, gt1, sh2, sc2, gt2 = [mod[:, k * D_MODEL:(k + 1) * D_MODEL] for k in range(N_MOD)]

    mixer_weights = (
        norm1_g[0].reshape(1, -1), w_in[0].astype(BF16), conv_w[0], ar, ai, w_br, w_bi, w_cr, w_ci,
        ssm_d[0].reshape(1, -1), w_glu[0].astype(BF16), b_glu[0].reshape(1, -1),
        out_norm_conv_g[0].reshape(1, -1), out_norm_ssm_g[0].reshape(1, -1), w_out[0].astype(BF16),
    )

    pad_rows = ROUTER_ROWS - N_EXPERT_GROUPS - N_EXPERTS
    w_router = jnp.concatenate(
        [w_router_group[0].T, w_router_expert[0].transpose(0, 2, 1).reshape(N_EXPERTS, D_MODEL),
         jnp.zeros((pad_rows, D_MODEL), F32)], axis=0)
    b_router = jnp.concatenate(
        [b_router_group[0], b_router_expert[0].reshape(-1), jnp.zeros((pad_rows,), F32)])
    b_router = jnp.broadcast_to(b_router[:, None], (ROUTER_ROWS, TOK_BLOCK))
    moe_weights = (norm2_g[0].reshape(1, -1), w_router, b_router,
                   w_expert_gate[0], w_expert_up[0], w_expert_down[0], final_norm_g.reshape(1, -1))

    def mix(x, lo, hi, h0r, h0i, cb0, chunk):
        nb, s, _ = x.shape
        x1, cbo, hro, hio = _mixer(
            _to_time_major(x), nb, s, chunk, sh1[lo:hi], sc1[lo:hi], gt1[lo:hi],
            h0r, h0i, cb0, mixer_weights)
        new_conv = cbo.reshape(2, nb, CONV_DIM).transpose(1, 0, 2)[None]
        new_re = hro.reshape(1, nb, SSM_GROUPS, SSM_STATE)
        new_im = hio.reshape(1, nb, SSM_GROUPS, SSM_STATE)
        return x1, new_conv, new_re, new_im

    zero_state = jnp.zeros((bp, N_STATE), F32)
    zero_conv = jnp.zeros((2 * bp, CONV_DIM), F32)
    x1p, conv_p, re_p, im_p = mix(x_prompt, 0, bp, zero_state, zero_state, zero_conv, 64)
    x1s, conv_s, re_s, im_s = mix(
        x_sample, bp, bp + bs,
        state_ssm_re[0].reshape(bs, N_STATE), state_ssm_im[0].reshape(bs, N_STATE),
        state_conv[0].transpose(1, 0, 2).reshape(2 * bs, CONV_DIM), ss)

    assert bs * ss == TOK_BLOCK and (bp * sp) % TOK_BLOCK == 0 and TOK_BLOCK % bp == 0

    def block_rows(m):
        return jnp.stack([jnp.tile(m[:bp], (TOK_BLOCK // bp, 1)), jnp.tile(m[bp:], (TOK_BLOCK // bs, 1))])

    y_p, y_s = _moe(x1p, x1s, block_rows(sc2), block_rows(sh2), block_rows(gt2), *moe_weights)
    return (_from_time_major(y_p, bp, sp), _from_time_major(y_s, bs, ss),
            conv_p, re_p, im_p, conv_s, re_s, im_s)
```

```python
import functools

import jax
import jax.numpy as jnp
from jax import lax
from jax.experimental import pallas as pl
from jax.experimental.pallas import tpu as pltpu

D_MODEL = 1024
CONV_DIM = 512
SSM_DIM = 512
SSM_GROUP = 16
SSM_GROUPS = 32
SSM_STATE = 64
N_STATE = SSM_GROUPS * SSM_STATE
HALF_SSM = SSM_DIM // 2
HALF_STATE = N_STATE // 2
IN_DIM = 2048
N_EXPERT_GROUPS = 4
EXPERTS_PER_GROUP = 8
N_EXPERTS = 32
D_EXPERT = 256
N_MOD = 6
EPS = 1e-6

SUBLANES = 8
LANES = 128
SCAN_COLS = 512
VMEM_LIMIT = 60 * 1024 * 1024

F32 = jnp.float32
BF16 = jnp.bfloat16


def _sigmoid(x):
    return 1.0 / (1.0 + jnp.exp(-x))


def _gelu_tanh(x):
    return 0.5 * x * (1.0 + jnp.tanh(0.7978845608028654 * (x + 0.044715 * (x * x * x))))


def _rms(x, g):
    return x * lax.rsqrt(jnp.mean(x * x, axis=-1, keepdims=True) + EPS) * g


def _bdot(a, b):
    return jnp.dot(a.astype(BF16), b.astype(BF16), preferred_element_type=F32)


def _per_seq(v, nb, scale, shift=None):
    rows, d = v.shape
    v3 = v.reshape(rows // nb, nb, d) * scale[None]
    if shift is not None:
        v3 = v3 + shift[None]
    return v3.reshape(rows, d)


def _disc_kernel(lr_ref, li_ref, ldt_ref, brt_ref, bit_ref, ar_ref, ai_ref, bbr_ref, bbi_ref):
    lr = lr_ref[...]
    li = li_ref[...]
    dt = jnp.exp(ldt_ref[...])
    mag = jnp.exp(lr * dt)
    ar = mag * jnp.cos(li * dt)
    ai = mag * jnp.sin(li * dt)
    den = lr * lr + li * li
    fr = ((ar - 1.0) * lr + ai * li) / den
    fi = (ai * lr - (ar - 1.0) * li) / den
    ar_ref[...] = ar
    ai_ref[...] = ai
    brt = brt_ref[...]
    bit = bit_ref[...]
    bbr_ref[...] = fr * brt - fi * bit
    bbi_ref[...] = fr * bit + fi * brt


def _discretise(lambda_re, lambda_im, log_dt, b_re, b_im):
    lr = lambda_re.reshape(1, N_STATE)
    li = lambda_im.reshape(1, N_STATE)
    ldt = jnp.repeat(log_dt, SSM_STATE).reshape(1, N_STATE)
    brt = b_re.transpose(2, 0, 1).reshape(SSM_GROUP, N_STATE)
    bit = b_im.transpose(2, 0, 1).reshape(SSM_GROUP, N_STATE)
    row = jax.ShapeDtypeStruct((1, N_STATE), F32)
    mat = jax.ShapeDtypeStruct((SSM_GROUP, N_STATE), F32)
    return pl.pallas_call(_disc_kernel, out_shape=(row, row, mat, mat), name="s5_discretise")(
        lr, li, ldt, brt, bit)


def _ada_kernel(c_ref, w_ref, b_ref, o_ref):
    c = c_ref[...]
    o_ref[...] = _bdot(c * _sigmoid(c), w_ref[...]) + b_ref[...]


def _adaln(c_all, w_ada, b_ada):
    n = c_all.shape[0]
    nblk = N_MOD
    return pl.pallas_call(
        _ada_kernel,
        grid=(nblk,),
        in_specs=[
            pl.BlockSpec((n, D_MODEL), lambda j: (0, 0)),
            pl.BlockSpec((D_MODEL, D_MODEL), lambda j: (0, j)),
            pl.BlockSpec((1, D_MODEL), lambda j: (0, j)),
        ],
        out_specs=pl.BlockSpec((n, D_MODEL), lambda j: (0, j)),
        out_shape=jax.ShapeDtypeStruct((n, N_MOD * D_MODEL), F32),
        compiler_params=pltpu.CompilerParams(
            dimension_semantics=("arbitrary",), vmem_limit_bytes=VMEM_LIMIT),
        name="adaln",
    )(c_all, w_ada, b_ada.reshape(1, -1))


def _mixer_kernel(nb, steps,
                  x_ref, sh_ref, sc_ref, gt_ref, h0r_ref, h0i_ref, cb0_ref,
                  n1g_ref, win_ref, cw_ref, ar_ref, ai_ref, wb_ref, wcr_ref, wci_ref,
                  dsk_ref, wglu_ref, bglu_ref, gcv_ref, gsm_ref, wout_ref,
                  x1_ref, cbo_ref, hro_ref, hio_ref,
                  z_scr, cv_scr, s_scr, hr_scr, hi_scr):
    rows = nb * steps
    i = pl.program_id(0)

    @pl.when(i == 0)
    def _():
        hr_scr[...] = h0r_ref[...]
        hi_scr[...] = h0i_ref[...]
        cv_scr[0:2 * nb, :] = cb0_ref[...]

    x = x_ref[...]
    h = _per_seq(_rms(x, n1g_ref[...]), nb, 1.0 + sc_ref[...], sh_ref[...])
    z_scr[...] = _bdot(h, win_ref[...])

    cv_scr[2 * nb:2 * nb + rows, :] = z_scr[:, CONV_DIM:2 * CONV_DIM] * z_scr[:, 2 * CONV_DIM:3 * CONV_DIM]
    conv = (cv_scr[0:rows, :] * cw_ref[0:1, :]
            + cv_scr[nb:nb + rows, :] * cw_ref[1:2, :]
            + cv_scr[2 * nb:2 * nb + rows, :] * cw_ref[2:3, :])
    y_conv = _rms(z_scr[:, 0:CONV_DIM] * conv, gcv_ref[...])
    carry = cv_scr[rows:rows + 2 * nb, :]
    cv_scr[0:2 * nb, :] = carry
    cbo_ref[...] = carry

    u = z_scr[:, 3 * CONV_DIM:IN_DIM]
    for k in range(2):
        s_scr[:, 2 * k * HALF_STATE:2 * (k + 1) * HALF_STATE] = _bdot(
            u[:, k * HALF_SSM:(k + 1) * HALF_SSM], wb_ref[k])

    for cg in range(N_STATE // SCAN_COLS):
        cols = slice(cg * SCAN_COLS, (cg + 1) * SCAN_COLS)
        half, off = divmod(cg * SCAN_COLS, HALF_STATE)
        re_cols = slice(2 * half * HALF_STATE + off, 2 * half * HALF_STATE + off + SCAN_COLS)
        im_cols = slice(re_cols.start + HALF_STATE, re_cols.stop + HALF_STATE)
        a_r = jnp.broadcast_to(ar_ref[:, cols], (SUBLANES, SCAN_COLS))
        a_i = jnp.broadcast_to(ai_ref[:, cols], (SUBLANES, SCAN_COLS))

        def tile_body(bt, _, cols=cols, re_cols=re_cols, im_cols=im_cols, a_r=a_r, a_i=a_i):
            r0 = pl.multiple_of(bt * SUBLANES, SUBLANES)

            def step(l, hc):
                hr, hi = hc
                row = pl.multiple_of(l * nb + r0, SUBLANES)
                nr = a_r * hr - a_i * hi + s_scr[pl.ds(row, SUBLANES), re_cols]
                ni = a_r * hi + a_i * hr + s_scr[pl.ds(row, SUBLANES), im_cols]
                s_scr[pl.ds(row, SUBLANES), re_cols] = nr
                s_scr[pl.ds(row, SUBLANES), im_cols] = ni
                return nr, ni

            hr, hi = lax.fori_loop(
                0, steps, step,
                (hr_scr[pl.ds(r0, SUBLANES), cols], hi_scr[pl.ds(r0, SUBLANES), cols]),
                unroll=min(steps, 4))
            hr_scr[pl.ds(r0, SUBLANES), cols] = hr
            hi_scr[pl.ds(r0, SUBLANES), cols] = hi
            return 0

        lax.fori_loop(0, nb // SUBLANES, tile_body, 0)

    hro_ref[...] = hr_scr[...]
    hio_ref[...] = hi_scr[...]

    y = jnp.concatenate(
        [_bdot(s_scr[:, 2 * k * HALF_STATE:(2 * k + 1) * HALF_STATE], wcr_ref[k])
         - _bdot(s_scr[:, (2 * k + 1) * HALF_STATE:(2 * k + 2) * HALF_STATE], wci_ref[k])
         for k in range(2)], axis=-1) + dsk_ref[...] * u
    g = _gelu_tanh(y)
    y_ssm = _rms(g * _sigmoid(_bdot(g, wglu_ref[...]) + bglu_ref[...]), gsm_ref[...])

    mixed = _bdot(y_conv, wout_ref[0:CONV_DIM, :]) + _bdot(y_ssm, wout_ref[CONV_DIM:2 * CONV_DIM, :])
    x1_ref[...] = x + _per_seq(mixed, nb, gt_ref[...])


def _const_spec(shape):
    return pl.BlockSpec(shape, lambda i: (0,) * len(shape), pipeline_mode=pl.Buffered(1))


def _mixer(x_tm, nb, steps_total, chunk, sh, sc, gt, h0r, h0i, cb0, weights):
    rows = nb * chunk
    nchunks = steps_total // chunk
    in_specs = [
        pl.BlockSpec((rows, D_MODEL), lambda i: (i, 0)),
        _const_spec((nb, D_MODEL)), _const_spec((nb, D_MODEL)), _const_spec((nb, D_MODEL)),
        _const_spec((nb, N_STATE)), _const_spec((nb, N_STATE)), _const_spec((2 * nb, CONV_DIM)),
    ] + [_const_spec(w.shape) for w in weights]
    out_specs = [
        pl.BlockSpec((rows, D_MODEL), lambda i: (i, 0)),
        _const_spec((2 * nb, CONV_DIM)), _const_spec((nb, N_STATE)), _const_spec((nb, N_STATE)),
    ]
    out_shape = [
        jax.ShapeDtypeStruct((steps_total * nb, D_MODEL), F32),
        jax.ShapeDtypeStruct((2 * nb, CONV_DIM), F32),
        jax.ShapeDtypeStruct((nb, N_STATE), F32),
        jax.ShapeDtypeStruct((nb, N_STATE), F32),
    ]
    scratch = [
        pltpu.VMEM((rows, IN_DIM), F32),
        pltpu.VMEM((rows + 2 * nb, CONV_DIM), F32),
        pltpu.VMEM((rows, 2 * N_STATE), F32),
        pltpu.VMEM((nb, N_STATE), F32),
        pltpu.VMEM((nb, N_STATE), F32),
    ]
    return pl.pallas_call(
        functools.partial(_mixer_kernel, nb, chunk),
        grid=(nchunks,),
        in_specs=in_specs, out_specs=out_specs, out_shape=out_shape,
        scratch_shapes=scratch,
        compiler_params=pltpu.CompilerParams(
            dimension_semantics=("arbitrary",), vmem_limit_bytes=VMEM_LIMIT),
        name="mixer",
    )(x_tm, sh, sc, gt, h0r, h0i, cb0, *weights)


TOK_BLOCK = 512
SEG_ALIGN = 16
ROW_TILE = 512
CHUNKS_PER_TILE = ROW_TILE // SEG_ALIGN
N_SLOTS = -(-(2 * TOK_BLOCK + N_EXPERTS * (SEG_ALIGN - 1)) // 256) * 256
SPARE_ROW = N_SLOTS - SEG_ALIGN
assert SPARE_ROW >= 2 * TOK_BLOCK + N_EXPERTS * (SEG_ALIGN - 1)
ROUTER_ROWS = 48
META_ROWS = SUBLANES

_NT = (((1,), (1,)), ((), ()))
_TN = (((0,), (0,)), ((), ()))


def _iota_f32(shape, dim):
    return lax.broadcasted_iota(jnp.int32, shape, dim).astype(F32)


def _split_bf16(v):
    hi = v.astype(BF16)
    return hi, (v - hi.astype(F32)).astype(BF16)


def _route_t(lg):
    row = _iota_f32(lg.shape, 0)
    neg = jnp.float32(-jnp.inf)
    none = float(ROUTER_ROWS)
    is_g = row < N_EXPERT_GROUPS
    gl = jnp.where(is_g, lg, neg)
    gmax = jnp.max(gl, axis=0, keepdims=True)
    gsum = jnp.sum(jnp.where(is_g, jnp.exp(gl - gmax), 0.0), axis=0, keepdims=True)
    g_prob = 1.0 / gsum
    g_idx = jnp.min(jnp.where(gl == gmax, row, none), axis=0, keepdims=True)
    lo = N_EXPERT_GROUPS + EXPERTS_PER_GROUP * g_idx
    el = jnp.where(row >= lo, jnp.where(row < lo + EXPERTS_PER_GROUP, lg, neg), neg)
    m1 = jnp.max(el, axis=0, keepdims=True)
    i1 = jnp.min(jnp.where(el == m1, row, none), axis=0, keepdims=True)
    el2 = jnp.where(row == i1, neg, el)
    m2 = jnp.max(el2, axis=0, keepdims=True)
    i2 = jnp.min(jnp.where(el2 == m2, row, none), axis=0, keepdims=True)
    e21 = jnp.exp(m2 - m1)
    return i1, i2, g_prob / (1.0 + e21), g_prob * e21 / (1.0 + e21)


def _dispatch_kernel(nbp, x1p_ref, x1s_ref, sc_ref, sh_ref, n2g_ref, wr_ref, br_ref,
                     xs_ref, meta_ref, cnt_ref):
    x1 = jnp.where(pl.program_id(0) < nbp, x1p_ref[...], x1s_ref[...])
    h = _rms(x1, n2g_ref[...]) * (1.0 + sc_ref[0]) + sh_ref[0]
    t = h.shape[0]
    hb, hl = _split_bf16(h)
    wh, wl = _split_bf16(wr_ref[...])
    lg = (lax.dot_general(wh, hb, _NT, preferred_element_type=F32)
          + lax.dot_general(wh, hl, _NT, preferred_element_type=F32)
          + lax.dot_general(wl, hb, _NT, preferred_element_type=F32)) + br_ref[...]
    i1, i2, w1, w2 = _route_t(lg)

    row = _iota_f32((ROUTER_ROWS, t), 0)
    hit1 = row == i1
    hit2 = row == i2
    onehot = jnp.where(hit1, 1.0, 0.0) + jnp.where(hit2, 1.0, 0.0)
    before = jnp.where(_iota_f32((t, t), 0) < _iota_f32((t, t), 1), 1.0, 0.0)
    rank = _bdot(onehot, before)
    cnt = jnp.sum(onehot, axis=1, keepdims=True)
    seg = jnp.floor((cnt + (SEG_ALIGN - 1)) * (1.0 / SEG_ALIGN)) * SEG_ALIGN
    earlier = jnp.where(_iota_f32((ROUTER_ROWS, ROUTER_ROWS), 1) < _iota_f32((ROUTER_ROWS, ROUTER_ROWS), 0),
                        1.0, 0.0)
    base = rank + _bdot(earlier, jnp.broadcast_to(seg, (ROUTER_ROWS, t)))
    slot1 = jnp.sum(jnp.where(hit1, base, 0.0), axis=0, keepdims=True)
    slot2 = jnp.sum(jnp.where(hit2, base, 0.0), axis=0, keepdims=True)
    j = _iota_f32((N_SLOTS, t), 0)
    place = jnp.where(j == slot1, 1.0, 0.0) + jnp.where(j == slot2, 1.0, 0.0)
    xs_ref[...] = jnp.dot(place.astype(BF16), hb, preferred_element_type=F32).astype(BF16)
    r = _iota_f32((META_ROWS, t), 0)
    meta_ref[0] = jnp.where(r == 0.0, slot1, jnp.where(r == 1.0, slot2,
                            jnp.where(r == 2.0, w1, jnp.where(r == 3.0, w2, 0.0))))
    cnt_ref[0] = jnp.broadcast_to(cnt, (ROUTER_ROWS, LANES))


def _combine_kernel(nbp, x1p_ref, x1s_ref, gt_ref, meta_ref, ys_ref, fg_ref, yp_ref, ysm_ref):
    i = pl.program_id(0)
    x1 = jnp.where(i < nbp, x1p_ref[...], x1s_ref[...])
    meta = meta_ref[0]
    slot1, slot2, w1, w2 = (meta[k:k + 1, :] for k in range(4))
    j = _iota_f32((N_SLOTS, x1.shape[0]), 0)
    pick = jnp.where(j == slot1, w1, 0.0) + jnp.where(j == slot2, w2, 0.0)
    moe = lax.dot_general(pick.astype(BF16), ys_ref[...], _TN, preferred_element_type=F32)
    y = _rms(x1 + gt_ref[0] * moe, fg_ref[...])

    @pl.when(i < nbp)
    def _():
        yp_ref[...] = y

    @pl.when(i >= nbp)
    def _():
        ysm_ref[...] = y


def _expert_kernel(te_ref, tnv_ref, segm_ref, segsrc_ref, nt_ref, xs_hbm, wg_ref, wu_ref, wd_ref,
                   ys_hbm, xbuf, ybuf, wgb, wub, wdb, rows, cur, in_sem, out_sem):
    t = pl.program_id(0)
    nt = nt_ref[0]
    slot = t % 2

    def chunk_rows(j):
        return pl.ds(pl.multiple_of(j * SEG_ALIGN, SEG_ALIGN), SEG_ALIGN)

    def hbm_rows(ref, r):
        return ref.at[pl.ds(pl.multiple_of(r, SEG_ALIGN), SEG_ALIGN), :]

    def in_copy(r, j, s):
        return pltpu.make_async_copy(hbm_rows(xs_hbm, r), xbuf.at[s, chunk_rows(j), :], in_sem.at[s])

    def out_copy(r, j, s):
        return pltpu.make_async_copy(ybuf.at[s, chunk_rows(j), :], hbm_rows(ys_hbm, r), out_sem.at[s])

    def plan_rows(tile, s):
        nv = tnv_ref[tile]

        def take(c):
            j, seg, k = c
            m = segm_ref[seg]
            n = jnp.minimum(jnp.maximum(m - k, 0), nv - j)
            base = segsrc_ref[seg] + SEG_ALIGN * k

            def put(i, _):
                rows[s, j + i] = base + SEG_ALIGN * i
                return 0

            lax.fori_loop(0, n, put, 0)
            done = k + n >= m
            return j + n, jnp.where(done, seg + 1, seg), jnp.where(done, 0, k + n)

        _, seg, k = lax.while_loop(lambda c: c[0] < nv, take, (0, cur[0], cur[1]))
        cur[0] = seg
        cur[1] = k

        def spare(i, _):
            rows[s, i] = SPARE_ROW
            return 0

        lax.fori_loop(nv, CHUNKS_PER_TILE, spare, 0)

    def start_gather(s):
        for j in range(CHUNKS_PER_TILE):
            in_copy(rows[s, j], j, s).start()

    def wait_gather(s):
        pltpu.make_async_copy(xs_hbm.at[pl.ds(0, ROW_TILE), :], xbuf.at[s], in_sem.at[s]).wait()

    def wait_scatter(tile, s):
        nv = tnv_ref[tile]

        @pl.when(nv == CHUNKS_PER_TILE)
        def _():
            pltpu.make_async_copy(ybuf.at[s], ys_hbm.at[pl.ds(0, ROW_TILE), :], out_sem.at[s]).wait()

        @pl.when(nv < CHUNKS_PER_TILE)
        def _():
            def body(j, _):
                out_copy(0, j, s).wait()
                return 0

            lax.fori_loop(0, nv, body, 0)

    def run_tile(full):
        start_gather(1 - slot)
        wait_gather(slot)
        x = xbuf[slot]
        a = jnp.dot(x, wgb[...], preferred_element_type=F32)
        b = jnp.dot(x, wub[...], preferred_element_type=F32)
        o = jnp.dot((a * _sigmoid(a) * b).astype(BF16), wdb[...], preferred_element_type=F32)
        ybuf[slot] = o.astype(BF16)
        if full:
            for j in range(CHUNKS_PER_TILE):
                out_copy(rows[slot, j], j, slot).start()
        else:
            def body(j, _):
                out_copy(rows[slot, j], j, slot).start()
                return 0

            lax.fori_loop(0, tnv_ref[t], body, 0)

    @pl.when(t == 0)
    def _():
        cur[0] = 0
        cur[1] = 0
        plan_rows(0, 0)
        start_gather(0)

    @pl.when(jnp.logical_or(t == 0, te_ref[t] != te_ref[jnp.maximum(t - 1, 0)]))
    def _():
        wgb[...] = wg_ref[0].astype(BF16)
        wub[...] = wu_ref[0].astype(BF16)
        wdb[...] = wd_ref[0].astype(BF16)

    @pl.when(t < nt)
    def _():
        plan_rows(t + 1, 1 - slot)

        @pl.when(t >= 2)
        def _():
            wait_scatter(t - 2, slot)

        pl.when(tnv_ref[t] == CHUNKS_PER_TILE)(lambda: run_tile(True))
        pl.when(tnv_ref[t] < CHUNKS_PER_TILE)(lambda: run_tile(False))

        @pl.when(t == nt - 1)
        def _():
            wait_gather(1 - slot)
            wait_scatter(t, slot)

            @pl.when(t >= 1)
            def _():
                wait_scatter(t - 1, 1 - slot)


def _tile_plan(cnt):
    nblk = cnt.shape[0]
    max_chunks = nblk * (2 * TOK_BLOCK // SEG_ALIGN + N_EXPERTS)
    max_tiles = max_chunks // CHUNKS_PER_TILE + N_EXPERTS
    m = (cnt + SEG_ALIGN - 1) // SEG_ALIGN
    seg_start = jnp.cumsum(m, axis=1) - m
    src0 = jnp.arange(nblk, dtype=jnp.int32)[:, None] * N_SLOTS + SEG_ALIGN * seg_start
    seg_m = jnp.concatenate([m.T.reshape(-1), jnp.full((1,), max_chunks, jnp.int32)])
    seg_src = jnp.concatenate([src0.T.reshape(-1), jnp.zeros((1,), jnp.int32)])
    ce = jnp.sum(m, axis=0)
    tiles = (ce + CHUNKS_PER_TILE - 1) // CHUNKS_PER_TILE
    t_cum = jnp.cumsum(tiles)
    nt = t_cum[-1]
    k = jnp.arange(max_tiles + 1, dtype=jnp.int32)
    kk = jnp.minimum(k, nt - 1)
    ek = jnp.minimum(jnp.sum((t_cum[None, :] <= kk[:, None]).astype(jnp.int32), axis=1), N_EXPERTS - 1)
    is_e = ek[:, None] == jnp.arange(N_EXPERTS, dtype=jnp.int32)[None, :]
    local = kk - jnp.sum(jnp.where(is_e, (t_cum - tiles)[None, :], 0), axis=1)
    left = jnp.sum(jnp.where(is_e, ce[None, :], 0), axis=1) - CHUNKS_PER_TILE * local
    tile_nv = jnp.where(k < nt, jnp.clip(left, 0, CHUNKS_PER_TILE), 0)
    return tuple(v.astype(jnp.int32) for v in (ek, tile_nv, seg_m, seg_src, nt.reshape(1)))


def _moe(x1p, x1s, sc, sh, gt, n2g, wr, br, wg, wu, wd, fg):
    nbp = x1p.shape[0] // TOK_BLOCK
    nblk = nbp + 1
    blk = (TOK_BLOCK, D_MODEL)
    p_spec = pl.BlockSpec(blk, lambda i: (jnp.minimum(i, nbp - 1), 0))
    s_spec = pl.BlockSpec(blk, lambda i: (0, 0))
    mod_spec = pl.BlockSpec((1,) + blk, lambda i: (i // nbp, 0, 0))
    meta_spec = pl.BlockSpec((1, META_ROWS, TOK_BLOCK), lambda i: (i, 0, 0))
    slots_spec = pl.BlockSpec((N_SLOTS, D_MODEL), lambda i: (i, 0))
    params = pltpu.CompilerParams(dimension_semantics=("arbitrary",), vmem_limit_bytes=VMEM_LIMIT)

    def cs(shape):
        return pl.BlockSpec(shape, lambda i: (0,) * len(shape))

    xs, meta, cnt = pl.pallas_call(
        functools.partial(_dispatch_kernel, nbp),
        grid=(nblk,),
        in_specs=[p_spec, s_spec, mod_spec, mod_spec, cs((1, D_MODEL)),
                  cs((ROUTER_ROWS, D_MODEL)), cs((ROUTER_ROWS, TOK_BLOCK))],
        out_specs=[slots_spec, meta_spec,
                   pl.BlockSpec((1, ROUTER_ROWS, LANES), lambda i: (i, 0, 0))],
        out_shape=[jax.ShapeDtypeStruct((nblk * N_SLOTS, D_MODEL), BF16),
                   jax.ShapeDtypeStruct((nblk, META_ROWS, TOK_BLOCK), F32),
                   jax.ShapeDtypeStruct((nblk, ROUTER_ROWS, LANES), F32)],
        compiler_params=params,
        name="moe_dispatch",
    )(x1p, x1s, sc, sh, n2g, wr, br)

    plan = _tile_plan(cnt[:, N_EXPERT_GROUPS:N_EXPERT_GROUPS + N_EXPERTS, 0].astype(jnp.int32))
    max_tiles = plan[0].shape[0] - 1

    def w_spec(shape):
        return pl.BlockSpec((1,) + shape, lambda t, te, *_: (te[t], 0, 0))

    ys = pl.pallas_call(
        _expert_kernel,
        grid_spec=pltpu.PrefetchScalarGridSpec(
            num_scalar_prefetch=len(plan),
            grid=(max_tiles,),
            in_specs=[pl.BlockSpec(memory_space=pl.ANY),
                      w_spec((D_MODEL, D_EXPERT)), w_spec((D_MODEL, D_EXPERT)), w_spec((D_EXPERT, D_MODEL))],
            out_specs=pl.BlockSpec(memory_space=pl.ANY),
            scratch_shapes=[pltpu.VMEM((2, ROW_TILE, D_MODEL), BF16),
                            pltpu.VMEM((2, ROW_TILE, D_MODEL), BF16),
                            pltpu.VMEM((D_MODEL, D_EXPERT), BF16),
                            pltpu.VMEM((D_MODEL, D_EXPERT), BF16),
                            pltpu.VMEM((D_EXPERT, D_MODEL), BF16),
                            pltpu.SMEM((2, CHUNKS_PER_TILE), jnp.int32),
                            pltpu.SMEM((2,), jnp.int32),
                            pltpu.SemaphoreType.DMA((2,)), pltpu.SemaphoreType.DMA((2,))]),
        out_shape=jax.ShapeDtypeStruct(xs.shape, BF16),
        input_output_aliases={len(plan): 0},
        compiler_params=params,
        name="moe_experts",
    )(*plan, xs, wg, wu, wd)

    return pl.pallas_call(
        functools.partial(_combine_kernel, nbp),
        grid=(nblk,),
        in_specs=[p_spec, s_spec, mod_spec, meta_spec, slots_spec, cs((1, D_MODEL))],
        out_specs=[p_spec, s_spec],
        out_shape=[jax.ShapeDtypeStruct(x1p.shape, F32), jax.ShapeDtypeStruct(x1s.shape, F32)],
        compiler_params=params,
        name="moe_combine",
    )(x1p, x1s, gt, meta, ys, fg)


def _block_diag_in(bbt):
    eye = jnp.eye(SSM_GROUPS, dtype=bbt.dtype)
    blocks = bbt.reshape(SSM_GROUP, SSM_GROUPS, SSM_STATE)
    w = eye[:, None, :, None] * blocks.transpose(1, 0, 2)[:, :, None, :]
    return w.reshape(SSM_DIM, N_STATE)


def _block_diag_out(c):
    eye = jnp.eye(SSM_GROUPS, dtype=c.dtype)
    w = eye[:, None, :, None] * c.transpose(0, 2, 1)[:, :, None, :]
    return w.reshape(N_STATE, SSM_DIM)


def _to_time_major(x):
    b, s, d = x.shape
    return x.transpose(1, 0, 2).reshape(s * b, d)


def _from_time_major(x, b, s):
    return x.reshape(s, b, -1).transpose(1, 0, 2)


def kernel(x_prompt, x_sample, state_conv, state_ssm_re, state_ssm_im, c_prompt, c_sample, w_ada, b_ada, norm1_g, norm2_g, w_in, conv_w, lambda_re, lambda_im, log_dt, ssm_b_re, ssm_b_im, ssm_c_re, ssm_c_im, ssm_d, w_glu, b_glu, out_norm_conv_g, out_norm_ssm_g, w_out, w_router_group, b_router_group, w_router_expert, b_router_expert, w_expert_gate, w_expert_up, w_expert_down, final_norm_g):
    assert w_ada.shape[0] == 1, "single-layer step"
    bp, sp, _ = x_prompt.shape
    bs, ss, _ = x_sample.shape

    ar, ai, bbr, bbi = _discretise(lambda_re[0], lambda_im[0], log_dt[0], ssm_b_re[0], ssm_b_im[0])
    def halves(w, rows, cols):
        return [w[k * rows:(k + 1) * rows, k * cols:(k + 1) * cols] for k in range(2)]

    w_b = jnp.stack([jnp.concatenate([r, i], axis=1) for r, i in zip(
        halves(_block_diag_in(bbr), HALF_SSM, HALF_STATE),
        halves(_block_diag_in(bbi), HALF_SSM, HALF_STATE))]).astype(BF16)
    w_cr = jnp.stack(halves(_block_diag_out(ssm_c_re[0]), HALF_STATE, HALF_SSM)).astype(BF16)
    w_ci = jnp.stack(halves(_block_diag_out(ssm_c_im[0]), HALF_STATE, HALF_SSM)).astype(BF16)

    mod = _adaln(jnp.concatenate([c_prompt, c_sample], axis=0), w_ada[0], b_ada[0])
    sh1, sc1, gt1, sh2, sc2, gt2 = [mod[:, k * D_MODEL:(k + 1) * D_MODEL] for k in range(N_MOD)]

    mixer_weights = (
        norm1_g[0].reshape(1, -1), w_in[0].astype(BF16), conv_w[0], ar, ai, w_b, w_cr, w_ci,
        ssm_d[0].reshape(1, -1), w_glu[0].astype(BF16), b_glu[0].reshape(1, -1),
        out_norm_conv_g[0].reshape(1, -1), out_norm_ssm_g[0].reshape(1, -1), w_out[0].astype(BF16),
    )

    pad_rows = ROUTER_ROWS - N_EXPERT_GROUPS - N_EXPERTS
    w_router = jnp.concatenate(
        [w_router_group[0].T, w_router_expert[0].transpose(0, 2, 1).reshape(N_EXPERTS, D_MODEL),
         jnp.zeros((pad_rows, D_MODEL), F32)], axis=0)
    b_router = jnp.concatenate(
        [b_router_group[0], b_router_expert[0].reshape(-1), jnp.zeros((pad_rows,), F32)])
    b_router = jnp.broadcast_to(b_router[:, None], (ROUTER_ROWS, TOK_BLOCK))
    moe_weights = (norm2_g[0].reshape(1, -1), w_router, b_router,
                   w_expert_gate[0], w_expert_up[0], w_expert_down[0], final_norm_g.reshape(1, -1))

    def mix(x, lo, hi, h0r, h0i, cb0, chunk):
        nb, s, _ = x.shape
        x1, cbo, hro, hio = _mixer(
            _to_time_major(x), nb, s, chunk, sh1[lo:hi], sc1[lo:hi], gt1[lo:hi],
            h0r, h0i, cb0, mixer_weights)
        new_conv = cbo.reshape(2, nb, CONV_DIM).transpose(1, 0, 2)[None]
        new_re = hro.reshape(1, nb, SSM_GROUPS, SSM_STATE)
        new_im = hio.reshape(1, nb, SSM_GROUPS, SSM_STATE)
        return x1, new_conv, new_re, new_im

    zero_state = jnp.zeros((bp, N_STATE), F32)
    zero_conv = jnp.zeros((2 * bp, CONV_DIM), F32)
    x1p, conv_p, re_p, im_p = mix(x_prompt, 0, bp, zero_state, zero_state, zero_conv, 64)
    x1s, conv_s, re_s, im_s = mix(
        x_sample, bp, bp + bs,
        state_ssm_re[0].reshape(bs, N_STATE), state_ssm_im[0].reshape(bs, N_STATE),
        state_conv[0].transpose(1, 0, 2).reshape(2 * bs, CONV_DIM), ss)

    assert bs * ss == TOK_BLOCK and (bp * sp) % TOK_BLOCK == 0 and TOK_BLOCK % bp == 0

    def block_rows(m):
        return jnp.stack([jnp.tile(m[:bp], (TOK_BLOCK // bp, 1)), jnp.tile(m[bp:], (TOK_BLOCK // bs, 1))])

    y_p, y_s = _moe(x1p, x1s, block_rows(sc2), block_rows(sh2), block_rows(gt2), *moe_weights)
    return (_from_time_major(y_p, bp, sp), _from_time_major(y_s, bs, ss),
            conv_p, re_p, im_p, conv_s, re_s, im_s)
```

```python
import functools

import jax
import jax.numpy as jnp
from jax import lax
from jax.experimental import pallas as pl
from jax.experimental.pallas import tpu as pltpu

D_MODEL = 1024
CONV_DIM = 512
SSM_DIM = 512
SSM_GROUP = 16
SSM_GROUPS = 32
SSM_STATE = 64
N_STATE = SSM_GROUPS * SSM_STATE
HALF_SSM = SSM_DIM // 2
HALF_STATE = N_STATE // 2
IN_DIM = 2048
N_EXPERT_GROUPS = 4
EXPERTS_PER_GROUP = 8
N_EXPERTS = 32
D_EXPERT = 256
N_MOD = 6
EPS = 1e-6

SUBLANES = 8
LANES = 128
SCAN_COLS = 512
STRIDE_PAD = 8
VMEM_LIMIT = 60 * 1024 * 1024

F32 = jnp.float32
BF16 = jnp.bfloat16


def _sigmoid(x):
    return 1.0 / (1.0 + jnp.exp(-x))


def _gelu_tanh(x):
    return 0.5 * x * (1.0 + jnp.tanh(0.7978845608028654 * (x + 0.044715 * (x * x * x))))


def _rms(x, g):
    return x * lax.rsqrt(jnp.mean(x * x, axis=-1, keepdims=True) + EPS) * g


def _bdot(a, b):
    return jnp.dot(a.astype(BF16), b.astype(BF16), preferred_element_type=F32)


def _per_seq(v, nb, scale, shift=None):
    rows, d = v.shape
    v3 = v.reshape(rows // nb, nb, d) * scale[None]
    if shift is not None:
        v3 = v3 + shift[None]
    return v3.reshape(rows, d)


def _disc_kernel(lr_ref, li_ref, ldt_ref, brt_ref, bit_ref, ar_ref, ai_ref, bbr_ref, bbi_ref):
    lr = lr_ref[...]
    li = li_ref[...]
    dt = jnp.exp(ldt_ref[...])
    mag = jnp.exp(lr * dt)
    ar = mag * jnp.cos(li * dt)
    ai = mag * jnp.sin(li * dt)
    den = lr * lr + li * li
    fr = ((ar - 1.0) * lr + ai * li) / den
    fi = (ai * lr - (ar - 1.0) * li) / den
    ar_ref[...] = ar
    ai_ref[...] = ai
    brt = brt_ref[...]
    bit = bit_ref[...]
    bbr_ref[...] = fr * brt - fi * bit
    bbi_ref[...] = fr * bit + fi * brt


def _discretise(lambda_re, lambda_im, log_dt, b_re, b_im):
    lr = lambda_re.reshape(1, N_STATE)
    li = lambda_im.reshape(1, N_STATE)
    ldt = jnp.repeat(log_dt, SSM_STATE).reshape(1, N_STATE)
    brt = b_re.transpose(2, 0, 1).reshape(SSM_GROUP, N_STATE)
    bit = b_im.transpose(2, 0, 1).reshape(SSM_GROUP, N_STATE)
    row = jax.ShapeDtypeStruct((1, N_STATE), F32)
    mat = jax.ShapeDtypeStruct((SSM_GROUP, N_STATE), F32)
    return pl.pallas_call(_disc_kernel, out_shape=(row, row, mat, mat), name="s5_discretise")(
        lr, li, ldt, brt, bit)


def _ada_kernel(c_ref, w_ref, b_ref, o_ref):
    c = c_ref[...]
    o_ref[...] = _bdot(c * _sigmoid(c), w_ref[...]) + b_ref[...]


def _adaln(c_all, w_ada, b_ada):
    n = c_all.shape[0]
    nblk = N_MOD
    return pl.pallas_call(
        _ada_kernel,
        grid=(nblk,),
        in_specs=[
            pl.BlockSpec((n, D_MODEL), lambda j: (0, 0)),
            pl.BlockSpec((D_MODEL, D_MODEL), lambda j: (0, j)),
            pl.BlockSpec((1, D_MODEL), lambda j: (0, j)),
        ],
        out_specs=pl.BlockSpec((n, D_MODEL), lambda j: (0, j)),
        out_shape=jax.ShapeDtypeStruct((n, N_MOD * D_MODEL), F32),
        compiler_params=pltpu.CompilerParams(
            dimension_semantics=("arbitrary",), vmem_limit_bytes=VMEM_LIMIT),
        name="adaln",
    )(c_all, w_ada, b_ada.reshape(1, -1))


def _rows_to_time_major(x_ref, slab, xt_scr):
    nb, steps, d = x_ref.shape
    pitch = slab.shape[1] // nb
    for b in range(nb):
        for j in range(d // LANES):
            slab[j, b * pitch:b * pitch + steps, :] = x_ref[b, :, j * LANES:(j + 1) * LANES]
    for l in range(steps):
        for j in range(d // LANES):
            xt_scr[l * nb:(l + 1) * nb, j * LANES:(j + 1) * LANES] = slab[j, pl.ds(l, nb, stride=pitch), :]


def _mixer_kernel(nb, steps, batch_major,
                  x_ref, sh_ref, sc_ref, gt_ref, h0r_ref, h0i_ref, cb0_ref,
                  n1g_ref, win_ref, cw_ref, ar_ref, ai_ref, wb_ref, wcr_ref, wci_ref,
                  dsk_ref, wglu_ref, bglu_ref, gcv_ref, gsm_ref, wout_ref,
                  x1_ref, cbo_ref, hro_ref, hio_ref,
                  z_scr, cv_scr, s_scr, hr_scr, hi_scr, *tm_scr):
    rows = nb * steps
    i = pl.program_id(0)

    @pl.when(i == 0)
    def _():
        hr_scr[...] = h0r_ref[...]
        hi_scr[...] = h0i_ref[...]
        cv_scr[0:2 * nb, :] = cb0_ref[...]

    if batch_major:
        slab, xt_scr = tm_scr
        _rows_to_time_major(x_ref, slab, xt_scr)
        x = xt_scr[...]
    else:
        x = x_ref[...]
    h = _per_seq(_rms(x, n1g_ref[...]), nb, 1.0 + sc_ref[...], sh_ref[...])
    z_scr[...] = _bdot(h, win_ref[...])

    cv_scr[2 * nb:2 * nb + rows, :] = z_scr[:, CONV_DIM:2 * CONV_DIM] * z_scr[:, 2 * CONV_DIM:3 * CONV_DIM]
    conv = (cv_scr[0:rows, :] * cw_ref[0:1, :]
            + cv_scr[nb:nb + rows, :] * cw_ref[1:2, :]
            + cv_scr[2 * nb:2 * nb + rows, :] * cw_ref[2:3, :])
    y_conv = _rms(z_scr[:, 0:CONV_DIM] * conv, gcv_ref[...])
    carry = cv_scr[rows:rows + 2 * nb, :]
    cv_scr[0:2 * nb, :] = carry
    cbo_ref[...] = carry

    u = z_scr[:, 3 * CONV_DIM:IN_DIM]
    for k in range(2):
        s_scr[:, 2 * k * HALF_STATE:2 * (k + 1) * HALF_STATE] = _bdot(
            u[:, k * HALF_SSM:(k + 1) * HALF_SSM], wb_ref[k])

    for cg in range(N_STATE // SCAN_COLS):
        cols = slice(cg * SCAN_COLS, (cg + 1) * SCAN_COLS)
        half, off = divmod(cg * SCAN_COLS, HALF_STATE)
        re_cols = slice(2 * half * HALF_STATE + off, 2 * half * HALF_STATE + off + SCAN_COLS)
        im_cols = slice(re_cols.start + HALF_STATE, re_cols.stop + HALF_STATE)
        a_r = jnp.broadcast_to(ar_ref[:, cols], (SUBLANES, SCAN_COLS))
        a_i = jnp.broadcast_to(ai_ref[:, cols], (SUBLANES, SCAN_COLS))

        def tile_body(bt, _, cols=cols, re_cols=re_cols, im_cols=im_cols, a_r=a_r, a_i=a_i):
            r0 = pl.multiple_of(bt * SUBLANES, SUBLANES)

            def step(l, hc):
                hr, hi = hc
                row = pl.multiple_of(l * nb + r0, SUBLANES)
                nr = a_r * hr - a_i * hi + s_scr[pl.ds(row, SUBLANES), re_cols]
                ni = a_r * hi + a_i * hr + s_scr[pl.ds(row, SUBLANES), im_cols]
                s_scr[pl.ds(row, SUBLANES), re_cols] = nr
                s_scr[pl.ds(row, SUBLANES), im_cols] = ni
                return nr, ni

            hr, hi = lax.fori_loop(
                0, steps, step,
                (hr_scr[pl.ds(r0, SUBLANES), cols], hi_scr[pl.ds(r0, SUBLANES), cols]),
                unroll=min(steps, 4))
            hr_scr[pl.ds(r0, SUBLANES), cols] = hr
            hi_scr[pl.ds(r0, SUBLANES), cols] = hi
            return 0

        lax.fori_loop(0, nb // SUBLANES, tile_body, 0)

    hro_ref[...] = hr_scr[...]
    hio_ref[...] = hi_scr[...]

    y = jnp.concatenate(
        [_bdot(s_scr[:, 2 * k * HALF_STATE:(2 * k + 1) * HALF_STATE], wcr_ref[k])
         - _bdot(s_scr[:, (2 * k + 1) * HALF_STATE:(2 * k + 2) * HALF_STATE], wci_ref[k])
         for k in range(2)], axis=-1) + dsk_ref[...] * u
    g = _gelu_tanh(y)
    y_ssm = _rms(g * _sigmoid(_bdot(g, wglu_ref[...]) + bglu_ref[...]), gsm_ref[...])

    mixed = _bdot(y_conv, wout_ref[0:CONV_DIM, :]) + _bdot(y_ssm, wout_ref[CONV_DIM:2 * CONV_DIM, :])
    x1_ref[...] = x + _per_seq(mixed, nb, gt_ref[...])


def _const_spec(shape):
    return pl.BlockSpec(shape, lambda i: (0,) * len(shape), pipeline_mode=pl.Buffered(1))


def _mixer(x, nb, steps_total, chunk, sh, sc, gt, h0r, h0i, cb0, weights):
    rows = nb * chunk
    nchunks = steps_total // chunk
    batch_major = x.ndim == 3
    if batch_major:
        assert nb == SUBLANES
        x_spec = pl.BlockSpec((nb, chunk, D_MODEL), lambda i: (0, i, 0))
        tm_scratch = [pltpu.VMEM((D_MODEL // LANES, nb * (chunk + STRIDE_PAD), LANES), F32),
                      pltpu.VMEM((rows, D_MODEL), F32)]
    else:
        x_spec = pl.BlockSpec((rows, D_MODEL), lambda i: (i, 0))
        tm_scratch = []
    in_specs = [
        x_spec,
        _const_spec((nb, D_MODEL)), _const_spec((nb, D_MODEL)), _const_spec((nb, D_MODEL)),
        _const_spec((nb, N_STATE)), _const_spec((nb, N_STATE)), _const_spec((2 * nb, CONV_DIM)),
    ] + [_const_spec(w.shape) for w in weights]
    out_specs = [
        pl.BlockSpec((rows, D_MODEL), lambda i: (i, 0)),
        _const_spec((2 * nb, CONV_DIM)), _const_spec((nb, N_STATE)), _const_spec((nb, N_STATE)),
    ]
    out_shape = [
        jax.ShapeDtypeStruct((steps_total * nb, D_MODEL), F32),
        jax.ShapeDtypeStruct((2 * nb, CONV_DIM), F32),
        jax.ShapeDtypeStruct((nb, N_STATE), F32),
        jax.ShapeDtypeStruct((nb, N_STATE), F32),
    ]
    scratch = [
        pltpu.VMEM((rows, IN_DIM), F32),
        pltpu.VMEM((rows + 2 * nb, CONV_DIM), F32),
        pltpu.VMEM((rows, 2 * N_STATE), F32),
        pltpu.VMEM((nb, N_STATE), F32),
        pltpu.VMEM((nb, N_STATE), F32),
    ] + tm_scratch
    return pl.pallas_call(
        functools.partial(_mixer_kernel, nb, chunk, batch_major),
        grid=(nchunks,),
        in_specs=in_specs, out_specs=out_specs, out_shape=out_shape,
        scratch_shapes=scratch,
        compiler_params=pltpu.CompilerParams(
            dimension_semantics=("arbitrary",), vmem_limit_bytes=VMEM_LIMIT),
        name="mixer",
    )(x, sh, sc, gt, h0r, h0i, cb0, *weights)


TOK_BLOCK = 512
SEG_ALIGN = 16
ROW_TILE = 512
CHUNKS_PER_TILE = ROW_TILE // SEG_ALIGN
N_SLOTS = -(-(2 * TOK_BLOCK + N_EXPERTS * (SEG_ALIGN - 1)) // 256) * 256
SPARE_ROW = N_SLOTS - SEG_ALIGN
assert SPARE_ROW >= 2 * TOK_BLOCK + N_EXPERTS * (SEG_ALIGN - 1)
ROUTER_ROWS = 48
META_ROWS = SUBLANES

_NT = (((1,), (1,)), ((), ()))
_TN = (((0,), (0,)), ((), ()))


def _iota_f32(shape, dim):
    return lax.broadcasted_iota(jnp.int32, shape, dim).astype(F32)


def _split_bf16(v):
    hi = v.astype(BF16)
    return hi, (v - hi.astype(F32)).astype(BF16)


def _route_t(lg):
    row = _iota_f32(lg.shape, 0)
    neg = jnp.float32(-jnp.inf)
    none = float(ROUTER_ROWS)
    is_g = row < N_EXPERT_GROUPS
    gl = jnp.where(is_g, lg, neg)
    gmax = jnp.max(gl, axis=0, keepdims=True)
    gsum = jnp.sum(jnp.where(is_g, jnp.exp(gl - gmax), 0.0), axis=0, keepdims=True)
    g_prob = 1.0 / gsum
    g_idx = jnp.min(jnp.where(gl == gmax, row, none), axis=0, keepdims=True)
    lo = N_EXPERT_GROUPS + EXPERTS_PER_GROUP * g_idx
    el = jnp.where(row >= lo, jnp.where(row < lo + EXPERTS_PER_GROUP, lg, neg), neg)
    m1 = jnp.max(el, axis=0, keepdims=True)
    i1 = jnp.min(jnp.where(el == m1, row, none), axis=0, keepdims=True)
    el2 = jnp.where(row == i1, neg, el)
    m2 = jnp.max(el2, axis=0, keepdims=True)
    i2 = jnp.min(jnp.where(el2 == m2, row, none), axis=0, keepdims=True)
    e21 = jnp.exp(m2 - m1)
    return i1, i2, g_prob / (1.0 + e21), g_prob * e21 / (1.0 + e21)


def _dispatch_kernel(nbp, x1p_ref, x1s_ref, sc_ref, sh_ref, n2g_ref, wr_ref, br_ref,
                     xs_ref, meta_ref, cnt_ref):
    x1 = jnp.where(pl.program_id(0) < nbp, x1p_ref[...], x1s_ref[...])
    h = _rms(x1, n2g_ref[...]) * (1.0 + sc_ref[0]) + sh_ref[0]
    t = h.shape[0]
    hb, hl = _split_bf16(h)
    wh, wl = _split_bf16(wr_ref[...])
    lg = (lax.dot_general(wh, hb, _NT, preferred_element_type=F32)
          + lax.dot_general(wh, hl, _NT, preferred_element_type=F32)
          + lax.dot_general(wl, hb, _NT, preferred_element_type=F32)) + br_ref[...]
    i1, i2, w1, w2 = _route_t(lg)

    row = _iota_f32((ROUTER_ROWS, t), 0)
    hit1 = row == i1
    hit2 = row == i2
    onehot = jnp.where(hit1, 1.0, 0.0) + jnp.where(hit2, 1.0, 0.0)
    before = jnp.where(_iota_f32((t, t), 0) < _iota_f32((t, t), 1), 1.0, 0.0)
    rank = _bdot(onehot, before)
    cnt = jnp.sum(onehot, axis=1, keepdims=True)
    seg = jnp.floor((cnt + (SEG_ALIGN - 1)) * (1.0 / SEG_ALIGN)) * SEG_ALIGN
    earlier = jnp.where(_iota_f32((ROUTER_ROWS, ROUTER_ROWS), 1) < _iota_f32((ROUTER_ROWS, ROUTER_ROWS), 0),
                        1.0, 0.0)
    base = rank + _bdot(earlier, jnp.broadcast_to(seg, (ROUTER_ROWS, t)))
    slot1 = jnp.sum(jnp.where(hit1, base, 0.0), axis=0, keepdims=True)
    slot2 = jnp.sum(jnp.where(hit2, base, 0.0), axis=0, keepdims=True)
    j = _iota_f32((N_SLOTS, t), 0)
    place = jnp.where(j == slot1, 1.0, 0.0) + jnp.where(j == slot2, 1.0, 0.0)
    xs_ref[...] = jnp.dot(place.astype(BF16), hb, preferred_element_type=F32).astype(BF16)
    r = _iota_f32((META_ROWS, t), 0)
    meta_ref[0] = jnp.where(r == 0.0, slot1, jnp.where(r == 1.0, slot2,
                            jnp.where(r == 2.0, w1, jnp.where(r == 3.0, w2, 0.0))))
    cnt_ref[0] = jnp.broadcast_to(cnt, (ROUTER_ROWS, LANES))


def _combine_kernel(nbp, x1p_ref, x1s_ref, gt_ref, meta_ref, ys_ref, fg_ref, yp_ref, ysm_ref, slab):
    i = pl.program_id(0)
    x1 = jnp.where(i < nbp, x1p_ref[...], x1s_ref[...])
    meta = meta_ref[0]
    slot1, slot2, w1, w2 = (meta[k:k + 1, :] for k in range(4))
    j = _iota_f32((N_SLOTS, x1.shape[0]), 0)
    pick = jnp.where(j == slot1, w1, 0.0) + jnp.where(j == slot2, w2, 0.0)
    moe = lax.dot_general(pick.astype(BF16), ys_ref[...], _TN, preferred_element_type=F32)
    y = _rms(x1 + gt_ref[0] * moe, fg_ref[...])

    @pl.when(i < nbp)
    def _():
        nb, steps, d = yp_ref.shape
        for j in range(d // LANES):
            slab[j] = y[:, j * LANES:(j + 1) * LANES]
        for b in range(nb):
            for j in range(d // LANES):
                yp_ref[b, :, j * LANES:(j + 1) * LANES] = slab[j, pl.ds(b, steps, stride=nb), :]

    @pl.when(i >= nbp)
    def _():
        ysm_ref[...] = y


def _expert_kernel(te_ref, tnv_ref, segm_ref, segsrc_ref, nt_ref, xs_hbm, wg_ref, wu_ref, wd_ref,
                   ys_hbm, xbuf, ybuf, wgb, wub, wdb, rows, cur, in_sem, out_sem):
    t = pl.program_id(0)
    nt = nt_ref[0]
    slot = t % 2

    def chunk_rows(j):
        return pl.ds(pl.multiple_of(j * SEG_ALIGN, SEG_ALIGN), SEG_ALIGN)

    def hbm_rows(ref, r):
        return ref.at[pl.ds(pl.multiple_of(r, SEG_ALIGN), SEG_ALIGN), :]

    def in_copy(r, j, s):
        return pltpu.make_async_copy(hbm_rows(xs_hbm, r), xbuf.at[s, chunk_rows(j), :], in_sem.at[s])

    def out_copy(r, j, s):
        return pltpu.make_async_copy(ybuf.at[s, chunk_rows(j), :], hbm_rows(ys_hbm, r), out_sem.at[s])

    def plan_rows(tile, s):
        nv = tnv_ref[tile]

        def take(c):
            j, seg, k = c
            m = segm_ref[seg]
            n = jnp.minimum(jnp.maximum(m - k, 0), nv - j)
            base = segsrc_ref[seg] + SEG_ALIGN * k

            def put(i, _):
                rows[s, j + i] = base + SEG_ALIGN * i
                return 0

            lax.fori_loop(0, n, put, 0)
            done = k + n >= m
            return j + n, jnp.where(done, seg + 1, seg), jnp.where(done, 0, k + n)

        _, seg, k = lax.while_loop(lambda c: c[0] < nv, take, (0, cur[0], cur[1]))
        cur[0] = seg
        cur[1] = k

        def spare(i, _):
            rows[s, i] = SPARE_ROW
            return 0

        lax.fori_loop(nv, CHUNKS_PER_TILE, spare, 0)

    def start_gather(s):
        for j in range(CHUNKS_PER_TILE):
            in_copy(rows[s, j], j, s).start()

    def wait_gather(s):
        pltpu.make_async_copy(xs_hbm.at[pl.ds(0, ROW_TILE), :], xbuf.at[s], in_sem.at[s]).wait()

    def wait_scatter(tile, s):
        nv = tnv_ref[tile]

        @pl.when(nv == CHUNKS_PER_TILE)
        def _():
            pltpu.make_async_copy(ybuf.at[s], ys_hbm.at[pl.ds(0, ROW_TILE), :], out_sem.at[s]).wait()

        @pl.when(nv < CHUNKS_PER_TILE)
        def _():
            def body(j, _):
                out_copy(0, j, s).wait()
                return 0

            lax.fori_loop(0, nv, body, 0)

    def run_tile(full):
        start_gather(1 - slot)
        wait_gather(slot)
        x = xbuf[slot]
        a = jnp.dot(x, wgb[...], preferred_element_type=F32)
        b = jnp.dot(x, wub[...], preferred_element_type=F32)
        o = jnp.dot((a * _sigmoid(a) * b).astype(BF16), wdb[...], preferred_element_type=F32)
        ybuf[slot] = o.astype(BF16)
        if full:
            for j in range(CHUNKS_PER_TILE):
                out_copy(rows[slot, j], j, slot).start()
        else:
            def body(j, _):
                out_copy(rows[slot, j], j, slot).start()
                return 0

            lax.fori_loop(0, tnv_ref[t], body, 0)

    @pl.when(t == 0)
    def _():
        cur[0] = 0
        cur[1] = 0
        plan_rows(0, 0)
        start_gather(0)

    @pl.when(jnp.logical_or(t == 0, te_ref[t] != te_ref[jnp.maximum(t - 1, 0)]))
    def _():
        wgb[...] = wg_ref[0].astype(BF16)
        wub[...] = wu_ref[0].astype(BF16)
        wdb[...] = wd_ref[0].astype(BF16)

    @pl.when(t < nt)
    def _():
        plan_rows(t + 1, 1 - slot)

        @pl.when(t >= 2)
        def _():
            wait_scatter(t - 2, slot)

        pl.when(tnv_ref[t] == CHUNKS_PER_TILE)(lambda: run_tile(True))
        pl.when(tnv_ref[t] < CHUNKS_PER_TILE)(lambda: run_tile(False))

        @pl.when(t == nt - 1)
        def _():
            wait_gather(1 - slot)
            wait_scatter(t, slot)

            @pl.when(t >= 1)
            def _():
                wait_scatter(t - 1, 1 - slot)


def _tile_plan(cnt):
    nblk = cnt.shape[0]
    max_chunks = nblk * (2 * TOK_BLOCK // SEG_ALIGN + N_EXPERTS)
    max_tiles = max_chunks // CHUNKS_PER_TILE + N_EXPERTS
    m = (cnt + SEG_ALIGN - 1) // SEG_ALIGN
    seg_start = jnp.cumsum(m, axis=1) - m
    src0 = jnp.arange(nblk, dtype=jnp.int32)[:, None] * N_SLOTS + SEG_ALIGN * seg_start
    seg_m = jnp.concatenate([m.T.reshape(-1), jnp.full((1,), max_chunks, jnp.int32)])
    seg_src = jnp.concatenate([src0.T.reshape(-1), jnp.zeros((1,), jnp.int32)])
    ce = jnp.sum(m, axis=0)
    tiles = (ce + CHUNKS_PER_TILE - 1) // CHUNKS_PER_TILE
    t_cum = jnp.cumsum(tiles)
    nt = t_cum[-1]
    k = jnp.arange(max_tiles + 1, dtype=jnp.int32)
    kk = jnp.minimum(k, nt - 1)
    ek = jnp.minimum(jnp.sum((t_cum[None, :] <= kk[:, None]).astype(jnp.int32), axis=1), N_EXPERTS - 1)
    is_e = ek[:, None] == jnp.arange(N_EXPERTS, dtype=jnp.int32)[None, :]
    local = kk - jnp.sum(jnp.where(is_e, (t_cum - tiles)[None, :], 0), axis=1)
    left = jnp.sum(jnp.where(is_e, ce[None, :], 0), axis=1) - CHUNKS_PER_TILE * local
    tile_nv = jnp.where(k < nt, jnp.clip(left, 0, CHUNKS_PER_TILE), 0)
    return tuple(v.astype(jnp.int32) for v in (ek, tile_nv, seg_m, seg_src, nt.reshape(1)))


def _moe(x1p, x1s, bp, sc, sh, gt, n2g, wr, br, wg, wu, wd, fg):
    nbp = x1p.shape[0] // TOK_BLOCK
    nblk = nbp + 1
    blk = (TOK_BLOCK, D_MODEL)
    p_spec = pl.BlockSpec(blk, lambda i: (jnp.minimum(i, nbp - 1), 0))
    s_spec = pl.BlockSpec(blk, lambda i: (0, 0))
    mod_spec = pl.BlockSpec((1,) + blk, lambda i: (i // nbp, 0, 0))
    meta_spec = pl.BlockSpec((1, META_ROWS, TOK_BLOCK), lambda i: (i, 0, 0))
    slots_spec = pl.BlockSpec((N_SLOTS, D_MODEL), lambda i: (i, 0))
    params = pltpu.CompilerParams(dimension_semantics=("arbitrary",), vmem_limit_bytes=VMEM_LIMIT)

    def cs(shape):
        return pl.BlockSpec(shape, lambda i: (0,) * len(shape))

    xs, meta, cnt = pl.pallas_call(
        functools.partial(_dispatch_kernel, nbp),
        grid=(nblk,),
        in_specs=[p_spec, s_spec, mod_spec, mod_spec, cs((1, D_MODEL)),
                  cs((ROUTER_ROWS, D_MODEL)), cs((ROUTER_ROWS, TOK_BLOCK))],
        out_specs=[slots_spec, meta_spec,
                   pl.BlockSpec((1, ROUTER_ROWS, LANES), lambda i: (i, 0, 0))],
        out_shape=[jax.ShapeDtypeStruct((nblk * N_SLOTS, D_MODEL), BF16),
                   jax.ShapeDtypeStruct((nblk, META_ROWS, TOK_BLOCK), F32),
                   jax.ShapeDtypeStruct((nblk, ROUTER_ROWS, LANES), F32)],
        compiler_params=params,
        name="moe_dispatch",
    )(x1p, x1s, sc, sh, n2g, wr, br)

    plan = _tile_plan(cnt[:, N_EXPERT_GROUPS:N_EXPERT_GROUPS + N_EXPERTS, 0].astype(jnp.int32))
    max_tiles = plan[0].shape[0] - 1

    def w_spec(shape):
        return pl.BlockSpec((1,) + shape, lambda t, te, *_: (te[t], 0, 0))

    ys = pl.pallas_call(
        _expert_kernel,
        grid_spec=pltpu.PrefetchScalarGridSpec(
            num_scalar_prefetch=len(plan),
            grid=(max_tiles,),
            in_specs=[pl.BlockSpec(memory_space=pl.ANY),
                      w_spec((D_MODEL, D_EXPERT)), w_spec((D_MODEL, D_EXPERT)), w_spec((D_EXPERT, D_MODEL))],
            out_specs=pl.BlockSpec(memory_space=pl.ANY),
            scratch_shapes=[pltpu.VMEM((2, ROW_TILE, D_MODEL), BF16),
                            pltpu.VMEM((2, ROW_TILE, D_MODEL), BF16),
                            pltpu.VMEM((D_MODEL, D_EXPERT), BF16),
                            pltpu.VMEM((D_MODEL, D_EXPERT), BF16),
                            pltpu.VMEM((D_EXPERT, D_MODEL), BF16),
                            pltpu.SMEM((2, CHUNKS_PER_TILE), jnp.int32),
                            pltpu.SMEM((2,), jnp.int32),
                            pltpu.SemaphoreType.DMA((2,)), pltpu.SemaphoreType.DMA((2,))]),
        out_shape=jax.ShapeDtypeStruct(xs.shape, BF16),
        input_output_aliases={len(plan): 0},
        compiler_params=params,
        name="moe_experts",
    )(*plan, xs, wg, wu, wd)

    return pl.pallas_call(
        functools.partial(_combine_kernel, nbp),
        grid=(nblk,),
        in_specs=[p_spec, s_spec, mod_spec, meta_spec, slots_spec, cs((1, D_MODEL))],
        out_specs=[pl.BlockSpec((bp, TOK_BLOCK // bp, D_MODEL), lambda i: (0, jnp.minimum(i, nbp - 1), 0)),
                   s_spec],
        out_shape=[jax.ShapeDtypeStruct((bp, x1p.shape[0] // bp, D_MODEL), F32),
                   jax.ShapeDtypeStruct(x1s.shape, F32)],
        scratch_shapes=[pltpu.VMEM((D_MODEL // LANES, TOK_BLOCK, LANES), F32)],
        compiler_params=params,
        name="moe_combine",
    )(x1p, x1s, gt, meta, ys, fg)


def _block_diag_in(bbt):
    eye = jnp.eye(SSM_GROUPS, dtype=bbt.dtype)
    blocks = bbt.reshape(SSM_GROUP, SSM_GROUPS, SSM_STATE)
    w = eye[:, None, :, None] * blocks.transpose(1, 0, 2)[:, :, None, :]
    return w.reshape(SSM_DIM, N_STATE)


def _block_diag_out(c):
    eye = jnp.eye(SSM_GROUPS, dtype=c.dtype)
    w = eye[:, None, :, None] * c.transpose(0, 2, 1)[:, :, None, :]
    return w.reshape(N_STATE, SSM_DIM)


def _to_time_major(x):
    b, s, d = x.shape
    return x.transpose(1, 0, 2).reshape(s * b, d)


def _from_time_major(x, b, s):
    return x.reshape(s, b, -1).transpose(1, 0, 2)


def kernel(x_prompt, x_sample, state_conv, state_ssm_re, state_ssm_im, c_prompt, c_sample, w_ada, b_ada, norm1_g, norm2_g, w_in, conv_w, lambda_re, lambda_im, log_dt, ssm_b_re, ssm_b_im, ssm_c_re, ssm_c_im, ssm_d, w_glu, b_glu, out_norm_conv_g, out_norm_ssm_g, w_out, w_router_group, b_router_group, w_router_expert, b_router_expert, w_expert_gate, w_expert_up, w_expert_down, final_norm_g):
    assert w_ada.shape[0] == 1, "single-layer step"
    bp, sp, _ = x_prompt.shape
    bs, ss, _ = x_sample.shape

    ar, ai, bbr, bbi = _discretise(lambda_re[0], lambda_im[0], log_dt[0], ssm_b_re[0], ssm_b_im[0])
    def halves(w, rows, cols):
        return [w[k * rows:(k + 1) * rows, k * cols:(k + 1) * cols] for k in range(2)]

    w_b = jnp.stack([jnp.concatenate([r, i], axis=1) for r, i in zip(
        halves(_block_diag_in(bbr), HALF_SSM, HALF_STATE),
        halves(_block_diag_in(bbi), HALF_SSM, HALF_STATE))]).astype(BF16)
    w_cr = jnp.stack(halves(_block_diag_out(ssm_c_re[0]), HALF_STATE, HALF_SSM)).astype(BF16)
    w_ci = jnp.stack(halves(_block_diag_out(ssm_c_im[0]), HALF_STATE, HALF_SSM)).astype(BF16)

    mod = _adaln(jnp.concatenate([c_prompt, c_sample], axis=0), w_ada[0], b_ada[0])
    sh1, sc1, gt1, sh2, sc2, gt2 = [mod[:, k * D_MODEL:(k + 1) * D_MODEL] for k in range(N_MOD)]

    mixer_weights = (
        norm1_g[0].reshape(1, -1), w_in[0].astype(BF16), conv_w[0], ar, ai, w_b, w_cr, w_ci,
        ssm_d[0].reshape(1, -1), w_glu[0].astype(BF16), b_glu[0].reshape(1, -1),
        out_norm_conv_g[0].reshape(1, -1), out_norm_ssm_g[0].reshape(1, -1), w_out[0].astype(BF16),
    )

    pad_rows = ROUTER_ROWS - N_EXPERT_GROUPS - N_EXPERTS
    w_router = jnp.concatenate(
        [w_router_group[0].T, w_router_expert[0].transpose(0, 2, 1).reshape(N_EXPERTS, D_MODEL),
         jnp.zeros((pad_rows, D_MODEL), F32)], axis=0)
    b_router = jnp.concatenate(
        [b_router_group[0], b_router_expert[0].reshape(-1), jnp.zeros((pad_rows,), F32)])
    b_router = jnp.broadcast_to(b_router[:, None], (ROUTER_ROWS, TOK_BLOCK))
    moe_weights = (norm2_g[0].reshape(1, -1), w_router, b_router,
                   w_expert_gate[0], w_expert_up[0], w_expert_down[0], final_norm_g.reshape(1, -1))

    def mix(x, lo, hi, h0r, h0i, cb0, chunk, in_kernel_transpose):
        nb, s, _ = x.shape
        x1, cbo, hro, hio = _mixer(
            x if in_kernel_transpose else _to_time_major(x), nb, s, chunk,
            sh1[lo:hi], sc1[lo:hi], gt1[lo:hi], h0r, h0i, cb0, mixer_weights)
        new_conv = cbo.reshape(2, nb, CONV_DIM).transpose(1, 0, 2)[None]
        new_re = hro.reshape(1, nb, SSM_GROUPS, SSM_STATE)
        new_im = hio.reshape(1, nb, SSM_GROUPS, SSM_STATE)
        return x1, new_conv, new_re, new_im

    zero_state = jnp.zeros((bp, N_STATE), F32)
    zero_conv = jnp.zeros((2 * bp, CONV_DIM), F32)
    x1p, conv_p, re_p, im_p = mix(x_prompt, 0, bp, zero_state, zero_state, zero_conv, TOK_BLOCK // bp, True)
    x1s, conv_s, re_s, im_s = mix(
        x_sample, bp, bp + bs,
        state_ssm_re[0].reshape(bs, N_STATE), state_ssm_im[0].reshape(bs, N_STATE),
        state_conv[0].transpose(1, 0, 2).reshape(2 * bs, CONV_DIM), ss, False)

    assert bs * ss == TOK_BLOCK and (bp * sp) % TOK_BLOCK == 0 and TOK_BLOCK % bp == 0

    def block_rows(m):
        return jnp.stack([jnp.tile(m[:bp], (TOK_BLOCK // bp, 1)), jnp.tile(m[bp:], (TOK_BLOCK // bs, 1))])

    y_p, y_s = _moe(x1p, x1s, bp, block_rows(sc2), block_rows(sh2), block_rows(gt2), *moe_weights)
    return (y_p, _from_time_major(y_s, bs, ss), conv_p, re_p, im_p, conv_s, re_s, im_s)
```

```python
import functools

import jax
import jax.numpy as jnp
from jax import lax
from jax.experimental import pallas as pl
from jax.experimental.pallas import tpu as pltpu

D_MODEL = 1024
CONV_DIM = 512
SSM_DIM = 512
SSM_GROUP = 16
SSM_GROUPS = 32
SSM_STATE = 64
N_STATE = SSM_GROUPS * SSM_STATE
HALF_SSM = SSM_DIM // 2
HALF_STATE = N_STATE // 2
IN_DIM = 2048
N_EXPERT_GROUPS = 4
EXPERTS_PER_GROUP = 8
N_EXPERTS = 32
D_EXPERT = 256
N_MOD = 6
EPS = 1e-6

SUBLANES = 8
LANES = 128
SCAN_COLS = 512
STRIDE_PAD = 8
VMEM_LIMIT = 60 * 1024 * 1024

F32 = jnp.float32
BF16 = jnp.bfloat16


def _sigmoid(x):
    return 1.0 / (1.0 + jnp.exp(-x))


def _gelu_tanh(x):
    return 0.5 * x * (1.0 + jnp.tanh(0.7978845608028654 * (x + 0.044715 * (x * x * x))))


def _rms(x, g):
    return x * lax.rsqrt(jnp.mean(x * x, axis=-1, keepdims=True) + EPS) * g


def _bdot(a, b):
    return jnp.dot(a.astype(BF16), b.astype(BF16), preferred_element_type=F32)


def _per_seq(v, nb, scale, shift=None):
    rows, d = v.shape
    v3 = v.reshape(rows // nb, nb, d) * scale[None]
    if shift is not None:
        v3 = v3 + shift[None]
    return v3.reshape(rows, d)


def _disc_kernel(lr_ref, li_ref, ldt_ref, brt_ref, bit_ref, ar_ref, ai_ref, bbr_ref, bbi_ref):
    lr = lr_ref[...]
    li = li_ref[...]
    dt = jnp.exp(ldt_ref[...])
    mag = jnp.exp(lr * dt)
    ar = mag * jnp.cos(li * dt)
    ai = mag * jnp.sin(li * dt)
    den = lr * lr + li * li
    fr = ((ar - 1.0) * lr + ai * li) / den
    fi = (ai * lr - (ar - 1.0) * li) / den
    ar_ref[...] = ar
    ai_ref[...] = ai
    brt = brt_ref[...]
    bit = bit_ref[...]
    bbr_ref[...] = fr * brt - fi * bit
    bbi_ref[...] = fr * bit + fi * brt


def _discretise(lambda_re, lambda_im, log_dt, b_re, b_im):
    lr = lambda_re.reshape(1, N_STATE)
    li = lambda_im.reshape(1, N_STATE)
    ldt = jnp.repeat(log_dt, SSM_STATE).reshape(1, N_STATE)
    brt = b_re.transpose(2, 0, 1).reshape(SSM_GROUP, N_STATE)
    bit = b_im.transpose(2, 0, 1).reshape(SSM_GROUP, N_STATE)
    row = jax.ShapeDtypeStruct((1, N_STATE), F32)
    mat = jax.ShapeDtypeStruct((SSM_GROUP, N_STATE), F32)
    return pl.pallas_call(_disc_kernel, out_shape=(row, row, mat, mat), name="s5_discretise")(
        lr, li, ldt, brt, bit)


def _ada_kernel(c_ref, w_ref, b_ref, o_ref):
    c = c_ref[...]
    o_ref[...] = _bdot(c * _sigmoid(c), w_ref[...]) + b_ref[...]


def _adaln(c_all, w_ada, b_ada):
    n = c_all.shape[0]
    nblk = N_MOD
    return pl.pallas_call(
        _ada_kernel,
        grid=(nblk,),
        in_specs=[
            pl.BlockSpec((n, D_MODEL), lambda j: (0, 0)),
            pl.BlockSpec((D_MODEL, D_MODEL), lambda j: (0, j)),
            pl.BlockSpec((1, D_MODEL), lambda j: (0, j)),
        ],
        out_specs=pl.BlockSpec((n, D_MODEL), lambda j: (0, j)),
        out_shape=jax.ShapeDtypeStruct((n, N_MOD * D_MODEL), F32),
        compiler_params=pltpu.CompilerParams(
            dimension_semantics=("arbitrary",), vmem_limit_bytes=VMEM_LIMIT),
        name="adaln",
    )(c_all, w_ada, b_ada.reshape(1, -1))


def _rows_to_time_major(x_ref, slab, xt_scr):
    nb, steps, d = x_ref.shape
    pitch = slab.shape[1] // nb
    for b in range(nb):
        for j in range(d // LANES):
            slab[j, b * pitch:b * pitch + steps, :] = x_ref[b, :, j * LANES:(j + 1) * LANES]
    for l in range(steps):
        for j in range(d // LANES):
            xt_scr[l * nb:(l + 1) * nb, j * LANES:(j + 1) * LANES] = slab[j, pl.ds(l, nb, stride=pitch), :]


def _mixer_kernel(nb, steps, batch_major,
                  x_ref, sh_ref, sc_ref, gt_ref, h0r_ref, h0i_ref, cb0_ref,
                  n1g_ref, win_ref, cw_ref, ar_ref, ai_ref, wb_ref, wcr_ref, wci_ref,
                  dsk_ref, wglu_ref, bglu_ref, gcv_ref, gsm_ref, wout_ref,
                  x1_ref, cbo_ref, hro_ref, hio_ref,
                  z_scr, cv_scr, s_scr, hr_scr, hi_scr, *tm_scr):
    rows = nb * steps
    i = pl.program_id(0)

    @pl.when(i == 0)
    def _():
        hr_scr[...] = h0r_ref[...]
        hi_scr[...] = h0i_ref[...]
        cv_scr[0:2 * nb, :] = cb0_ref[...]

    if batch_major:
        slab, xt_scr = tm_scr
        _rows_to_time_major(x_ref, slab, xt_scr)
        x = xt_scr[...]
    else:
        x = x_ref[...]
    h = _per_seq(_rms(x, n1g_ref[...]), nb, 1.0 + sc_ref[...], sh_ref[...])
    z_scr[...] = _bdot(h, win_ref[...])

    cv_scr[2 * nb:2 * nb + rows, :] = z_scr[:, CONV_DIM:2 * CONV_DIM] * z_scr[:, 2 * CONV_DIM:3 * CONV_DIM]
    conv = (cv_scr[0:rows, :] * cw_ref[0:1, :]
            + cv_scr[nb:nb + rows, :] * cw_ref[1:2, :]
            + cv_scr[2 * nb:2 * nb + rows, :] * cw_ref[2:3, :])
    y_conv = _rms(z_scr[:, 0:CONV_DIM] * conv, gcv_ref[...])
    carry = cv_scr[rows:rows + 2 * nb, :]
    cv_scr[0:2 * nb, :] = carry
    cbo_ref[...] = carry

    u = z_scr[:, 3 * CONV_DIM:IN_DIM]
    for k in range(2):
        s_scr[:, 2 * k * HALF_STATE:2 * (k + 1) * HALF_STATE] = _bdot(
            u[:, k * HALF_SSM:(k + 1) * HALF_SSM], wb_ref[k])

    def scan_cols(cg):
        cols = slice(cg * SCAN_COLS, (cg + 1) * SCAN_COLS)
        half, off = divmod(cg * SCAN_COLS, HALF_STATE)
        re_cols = slice(2 * half * HALF_STATE + off, 2 * half * HALF_STATE + off + SCAN_COLS)
        im_cols = slice(re_cols.start + HALF_STATE, re_cols.stop + HALF_STATE)
        a_r = jnp.broadcast_to(ar_ref[:, cols], (SUBLANES, SCAN_COLS))
        a_i = jnp.broadcast_to(ai_ref[:, cols], (SUBLANES, SCAN_COLS))

        def advance(hr, hi, rows_l):
            nr = a_r * hr - a_i * hi + s_scr[rows_l, re_cols]
            ni = a_r * hi + a_i * hr + s_scr[rows_l, im_cols]
            s_scr[rows_l, re_cols] = nr
            s_scr[rows_l, im_cols] = ni
            return nr, ni

        if nb == SUBLANES:
            hr, hi = hr_scr[:, cols], hi_scr[:, cols]
            for l in range(steps):
                hr, hi = advance(hr, hi, slice(l * nb, (l + 1) * nb))
            hr_scr[:, cols] = hr
            hi_scr[:, cols] = hi
            return

        def tile_body(bt, _):
            r0 = pl.multiple_of(bt * SUBLANES, SUBLANES)
            tile = pl.ds(r0, SUBLANES)
            hr, hi = lax.fori_loop(
                0, steps,
                lambda l, hc: advance(*hc, pl.ds(pl.multiple_of(l * nb + r0, SUBLANES), SUBLANES)),
                (hr_scr[tile, cols], hi_scr[tile, cols]), unroll=min(steps, 4))
            hr_scr[tile, cols] = hr
            hi_scr[tile, cols] = hi
            return 0

        lax.fori_loop(0, nb // SUBLANES, tile_body, 0)

    y_halves = []
    for k in range(2):
        for cg in range(k * HALF_STATE // SCAN_COLS, (k + 1) * HALF_STATE // SCAN_COLS):
            scan_cols(cg)
        y_halves.append(
            _bdot(s_scr[:, 2 * k * HALF_STATE:(2 * k + 1) * HALF_STATE], wcr_ref[k])
            - _bdot(s_scr[:, (2 * k + 1) * HALF_STATE:(2 * k + 2) * HALF_STATE], wci_ref[k]))

    hro_ref[...] = hr_scr[...]
    hio_ref[...] = hi_scr[...]

    y = jnp.concatenate(y_halves, axis=-1) + dsk_ref[...] * u
    g = _gelu_tanh(y)
    y_ssm = _rms(g * _sigmoid(_bdot(g, wglu_ref[...]) + bglu_ref[...]), gsm_ref[...])

    mixed = _bdot(y_conv, wout_ref[0:CONV_DIM, :]) + _bdot(y_ssm, wout_ref[CONV_DIM:2 * CONV_DIM, :])
    x1_ref[...] = x + _per_seq(mixed, nb, gt_ref[...])


def _const_spec(shape):
    return pl.BlockSpec(shape, lambda i: (0,) * len(shape), pipeline_mode=pl.Buffered(1))


def _mixer(x, nb, steps_total, chunk, sh, sc, gt, h0r, h0i, cb0, weights):
    rows = nb * chunk
    nchunks = steps_total // chunk
    batch_major = x.ndim == 3
    if batch_major:
        assert nb == SUBLANES
        x_spec = pl.BlockSpec((nb, chunk, D_MODEL), lambda i: (0, i, 0))
        tm_scratch = [pltpu.VMEM((D_MODEL // LANES, nb * (chunk + STRIDE_PAD), LANES), F32),
                      pltpu.VMEM((rows, D_MODEL), F32)]
    else:
        x_spec = pl.BlockSpec((rows, D_MODEL), lambda i: (i, 0))
        tm_scratch = []
    in_specs = [
        x_spec,
        _const_spec((nb, D_MODEL)), _const_spec((nb, D_MODEL)), _const_spec((nb, D_MODEL)),
        _const_spec((nb, N_STATE)), _const_spec((nb, N_STATE)), _const_spec((2 * nb, CONV_DIM)),
    ] + [_const_spec(w.shape) for w in weights]
    out_specs = [
        pl.BlockSpec((rows, D_MODEL), lambda i: (i, 0)),
        _const_spec((2 * nb, CONV_DIM)), _const_spec((nb, N_STATE)), _const_spec((nb, N_STATE)),
    ]
    out_shape = [
        jax.ShapeDtypeStruct((steps_total * nb, D_MODEL), F32),
        jax.ShapeDtypeStruct((2 * nb, CONV_DIM), F32),
        jax.ShapeDtypeStruct((nb, N_STATE), F32),
        jax.ShapeDtypeStruct((nb, N_STATE), F32),
    ]
    scratch = [
        pltpu.VMEM((rows, IN_DIM), F32),
        pltpu.VMEM((rows + 2 * nb, CONV_DIM), F32),
        pltpu.VMEM((rows, 2 * N_STATE), F32),
        pltpu.VMEM((nb, N_STATE), F32),
        pltpu.VMEM((nb, N_STATE), F32),
    ] + tm_scratch
    return pl.pallas_call(
        functools.partial(_mixer_kernel, nb, chunk, batch_major),
        grid=(nchunks,),
        in_specs=in_specs, out_specs=out_specs, out_shape=out_shape,
        scratch_shapes=scratch,
        compiler_params=pltpu.CompilerParams(
            dimension_semantics=("arbitrary",), vmem_limit_bytes=VMEM_LIMIT),
        name="mixer",
    )(x, sh, sc, gt, h0r, h0i, cb0, *weights)


TOK_BLOCK = 512
SEG_ALIGN = 16
ROW_TILE = 512
CHUNKS_PER_TILE = ROW_TILE // SEG_ALIGN
N_SLOTS = -(-(2 * TOK_BLOCK + N_EXPERTS * (SEG_ALIGN - 1)) // 256) * 256
SPARE_ROW = N_SLOTS - SEG_ALIGN
assert SPARE_ROW >= 2 * TOK_BLOCK + N_EXPERTS * (SEG_ALIGN - 1)
ROUTER_ROWS = 48
META_ROWS = SUBLANES

_NT = (((1,), (1,)), ((), ()))
_TN = (((0,), (0,)), ((), ()))


def _iota_f32(shape, dim):
    return lax.broadcasted_iota(jnp.int32, shape, dim).astype(F32)


def _split_bf16(v):
    hi = v.astype(BF16)
    return hi, (v - hi.astype(F32)).astype(BF16)


def _route_t(lg):
    row = _iota_f32(lg.shape, 0)
    neg = jnp.float32(-jnp.inf)
    none = float(ROUTER_ROWS)
    is_g = row < N_EXPERT_GROUPS
    gl = jnp.where(is_g, lg, neg)
    gmax = jnp.max(gl, axis=0, keepdims=True)
    gsum = jnp.sum(jnp.where(is_g, jnp.exp(gl - gmax), 0.0), axis=0, keepdims=True)
    g_prob = 1.0 / gsum
    g_idx = jnp.min(jnp.where(gl == gmax, row, none), axis=0, keepdims=True)
    lo = N_EXPERT_GROUPS + EXPERTS_PER_GROUP * g_idx
    el = jnp.where(row >= lo, jnp.where(row < lo + EXPERTS_PER_GROUP, lg, neg), neg)
    m1 = jnp.max(el, axis=0, keepdims=True)
    i1 = jnp.min(jnp.where(el == m1, row, none), axis=0, keepdims=True)
    el2 = jnp.where(row == i1, neg, el)
    m2 = jnp.max(el2, axis=0, keepdims=True)
    i2 = jnp.min(jnp.where(el2 == m2, row, none), axis=0, keepdims=True)
    e21 = jnp.exp(m2 - m1)
    return i1, i2, g_prob / (1.0 + e21), g_prob * e21 / (1.0 + e21)


def _dispatch_kernel(nbp, x1p_ref, x1s_ref, sc_ref, sh_ref, n2g_ref, wr_ref, br_ref,
                     xs_ref, meta_ref, cnt_ref):
    x1 = jnp.where(pl.program_id(0) < nbp, x1p_ref[...], x1s_ref[...])
    h = _rms(x1, n2g_ref[...]) * (1.0 + sc_ref[0]) + sh_ref[0]
    t = h.shape[0]
    hb, hl = _split_bf16(h)
    wh, wl = _split_bf16(wr_ref[...])
    lg = (lax.dot_general(wh, hb, _NT, preferred_element_type=F32)
          + lax.dot_general(wh, hl, _NT, preferred_element_type=F32)
          + lax.dot_general(wl, hb, _NT, preferred_element_type=F32)) + br_ref[...]
    i1, i2, w1, w2 = _route_t(lg)

    row = _iota_f32((ROUTER_ROWS, t), 0)
    hit1 = row == i1
    hit2 = row == i2
    onehot = jnp.where(hit1, 1.0, 0.0) + jnp.where(hit2, 1.0, 0.0)
    before = jnp.where(_iota_f32((t, t), 0) < _iota_f32((t, t), 1), 1.0, 0.0)
    rank = _bdot(onehot, before)
    cnt = jnp.sum(onehot, axis=1, keepdims=True)
    seg = jnp.floor((cnt + (SEG_ALIGN - 1)) * (1.0 / SEG_ALIGN)) * SEG_ALIGN
    earlier = jnp.where(_iota_f32((ROUTER_ROWS, ROUTER_ROWS), 1) < _iota_f32((ROUTER_ROWS, ROUTER_ROWS), 0),
                        1.0, 0.0)
    base = rank + _bdot(earlier, jnp.broadcast_to(seg, (ROUTER_ROWS, t)))
    slot1 = jnp.sum(jnp.where(hit1, base, 0.0), axis=0, keepdims=True)
    slot2 = jnp.sum(jnp.where(hit2, base, 0.0), axis=0, keepdims=True)
    j = _iota_f32((N_SLOTS, t), 0)
    place = jnp.where(j == slot1, 1.0, 0.0) + jnp.where(j == slot2, 1.0, 0.0)
    xs_ref[...] = jnp.dot(place.astype(BF16), hb, preferred_element_type=F32).astype(BF16)
    r = _iota_f32((META_ROWS, t), 0)
    meta_ref[0] = jnp.where(r == 0.0, slot1, jnp.where(r == 1.0, slot2,
                            jnp.where(r == 2.0, w1, jnp.where(r == 3.0, w2, 0.0))))
    cnt_ref[0] = jnp.broadcast_to(cnt, (ROUTER_ROWS, LANES))


def _combine_kernel(nbp, x1p_ref, x1s_ref, gt_ref, meta_ref, ys_ref, fg_ref, yp_ref, ysm_ref, slab):
    i = pl.program_id(0)
    x1 = jnp.where(i < nbp, x1p_ref[...], x1s_ref[...])
    meta = meta_ref[0]
    slot1, slot2, w1, w2 = (meta[k:k + 1, :] for k in range(4))
    j = _iota_f32((N_SLOTS, x1.shape[0]), 0)
    pick = jnp.where(j == slot1, w1, 0.0) + jnp.where(j == slot2, w2, 0.0)
    moe = lax.dot_general(pick.astype(BF16), ys_ref[...], _TN, preferred_element_type=F32)
    y = _rms(x1 + gt_ref[0] * moe, fg_ref[...])

    @pl.when(i < nbp)
    def _():
        nb, steps, d = yp_ref.shape
        for j in range(d // LANES):
            slab[j] = y[:, j * LANES:(j + 1) * LANES]
        for b in range(nb):
            for j in range(d // LANES):
                yp_ref[b, :, j * LANES:(j + 1) * LANES] = slab[j, pl.ds(b, steps, stride=nb), :]

    @pl.when(i >= nbp)
    def _():
        ysm_ref[...] = y


def _expert_kernel(te_ref, tnv_ref, segm_ref, segsrc_ref, nt_ref, xs_hbm, wg_ref, wu_ref, wd_ref,
                   ys_hbm, xbuf, ybuf, wgb, wub, wdb, rows, cur, in_sem, out_sem):
    t = pl.program_id(0)
    nt = nt_ref[0]
    slot = t % 2

    def chunk_rows(j):
        return pl.ds(pl.multiple_of(j * SEG_ALIGN, SEG_ALIGN), SEG_ALIGN)

    def hbm_rows(ref, r):
        return ref.at[pl.ds(pl.multiple_of(r, SEG_ALIGN), SEG_ALIGN), :]

    def in_copy(r, j, s):
        return pltpu.make_async_copy(hbm_rows(xs_hbm, r), xbuf.at[s, chunk_rows(j), :], in_sem.at[s])

    def out_copy(r, j, s):
        return pltpu.make_async_copy(ybuf.at[s, chunk_rows(j), :], hbm_rows(ys_hbm, r), out_sem.at[s])

    def plan_rows(tile, s):
        nv = tnv_ref[tile]

        def take(c):
            j, seg, k = c
            m = segm_ref[seg]
            n = jnp.minimum(jnp.maximum(m - k, 0), nv - j)
            base = segsrc_ref[seg] + SEG_ALIGN * k

            def put(i, _):
                rows[s, j + i] = base + SEG_ALIGN * i
                return 0

            lax.fori_loop(0, n, put, 0)
            done = k + n >= m
            return j + n, jnp.where(done, seg + 1, seg), jnp.where(done, 0, k + n)

        _, seg, k = lax.while_loop(lambda c: c[0] < nv, take, (0, cur[0], cur[1]))
        cur[0] = seg
        cur[1] = k

        def spare(i, _):
            rows[s, i] = SPARE_ROW
            return 0

        lax.fori_loop(nv, CHUNKS_PER_TILE, spare, 0)

    def start_gather(s):
        for j in range(CHUNKS_PER_TILE):
            in_copy(rows[s, j], j, s).start()

    def wait_gather(s):
        pltpu.make_async_copy(xs_hbm.at[pl.ds(0, ROW_TILE), :], xbuf.at[s], in_sem.at[s]).wait()

    def wait_scatter(tile, s):
        nv = tnv_ref[tile]

        @pl.when(nv == CHUNKS_PER_TILE)
        def _():
            pltpu.make_async_copy(ybuf.at[s], ys_hbm.at[pl.ds(0, ROW_TILE), :], out_sem.at[s]).wait()

        @pl.when(nv < CHUNKS_PER_TILE)
        def _():
            def body(j, _):
                out_copy(0, j, s).wait()
                return 0

            lax.fori_loop(0, nv, body, 0)

    def run_tile(full):
        start_gather(1 - slot)
        wait_gather(slot)
        x = xbuf[slot]
        a = jnp.dot(x, wgb[...], preferred_element_type=F32)
        b = jnp.dot(x, wub[...], preferred_element_type=F32)
        o = jnp.dot((a * _sigmoid(a) * b).astype(BF16), wdb[...], preferred_element_type=F32)
        ybuf[slot] = o.astype(BF16)
        if full:
            for j in range(CHUNKS_PER_TILE):
                out_copy(rows[slot, j], j, slot).start()
        else:
            def body(j, _):
                out_copy(rows[slot, j], j, slot).start()
                return 0

            lax.fori_loop(0, tnv_ref[t], body, 0)

    @pl.when(t == 0)
    def _():
        cur[0] = 0
        cur[1] = 0
        plan_rows(0, 0)
        start_gather(0)

    @pl.when(jnp.logical_or(t == 0, te_ref[t] != te_ref[jnp.maximum(t - 1, 0)]))
    def _():
        wgb[...] = wg_ref[0].astype(BF16)
        wub[...] = wu_ref[0].astype(BF16)
        wdb[...] = wd_ref[0].astype(BF16)

    @pl.when(t < nt)
    def _():
        plan_rows(t + 1, 1 - slot)

        @pl.when(t >= 2)
        def _():
            wait_scatter(t - 2, slot)

        pl.when(tnv_ref[t] == CHUNKS_PER_TILE)(lambda: run_tile(True))
        pl.when(tnv_ref[t] < CHUNKS_PER_TILE)(lambda: run_tile(False))

        @pl.when(t == nt - 1)
        def _():
            wait_gather(1 - slot)
            wait_scatter(t, slot)

            @pl.when(t >= 1)
            def _():
                wait_scatter(t - 1, 1 - slot)


def _tile_plan(cnt):
    nblk = cnt.shape[0]
    max_chunks = nblk * (2 * TOK_BLOCK // SEG_ALIGN + N_EXPERTS)
    max_tiles = max_chunks // CHUNKS_PER_TILE + N_EXPERTS
    m = (cnt + SEG_ALIGN - 1) // SEG_ALIGN
    seg_start = jnp.cumsum(m, axis=1) - m
    src0 = jnp.arange(nblk, dtype=jnp.int32)[:, None] * N_SLOTS + SEG_ALIGN * seg_start
    seg_m = jnp.concatenate([m.T.reshape(-1), jnp.full((1,), max_chunks, jnp.int32)])
    seg_src = jnp.concatenate([src0.T.reshape(-1), jnp.zeros((1,), jnp.int32)])
    ce = jnp.sum(m, axis=0)
    tiles = (ce + CHUNKS_PER_TILE - 1) // CHUNKS_PER_TILE
    t_cum = jnp.cumsum(tiles)
    nt = t_cum[-1]
    k = jnp.arange(max_tiles + 1, dtype=jnp.int32)
    kk = jnp.minimum(k, nt - 1)
    ek = jnp.minimum(jnp.sum((t_cum[None, :] <= kk[:, None]).astype(jnp.int32), axis=1), N_EXPERTS - 1)
    is_e = ek[:, None] == jnp.arange(N_EXPERTS, dtype=jnp.int32)[None, :]
    local = kk - jnp.sum(jnp.where(is_e, (t_cum - tiles)[None, :], 0), axis=1)
    left = jnp.sum(jnp.where(is_e, ce[None, :], 0), axis=1) - CHUNKS_PER_TILE * local
    tile_nv = jnp.where(k < nt, jnp.clip(left, 0, CHUNKS_PER_TILE), 0)
    return tuple(v.astype(jnp.int32) for v in (ek, tile_nv, seg_m, seg_src, nt.reshape(1)))


def _moe(x1p, x1s, bp, sc, sh, gt, n2g, wr, br, wg, wu, wd, fg):
    nbp = x1p.shape[0] // TOK_BLOCK
    nblk = nbp + 1
    blk = (TOK_BLOCK, D_MODEL)
    p_spec = pl.BlockSpec(blk, lambda i: (jnp.minimum(i, nbp - 1), 0))
    s_spec = pl.BlockSpec(blk, lambda i: (0, 0))
    mod_spec = pl.BlockSpec((1,) + blk, lambda i: (i // nbp, 0, 0))
    meta_spec = pl.BlockSpec((1, META_ROWS, TOK_BLOCK), lambda i: (i, 0, 0))
    slots_spec = pl.BlockSpec((N_SLOTS, D_MODEL), lambda i: (i, 0))
    params = pltpu.CompilerParams(dimension_semantics=("arbitrary",), vmem_limit_bytes=VMEM_LIMIT)

    def cs(shape):
        return pl.BlockSpec(shape, lambda i: (0,) * len(shape))

    xs, meta, cnt = pl.pallas_call(
        functools.partial(_dispatch_kernel, nbp),
        grid=(nblk,),
        in_specs=[p_spec, s_spec, mod_spec, mod_spec, cs((1, D_MODEL)),
                  cs((ROUTER_ROWS, D_MODEL)), cs((ROUTER_ROWS, TOK_BLOCK))],
        out_specs=[slots_spec, meta_spec,
                   pl.BlockSpec((1, ROUTER_ROWS, LANES), lambda i: (i, 0, 0))],
        out_shape=[jax.ShapeDtypeStruct((nblk * N_SLOTS, D_MODEL), BF16),
                   jax.ShapeDtypeStruct((nblk, META_ROWS, TOK_BLOCK), F32),
                   jax.ShapeDtypeStruct((nblk, ROUTER_ROWS, LANES), F32)],
        compiler_params=params,
        name="moe_dispatch",
    )(x1p, x1s, sc, sh, n2g, wr, br)

    plan = _tile_plan(cnt[:, N_EXPERT_GROUPS:N_EXPERT_GROUPS + N_EXPERTS, 0].astype(jnp.int32))
    max_tiles = plan[0].shape[0] - 1

    def w_spec(shape):
        return pl.BlockSpec((1,) + shape, lambda t, te, *_: (te[t], 0, 0))

    ys = pl.pallas_call(
        _expert_kernel,
        grid_spec=pltpu.PrefetchScalarGridSpec(
            num_scalar_prefetch=len(plan),
            grid=(max_tiles,),
            in_specs=[pl.BlockSpec(memory_space=pl.ANY),
                      w_spec((D_MODEL, D_EXPERT)), w_spec((D_MODEL, D_EXPERT)), w_spec((D_EXPERT, D_MODEL))],
            out_specs=pl.BlockSpec(memory_space=pl.ANY),
            scratch_shapes=[pltpu.VMEM((2, ROW_TILE, D_MODEL), BF16),
                            pltpu.VMEM((2, ROW_TILE, D_MODEL), BF16),
                            pltpu.VMEM((D_MODEL, D_EXPERT), BF16),
                            pltpu.VMEM((D_MODEL, D_EXPERT), BF16),
                            pltpu.VMEM((D_EXPERT, D_MODEL), BF16),
                            pltpu.SMEM((2, CHUNKS_PER_TILE), jnp.int32),
                            pltpu.SMEM((2,), jnp.int32),
                            pltpu.SemaphoreType.DMA((2,)), pltpu.SemaphoreType.DMA((2,))]),
        out_shape=jax.ShapeDtypeStruct(xs.shape, BF16),
        input_output_aliases={len(plan): 0},
        compiler_params=params,
        name="moe_experts",
    )(*plan, xs, wg, wu, wd)

    return pl.pallas_call(
        functools.partial(_combine_kernel, nbp),
        grid=(nblk,),
        in_specs=[p_spec, s_spec, mod_spec, meta_spec, slots_spec, cs((1, D_MODEL))],
        out_specs=[pl.BlockSpec((bp, TOK_BLOCK // bp, D_MODEL), lambda i: (0, jnp.minimum(i, nbp - 1), 0)),
                   s_spec],
        out_shape=[jax.ShapeDtypeStruct((bp, x1p.shape[0] // bp, D_MODEL), F32),
                   jax.ShapeDtypeStruct(x1s.shape, F32)],
        scratch_shapes=[pltpu.VMEM((D_MODEL // LANES, TOK_BLOCK, LANES), F32)],
        compiler_params=params,
        name="moe_combine",
    )(x1p, x1s, gt, meta, ys, fg)


def _block_diag_in(bbt):
    eye = jnp.eye(SSM_GROUPS, dtype=bbt.dtype)
    blocks = bbt.reshape(SSM_GROUP, SSM_GROUPS, SSM_STATE)
    w = eye[:, None, :, None] * blocks.transpose(1, 0, 2)[:, :, None, :]
    return w.reshape(SSM_DIM, N_STATE)


def _block_diag_out(c):
    eye = jnp.eye(SSM_GROUPS, dtype=c.dtype)
    w = eye[:, None, :, None] * c.transpose(0, 2, 1)[:, :, None, :]
    return w.reshape(N_STATE, SSM_DIM)


def _to_time_major(x):
    b, s, d = x.shape
    return x.transpose(1, 0, 2).reshape(s * b, d)


def _from_time_major(x, b, s):
    return x.reshape(s, b, -1).transpose(1, 0, 2)


def kernel(x_prompt, x_sample, state_conv, state_ssm_re, state_ssm_im, c_prompt, c_sample, w_ada, b_ada, norm1_g, norm2_g, w_in, conv_w, lambda_re, lambda_im, log_dt, ssm_b_re, ssm_b_im, ssm_c_re, ssm_c_im, ssm_d, w_glu, b_glu, out_norm_conv_g, out_norm_ssm_g, w_out, w_router_group, b_router_group, w_router_expert, b_router_expert, w_expert_gate, w_expert_up, w_expert_down, final_norm_g):
    assert w_ada.shape[0] == 1, "single-layer step"
    bp, sp, _ = x_prompt.shape
    bs, ss, _ = x_sample.shape

    ar, ai, bbr, bbi = _discretise(lambda_re[0], lambda_im[0], log_dt[0], ssm_b_re[0], ssm_b_im[0])
    def halves(w, rows, cols):
        return [w[k * rows:(k + 1) * rows, k * cols:(k + 1) * cols] for k in range(2)]

    w_b = jnp.stack([jnp.concatenate([r, i], axis=1) for r, i in zip(
        halves(_block_diag_in(bbr), HALF_SSM, HALF_STATE),
        halves(_block_diag_in(bbi), HALF_SSM, HALF_STATE))]).astype(BF16)
    w_cr = jnp.stack(halves(_block_diag_out(ssm_c_re[0]), HALF_STATE, HALF_SSM)).astype(BF16)
    w_ci = jnp.stack(halves(_block_diag_out(ssm_c_im[0]), HALF_STATE, HALF_SSM)).astype(BF16)

    mod = _adaln(jnp.concatenate([c_prompt, c_sample], axis=0), w_ada[0], b_ada[0])
    sh1, sc1, gt1, sh2, sc2, gt2 = [mod[:, k * D_MODEL:(k + 1) * D_MODEL] for k in range(N_MOD)]

    mixer_weights = (
        norm1_g[0].reshape(1, -1), w_in[0].astype(BF16), conv_w[0], ar, ai, w_b, w_cr, w_ci,
        ssm_d[0].reshape(1, -1), w_glu[0].astype(BF16), b_glu[0].reshape(1, -1),
        out_norm_conv_g[0].reshape(1, -1), out_norm_ssm_g[0].reshape(1, -1), w_out[0].astype(BF16),
    )

    pad_rows = ROUTER_ROWS - N_EXPERT_GROUPS - N_EXPERTS
    w_router = jnp.concatenate(
        [w_router_group[0].T, w_router_expert[0].transpose(0, 2, 1).reshape(N_EXPERTS, D_MODEL),
         jnp.zeros((pad_rows, D_MODEL), F32)], axis=0)
    b_router = jnp.concatenate(
        [b_router_group[0], b_router_expert[0].reshape(-1), jnp.zeros((pad_rows,), F32)])
    b_router = jnp.broadcast_to(b_router[:, None], (ROUTER_ROWS, TOK_BLOCK))
    moe_weights = (norm2_g[0].reshape(1, -1), w_router, b_router,
                   w_expert_gate[0], w_expert_up[0], w_expert_down[0], final_norm_g.reshape(1, -1))

    def mix(x, lo, hi, h0r, h0i, cb0, chunk, in_kernel_transpose):
        nb, s, _ = x.shape
        x1, cbo, hro, hio = _mixer(
            x if in_kernel_transpose else _to_time_major(x), nb, s, chunk,
            sh1[lo:hi], sc1[lo:hi], gt1[lo:hi], h0r, h0i, cb0, mixer_weights)
        new_conv = cbo.reshape(2, nb, CONV_DIM).transpose(1, 0, 2)[None]
        new_re = hro.reshape(1, nb, SSM_GROUPS, SSM_STATE)
        new_im = hio.reshape(1, nb, SSM_GROUPS, SSM_STATE)
        return x1, new_conv, new_re, new_im

    zero_state = jnp.zeros((bp, N_STATE), F32)
    zero_conv = jnp.zeros((2 * bp, CONV_DIM), F32)
    x1p, conv_p, re_p, im_p = mix(x_prompt, 0, bp, zero_state, zero_state, zero_conv, TOK_BLOCK // bp, True)
    x1s, conv_s, re_s, im_s = mix(
        x_sample, bp, bp + bs,
        state_ssm_re[0].reshape(bs, N_STATE), state_ssm_im[0].reshape(bs, N_STATE),
        state_conv[0].transpose(1, 0, 2).reshape(2 * bs, CONV_DIM), ss, False)

    assert bs * ss == TOK_BLOCK and (bp * sp) % TOK_BLOCK == 0 and TOK_BLOCK % bp == 0

    def block_rows(m):
        return jnp.stack([jnp.tile(m[:bp], (TOK_BLOCK // bp, 1)), jnp.tile(m[bp:], (TOK_BLOCK // bs, 1))])

    y_p, y_s = _moe(x1p, x1s, bp, block_rows(sc2), block_rows(sh2), block_rows(gt2), *moe_weights)
    return (y_p, _from_time_major(y_s, bs, ss), conv_p, re_p, im_p, conv_s, re_s, im_s)
```

```python
import functools

import jax
import jax.numpy as jnp
from jax import lax
from jax.experimental import pallas as pl
from jax.experimental.pallas import tpu as pltpu

D_MODEL = 1024
CONV_DIM = 512
SSM_DIM = 512
SSM_GROUP = 16
SSM_GROUPS = 32
SSM_STATE = 64
N_STATE = SSM_GROUPS * SSM_STATE
HALF_SSM = SSM_DIM // 2
HALF_STATE = N_STATE // 2
IN_DIM = 2048
N_EXPERT_GROUPS = 4
EXPERTS_PER_GROUP = 8
N_EXPERTS = 32
D_EXPERT = 256
N_MOD = 6
EPS = 1e-6

SUBLANES = 8
LANES = 128
SCAN_COLS = 512
STRIDE_PAD = 8
VMEM_LIMIT = 60 * 1024 * 1024

F32 = jnp.float32
BF16 = jnp.bfloat16


def _sigmoid(x):
    return 1.0 / (1.0 + jnp.exp(-x))


def _gelu_tanh(x):
    return 0.5 * x * (1.0 + jnp.tanh(0.7978845608028654 * (x + 0.044715 * (x * x * x))))


def _rms(x, g):
    return x * lax.rsqrt(jnp.mean(x * x, axis=-1, keepdims=True) + EPS) * g


def _bdot(a, b):
    return jnp.dot(a.astype(BF16), b.astype(BF16), preferred_element_type=F32)


def _per_seq(v, nb, scale, shift=None):
    rows, d = v.shape
    v3 = v.reshape(rows // nb, nb, d) * scale[None]
    if shift is not None:
        v3 = v3 + shift[None]
    return v3.reshape(rows, d)


def _disc_kernel(lr_ref, li_ref, ldt_ref, brt_ref, bit_ref, ar_ref, ai_ref, bbr_ref, bbi_ref):
    lr = lr_ref[...]
    li = li_ref[...]
    dt = jnp.exp(ldt_ref[...])
    mag = jnp.exp(lr * dt)
    ar = mag * jnp.cos(li * dt)
    ai = mag * jnp.sin(li * dt)
    den = lr * lr + li * li
    fr = ((ar - 1.0) * lr + ai * li) / den
    fi = (ai * lr - (ar - 1.0) * li) / den
    ar_ref[...] = ar
    ai_ref[...] = ai
    brt = brt_ref[...]
    bit = bit_ref[...]
    bbr_ref[...] = fr * brt - fi * bit
    bbi_ref[...] = fr * bit + fi * brt


def _discretise(lambda_re, lambda_im, log_dt, b_re, b_im):
    lr = lambda_re.reshape(1, N_STATE)
    li = lambda_im.reshape(1, N_STATE)
    ldt = jnp.repeat(log_dt, SSM_STATE).reshape(1, N_STATE)
    brt = b_re.transpose(2, 0, 1).reshape(SSM_GROUP, N_STATE)
    bit = b_im.transpose(2, 0, 1).reshape(SSM_GROUP, N_STATE)
    row = jax.ShapeDtypeStruct((1, N_STATE), F32)
    mat = jax.ShapeDtypeStruct((SSM_GROUP, N_STATE), F32)
    return pl.pallas_call(_disc_kernel, out_shape=(row, row, mat, mat), name="s5_discretise")(
        lr, li, ldt, brt, bit)


def _ada_kernel(c_ref, w_ref, b_ref, o_ref):
    c = c_ref[...]
    o_ref[...] = _bdot(c * _sigmoid(c), w_ref[...]) + b_ref[...]


def _adaln(c_all, w_ada, b_ada):
    n = c_all.shape[0]
    nblk = N_MOD
    return pl.pallas_call(
        _ada_kernel,
        grid=(nblk,),
        in_specs=[
            pl.BlockSpec((n, D_MODEL), lambda j: (0, 0)),
            pl.BlockSpec((D_MODEL, D_MODEL), lambda j: (0, j)),
            pl.BlockSpec((1, D_MODEL), lambda j: (0, j)),
        ],
        out_specs=pl.BlockSpec((n, D_MODEL), lambda j: (0, j)),
        out_shape=jax.ShapeDtypeStruct((n, N_MOD * D_MODEL), F32),
        compiler_params=pltpu.CompilerParams(
            dimension_semantics=("arbitrary",), vmem_limit_bytes=VMEM_LIMIT),
        name="adaln",
    )(c_all, w_ada, b_ada.reshape(1, -1))


def _rows_to_time_major(x_ref, slab, xt_scr):
    nb, steps, d = x_ref.shape
    pitch = slab.shape[1] // nb
    for b in range(nb):
        for j in range(d // LANES):
            slab[j, b * pitch:b * pitch + steps, :] = x_ref[b, :, j * LANES:(j + 1) * LANES]
    for l in range(steps):
        for j in range(d // LANES):
            xt_scr[l * nb:(l + 1) * nb, j * LANES:(j + 1) * LANES] = slab[j, pl.ds(l, nb, stride=pitch), :]


def _mixer_kernel(nb, steps, batch_major,
                  x_ref, sh_ref, sc_ref, gt_ref, h0r_ref, h0i_ref, cb0_ref,
                  n1g_ref, win_ref, cw_ref, ar_ref, ai_ref, wb_ref, wcr_ref, wci_ref,
                  dsk_ref, wglu_ref, bglu_ref, gcv_ref, gsm_ref, wout_ref,
                  x1_ref, cbo_ref, hro_ref, hio_ref,
                  z_scr, cv_scr, s_scr, hr_scr, hi_scr, *tm_scr):
    rows = nb * steps
    i = pl.program_id(0)

    @pl.when(i == 0)
    def _():
        hr_scr[...] = h0r_ref[...]
        hi_scr[...] = h0i_ref[...]
        cv_scr[0:2 * nb, :] = cb0_ref[...]

    if batch_major:
        slab, xt_scr = tm_scr
        _rows_to_time_major(x_ref, slab, xt_scr)
        x = xt_scr[...]
    else:
        x = x_ref[...]
    h = _per_seq(_rms(x, n1g_ref[...]), nb, 1.0 + sc_ref[...], sh_ref[...])
    z_scr[...] = _bdot(h, win_ref[...])

    cv_scr[2 * nb:2 * nb + rows, :] = z_scr[:, CONV_DIM:2 * CONV_DIM] * z_scr[:, 2 * CONV_DIM:3 * CONV_DIM]
    conv = (cv_scr[0:rows, :] * cw_ref[0:1, :]
            + cv_scr[nb:nb + rows, :] * cw_ref[1:2, :]
            + cv_scr[2 * nb:2 * nb + rows, :] * cw_ref[2:3, :])
    y_conv = _rms(z_scr[:, 0:CONV_DIM] * conv, gcv_ref[...])
    carry = cv_scr[rows:rows + 2 * nb, :]
    cv_scr[0:2 * nb, :] = carry
    cbo_ref[...] = carry

    u = z_scr[:, 3 * CONV_DIM:IN_DIM]
    for k in range(2):
        s_scr[:, 2 * k * HALF_STATE:2 * (k + 1) * HALF_STATE] = _bdot(
            u[:, k * HALF_SSM:(k + 1) * HALF_SSM], wb_ref[k])

    def scan_cols(cg):
        cols = slice(cg * SCAN_COLS, (cg + 1) * SCAN_COLS)
        half, off = divmod(cg * SCAN_COLS, HALF_STATE)
        re_cols = slice(2 * half * HALF_STATE + off, 2 * half * HALF_STATE + off + SCAN_COLS)
        im_cols = slice(re_cols.start + HALF_STATE, re_cols.stop + HALF_STATE)
        a_r = jnp.broadcast_to(ar_ref[:, cols], (SUBLANES, SCAN_COLS))
        a_i = jnp.broadcast_to(ai_ref[:, cols], (SUBLANES, SCAN_COLS))

        def advance(hr, hi, rows_l):
            nr = a_r * hr - a_i * hi + s_scr[rows_l, re_cols]
            ni = a_r * hi + a_i * hr + s_scr[rows_l, im_cols]
            s_scr[rows_l, re_cols] = nr
            s_scr[rows_l, im_cols] = ni
            return nr, ni

        if nb == SUBLANES:
            hr, hi = hr_scr[:, cols], hi_scr[:, cols]
            for l in range(steps):
                hr, hi = advance(hr, hi, slice(l * nb, (l + 1) * nb))
            hr_scr[:, cols] = hr
            hi_scr[:, cols] = hi
            return

        def tile_body(bt, _):
            r0 = pl.multiple_of(bt * SUBLANES, SUBLANES)
            tile = pl.ds(r0, SUBLANES)
            hr, hi = lax.fori_loop(
                0, steps,
                lambda l, hc: advance(*hc, pl.ds(pl.multiple_of(l * nb + r0, SUBLANES), SUBLANES)),
                (hr_scr[tile, cols], hi_scr[tile, cols]), unroll=min(steps, 4))
            hr_scr[tile, cols] = hr
            hi_scr[tile, cols] = hi
            return 0

        lax.fori_loop(0, nb // SUBLANES, tile_body, 0)

    y_halves = []
    for k in range(2):
        for cg in range(k * HALF_STATE // SCAN_COLS, (k + 1) * HALF_STATE // SCAN_COLS):
            scan_cols(cg)
        y_halves.append(
            _bdot(s_scr[:, 2 * k * HALF_STATE:(2 * k + 1) * HALF_STATE], wcr_ref[k])
            - _bdot(s_scr[:, (2 * k + 1) * HALF_STATE:(2 * k + 2) * HALF_STATE], wci_ref[k]))

    hro_ref[...] = hr_scr[...]
    hio_ref[...] = hi_scr[...]

    y = jnp.concatenate(y_halves, axis=-1) + dsk_ref[...] * u
    g = _gelu_tanh(y)
    y_ssm = _rms(g * _sigmoid(_bdot(g, wglu_ref[...]) + bglu_ref[...]), gsm_ref[...])

    mixed = _bdot(y_conv, wout_ref[0:CONV_DIM, :]) + _bdot(y_ssm, wout_ref[CONV_DIM:2 * CONV_DIM, :])
    x1_ref[...] = x + _per_seq(mixed, nb, gt_ref[...])


def _const_spec(shape):
    return pl.BlockSpec(shape, lambda i: (0,) * len(shape), pipeline_mode=pl.Buffered(1))


def _mixer(x, nb, steps_total, chunk, sh, sc, gt, h0r, h0i, cb0, weights):
    rows = nb * chunk
    nchunks = steps_total // chunk
    batch_major = x.ndim == 3
    if batch_major:
        assert nb == SUBLANES
        x_spec = pl.BlockSpec((nb, chunk, D_MODEL), lambda i: (0, i, 0))
        tm_scratch = [pltpu.VMEM((D_MODEL // LANES, nb * (chunk + STRIDE_PAD), LANES), F32),
                      pltpu.VMEM((rows, D_MODEL), F32)]
    else:
        x_spec = pl.BlockSpec((rows, D_MODEL), lambda i: (i, 0))
        tm_scratch = []
    in_specs = [
        x_spec,
        _const_spec((nb, D_MODEL)), _const_spec((nb, D_MODEL)), _const_spec((nb, D_MODEL)),
        _const_spec((nb, N_STATE)), _const_spec((nb, N_STATE)), _const_spec((2 * nb, CONV_DIM)),
    ] + [_const_spec(w.shape) for w in weights]
    out_specs = [
        pl.BlockSpec((rows, D_MODEL), lambda i: (i, 0)),
        _const_spec((2 * nb, CONV_DIM)), _const_spec((nb, N_STATE)), _const_spec((nb, N_STATE)),
    ]
    out_shape = [
        jax.ShapeDtypeStruct((steps_total * nb, D_MODEL), F32),
        jax.ShapeDtypeStruct((2 * nb, CONV_DIM), F32),
        jax.ShapeDtypeStruct((nb, N_STATE), F32),
        jax.ShapeDtypeStruct((nb, N_STATE), F32),
    ]
    scratch = [
        pltpu.VMEM((rows, IN_DIM), F32),
        pltpu.VMEM((rows + 2 * nb, CONV_DIM), F32),
        pltpu.VMEM((rows, 2 * N_STATE), F32),
        pltpu.VMEM((nb, N_STATE), F32),
        pltpu.VMEM((nb, N_STATE), F32),
    ] + tm_scratch
    return pl.pallas_call(
        functools.partial(_mixer_kernel, nb, chunk, batch_major),
        grid=(nchunks,),
        in_specs=in_specs, out_specs=out_specs, out_shape=out_shape,
        scratch_shapes=scratch,
        compiler_params=pltpu.CompilerParams(
            dimension_semantics=("arbitrary",), vmem_limit_bytes=VMEM_LIMIT),
        name="mixer",
    )(x, sh, sc, gt, h0r, h0i, cb0, *weights)


TOK_BLOCK = 512
SEG_ALIGN = 16
ROW_TILE = 512
CHUNKS_PER_TILE = ROW_TILE // SEG_ALIGN
TABLE_SEG_ROWS = 32
N_SLOTS = -(-(2 * TOK_BLOCK + N_EXPERTS * (SEG_ALIGN - 1)) // 256) * 256
SPARE_ROW = N_SLOTS - SEG_ALIGN
assert SPARE_ROW >= 2 * TOK_BLOCK + N_EXPERTS * (SEG_ALIGN - 1)
ROUTER_ROWS = 48
META_ROWS = SUBLANES

_NT = (((1,), (1,)), ((), ()))
_TN = (((0,), (0,)), ((), ()))


def _iota_f32(shape, dim):
    return lax.broadcasted_iota(jnp.int32, shape, dim).astype(F32)


def _split_bf16(v):
    hi = v.astype(BF16)
    return hi, (v - hi.astype(F32)).astype(BF16)


def _route_t(lg):
    row = _iota_f32(lg.shape, 0)
    neg = jnp.float32(-jnp.inf)
    none = float(ROUTER_ROWS)
    is_g = row < N_EXPERT_GROUPS
    gl = jnp.where(is_g, lg, neg)
    gmax = jnp.max(gl, axis=0, keepdims=True)
    gsum = jnp.sum(jnp.where(is_g, jnp.exp(gl - gmax), 0.0), axis=0, keepdims=True)
    g_prob = 1.0 / gsum
    g_idx = jnp.min(jnp.where(gl == gmax, row, none), axis=0, keepdims=True)
    lo = N_EXPERT_GROUPS + EXPERTS_PER_GROUP * g_idx
    el = jnp.where(row >= lo, jnp.where(row < lo + EXPERTS_PER_GROUP, lg, neg), neg)
    m1 = jnp.max(el, axis=0, keepdims=True)
    i1 = jnp.min(jnp.where(el == m1, row, none), axis=0, keepdims=True)
    el2 = jnp.where(row == i1, neg, el)
    m2 = jnp.max(el2, axis=0, keepdims=True)
    i2 = jnp.min(jnp.where(el2 == m2, row, none), axis=0, keepdims=True)
    e21 = jnp.exp(m2 - m1)
    return i1, i2, g_prob / (1.0 + e21), g_prob * e21 / (1.0 + e21)


def _dispatch_kernel(nbp, x1p_ref, x1s_ref, sc_ref, sh_ref, n2g_ref, wr_ref, br_ref,
                     xs_ref, meta_ref, cnt_ref):
    x1 = jnp.where(pl.program_id(0) < nbp, x1p_ref[...], x1s_ref[...])
    h = _rms(x1, n2g_ref[...]) * (1.0 + sc_ref[0]) + sh_ref[0]
    t = h.shape[0]
    hb, hl = _split_bf16(h)
    wh, wl = _split_bf16(wr_ref[...])
    lg = (lax.dot_general(wh, hb, _NT, preferred_element_type=F32)
          + lax.dot_general(wh, hl, _NT, preferred_element_type=F32)
          + lax.dot_general(wl, hb, _NT, preferred_element_type=F32)) + br_ref[...]
    i1, i2, w1, w2 = _route_t(lg)

    row = _iota_f32((ROUTER_ROWS, t), 0)
    hit1 = row == i1
    hit2 = row == i2
    onehot = jnp.where(hit1, 1.0, 0.0) + jnp.where(hit2, 1.0, 0.0)
    before = jnp.where(_iota_f32((t, t), 0) < _iota_f32((t, t), 1), 1.0, 0.0)
    rank = _bdot(onehot, before)
    cnt = jnp.sum(onehot, axis=1, keepdims=True)
    seg = jnp.floor((cnt + (SEG_ALIGN - 1)) * (1.0 / SEG_ALIGN)) * SEG_ALIGN
    earlier = jnp.where(_iota_f32((ROUTER_ROWS, ROUTER_ROWS), 1) < _iota_f32((ROUTER_ROWS, ROUTER_ROWS), 0),
                        1.0, 0.0)
    base = rank + _bdot(earlier, jnp.broadcast_to(seg, (ROUTER_ROWS, t)))
    slot1 = jnp.sum(jnp.where(hit1, base, 0.0), axis=0, keepdims=True)
    slot2 = jnp.sum(jnp.where(hit2, base, 0.0), axis=0, keepdims=True)
    j = _iota_f32((N_SLOTS, t), 0)
    place = jnp.where(j == slot1, 1.0, 0.0) + jnp.where(j == slot2, 1.0, 0.0)
    xs_ref[...] = jnp.dot(place.astype(BF16), hb, preferred_element_type=F32).astype(BF16)
    r = _iota_f32((META_ROWS, t), 0)
    meta_ref[0] = jnp.where(r == 0.0, slot1, jnp.where(r == 1.0, slot2,
                            jnp.where(r == 2.0, w1, jnp.where(r == 3.0, w2, 0.0))))
    cnt_ref[0] = jnp.broadcast_to(cnt, (ROUTER_ROWS, LANES))


def _combine_kernel(nbp, x1p_ref, x1s_ref, gt_ref, meta_ref, ys_ref, fg_ref, yp_ref, ysm_ref, slab):
    i = pl.program_id(0)
    x1 = jnp.where(i < nbp, x1p_ref[...], x1s_ref[...])
    meta = meta_ref[0]
    slot1, slot2, w1, w2 = (meta[k:k + 1, :] for k in range(4))
    j = _iota_f32((N_SLOTS, x1.shape[0]), 0)
    pick = jnp.where(j == slot1, w1, 0.0) + jnp.where(j == slot2, w2, 0.0)
    moe = lax.dot_general(pick.astype(BF16), ys_ref[...], _TN, preferred_element_type=F32)
    y = _rms(x1 + gt_ref[0] * moe, fg_ref[...])

    @pl.when(i < nbp)
    def _():
        nb, steps, d = yp_ref.shape
        for j in range(d // LANES):
            slab[j] = y[:, j * LANES:(j + 1) * LANES]
        for b in range(nb):
            for j in range(d // LANES):
                yp_ref[b, :, j * LANES:(j + 1) * LANES] = slab[j, pl.ds(b, steps, stride=nb), :]

    @pl.when(i >= nbp)
    def _():
        ysm_ref[...] = y


def _chunk_table_kernel(start_ref, m_ref, src_ref, out_ref):
    nseg = start_ref.shape[0]
    i = _iota_f32((1, out_ref.shape[1]), 1)

    def body(b, acc):
        rows = pl.ds(pl.multiple_of(b * TABLE_SEG_ROWS, TABLE_SEG_ROWS), TABLE_SEG_ROWS)
        start = start_ref[rows, :]
        val = src_ref[rows, :] + SEG_ALIGN * (i - start)
        return acc + jnp.where(i >= start, jnp.where(i < start + m_ref[rows, :], val, 0.0), 0.0)

    acc = lax.fori_loop(0, nseg // TABLE_SEG_ROWS, body,
                        jnp.zeros((TABLE_SEG_ROWS, out_ref.shape[1]), F32))
    out_ref[...] = jnp.sum(acc, axis=0, keepdims=True).astype(jnp.int32)


def _expert_kernel(te_ref, tnv_ref, tc0_ref, ctab_ref, nt_ref, xs_hbm, wg_ref, wu_ref, wd_ref,
                   ys_hbm, xbuf, ybuf, wgb, wub, wdb, in_sem, out_sem):
    t = pl.program_id(0)
    nt = nt_ref[0]
    slot = t % 2

    def chunk_rows(j):
        return pl.ds(pl.multiple_of(j * SEG_ALIGN, SEG_ALIGN), SEG_ALIGN)

    def hbm_rows(ref, r):
        return ref.at[pl.ds(pl.multiple_of(r, SEG_ALIGN), SEG_ALIGN), :]

    def in_copy(r, j, s):
        return pltpu.make_async_copy(hbm_rows(xs_hbm, r), xbuf.at[s, chunk_rows(j), :], in_sem.at[s])

    def out_copy(r, j, s):
        return pltpu.make_async_copy(ybuf.at[s, chunk_rows(j), :], hbm_rows(ys_hbm, r), out_sem.at[s])

    def start_gather(tile, s):
        c0 = tc0_ref[tile]
        nv = tnv_ref[tile]
        for j in range(CHUNKS_PER_TILE):
            in_copy(jnp.where(j < nv, ctab_ref[c0 + j], SPARE_ROW), j, s).start()

    def wait_gather(s):
        pltpu.make_async_copy(xs_hbm.at[pl.ds(0, ROW_TILE), :], xbuf.at[s], in_sem.at[s]).wait()

    def wait_scatter(tile, s):
        nv = tnv_ref[tile]

        @pl.when(nv == CHUNKS_PER_TILE)
        def _():
            pltpu.make_async_copy(ybuf.at[s], ys_hbm.at[pl.ds(0, ROW_TILE), :], out_sem.at[s]).wait()

        @pl.when(nv < CHUNKS_PER_TILE)
        def _():
            def body(j, _):
                out_copy(0, j, s).wait()
                return 0

            lax.fori_loop(0, nv, body, 0)

    def run_tile(full):
        start_gather(t + 1, 1 - slot)
        wait_gather(slot)
        x = xbuf[slot]
        a = jnp.dot(x, wgb[...], preferred_element_type=F32)
        b = jnp.dot(x, wub[...], preferred_element_type=F32)
        o = jnp.dot((a * _sigmoid(a) * b).astype(BF16), wdb[...], preferred_element_type=F32)
        ybuf[slot] = o.astype(BF16)
        c0 = tc0_ref[t]
        if full:
            for j in range(CHUNKS_PER_TILE):
                out_copy(ctab_ref[c0 + j], j, slot).start()
        else:
            def body(j, _):
                out_copy(ctab_ref[c0 + j], j, slot).start()
                return 0

            lax.fori_loop(0, tnv_ref[t], body, 0)

    @pl.when(t == 0)
    def _():
        start_gather(0, 0)

    @pl.when(jnp.logical_or(t == 0, te_ref[t] != te_ref[jnp.maximum(t - 1, 0)]))
    def _():
        wgb[...] = wg_ref[0].astype(BF16)
        wub[...] = wu_ref[0].astype(BF16)
        wdb[...] = wd_ref[0].astype(BF16)

    @pl.when(t < nt)
    def _():
        @pl.when(t >= 2)
        def _():
            wait_scatter(t - 2, slot)

        pl.when(tnv_ref[t] == CHUNKS_PER_TILE)(lambda: run_tile(True))
        pl.when(tnv_ref[t] < CHUNKS_PER_TILE)(lambda: run_tile(False))

        @pl.when(t == nt - 1)
        def _():
            wait_gather(1 - slot)
            wait_scatter(t, slot)

            @pl.when(t >= 1)
            def _():
                wait_scatter(t - 1, 1 - slot)


def _tile_plan(cnt):
    nblk = cnt.shape[0]
    max_chunks = nblk * (2 * TOK_BLOCK // SEG_ALIGN + N_EXPERTS)
    max_tiles = max_chunks // CHUNKS_PER_TILE + N_EXPERTS
    table_len = -(-(max_chunks + CHUNKS_PER_TILE) // LANES) * LANES
    m = (cnt + SEG_ALIGN - 1) // SEG_ALIGN
    seg_start = jnp.cumsum(m, axis=1) - m
    src0 = jnp.arange(nblk, dtype=jnp.int32)[:, None] * N_SLOTS + SEG_ALIGN * seg_start
    seg_m = m.T.reshape(-1, 1)
    seg_first = jnp.cumsum(seg_m, axis=0) - seg_m
    assert seg_m.shape[0] % TABLE_SEG_ROWS == 0
    col = lambda v: v.astype(F32)
    chunk_src = pl.pallas_call(
        _chunk_table_kernel,
        out_shape=jax.ShapeDtypeStruct((1, table_len), jnp.int32),
        name="moe_chunk_table",
    )(col(seg_first), col(seg_m), col(src0.T.reshape(-1, 1))).reshape(-1)
    ce = jnp.sum(m, axis=0)
    c_start = jnp.cumsum(ce) - ce
    tiles = (ce + CHUNKS_PER_TILE - 1) // CHUNKS_PER_TILE
    t_cum = jnp.cumsum(tiles)
    nt = t_cum[-1]
    k = jnp.arange(max_tiles + 1, dtype=jnp.int32)
    kk = jnp.minimum(k, nt - 1)
    ek = jnp.minimum(jnp.sum((t_cum[None, :] <= kk[:, None]).astype(jnp.int32), axis=1), N_EXPERTS - 1)
    is_e = ek[:, None] == jnp.arange(N_EXPERTS, dtype=jnp.int32)[None, :]
    pick = lambda v: jnp.sum(jnp.where(is_e, v[None, :], 0), axis=1)
    local = kk - pick(t_cum - tiles)
    tile_nv = jnp.where(k < nt, jnp.clip(pick(ce) - CHUNKS_PER_TILE * local, 0, CHUNKS_PER_TILE), 0)
    tile_c0 = jnp.where(k < nt, pick(c_start) + CHUNKS_PER_TILE * local, 0)
    return tuple(v.astype(jnp.int32) for v in (ek, tile_nv, tile_c0, chunk_src, nt.reshape(1)))


def _moe(x1p, x1s, bp, sc, sh, gt, n2g, wr, br, wg, wu, wd, fg):
    nbp = x1p.shape[0] // TOK_BLOCK
    nblk = nbp + 1
    blk = (TOK_BLOCK, D_MODEL)
    p_spec = pl.BlockSpec(blk, lambda i: (jnp.minimum(i, nbp - 1), 0))
    s_spec = pl.BlockSpec(blk, lambda i: (0, 0))
    mod_spec = pl.BlockSpec((1,) + blk, lambda i: (jnp.minimum(i // nbp, 1), 0, 0))
    meta_spec = pl.BlockSpec((1, META_ROWS, TOK_BLOCK), lambda i: (i, 0, 0))
    slots_spec = pl.BlockSpec((N_SLOTS, D_MODEL), lambda i: (i, 0))
    params = pltpu.CompilerParams(dimension_semantics=("arbitrary",), vmem_limit_bytes=VMEM_LIMIT)

    def cs(shape):
        return pl.BlockSpec(shape, lambda i: (0,) * len(shape))

    xs, meta, cnt = pl.pallas_call(
        functools.partial(_dispatch_kernel, nbp),
        grid=(nblk,),
        in_specs=[p_spec, s_spec, mod_spec, mod_spec, cs((1, D_MODEL)),
                  cs((ROUTER_ROWS, D_MODEL)), cs((ROUTER_ROWS, TOK_BLOCK))],
        out_specs=[slots_spec, meta_spec,
                   pl.BlockSpec((1, ROUTER_ROWS, LANES), lambda i: (i, 0, 0))],
        out_shape=[jax.ShapeDtypeStruct((nblk * N_SLOTS, D_MODEL), BF16),
                   jax.ShapeDtypeStruct((nblk, META_ROWS, TOK_BLOCK), F32),
                   jax.ShapeDtypeStruct((nblk, ROUTER_ROWS, LANES), F32)],
        compiler_params=params,
        name="moe_dispatch",
    )(x1p, x1s, sc, sh, n2g, wr, br)

    plan = _tile_plan(cnt[:, N_EXPERT_GROUPS:N_EXPERT_GROUPS + N_EXPERTS, 0].astype(jnp.int32))
    max_tiles = plan[0].shape[0] - 1

    def w_spec(shape):
        return pl.BlockSpec((1,) + shape, lambda t, te, *_: (te[t], 0, 0))

    ys = pl.pallas_call(
        _expert_kernel,
        grid_spec=pltpu.PrefetchScalarGridSpec(
            num_scalar_prefetch=len(plan),
            grid=(max_tiles,),
            in_specs=[pl.BlockSpec(memory_space=pl.ANY),
                      w_spec((D_MODEL, D_EXPERT)), w_spec((D_MODEL, D_EXPERT)), w_spec((D_EXPERT, D_MODEL))],
            out_specs=pl.BlockSpec(memory_space=pl.ANY),
            scratch_shapes=[pltpu.VMEM((2, ROW_TILE, D_MODEL), BF16),
                            pltpu.VMEM((2, ROW_TILE, D_MODEL), BF16),
                            pltpu.VMEM((D_MODEL, D_EXPERT), BF16),
                            pltpu.VMEM((D_MODEL, D_EXPERT), BF16),
                            pltpu.VMEM((D_EXPERT, D_MODEL), BF16),
                            pltpu.SemaphoreType.DMA((2,)), pltpu.SemaphoreType.DMA((2,))]),
        out_shape=jax.ShapeDtypeStruct(xs.shape, BF16),
        input_output_aliases={len(plan): 0},
        compiler_params=params,
        name="moe_experts",
    )(*plan, xs, wg, wu, wd)

    return pl.pallas_call(
        functools.partial(_combine_kernel, nbp),
        grid=(nblk,),
        in_specs=[p_spec, s_spec, mod_spec, meta_spec, slots_spec, cs((1, D_MODEL))],
        out_specs=[pl.BlockSpec((bp, TOK_BLOCK // bp, D_MODEL), lambda i: (0, jnp.minimum(i, nbp - 1), 0)),
                   s_spec],
        out_shape=[jax.ShapeDtypeStruct((bp, x1p.shape[0] // bp, D_MODEL), F32),
                   jax.ShapeDtypeStruct(x1s.shape, F32)],
        scratch_shapes=[pltpu.VMEM((D_MODEL // LANES, TOK_BLOCK, LANES), F32)],
        compiler_params=params,
        name="moe_combine",
    )(x1p, x1s, gt, meta, ys, fg)


def _diag_halves_in(bbt):
    eye = jnp.eye(SSM_GROUPS // 2, dtype=bbt.dtype)
    blocks = bbt.reshape(SSM_GROUP, 2, SSM_GROUPS // 2, SSM_STATE).transpose(1, 0, 2, 3)
    w = eye[None, :, None, :, None] * blocks[:, None, :, :, :]
    return w.reshape(2, HALF_SSM, HALF_STATE)


def _diag_halves_out(c):
    eye = jnp.eye(SSM_GROUPS // 2, dtype=c.dtype)
    blocks = c.reshape(2, SSM_GROUPS // 2, SSM_GROUP, SSM_STATE).transpose(0, 1, 3, 2)
    w = eye[None, :, None, :, None] * blocks[:, :, :, None, :]
    return w.reshape(2, HALF_STATE, HALF_SSM)


def _to_time_major(x):
    b, s, d = x.shape
    return x.transpose(1, 0, 2).reshape(s * b, d)


def _from_time_major(x, b, s):
    return x.reshape(s, b, -1).transpose(1, 0, 2)


def kernel(x_prompt, x_sample, state_conv, state_ssm_re, state_ssm_im, c_prompt, c_sample, w_ada, b_ada, norm1_g, norm2_g, w_in, conv_w, lambda_re, lambda_im, log_dt, ssm_b_re, ssm_b_im, ssm_c_re, ssm_c_im, ssm_d, w_glu, b_glu, out_norm_conv_g, out_norm_ssm_g, w_out, w_router_group, b_router_group, w_router_expert, b_router_expert, w_expert_gate, w_expert_up, w_expert_down, final_norm_g):
    assert w_ada.shape[0] == 1, "single-layer step"
    bp, sp, _ = x_prompt.shape
    bs, ss, _ = x_sample.shape

    ar, ai, bbr, bbi = _discretise(lambda_re[0], lambda_im[0], log_dt[0], ssm_b_re[0], ssm_b_im[0])
    w_b = jnp.concatenate([_diag_halves_in(bbr), _diag_halves_in(bbi)], axis=2).astype(BF16)
    w_cr = _diag_halves_out(ssm_c_re[0]).astype(BF16)
    w_ci = _diag_halves_out(ssm_c_im[0]).astype(BF16)

    mod = _adaln(jnp.concatenate([c_prompt, c_sample], axis=0), w_ada[0], b_ada[0])
    sh1, sc1, gt1, sh2, sc2, gt2 = [mod[:, k * D_MODEL:(k + 1) * D_MODEL] for k in range(N_MOD)]

    mixer_weights = (
        norm1_g[0].reshape(1, -1), w_in[0].astype(BF16), conv_w[0], ar, ai, w_b, w_cr, w_ci,
        ssm_d[0].reshape(1, -1), w_glu[0].astype(BF16), b_glu[0].reshape(1, -1),
        out_norm_conv_g[0].reshape(1, -1), out_norm_ssm_g[0].reshape(1, -1), w_out[0].astype(BF16),
    )

    pad_rows = ROUTER_ROWS - N_EXPERT_GROUPS - N_EXPERTS
    w_router = jnp.concatenate(
        [w_router_group[0].T, w_router_expert[0].transpose(0, 2, 1).reshape(N_EXPERTS, D_MODEL),
         jnp.zeros((pad_rows, D_MODEL), F32)], axis=0)
    b_router = jnp.concatenate(
        [b_router_group[0], b_router_expert[0].reshape(-1), jnp.zeros((pad_rows,), F32)])
    b_router = jnp.broadcast_to(b_router[:, None], (ROUTER_ROWS, TOK_BLOCK))
    moe_weights = (norm2_g[0].reshape(1, -1), w_router, b_router,
                   w_expert_gate[0], w_expert_up[0], w_expert_down[0], final_norm_g.reshape(1, -1))

    def mix(x, lo, hi, h0r, h0i, cb0, chunk, in_kernel_transpose):
        nb, s, _ = x.shape
        x1, cbo, hro, hio = _mixer(
            x if in_kernel_transpose else _to_time_major(x), nb, s, chunk,
            sh1[lo:hi], sc1[lo:hi], gt1[lo:hi], h0r, h0i, cb0, mixer_weights)
        new_conv = cbo.reshape(2, nb, CONV_DIM).transpose(1, 0, 2)[None]
        new_re = hro.reshape(1, nb, SSM_GROUPS, SSM_STATE)
        new_im = hio.reshape(1, nb, SSM_GROUPS, SSM_STATE)
        return x1, new_conv, new_re, new_im

    zero_state = jnp.zeros((bp, N_STATE), F32)
    zero_conv = jnp.zeros((2 * bp, CONV_DIM), F32)
    x1p, conv_p, re_p, im_p = mix(x_prompt, 0, bp, zero_state, zero_state, zero_conv, TOK_BLOCK // bp, True)
    x1s, conv_s, re_s, im_s = mix(
        x_sample, bp, bp + bs,
        state_ssm_re[0].reshape(bs, N_STATE), state_ssm_im[0].reshape(bs, N_STATE),
        state_conv[0].transpose(1, 0, 2).reshape(2 * bs, CONV_DIM), ss, False)

    assert bs * ss == TOK_BLOCK and (bp * sp) % TOK_BLOCK == 0 and TOK_BLOCK % bp == 0

    def block_rows(m):
        return jnp.stack([jnp.tile(m[:bp], (TOK_BLOCK // bp, 1)), jnp.tile(m[bp:], (TOK_BLOCK // bs, 1))])

    y_p, y_s = _moe(x1p, x1s, bp, block_rows(sc2), block_rows(sh2), block_rows(gt2), *moe_weights)
    return (y_p, _from_time_major(y_s, bs, ss), conv_p, re_p, im_p, conv_s, re_s, im_s)
```

```python
import functools

import jax
import jax.numpy as jnp
from jax import lax
from jax.experimental import pallas as pl
from jax.experimental.pallas import tpu as pltpu

D_MODEL = 1024
CONV_DIM = 512
SSM_DIM = 512
SSM_GROUP = 16
SSM_GROUPS = 32
SSM_STATE = 64
N_STATE = SSM_GROUPS * SSM_STATE
HALF_SSM = SSM_DIM // 2
HALF_STATE = N_STATE // 2
IN_DIM = 2048
N_EXPERT_GROUPS = 4
EXPERTS_PER_GROUP = 8
N_EXPERTS = 32
D_EXPERT = 256
N_MOD = 6
EPS = 1e-6

SUBLANES = 8
LANES = 128
SCAN_COLS = 512
STRIDE_PAD = 8
VMEM_LIMIT = 60 * 1024 * 1024

F32 = jnp.float32
BF16 = jnp.bfloat16


def _sigmoid(x):
    return 1.0 / (1.0 + jnp.exp(-x))


def _gelu_tanh(x):
    return 0.5 * x * (1.0 + jnp.tanh(0.7978845608028654 * (x + 0.044715 * (x * x * x))))


def _rms(x, g):
    return x * lax.rsqrt(jnp.mean(x * x, axis=-1, keepdims=True) + EPS) * g


def _bdot(a, b):
    return jnp.dot(a.astype(BF16), b.astype(BF16), preferred_element_type=F32)


def _per_seq(v, nb, scale, shift=None):
    rows, d = v.shape
    v3 = v.reshape(rows // nb, nb, d) * scale[None]
    if shift is not None:
        v3 = v3 + shift[None]
    return v3.reshape(rows, d)


def _disc_kernel(lr_ref, li_ref, ldt_ref, brt_ref, bit_ref, ar_ref, ai_ref, bbr_ref, bbi_ref):
    lr = lr_ref[...]
    li = li_ref[...]
    dt = jnp.exp(ldt_ref[...])
    mag = jnp.exp(lr * dt)
    ar = mag * jnp.cos(li * dt)
    ai = mag * jnp.sin(li * dt)
    den = lr * lr + li * li
    fr = ((ar - 1.0) * lr + ai * li) / den
    fi = (ai * lr - (ar - 1.0) * li) / den
    ar_ref[...] = ar
    ai_ref[...] = ai
    brt = brt_ref[...]
    bit = bit_ref[...]
    bbr_ref[...] = fr * brt - fi * bit
    bbi_ref[...] = fr * bit + fi * brt


def _discretise(lambda_re, lambda_im, log_dt, b_re, b_im):
    lr = lambda_re.reshape(1, N_STATE)
    li = lambda_im.reshape(1, N_STATE)
    ldt = jnp.repeat(log_dt, SSM_STATE).reshape(1, N_STATE)
    brt = b_re.transpose(2, 0, 1).reshape(SSM_GROUP, N_STATE)
    bit = b_im.transpose(2, 0, 1).reshape(SSM_GROUP, N_STATE)
    row = jax.ShapeDtypeStruct((1, N_STATE), F32)
    mat = jax.ShapeDtypeStruct((SSM_GROUP, N_STATE), F32)
    return pl.pallas_call(_disc_kernel, out_shape=(row, row, mat, mat), name="s5_discretise")(
        lr, li, ldt, brt, bit)


def _ada_kernel(c_ref, w_ref, b_ref, o_ref):
    c = c_ref[...]
    o_ref[...] = _bdot(c * _sigmoid(c), w_ref[...]) + b_ref[...]


def _adaln(c_all, w_ada, b_ada):
    n = c_all.shape[0]
    nblk = N_MOD
    return pl.pallas_call(
        _ada_kernel,
        grid=(nblk,),
        in_specs=[
            pl.BlockSpec((n, D_MODEL), lambda j: (0, 0)),
            pl.BlockSpec((D_MODEL, D_MODEL), lambda j: (0, j)),
            pl.BlockSpec((1, D_MODEL), lambda j: (0, j)),
        ],
        out_specs=pl.BlockSpec((n, D_MODEL), lambda j: (0, j)),
        out_shape=jax.ShapeDtypeStruct((n, N_MOD * D_MODEL), F32),
        compiler_params=pltpu.CompilerParams(
            dimension_semantics=("arbitrary",), vmem_limit_bytes=VMEM_LIMIT),
        name="adaln",
    )(c_all, w_ada, b_ada.reshape(1, -1))


def _rows_to_time_major(x_ref, slab, xt_scr):
    nb, steps, d = x_ref.shape
    pitch = slab.shape[1] // nb
    for b in range(nb):
        for j in range(d // LANES):
            slab[j, b * pitch:b * pitch + steps, :] = x_ref[b, :, j * LANES:(j + 1) * LANES]
    for l in range(steps):
        for j in range(d // LANES):
            xt_scr[l * nb:(l + 1) * nb, j * LANES:(j + 1) * LANES] = slab[j, pl.ds(l, nb, stride=pitch), :]


def _mixer_kernel(nb, steps, batch_major,
                  x_ref, sh_ref, sc_ref, gt_ref, h0r_ref, h0i_ref, cb0_ref,
                  n1g_ref, win_ref, cw_ref, ar_ref, ai_ref, wb_ref, wcr_ref, wci_ref,
                  dsk_ref, wglu_ref, bglu_ref, gcv_ref, gsm_ref, wout_ref,
                  x1_ref, cbo_ref, hro_ref, hio_ref,
                  z_scr, cv_scr, s_scr, hr_scr, hi_scr, *tm_scr):
    rows = nb * steps
    i = pl.program_id(0)

    @pl.when(i == 0)
    def _():
        hr_scr[...] = h0r_ref[...]
        hi_scr[...] = h0i_ref[...]
        cv_scr[0:2 * nb, :] = cb0_ref[...]

    if batch_major:
        slab, x_tm = tm_scr
        _rows_to_time_major(x_ref, slab, x_tm)
    else:
        x_tm = x_ref
    h = _per_seq(_rms(x_tm[...], n1g_ref[...]), nb, 1.0 + sc_ref[...], sh_ref[...])
    z_scr[...] = _bdot(h, win_ref[...])

    u_cols = 3 * CONV_DIM
    for k in range(2):
        s_scr[:, 2 * k * HALF_STATE:2 * (k + 1) * HALF_STATE] = _bdot(
            z_scr[:, u_cols + k * HALF_SSM:u_cols + (k + 1) * HALF_SSM], wb_ref[k])

    def scan_cols(cg):
        cols = slice(cg * SCAN_COLS, (cg + 1) * SCAN_COLS)
        half, off = divmod(cg * SCAN_COLS, HALF_STATE)
        re_cols = slice(2 * half * HALF_STATE + off, 2 * half * HALF_STATE + off + SCAN_COLS)
        im_cols = slice(re_cols.start + HALF_STATE, re_cols.stop + HALF_STATE)
        a_r = jnp.broadcast_to(ar_ref[:, cols], (SUBLANES, SCAN_COLS))
        a_i = jnp.broadcast_to(ai_ref[:, cols], (SUBLANES, SCAN_COLS))

        def advance(hr, hi, rows_l):
            nr = a_r * hr - a_i * hi + s_scr[rows_l, re_cols]
            ni = a_r * hi + a_i * hr + s_scr[rows_l, im_cols]
            s_scr[rows_l, re_cols] = nr
            s_scr[rows_l, im_cols] = ni
            return nr, ni

        if nb == SUBLANES:
            hr, hi = hr_scr[:, cols], hi_scr[:, cols]
            for l in range(steps):
                hr, hi = advance(hr, hi, slice(l * nb, (l + 1) * nb))
            hr_scr[:, cols] = hr
            hi_scr[:, cols] = hi
            return

        def tile_body(bt, _):
            r0 = pl.multiple_of(bt * SUBLANES, SUBLANES)
            tile = pl.ds(r0, SUBLANES)
            hr, hi = lax.fori_loop(
                0, steps,
                lambda l, hc: advance(*hc, pl.ds(pl.multiple_of(l * nb + r0, SUBLANES), SUBLANES)),
                (hr_scr[tile, cols], hi_scr[tile, cols]), unroll=min(steps, 4))
            hr_scr[tile, cols] = hr
            hi_scr[tile, cols] = hi
            return 0

        lax.fori_loop(0, nb // SUBLANES, tile_body, 0)

    y_halves = []
    for k in range(2):
        for cg in range(k * HALF_STATE // SCAN_COLS, (k + 1) * HALF_STATE // SCAN_COLS):
            scan_cols(cg)
        y_halves.append(
            _bdot(s_scr[:, 2 * k * HALF_STATE:(2 * k + 1) * HALF_STATE], wcr_ref[k])
            - _bdot(s_scr[:, (2 * k + 1) * HALF_STATE:(2 * k + 2) * HALF_STATE], wci_ref[k]))

    hro_ref[...] = hr_scr[...]
    hio_ref[...] = hi_scr[...]

    y = jnp.concatenate(y_halves, axis=-1) + dsk_ref[...] * z_scr[:, u_cols:IN_DIM]
    g = _gelu_tanh(y)
    y_ssm = _rms(g * _sigmoid(_bdot(g, wglu_ref[...]) + bglu_ref[...]), gsm_ref[...])

    cv_scr[2 * nb:2 * nb + rows, :] = z_scr[:, CONV_DIM:2 * CONV_DIM] * z_scr[:, 2 * CONV_DIM:3 * CONV_DIM]
    conv = (cv_scr[0:rows, :] * cw_ref[0:1, :]
            + cv_scr[nb:nb + rows, :] * cw_ref[1:2, :]
            + cv_scr[2 * nb:2 * nb + rows, :] * cw_ref[2:3, :])
    y_conv = _rms(z_scr[:, 0:CONV_DIM] * conv, gcv_ref[...])
    carry = cv_scr[rows:rows + 2 * nb, :]
    cv_scr[0:2 * nb, :] = carry
    cbo_ref[...] = carry

    mixed = _bdot(y_conv, wout_ref[0:CONV_DIM, :]) + _bdot(y_ssm, wout_ref[CONV_DIM:2 * CONV_DIM, :])
    x1_ref[...] = x_tm[...] + _per_seq(mixed, nb, gt_ref[...])


def _const_spec(shape):
    return pl.BlockSpec(shape, lambda i: (0,) * len(shape), pipeline_mode=pl.Buffered(1))


def _mixer(x, nb, steps_total, chunk, sh, sc, gt, h0r, h0i, cb0, weights):
    rows = nb * chunk
    nchunks = steps_total // chunk
    batch_major = x.ndim == 3
    if batch_major:
        assert nb == SUBLANES
        x_spec = pl.BlockSpec((nb, chunk, D_MODEL), lambda i: (0, i, 0))
        tm_scratch = [pltpu.VMEM((D_MODEL // LANES, nb * (chunk + STRIDE_PAD), LANES), F32),
                      pltpu.VMEM((rows, D_MODEL), F32)]
    else:
        x_spec = pl.BlockSpec((rows, D_MODEL), lambda i: (i, 0))
        tm_scratch = []
    in_specs = [
        x_spec,
        _const_spec((nb, D_MODEL)), _const_spec((nb, D_MODEL)), _const_spec((nb, D_MODEL)),
        _const_spec((nb, N_STATE)), _const_spec((nb, N_STATE)), _const_spec((2 * nb, CONV_DIM)),
    ] + [_const_spec(w.shape) for w in weights]
    out_specs = [
        pl.BlockSpec((rows, D_MODEL), lambda i: (i, 0)),
        _const_spec((2 * nb, CONV_DIM)), _const_spec((nb, N_STATE)), _const_spec((nb, N_STATE)),
    ]
    out_shape = [
        jax.ShapeDtypeStruct((steps_total * nb, D_MODEL), F32),
        jax.ShapeDtypeStruct((2 * nb, CONV_DIM), F32),
        jax.ShapeDtypeStruct((nb, N_STATE), F32),
        jax.ShapeDtypeStruct((nb, N_STATE), F32),
    ]
    scratch = [
        pltpu.VMEM((rows, IN_DIM), F32),
        pltpu.VMEM((rows + 2 * nb, CONV_DIM), F32),
        pltpu.VMEM((rows, 2 * N_STATE), F32),
        pltpu.VMEM((nb, N_STATE), F32),
        pltpu.VMEM((nb, N_STATE), F32),
    ] + tm_scratch
    return pl.pallas_call(
        functools.partial(_mixer_kernel, nb, chunk, batch_major),
        grid=(nchunks,),
        in_specs=in_specs, out_specs=out_specs, out_shape=out_shape,
        scratch_shapes=scratch,
        compiler_params=pltpu.CompilerParams(
            dimension_semantics=("arbitrary",), vmem_limit_bytes=VMEM_LIMIT),
        name="mixer",
    )(x, sh, sc, gt, h0r, h0i, cb0, *weights)


TOK_BLOCK = 512
SEG_ALIGN = 16
ROW_TILE = 512
CHUNKS_PER_TILE = ROW_TILE // SEG_ALIGN
SCATTER_GROUPS = 8
TABLE_SEG_ROWS = 32
N_SLOTS = -(-(2 * TOK_BLOCK + N_EXPERTS * (SEG_ALIGN - 1)) // 256) * 256
SPARE_ROW = N_SLOTS - SEG_ALIGN
assert SPARE_ROW >= 2 * TOK_BLOCK + N_EXPERTS * (SEG_ALIGN - 1)
ROUTER_ROWS = 48
META_ROWS = SUBLANES

_NT = (((1,), (1,)), ((), ()))
_TN = (((0,), (0,)), ((), ()))


def _iota_f32(shape, dim):
    return lax.broadcasted_iota(jnp.int32, shape, dim).astype(F32)


def _split_bf16(v):
    hi = v.astype(BF16)
    return hi, (v - hi.astype(F32)).astype(BF16)


def _route_t(lg):
    row = _iota_f32(lg.shape, 0)
    neg = jnp.float32(-jnp.inf)
    none = float(ROUTER_ROWS)
    is_g = row < N_EXPERT_GROUPS
    gl = jnp.where(is_g, lg, neg)
    gmax = jnp.max(gl, axis=0, keepdims=True)
    gsum = jnp.sum(jnp.where(is_g, jnp.exp(gl - gmax), 0.0), axis=0, keepdims=True)
    g_prob = 1.0 / gsum
    g_idx = jnp.min(jnp.where(gl == gmax, row, none), axis=0, keepdims=True)
    lo = N_EXPERT_GROUPS + EXPERTS_PER_GROUP * g_idx
    el = jnp.where(row >= lo, jnp.where(row < lo + EXPERTS_PER_GROUP, lg, neg), neg)
    m1 = jnp.max(el, axis=0, keepdims=True)
    i1 = jnp.min(jnp.where(el == m1, row, none), axis=0, keepdims=True)
    el2 = jnp.where(row == i1, neg, el)
    m2 = jnp.max(el2, axis=0, keepdims=True)
    i2 = jnp.min(jnp.where(el2 == m2, row, none), axis=0, keepdims=True)
    e21 = jnp.exp(m2 - m1)
    return i1, i2, g_prob / (1.0 + e21), g_prob * e21 / (1.0 + e21)


def _dispatch_kernel(nbp, x1p_ref, x1s_ref, sc_ref, sh_ref, n2g_ref, wr_ref, br_ref,
                     xs_ref, meta_ref, cnt_ref):
    x1 = jnp.where(pl.program_id(0) < nbp, x1p_ref[...], x1s_ref[...])
    h = _rms(x1, n2g_ref[...]) * (1.0 + sc_ref[0]) + sh_ref[0]
    t = h.shape[0]
    hb, hl = _split_bf16(h)
    wh, wl = _split_bf16(wr_ref[...])
    lg = (lax.dot_general(wh, hb, _NT, preferred_element_type=F32)
          + lax.dot_general(wh, hl, _NT, preferred_element_type=F32)
          + lax.dot_general(wl, hb, _NT, preferred_element_type=F32)) + br_ref[...]
    i1, i2, w1, w2 = _route_t(lg)

    row = _iota_f32((ROUTER_ROWS, t), 0)
    hit1 = row == i1
    hit2 = row == i2
    onehot = jnp.where(hit1, 1.0, 0.0) + jnp.where(hit2, 1.0, 0.0)
    before = jnp.where(_iota_f32((t, t), 0) < _iota_f32((t, t), 1), 1.0, 0.0)
    rank = _bdot(onehot, before)
    cnt = jnp.sum(onehot, axis=1, keepdims=True)
    seg = jnp.floor((cnt + (SEG_ALIGN - 1)) * (1.0 / SEG_ALIGN)) * SEG_ALIGN
    earlier = jnp.where(_iota_f32((ROUTER_ROWS, ROUTER_ROWS), 1) < _iota_f32((ROUTER_ROWS, ROUTER_ROWS), 0),
                        1.0, 0.0)
    base = rank + _bdot(earlier, jnp.broadcast_to(seg, (ROUTER_ROWS, t)))
    slot1 = jnp.sum(jnp.where(hit1, base, 0.0), axis=0, keepdims=True)
    slot2 = jnp.sum(jnp.where(hit2, base, 0.0), axis=0, keepdims=True)
    j = _iota_f32((N_SLOTS, t), 0)
    place = jnp.where(j == slot1, 1.0, 0.0) + jnp.where(j == slot2, 1.0, 0.0)
    xs_ref[...] = jnp.dot(place.astype(BF16), hb, preferred_element_type=F32).astype(BF16)
    r = _iota_f32((META_ROWS, t), 0)
    meta_ref[0] = jnp.where(r == 0.0, slot1, jnp.where(r == 1.0, slot2,
                            jnp.where(r == 2.0, w1, jnp.where(r == 3.0, w2, 0.0))))
    cnt_ref[0] = jnp.broadcast_to(cnt, (ROUTER_ROWS, LANES))


def _combine_kernel(nbp, x1p_ref, x1s_ref, gt_ref, meta_ref, ys_ref, fg_ref, yp_ref, ysm_ref, slab):
    i = pl.program_id(0)
    x1 = jnp.where(i < nbp, x1p_ref[...], x1s_ref[...])
    meta = meta_ref[0]
    slot1, slot2, w1, w2 = (meta[k:k + 1, :] for k in range(4))
    j = _iota_f32((N_SLOTS, x1.shape[0]), 0)
    pick = jnp.where(j == slot1, w1, 0.0) + jnp.where(j == slot2, w2, 0.0)
    moe = lax.dot_general(pick.astype(BF16), ys_ref[...], _TN, preferred_element_type=F32)
    y = _rms(x1 + gt_ref[0] * moe, fg_ref[...])

    @pl.when(i < nbp)
    def _():
        nb, steps, d = yp_ref.shape
        for j in range(d // LANES):
            slab[j] = y[:, j * LANES:(j + 1) * LANES]
        for b in range(nb):
            for j in range(d // LANES):
                yp_ref[b, :, j * LANES:(j + 1) * LANES] = slab[j, pl.ds(b, steps, stride=nb), :]

    @pl.when(i >= nbp)
    def _():
        ysm_ref[...] = y


def _chunk_table_kernel(start_ref, m_ref, src_ref, out_ref):
    nseg = start_ref.shape[0]
    i = _iota_f32((1, out_ref.shape[1]), 1)

    def body(b, acc):
        rows = pl.ds(pl.multiple_of(b * TABLE_SEG_ROWS, TABLE_SEG_ROWS), TABLE_SEG_ROWS)
        start = start_ref[rows, :]
        val = src_ref[rows, :] + SEG_ALIGN * (i - start)
        return acc + jnp.where(i >= start, jnp.where(i < start + m_ref[rows, :], val, 0.0), 0.0)

    acc = lax.fori_loop(0, nseg // TABLE_SEG_ROWS, body,
                        jnp.zeros((TABLE_SEG_ROWS, out_ref.shape[1]), F32))
    out_ref[...] = jnp.sum(acc, axis=0, keepdims=True).astype(jnp.int32)


def _expert_kernel(te_ref, tnv_ref, tc0_ref, ctab_ref, nt_ref, xs_hbm, wg_ref, wu_ref, wd_ref,
                   ys_hbm, xbuf, ybuf, wgub, wdb, in_sem, out_sem):
    t = pl.program_id(0)
    nt = nt_ref[0]
    slot = t % 2

    def chunk_rows(j):
        return pl.ds(pl.multiple_of(j * SEG_ALIGN, SEG_ALIGN), SEG_ALIGN)

    def hbm_rows(ref, r):
        return ref.at[pl.ds(pl.multiple_of(r, SEG_ALIGN), SEG_ALIGN), :]

    def in_copy(r, j, s):
        return pltpu.make_async_copy(hbm_rows(xs_hbm, r), xbuf.at[s, chunk_rows(j), :], in_sem.at[s])

    def out_copy(r, j, s):
        return pltpu.make_async_copy(ybuf.at[s, chunk_rows(j), :], hbm_rows(ys_hbm, r), out_sem.at[s])

    def start_gather(tile, s):
        c0 = tc0_ref[tile]
        nv = tnv_ref[tile]
        for j in range(CHUNKS_PER_TILE):
            in_copy(jnp.where(j < nv, ctab_ref[c0 + j], SPARE_ROW), j, s).start()

    def wait_gather(s):
        pltpu.make_async_copy(xs_hbm.at[pl.ds(0, ROW_TILE), :], xbuf.at[s], in_sem.at[s]).wait()

    def wait_scatter(tile, s):
        nv = tnv_ref[tile]

        @pl.when(nv == CHUNKS_PER_TILE)
        def _():
            pltpu.make_async_copy(ybuf.at[s], ys_hbm.at[pl.ds(0, ROW_TILE), :], out_sem.at[s]).wait()

        @pl.when(nv < CHUNKS_PER_TILE)
        def _():
            def body(j, _):
                out_copy(0, j, s).wait()
                return 0

            lax.fori_loop(0, nv, body, 0)

    def run_tile(full):
        wait_gather(slot)
        ab = jnp.dot(xbuf[slot], wgub[...], preferred_element_type=F32)
        start_gather(t + 1, 1 - slot)
        a, b = ab[:, :D_EXPERT], ab[:, D_EXPERT:]
        o = jnp.dot((a * _sigmoid(a) * b).astype(BF16), wdb[...], preferred_element_type=F32)
        c0 = tc0_ref[t]
        if full:
            group = ROW_TILE // SCATTER_GROUPS
            for r in range(0, ROW_TILE, group):
                ybuf[slot, r:r + group, :] = o[r:r + group].astype(BF16)
                for j in range(r // SEG_ALIGN, (r + group) // SEG_ALIGN):
                    out_copy(ctab_ref[c0 + j], j, slot).start()
        else:
            ybuf[slot] = o.astype(BF16)

            def body(j, _):
                out_copy(ctab_ref[c0 + j], j, slot).start()
                return 0

            lax.fori_loop(0, tnv_ref[t], body, 0)

    @pl.when(t == 0)
    def _():
        start_gather(0, 0)

    @pl.when(jnp.logical_or(t == 0, te_ref[t] != te_ref[jnp.maximum(t - 1, 0)]))
    def _():
        wgub[:, :D_EXPERT] = wg_ref[0].astype(BF16)
        wgub[:, D_EXPERT:] = wu_ref[0].astype(BF16)
        wdb[...] = wd_ref[0].astype(BF16)

    @pl.when(t < nt)
    def _():
        @pl.when(t >= 2)
        def _():
            wait_scatter(t - 2, slot)

        pl.when(tnv_ref[t] == CHUNKS_PER_TILE)(lambda: run_tile(True))
        pl.when(tnv_ref[t] < CHUNKS_PER_TILE)(lambda: run_tile(False))

        @pl.when(t == nt - 1)
        def _():
            wait_gather(1 - slot)
            wait_scatter(t, slot)

            @pl.when(t >= 1)
            def _():
                wait_scatter(t - 1, 1 - slot)


def _tile_plan(cnt):
    nblk = cnt.shape[0]
    max_chunks = nblk * (2 * TOK_BLOCK // SEG_ALIGN + N_EXPERTS)
    max_tiles = max_chunks // CHUNKS_PER_TILE + N_EXPERTS
    table_len = -(-(max_chunks + CHUNKS_PER_TILE) // LANES) * LANES
    m = (cnt + SEG_ALIGN - 1) // SEG_ALIGN
    seg_start = jnp.cumsum(m, axis=1) - m
    src0 = jnp.arange(nblk, dtype=jnp.int32)[:, None] * N_SLOTS + SEG_ALIGN * seg_start
    seg_m = m.T.reshape(-1, 1)
    seg_first = jnp.cumsum(seg_m, axis=0) - seg_m
    assert seg_m.shape[0] % TABLE_SEG_ROWS == 0
    col = lambda v: v.astype(F32)
    chunk_src = pl.pallas_call(
        _chunk_table_kernel,
        out_shape=jax.ShapeDtypeStruct((1, table_len), jnp.int32),
        name="moe_chunk_table",
    )(col(seg_first), col(seg_m), col(src0.T.reshape(-1, 1))).reshape(-1)
    ce = jnp.sum(m, axis=0)
    c_start = jnp.cumsum(ce) - ce
    tiles = (ce + CHUNKS_PER_TILE - 1) // CHUNKS_PER_TILE
    t_cum = jnp.cumsum(tiles)
    nt = t_cum[-1]
    k = jnp.arange(max_tiles + 1, dtype=jnp.int32)
    kk = jnp.minimum(k, nt - 1)
    ek = jnp.minimum(jnp.sum((t_cum[None, :] <= kk[:, None]).astype(jnp.int32), axis=1), N_EXPERTS - 1)
    is_e = ek[:, None] == jnp.arange(N_EXPERTS, dtype=jnp.int32)[None, :]
    pick = lambda v: jnp.sum(jnp.where(is_e, v[None, :], 0), axis=1)
    local = kk - pick(t_cum - tiles)
    tile_nv = jnp.where(k < nt, jnp.clip(pick(ce) - CHUNKS_PER_TILE * local, 0, CHUNKS_PER_TILE), 0)
    tile_c0 = jnp.where(k < nt, pick(c_start) + CHUNKS_PER_TILE * local, 0)
    return tuple(v.astype(jnp.int32) for v in (ek, tile_nv, tile_c0, chunk_src, nt.reshape(1)))


def _moe(x1p, x1s, bp, sc, sh, gt, n2g, wr, br, wg, wu, wd, fg):
    nbp = x1p.shape[0] // TOK_BLOCK
    nblk = nbp + 1
    blk = (TOK_BLOCK, D_MODEL)
    p_spec = pl.BlockSpec(blk, lambda i: (jnp.minimum(i, nbp - 1), 0))
    s_spec = pl.BlockSpec(blk, lambda i: (0, 0))
    mod_spec = pl.BlockSpec((1,) + blk, lambda i: (jnp.minimum(i // nbp, 1), 0, 0))
    meta_spec = pl.BlockSpec((1, META_ROWS, TOK_BLOCK), lambda i: (i, 0, 0))
    slots_spec = pl.BlockSpec((N_SLOTS, D_MODEL), lambda i: (i, 0))
    params = pltpu.CompilerParams(dimension_semantics=("arbitrary",), vmem_limit_bytes=VMEM_LIMIT)

    def cs(shape):
        return pl.BlockSpec(shape, lambda i: (0,) * len(shape))

    xs, meta, cnt = pl.pallas_call(
        functools.partial(_dispatch_kernel, nbp),
        grid=(nblk,),
        in_specs=[p_spec, s_spec, mod_spec, mod_spec, cs((1, D_MODEL)),
                  cs((ROUTER_ROWS, D_MODEL)), cs((ROUTER_ROWS, TOK_BLOCK))],
        out_specs=[slots_spec, meta_spec,
                   pl.BlockSpec((1, ROUTER_ROWS, LANES), lambda i: (i, 0, 0))],
        out_shape=[jax.ShapeDtypeStruct((nblk * N_SLOTS, D_MODEL), BF16),
                   jax.ShapeDtypeStruct((nblk, META_ROWS, TOK_BLOCK), F32),
                   jax.ShapeDtypeStruct((nblk, ROUTER_ROWS, LANES), F32)],
        compiler_params=params,
        name="moe_dispatch",
    )(x1p, x1s, sc, sh, n2g, wr, br)

    plan = _tile_plan(cnt[:, N_EXPERT_GROUPS:N_EXPERT_GROUPS + N_EXPERTS, 0].astype(jnp.int32))
    max_tiles = plan[0].shape[0] - 1

    def w_spec(shape):
        return pl.BlockSpec((1,) + shape, lambda t, te, *_: (te[t], 0, 0))

    ys = pl.pallas_call(
        _expert_kernel,
        grid_spec=pltpu.PrefetchScalarGridSpec(
            num_scalar_prefetch=len(plan),
            grid=(max_tiles,),
            in_specs=[pl.BlockSpec(memory_space=pl.ANY),
                      w_spec((D_MODEL, D_EXPERT)), w_spec((D_MODEL, D_EXPERT)), w_spec((D_EXPERT, D_MODEL))],
            out_specs=pl.BlockSpec(memory_space=pl.ANY),
            scratch_shapes=[pltpu.VMEM((2, ROW_TILE, D_MODEL), BF16),
                            pltpu.VMEM((2, ROW_TILE, D_MODEL), BF16),
                            pltpu.VMEM((D_MODEL, 2 * D_EXPERT), BF16),
                            pltpu.VMEM((D_EXPERT, D_MODEL), BF16),
                            pltpu.SemaphoreType.DMA((2,)), pltpu.SemaphoreType.DMA((2,))]),
        out_shape=jax.ShapeDtypeStruct(xs.shape, BF16),
        input_output_aliases={len(plan): 0},
        compiler_params=params,
        name="moe_experts",
    )(*plan, xs, wg, wu, wd)

    return pl.pallas_call(
        functools.partial(_combine_kernel, nbp),
        grid=(nblk,),
        in_specs=[p_spec, s_spec, mod_spec, meta_spec, slots_spec, cs((1, D_MODEL))],
        out_specs=[pl.BlockSpec((bp, TOK_BLOCK // bp, D_MODEL), lambda i: (0, jnp.minimum(i, nbp - 1), 0)),
                   s_spec],
        out_shape=[jax.ShapeDtypeStruct((bp, x1p.shape[0] // bp, D_MODEL), F32),
                   jax.ShapeDtypeStruct(x1s.shape, F32)],
        scratch_shapes=[pltpu.VMEM((D_MODEL // LANES, TOK_BLOCK, LANES), F32)],
        compiler_params=params,
        name="moe_combine",
    )(x1p, x1s, gt, meta, ys, fg)


def _diag_halves_in(bbt):
    eye = jnp.eye(SSM_GROUPS // 2, dtype=bbt.dtype)
    blocks = bbt.reshape(SSM_GROUP, 2, SSM_GROUPS // 2, SSM_STATE).transpose(1, 0, 2, 3)
    w = eye[None, :, None, :, None] * blocks[:, None, :, :, :]
    return w.reshape(2, HALF_SSM, HALF_STATE)


def _diag_halves_out(c):
    eye = jnp.eye(SSM_GROUPS // 2, dtype=c.dtype)
    blocks = c.reshape(2, SSM_GROUPS // 2, SSM_GROUP, SSM_STATE).transpose(0, 1, 3, 2)
    w = eye[None, :, None, :, None] * blocks[:, :, :, None, :]
    return w.reshape(2, HALF_STATE, HALF_SSM)


def _to_time_major(x):
    b, s, d = x.shape
    return x.transpose(1, 0, 2).reshape(s * b, d)


def _from_time_major(x, b, s):
    return x.reshape(s, b, -1).transpose(1, 0, 2)


def kernel(x_prompt, x_sample, state_conv, state_ssm_re, state_ssm_im, c_prompt, c_sample, w_ada, b_ada, norm1_g, norm2_g, w_in, conv_w, lambda_re, lambda_im, log_dt, ssm_b_re, ssm_b_im, ssm_c_re, ssm_c_im, ssm_d, w_glu, b_glu, out_norm_conv_g, out_norm_ssm_g, w_out, w_router_group, b_router_group, w_router_expert, b_router_expert, w_expert_gate, w_expert_up, w_expert_down, final_norm_g):
    assert w_ada.shape[0] == 1, "single-layer step"
    bp, sp, _ = x_prompt.shape
    bs, ss, _ = x_sample.shape

    ar, ai, bbr, bbi = _discretise(lambda_re[0], lambda_im[0], log_dt[0], ssm_b_re[0], ssm_b_im[0])
    w_b = jnp.concatenate([_diag_halves_in(bbr), _diag_halves_in(bbi)], axis=2).astype(BF16)
    w_cr = _diag_halves_out(ssm_c_re[0]).astype(BF16)
    w_ci = _diag_halves_out(ssm_c_im[0]).astype(BF16)

    mod = _adaln(jnp.concatenate([c_prompt, c_sample], axis=0), w_ada[0], b_ada[0])
    sh1, sc1, gt1, sh2, sc2, gt2 = [mod[:, k * D_MODEL:(k + 1) * D_MODEL] for k in range(N_MOD)]

    mixer_weights = (
        norm1_g[0].reshape(1, -1), w_in[0].astype(BF16), conv_w[0], ar, ai, w_b, w_cr, w_ci,
        ssm_d[0].reshape(1, -1), w_glu[0].astype(BF16), b_glu[0].reshape(1, -1),
        out_norm_conv_g[0].reshape(1, -1), out_norm_ssm_g[0].reshape(1, -1), w_out[0].astype(BF16),
    )

    pad_rows = ROUTER_ROWS - N_EXPERT_GROUPS - N_EXPERTS
    w_router = jnp.concatenate(
        [w_router_group[0].T, w_router_expert[0].transpose(0, 2, 1).reshape(N_EXPERTS, D_MODEL),
         jnp.zeros((pad_rows, D_MODEL), F32)], axis=0)
    b_router = jnp.concatenate(
        [b_router_group[0], b_router_expert[0].reshape(-1), jnp.zeros((pad_rows,), F32)])
    b_router = jnp.broadcast_to(b_router[:, None], (ROUTER_ROWS, TOK_BLOCK))
    moe_weights = (norm2_g[0].reshape(1, -1), w_router, b_router,
                   w_expert_gate[0], w_expert_up[0], w_expert_down[0], final_norm_g.reshape(1, -1))

    def mix(x, lo, hi, h0r, h0i, cb0, chunk, in_kernel_transpose):
        nb, s, _ = x.shape
        x1, cbo, hro, hio = _mixer(
            x if in_kernel_transpose else _to_time_major(x), nb, s, chunk,
            sh1[lo:hi], sc1[lo:hi], gt1[lo:hi], h0r, h0i, cb0, mixer_weights)
        new_conv = cbo.reshape(2, nb, CONV_DIM).transpose(1, 0, 2)[None]
        new_re = hro.reshape(1, nb, SSM_GROUPS, SSM_STATE)
        new_im = hio.reshape(1, nb, SSM_GROUPS, SSM_STATE)
        return x1, new_conv, new_re, new_im

    zero_state = jnp.zeros((bp, N_STATE), F32)
    zero_conv = jnp.zeros((2 * bp, CONV_DIM), F32)
    x1p, conv_p, re_p, im_p = mix(x_prompt, 0, bp, zero_state, zero_state, zero_conv, TOK_BLOCK // bp, True)
    x1s, conv_s, re_s, im_s = mix(
        x_sample, bp, bp + bs,
        state_ssm_re[0].reshape(bs, N_STATE), state_ssm_im[0].reshape(bs, N_STATE),
        state_conv[0].transpose(1, 0, 2).reshape(2 * bs, CONV_DIM), ss, False)

    assert bs * ss == TOK_BLOCK and (bp * sp) % TOK_BLOCK == 0 and TOK_BLOCK % bp == 0

    def block_rows(m):
        return jnp.stack([jnp.tile(m[:bp], (TOK_BLOCK // bp, 1)), jnp.tile(m[bp:], (TOK_BLOCK // bs, 1))])

    y_p, y_s = _moe(x1p, x1s, bp, block_rows(sc2), block_rows(sh2), block_rows(gt2), *moe_weights)
    return (y_p, _from_time_major(y_s, bs, ss), conv_p, re_p, im_p, conv_s, re_s, im_s)
```

```python
import functools

import jax
import jax.numpy as jnp
from jax import lax
from jax.experimental import pallas as pl
from jax.experimental.pallas import tpu as pltpu

D_MODEL = 1024
CONV_DIM = 512
SSM_DIM = 512
SSM_GROUP = 16
SSM_GROUPS = 32
SSM_STATE = 64
N_STATE = SSM_GROUPS * SSM_STATE
HALF_SSM = SSM_DIM // 2
HALF_STATE = N_STATE // 2
IN_DIM = 2048
N_EXPERT_GROUPS = 4
EXPERTS_PER_GROUP = 8
N_EXPERTS = 32
D_EXPERT = 256
N_MOD = 6
EPS = 1e-6

SUBLANES = 8
LANES = 128
SCAN_COLS = 512
STRIDE_PAD = 8
VMEM_LIMIT = 60 * 1024 * 1024

F32 = jnp.float32
BF16 = jnp.bfloat16


def _sigmoid(x):
    return 1.0 / (1.0 + jnp.exp(-x))


def _gelu_tanh(x):
    return 0.5 * x * (1.0 + jnp.tanh(0.7978845608028654 * (x + 0.044715 * (x * x * x))))


def _rms(x, g):
    return x * lax.rsqrt(jnp.mean(x * x, axis=-1, keepdims=True) + EPS) * g


def _bdot(a, b):
    return jnp.dot(a.astype(BF16), b.astype(BF16), preferred_element_type=F32)


def _per_seq(v, nb, scale, shift=None):
    rows, d = v.shape
    v3 = v.reshape(rows // nb, nb, d) * scale[None]
    if shift is not None:
        v3 = v3 + shift[None]
    return v3.reshape(rows, d)


def _disc_kernel(lr_ref, li_ref, ldt_ref, brt_ref, bit_ref, ar_ref, ai_ref, bbr_ref, bbi_ref):
    lr = lr_ref[...]
    li = li_ref[...]
    dt = jnp.exp(ldt_ref[...])
    mag = jnp.exp(lr * dt)
    ar = mag * jnp.cos(li * dt)
    ai = mag * jnp.sin(li * dt)
    den = lr * lr + li * li
    fr = ((ar - 1.0) * lr + ai * li) / den
    fi = (ai * lr - (ar - 1.0) * li) / den
    ar_ref[...] = ar
    ai_ref[...] = ai
    brt = brt_ref[...]
    bit = bit_ref[...]
    bbr_ref[...] = fr * brt - fi * bit
    bbi_ref[...] = fr * bit + fi * brt


def _discretise(lambda_re, lambda_im, log_dt, b_re, b_im):
    lr = lambda_re.reshape(1, N_STATE)
    li = lambda_im.reshape(1, N_STATE)
    ldt = jnp.repeat(log_dt, SSM_STATE).reshape(1, N_STATE)
    brt = b_re.transpose(2, 0, 1).reshape(SSM_GROUP, N_STATE)
    bit = b_im.transpose(2, 0, 1).reshape(SSM_GROUP, N_STATE)
    row = jax.ShapeDtypeStruct((1, N_STATE), F32)
    mat = jax.ShapeDtypeStruct((SSM_GROUP, N_STATE), F32)
    return pl.pallas_call(_disc_kernel, out_shape=(row, row, mat, mat), name="s5_discretise")(
        lr, li, ldt, brt, bit)


def _ada_kernel(c_ref, w_ref, b_ref, o_ref):
    c = c_ref[...]
    o_ref[...] = _bdot(c * _sigmoid(c), w_ref[...]) + b_ref[...]


def _adaln(c_all, w_ada, b_ada):
    n = c_all.shape[0]
    nblk = N_MOD
    return pl.pallas_call(
        _ada_kernel,
        grid=(nblk,),
        in_specs=[
            pl.BlockSpec((n, D_MODEL), lambda j: (0, 0)),
            pl.BlockSpec((D_MODEL, D_MODEL), lambda j: (0, j)),
            pl.BlockSpec((1, D_MODEL), lambda j: (0, j)),
        ],
        out_specs=pl.BlockSpec((n, D_MODEL), lambda j: (0, j)),
        out_shape=jax.ShapeDtypeStruct((n, N_MOD * D_MODEL), F32),
        compiler_params=pltpu.CompilerParams(
            dimension_semantics=("arbitrary",), vmem_limit_bytes=VMEM_LIMIT),
        name="adaln",
    )(c_all, w_ada, b_ada.reshape(1, -1))


def _rows_to_time_major(x_ref, slab, xt_scr):
    nb, steps, d = x_ref.shape
    pitch = slab.shape[1] // nb
    for b in range(nb):
        for j in range(d // LANES):
            slab[j, b * pitch:b * pitch + steps, :] = x_ref[b, :, j * LANES:(j + 1) * LANES]
    for l in range(steps):
        for j in range(d // LANES):
            xt_scr[l * nb:(l + 1) * nb, j * LANES:(j + 1) * LANES] = slab[j, pl.ds(l, nb, stride=pitch), :]


def _mixer_kernel(nb, steps, batch_major,
                  x_ref, sh_ref, sc_ref, gt_ref, h0r_ref, h0i_ref, cb0_ref,
                  n1g_ref, win_ref, cw_ref, ar_ref, ai_ref, wb_ref, wcr_ref, wci_ref,
                  dsk_ref, wglu_ref, bglu_ref, gcv_ref, gsm_ref, wout_ref,
                  x1_ref, cbo_ref, hro_ref, hio_ref,
                  z_scr, cv_scr, s_scr, hr_scr, hi_scr, *tm_scr):
    rows = nb * steps
    i = pl.program_id(0)

    @pl.when(i == 0)
    def _():
        hr_scr[...] = h0r_ref[...]
        hi_scr[...] = h0i_ref[...]
        cv_scr[0:2 * nb, :] = cb0_ref[...]

    if batch_major:
        slab, x_tm = tm_scr
        _rows_to_time_major(x_ref, slab, x_tm)
    else:
        x_tm = x_ref
    h = _per_seq(_rms(x_tm[...], n1g_ref[...]), nb, 1.0 + sc_ref[...], sh_ref[...])
    z_scr[...] = _bdot(h, win_ref[...])

    u_cols = 3 * CONV_DIM
    for k in range(2):
        s_scr[:, 2 * k * HALF_STATE:2 * (k + 1) * HALF_STATE] = _bdot(
            z_scr[:, u_cols + k * HALF_SSM:u_cols + (k + 1) * HALF_SSM], wb_ref[k])

    def scan_cols(cg):
        cols = slice(cg * SCAN_COLS, (cg + 1) * SCAN_COLS)
        half, off = divmod(cg * SCAN_COLS, HALF_STATE)
        re_cols = slice(2 * half * HALF_STATE + off, 2 * half * HALF_STATE + off + SCAN_COLS)
        im_cols = slice(re_cols.start + HALF_STATE, re_cols.stop + HALF_STATE)
        a_r = jnp.broadcast_to(ar_ref[:, cols], (SUBLANES, SCAN_COLS))
        a_i = jnp.broadcast_to(ai_ref[:, cols], (SUBLANES, SCAN_COLS))

        def advance(hr, hi, rows_l):
            nr = a_r * hr - a_i * hi + s_scr[rows_l, re_cols]
            ni = a_r * hi + a_i * hr + s_scr[rows_l, im_cols]
            s_scr[rows_l, re_cols] = nr
            s_scr[rows_l, im_cols] = ni
            return nr, ni

        if nb == SUBLANES:
            hr, hi = hr_scr[:, cols], hi_scr[:, cols]
            for l in range(steps):
                hr, hi = advance(hr, hi, slice(l * nb, (l + 1) * nb))
            hr_scr[:, cols] = hr
            hi_scr[:, cols] = hi
            return

        def tile_body(bt, _):
            r0 = pl.multiple_of(bt * SUBLANES, SUBLANES)
            tile = pl.ds(r0, SUBLANES)
            hr, hi = lax.fori_loop(
                0, steps,
                lambda l, hc: advance(*hc, pl.ds(pl.multiple_of(l * nb + r0, SUBLANES), SUBLANES)),
                (hr_scr[tile, cols], hi_scr[tile, cols]), unroll=min(steps, 4))
            hr_scr[tile, cols] = hr
            hi_scr[tile, cols] = hi
            return 0

        lax.fori_loop(0, nb // SUBLANES, tile_body, 0)

    y_halves = []
    for k in range(2):
        for cg in range(k * HALF_STATE // SCAN_COLS, (k + 1) * HALF_STATE // SCAN_COLS):
            scan_cols(cg)
        y_halves.append(
            _bdot(s_scr[:, 2 * k * HALF_STATE:(2 * k + 1) * HALF_STATE], wcr_ref[k])
            - _bdot(s_scr[:, (2 * k + 1) * HALF_STATE:(2 * k + 2) * HALF_STATE], wci_ref[k]))

    hro_ref[...] = hr_scr[...]
    hio_ref[...] = hi_scr[...]

    y = jnp.concatenate(y_halves, axis=-1) + dsk_ref[...] * z_scr[:, u_cols:IN_DIM]
    g = _gelu_tanh(y)
    y_ssm = _rms(g * _sigmoid(_bdot(g, wglu_ref[...]) + bglu_ref[...]), gsm_ref[...])

    cv_scr[2 * nb:2 * nb + rows, :] = z_scr[:, CONV_DIM:2 * CONV_DIM] * z_scr[:, 2 * CONV_DIM:3 * CONV_DIM]
    conv = (cv_scr[0:rows, :] * cw_ref[0:1, :]
            + cv_scr[nb:nb + rows, :] * cw_ref[1:2, :]
            + cv_scr[2 * nb:2 * nb + rows, :] * cw_ref[2:3, :])
    y_conv = _rms(z_scr[:, 0:CONV_DIM] * conv, gcv_ref[...])
    carry = cv_scr[rows:rows + 2 * nb, :]
    cv_scr[0:2 * nb, :] = carry
    cbo_ref[...] = carry

    mixed = _bdot(y_conv, wout_ref[0:CONV_DIM, :]) + _bdot(y_ssm, wout_ref[CONV_DIM:2 * CONV_DIM, :])
    x1_ref[...] = x_tm[...] + _per_seq(mixed, nb, gt_ref[...])


def _const_spec(shape):
    return pl.BlockSpec(shape, lambda i: (0,) * len(shape), pipeline_mode=pl.Buffered(1))


def _mixer(x, nb, steps_total, chunk, sh, sc, gt, h0r, h0i, cb0, weights):
    rows = nb * chunk
    nchunks = steps_total // chunk
    batch_major = x.ndim == 3
    if batch_major:
        assert nb == SUBLANES
        x_spec = pl.BlockSpec((nb, chunk, D_MODEL), lambda i: (0, i, 0))
        tm_scratch = [pltpu.VMEM((D_MODEL // LANES, nb * (chunk + STRIDE_PAD), LANES), F32),
                      pltpu.VMEM((rows, D_MODEL), F32)]
    else:
        x_spec = pl.BlockSpec((rows, D_MODEL), lambda i: (i, 0))
        tm_scratch = []
    in_specs = [
        x_spec,
        _const_spec((nb, D_MODEL)), _const_spec((nb, D_MODEL)), _const_spec((nb, D_MODEL)),
        _const_spec((nb, N_STATE)), _const_spec((nb, N_STATE)), _const_spec((2 * nb, CONV_DIM)),
    ] + [_const_spec(w.shape) for w in weights]
    out_specs = [
        pl.BlockSpec((rows, D_MODEL), lambda i: (i, 0)),
        _const_spec((2 * nb, CONV_DIM)), _const_spec((nb, N_STATE)), _const_spec((nb, N_STATE)),
    ]
    out_shape = [
        jax.ShapeDtypeStruct((steps_total * nb, D_MODEL), F32),
        jax.ShapeDtypeStruct((2 * nb, CONV_DIM), F32),
        jax.ShapeDtypeStruct((nb, N_STATE), F32),
        jax.ShapeDtypeStruct((nb, N_STATE), F32),
    ]
    scratch = [
        pltpu.VMEM((rows, IN_DIM), F32),
        pltpu.VMEM((rows + 2 * nb, CONV_DIM), F32),
        pltpu.VMEM((rows, 2 * N_STATE), F32),
        pltpu.VMEM((nb, N_STATE), F32),
        pltpu.VMEM((nb, N_STATE), F32),
    ] + tm_scratch
    return pl.pallas_call(
        functools.partial(_mixer_kernel, nb, chunk, batch_major),
        grid=(nchunks,),
        in_specs=in_specs, out_specs=out_specs, out_shape=out_shape,
        scratch_shapes=scratch,
        compiler_params=pltpu.CompilerParams(
            dimension_semantics=("arbitrary",), vmem_limit_bytes=VMEM_LIMIT),
        name="mixer",
    )(x, sh, sc, gt, h0r, h0i, cb0, *weights)


TOK_BLOCK = 512
SEG_ALIGN = 16
ROW_TILE = 512
CHUNKS_PER_TILE = ROW_TILE // SEG_ALIGN
SCATTER_GROUPS = 8
GATHER_BUFFERS = 3
TABLE_SEG_ROWS = 32
N_SLOTS = -(-(2 * TOK_BLOCK + N_EXPERTS * (SEG_ALIGN - 1)) // 256) * 256
SPARE_ROW = N_SLOTS - SEG_ALIGN
assert SPARE_ROW >= 2 * TOK_BLOCK + N_EXPERTS * (SEG_ALIGN - 1)
ROUTER_ROWS = 48
META_ROWS = SUBLANES

_NT = (((1,), (1,)), ((), ()))
_TN = (((0,), (0,)), ((), ()))


def _iota_f32(shape, dim):
    return lax.broadcasted_iota(jnp.int32, shape, dim).astype(F32)


def _split_bf16(v):
    hi = v.astype(BF16)
    return hi, (v - hi.astype(F32)).astype(BF16)


def _route_t(lg):
    row = _iota_f32(lg.shape, 0)
    neg = jnp.float32(-jnp.inf)
    none = float(ROUTER_ROWS)
    is_g = row < N_EXPERT_GROUPS
    gl = jnp.where(is_g, lg, neg)
    gmax = jnp.max(gl, axis=0, keepdims=True)
    gsum = jnp.sum(jnp.where(is_g, jnp.exp(gl - gmax), 0.0), axis=0, keepdims=True)
    g_prob = 1.0 / gsum
    g_idx = jnp.min(jnp.where(gl == gmax, row, none), axis=0, keepdims=True)
    lo = N_EXPERT_GROUPS + EXPERTS_PER_GROUP * g_idx
    el = jnp.where(row >= lo, jnp.where(row < lo + EXPERTS_PER_GROUP, lg, neg), neg)
    m1 = jnp.max(el, axis=0, keepdims=True)
    i1 = jnp.min(jnp.where(el == m1, row, none), axis=0, keepdims=True)
    el2 = jnp.where(row == i1, neg, el)
    m2 = jnp.max(el2, axis=0, keepdims=True)
    i2 = jnp.min(jnp.where(el2 == m2, row, none), axis=0, keepdims=True)
    e21 = jnp.exp(m2 - m1)
    return i1, i2, g_prob / (1.0 + e21), g_prob * e21 / (1.0 + e21)


def _dispatch_kernel(nbp, x1p_ref, x1s_ref, sc_ref, sh_ref, n2g_ref, wr_ref, br_ref,
                     xs_ref, meta_ref, cnt_ref):
    x1 = jnp.where(pl.program_id(0) < nbp, x1p_ref[...], x1s_ref[...])
    h = _rms(x1, n2g_ref[...]) * (1.0 + sc_ref[0]) + sh_ref[0]
    t = h.shape[0]
    hb, hl = _split_bf16(h)
    wh, wl = _split_bf16(wr_ref[...])
    lg = (lax.dot_general(wh, hb, _NT, preferred_element_type=F32)
          + lax.dot_general(wh, hl, _NT, preferred_element_type=F32)
          + lax.dot_general(wl, hb, _NT, preferred_element_type=F32)) + br_ref[...]
    i1, i2, w1, w2 = _route_t(lg)

    row = _iota_f32((ROUTER_ROWS, t), 0)
    hit1 = row == i1
    hit2 = row == i2
    onehot = jnp.where(hit1, 1.0, 0.0) + jnp.where(hit2, 1.0, 0.0)
    before = jnp.where(_iota_f32((t, t), 0) < _iota_f32((t, t), 1), 1.0, 0.0)
    rank = _bdot(onehot, before)
    cnt = jnp.sum(onehot, axis=1, keepdims=True)
    seg = jnp.floor((cnt + (SEG_ALIGN - 1)) * (1.0 / SEG_ALIGN)) * SEG_ALIGN
    earlier = jnp.where(_iota_f32((ROUTER_ROWS, ROUTER_ROWS), 1) < _iota_f32((ROUTER_ROWS, ROUTER_ROWS), 0),
                        1.0, 0.0)
    base = rank + _bdot(earlier, jnp.broadcast_to(seg, (ROUTER_ROWS, t)))
    slot1 = jnp.sum(jnp.where(hit1, base, 0.0), axis=0, keepdims=True)
    slot2 = jnp.sum(jnp.where(hit2, base, 0.0), axis=0, keepdims=True)
    j = _iota_f32((N_SLOTS, t), 0)
    place = jnp.where(j == slot1, 1.0, 0.0) + jnp.where(j == slot2, 1.0, 0.0)
    xs_ref[...] = jnp.dot(place.astype(BF16), hb, preferred_element_type=F32).astype(BF16)
    r = _iota_f32((META_ROWS, t), 0)
    meta_ref[0] = jnp.where(r == 0.0, slot1, jnp.where(r == 1.0, slot2,
                            jnp.where(r == 2.0, w1, jnp.where(r == 3.0, w2, 0.0))))
    cnt_ref[0] = jnp.broadcast_to(cnt, (ROUTER_ROWS, LANES))


def _combine_kernel(nbp, x1p_ref, x1s_ref, gt_ref, meta_ref, ys_ref, fg_ref, yp_ref, ysm_ref, slab):
    i = pl.program_id(0)
    x1 = jnp.where(i < nbp, x1p_ref[...], x1s_ref[...])
    meta = meta_ref[0]
    slot1, slot2, w1, w2 = (meta[k:k + 1, :] for k in range(4))
    j = _iota_f32((N_SLOTS, x1.shape[0]), 0)
    pick = jnp.where(j == slot1, w1, 0.0) + jnp.where(j == slot2, w2, 0.0)
    moe = lax.dot_general(pick.astype(BF16), ys_ref[...], _TN, preferred_element_type=F32)
    y = _rms(x1 + gt_ref[0] * moe, fg_ref[...])

    @pl.when(i < nbp)
    def _():
        nb, steps, d = yp_ref.shape
        for j in range(d // LANES):
            slab[j] = y[:, j * LANES:(j + 1) * LANES]
        for b in range(nb):
            for j in range(d // LANES):
                yp_ref[b, :, j * LANES:(j + 1) * LANES] = slab[j, pl.ds(b, steps, stride=nb), :]

    @pl.when(i >= nbp)
    def _():
        ysm_ref[...] = y


def _chunk_table_kernel(start_ref, m_ref, src_ref, out_ref):
    nseg = start_ref.shape[0]
    i = _iota_f32((1, out_ref.shape[1]), 1)

    def body(b, acc):
        rows = pl.ds(pl.multiple_of(b * TABLE_SEG_ROWS, TABLE_SEG_ROWS), TABLE_SEG_ROWS)
        start = start_ref[rows, :]
        val = src_ref[rows, :] + SEG_ALIGN * (i - start)
        return acc + jnp.where(i >= start, jnp.where(i < start + m_ref[rows, :], val, 0.0), 0.0)

    acc = lax.fori_loop(0, nseg // TABLE_SEG_ROWS, body,
                        jnp.zeros((TABLE_SEG_ROWS, out_ref.shape[1]), F32))
    out_ref[...] = jnp.sum(acc, axis=0, keepdims=True).astype(jnp.int32)


def _expert_kernel(te_ref, tnv_ref, tc0_ref, ctab_ref, nt_ref, xs_hbm, wg_ref, wu_ref, wd_ref,
                   ys_hbm, xbuf, ybuf, wgub, wdb, in_sem, out_sem):
    t = pl.program_id(0)
    nt = nt_ref[0]
    slot = t % 2
    xslot = t % GATHER_BUFFERS

    def chunk_rows(j):
        return pl.ds(pl.multiple_of(j * SEG_ALIGN, SEG_ALIGN), SEG_ALIGN)

    def hbm_rows(ref, r):
        return ref.at[pl.ds(pl.multiple_of(r, SEG_ALIGN), SEG_ALIGN), :]

    def in_copy(r, j, s):
        return pltpu.make_async_copy(hbm_rows(xs_hbm, r), xbuf.at[s, chunk_rows(j), :], in_sem.at[s])

    def out_copy(r, j, s):
        return pltpu.make_async_copy(ybuf.at[s, chunk_rows(j), :], hbm_rows(ys_hbm, r), out_sem.at[s])

    def start_gather(tile, s):
        c0 = tc0_ref[tile]
        nv = tnv_ref[tile]
        for j in range(CHUNKS_PER_TILE):
            in_copy(jnp.where(j < nv, ctab_ref[c0 + j], SPARE_ROW), j, s).start()

    def wait_gather(s):
        pltpu.make_async_copy(xs_hbm.at[pl.ds(0, ROW_TILE), :], xbuf.at[s], in_sem.at[s]).wait()

    def wait_scatter(tile, s):
        nv = tnv_ref[tile]

        @pl.when(nv == CHUNKS_PER_TILE)
        def _():
            pltpu.make_async_copy(ybuf.at[s], ys_hbm.at[pl.ds(0, ROW_TILE), :], out_sem.at[s]).wait()

        @pl.when(nv < CHUNKS_PER_TILE)
        def _():
            def body(j, _):
                out_copy(0, j, s).wait()
                return 0

            lax.fori_loop(0, nv, body, 0)

    def run_tile(full):
        ahead = t + GATHER_BUFFERS - 1
        start_gather(ahead, ahead % GATHER_BUFFERS)
        wait_gather(xslot)
        ab = jnp.dot(xbuf[xslot], wgub[...], preferred_element_type=F32)
        a, b = ab[:, :D_EXPERT], ab[:, D_EXPERT:]
        o = jnp.dot((a * _sigmoid(a) * b).astype(BF16), wdb[...], preferred_element_type=F32)
        c0 = tc0_ref[t]
        if full:
            group = ROW_TILE // SCATTER_GROUPS
            for r in range(0, ROW_TILE, group):
                ybuf[slot, r:r + group, :] = o[r:r + group].astype(BF16)
                for j in range(r // SEG_ALIGN, (r + group) // SEG_ALIGN):
                    out_copy(ctab_ref[c0 + j], j, slot).start()
        else:
            ybuf[slot] = o.astype(BF16)

            def body(j, _):
                out_copy(ctab_ref[c0 + j], j, slot).start()
                return 0

            lax.fori_loop(0, tnv_ref[t], body, 0)

    @pl.when(t == 0)
    def _():
        for k in range(GATHER_BUFFERS - 1):
            start_gather(k, k)

    @pl.when(jnp.logical_or(t == 0, te_ref[t] != te_ref[jnp.maximum(t - 1, 0)]))
    def _():
        wgub[:, :D_EXPERT] = wg_ref[0].astype(BF16)
        wgub[:, D_EXPERT:] = wu_ref[0].astype(BF16)
        wdb[...] = wd_ref[0].astype(BF16)

    @pl.when(t < nt)
    def _():
        @pl.when(t >= 2)
        def _():
            wait_scatter(t - 2, slot)

        pl.when(tnv_ref[t] == CHUNKS_PER_TILE)(lambda: run_tile(True))
        pl.when(tnv_ref[t] < CHUNKS_PER_TILE)(lambda: run_tile(False))

        @pl.when(t == nt - 1)
        def _():
            for k in range(1, GATHER_BUFFERS):
                wait_gather((t + k) % GATHER_BUFFERS)
            wait_scatter(t, slot)

            @pl.when(t >= 1)
            def _():
                wait_scatter(t - 1, 1 - slot)


def _tile_plan(cnt):
    nblk = cnt.shape[0]
    max_chunks = nblk * (2 * TOK_BLOCK // SEG_ALIGN + N_EXPERTS)
    max_tiles = max_chunks // CHUNKS_PER_TILE + N_EXPERTS
    table_len = -(-(max_chunks + CHUNKS_PER_TILE) // LANES) * LANES
    m = (cnt + SEG_ALIGN - 1) // SEG_ALIGN
    seg_start = jnp.cumsum(m, axis=1) - m
    src0 = jnp.arange(nblk, dtype=jnp.int32)[:, None] * N_SLOTS + SEG_ALIGN * seg_start
    seg_m = m.T.reshape(-1, 1)
    seg_first = jnp.cumsum(seg_m, axis=0) - seg_m
    assert seg_m.shape[0] % TABLE_SEG_ROWS == 0
    col = lambda v: v.astype(F32)
    chunk_src = pl.pallas_call(
        _chunk_table_kernel,
        out_shape=jax.ShapeDtypeStruct((1, table_len), jnp.int32),
        name="moe_chunk_table",
    )(col(seg_first), col(seg_m), col(src0.T.reshape(-1, 1))).reshape(-1)
    ce = jnp.sum(m, axis=0)
    c_start = jnp.cumsum(ce) - ce
    tiles = (ce + CHUNKS_PER_TILE - 1) // CHUNKS_PER_TILE
    t_cum = jnp.cumsum(tiles)
    nt = t_cum[-1]
    k = jnp.arange(max_tiles + GATHER_BUFFERS - 1, dtype=jnp.int32)
    kk = jnp.minimum(k, nt - 1)
    ek = jnp.minimum(jnp.sum((t_cum[None, :] <= kk[:, None]).astype(jnp.int32), axis=1), N_EXPERTS - 1)
    is_e = ek[:, None] == jnp.arange(N_EXPERTS, dtype=jnp.int32)[None, :]
    pick = lambda v: jnp.sum(jnp.where(is_e, v[None, :], 0), axis=1)
    local = kk - pick(t_cum - tiles)
    tile_nv = jnp.where(k < nt, jnp.clip(pick(ce) - CHUNKS_PER_TILE * local, 0, CHUNKS_PER_TILE), 0)
    tile_c0 = jnp.where(k < nt, pick(c_start) + CHUNKS_PER_TILE * local, 0)
    return tuple(v.astype(jnp.int32) for v in (ek, tile_nv, tile_c0, chunk_src, nt.reshape(1)))


def _moe(x1p, x1s, bp, sc, sh, gt, n2g, wr, br, wg, wu, wd, fg):
    nbp = x1p.shape[0] // TOK_BLOCK
    nblk = nbp + 1
    blk = (TOK_BLOCK, D_MODEL)
    p_spec = pl.BlockSpec(blk, lambda i: (jnp.minimum(i, nbp - 1), 0))
    s_spec = pl.BlockSpec(blk, lambda i: (0, 0))
    mod_spec = pl.BlockSpec((1,) + blk, lambda i: (jnp.minimum(i // nbp, 1), 0, 0))
    meta_spec = pl.BlockSpec((1, META_ROWS, TOK_BLOCK), lambda i: (i, 0, 0))
    slots_spec = pl.BlockSpec((N_SLOTS, D_MODEL), lambda i: (i, 0))
    params = pltpu.CompilerParams(dimension_semantics=("arbitrary",), vmem_limit_bytes=VMEM_LIMIT)

    def cs(shape):
        return pl.BlockSpec(shape, lambda i: (0,) * len(shape))

    xs, meta, cnt = pl.pallas_call(
        functools.partial(_dispatch_kernel, nbp),
        grid=(nblk,),
        in_specs=[p_spec, s_spec, mod_spec, mod_spec, cs((1, D_MODEL)),
                  cs((ROUTER_ROWS, D_MODEL)), cs((ROUTER_ROWS, TOK_BLOCK))],
        out_specs=[slots_spec, meta_spec,
                   pl.BlockSpec((1, ROUTER_ROWS, LANES), lambda i: (i, 0, 0))],
        out_shape=[jax.ShapeDtypeStruct((nblk * N_SLOTS, D_MODEL), BF16),
                   jax.ShapeDtypeStruct((nblk, META_ROWS, TOK_BLOCK), F32),
                   jax.ShapeDtypeStruct((nblk, ROUTER_ROWS, LANES), F32)],
        compiler_params=params,
        name="moe_dispatch",
    )(x1p, x1s, sc, sh, n2g, wr, br)

    plan = _tile_plan(cnt[:, N_EXPERT_GROUPS:N_EXPERT_GROUPS + N_EXPERTS, 0].astype(jnp.int32))
    max_tiles = plan[0].shape[0] - (GATHER_BUFFERS - 1)

    def w_spec(shape):
        return pl.BlockSpec((1,) + shape, lambda t, te, *_: (te[t], 0, 0))

    ys = pl.pallas_call(
        _expert_kernel,
        grid_spec=pltpu.PrefetchScalarGridSpec(
            num_scalar_prefetch=len(plan),
            grid=(max_tiles,),
            in_specs=[pl.BlockSpec(memory_space=pl.ANY),
                      w_spec((D_MODEL, D_EXPERT)), w_spec((D_MODEL, D_EXPERT)), w_spec((D_EXPERT, D_MODEL))],
            out_specs=pl.BlockSpec(memory_space=pl.ANY),
            scratch_shapes=[pltpu.VMEM((GATHER_BUFFERS, ROW_TILE, D_MODEL), BF16),
                            pltpu.VMEM((2, ROW_TILE, D_MODEL), BF16),
                            pltpu.VMEM((D_MODEL, 2 * D_EXPERT), BF16),
                            pltpu.VMEM((D_EXPERT, D_MODEL), BF16),
                            pltpu.SemaphoreType.DMA((GATHER_BUFFERS,)), pltpu.SemaphoreType.DMA((2,))]),
        out_shape=jax.ShapeDtypeStruct(xs.shape, BF16),
        input_output_aliases={len(plan): 0},
        compiler_params=params,
        name="moe_experts",
    )(*plan, xs, wg, wu, wd)

    return pl.pallas_call(
        functools.partial(_combine_kernel, nbp),
        grid=(nblk,),
        in_specs=[p_spec, s_spec, mod_spec, meta_spec, slots_spec, cs((1, D_MODEL))],
        out_specs=[pl.BlockSpec((bp, TOK_BLOCK // bp, D_MODEL), lambda i: (0, jnp.minimum(i, nbp - 1), 0)),
                   s_spec],
        out_shape=[jax.ShapeDtypeStruct((bp, x1p.shape[0] // bp, D_MODEL), F32),
                   jax.ShapeDtypeStruct(x1s.shape, F32)],
        scratch_shapes=[pltpu.VMEM((D_MODEL // LANES, TOK_BLOCK, LANES), F32)],
        compiler_params=params,
        name="moe_combine",
    )(x1p, x1s, gt, meta, ys, fg)


def _diag_halves_in(bbt):
    eye = jnp.eye(SSM_GROUPS // 2, dtype=bbt.dtype)
    blocks = bbt.reshape(SSM_GROUP, 2, SSM_GROUPS // 2, SSM_STATE).transpose(1, 0, 2, 3)
    w = eye[None, :, None, :, None] * blocks[:, None, :, :, :]
    return w.reshape(2, HALF_SSM, HALF_STATE)


def _diag_halves_out(c):
    eye = jnp.eye(SSM_GROUPS // 2, dtype=c.dtype)
    blocks = c.reshape(2, SSM_GROUPS // 2, SSM_GROUP, SSM_STATE).transpose(0, 1, 3, 2)
    w = eye[None, :, None, :, None] * blocks[:, :, :, None, :]
    return w.reshape(2, HALF_STATE, HALF_SSM)


def _to_time_major(x):
    b, s, d = x.shape
    return x.transpose(1, 0, 2).reshape(s * b, d)


def _from_time_major(x, b, s):
    return x.reshape(s, b, -1).transpose(1, 0, 2)


def kernel(x_prompt, x_sample, state_conv, state_ssm_re, state_ssm_im, c_prompt, c_sample, w_ada, b_ada, norm1_g, norm2_g, w_in, conv_w, lambda_re, lambda_im, log_dt, ssm_b_re, ssm_b_im, ssm_c_re, ssm_c_im, ssm_d, w_glu, b_glu, out_norm_conv_g, out_norm_ssm_g, w_out, w_router_group, b_router_group, w_router_expert, b_router_expert, w_expert_gate, w_expert_up, w_expert_down, final_norm_g):
    assert w_ada.shape[0] == 1, "single-layer step"
    bp, sp, _ = x_prompt.shape
    bs, ss, _ = x_sample.shape

    ar, ai, bbr, bbi = _discretise(lambda_re[0], lambda_im[0], log_dt[0], ssm_b_re[0], ssm_b_im[0])
    w_b = jnp.concatenate([_diag_halves_in(bbr), _diag_halves_in(bbi)], axis=2).astype(BF16)
    w_cr = _diag_halves_out(ssm_c_re[0]).astype(BF16)
    w_ci = _diag_halves_out(ssm_c_im[0]).astype(BF16)

    mod = _adaln(jnp.concatenate([c_prompt, c_sample], axis=0), w_ada[0], b_ada[0])
    sh1, sc1, gt1, sh2, sc2, gt2 = [mod[:, k * D_MODEL:(k + 1) * D_MODEL] for k in range(N_MOD)]

    mixer_weights = (
        norm1_g[0].reshape(1, -1), w_in[0].astype(BF16), conv_w[0], ar, ai, w_b, w_cr, w_ci,
        ssm_d[0].reshape(1, -1), w_glu[0].astype(BF16), b_glu[0].reshape(1, -1),
        out_norm_conv_g[0].reshape(1, -1), out_norm_ssm_g[0].reshape(1, -1), w_out[0].astype(BF16),
    )

    pad_rows = ROUTER_ROWS - N_EXPERT_GROUPS - N_EXPERTS
    w_router = jnp.concatenate(
        [w_router_group[0].T, w_router_expert[0].transpose(0, 2, 1).reshape(N_EXPERTS, D_MODEL),
         jnp.zeros((pad_rows, D_MODEL), F32)], axis=0)
    b_router = jnp.concatenate(
        [b_router_group[0], b_router_expert[0].reshape(-1), jnp.zeros((pad_rows,), F32)])
    b_router = jnp.broadcast_to(b_router[:, None], (ROUTER_ROWS, TOK_BLOCK))
    moe_weights = (norm2_g[0].reshape(1, -1), w_router, b_router,
                   w_expert_gate[0], w_expert_up[0], w_expert_down[0], final_norm_g.reshape(1, -1))

    def mix(x, lo, hi, h0r, h0i, cb0, chunk, in_kernel_transpose):
        nb, s, _ = x.shape
        x1, cbo, hro, hio = _mixer(
            x if in_kernel_transpose else _to_time_major(x), nb, s, chunk,
            sh1[lo:hi], sc1[lo:hi], gt1[lo:hi], h0r, h0i, cb0, mixer_weights)
        new_conv = cbo.reshape(2, nb, CONV_DIM).transpose(1, 0, 2)[None]
        new_re = hro.reshape(1, nb, SSM_GROUPS, SSM_STATE)
        new_im = hio.reshape(1, nb, SSM_GROUPS, SSM_STATE)
        return x1, new_conv, new_re, new_im

    zero_state = jnp.zeros((bp, N_STATE), F32)
    zero_conv = jnp.zeros((2 * bp, CONV_DIM), F32)
    x1p, conv_p, re_p, im_p = mix(x_prompt, 0, bp, zero_state, zero_state, zero_conv, TOK_BLOCK // bp, True)
    x1s, conv_s, re_s, im_s = mix(
        x_sample, bp, bp + bs,
        state_ssm_re[0].reshape(bs, N_STATE), state_ssm_im[0].reshape(bs, N_STATE),
        state_conv[0].transpose(1, 0, 2).reshape(2 * bs, CONV_DIM), ss, False)

    assert bs * ss == TOK_BLOCK and (bp * sp) % TOK_BLOCK == 0 and TOK_BLOCK % bp == 0

    def block_rows(m):
        return jnp.stack([jnp.tile(m[:bp], (TOK_BLOCK // bp, 1)), jnp.tile(m[bp:], (TOK_BLOCK // bs, 1))])

    y_p, y_s = _moe(x1p, x1s, bp, block_rows(sc2), block_rows(sh2), block_rows(gt2), *moe_weights)
    return (y_p, _from_time_major(y_s, bs, ss), conv_p, re_p, im_p, conv_s, re_s, im_s)
```

```python
import functools

import jax
import jax.numpy as jnp
from jax import lax
from jax.experimental import pallas as pl
from jax.experimental.pallas import tpu as pltpu

D_MODEL = 1024
CONV_DIM = 512
SSM_DIM = 512
SSM_GROUP = 16
SSM_GROUPS = 32
SSM_STATE = 64
N_STATE = SSM_GROUPS * SSM_STATE
HALF_SSM = SSM_DIM // 2
HALF_STATE = N_STATE // 2
IN_DIM = 2048
N_EXPERT_GROUPS = 4
EXPERTS_PER_GROUP = 8
N_EXPERTS = 32
D_EXPERT = 256
N_MOD = 6
EPS = 1e-6

SUBLANES = 8
LANES = 128
SCAN_COLS = 512
STRIDE_PAD = 8
VMEM_LIMIT = 60 * 1024 * 1024

F32 = jnp.float32
BF16 = jnp.bfloat16


def _sigmoid(x):
    return 1.0 / (1.0 + jnp.exp(-x))


def _gelu_tanh(x):
    return 0.5 * x * (1.0 + jnp.tanh(0.7978845608028654 * (x + 0.044715 * (x * x * x))))


def _rms(x, g):
    return x * lax.rsqrt(jnp.mean(x * x, axis=-1, keepdims=True) + EPS) * g


def _bdot(a, b):
    return jnp.dot(a.astype(BF16), b.astype(BF16), preferred_element_type=F32)


def _per_seq(v, nb, scale, shift=None):
    rows, d = v.shape
    v3 = v.reshape(rows // nb, nb, d) * scale[None]
    if shift is not None:
        v3 = v3 + shift[None]
    return v3.reshape(rows, d)


def _disc_kernel(lr_ref, li_ref, ldt_ref, brt_ref, bit_ref, ar_ref, ai_ref, bbr_ref, bbi_ref):
    lr = lr_ref[...]
    li = li_ref[...]
    dt = jnp.exp(ldt_ref[...])
    mag = jnp.exp(lr * dt)
    ar = mag * jnp.cos(li * dt)
    ai = mag * jnp.sin(li * dt)
    den = lr * lr + li * li
    fr = ((ar - 1.0) * lr + ai * li) / den
    fi = (ai * lr - (ar - 1.0) * li) / den
    ar_ref[...] = ar
    ai_ref[...] = ai
    brt = brt_ref[...]
    bit = bit_ref[...]
    bbr_ref[...] = fr * brt - fi * bit
    bbi_ref[...] = fr * bit + fi * brt


def _discretise(lambda_re, lambda_im, log_dt, b_re, b_im):
    lr = lambda_re.reshape(1, N_STATE)
    li = lambda_im.reshape(1, N_STATE)
    ldt = jnp.repeat(log_dt, SSM_STATE).reshape(1, N_STATE)
    brt = b_re.transpose(2, 0, 1).reshape(SSM_GROUP, N_STATE)
    bit = b_im.transpose(2, 0, 1).reshape(SSM_GROUP, N_STATE)
    row = jax.ShapeDtypeStruct((1, N_STATE), F32)
    mat = jax.ShapeDtypeStruct((SSM_GROUP, N_STATE), F32)
    return pl.pallas_call(_disc_kernel, out_shape=(row, row, mat, mat), name="s5_discretise")(
        lr, li, ldt, brt, bit)


def _ada_kernel(c_ref, w_ref, b_ref, o_ref):
    c = c_ref[...]
    o_ref[...] = _bdot(c * _sigmoid(c), w_ref[...]) + b_ref[...]


def _adaln(c_all, w_ada, b_ada):
    n = c_all.shape[0]
    nblk = N_MOD
    return pl.pallas_call(
        _ada_kernel,
        grid=(nblk,),
        in_specs=[
            pl.BlockSpec((n, D_MODEL), lambda j: (0, 0)),
            pl.BlockSpec((D_MODEL, D_MODEL), lambda j: (0, j)),
            pl.BlockSpec((1, D_MODEL), lambda j: (0, j)),
        ],
        out_specs=pl.BlockSpec((n, D_MODEL), lambda j: (0, j)),
        out_shape=jax.ShapeDtypeStruct((n, N_MOD * D_MODEL), F32),
        compiler_params=pltpu.CompilerParams(
            dimension_semantics=("arbitrary",), vmem_limit_bytes=VMEM_LIMIT),
        name="adaln",
    )(c_all, w_ada, b_ada.reshape(1, -1))


def _rows_to_time_major(x_ref, slab, xt_scr):
    nb, steps, d = x_ref.shape
    pitch = slab.shape[1] // nb
    for b in range(nb):
        for j in range(d // LANES):
            slab[j, b * pitch:b * pitch + steps, :] = x_ref[b, :, j * LANES:(j + 1) * LANES]
    for l in range(steps):
        for j in range(d // LANES):
            xt_scr[l * nb:(l + 1) * nb, j * LANES:(j + 1) * LANES] = slab[j, pl.ds(l, nb, stride=pitch), :]


def _mixer_kernel(nb, steps, batch_major,
                  x_ref, sh_ref, sc_ref, gt_ref, h0r_ref, h0i_ref, cb0_ref,
                  n1g_ref, win_ref, cw_ref, ar_ref, ai_ref, wb_ref, wcr_ref, wci_ref,
                  dsk_ref, wglu_ref, bglu_ref, gcv_ref, gsm_ref, wout_ref,
                  x1_ref, cbo_ref, hro_ref, hio_ref,
                  z_scr, cv_scr, s_scr, hr_scr, hi_scr, *tm_scr):
    rows = nb * steps
    i = pl.program_id(0)

    @pl.when(i == 0)
    def _():
        hr_scr[...] = h0r_ref[...]
        hi_scr[...] = h0i_ref[...]
        cv_scr[0:2 * nb, :] = cb0_ref[...]

    if batch_major:
        slab, x_tm = tm_scr
        _rows_to_time_major(x_ref, slab, x_tm)
    else:
        x_tm = x_ref
    h = _per_seq(_rms(x_tm[...], n1g_ref[...]), nb, 1.0 + sc_ref[...], sh_ref[...])
    z_scr[...] = _bdot(h, win_ref[...])

    u_cols = 3 * CONV_DIM
    for k in range(2):
        s_scr[:, 2 * k * HALF_STATE:2 * (k + 1) * HALF_STATE] = _bdot(
            z_scr[:, u_cols + k * HALF_SSM:u_cols + (k + 1) * HALF_SSM], wb_ref[k])

    def scan_cols(cg):
        cols = slice(cg * SCAN_COLS, (cg + 1) * SCAN_COLS)
        half, off = divmod(cg * SCAN_COLS, HALF_STATE)
        re_cols = slice(2 * half * HALF_STATE + off, 2 * half * HALF_STATE + off + SCAN_COLS)
        im_cols = slice(re_cols.start + HALF_STATE, re_cols.stop + HALF_STATE)
        a_r = jnp.broadcast_to(ar_ref[:, cols], (SUBLANES, SCAN_COLS))
        a_i = jnp.broadcast_to(ai_ref[:, cols], (SUBLANES, SCAN_COLS))

        def advance(hr, hi, rows_l):
            nr = a_r * hr - a_i * hi + s_scr[rows_l, re_cols]
            ni = a_r * hi + a_i * hr + s_scr[rows_l, im_cols]
            s_scr[rows_l, re_cols] = nr
            s_scr[rows_l, im_cols] = ni
            return nr, ni

        if nb == SUBLANES:
            hr, hi = hr_scr[:, cols], hi_scr[:, cols]
            for l in range(steps):
                hr, hi = advance(hr, hi, slice(l * nb, (l + 1) * nb))
            hr_scr[:, cols] = hr
            hi_scr[:, cols] = hi
            return

        def tile_body(bt, _):
            r0 = pl.multiple_of(bt * SUBLANES, SUBLANES)
            tile = pl.ds(r0, SUBLANES)
            hr, hi = lax.fori_loop(
                0, steps,
                lambda l, hc: advance(*hc, pl.ds(pl.multiple_of(l * nb + r0, SUBLANES), SUBLANES)),
                (hr_scr[tile, cols], hi_scr[tile, cols]), unroll=min(steps, 4))
            hr_scr[tile, cols] = hr
            hi_scr[tile, cols] = hi
            return 0

        lax.fori_loop(0, nb // SUBLANES, tile_body, 0)

    y_halves = []
    for k in range(2):
        for cg in range(k * HALF_STATE // SCAN_COLS, (k + 1) * HALF_STATE // SCAN_COLS):
            scan_cols(cg)
        y_halves.append(
            _bdot(s_scr[:, 2 * k * HALF_STATE:(2 * k + 1) * HALF_STATE], wcr_ref[k])
            - _bdot(s_scr[:, (2 * k + 1) * HALF_STATE:(2 * k + 2) * HALF_STATE], wci_ref[k]))

    hro_ref[...] = hr_scr[...]
    hio_ref[...] = hi_scr[...]

    y = jnp.concatenate(y_halves, axis=-1) + dsk_ref[...] * z_scr[:, u_cols:IN_DIM]
    g = _gelu_tanh(y)
    y_ssm = _rms(g * _sigmoid(_bdot(g, wglu_ref[...]) + bglu_ref[...]), gsm_ref[...])

    cv_scr[2 * nb:2 * nb + rows, :] = z_scr[:, CONV_DIM:2 * CONV_DIM] * z_scr[:, 2 * CONV_DIM:3 * CONV_DIM]
    conv = (cv_scr[0:rows, :] * cw_ref[0:1, :]
            + cv_scr[nb:nb + rows, :] * cw_ref[1:2, :]
            + cv_scr[2 * nb:2 * nb + rows, :] * cw_ref[2:3, :])
    y_conv = _rms(z_scr[:, 0:CONV_DIM] * conv, gcv_ref[...])
    carry = cv_scr[rows:rows + 2 * nb, :]
    cv_scr[0:2 * nb, :] = carry
    cbo_ref[...] = carry

    mixed = _bdot(y_conv, wout_ref[0:CONV_DIM, :]) + _bdot(y_ssm, wout_ref[CONV_DIM:2 * CONV_DIM, :])
    x1_ref[...] = x_tm[...] + _per_seq(mixed, nb, gt_ref[...])


def _const_spec(shape):
    return pl.BlockSpec(shape, lambda i: (0,) * len(shape), pipeline_mode=pl.Buffered(1))


def _mixer(x, nb, steps_total, chunk, sh, sc, gt, h0r, h0i, cb0, weights):
    rows = nb * chunk
    nchunks = steps_total // chunk
    batch_major = x.ndim == 3
    if batch_major:
        assert nb == SUBLANES
        x_spec = pl.BlockSpec((nb, chunk, D_MODEL), lambda i: (0, i, 0))
        tm_scratch = [pltpu.VMEM((D_MODEL // LANES, nb * (chunk + STRIDE_PAD), LANES), F32),
                      pltpu.VMEM((rows, D_MODEL), F32)]
    else:
        x_spec = pl.BlockSpec((rows, D_MODEL), lambda i: (i, 0))
        tm_scratch = []
    in_specs = [
        x_spec,
        _const_spec((nb, D_MODEL)), _const_spec((nb, D_MODEL)), _const_spec((nb, D_MODEL)),
        _const_spec((nb, N_STATE)), _const_spec((nb, N_STATE)), _const_spec((2 * nb, CONV_DIM)),
    ] + [_const_spec(w.shape) for w in weights]
    out_specs = [
        pl.BlockSpec((rows, D_MODEL), lambda i: (i, 0)),
        _const_spec((2 * nb, CONV_DIM)), _const_spec((nb, N_STATE)), _const_spec((nb, N_STATE)),
    ]
    out_shape = [
        jax.ShapeDtypeStruct((steps_total * nb, D_MODEL), F32),
        jax.ShapeDtypeStruct((2 * nb, CONV_DIM), F32),
        jax.ShapeDtypeStruct((nb, N_STATE), F32),
        jax.ShapeDtypeStruct((nb, N_STATE), F32),
    ]
    scratch = [
        pltpu.VMEM((rows, IN_DIM), F32),
        pltpu.VMEM((rows + 2 * nb, CONV_DIM), F32),
        pltpu.VMEM((rows, 2 * N_STATE), F32),
        pltpu.VMEM((nb, N_STATE), F32),
        pltpu.VMEM((nb, N_STATE), F32),
    ] + tm_scratch
    return pl.pallas_call(
        functools.partial(_mixer_kernel, nb, chunk, batch_major),
        grid=(nchunks,),
        in_specs=in_specs, out_specs=out_specs, out_shape=out_shape,
        scratch_shapes=scratch,
        compiler_params=pltpu.CompilerParams(
            dimension_semantics=("arbitrary",), vmem_limit_bytes=VMEM_LIMIT),
        name="mixer",
    )(x, sh, sc, gt, h0r, h0i, cb0, *weights)


TOK_BLOCK = 512
SEG_ALIGN = 16
ROW_TILE = 512
CHUNKS_PER_TILE = ROW_TILE // SEG_ALIGN
SCATTER_GROUPS = 8
GATHER_BUFFERS = 4
TABLE_SEG_ROWS = 32
N_SLOTS = -(-(2 * TOK_BLOCK + N_EXPERTS * (SEG_ALIGN - 1)) // 256) * 256
SPARE_ROW = N_SLOTS - SEG_ALIGN
assert SPARE_ROW >= 2 * TOK_BLOCK + N_EXPERTS * (SEG_ALIGN - 1)
ROUTER_ROWS = 48
META_ROWS = SUBLANES

_NT = (((1,), (1,)), ((), ()))
_TN = (((0,), (0,)), ((), ()))


def _iota_f32(shape, dim):
    return lax.broadcasted_iota(jnp.int32, shape, dim).astype(F32)


def _split_bf16(v):
    hi = v.astype(BF16)
    return hi, (v - hi.astype(F32)).astype(BF16)


def _route_t(lg):
    row = _iota_f32(lg.shape, 0)
    neg = jnp.float32(-jnp.inf)
    none = float(ROUTER_ROWS)
    is_g = row < N_EXPERT_GROUPS
    gl = jnp.where(is_g, lg, neg)
    gmax = jnp.max(gl, axis=0, keepdims=True)
    gsum = jnp.sum(jnp.where(is_g, jnp.exp(gl - gmax), 0.0), axis=0, keepdims=True)
    g_prob = 1.0 / gsum
    g_idx = jnp.min(jnp.where(gl == gmax, row, none), axis=0, keepdims=True)
    lo = N_EXPERT_GROUPS + EXPERTS_PER_GROUP * g_idx
    el = jnp.where(row >= lo, jnp.where(row < lo + EXPERTS_PER_GROUP, lg, neg), neg)
    m1 = jnp.max(el, axis=0, keepdims=True)
    i1 = jnp.min(jnp.where(el == m1, row, none), axis=0, keepdims=True)
    el2 = jnp.where(row == i1, neg, el)
    m2 = jnp.max(el2, axis=0, keepdims=True)
    i2 = jnp.min(jnp.where(el2 == m2, row, none), axis=0, keepdims=True)
    e21 = jnp.exp(m2 - m1)
    return i1, i2, g_prob / (1.0 + e21), g_prob * e21 / (1.0 + e21)


def _dispatch_kernel(nbp, x1p_ref, x1s_ref, sc_ref, sh_ref, n2g_ref, wr_ref, br_ref,
                     xs_ref, meta_ref, cnt_ref):
    x1 = jnp.where(pl.program_id(0) < nbp, x1p_ref[...], x1s_ref[...])
    h = _rms(x1, n2g_ref[...]) * (1.0 + sc_ref[0]) + sh_ref[0]
    t = h.shape[0]
    hb, hl = _split_bf16(h)
    wh, wl = _split_bf16(wr_ref[...])
    lg = (lax.dot_general(wh, hb, _NT, preferred_element_type=F32)
          + lax.dot_general(wh, hl, _NT, preferred_element_type=F32)
          + lax.dot_general(wl, hb, _NT, preferred_element_type=F32)) + br_ref[...]
    i1, i2, w1, w2 = _route_t(lg)

    row = _iota_f32((ROUTER_ROWS, t), 0)
    hit1 = row == i1
    hit2 = row == i2
    onehot = jnp.where(hit1, 1.0, 0.0) + jnp.where(hit2, 1.0, 0.0)
    before = jnp.where(_iota_f32((t, t), 0) < _iota_f32((t, t), 1), 1.0, 0.0)
    rank = _bdot(onehot, before)
    cnt = jnp.sum(onehot, axis=1, keepdims=True)
    seg = jnp.floor((cnt + (SEG_ALIGN - 1)) * (1.0 / SEG_ALIGN)) * SEG_ALIGN
    earlier = jnp.where(_iota_f32((ROUTER_ROWS, ROUTER_ROWS), 1) < _iota_f32((ROUTER_ROWS, ROUTER_ROWS), 0),
                        1.0, 0.0)
    base = rank + _bdot(earlier, jnp.broadcast_to(seg, (ROUTER_ROWS, t)))
    slot1 = jnp.sum(jnp.where(hit1, base, 0.0), axis=0, keepdims=True)
    slot2 = jnp.sum(jnp.where(hit2, base, 0.0), axis=0, keepdims=True)
    j = _iota_f32((N_SLOTS, t), 0)
    place = jnp.where(j == slot1, 1.0, 0.0) + jnp.where(j == slot2, 1.0, 0.0)
    xs_ref[...] = jnp.dot(place.astype(BF16), hb, preferred_element_type=F32).astype(BF16)
    r = _iota_f32((META_ROWS, t), 0)
    meta_ref[0] = jnp.where(r == 0.0, slot1, jnp.where(r == 1.0, slot2,
                            jnp.where(r == 2.0, w1, jnp.where(r == 3.0, w2, 0.0))))
    cnt_ref[0] = jnp.broadcast_to(cnt, (ROUTER_ROWS, LANES))


def _combine_kernel(nbp, x1p_ref, x1s_ref, gt_ref, meta_ref, ys_ref, fg_ref, yp_ref, ysm_ref, slab):
    i = pl.program_id(0)
    x1 = jnp.where(i < nbp, x1p_ref[...], x1s_ref[...])
    meta = meta_ref[0]
    slot1, slot2, w1, w2 = (meta[k:k + 1, :] for k in range(4))
    j = _iota_f32((N_SLOTS, x1.shape[0]), 0)
    pick = jnp.where(j == slot1, w1, 0.0) + jnp.where(j == slot2, w2, 0.0)
    moe = lax.dot_general(pick.astype(BF16), ys_ref[...], _TN, preferred_element_type=F32)
    y = _rms(x1 + gt_ref[0] * moe, fg_ref[...])

    @pl.when(i < nbp)
    def _():
        nb, steps, d = yp_ref.shape
        for j in range(d // LANES):
            slab[j] = y[:, j * LANES:(j + 1) * LANES]
        for b in range(nb):
            for j in range(d // LANES):
                yp_ref[b, :, j * LANES:(j + 1) * LANES] = slab[j, pl.ds(b, steps, stride=nb), :]

    @pl.when(i >= nbp)
    def _():
        ysm_ref[...] = y


def _chunk_table_kernel(start_ref, m_ref, src_ref, out_ref):
    nseg = start_ref.shape[0]
    i = _iota_f32((1, out_ref.shape[1]), 1)

    def body(b, acc):
        rows = pl.ds(pl.multiple_of(b * TABLE_SEG_ROWS, TABLE_SEG_ROWS), TABLE_SEG_ROWS)
        start = start_ref[rows, :]
        val = src_ref[rows, :] + SEG_ALIGN * (i - start)
        return acc + jnp.where(i >= start, jnp.where(i < start + m_ref[rows, :], val, 0.0), 0.0)

    acc = lax.fori_loop(0, nseg // TABLE_SEG_ROWS, body,
                        jnp.zeros((TABLE_SEG_ROWS, out_ref.shape[1]), F32))
    out_ref[...] = jnp.sum(acc, axis=0, keepdims=True).astype(jnp.int32)


def _expert_kernel(te_ref, tnv_ref, tc0_ref, ctab_ref, nt_ref, xs_hbm, wg_ref, wu_ref, wd_ref,
                   ys_hbm, xbuf, ybuf, wgub, wdb, in_sem, out_sem):
    t = pl.program_id(0)
    nt = nt_ref[0]
    slot = t % 2
    xslot = t % GATHER_BUFFERS

    def chunk_rows(j):
        return pl.ds(pl.multiple_of(j * SEG_ALIGN, SEG_ALIGN), SEG_ALIGN)

    def hbm_rows(ref, r):
        return ref.at[pl.ds(pl.multiple_of(r, SEG_ALIGN), SEG_ALIGN), :]

    def in_copy(r, j, s):
        return pltpu.make_async_copy(hbm_rows(xs_hbm, r), xbuf.at[s, chunk_rows(j), :], in_sem.at[s])

    def out_copy(r, j, s):
        return pltpu.make_async_copy(ybuf.at[s, chunk_rows(j), :], hbm_rows(ys_hbm, r), out_sem.at[s])

    def start_gather(tile, s):
        c0 = tc0_ref[tile]
        nv = tnv_ref[tile]
        for j in range(CHUNKS_PER_TILE):
            in_copy(jnp.where(j < nv, ctab_ref[c0 + j], SPARE_ROW), j, s).start()

    def wait_gather(s):
        pltpu.make_async_copy(xs_hbm.at[pl.ds(0, ROW_TILE), :], xbuf.at[s], in_sem.at[s]).wait()

    def wait_scatter(tile, s):
        nv = tnv_ref[tile]

        @pl.when(nv == CHUNKS_PER_TILE)
        def _():
            pltpu.make_async_copy(ybuf.at[s], ys_hbm.at[pl.ds(0, ROW_TILE), :], out_sem.at[s]).wait()

        @pl.when(nv < CHUNKS_PER_TILE)
        def _():
            def body(j, _):
                out_copy(0, j, s).wait()
                return 0

            lax.fori_loop(0, nv, body, 0)

    def run_tile(full):
        ahead = t + GATHER_BUFFERS - 1
        start_gather(ahead, ahead % GATHER_BUFFERS)
        wait_gather(xslot)
        ab = jnp.dot(xbuf[xslot], wgub[...], preferred_element_type=F32)
        a, b = ab[:, :D_EXPERT], ab[:, D_EXPERT:]
        o = jnp.dot((a * _sigmoid(a) * b).astype(BF16), wdb[...], preferred_element_type=F32)
        c0 = tc0_ref[t]
        if full:
            group = ROW_TILE // SCATTER_GROUPS
            for r in range(0, ROW_TILE, group):
                ybuf[slot, r:r + group, :] = o[r:r + group].astype(BF16)
                for j in range(r // SEG_ALIGN, (r + group) // SEG_ALIGN):
                    out_copy(ctab_ref[c0 + j], j, slot).start()
        else:
            ybuf[slot] = o.astype(BF16)

            def body(j, _):
                out_copy(ctab_ref[c0 + j], j, slot).start()
                return 0

            lax.fori_loop(0, tnv_ref[t], body, 0)

    @pl.when(t == 0)
    def _():
        for k in range(GATHER_BUFFERS - 1):
            start_gather(k, k)

    @pl.when(jnp.logical_or(t == 0, te_ref[t] != te_ref[jnp.maximum(t - 1, 0)]))
    def _():
        wgub[:, :D_EXPERT] = wg_ref[0].astype(BF16)
        wgub[:, D_EXPERT:] = wu_ref[0].astype(BF16)
        wdb[...] = wd_ref[0].astype(BF16)

    @pl.when(t < nt)
    def _():
        @pl.when(t >= 2)
        def _():
            wait_scatter(t - 2, slot)

        pl.when(tnv_ref[t] == CHUNKS_PER_TILE)(lambda: run_tile(True))
        pl.when(tnv_ref[t] < CHUNKS_PER_TILE)(lambda: run_tile(False))

        @pl.when(t == nt - 1)
        def _():
            for k in range(1, GATHER_BUFFERS):
                wait_gather((t + k) % GATHER_BUFFERS)
            wait_scatter(t, slot)

            @pl.when(t >= 1)
            def _():
                wait_scatter(t - 1, 1 - slot)


def _tile_plan(cnt):
    nblk = cnt.shape[0]
    max_chunks = nblk * (2 * TOK_BLOCK // SEG_ALIGN + N_EXPERTS)
    max_tiles = max_chunks // CHUNKS_PER_TILE + N_EXPERTS
    table_len = -(-(max_chunks + CHUNKS_PER_TILE) // LANES) * LANES
    m = (cnt + SEG_ALIGN - 1) // SEG_ALIGN
    seg_start = jnp.cumsum(m, axis=1) - m
    src0 = jnp.arange(nblk, dtype=jnp.int32)[:, None] * N_SLOTS + SEG_ALIGN * seg_start
    seg_m = m.T.reshape(-1, 1)
    seg_first = jnp.cumsum(seg_m, axis=0) - seg_m
    assert seg_m.shape[0] % TABLE_SEG_ROWS == 0
    col = lambda v: v.astype(F32)
    chunk_src = pl.pallas_call(
        _chunk_table_kernel,
        out_shape=jax.ShapeDtypeStruct((1, table_len), jnp.int32),
        name="moe_chunk_table",
    )(col(seg_first), col(seg_m), col(src0.T.reshape(-1, 1))).reshape(-1)
    ce = jnp.sum(m, axis=0)
    c_start = jnp.cumsum(ce) - ce
    tiles = (ce + CHUNKS_PER_TILE - 1) // CHUNKS_PER_TILE
    t_cum = jnp.cumsum(tiles)
    nt = t_cum[-1]
    k = jnp.arange(max_tiles + GATHER_BUFFERS - 1, dtype=jnp.int32)
    kk = jnp.minimum(k, nt - 1)
    ek = jnp.minimum(jnp.sum((t_cum[None, :] <= kk[:, None]).astype(jnp.int32), axis=1), N_EXPERTS - 1)
    is_e = ek[:, None] == jnp.arange(N_EXPERTS, dtype=jnp.int32)[None, :]
    pick = lambda v: jnp.sum(jnp.where(is_e, v[None, :], 0), axis=1)
    local = kk - pick(t_cum - tiles)
    tile_nv = jnp.where(k < nt, jnp.clip(pick(ce) - CHUNKS_PER_TILE * local, 0, CHUNKS_PER_TILE), 0)
    tile_c0 = jnp.where(k < nt, pick(c_start) + CHUNKS_PER_TILE * local, 0)
    return tuple(v.astype(jnp.int32) for v in (ek, tile_nv, tile_c0, chunk_src, nt.reshape(1)))


def _moe(x1p, x1s, bp, sc, sh, gt, n2g, wr, br, wg, wu, wd, fg):
    nbp = x1p.shape[0] // TOK_BLOCK
    nblk = nbp + 1
    blk = (TOK_BLOCK, D_MODEL)
    p_spec = pl.BlockSpec(blk, lambda i: (jnp.minimum(i, nbp - 1), 0))
    s_spec = pl.BlockSpec(blk, lambda i: (0, 0))
    mod_spec = pl.BlockSpec((1,) + blk, lambda i: (jnp.minimum(i // nbp, 1), 0, 0))
    meta_spec = pl.BlockSpec((1, META_ROWS, TOK_BLOCK), lambda i: (i, 0, 0))
    slots_spec = pl.BlockSpec((N_SLOTS, D_MODEL), lambda i: (i, 0))
    params = pltpu.CompilerParams(dimension_semantics=("arbitrary",), vmem_limit_bytes=VMEM_LIMIT)

    def cs(shape):
        return pl.BlockSpec(shape, lambda i: (0,) * len(shape))

    xs, meta, cnt = pl.pallas_call(
        functools.partial(_dispatch_kernel, nbp),
        grid=(nblk,),
        in_specs=[p_spec, s_spec, mod_spec, mod_spec, cs((1, D_MODEL)),
                  cs((ROUTER_ROWS, D_MODEL)), cs((ROUTER_ROWS, TOK_BLOCK))],
        out_specs=[slots_spec, meta_spec,
                   pl.BlockSpec((1, ROUTER_ROWS, LANES), lambda i: (i, 0, 0))],
        out_shape=[jax.ShapeDtypeStruct((nblk * N_SLOTS, D_MODEL), BF16),
                   jax.ShapeDtypeStruct((nblk, META_ROWS, TOK_BLOCK), F32),
                   jax.ShapeDtypeStruct((nblk, ROUTER_ROWS, LANES), F32)],
        compiler_params=params,
        name="moe_dispatch",
    )(x1p, x1s, sc, sh, n2g, wr, br)

    plan = _tile_plan(cnt[:, N_EXPERT_GROUPS:N_EXPERT_GROUPS + N_EXPERTS, 0].astype(jnp.int32))
    max_tiles = plan[0].shape[0] - (GATHER_BUFFERS - 1)

    def w_spec(shape):
        return pl.BlockSpec((1,) + shape, lambda t, te, *_: (te[t], 0, 0))

    ys = pl.pallas_call(
        _expert_kernel,
        grid_spec=pltpu.PrefetchScalarGridSpec(
            num_scalar_prefetch=len(plan),
            grid=(max_tiles,),
            in_specs=[pl.BlockSpec(memory_space=pl.ANY),
                      w_spec((D_MODEL, D_EXPERT)), w_spec((D_MODEL, D_EXPERT)), w_spec((D_EXPERT, D_MODEL))],
            out_specs=pl.BlockSpec(memory_space=pl.ANY),
            scratch_shapes=[pltpu.VMEM((GATHER_BUFFERS, ROW_TILE, D_MODEL), BF16),
                            pltpu.VMEM((2, ROW_TILE, D_MODEL), BF16),
                            pltpu.VMEM((D_MODEL, 2 * D_EXPERT), BF16),
                            pltpu.VMEM((D_EXPERT, D_MODEL), BF16),
                            pltpu.SemaphoreType.DMA((GATHER_BUFFERS,)), pltpu.SemaphoreType.DMA((2,))]),
        out_shape=jax.ShapeDtypeStruct(xs.shape, BF16),
        input_output_aliases={len(plan): 0},
        compiler_params=params,
        name="moe_experts",
    )(*plan, xs, wg, wu, wd)

    return pl.pallas_call(
        functools.partial(_combine_kernel, nbp),
        grid=(nblk,),
        in_specs=[p_spec, s_spec, mod_spec, meta_spec, slots_spec, cs((1, D_MODEL))],
        out_specs=[pl.BlockSpec((bp, TOK_BLOCK // bp, D_MODEL), lambda i: (0, jnp.minimum(i, nbp - 1), 0)),
                   s_spec],
        out_shape=[jax.ShapeDtypeStruct((bp, x1p.shape[0] // bp, D_MODEL), F32),
                   jax.ShapeDtypeStruct(x1s.shape, F32)],
        scratch_shapes=[pltpu.VMEM((D_MODEL // LANES, TOK_BLOCK, LANES), F32)],
        compiler_params=params,
        name="moe_combine",
    )(x1p, x1s, gt, meta, ys, fg)


def _diag_halves_in(bbt):
    eye = jnp.eye(SSM_GROUPS // 2, dtype=bbt.dtype)
    blocks = bbt.reshape(SSM_GROUP, 2, SSM_GROUPS // 2, SSM_STATE).transpose(1, 0, 2, 3)
    w = eye[None, :, None, :, None] * blocks[:, None, :, :, :]
    return w.reshape(2, HALF_SSM, HALF_STATE)


def _diag_halves_out(c):
    eye = jnp.eye(SSM_GROUPS // 2, dtype=c.dtype)
    blocks = c.reshape(2, SSM_GROUPS // 2, SSM_GROUP, SSM_STATE).transpose(0, 1, 3, 2)
    w = eye[None, :, None, :, None] * blocks[:, :, :, None, :]
    return w.reshape(2, HALF_STATE, HALF_SSM)


def _to_time_major(x):
    b, s, d = x.shape
    return x.transpose(1, 0, 2).reshape(s * b, d)


def _from_time_major(x, b, s):
    return x.reshape(s, b, -1).transpose(1, 0, 2)


def kernel(x_prompt, x_sample, state_conv, state_ssm_re, state_ssm_im, c_prompt, c_sample, w_ada, b_ada, norm1_g, norm2_g, w_in, conv_w, lambda_re, lambda_im, log_dt, ssm_b_re, ssm_b_im, ssm_c_re, ssm_c_im, ssm_d, w_glu, b_glu, out_norm_conv_g, out_norm_ssm_g, w_out, w_router_group, b_router_group, w_router_expert, b_router_expert, w_expert_gate, w_expert_up, w_expert_down, final_norm_g):
    assert w_ada.shape[0] == 1, "single-layer step"
    bp, sp, _ = x_prompt.shape
    bs, ss, _ = x_sample.shape

    ar, ai, bbr, bbi = _discretise(lambda_re[0], lambda_im[0], log_dt[0], ssm_b_re[0], ssm_b_im[0])
    w_b = jnp.concatenate([_diag_halves_in(bbr), _diag_halves_in(bbi)], axis=2).astype(BF16)
    w_cr = _diag_halves_out(ssm_c_re[0]).astype(BF16)
    w_ci = _diag_halves_out(ssm_c_im[0]).astype(BF16)

    mod = _adaln(jnp.concatenate([c_prompt, c_sample], axis=0), w_ada[0], b_ada[0])
    sh1, sc1, gt1, sh2, sc2, gt2 = [mod[:, k * D_MODEL:(k + 1) * D_MODEL] for k in range(N_MOD)]

    mixer_weights = (
        norm1_g[0].reshape(1, -1), w_in[0].astype(BF16), conv_w[0], ar, ai, w_b, w_cr, w_ci,
        ssm_d[0].reshape(1, -1), w_glu[0].astype(BF16), b_glu[0].reshape(1, -1),
        out_norm_conv_g[0].reshape(1, -1), out_norm_ssm_g[0].reshape(1, -1), w_out[0].astype(BF16),
    )

    pad_rows = ROUTER_ROWS - N_EXPERT_GROUPS - N_EXPERTS
    w_router = jnp.concatenate(
        [w_router_group[0].T, w_router_expert[0].transpose(0, 2, 1).reshape(N_EXPERTS, D_MODEL),
         jnp.zeros((pad_rows, D_MODEL), F32)], axis=0)
    b_router = jnp.concatenate(
        [b_router_group[0], b_router_expert[0].reshape(-1), jnp.zeros((pad_rows,), F32)])
    b_router = jnp.broadcast_to(b_router[:, None], (ROUTER_ROWS, TOK_BLOCK))
    moe_weights = (norm2_g[0].reshape(1, -1), w_router, b_router,
                   w_expert_gate[0], w_expert_up[0], w_expert_down[0], final_norm_g.reshape(1, -1))

    def mix(x, lo, hi, h0r, h0i, cb0, chunk, in_kernel_transpose):
        nb, s, _ = x.shape
        x1, cbo, hro, hio = _mixer(
            x if in_kernel_transpose else _to_time_major(x), nb, s, chunk,
            sh1[lo:hi], sc1[lo:hi], gt1[lo:hi], h0r, h0i, cb0, mixer_weights)
        new_conv = cbo.reshape(2, nb, CONV_DIM).transpose(1, 0, 2)[None]
        new_re = hro.reshape(1, nb, SSM_GROUPS, SSM_STATE)
        new_im = hio.reshape(1, nb, SSM_GROUPS, SSM_STATE)
        return x1, new_conv, new_re, new_im

    zero_state = jnp.zeros((bp, N_STATE), F32)
    zero_conv = jnp.zeros((2 * bp, CONV_DIM), F32)
    x1p, conv_p, re_p, im_p = mix(x_prompt, 0, bp, zero_state, zero_state, zero_conv, TOK_BLOCK // bp, True)
    x1s, conv_s, re_s, im_s = mix(
        x_sample, bp, bp + bs,
        state_ssm_re[0].reshape(bs, N_STATE), state_ssm_im[0].reshape(bs, N_STATE),
        state_conv[0].transpose(1, 0, 2).reshape(2 * bs, CONV_DIM), ss, False)

    assert bs * ss == TOK_BLOCK and (bp * sp) % TOK_BLOCK == 0 and TOK_BLOCK % bp == 0

    def block_rows(m):
        return jnp.stack([jnp.tile(m[:bp], (TOK_BLOCK // bp, 1)), jnp.tile(m[bp:], (TOK_BLOCK // bs, 1))])

    y_p, y_s = _moe(x1p, x1s, bp, block_rows(sc2), block_rows(sh2), block_rows(gt2), *moe_weights)
    return (y_p, _from_time_major(y_s, bs, ss), conv_p, re_p, im_p, conv_s, re_s, im_s)
```

```python
import functools

import jax
import jax.numpy as jnp
from jax import lax
from jax.experimental import pallas as pl
from jax.experimental.pallas import tpu as pltpu

D_MODEL = 1024
CONV_DIM = 512
SSM_DIM = 512
SSM_GROUP = 16
SSM_GROUPS = 32
SSM_STATE = 64
N_STATE = SSM_GROUPS * SSM_STATE
HALF_SSM = SSM_DIM // 2
HALF_STATE = N_STATE // 2
IN_DIM = 2048
N_EXPERT_GROUPS = 4
EXPERTS_PER_GROUP = 8
N_EXPERTS = 32
D_EXPERT = 256
N_MOD = 6
EPS = 1e-6

SUBLANES = 8
LANES = 128
SCAN_COLS = 512
STRIDE_PAD = 8
VMEM_LIMIT = 60 * 1024 * 1024

F32 = jnp.float32
BF16 = jnp.bfloat16


def _sigmoid(x):
    return 1.0 / (1.0 + jnp.exp(-x))


def _gelu_tanh(x):
    return 0.5 * x * (1.0 + jnp.tanh(0.7978845608028654 * (x + 0.044715 * (x * x * x))))


def _rms(x, g):
    return x * lax.rsqrt(jnp.mean(x * x, axis=-1, keepdims=True) + EPS) * g


def _bdot(a, b):
    return jnp.dot(a.astype(BF16), b.astype(BF16), preferred_element_type=F32)


def _per_seq(v, nb, scale, shift=None):
    rows, d = v.shape
    v3 = v.reshape(rows // nb, nb, d) * scale[None]
    if shift is not None:
        v3 = v3 + shift[None]
    return v3.reshape(rows, d)


def _disc_kernel(lr_ref, li_ref, ldt_ref, brt_ref, bit_ref, ar_ref, ai_ref, bbr_ref, bbi_ref):
    lr = lr_ref[...]
    li = li_ref[...]
    dt = jnp.exp(ldt_ref[...])
    mag = jnp.exp(lr * dt)
    ar = mag * jnp.cos(li * dt)
    ai = mag * jnp.sin(li * dt)
    den = lr * lr + li * li
    fr = ((ar - 1.0) * lr + ai * li) / den
    fi = (ai * lr - (ar - 1.0) * li) / den
    ar_ref[...] = ar
    ai_ref[...] = ai
    brt = brt_ref[...]
    bit = bit_ref[...]
    bbr_ref[...] = fr * brt - fi * bit
    bbi_ref[...] = fr * bit + fi * brt


def _discretise(lambda_re, lambda_im, log_dt, b_re, b_im):
    lr = lambda_re.reshape(1, N_STATE)
    li = lambda_im.reshape(1, N_STATE)
    ldt = jnp.repeat(log_dt, SSM_STATE).reshape(1, N_STATE)
    brt = b_re.transpose(2, 0, 1).reshape(SSM_GROUP, N_STATE)
    bit = b_im.transpose(2, 0, 1).reshape(SSM_GROUP, N_STATE)
    row = jax.ShapeDtypeStruct((1, N_STATE), F32)
    mat = jax.ShapeDtypeStruct((SSM_GROUP, N_STATE), F32)
    return pl.pallas_call(_disc_kernel, out_shape=(row, row, mat, mat), name="s5_discretise")(
        lr, li, ldt, brt, bit)


def _ada_kernel(c_ref, w_ref, b_ref, o_ref):
    c = c_ref[...]
    o_ref[...] = _bdot(c * _sigmoid(c), w_ref[...]) + b_ref[...]


def _adaln(c_all, w_ada, b_ada):
    n = c_all.shape[0]
    nblk = N_MOD
    return pl.pallas_call(
        _ada_kernel,
        grid=(nblk,),
        in_specs=[
            pl.BlockSpec((n, D_MODEL), lambda j: (0, 0)),
            pl.BlockSpec((D_MODEL, D_MODEL), lambda j: (0, j)),
            pl.BlockSpec((1, D_MODEL), lambda j: (0, j)),
        ],
        out_specs=pl.BlockSpec((n, D_MODEL), lambda j: (0, j)),
        out_shape=jax.ShapeDtypeStruct((n, N_MOD * D_MODEL), F32),
        compiler_params=pltpu.CompilerParams(
            dimension_semantics=("arbitrary",), vmem_limit_bytes=VMEM_LIMIT),
        name="adaln",
    )(c_all, w_ada, b_ada.reshape(1, -1))


def _rows_to_time_major(x_ref, slab, xt_scr):
    nb, steps, d = x_ref.shape
    pitch = slab.shape[1] // nb
    for b in range(nb):
        for j in range(d // LANES):
            slab[j, b * pitch:b * pitch + steps, :] = x_ref[b, :, j * LANES:(j + 1) * LANES]
    for l in range(steps):
        for j in range(d // LANES):
            xt_scr[l * nb:(l + 1) * nb, j * LANES:(j + 1) * LANES] = slab[j, pl.ds(l, nb, stride=pitch), :]


def _mixer_kernel(nb, steps, batch_major,
                  x_ref, sh_ref, sc_ref, gt_ref, h0r_ref, h0i_ref, cb0_ref,
                  n1g_ref, win_ref, cw_ref, ar_ref, ai_ref, wb_ref, wcr_ref, wci_ref,
                  dsk_ref, wglu_ref, bglu_ref, gcv_ref, gsm_ref, wout_ref,
                  x1_ref, cbo_ref, hro_ref, hio_ref,
                  z_scr, cv_scr, s_scr, hr_scr, hi_scr, win_b, wglu_b, wout_b, *tm_scr):
    rows = nb * steps
    i = pl.program_id(0)

    @pl.when(i == 0)
    def _():
        hr_scr[...] = h0r_ref[...]
        hi_scr[...] = h0i_ref[...]
        cv_scr[0:2 * nb, :] = cb0_ref[...]
        win_b[...] = win_ref[...].astype(BF16)
        wglu_b[...] = wglu_ref[...].astype(BF16)
        wout_b[...] = wout_ref[...].astype(BF16)

    if batch_major:
        slab, x_tm = tm_scr
        _rows_to_time_major(x_ref, slab, x_tm)
    else:
        x_tm = x_ref
    h = _per_seq(_rms(x_tm[...], n1g_ref[...]), nb, 1.0 + sc_ref[...], sh_ref[...])
    z_scr[...] = _bdot(h, win_b[...])

    u_cols = 3 * CONV_DIM
    for k in range(2):
        s_scr[:, 2 * k * HALF_STATE:2 * (k + 1) * HALF_STATE] = _bdot(
            z_scr[:, u_cols + k * HALF_SSM:u_cols + (k + 1) * HALF_SSM], wb_ref[k])

    def scan_cols(cg):
        cols = slice(cg * SCAN_COLS, (cg + 1) * SCAN_COLS)
        half, off = divmod(cg * SCAN_COLS, HALF_STATE)
        re_cols = slice(2 * half * HALF_STATE + off, 2 * half * HALF_STATE + off + SCAN_COLS)
        im_cols = slice(re_cols.start + HALF_STATE, re_cols.stop + HALF_STATE)
        a_r = jnp.broadcast_to(ar_ref[:, cols], (SUBLANES, SCAN_COLS))
        a_i = jnp.broadcast_to(ai_ref[:, cols], (SUBLANES, SCAN_COLS))

        def advance(hr, hi, rows_l):
            nr = a_r * hr - a_i * hi + s_scr[rows_l, re_cols]
            ni = a_r * hi + a_i * hr + s_scr[rows_l, im_cols]
            s_scr[rows_l, re_cols] = nr
            s_scr[rows_l, im_cols] = ni
            return nr, ni

        if nb == SUBLANES:
            hr, hi = hr_scr[:, cols], hi_scr[:, cols]
            for l in range(steps):
                hr, hi = advance(hr, hi, slice(l * nb, (l + 1) * nb))
            hr_scr[:, cols] = hr
            hi_scr[:, cols] = hi
            return

        def tile_body(bt, _):
            r0 = pl.multiple_of(bt * SUBLANES, SUBLANES)
            tile = pl.ds(r0, SUBLANES)
            hr, hi = lax.fori_loop(
                0, steps,
                lambda l, hc: advance(*hc, pl.ds(pl.multiple_of(l * nb + r0, SUBLANES), SUBLANES)),
                (hr_scr[tile, cols], hi_scr[tile, cols]), unroll=min(steps, 4))
            hr_scr[tile, cols] = hr
            hi_scr[tile, cols] = hi
            return 0

        lax.fori_loop(0, nb // SUBLANES, tile_body, 0)

    y_halves = []
    for k in range(2):
        for cg in range(k * HALF_STATE // SCAN_COLS, (k + 1) * HALF_STATE // SCAN_COLS):
            scan_cols(cg)
        y_halves.append(
            _bdot(s_scr[:, 2 * k * HALF_STATE:(2 * k + 1) * HALF_STATE], wcr_ref[k])
            - _bdot(s_scr[:, (2 * k + 1) * HALF_STATE:(2 * k + 2) * HALF_STATE], wci_ref[k]))

    hro_ref[...] = hr_scr[...]
    hio_ref[...] = hi_scr[...]

    y = jnp.concatenate(y_halves, axis=-1) + dsk_ref[...] * z_scr[:, u_cols:IN_DIM]
    g = _gelu_tanh(y)
    y_ssm = _rms(g * _sigmoid(_bdot(g, wglu_b[...]) + bglu_ref[...]), gsm_ref[...])

    cv_scr[2 * nb:2 * nb + rows, :] = z_scr[:, CONV_DIM:2 * CONV_DIM] * z_scr[:, 2 * CONV_DIM:3 * CONV_DIM]
    conv = (cv_scr[0:rows, :] * cw_ref[0:1, :]
            + cv_scr[nb:nb + rows, :] * cw_ref[1:2, :]
            + cv_scr[2 * nb:2 * nb + rows, :] * cw_ref[2:3, :])
    y_conv = _rms(z_scr[:, 0:CONV_DIM] * conv, gcv_ref[...])
    carry = cv_scr[rows:rows + 2 * nb, :]
    cv_scr[0:2 * nb, :] = carry
    cbo_ref[...] = carry

    mixed = _bdot(y_conv, wout_b[0:CONV_DIM, :]) + _bdot(y_ssm, wout_b[CONV_DIM:2 * CONV_DIM, :])
    x1_ref[...] = x_tm[...] + _per_seq(mixed, nb, gt_ref[...])


def _const_spec(shape):
    return pl.BlockSpec(shape, lambda i: (0,) * len(shape), pipeline_mode=pl.Buffered(1))


def _mixer(x, nb, steps_total, chunk, mod, mod_row0, h0r, h0i, cb0, weights):
    rows = nb * chunk
    nchunks = steps_total // chunk
    batch_major = x.ndim == 3
    if batch_major:
        assert nb == SUBLANES
        x_spec = pl.BlockSpec((nb, chunk, D_MODEL), lambda i: (0, i, 0))
        tm_scratch = [pltpu.VMEM((D_MODEL // LANES, nb * (chunk + STRIDE_PAD), LANES), F32),
                      pltpu.VMEM((rows, D_MODEL), F32)]
    else:
        x_spec = pl.BlockSpec((rows, D_MODEL), lambda i: (i, 0))
        tm_scratch = []
    assert mod_row0 % nb == 0

    def mod_spec(k):
        return pl.BlockSpec((nb, D_MODEL), lambda i: (mod_row0 // nb, k), pipeline_mode=pl.Buffered(1))

    in_specs = [
        x_spec, mod_spec(0), mod_spec(1), mod_spec(2),
        _const_spec((nb, N_STATE)), _const_spec((nb, N_STATE)), _const_spec((2 * nb, CONV_DIM)),
    ] + [_const_spec(w.shape) for w in weights]
    out_specs = [
        pl.BlockSpec((rows, D_MODEL), lambda i: (i, 0)),
        _const_spec((2 * nb, CONV_DIM)), _const_spec((nb, N_STATE)), _const_spec((nb, N_STATE)),
    ]
    out_shape = [
        jax.ShapeDtypeStruct((steps_total * nb, D_MODEL), F32),
        jax.ShapeDtypeStruct((2 * nb, CONV_DIM), F32),
        jax.ShapeDtypeStruct((nb, N_STATE), F32),
        jax.ShapeDtypeStruct((nb, N_STATE), F32),
    ]
    scratch = [
        pltpu.VMEM((rows, IN_DIM), F32),
        pltpu.VMEM((rows + 2 * nb, CONV_DIM), F32),
        pltpu.VMEM((rows, 2 * N_STATE), F32),
        pltpu.VMEM((nb, N_STATE), F32),
        pltpu.VMEM((nb, N_STATE), F32),
        pltpu.VMEM((D_MODEL, IN_DIM), BF16),
        pltpu.VMEM((SSM_DIM, SSM_DIM), BF16),
        pltpu.VMEM((D_MODEL, D_MODEL), BF16),
    ] + tm_scratch
    return pl.pallas_call(
        functools.partial(_mixer_kernel, nb, chunk, batch_major),
        grid=(nchunks,),
        in_specs=in_specs, out_specs=out_specs, out_shape=out_shape,
        scratch_shapes=scratch,
        compiler_params=pltpu.CompilerParams(
            dimension_semantics=("arbitrary",), vmem_limit_bytes=VMEM_LIMIT),
        name="mixer",
    )(x, mod, mod, mod, h0r, h0i, cb0, *weights)


TOK_BLOCK = 512
SEG_ALIGN = 16
ROW_TILE = 512
CHUNKS_PER_TILE = ROW_TILE // SEG_ALIGN
SCATTER_GROUPS = 8
GATHER_BUFFERS = 3
TABLE_SEG_ROWS = 32
N_SLOTS = -(-(2 * TOK_BLOCK + N_EXPERTS * (SEG_ALIGN - 1)) // 256) * 256
SPARE_ROW = N_SLOTS - SEG_ALIGN
assert SPARE_ROW >= 2 * TOK_BLOCK + N_EXPERTS * (SEG_ALIGN - 1)
ROUTER_ROWS = 48
META_ROWS = SUBLANES

_NT = (((1,), (1,)), ((), ()))
_TN = (((0,), (0,)), ((), ()))


def _iota_f32(shape, dim):
    return lax.broadcasted_iota(jnp.int32, shape, dim).astype(F32)


def _split_bf16(v):
    hi = v.astype(BF16)
    return hi, (v - hi.astype(F32)).astype(BF16)


def _route_t(lg):
    row = _iota_f32(lg.shape, 0)
    neg = jnp.float32(-jnp.inf)
    none = float(ROUTER_ROWS)
    is_g = row < N_EXPERT_GROUPS
    gl = jnp.where(is_g, lg, neg)
    gmax = jnp.max(gl, axis=0, keepdims=True)
    gsum = jnp.sum(jnp.where(is_g, jnp.exp(gl - gmax), 0.0), axis=0, keepdims=True)
    g_prob = 1.0 / gsum
    g_idx = jnp.min(jnp.where(gl == gmax, row, none), axis=0, keepdims=True)
    lo = N_EXPERT_GROUPS + EXPERTS_PER_GROUP * g_idx
    el = jnp.where(row >= lo, jnp.where(row < lo + EXPERTS_PER_GROUP, lg, neg), neg)
    m1 = jnp.max(el, axis=0, keepdims=True)
    i1 = jnp.min(jnp.where(el == m1, row, none), axis=0, keepdims=True)
    el2 = jnp.where(row == i1, neg, el)
    m2 = jnp.max(el2, axis=0, keepdims=True)
    i2 = jnp.min(jnp.where(el2 == m2, row, none), axis=0, keepdims=True)
    e21 = jnp.exp(m2 - m1)
    return i1, i2, g_prob / (1.0 + e21), g_prob * e21 / (1.0 + e21)


def _dispatch_kernel(nbp, x1p_ref, x1s_ref, sc_ref, sh_ref, n2g_ref, wr_ref, br_ref,
                     xs_ref, meta_ref, cnt_ref):
    x1 = jnp.where(pl.program_id(0) < nbp, x1p_ref[...], x1s_ref[...])
    h = _rms(x1, n2g_ref[...]) * (1.0 + sc_ref[0]) + sh_ref[0]
    t = h.shape[0]
    hb, hl = _split_bf16(h)
    wh, wl = _split_bf16(wr_ref[...])
    lg = (lax.dot_general(wh, hb, _NT, preferred_element_type=F32)
          + lax.dot_general(wh, hl, _NT, preferred_element_type=F32)
          + lax.dot_general(wl, hb, _NT, preferred_element_type=F32)) + br_ref[...]
    i1, i2, w1, w2 = _route_t(lg)

    row = _iota_f32((ROUTER_ROWS, t), 0)
    hit1 = row == i1
    hit2 = row == i2
    onehot = jnp.where(hit1, 1.0, 0.0) + jnp.where(hit2, 1.0, 0.0)
    before = jnp.where(_iota_f32((t, t), 0) < _iota_f32((t, t), 1), 1.0, 0.0)
    rank = _bdot(onehot, before)
    cnt = jnp.sum(onehot, axis=1, keepdims=True)
    seg = jnp.floor((cnt + (SEG_ALIGN - 1)) * (1.0 / SEG_ALIGN)) * SEG_ALIGN
    earlier = jnp.where(_iota_f32((ROUTER_ROWS, ROUTER_ROWS), 1) < _iota_f32((ROUTER_ROWS, ROUTER_ROWS), 0),
                        1.0, 0.0)
    base = rank + _bdot(earlier, jnp.broadcast_to(seg, (ROUTER_ROWS, t)))
    slot1 = jnp.sum(jnp.where(hit1, base, 0.0), axis=0, keepdims=True)
    slot2 = jnp.sum(jnp.where(hit2, base, 0.0), axis=0, keepdims=True)
    j = _iota_f32((N_SLOTS, t), 0)
    place = jnp.where(j == slot1, 1.0, 0.0) + jnp.where(j == slot2, 1.0, 0.0)
    xs_ref[...] = jnp.dot(place.astype(BF16), hb, preferred_element_type=F32).astype(BF16)
    r = _iota_f32((META_ROWS, t), 0)
    meta_ref[0] = jnp.where(r == 0.0, slot1, jnp.where(r == 1.0, slot2,
                            jnp.where(r == 2.0, w1, jnp.where(r == 3.0, w2, 0.0))))
    cnt_ref[0] = jnp.broadcast_to(cnt, (ROUTER_ROWS, LANES))


def _combine_kernel(nbp, x1p_ref, x1s_ref, gt_ref, meta_ref, ys_ref, fg_ref, yp_ref, ysm_ref, slab):
    i = pl.program_id(0)
    x1 = jnp.where(i < nbp, x1p_ref[...], x1s_ref[...])
    meta = meta_ref[0]
    slot1, slot2, w1, w2 = (meta[k:k + 1, :] for k in range(4))
    j = _iota_f32((N_SLOTS, x1.shape[0]), 0)
    pick = jnp.where(j == slot1, w1, 0.0) + jnp.where(j == slot2, w2, 0.0)
    moe = lax.dot_general(pick.astype(BF16), ys_ref[...], _TN, preferred_element_type=F32)
    y = _rms(x1 + gt_ref[0] * moe, fg_ref[...])

    @pl.when(i < nbp)
    def _():
        nb, steps, d = yp_ref.shape
        for j in range(d // LANES):
            slab[j] = y[:, j * LANES:(j + 1) * LANES]
        for b in range(nb):
            for j in range(d // LANES):
                yp_ref[b, :, j * LANES:(j + 1) * LANES] = slab[j, pl.ds(b, steps, stride=nb), :]

    @pl.when(i >= nbp)
    def _():
        ysm_ref[...] = y


def _chunk_table_kernel(start_ref, m_ref, src_ref, out_ref):
    nseg = start_ref.shape[0]
    i = _iota_f32((1, out_ref.shape[1]), 1)

    def body(b, acc):
        rows = pl.ds(pl.multiple_of(b * TABLE_SEG_ROWS, TABLE_SEG_ROWS), TABLE_SEG_ROWS)
        start = start_ref[rows, :]
        val = src_ref[rows, :] + SEG_ALIGN * (i - start)
        return acc + jnp.where(i >= start, jnp.where(i < start + m_ref[rows, :], val, 0.0), 0.0)

    acc = lax.fori_loop(0, nseg // TABLE_SEG_ROWS, body,
                        jnp.zeros((TABLE_SEG_ROWS, out_ref.shape[1]), F32))
    out_ref[...] = jnp.sum(acc, axis=0, keepdims=True).astype(jnp.int32)


def _expert_kernel(te_ref, tnv_ref, tc0_ref, ctab_ref, nt_ref, xs_hbm, wg_ref, wu_ref, wd_ref,
                   ys_hbm, xbuf, ybuf, wgub, wdb, in_sem, out_sem):
    t = pl.program_id(0)
    nt = nt_ref[0]
    slot = t % 2
    xslot = t % GATHER_BUFFERS

    def chunk_rows(j):
        return pl.ds(pl.multiple_of(j * SEG_ALIGN, SEG_ALIGN), SEG_ALIGN)

    def hbm_rows(ref, r):
        return ref.at[pl.ds(pl.multiple_of(r, SEG_ALIGN), SEG_ALIGN), :]

    def in_copy(r, j, s):
        return pltpu.make_async_copy(hbm_rows(xs_hbm, r), xbuf.at[s, chunk_rows(j), :], in_sem.at[s])

    def out_copy(r, j, s):
        return pltpu.make_async_copy(ybuf.at[s, chunk_rows(j), :], hbm_rows(ys_hbm, r), out_sem.at[s])

    def start_gather(tile, s):
        c0 = tc0_ref[tile]
        nv = tnv_ref[tile]
        for j in range(CHUNKS_PER_TILE):
            in_copy(jnp.where(j < nv, ctab_ref[c0 + j], SPARE_ROW), j, s).start()

    def wait_gather(s):
        pltpu.make_async_copy(xs_hbm.at[pl.ds(0, ROW_TILE), :], xbuf.at[s], in_sem.at[s]).wait()

    def wait_scatter(tile, s):
        nv = tnv_ref[tile]

        @pl.when(nv == CHUNKS_PER_TILE)
        def _():
            pltpu.make_async_copy(ybuf.at[s], ys_hbm.at[pl.ds(0, ROW_TILE), :], out_sem.at[s]).wait()

        @pl.when(nv < CHUNKS_PER_TILE)
        def _():
            def body(j, _):
                out_copy(0, j, s).wait()
                return 0

            lax.fori_loop(0, nv, body, 0)

    def run_tile(full):
        ahead = t + GATHER_BUFFERS - 1
        start_gather(ahead, ahead % GATHER_BUFFERS)
        wait_gather(xslot)
        ab = jnp.dot(xbuf[xslot], wgub[...], preferred_element_type=F32)
        a, b = ab[:, :D_EXPERT], ab[:, D_EXPERT:]
        o = jnp.dot((a * _sigmoid(a) * b).astype(BF16), wdb[...], preferred_element_type=F32)
        c0 = tc0_ref[t]
        if full:
            group = ROW_TILE // SCATTER_GROUPS
            for r in range(0, ROW_TILE, group):
                ybuf[slot, r:r + group, :] = o[r:r + group].astype(BF16)
                for j in range(r // SEG_ALIGN, (r + group) // SEG_ALIGN):
                    out_copy(ctab_ref[c0 + j], j, slot).start()
        else:
            ybuf[slot] = o.astype(BF16)

            def body(j, _):
                out_copy(ctab_ref[c0 + j], j, slot).start()
                return 0

            lax.fori_loop(0, tnv_ref[t], body, 0)

    @pl.when(t == 0)
    def _():
        for k in range(GATHER_BUFFERS - 1):
            start_gather(k, k)

    @pl.when(jnp.logical_or(t == 0, te_ref[t] != te_ref[jnp.maximum(t - 1, 0)]))
    def _():
        wgub[:, :D_EXPERT] = wg_ref[0].astype(BF16)
        wgub[:, D_EXPERT:] = wu_ref[0].astype(BF16)
        wdb[...] = wd_ref[0].astype(BF16)

    @pl.when(t < nt)
    def _():
        @pl.when(t >= 2)
        def _():
            wait_scatter(t - 2, slot)

        pl.when(tnv_ref[t] == CHUNKS_PER_TILE)(lambda: run_tile(True))
        pl.when(tnv_ref[t] < CHUNKS_PER_TILE)(lambda: run_tile(False))

        @pl.when(t == nt - 1)
        def _():
            for k in range(1, GATHER_BUFFERS):
                wait_gather((t + k) % GATHER_BUFFERS)
            wait_scatter(t, slot)

            @pl.when(t >= 1)
            def _():
                wait_scatter(t - 1, 1 - slot)


def _tile_plan(cnt):
    nblk = cnt.shape[0]
    max_chunks = nblk * (2 * TOK_BLOCK // SEG_ALIGN + N_EXPERTS)
    max_tiles = max_chunks // CHUNKS_PER_TILE + N_EXPERTS
    table_len = -(-(max_chunks + CHUNKS_PER_TILE) // LANES) * LANES
    m = (cnt + SEG_ALIGN - 1) // SEG_ALIGN
    seg_start = jnp.cumsum(m, axis=1) - m
    src0 = jnp.arange(nblk, dtype=jnp.int32)[:, None] * N_SLOTS + SEG_ALIGN * seg_start
    seg_m = m.T.reshape(-1, 1)
    seg_first = jnp.cumsum(seg_m, axis=0) - seg_m
    assert seg_m.shape[0] % TABLE_SEG_ROWS == 0
    col = lambda v: v.astype(F32)
    chunk_src = pl.pallas_call(
        _chunk_table_kernel,
        out_shape=jax.ShapeDtypeStruct((1, table_len), jnp.int32),
        name="moe_chunk_table",
    )(col(seg_first), col(seg_m), col(src0.T.reshape(-1, 1))).reshape(-1)
    ce = jnp.sum(m, axis=0)
    c_start = jnp.cumsum(ce) - ce
    tiles = (ce + CHUNKS_PER_TILE - 1) // CHUNKS_PER_TILE
    t_cum = jnp.cumsum(tiles)
    nt = t_cum[-1]
    k = jnp.arange(max_tiles + GATHER_BUFFERS - 1, dtype=jnp.int32)
    kk = jnp.minimum(k, nt - 1)
    ek = jnp.minimum(jnp.sum((t_cum[None, :] <= kk[:, None]).astype(jnp.int32), axis=1), N_EXPERTS - 1)
    is_e = ek[:, None] == jnp.arange(N_EXPERTS, dtype=jnp.int32)[None, :]
    pick = lambda v: jnp.sum(jnp.where(is_e, v[None, :], 0), axis=1)
    local = kk - pick(t_cum - tiles)
    tile_nv = jnp.where(k < nt, jnp.clip(pick(ce) - CHUNKS_PER_TILE * local, 0, CHUNKS_PER_TILE), 0)
    tile_c0 = jnp.where(k < nt, pick(c_start) + CHUNKS_PER_TILE * local, 0)
    return tuple(v.astype(jnp.int32) for v in (ek, tile_nv, tile_c0, chunk_src, nt.reshape(1)))


def _moe(x1p, x1s, bp, sc, sh, gt, n2g, wr, br, wg, wu, wd, fg):
    nbp = x1p.shape[0] // TOK_BLOCK
    nblk = nbp + 1
    blk = (TOK_BLOCK, D_MODEL)
    p_spec = pl.BlockSpec(blk, lambda i: (jnp.minimum(i, nbp - 1), 0))
    s_spec = pl.BlockSpec(blk, lambda i: (0, 0))
    mod_spec = pl.BlockSpec((1,) + blk, lambda i: (jnp.minimum(i // nbp, 1), 0, 0))
    meta_spec = pl.BlockSpec((1, META_ROWS, TOK_BLOCK), lambda i: (i, 0, 0))
    slots_spec = pl.BlockSpec((N_SLOTS, D_MODEL), lambda i: (i, 0))
    params = pltpu.CompilerParams(dimension_semantics=("arbitrary",), vmem_limit_bytes=VMEM_LIMIT)

    def cs(shape):
        return pl.BlockSpec(shape, lambda i: (0,) * len(shape))

    xs, meta, cnt = pl.pallas_call(
        functools.partial(_dispatch_kernel, nbp),
        grid=(nblk,),
        in_specs=[p_spec, s_spec, mod_spec, mod_spec, cs((1, D_MODEL)),
                  cs((ROUTER_ROWS, D_MODEL)), cs((ROUTER_ROWS, TOK_BLOCK))],
        out_specs=[slots_spec, meta_spec,
                   pl.BlockSpec((1, ROUTER_ROWS, LANES), lambda i: (i, 0, 0))],
        out_shape=[jax.ShapeDtypeStruct((nblk * N_SLOTS, D_MODEL), BF16),
                   jax.ShapeDtypeStruct((nblk, META_ROWS, TOK_BLOCK), F32),
                   jax.ShapeDtypeStruct((nblk, ROUTER_ROWS, LANES), F32)],
        compiler_params=params,
        name="moe_dispatch",
    )(x1p, x1s, sc, sh, n2g, wr, br)

    plan = _tile_plan(cnt[:, N_EXPERT_GROUPS:N_EXPERT_GROUPS + N_EXPERTS, 0].astype(jnp.int32))
    max_tiles = plan[0].shape[0] - (GATHER_BUFFERS - 1)

    def w_spec(shape):
        return pl.BlockSpec((1,) + shape, lambda t, te, *_: (te[t], 0, 0))

    ys = pl.pallas_call(
        _expert_kernel,
        grid_spec=pltpu.PrefetchScalarGridSpec(
            num_scalar_prefetch=len(plan),
            grid=(max_tiles,),
            in_specs=[pl.BlockSpec(memory_space=pl.ANY),
                      w_spec((D_MODEL, D_EXPERT)), w_spec((D_MODEL, D_EXPERT)), w_spec((D_EXPERT, D_MODEL))],
            out_specs=pl.BlockSpec(memory_space=pl.ANY),
            scratch_shapes=[pltpu.VMEM((GATHER_BUFFERS, ROW_TILE, D_MODEL), BF16),
                            pltpu.VMEM((2, ROW_TILE, D_MODEL), BF16),
                            pltpu.VMEM((D_MODEL, 2 * D_EXPERT), BF16),
                            pltpu.VMEM((D_EXPERT, D_MODEL), BF16),
                            pltpu.SemaphoreType.DMA((GATHER_BUFFERS,)), pltpu.SemaphoreType.DMA((2,))]),
        out_shape=jax.ShapeDtypeStruct(xs.shape, BF16),
        input_output_aliases={len(plan): 0},
        compiler_params=params,
        name="moe_experts",
    )(*plan, xs, wg, wu, wd)

    return pl.pallas_call(
        functools.partial(_combine_kernel, nbp),
        grid=(nblk,),
        in_specs=[p_spec, s_spec, mod_spec, meta_spec, slots_spec, cs((1, D_MODEL))],
        out_specs=[pl.BlockSpec((bp, TOK_BLOCK // bp, D_MODEL), lambda i: (0, jnp.minimum(i, nbp - 1), 0)),
                   s_spec],
        out_shape=[jax.ShapeDtypeStruct((bp, x1p.shape[0] // bp, D_MODEL), F32),
                   jax.ShapeDtypeStruct(x1s.shape, F32)],
        scratch_shapes=[pltpu.VMEM((D_MODEL // LANES, TOK_BLOCK, LANES), F32)],
        compiler_params=params,
        name="moe_combine",
    )(x1p, x1s, gt, meta, ys, fg)


def _diag_halves_in(bbt):
    eye = jnp.eye(SSM_GROUPS // 2, dtype=bbt.dtype)
    blocks = bbt.reshape(SSM_GROUP, 2, SSM_GROUPS // 2, SSM_STATE).transpose(1, 0, 2, 3)
    w = eye[None, :, None, :, None] * blocks[:, None, :, :, :]
    return w.reshape(2, HALF_SSM, HALF_STATE)


def _diag_halves_out(c):
    eye = jnp.eye(SSM_GROUPS // 2, dtype=c.dtype)
    blocks = c.reshape(2, SSM_GROUPS // 2, SSM_GROUP, SSM_STATE).transpose(0, 1, 3, 2)
    w = eye[None, :, None, :, None] * blocks[:, :, :, None, :]
    return w.reshape(2, HALF_STATE, HALF_SSM)


def _to_time_major(x):
    b, s, d = x.shape
    return x.transpose(1, 0, 2).reshape(s * b, d)


def _from_time_major(x, b, s):
    return x.reshape(s, b, -1).transpose(1, 0, 2)


def kernel(x_prompt, x_sample, state_conv, state_ssm_re, state_ssm_im, c_prompt, c_sample, w_ada, b_ada, norm1_g, norm2_g, w_in, conv_w, lambda_re, lambda_im, log_dt, ssm_b_re, ssm_b_im, ssm_c_re, ssm_c_im, ssm_d, w_glu, b_glu, out_norm_conv_g, out_norm_ssm_g, w_out, w_router_group, b_router_group, w_router_expert, b_router_expert, w_expert_gate, w_expert_up, w_expert_down, final_norm_g):
    assert w_ada.shape[0] == 1, "single-layer step"
    bp, sp, _ = x_prompt.shape
    bs, ss, _ = x_sample.shape

    ar, ai, bbr, bbi = _discretise(lambda_re[0], lambda_im[0], log_dt[0], ssm_b_re[0], ssm_b_im[0])
    w_b = jnp.concatenate([_diag_halves_in(bbr), _diag_halves_in(bbi)], axis=2).astype(BF16)
    w_cr = _diag_halves_out(ssm_c_re[0]).astype(BF16)
    w_ci = _diag_halves_out(ssm_c_im[0]).astype(BF16)

    assert bs % bp == 0
    mod = _adaln(jnp.concatenate([c_sample, c_prompt], axis=0), w_ada[0], b_ada[0])
    sh2, sc2, gt2 = [mod[:, k * D_MODEL:(k + 1) * D_MODEL] for k in range(N_MOD // 2, N_MOD)]

    mixer_weights = (
        norm1_g[0].reshape(1, -1), w_in[0], conv_w[0], ar, ai, w_b, w_cr, w_ci,
        ssm_d[0].reshape(1, -1), w_glu[0], b_glu[0].reshape(1, -1),
        out_norm_conv_g[0].reshape(1, -1), out_norm_ssm_g[0].reshape(1, -1), w_out[0],
    )

    pad_rows = ROUTER_ROWS - N_EXPERT_GROUPS - N_EXPERTS
    w_router = jnp.concatenate(
        [w_router_group[0].T, w_router_expert[0].transpose(0, 2, 1).reshape(N_EXPERTS, D_MODEL),
         jnp.zeros((pad_rows, D_MODEL), F32)], axis=0)
    b_router = jnp.concatenate(
        [b_router_group[0], b_router_expert[0].reshape(-1), jnp.zeros((pad_rows,), F32)])
    b_router = jnp.broadcast_to(b_router[:, None], (ROUTER_ROWS, TOK_BLOCK))
    moe_weights = (norm2_g[0].reshape(1, -1), w_router, b_router,
                   w_expert_gate[0], w_expert_up[0], w_expert_down[0], final_norm_g.reshape(1, -1))

    def mix(x, mod_row0, h0r, h0i, cb0, chunk, in_kernel_transpose):
        nb, s, _ = x.shape
        x1, cbo, hro, hio = _mixer(
            x if in_kernel_transpose else _to_time_major(x), nb, s, chunk,
            mod, mod_row0, h0r, h0i, cb0, mixer_weights)
        new_conv = cbo.reshape(2, nb, CONV_DIM).transpose(1, 0, 2)[None]
        new_re = hro.reshape(1, nb, SSM_GROUPS, SSM_STATE)
        new_im = hio.reshape(1, nb, SSM_GROUPS, SSM_STATE)
        return x1, new_conv, new_re, new_im

    zero_state = jnp.zeros((bp, N_STATE), F32)
    zero_conv = jnp.zeros((2 * bp, CONV_DIM), F32)
    x1p, conv_p, re_p, im_p = mix(x_prompt, bs, zero_state, zero_state, zero_conv, TOK_BLOCK // bp, True)
    x1s, conv_s, re_s, im_s = mix(
        x_sample, 0,
        state_ssm_re[0].reshape(bs, N_STATE), state_ssm_im[0].reshape(bs, N_STATE),
        state_conv[0].transpose(1, 0, 2).reshape(2 * bs, CONV_DIM), ss, False)

    assert bs * ss == TOK_BLOCK and (bp * sp) % TOK_BLOCK == 0 and TOK_BLOCK % bp == 0

    def block_rows(m):
        return jnp.stack([jnp.tile(m[bs:], (TOK_BLOCK // bp, 1)), jnp.tile(m[:bs], (TOK_BLOCK // bs, 1))])

    y_p, y_s = _moe(x1p, x1s, bp, block_rows(sc2), block_rows(sh2), block_rows(gt2), *moe_weights)
    return (y_p, _from_time_major(y_s, bs, ss), conv_p, re_p, im_p, conv_s, re_s, im_s)
```

```python
import functools

import jax
import jax.numpy as jnp
from jax import lax
from jax.experimental import pallas as pl
from jax.experimental.pallas import tpu as pltpu

D_MODEL = 1024
CONV_DIM = 512
SSM_DIM = 512
SSM_GROUP = 16
SSM_GROUPS = 32
SSM_STATE = 64
N_STATE = SSM_GROUPS * SSM_STATE
HALF_SSM = SSM_DIM // 2
HALF_STATE = N_STATE // 2
IN_DIM = 2048
N_EXPERT_GROUPS = 4
EXPERTS_PER_GROUP = 8
N_EXPERTS = 32
D_EXPERT = 256
N_MOD = 6
EPS = 1e-6

SUBLANES = 8
LANES = 128
SCAN_COLS = 512
STRIDE_PAD = 8
VMEM_LIMIT = 60 * 1024 * 1024

F32 = jnp.float32
BF16 = jnp.bfloat16


def _sigmoid(x):
    return 1.0 / (1.0 + jnp.exp(-x))


def _gelu_tanh(x):
    return 0.5 * x * (1.0 + jnp.tanh(0.7978845608028654 * (x + 0.044715 * (x * x * x))))


def _rms(x, g):
    return x * lax.rsqrt(jnp.mean(x * x, axis=-1, keepdims=True) + EPS) * g


def _bdot(a, b):
    return jnp.dot(a.astype(BF16), b.astype(BF16), preferred_element_type=F32)


def _per_seq(v, nb, scale, shift=None):
    rows, d = v.shape
    v3 = v.reshape(rows // nb, nb, d) * scale[None]
    if shift is not None:
        v3 = v3 + shift[None]
    return v3.reshape(rows, d)


def _disc_kernel(lr_ref, li_ref, ldt_ref, brt_ref, bit_ref, ar_ref, ai_ref, bbr_ref, bbi_ref):
    lr = lr_ref[...]
    li = li_ref[...]
    dt = jnp.exp(ldt_ref[...])
    mag = jnp.exp(lr * dt)
    ar = mag * jnp.cos(li * dt)
    ai = mag * jnp.sin(li * dt)
    den = lr * lr + li * li
    fr = ((ar - 1.0) * lr + ai * li) / den
    fi = (ai * lr - (ar - 1.0) * li) / den
    ar_ref[...] = ar
    ai_ref[...] = ai
    brt = brt_ref[...]
    bit = bit_ref[...]
    bbr_ref[...] = fr * brt - fi * bit
    bbi_ref[...] = fr * bit + fi * brt


def _discretise(lambda_re, lambda_im, log_dt, b_re, b_im):
    lr = lambda_re.reshape(1, N_STATE)
    li = lambda_im.reshape(1, N_STATE)
    ldt = jnp.repeat(log_dt, SSM_STATE).reshape(1, N_STATE)
    brt = b_re.transpose(2, 0, 1).reshape(SSM_GROUP, N_STATE)
    bit = b_im.transpose(2, 0, 1).reshape(SSM_GROUP, N_STATE)
    row = jax.ShapeDtypeStruct((1, N_STATE), F32)
    mat = jax.ShapeDtypeStruct((SSM_GROUP, N_STATE), F32)
    return pl.pallas_call(_disc_kernel, out_shape=(row, row, mat, mat), name="s5_discretise")(
        lr, li, ldt, brt, bit)


def _ada_kernel(c_ref, w_ref, b_ref, o_ref):
    c = c_ref[...]
    o_ref[...] = _bdot(c * _sigmoid(c), w_ref[...]) + b_ref[...]


def _adaln(c_all, w_ada, b_ada):
    n = c_all.shape[0]
    nblk = N_MOD
    return pl.pallas_call(
        _ada_kernel,
        grid=(nblk,),
        in_specs=[
            pl.BlockSpec((n, D_MODEL), lambda j: (0, 0)),
            pl.BlockSpec((D_MODEL, D_MODEL), lambda j: (0, j)),
            pl.BlockSpec((1, D_MODEL), lambda j: (0, j)),
        ],
        out_specs=pl.BlockSpec((n, D_MODEL), lambda j: (0, j)),
        out_shape=jax.ShapeDtypeStruct((n, N_MOD * D_MODEL), F32),
        compiler_params=pltpu.CompilerParams(
            dimension_semantics=("arbitrary",), vmem_limit_bytes=VMEM_LIMIT),
        name="adaln",
    )(c_all, w_ada, b_ada.reshape(1, -1))


def _rows_to_time_major(x_ref, slab, xt_scr):
    nb, steps, d = x_ref.shape
    pitch = slab.shape[1] // nb
    for b in range(nb):
        for j in range(d // LANES):
            slab[j, b * pitch:b * pitch + steps, :] = x_ref[b, :, j * LANES:(j + 1) * LANES]
    for l in range(steps):
        for j in range(d // LANES):
            xt_scr[l * nb:(l + 1) * nb, j * LANES:(j + 1) * LANES] = slab[j, pl.ds(l, nb, stride=pitch), :]


def _mixer_kernel(nb, steps, batch_major,
                  x_ref, sh_ref, sc_ref, gt_ref, h0r_ref, h0i_ref, cb0_ref,
                  n1g_ref, win_ref, cw_ref, ar_ref, ai_ref, wb_ref, wcr_ref, wci_ref,
                  dsk_ref, wglu_ref, bglu_ref, gcv_ref, gsm_ref, wout_ref,
                  x1_ref, cbo_ref, hro_ref, hio_ref,
                  z_scr, cv_scr, s_scr, hr_scr, hi_scr, win_b, wglu_b, wout_b, *tm_scr):
    rows = nb * steps
    i = pl.program_id(0)

    @pl.when(i == 0)
    def _():
        hr_scr[...] = h0r_ref[...]
        hi_scr[...] = h0i_ref[...]
        cv_scr[0:2 * nb, :] = cb0_ref[...]
        win_b[...] = win_ref[...].astype(BF16)
        wglu_b[...] = wglu_ref[...].astype(BF16)
        wout_b[...] = wout_ref[...].astype(BF16)

    if batch_major:
        slab, x_tm = tm_scr
        _rows_to_time_major(x_ref, slab, x_tm)
    else:
        x_tm = x_ref
    h = _per_seq(_rms(x_tm[...], n1g_ref[...]), nb, 1.0 + sc_ref[...], sh_ref[...])
    z_scr[...] = _bdot(h, win_b[...])

    u_cols = 3 * CONV_DIM
    for k in range(2):
        s_scr[:, 2 * k * HALF_STATE:2 * (k + 1) * HALF_STATE] = _bdot(
            z_scr[:, u_cols + k * HALF_SSM:u_cols + (k + 1) * HALF_SSM], wb_ref[k])

    def scan_cols(cg):
        cols = slice(cg * SCAN_COLS, (cg + 1) * SCAN_COLS)
        half, off = divmod(cg * SCAN_COLS, HALF_STATE)
        re_cols = slice(2 * half * HALF_STATE + off, 2 * half * HALF_STATE + off + SCAN_COLS)
        im_cols = slice(re_cols.start + HALF_STATE, re_cols.stop + HALF_STATE)
        a_r = jnp.broadcast_to(ar_ref[:, cols], (SUBLANES, SCAN_COLS))
        a_i = jnp.broadcast_to(ai_ref[:, cols], (SUBLANES, SCAN_COLS))

        def advance(hr, hi, rows_l):
            nr = a_r * hr - a_i * hi + s_scr[rows_l, re_cols]
            ni = a_r * hi + a_i * hr + s_scr[rows_l, im_cols]
            s_scr[rows_l, re_cols] = nr
            s_scr[rows_l, im_cols] = ni
            return nr, ni

        for r0 in range(0, nb, SUBLANES):
            tile = slice(r0, r0 + SUBLANES)
            hr, hi = hr_scr[tile, cols], hi_scr[tile, cols]
            for l in range(steps):
                hr, hi = advance(hr, hi, slice(l * nb + r0, l * nb + r0 + SUBLANES))
            hr_scr[tile, cols] = hr
            hi_scr[tile, cols] = hi

    y_halves = []
    for k in range(2):
        for cg in range(k * HALF_STATE // SCAN_COLS, (k + 1) * HALF_STATE // SCAN_COLS):
            scan_cols(cg)
        y_halves.append(
            _bdot(s_scr[:, 2 * k * HALF_STATE:(2 * k + 1) * HALF_STATE], wcr_ref[k])
            - _bdot(s_scr[:, (2 * k + 1) * HALF_STATE:(2 * k + 2) * HALF_STATE], wci_ref[k]))

    hro_ref[...] = hr_scr[...]
    hio_ref[...] = hi_scr[...]

    y = jnp.concatenate(y_halves, axis=-1) + dsk_ref[...] * z_scr[:, u_cols:IN_DIM]
    g = _gelu_tanh(y)
    y_ssm = _rms(g * _sigmoid(_bdot(g, wglu_b[...]) + bglu_ref[...]), gsm_ref[...])

    cv_scr[2 * nb:2 * nb + rows, :] = z_scr[:, CONV_DIM:2 * CONV_DIM] * z_scr[:, 2 * CONV_DIM:3 * CONV_DIM]
    conv = (cv_scr[0:rows, :] * cw_ref[0:1, :]
            + cv_scr[nb:nb + rows, :] * cw_ref[1:2, :]
            + cv_scr[2 * nb:2 * nb + rows, :] * cw_ref[2:3, :])
    y_conv = _rms(z_scr[:, 0:CONV_DIM] * conv, gcv_ref[...])
    carry = cv_scr[rows:rows + 2 * nb, :]
    cv_scr[0:2 * nb, :] = carry
    cbo_ref[...] = carry

    mixed = _bdot(y_conv, wout_b[0:CONV_DIM, :]) + _bdot(y_ssm, wout_b[CONV_DIM:2 * CONV_DIM, :])
    x1_ref[...] = x_tm[...] + _per_seq(mixed, nb, gt_ref[...])


def _const_spec(shape):
    return pl.BlockSpec(shape, lambda i: (0,) * len(shape), pipeline_mode=pl.Buffered(1))


def _mixer(x, nb, steps_total, chunk, mod, mod_row0, h0r, h0i, cb0, weights):
    rows = nb * chunk
    nchunks = steps_total // chunk
    batch_major = x.ndim == 3
    if batch_major:
        assert nb == SUBLANES
        x_spec = pl.BlockSpec((nb, chunk, D_MODEL), lambda i: (0, i, 0))
        tm_scratch = [pltpu.VMEM((D_MODEL // LANES, nb * (chunk + STRIDE_PAD), LANES), F32),
                      pltpu.VMEM((rows, D_MODEL), F32)]
    else:
        x_spec = pl.BlockSpec((rows, D_MODEL), lambda i: (i, 0))
        tm_scratch = []
    assert mod_row0 % nb == 0

    def mod_spec(k):
        return pl.BlockSpec((nb, D_MODEL), lambda i: (mod_row0 // nb, k), pipeline_mode=pl.Buffered(1))

    in_specs = [
        x_spec, mod_spec(0), mod_spec(1), mod_spec(2),
        _const_spec((nb, N_STATE)), _const_spec((nb, N_STATE)), _const_spec((2 * nb, CONV_DIM)),
    ] + [_const_spec(w.shape) for w in weights]
    out_specs = [
        pl.BlockSpec((rows, D_MODEL), lambda i: (i, 0)),
        _const_spec((2 * nb, CONV_DIM)), _const_spec((nb, N_STATE)), _const_spec((nb, N_STATE)),
    ]
    out_shape = [
        jax.ShapeDtypeStruct((steps_total * nb, D_MODEL), F32),
        jax.ShapeDtypeStruct((2 * nb, CONV_DIM), F32),
        jax.ShapeDtypeStruct((nb, N_STATE), F32),
        jax.ShapeDtypeStruct((nb, N_STATE), F32),
    ]
    scratch = [
        pltpu.VMEM((rows, IN_DIM), F32),
        pltpu.VMEM((rows + 2 * nb, CONV_DIM), F32),
        pltpu.VMEM((rows, 2 * N_STATE), F32),
        pltpu.VMEM((nb, N_STATE), F32),
        pltpu.VMEM((nb, N_STATE), F32),
        pltpu.VMEM((D_MODEL, IN_DIM), BF16),
        pltpu.VMEM((SSM_DIM, SSM_DIM), BF16),
        pltpu.VMEM((D_MODEL, D_MODEL), BF16),
    ] + tm_scratch
    return pl.pallas_call(
        functools.partial(_mixer_kernel, nb, chunk, batch_major),
        grid=(nchunks,),
        in_specs=in_specs, out_specs=out_specs, out_shape=out_shape,
        scratch_shapes=scratch,
        compiler_params=pltpu.CompilerParams(
            dimension_semantics=("arbitrary",), vmem_limit_bytes=VMEM_LIMIT),
        name="mixer",
    )(x, mod, mod, mod, h0r, h0i, cb0, *weights)


TOK_BLOCK = 512
SEG_ALIGN = 16
ROW_TILE = 512
CHUNKS_PER_TILE = ROW_TILE // SEG_ALIGN
SCATTER_GROUPS = 8
GATHER_BUFFERS = 3
TABLE_CHUNK_ROWS = 64
N_SLOTS = -(-(2 * TOK_BLOCK + N_EXPERTS * (SEG_ALIGN - 1)) // 256) * 256
SPARE_ROW = N_SLOTS - SEG_ALIGN
assert SPARE_ROW >= 2 * TOK_BLOCK + N_EXPERTS * (SEG_ALIGN - 1)
ROUTER_ROWS = 48
META_ROWS = SUBLANES

_NT = (((1,), (1,)), ((), ()))
_TN = (((0,), (0,)), ((), ()))


def _iota_f32(shape, dim):
    return lax.broadcasted_iota(jnp.int32, shape, dim).astype(F32)


def _split_bf16(v):
    hi = v.astype(BF16)
    return hi, (v - hi.astype(F32)).astype(BF16)


def _route_t(lg):
    row = _iota_f32(lg.shape, 0)
    neg = jnp.float32(-jnp.inf)
    none = float(ROUTER_ROWS)
    is_g = row < N_EXPERT_GROUPS
    gl = jnp.where(is_g, lg, neg)
    gmax = jnp.max(gl, axis=0, keepdims=True)
    gsum = jnp.sum(jnp.where(is_g, jnp.exp(gl - gmax), 0.0), axis=0, keepdims=True)
    g_prob = 1.0 / gsum
    g_idx = jnp.min(jnp.where(gl == gmax, row, none), axis=0, keepdims=True)
    lo = N_EXPERT_GROUPS + EXPERTS_PER_GROUP * g_idx
    el = jnp.where(row >= lo, jnp.where(row < lo + EXPERTS_PER_GROUP, lg, neg), neg)
    m1 = jnp.max(el, axis=0, keepdims=True)
    i1 = jnp.min(jnp.where(el == m1, row, none), axis=0, keepdims=True)
    el2 = jnp.where(row == i1, neg, el)
    m2 = jnp.max(el2, axis=0, keepdims=True)
    i2 = jnp.min(jnp.where(el2 == m2, row, none), axis=0, keepdims=True)
    e21 = jnp.exp(m2 - m1)
    return i1, i2, g_prob / (1.0 + e21), g_prob * e21 / (1.0 + e21)


def _dispatch_kernel(nbp, x1p_ref, x1s_ref, sc_ref, sh_ref, n2g_ref, wr_ref, br_ref,
                     xs_ref, meta_ref, cnt_ref):
    x1 = jnp.where(pl.program_id(0) < nbp, x1p_ref[...], x1s_ref[...])
    h = _rms(x1, n2g_ref[...]) * (1.0 + sc_ref[0]) + sh_ref[0]
    t = h.shape[0]
    hb, hl = _split_bf16(h)
    wh, wl = _split_bf16(wr_ref[...])
    lg = (lax.dot_general(wh, hb, _NT, preferred_element_type=F32)
          + lax.dot_general(wh, hl, _NT, preferred_element_type=F32)
          + lax.dot_general(wl, hb, _NT, preferred_element_type=F32)) + br_ref[...]
    i1, i2, w1, w2 = _route_t(lg)

    row = _iota_f32((ROUTER_ROWS, t), 0)
    hit1 = row == i1
    hit2 = row == i2
    onehot = jnp.where(hit1, 1.0, 0.0) + jnp.where(hit2, 1.0, 0.0)
    before = jnp.where(_iota_f32((t, t), 0) < _iota_f32((t, t), 1), 1.0, 0.0)
    rank = _bdot(onehot, before)
    cnt = jnp.sum(onehot, axis=1, keepdims=True)
    seg = jnp.floor((cnt + (SEG_ALIGN - 1)) * (1.0 / SEG_ALIGN)) * SEG_ALIGN
    earlier = jnp.where(_iota_f32((ROUTER_ROWS, ROUTER_ROWS), 1) < _iota_f32((ROUTER_ROWS, ROUTER_ROWS), 0),
                        1.0, 0.0)
    base = rank + _bdot(earlier, jnp.broadcast_to(seg, (ROUTER_ROWS, t)))
    slot1 = jnp.sum(jnp.where(hit1, base, 0.0), axis=0, keepdims=True)
    slot2 = jnp.sum(jnp.where(hit2, base, 0.0), axis=0, keepdims=True)
    j = _iota_f32((N_SLOTS, t), 0)
    place = jnp.where(j == slot1, 1.0, 0.0) + jnp.where(j == slot2, 1.0, 0.0)
    xs_ref[...] = jnp.dot(place.astype(BF16), hb, preferred_element_type=F32).astype(BF16)
    r = _iota_f32((META_ROWS, t), 0)
    meta_ref[0] = jnp.where(r == 0.0, slot1, jnp.where(r == 1.0, slot2,
                            jnp.where(r == 2.0, w1, jnp.where(r == 3.0, w2, 0.0))))
    cnt_ref[0] = jnp.broadcast_to(cnt, (ROUTER_ROWS, LANES))


def _combine_kernel(nbp, x1p_ref, x1s_ref, gt_ref, meta_ref, ys_ref, fg_ref, yp_ref, ysm_ref, slab):
    i = pl.program_id(0)
    x1 = jnp.where(i < nbp, x1p_ref[...], x1s_ref[...])
    meta = meta_ref[0]
    slot1, slot2, w1, w2 = (meta[k:k + 1, :] for k in range(4))
    j = _iota_f32((N_SLOTS, x1.shape[0]), 0)
    pick = jnp.where(j == slot1, w1, 0.0) + jnp.where(j == slot2, w2, 0.0)
    moe = lax.dot_general(pick.astype(BF16), ys_ref[...], _TN, preferred_element_type=F32)
    y = _rms(x1 + gt_ref[0] * moe, fg_ref[...])

    @pl.when(i < nbp)
    def _():
        nb, steps, d = yp_ref.shape
        for j in range(d // LANES):
            slab[j] = y[:, j * LANES:(j + 1) * LANES]
        for b in range(nb):
            for j in range(d // LANES):
                yp_ref[b, :, j * LANES:(j + 1) * LANES] = slab[j, pl.ds(b, steps, stride=nb), :]

    @pl.when(i >= nbp)
    def _():
        ysm_ref[...] = y


def _chunk_table_kernel(start_ref, m_ref, src_ref, out_ref):
    start = start_ref[...]
    end = start + m_ref[...]
    src = src_ref[...]

    def body(b, _):
        rows = pl.ds(pl.multiple_of(b * TABLE_CHUNK_ROWS, TABLE_CHUNK_ROWS), TABLE_CHUNK_ROWS)
        i = _iota_f32((TABLE_CHUNK_ROWS, start.shape[1]), 0) + (b * TABLE_CHUNK_ROWS).astype(F32)
        val = jnp.where(i >= start, jnp.where(i < end, src + SEG_ALIGN * (i - start), 0.0), 0.0)
        out_ref[rows, :] = jnp.sum(val, axis=1, keepdims=True).astype(jnp.int32)
        return 0

    lax.fori_loop(0, out_ref.shape[0] // TABLE_CHUNK_ROWS, body, 0)


def _expert_kernel(te_ref, tnv_ref, tc0_ref, ctab_ref, nt_ref, xs_hbm, wg_ref, wu_ref, wd_ref,
                   ys_hbm, xbuf, ybuf, wgub, wdb, in_sem, out_sem):
    t = pl.program_id(0)
    nt = nt_ref[0]
    slot = t % 2
    xslot = t % GATHER_BUFFERS

    def chunk_rows(j):
        return pl.ds(pl.multiple_of(j * SEG_ALIGN, SEG_ALIGN), SEG_ALIGN)

    def hbm_rows(ref, r):
        return ref.at[pl.ds(pl.multiple_of(r, SEG_ALIGN), SEG_ALIGN), :]

    def in_copy(r, j, s):
        return pltpu.make_async_copy(hbm_rows(xs_hbm, r), xbuf.at[s, chunk_rows(j), :], in_sem.at[s])

    def out_copy(r, j, s):
        return pltpu.make_async_copy(ybuf.at[s, chunk_rows(j), :], hbm_rows(ys_hbm, r), out_sem.at[s])

    def start_gather(tile, s):
        c0 = tc0_ref[tile]
        nv = tnv_ref[tile]
        for j in range(CHUNKS_PER_TILE):
            in_copy(jnp.where(j < nv, ctab_ref[c0 + j], SPARE_ROW), j, s).start()

    def wait_gather(s):
        pltpu.make_async_copy(xs_hbm.at[pl.ds(0, ROW_TILE), :], xbuf.at[s], in_sem.at[s]).wait()

    def wait_scatter(tile, s):
        nv = tnv_ref[tile]

        @pl.when(nv == CHUNKS_PER_TILE)
        def _():
            pltpu.make_async_copy(ybuf.at[s], ys_hbm.at[pl.ds(0, ROW_TILE), :], out_sem.at[s]).wait()

        @pl.when(nv < CHUNKS_PER_TILE)
        def _():
            def body(j, _):
                out_copy(0, j, s).wait()
                return 0

            lax.fori_loop(0, nv, body, 0)

    def run_tile(full):
        ahead = t + GATHER_BUFFERS - 1
        start_gather(ahead, ahead % GATHER_BUFFERS)
        wait_gather(xslot)
        ab = jnp.dot(xbuf[xslot], wgub[...], preferred_element_type=F32)
        a, b = ab[:, :D_EXPERT], ab[:, D_EXPERT:]
        o = jnp.dot((a * _sigmoid(a) * b).astype(BF16), wdb[...], preferred_element_type=F32)
        c0 = tc0_ref[t]
        if full:
            group = ROW_TILE // SCATTER_GROUPS
            for r in range(0, ROW_TILE, group):
                ybuf[slot, r:r + group, :] = o[r:r + group].astype(BF16)
                for j in range(r // SEG_ALIGN, (r + group) // SEG_ALIGN):
                    out_copy(ctab_ref[c0 + j], j, slot).start()
        else:
            ybuf[slot] = o.astype(BF16)

            def body(j, _):
                out_copy(ctab_ref[c0 + j], j, slot).start()
                return 0

            lax.fori_loop(0, tnv_ref[t], body, 0)

    @pl.when(t == 0)
    def _():
        for k in range(GATHER_BUFFERS - 1):
            start_gather(k, k)

    @pl.when(jnp.logical_or(t == 0, te_ref[t] != te_ref[jnp.maximum(t - 1, 0)]))
    def _():
        wgub[:, :D_EXPERT] = wg_ref[0].astype(BF16)
        wgub[:, D_EXPERT:] = wu_ref[0].astype(BF16)
        wdb[...] = wd_ref[0].astype(BF16)

    @pl.when(t < nt)
    def _():
        @pl.when(t >= 2)
        def _():
            wait_scatter(t - 2, slot)

        pl.when(tnv_ref[t] == CHUNKS_PER_TILE)(lambda: run_tile(True))
        pl.when(tnv_ref[t] < CHUNKS_PER_TILE)(lambda: run_tile(False))

        @pl.when(t == nt - 1)
        def _():
            for k in range(1, GATHER_BUFFERS):
                wait_gather((t + k) % GATHER_BUFFERS)
            wait_scatter(t, slot)

            @pl.when(t >= 1)
            def _():
                wait_scatter(t - 1, 1 - slot)


def _tile_plan(cnt):
    nblk = cnt.shape[0]
    max_chunks = nblk * (2 * TOK_BLOCK // SEG_ALIGN + N_EXPERTS)
    max_tiles = max_chunks // CHUNKS_PER_TILE + N_EXPERTS
    table_len = -(-(max_chunks + CHUNKS_PER_TILE) // LANES) * LANES
    m = (cnt + SEG_ALIGN - 1) // SEG_ALIGN
    seg_start = jnp.cumsum(m, axis=1) - m
    src0 = jnp.arange(nblk, dtype=jnp.int32)[:, None] * N_SLOTS + SEG_ALIGN * seg_start
    seg_m = m.T.reshape(1, -1)
    seg_first = jnp.cumsum(seg_m, axis=1) - seg_m
    assert table_len % TABLE_CHUNK_ROWS == 0

    def row(v):
        return jnp.pad(v.astype(F32), ((0, 0), (0, -v.shape[1] % LANES)))

    chunk_src = pl.pallas_call(
        _chunk_table_kernel,
        out_shape=jax.ShapeDtypeStruct((table_len, 1), jnp.int32),
        name="moe_chunk_table",
    )(row(seg_first), row(seg_m), row(src0.T.reshape(1, -1))).reshape(-1)
    ce = jnp.sum(m, axis=0)
    c_start = jnp.cumsum(ce) - ce
    tiles = (ce + CHUNKS_PER_TILE - 1) // CHUNKS_PER_TILE
    t_cum = jnp.cumsum(tiles)
    nt = t_cum[-1]
    k = jnp.arange(max_tiles + GATHER_BUFFERS - 1, dtype=jnp.int32)
    kk = jnp.minimum(k, nt - 1)
    ek = jnp.minimum(jnp.sum((t_cum[None, :] <= kk[:, None]).astype(jnp.int32), axis=1), N_EXPERTS - 1)
    is_e = ek[:, None] == jnp.arange(N_EXPERTS, dtype=jnp.int32)[None, :]
    pick = lambda v: jnp.sum(jnp.where(is_e, v[None, :], 0), axis=1)
    local = kk - pick(t_cum - tiles)
    tile_nv = jnp.where(k < nt, jnp.clip(pick(ce) - CHUNKS_PER_TILE * local, 0, CHUNKS_PER_TILE), 0)
    tile_c0 = jnp.where(k < nt, pick(c_start) + CHUNKS_PER_TILE * local, 0)
    return tuple(v.astype(jnp.int32) for v in (ek, tile_nv, tile_c0, chunk_src, nt.reshape(1)))


def _moe(x1p, x1s, bp, sc, sh, gt, n2g, wr, br, wg, wu, wd, fg):
    nbp = x1p.shape[0] // TOK_BLOCK
    nblk = nbp + 1
    blk = (TOK_BLOCK, D_MODEL)
    p_spec = pl.BlockSpec(blk, lambda i: (jnp.minimum(i, nbp - 1), 0))
    s_spec = pl.BlockSpec(blk, lambda i: (0, 0))
    mod_spec = pl.BlockSpec((1,) + blk, lambda i: (jnp.minimum(i // nbp, 1), 0, 0))
    meta_spec = pl.BlockSpec((1, META_ROWS, TOK_BLOCK), lambda i: (i, 0, 0))
    slots_spec = pl.BlockSpec((N_SLOTS, D_MODEL), lambda i: (i, 0))
    params = pltpu.CompilerParams(dimension_semantics=("arbitrary",), vmem_limit_bytes=VMEM_LIMIT)

    def cs(shape):
        return pl.BlockSpec(shape, lambda i: (0,) * len(shape))

    xs, meta, cnt = pl.pallas_call(
        functools.partial(_dispatch_kernel, nbp),
        grid=(nblk,),
        in_specs=[p_spec, s_spec, mod_spec, mod_spec, cs((1, D_MODEL)),
                  cs((ROUTER_ROWS, D_MODEL)), cs((ROUTER_ROWS, TOK_BLOCK))],
        out_specs=[slots_spec, meta_spec,
                   pl.BlockSpec((1, ROUTER_ROWS, LANES), lambda i: (i, 0, 0))],
        out_shape=[jax.ShapeDtypeStruct((nblk * N_SLOTS, D_MODEL), BF16),
                   jax.ShapeDtypeStruct((nblk, META_ROWS, TOK_BLOCK), F32),
                   jax.ShapeDtypeStruct((nblk, ROUTER_ROWS, LANES), F32)],
        compiler_params=params,
        name="moe_dispatch",
    )(x1p, x1s, sc, sh, n2g, wr, br)

    plan = _tile_plan(cnt[:, N_EXPERT_GROUPS:N_EXPERT_GROUPS + N_EXPERTS, 0].astype(jnp.int32))
    max_tiles = plan[0].shape[0] - (GATHER_BUFFERS - 1)

    def w_spec(shape):
        return pl.BlockSpec((1,) + shape, lambda t, te, *_: (te[t], 0, 0))

    ys = pl.pallas_call(
        _expert_kernel,
        grid_spec=pltpu.PrefetchScalarGridSpec(
            num_scalar_prefetch=len(plan),
            grid=(max_tiles,),
            in_specs=[pl.BlockSpec(memory_space=pl.ANY),
                      w_spec((D_MODEL, D_EXPERT)), w_spec((D_MODEL, D_EXPERT)), w_spec((D_EXPERT, D_MODEL))],
            out_specs=pl.BlockSpec(memory_space=pl.ANY),
            scratch_shapes=[pltpu.VMEM((GATHER_BUFFERS, ROW_TILE, D_MODEL), BF16),
                            pltpu.VMEM((2, ROW_TILE, D_MODEL), BF16),
                            pltpu.VMEM((D_MODEL, 2 * D_EXPERT), BF16),
                            pltpu.VMEM((D_EXPERT, D_MODEL), BF16),
                            pltpu.SemaphoreType.DMA((GATHER_BUFFERS,)), pltpu.SemaphoreType.DMA((2,))]),
        out_shape=jax.ShapeDtypeStruct(xs.shape, BF16),
        input_output_aliases={len(plan): 0},
        compiler_params=params,
        name="moe_experts",
    )(*plan, xs, wg, wu, wd)

    return pl.pallas_call(
        functools.partial(_combine_kernel, nbp),
        grid=(nblk,),
        in_specs=[p_spec, s_spec, mod_spec, meta_spec, slots_spec, cs((1, D_MODEL))],
        out_specs=[pl.BlockSpec((bp, TOK_BLOCK // bp, D_MODEL), lambda i: (0, jnp.minimum(i, nbp - 1), 0)),
                   s_spec],
        out_shape=[jax.ShapeDtypeStruct((bp, x1p.shape[0] // bp, D_MODEL), F32),
                   jax.ShapeDtypeStruct(x1s.shape, F32)],
        scratch_shapes=[pltpu.VMEM((D_MODEL // LANES, TOK_BLOCK, LANES), F32)],
        compiler_params=params,
        name="moe_combine",
    )(x1p, x1s, gt, meta, ys, fg)


def _diag_halves_in(bbt):
    eye = jnp.eye(SSM_GROUPS // 2, dtype=bbt.dtype)
    blocks = bbt.reshape(SSM_GROUP, 2, SSM_GROUPS // 2, SSM_STATE).transpose(1, 0, 2, 3)
    w = eye[None, :, None, :, None] * blocks[:, None, :, :, :]
    return w.reshape(2, HALF_SSM, HALF_STATE)


def _diag_halves_out(c):
    eye = jnp.eye(SSM_GROUPS // 2, dtype=c.dtype)
    blocks = c.reshape(2, SSM_GROUPS // 2, SSM_GROUP, SSM_STATE).transpose(0, 1, 3, 2)
    w = eye[None, :, None, :, None] * blocks[:, :, :, None, :]
    return w.reshape(2, HALF_STATE, HALF_SSM)


def _to_time_major(x):
    b, s, d = x.shape
    return x.transpose(1, 0, 2).reshape(s * b, d)


def _from_time_major(x, b, s):
    return x.reshape(s, b, -1).transpose(1, 0, 2)


def kernel(x_prompt, x_sample, state_conv, state_ssm_re, state_ssm_im, c_prompt, c_sample, w_ada, b_ada, norm1_g, norm2_g, w_in, conv_w, lambda_re, lambda_im, log_dt, ssm_b_re, ssm_b_im, ssm_c_re, ssm_c_im, ssm_d, w_glu, b_glu, out_norm_conv_g, out_norm_ssm_g, w_out, w_router_group, b_router_group, w_router_expert, b_router_expert, w_expert_gate, w_expert_up, w_expert_down, final_norm_g):
    assert w_ada.shape[0] == 1, "single-layer step"
    bp, sp, _ = x_prompt.shape
    bs, ss, _ = x_sample.shape

    ar, ai, bbr, bbi = _discretise(lambda_re[0], lambda_im[0], log_dt[0], ssm_b_re[0], ssm_b_im[0])
    w_b = jnp.concatenate([_diag_halves_in(bbr), _diag_halves_in(bbi)], axis=2).astype(BF16)
    w_cr = _diag_halves_out(ssm_c_re[0]).astype(BF16)
    w_ci = _diag_halves_out(ssm_c_im[0]).astype(BF16)

    assert bs % bp == 0
    mod = _adaln(jnp.concatenate([c_sample, c_prompt], axis=0), w_ada[0], b_ada[0])
    sh2, sc2, gt2 = [mod[:, k * D_MODEL:(k + 1) * D_MODEL] for k in range(N_MOD // 2, N_MOD)]

    mixer_weights = (
        norm1_g[0].reshape(1, -1), w_in[0], conv_w[0], ar, ai, w_b, w_cr, w_ci,
        ssm_d[0].reshape(1, -1), w_glu[0], b_glu[0].reshape(1, -1),
        out_norm_conv_g[0].reshape(1, -1), out_norm_ssm_g[0].reshape(1, -1), w_out[0],
    )

    pad_rows = ROUTER_ROWS - N_EXPERT_GROUPS - N_EXPERTS
    w_router = jnp.concatenate(
        [w_router_group[0].T, w_router_expert[0].transpose(0, 2, 1).reshape(N_EXPERTS, D_MODEL),
         jnp.zeros((pad_rows, D_MODEL), F32)], axis=0)
    b_router = jnp.concatenate(
        [b_router_group[0], b_router_expert[0].reshape(-1), jnp.zeros((pad_rows,), F32)])
    b_router = jnp.broadcast_to(b_router[:, None], (ROUTER_ROWS, TOK_BLOCK))
    moe_weights = (norm2_g[0].reshape(1, -1), w_router, b_router,
                   w_expert_gate[0], w_expert_up[0], w_expert_down[0], final_norm_g.reshape(1, -1))

    def mix(x, mod_row0, h0r, h0i, cb0, chunk, in_kernel_transpose):
        nb, s, _ = x.shape
        x1, cbo, hro, hio = _mixer(
            x if in_kernel_transpose else _to_time_major(x), nb, s, chunk,
            mod, mod_row0, h0r, h0i, cb0, mixer_weights)
        new_conv = cbo.reshape(2, nb, CONV_DIM).transpose(1, 0, 2)[None]
        new_re = hro.reshape(1, nb, SSM_GROUPS, SSM_STATE)
        new_im = hio.reshape(1, nb, SSM_GROUPS, SSM_STATE)
        return x1, new_conv, new_re, new_im

    zero_state = jnp.zeros((bp, N_STATE), F32)
    zero_conv = jnp.zeros((2 * bp, CONV_DIM), F32)
    x1p, conv_p, re_p, im_p = mix(x_prompt, bs, zero_state, zero_state, zero_conv, TOK_BLOCK // bp, True)
    x1s, conv_s, re_s, im_s = mix(
        x_sample, 0,
        state_ssm_re[0].reshape(bs, N_STATE), state_ssm_im[0].reshape(bs, N_STATE),
        state_conv[0].transpose(1, 0, 2).reshape(2 * bs, CONV_DIM), ss, False)

    assert bs * ss == TOK_BLOCK and (bp * sp) % TOK_BLOCK == 0 and TOK_BLOCK % bp == 0

    def block_rows(m):
        return jnp.stack([jnp.tile(m[bs:], (TOK_BLOCK // bp, 1)), jnp.tile(m[:bs], (TOK_BLOCK // bs, 1))])

    y_p, y_s = _moe(x1p, x1s, bp, block_rows(sc2), block_rows(sh2), block_rows(gt2), *moe_weights)
    return (y_p, _from_time_major(y_s, bs, ss), conv_p, re_p, im_p, conv_s, re_s, im_s)
```

```python
import functools

import jax
import jax.numpy as jnp
from jax import lax
from jax.experimental import pallas as pl
from jax.experimental.pallas import tpu as pltpu

D_MODEL = 1024
CONV_DIM = 512
SSM_DIM = 512
SSM_GROUP = 16
SSM_GROUPS = 32
SSM_STATE = 64
N_STATE = SSM_GROUPS * SSM_STATE
HALF_SSM = SSM_DIM // 2
HALF_STATE = N_STATE // 2
IN_DIM = 2048
N_EXPERT_GROUPS = 4
EXPERTS_PER_GROUP = 8
N_EXPERTS = 32
D_EXPERT = 256
N_MOD = 6
EPS = 1e-6

SUBLANES = 8
LANES = 128
SCAN_COLS = 512
STRIDE_PAD = 8
VMEM_LIMIT = 60 * 1024 * 1024

F32 = jnp.float32
BF16 = jnp.bfloat16


def _sigmoid(x):
    return 1.0 / (1.0 + jnp.exp(-x))


def _gelu_tanh(x):
    return 0.5 * x * (1.0 + jnp.tanh(0.7978845608028654 * (x + 0.044715 * (x * x * x))))


def _rms(x, g):
    return x * lax.rsqrt(jnp.mean(x * x, axis=-1, keepdims=True) + EPS) * g


def _bdot(a, b):
    return jnp.dot(a.astype(BF16), b.astype(BF16), preferred_element_type=F32)


def _per_seq(v, nb, scale, shift=None):
    rows, d = v.shape
    v3 = v.reshape(rows // nb, nb, d) * scale[None]
    if shift is not None:
        v3 = v3 + shift[None]
    return v3.reshape(rows, d)


def _disc_kernel(lr_ref, li_ref, ldt_ref, brt_ref, bit_ref, ar_ref, ai_ref, bbr_ref, bbi_ref):
    lr = lr_ref[...]
    li = li_ref[...]
    dt = jnp.exp(ldt_ref[...])
    mag = jnp.exp(lr * dt)
    ar = mag * jnp.cos(li * dt)
    ai = mag * jnp.sin(li * dt)
    den = lr * lr + li * li
    fr = ((ar - 1.0) * lr + ai * li) / den
    fi = (ai * lr - (ar - 1.0) * li) / den
    ar_ref[...] = ar
    ai_ref[...] = ai
    brt = brt_ref[...]
    bit = bit_ref[...]
    bbr_ref[...] = fr * brt - fi * bit
    bbi_ref[...] = fr * bit + fi * brt


def _discretise(lambda_re, lambda_im, log_dt, b_re, b_im):
    lr = lambda_re.reshape(1, N_STATE)
    li = lambda_im.reshape(1, N_STATE)
    ldt = jnp.repeat(log_dt, SSM_STATE).reshape(1, N_STATE)
    brt = b_re.transpose(2, 0, 1).reshape(SSM_GROUP, N_STATE)
    bit = b_im.transpose(2, 0, 1).reshape(SSM_GROUP, N_STATE)
    row = jax.ShapeDtypeStruct((1, N_STATE), F32)
    mat = jax.ShapeDtypeStruct((SSM_GROUP, N_STATE), F32)
    return pl.pallas_call(_disc_kernel, out_shape=(row, row, mat, mat), name="s5_discretise")(
        lr, li, ldt, brt, bit)


def _ada_kernel(n_first, block_rows, c_ref, w_ref, b_ref, o_ref, rows_ref):
    c = c_ref[...]
    o = _bdot(c * _sigmoid(c), w_ref[...]) + b_ref[...]
    o_ref[...] = o

    @pl.when(pl.program_id(0) >= N_MOD // 2)
    def _():
        n_rest = o.shape[0] - n_first
        rows_ref[0, 0] = jnp.tile(o[n_first:], (block_rows // n_rest, 1))
        rows_ref[0, 1] = jnp.tile(o[:n_first], (block_rows // n_first, 1))


def _adaln(c_all, n_first, block_rows, w_ada, b_ada):
    n = c_all.shape[0]
    half = N_MOD // 2
    return pl.pallas_call(
        functools.partial(_ada_kernel, n_first, block_rows),
        grid=(N_MOD,),
        in_specs=[
            pl.BlockSpec((n, D_MODEL), lambda j: (0, 0)),
            pl.BlockSpec((D_MODEL, D_MODEL), lambda j: (0, j)),
            pl.BlockSpec((1, D_MODEL), lambda j: (0, j)),
        ],
        out_specs=[pl.BlockSpec((n, D_MODEL), lambda j: (0, j)),
                   pl.BlockSpec((1, 2, block_rows, D_MODEL), lambda j: (jnp.maximum(j - half, 0), 0, 0, 0))],
        out_shape=[jax.ShapeDtypeStruct((n, N_MOD * D_MODEL), F32),
                   jax.ShapeDtypeStruct((half, 2, block_rows, D_MODEL), F32)],
        compiler_params=pltpu.CompilerParams(
            dimension_semantics=("arbitrary",), vmem_limit_bytes=VMEM_LIMIT),
        name="adaln",
    )(c_all, w_ada, b_ada.reshape(1, -1))


def _rows_to_time_major(x_ref, slab, xt_scr):
    nb, steps, d = x_ref.shape
    pitch = slab.shape[1] // nb
    for b in range(nb):
        for j in range(d // LANES):
            slab[j, b * pitch:b * pitch + steps, :] = x_ref[b, :, j * LANES:(j + 1) * LANES]
    for l in range(steps):
        for j in range(d // LANES):
            xt_scr[l * nb:(l + 1) * nb, j * LANES:(j + 1) * LANES] = slab[j, pl.ds(l, nb, stride=pitch), :]


def _mixer_kernel(nb, steps, batch_major,
                  x_ref, sh_ref, sc_ref, gt_ref, h0r_ref, h0i_ref, cb0_ref,
                  n1g_ref, win_ref, cw_ref, ar_ref, ai_ref, wb_ref, wcr_ref, wci_ref,
                  dsk_ref, wglu_ref, bglu_ref, gcv_ref, gsm_ref, wout_ref,
                  x1_ref, cbo_ref, hro_ref, hio_ref,
                  z_scr, cv_scr, s_scr, hr_scr, hi_scr, win_b, wglu_b, wout_b, *tm_scr):
    rows = nb * steps
    i = pl.program_id(0)

    @pl.when(i == 0)
    def _():
        hr_scr[...] = h0r_ref[...]
        hi_scr[...] = h0i_ref[...]
        cv_scr[0:2 * nb, :] = cb0_ref[...]
        win_b[...] = win_ref[...].astype(BF16)
        wglu_b[...] = wglu_ref[...].astype(BF16)
        wout_b[...] = wout_ref[...].astype(BF16)

    if batch_major:
        slab, x_tm = tm_scr
        _rows_to_time_major(x_ref, slab, x_tm)
    else:
        x_tm = x_ref
    h = _per_seq(_rms(x_tm[...], n1g_ref[...]), nb, 1.0 + sc_ref[...], sh_ref[...])
    z_scr[...] = _bdot(h, win_b[...])

    u_cols = 3 * CONV_DIM
    for k in range(2):
        s_scr[:, 2 * k * HALF_STATE:2 * (k + 1) * HALF_STATE] = _bdot(
            z_scr[:, u_cols + k * HALF_SSM:u_cols + (k + 1) * HALF_SSM], wb_ref[k])

    def scan_cols(cg):
        cols = slice(cg * SCAN_COLS, (cg + 1) * SCAN_COLS)
        half, off = divmod(cg * SCAN_COLS, HALF_STATE)
        re_cols = slice(2 * half * HALF_STATE + off, 2 * half * HALF_STATE + off + SCAN_COLS)
        im_cols = slice(re_cols.start + HALF_STATE, re_cols.stop + HALF_STATE)
        a_r = jnp.broadcast_to(ar_ref[:, cols], (SUBLANES, SCAN_COLS))
        a_i = jnp.broadcast_to(ai_ref[:, cols], (SUBLANES, SCAN_COLS))

        def advance(hr, hi, rows_l):
            nr = a_r * hr - a_i * hi + s_scr[rows_l, re_cols]
            ni = a_r * hi + a_i * hr + s_scr[rows_l, im_cols]
            s_scr[rows_l, re_cols] = nr
            s_scr[rows_l, im_cols] = ni
            return nr, ni

        for r0 in range(0, nb, SUBLANES):
            tile = slice(r0, r0 + SUBLANES)
            hr, hi = hr_scr[tile, cols], hi_scr[tile, cols]
            for l in range(steps):
                hr, hi = advance(hr, hi, slice(l * nb + r0, l * nb + r0 + SUBLANES))
            hr_scr[tile, cols] = hr
            hi_scr[tile, cols] = hi

    y_halves = []
    for k in range(2):
        for cg in range(k * HALF_STATE // SCAN_COLS, (k + 1) * HALF_STATE // SCAN_COLS):
            scan_cols(cg)
        y_halves.append(
            _bdot(s_scr[:, 2 * k * HALF_STATE:(2 * k + 1) * HALF_STATE], wcr_ref[k])
            - _bdot(s_scr[:, (2 * k + 1) * HALF_STATE:(2 * k + 2) * HALF_STATE], wci_ref[k]))

    hro_ref[...] = hr_scr[...]
    hio_ref[...] = hi_scr[...]

    y = jnp.concatenate(y_halves, axis=-1) + dsk_ref[...] * z_scr[:, u_cols:IN_DIM]
    g = _gelu_tanh(y)
    y_ssm = _rms(g * _sigmoid(_bdot(g, wglu_b[...]) + bglu_ref[...]), gsm_ref[...])

    cv_scr[2 * nb:2 * nb + rows, :] = z_scr[:, CONV_DIM:2 * CONV_DIM] * z_scr[:, 2 * CONV_DIM:3 * CONV_DIM]
    conv = (cv_scr[0:rows, :] * cw_ref[0:1, :]
            + cv_scr[nb:nb + rows, :] * cw_ref[1:2, :]
            + cv_scr[2 * nb:2 * nb + rows, :] * cw_ref[2:3, :])
    y_conv = _rms(z_scr[:, 0:CONV_DIM] * conv, gcv_ref[...])
    carry = cv_scr[rows:rows + 2 * nb, :]
    cv_scr[0:2 * nb, :] = carry
    cbo_ref[...] = carry

    mixed = _bdot(y_conv, wout_b[0:CONV_DIM, :]) + _bdot(y_ssm, wout_b[CONV_DIM:2 * CONV_DIM, :])
    x1_ref[...] = x_tm[...] + _per_seq(mixed, nb, gt_ref[...])


def _const_spec(shape):
    return pl.BlockSpec(shape, lambda i: (0,) * len(shape), pipeline_mode=pl.Buffered(1))


def _mixer(x, nb, steps_total, chunk, mod, mod_row0, h0r, h0i, cb0, weights):
    rows = nb * chunk
    nchunks = steps_total // chunk
    batch_major = x.ndim == 3
    if batch_major:
        assert nb == SUBLANES
        x_spec = pl.BlockSpec((nb, chunk, D_MODEL), lambda i: (0, i, 0))
        tm_scratch = [pltpu.VMEM((D_MODEL // LANES, nb * (chunk + STRIDE_PAD), LANES), F32),
                      pltpu.VMEM((rows, D_MODEL), F32)]
    else:
        x_spec = pl.BlockSpec((rows, D_MODEL), lambda i: (i, 0))
        tm_scratch = []
    assert mod_row0 % nb == 0

    def mod_spec(k):
        return pl.BlockSpec((nb, D_MODEL), lambda i: (mod_row0 // nb, k), pipeline_mode=pl.Buffered(1))

    in_specs = [
        x_spec, mod_spec(0), mod_spec(1), mod_spec(2),
        _const_spec((nb, N_STATE)), _const_spec((nb, N_STATE)), _const_spec((2 * nb, CONV_DIM)),
    ] + [_const_spec(w.shape) for w in weights]
    out_specs = [
        pl.BlockSpec((rows, D_MODEL), lambda i: (i, 0)),
        _const_spec((2 * nb, CONV_DIM)), _const_spec((nb, N_STATE)), _const_spec((nb, N_STATE)),
    ]
    out_shape = [
        jax.ShapeDtypeStruct((steps_total * nb, D_MODEL), F32),
        jax.ShapeDtypeStruct((2 * nb, CONV_DIM), F32),
        jax.ShapeDtypeStruct((nb, N_STATE), F32),
        jax.ShapeDtypeStruct((nb, N_STATE), F32),
    ]
    scratch = [
        pltpu.VMEM((rows, IN_DIM), F32),
        pltpu.VMEM((rows + 2 * nb, CONV_DIM), F32),
        pltpu.VMEM((rows, 2 * N_STATE), F32),
        pltpu.VMEM((nb, N_STATE), F32),
        pltpu.VMEM((nb, N_STATE), F32),
        pltpu.VMEM((D_MODEL, IN_DIM), BF16),
        pltpu.VMEM((SSM_DIM, SSM_DIM), BF16),
        pltpu.VMEM((D_MODEL, D_MODEL), BF16),
    ] + tm_scratch
    return pl.pallas_call(
        functools.partial(_mixer_kernel, nb, chunk, batch_major),
        grid=(nchunks,),
        in_specs=in_specs, out_specs=out_specs, out_shape=out_shape,
        scratch_shapes=scratch,
        compiler_params=pltpu.CompilerParams(
            dimension_semantics=("arbitrary",), vmem_limit_bytes=VMEM_LIMIT),
        name="mixer",
    )(x, mod, mod, mod, h0r, h0i, cb0, *weights)


TOK_BLOCK = 512
SEG_ALIGN = 16
ROW_TILE = 512
CHUNKS_PER_TILE = ROW_TILE // SEG_ALIGN
SCATTER_GROUPS = 8
GATHER_BUFFERS = 3
TABLE_CHUNK_ROWS = 64
N_SLOTS = -(-(2 * TOK_BLOCK + N_EXPERTS * (SEG_ALIGN - 1)) // 256) * 256
SPARE_ROW = N_SLOTS - SEG_ALIGN
assert SPARE_ROW >= 2 * TOK_BLOCK + N_EXPERTS * (SEG_ALIGN - 1)
ROUTER_ROWS = 48
META_ROWS = SUBLANES

_NT = (((1,), (1,)), ((), ()))
_TN = (((0,), (0,)), ((), ()))


def _iota_f32(shape, dim):
    return lax.broadcasted_iota(jnp.int32, shape, dim).astype(F32)


def _split_bf16(v):
    hi = v.astype(BF16)
    return hi, (v - hi.astype(F32)).astype(BF16)


def _route_t(lg):
    row = _iota_f32(lg.shape, 0)
    neg = jnp.float32(-jnp.inf)
    none = float(ROUTER_ROWS)
    is_g = row < N_EXPERT_GROUPS
    gl = jnp.where(is_g, lg, neg)
    gmax = jnp.max(gl, axis=0, keepdims=True)
    gsum = jnp.sum(jnp.where(is_g, jnp.exp(gl - gmax), 0.0), axis=0, keepdims=True)
    g_prob = 1.0 / gsum
    g_idx = jnp.min(jnp.where(gl == gmax, row, none), axis=0, keepdims=True)
    lo = N_EXPERT_GROUPS + EXPERTS_PER_GROUP * g_idx
    el = jnp.where(row >= lo, jnp.where(row < lo + EXPERTS_PER_GROUP, lg, neg), neg)
    m1 = jnp.max(el, axis=0, keepdims=True)
    i1 = jnp.min(jnp.where(el == m1, row, none), axis=0, keepdims=True)
    el2 = jnp.where(row == i1, neg, el)
    m2 = jnp.max(el2, axis=0, keepdims=True)
    i2 = jnp.min(jnp.where(el2 == m2, row, none), axis=0, keepdims=True)
    e21 = jnp.exp(m2 - m1)
    return i1, i2, g_prob / (1.0 + e21), g_prob * e21 / (1.0 + e21)


def _dispatch_kernel(nbp, x1p_ref, x1s_ref, sc_ref, sh_ref, n2g_ref, wr_ref, br_ref,
                     xs_ref, meta_ref, cnt_ref):
    x1 = jnp.where(pl.program_id(0) < nbp, x1p_ref[...], x1s_ref[...])
    h = _rms(x1, n2g_ref[...]) * (1.0 + sc_ref[0, 0]) + sh_ref[0, 0]
    t = h.shape[0]
    hb, hl = _split_bf16(h)
    wh, wl = _split_bf16(wr_ref[...])
    lg = (lax.dot_general(wh, hb, _NT, preferred_element_type=F32)
          + lax.dot_general(wh, hl, _NT, preferred_element_type=F32)
          + lax.dot_general(wl, hb, _NT, preferred_element_type=F32)) + br_ref[...]
    i1, i2, w1, w2 = _route_t(lg)

    row = _iota_f32((ROUTER_ROWS, t), 0)
    hit1 = row == i1
    hit2 = row == i2
    onehot = jnp.where(hit1, 1.0, 0.0) + jnp.where(hit2, 1.0, 0.0)
    before = jnp.where(_iota_f32((t, t), 0) < _iota_f32((t, t), 1), 1.0, 0.0)
    rank = _bdot(onehot, before)
    cnt = jnp.sum(onehot, axis=1, keepdims=True)
    seg = jnp.floor((cnt + (SEG_ALIGN - 1)) * (1.0 / SEG_ALIGN)) * SEG_ALIGN
    earlier = jnp.where(_iota_f32((ROUTER_ROWS, ROUTER_ROWS), 1) < _iota_f32((ROUTER_ROWS, ROUTER_ROWS), 0),
                        1.0, 0.0)
    base = rank + _bdot(earlier, jnp.broadcast_to(seg, (ROUTER_ROWS, t)))
    slot1 = jnp.sum(jnp.where(hit1, base, 0.0), axis=0, keepdims=True)
    slot2 = jnp.sum(jnp.where(hit2, base, 0.0), axis=0, keepdims=True)
    j = _iota_f32((N_SLOTS, t), 0)
    place = jnp.where(j == slot1, 1.0, 0.0) + jnp.where(j == slot2, 1.0, 0.0)
    xs_ref[...] = jnp.dot(place.astype(BF16), hb, preferred_element_type=F32).astype(BF16)
    r = _iota_f32((META_ROWS, t), 0)
    meta_ref[0] = jnp.where(r == 0.0, slot1, jnp.where(r == 1.0, slot2,
                            jnp.where(r == 2.0, w1, jnp.where(r == 3.0, w2, 0.0))))
    cnt_ref[0] = jnp.broadcast_to(cnt, (ROUTER_ROWS, LANES))


def _combine_kernel(nbp, x1p_ref, x1s_ref, gt_ref, meta_ref, ys_ref, fg_ref, yp_ref, ysm_ref, slab):
    i = pl.program_id(0)
    x1 = jnp.where(i < nbp, x1p_ref[...], x1s_ref[...])
    meta = meta_ref[0]
    slot1, slot2, w1, w2 = (meta[k:k + 1, :] for k in range(4))
    j = _iota_f32((N_SLOTS, x1.shape[0]), 0)
    pick = jnp.where(j == slot1, w1, 0.0) + jnp.where(j == slot2, w2, 0.0)
    moe = lax.dot_general(pick.astype(BF16), ys_ref[...], _TN, preferred_element_type=F32)
    y = _rms(x1 + gt_ref[0, 0] * moe, fg_ref[...])

    @pl.when(i < nbp)
    def _():
        nb, steps, d = yp_ref.shape
        for j in range(d // LANES):
            slab[j] = y[:, j * LANES:(j + 1) * LANES]
        for b in range(nb):
            for j in range(d // LANES):
                yp_ref[b, :, j * LANES:(j + 1) * LANES] = slab[j, pl.ds(b, steps, stride=nb), :]

    @pl.when(i >= nbp)
    def _():
        ysm_ref[...] = y


def _chunk_table_kernel(start_ref, m_ref, src_ref, out_ref):
    start = start_ref[...]
    end = start + m_ref[...]
    src = src_ref[...]

    def body(b, _):
        rows = pl.ds(pl.multiple_of(b * TABLE_CHUNK_ROWS, TABLE_CHUNK_ROWS), TABLE_CHUNK_ROWS)
        first = lax.convert_element_type(b * TABLE_CHUNK_ROWS, F32)
        i = _iota_f32((TABLE_CHUNK_ROWS, start.shape[1]), 0) + first
        val = jnp.where(i >= start, jnp.where(i < end, src + SEG_ALIGN * (i - start), 0.0), 0.0)
        out_ref[rows, :] = jnp.sum(val, axis=1, keepdims=True).astype(jnp.int32)
        return 0

    lax.fori_loop(0, out_ref.shape[0] // TABLE_CHUNK_ROWS, body, 0)


def _expert_kernel(te_ref, tnv_ref, tc0_ref, ctab_ref, nt_ref, xs_hbm, wg_ref, wu_ref, wd_ref,
                   ys_hbm, xbuf, ybuf, wgub, wdb, in_sem, out_sem):
    t = pl.program_id(0)
    nt = nt_ref[0]
    slot = t % 2
    xslot = t % GATHER_BUFFERS

    def chunk_rows(j):
        return pl.ds(pl.multiple_of(j * SEG_ALIGN, SEG_ALIGN), SEG_ALIGN)

    def hbm_rows(ref, r):
        return ref.at[pl.ds(pl.multiple_of(r, SEG_ALIGN), SEG_ALIGN), :]

    def in_copy(r, j, s):
        return pltpu.make_async_copy(hbm_rows(xs_hbm, r), xbuf.at[s, chunk_rows(j), :], in_sem.at[s])

    def out_copy(r, j, s):
        return pltpu.make_async_copy(ybuf.at[s, chunk_rows(j), :], hbm_rows(ys_hbm, r), out_sem.at[s])

    def start_gather(tile, s):
        c0 = tc0_ref[tile]
        nv = tnv_ref[tile]
        for j in range(CHUNKS_PER_TILE):
            in_copy(jnp.where(j < nv, ctab_ref[c0 + j], SPARE_ROW), j, s).start()

    def wait_gather(s):
        pltpu.make_async_copy(xs_hbm.at[pl.ds(0, ROW_TILE), :], xbuf.at[s], in_sem.at[s]).wait()

    def wait_scatter(tile, s):
        nv = tnv_ref[tile]

        @pl.when(nv == CHUNKS_PER_TILE)
        def _():
            pltpu.make_async_copy(ybuf.at[s], ys_hbm.at[pl.ds(0, ROW_TILE), :], out_sem.at[s]).wait()

        @pl.when(nv < CHUNKS_PER_TILE)
        def _():
            def body(j, _):
                out_copy(0, j, s).wait()
                return 0

            lax.fori_loop(0, nv, body, 0)

    def run_tile(full):
        ahead = t + GATHER_BUFFERS - 1
        start_gather(ahead, ahead % GATHER_BUFFERS)
        wait_gather(xslot)
        ab = jnp.dot(xbuf[xslot], wgub[...], preferred_element_type=F32)
        a, b = ab[:, :D_EXPERT], ab[:, D_EXPERT:]
        o = jnp.dot((a * _sigmoid(a) * b).astype(BF16), wdb[...], preferred_element_type=F32)
        c0 = tc0_ref[t]
        if full:
            group = ROW_TILE // SCATTER_GROUPS
            for r in range(0, ROW_TILE, group):
                ybuf[slot, r:r + group, :] = o[r:r + group].astype(BF16)
                for j in range(r // SEG_ALIGN, (r + group) // SEG_ALIGN):
                    out_copy(ctab_ref[c0 + j], j, slot).start()
        else:
            ybuf[slot] = o.astype(BF16)

            def body(j, _):
                out_copy(ctab_ref[c0 + j], j, slot).start()
                return 0

            lax.fori_loop(0, tnv_ref[t], body, 0)

    @pl.when(t == 0)
    def _():
        for k in range(GATHER_BUFFERS - 1):
            start_gather(k, k)

    @pl.when(jnp.logical_or(t == 0, te_ref[t] != te_ref[jnp.maximum(t - 1, 0)]))
    def _():
        wgub[:, :D_EXPERT] = wg_ref[0].astype(BF16)
        wgub[:, D_EXPERT:] = wu_ref[0].astype(BF16)
        wdb[...] = wd_ref[0].astype(BF16)

    @pl.when(t < nt)
    def _():
        @pl.when(t >= 2)
        def _():
            wait_scatter(t - 2, slot)

        pl.when(tnv_ref[t] == CHUNKS_PER_TILE)(lambda: run_tile(True))
        pl.when(tnv_ref[t] < CHUNKS_PER_TILE)(lambda: run_tile(False))

        @pl.when(t == nt - 1)
        def _():
            for k in range(1, GATHER_BUFFERS):
                wait_gather((t + k) % GATHER_BUFFERS)
            wait_scatter(t, slot)

            @pl.when(t >= 1)
            def _():
                wait_scatter(t - 1, 1 - slot)


def _tile_plan(cnt):
    nblk = cnt.shape[0]
    max_chunks = nblk * (2 * TOK_BLOCK // SEG_ALIGN + N_EXPERTS)
    max_tiles = max_chunks // CHUNKS_PER_TILE + N_EXPERTS
    table_len = -(-(max_chunks + CHUNKS_PER_TILE) // LANES) * LANES
    m = (cnt + SEG_ALIGN - 1) // SEG_ALIGN
    seg_start = jnp.cumsum(m, axis=1) - m
    src0 = jnp.arange(nblk, dtype=jnp.int32)[:, None] * N_SLOTS + SEG_ALIGN * seg_start
    seg_m = m.T.reshape(1, -1)
    seg_first = jnp.cumsum(seg_m, axis=1) - seg_m
    assert table_len % TABLE_CHUNK_ROWS == 0

    def row(v):
        return jnp.pad(v.astype(F32), ((0, 0), (0, -v.shape[1] % LANES)))

    chunk_src = pl.pallas_call(
        _chunk_table_kernel,
        out_shape=jax.ShapeDtypeStruct((table_len, 1), jnp.int32),
        name="moe_chunk_table",
    )(row(seg_first), row(seg_m), row(src0.T.reshape(1, -1))).reshape(-1)
    ce = jnp.sum(m, axis=0)
    c_start = jnp.cumsum(ce) - ce
    tiles = (ce + CHUNKS_PER_TILE - 1) // CHUNKS_PER_TILE
    t_cum = jnp.cumsum(tiles)
    nt = t_cum[-1]
    k = jnp.arange(max_tiles + GATHER_BUFFERS - 1, dtype=jnp.int32)
    kk = jnp.minimum(k, nt - 1)
    ek = jnp.minimum(jnp.sum((t_cum[None, :] <= kk[:, None]).astype(jnp.int32), axis=1), N_EXPERTS - 1)
    is_e = ek[:, None] == jnp.arange(N_EXPERTS, dtype=jnp.int32)[None, :]
    pick = lambda v: jnp.sum(jnp.where(is_e, v[None, :], 0), axis=1)
    local = kk - pick(t_cum - tiles)
    tile_nv = jnp.where(k < nt, jnp.clip(pick(ce) - CHUNKS_PER_TILE * local, 0, CHUNKS_PER_TILE), 0)
    tile_c0 = jnp.where(k < nt, pick(c_start) + CHUNKS_PER_TILE * local, 0)
    return tuple(v.astype(jnp.int32) for v in (ek, tile_nv, tile_c0, chunk_src, nt.reshape(1)))


def _moe(x1p, x1s, bp, mod_rows, n2g, wr, br, wg, wu, wd, fg):
    nbp = x1p.shape[0] // TOK_BLOCK
    nblk = nbp + 1
    blk = (TOK_BLOCK, D_MODEL)
    p_spec = pl.BlockSpec(blk, lambda i: (jnp.minimum(i, nbp - 1), 0))
    s_spec = pl.BlockSpec(blk, lambda i: (0, 0))

    def mod_spec(k):
        return pl.BlockSpec((1, 1) + blk, lambda i: (k, jnp.minimum(i // nbp, 1), 0, 0))

    sh_spec, sc_spec, gt_spec = mod_spec(0), mod_spec(1), mod_spec(2)
    meta_spec = pl.BlockSpec((1, META_ROWS, TOK_BLOCK), lambda i: (i, 0, 0))
    slots_spec = pl.BlockSpec((N_SLOTS, D_MODEL), lambda i: (i, 0))
    params = pltpu.CompilerParams(dimension_semantics=("arbitrary",), vmem_limit_bytes=VMEM_LIMIT)

    def cs(shape):
        return pl.BlockSpec(shape, lambda i: (0,) * len(shape))

    xs, meta, cnt = pl.pallas_call(
        functools.partial(_dispatch_kernel, nbp),
        grid=(nblk,),
        in_specs=[p_spec, s_spec, sc_spec, sh_spec, cs((1, D_MODEL)),
                  cs((ROUTER_ROWS, D_MODEL)), cs((ROUTER_ROWS, TOK_BLOCK))],
        out_specs=[slots_spec, meta_spec,
                   pl.BlockSpec((1, ROUTER_ROWS, LANES), lambda i: (i, 0, 0))],
        out_shape=[jax.ShapeDtypeStruct((nblk * N_SLOTS, D_MODEL), BF16),
                   jax.ShapeDtypeStruct((nblk, META_ROWS, TOK_BLOCK), F32),
                   jax.ShapeDtypeStruct((nblk, ROUTER_ROWS, LANES), F32)],
        compiler_params=params,
        name="moe_dispatch",
    )(x1p, x1s, mod_rows, mod_rows, n2g, wr, br)

    plan = _tile_plan(cnt[:, N_EXPERT_GROUPS:N_EXPERT_GROUPS + N_EXPERTS, 0].astype(jnp.int32))
    max_tiles = plan[0].shape[0] - (GATHER_BUFFERS - 1)

    def w_spec(shape):
        return pl.BlockSpec((1,) + shape, lambda t, te, *_: (te[t], 0, 0))

    ys = pl.pallas_call(
        _expert_kernel,
        grid_spec=pltpu.PrefetchScalarGridSpec(
            num_scalar_prefetch=len(plan),
            grid=(max_tiles,),
            in_specs=[pl.BlockSpec(memory_space=pl.ANY),
                      w_spec((D_MODEL, D_EXPERT)), w_spec((D_MODEL, D_EXPERT)), w_spec((D_EXPERT, D_MODEL))],
            out_specs=pl.BlockSpec(memory_space=pl.ANY),
            scratch_shapes=[pltpu.VMEM((GATHER_BUFFERS, ROW_TILE, D_MODEL), BF16),
                            pltpu.VMEM((2, ROW_TILE, D_MODEL), BF16),
                            pltpu.VMEM((D_MODEL, 2 * D_EXPERT), BF16),
                            pltpu.VMEM((D_EXPERT, D_MODEL), BF16),
                            pltpu.SemaphoreType.DMA((GATHER_BUFFERS,)), pltpu.SemaphoreType.DMA((2,))]),
        out_shape=jax.ShapeDtypeStruct(xs.shape, BF16),
        input_output_aliases={len(plan): 0},
        compiler_params=params,
        name="moe_experts",
    )(*plan, xs, wg, wu, wd)

    return pl.pallas_call(
        functools.partial(_combine_kernel, nbp),
        grid=(nblk,),
        in_specs=[p_spec, s_spec, gt_spec, meta_spec, slots_spec, cs((1, D_MODEL))],
        out_specs=[pl.BlockSpec((bp, TOK_BLOCK // bp, D_MODEL), lambda i: (0, jnp.minimum(i, nbp - 1), 0)),
                   s_spec],
        out_shape=[jax.ShapeDtypeStruct((bp, x1p.shape[0] // bp, D_MODEL), F32),
                   jax.ShapeDtypeStruct(x1s.shape, F32)],
        scratch_shapes=[pltpu.VMEM((D_MODEL // LANES, TOK_BLOCK, LANES), F32)],
        compiler_params=params,
        name="moe_combine",
    )(x1p, x1s, mod_rows, meta, ys, fg)


def _diag_halves_in(bbt):
    eye = jnp.eye(SSM_GROUPS // 2, dtype=bbt.dtype)
    blocks = bbt.reshape(SSM_GROUP, 2, SSM_GROUPS // 2, SSM_STATE).transpose(1, 0, 2, 3)
    w = eye[None, :, None, :, None] * blocks[:, None, :, :, :]
    return w.reshape(2, HALF_SSM, HALF_STATE)


def _diag_halves_out(c):
    eye = jnp.eye(SSM_GROUPS // 2, dtype=c.dtype)
    blocks = c.reshape(2, SSM_GROUPS // 2, SSM_GROUP, SSM_STATE).transpose(0, 1, 3, 2)
    w = eye[None, :, None, :, None] * blocks[:, :, :, None, :]
    return w.reshape(2, HALF_STATE, HALF_SSM)


def _to_time_major(x):
    b, s, d = x.shape
    return x.transpose(1, 0, 2).reshape(s * b, d)


def _from_time_major(x, b, s):
    return x.reshape(s, b, -1).transpose(1, 0, 2)


def kernel(x_prompt, x_sample, state_conv, state_ssm_re, state_ssm_im, c_prompt, c_sample, w_ada, b_ada, norm1_g, norm2_g, w_in, conv_w, lambda_re, lambda_im, log_dt, ssm_b_re, ssm_b_im, ssm_c_re, ssm_c_im, ssm_d, w_glu, b_glu, out_norm_conv_g, out_norm_ssm_g, w_out, w_router_group, b_router_group, w_router_expert, b_router_expert, w_expert_gate, w_expert_up, w_expert_down, final_norm_g):
    assert w_ada.shape[0] == 1, "single-layer step"
    bp, sp, _ = x_prompt.shape
    bs, ss, _ = x_sample.shape

    ar, ai, bbr, bbi = _discretise(lambda_re[0], lambda_im[0], log_dt[0], ssm_b_re[0], ssm_b_im[0])
    w_b = jnp.concatenate([_diag_halves_in(bbr), _diag_halves_in(bbi)], axis=2).astype(BF16)
    w_cr = _diag_halves_out(ssm_c_re[0]).astype(BF16)
    w_ci = _diag_halves_out(ssm_c_im[0]).astype(BF16)

    assert bs % bp == 0
    mod, mod_rows = _adaln(jnp.concatenate([c_sample, c_prompt], axis=0), bs, TOK_BLOCK, w_ada[0], b_ada[0])

    mixer_weights = (
        norm1_g[0].reshape(1, -1), w_in[0], conv_w[0], ar, ai, w_b, w_cr, w_ci,
        ssm_d[0].reshape(1, -1), w_glu[0], b_glu[0].reshape(1, -1),
        out_norm_conv_g[0].reshape(1, -1), out_norm_ssm_g[0].reshape(1, -1), w_out[0],
    )

    pad_rows = ROUTER_ROWS - N_EXPERT_GROUPS - N_EXPERTS
    w_router = jnp.concatenate(
        [w_router_group[0].T, w_router_expert[0].transpose(0, 2, 1).reshape(N_EXPERTS, D_MODEL),
         jnp.zeros((pad_rows, D_MODEL), F32)], axis=0)
    b_router = jnp.concatenate(
        [b_router_group[0], b_router_expert[0].reshape(-1), jnp.zeros((pad_rows,), F32)])
    b_router = jnp.broadcast_to(b_router[:, None], (ROUTER_ROWS, TOK_BLOCK))
    moe_weights = (norm2_g[0].reshape(1, -1), w_router, b_router,
                   w_expert_gate[0], w_expert_up[0], w_expert_down[0], final_norm_g.reshape(1, -1))

    def mix(x, mod_row0, h0r, h0i, cb0, chunk, in_kernel_transpose):
        nb, s, _ = x.shape
        x1, cbo, hro, hio = _mixer(
            x if in_kernel_transpose else _to_time_major(x), nb, s, chunk,
            mod, mod_row0, h0r, h0i, cb0, mixer_weights)
        new_conv = cbo.reshape(2, nb, CONV_DIM).transpose(1, 0, 2)[None]
        new_re = hro.reshape(1, nb, SSM_GROUPS, SSM_STATE)
        new_im = hio.reshape(1, nb, SSM_GROUPS, SSM_STATE)
        return x1, new_conv, new_re, new_im

    zero_state = jnp.zeros((bp, N_STATE), F32)
    zero_conv = jnp.zeros((2 * bp, CONV_DIM), F32)
    x1p, conv_p, re_p, im_p = mix(x_prompt, bs, zero_state, zero_state, zero_conv, TOK_BLOCK // bp, True)
    x1s, conv_s, re_s, im_s = mix(
        x_sample, 0,
        state_ssm_re[0].reshape(bs, N_STATE), state_ssm_im[0].reshape(bs, N_STATE),
        state_conv[0].transpose(1, 0, 2).reshape(2 * bs, CONV_DIM), ss, False)

    assert bs * ss == TOK_BLOCK and (bp * sp) % TOK_BLOCK == 0 and TOK_BLOCK % bp == 0

    y_p, y_s = _moe(x1p, x1s, bp, mod_rows, *moe_weights)
    return (y_p, _from_time_major(y_s, bs, ss), conv_p, re_p, im_p, conv_s, re_s, im_s)
```

```python
import functools

import jax
import jax.numpy as jnp
from jax import lax
from jax.experimental import pallas as pl
from jax.experimental.pallas import tpu as pltpu

D_MODEL = 1024
CONV_DIM = 512
SSM_DIM = 512
SSM_GROUP = 16
SSM_GROUPS = 32
SSM_STATE = 64
N_STATE = SSM_GROUPS * SSM_STATE
HALF_SSM = SSM_DIM // 2
HALF_STATE = N_STATE // 2
IN_DIM = 2048
N_EXPERT_GROUPS = 4
EXPERTS_PER_GROUP = 8
N_EXPERTS = 32
D_EXPERT = 256
N_MOD = 6
EPS = 1e-6

SUBLANES = 8
LANES = 128
SCAN_COLS = 512
STRIDE_PAD = 8
VMEM_LIMIT = 60 * 1024 * 1024

F32 = jnp.float32
BF16 = jnp.bfloat16


def _sigmoid(x):
    return 1.0 / (1.0 + jnp.exp(-x))


def _gelu_tanh(x):
    return 0.5 * x * (1.0 + jnp.tanh(0.7978845608028654 * (x + 0.044715 * (x * x * x))))


def _rms(x, g):
    return x * lax.rsqrt(jnp.mean(x * x, axis=-1, keepdims=True) + EPS) * g


def _bdot(a, b):
    return jnp.dot(a.astype(BF16), b.astype(BF16), preferred_element_type=F32)


def _per_seq(v, nb, scale, shift=None):
    rows, d = v.shape
    v3 = v.reshape(rows // nb, nb, d) * scale[None]
    if shift is not None:
        v3 = v3 + shift[None]
    return v3.reshape(rows, d)


def _disc_kernel(lr_ref, li_ref, ldt_ref, brt_ref, bit_ref, ar_ref, ai_ref, bbr_ref, bbi_ref):
    lr = lr_ref[...]
    li = li_ref[...]
    dt = jnp.exp(ldt_ref[...])
    mag = jnp.exp(lr * dt)
    ar = mag * jnp.cos(li * dt)
    ai = mag * jnp.sin(li * dt)
    den = lr * lr + li * li
    fr = ((ar - 1.0) * lr + ai * li) / den
    fi = (ai * lr - (ar - 1.0) * li) / den
    ar_ref[...] = ar
    ai_ref[...] = ai
    brt = brt_ref[...]
    bit = bit_ref[...]
    bbr_ref[...] = fr * brt - fi * bit
    bbi_ref[...] = fr * bit + fi * brt


def _discretise(lambda_re, lambda_im, log_dt, b_re, b_im):
    lr = lambda_re.reshape(1, N_STATE)
    li = lambda_im.reshape(1, N_STATE)
    ldt = jnp.repeat(log_dt, SSM_STATE).reshape(1, N_STATE)
    brt = b_re.transpose(2, 0, 1).reshape(SSM_GROUP, N_STATE)
    bit = b_im.transpose(2, 0, 1).reshape(SSM_GROUP, N_STATE)
    row = jax.ShapeDtypeStruct((1, N_STATE), F32)
    mat = jax.ShapeDtypeStruct((SSM_GROUP, N_STATE), F32)
    return pl.pallas_call(_disc_kernel, out_shape=(row, row, mat, mat), name="s5_discretise")(
        lr, li, ldt, brt, bit)


def _ada_kernel(n_first, block_rows, c_ref, w_ref, b_ref, o_ref, rows_ref):
    c = c_ref[...]
    o = _bdot(c * _sigmoid(c), w_ref[...]) + b_ref[...]
    o_ref[...] = o

    @pl.when(pl.program_id(0) >= N_MOD // 2)
    def _():
        n_rest = o.shape[0] - n_first
        rows_ref[0, 0] = jnp.tile(o[n_first:], (block_rows // n_rest, 1))
        rows_ref[0, 1] = jnp.tile(o[:n_first], (block_rows // n_first, 1))


def _adaln(c_all, n_first, block_rows, w_ada, b_ada):
    n = c_all.shape[0]
    half = N_MOD // 2
    return pl.pallas_call(
        functools.partial(_ada_kernel, n_first, block_rows),
        grid=(N_MOD,),
        in_specs=[
            pl.BlockSpec((n, D_MODEL), lambda j: (0, 0)),
            pl.BlockSpec((D_MODEL, D_MODEL), lambda j: (0, j)),
            pl.BlockSpec((1, D_MODEL), lambda j: (0, j)),
        ],
        out_specs=[pl.BlockSpec((n, D_MODEL), lambda j: (0, j)),
                   pl.BlockSpec((1, 2, block_rows, D_MODEL), lambda j: (jnp.maximum(j - half, 0), 0, 0, 0))],
        out_shape=[jax.ShapeDtypeStruct((n, N_MOD * D_MODEL), F32),
                   jax.ShapeDtypeStruct((half, 2, block_rows, D_MODEL), F32)],
        compiler_params=pltpu.CompilerParams(
            dimension_semantics=("arbitrary",), vmem_limit_bytes=VMEM_LIMIT),
        name="adaln",
    )(c_all, w_ada, b_ada.reshape(1, -1))


def _rows_to_time_major(x_ref, slab, xt_scr):
    nb, steps, d = x_ref.shape
    pitch = slab.shape[1] // nb
    for b in range(nb):
        for j in range(d // LANES):
            slab[j, b * pitch:b * pitch + steps, :] = x_ref[b, :, j * LANES:(j + 1) * LANES]
    for l in range(steps):
        for j in range(d // LANES):
            xt_scr[l * nb:(l + 1) * nb, j * LANES:(j + 1) * LANES] = slab[j, pl.ds(l, nb, stride=pitch), :]


def _group_diagonal(w_t, k):
    tiled = jnp.tile(w_t[:, k * HALF_STATE:(k + 1) * HALF_STATE], (SSM_GROUPS // 2, 1))
    row_group = lax.shift_right_logical(lax.broadcasted_iota(jnp.int32, tiled.shape, 0),
                                        SSM_GROUP.bit_length() - 1)
    lane_group = lax.shift_right_logical(lax.broadcasted_iota(jnp.int32, tiled.shape, 1),
                                         SSM_STATE.bit_length() - 1)
    return jnp.where(row_group == lane_group, tiled, 0.0).astype(BF16)


def _mixer_kernel(nb, steps, batch_major,
                  x_ref, sh_ref, sc_ref, gt_ref, h0r_ref, h0i_ref, cb0_ref,
                  n1g_ref, win_ref, cw_ref, ar_ref, ai_ref, bbr_ref, bbi_ref, crt_ref, cit_ref,
                  dsk_ref, wglu_ref, bglu_ref, gcv_ref, gsm_ref, wout_ref,
                  x1_ref, cbo_ref, hro_ref, hio_ref,
                  z_scr, cv_scr, s_scr, hr_scr, hi_scr, win_b, wglu_b, wout_b, wb_ref, wcr_ref, wci_ref,
                  *tm_scr):
    rows = nb * steps
    i = pl.program_id(0)

    @pl.when(i == 0)
    def _():
        hr_scr[...] = h0r_ref[...]
        hi_scr[...] = h0i_ref[...]
        cv_scr[0:2 * nb, :] = cb0_ref[...]
        win_b[...] = win_ref[...].astype(BF16)
        wglu_b[...] = wglu_ref[...].astype(BF16)
        wout_b[...] = wout_ref[...].astype(BF16)
        for k in range(2):
            wb_ref[k, :, 0:HALF_STATE] = _group_diagonal(bbr_ref[...], k)
            wb_ref[k, :, HALF_STATE:2 * HALF_STATE] = _group_diagonal(bbi_ref[...], k)
            wcr_ref[k] = _group_diagonal(crt_ref[...], k)
            wci_ref[k] = _group_diagonal(cit_ref[...], k)

    if batch_major:
        slab, x_tm = tm_scr
        _rows_to_time_major(x_ref, slab, x_tm)
    else:
        x_tm = x_ref
    h = _per_seq(_rms(x_tm[...], n1g_ref[...]), nb, 1.0 + sc_ref[...], sh_ref[...])
    z_scr[...] = _bdot(h, win_b[...])

    u_cols = 3 * CONV_DIM
    for k in range(2):
        s_scr[:, 2 * k * HALF_STATE:2 * (k + 1) * HALF_STATE] = _bdot(
            z_scr[:, u_cols + k * HALF_SSM:u_cols + (k + 1) * HALF_SSM], wb_ref[k])

    def scan_cols(cg):
        cols = slice(cg * SCAN_COLS, (cg + 1) * SCAN_COLS)
        half, off = divmod(cg * SCAN_COLS, HALF_STATE)
        re_cols = slice(2 * half * HALF_STATE + off, 2 * half * HALF_STATE + off + SCAN_COLS)
        im_cols = slice(re_cols.start + HALF_STATE, re_cols.stop + HALF_STATE)
        a_r = jnp.broadcast_to(ar_ref[:, cols], (SUBLANES, SCAN_COLS))
        a_i = jnp.broadcast_to(ai_ref[:, cols], (SUBLANES, SCAN_COLS))

        def advance(hr, hi, rows_l):
            nr = a_r * hr - a_i * hi + s_scr[rows_l, re_cols]
            ni = a_r * hi + a_i * hr + s_scr[rows_l, im_cols]
            s_scr[rows_l, re_cols] = nr
            s_scr[rows_l, im_cols] = ni
            return nr, ni

        for r0 in range(0, nb, SUBLANES):
            tile = slice(r0, r0 + SUBLANES)
            hr, hi = hr_scr[tile, cols], hi_scr[tile, cols]
            for l in range(steps):
                hr, hi = advance(hr, hi, slice(l * nb + r0, l * nb + r0 + SUBLANES))
            hr_scr[tile, cols] = hr
            hi_scr[tile, cols] = hi

    y_halves = []
    for k in range(2):
        for cg in range(k * HALF_STATE // SCAN_COLS, (k + 1) * HALF_STATE // SCAN_COLS):
            scan_cols(cg)
        s_re = s_scr[:, 2 * k * HALF_STATE:(2 * k + 1) * HALF_STATE].astype(BF16)
        s_im = s_scr[:, (2 * k + 1) * HALF_STATE:(2 * k + 2) * HALF_STATE].astype(BF16)
        y_halves.append(lax.dot_general(s_re, wcr_ref[k], _NT, preferred_element_type=F32)
                        - lax.dot_general(s_im, wci_ref[k], _NT, preferred_element_type=F32))

    hro_ref[...] = hr_scr[...]
    hio_ref[...] = hi_scr[...]

    y = jnp.concatenate(y_halves, axis=-1) + dsk_ref[...] * z_scr[:, u_cols:IN_DIM]
    g = _gelu_tanh(y)
    y_ssm = _rms(g * _sigmoid(_bdot(g, wglu_b[...]) + bglu_ref[...]), gsm_ref[...])

    cv_scr[2 * nb:2 * nb + rows, :] = z_scr[:, CONV_DIM:2 * CONV_DIM] * z_scr[:, 2 * CONV_DIM:3 * CONV_DIM]
    conv = (cv_scr[0:rows, :] * cw_ref[0:1, :]
            + cv_scr[nb:nb + rows, :] * cw_ref[1:2, :]
            + cv_scr[2 * nb:2 * nb + rows, :] * cw_ref[2:3, :])
    y_conv = _rms(z_scr[:, 0:CONV_DIM] * conv, gcv_ref[...])
    carry = cv_scr[rows:rows + 2 * nb, :]
    cv_scr[0:2 * nb, :] = carry
    cbo_ref[...] = carry

    mixed = _bdot(y_conv, wout_b[0:CONV_DIM, :]) + _bdot(y_ssm, wout_b[CONV_DIM:2 * CONV_DIM, :])
    x1_ref[...] = x_tm[...] + _per_seq(mixed, nb, gt_ref[...])


def _const_spec(shape):
    return pl.BlockSpec(shape, lambda i: (0,) * len(shape), pipeline_mode=pl.Buffered(1))


def _mixer(x, nb, steps_total, chunk, mod, mod_row0, h0r, h0i, cb0, weights):
    rows = nb * chunk
    nchunks = steps_total // chunk
    batch_major = x.ndim == 3
    if batch_major:
        assert nb == SUBLANES
        x_spec = pl.BlockSpec((nb, chunk, D_MODEL), lambda i: (0, i, 0))
        tm_scratch = [pltpu.VMEM((D_MODEL // LANES, nb * (chunk + STRIDE_PAD), LANES), F32),
                      pltpu.VMEM((rows, D_MODEL), F32)]
    else:
        x_spec = pl.BlockSpec((rows, D_MODEL), lambda i: (i, 0))
        tm_scratch = []
    assert mod_row0 % nb == 0

    def mod_spec(k):
        return pl.BlockSpec((nb, D_MODEL), lambda i: (mod_row0 // nb, k), pipeline_mode=pl.Buffered(1))

    in_specs = [
        x_spec, mod_spec(0), mod_spec(1), mod_spec(2),
        _const_spec((nb, N_STATE)), _const_spec((nb, N_STATE)), _const_spec((2 * nb, CONV_DIM)),
    ] + [_const_spec(w.shape) for w in weights]
    out_specs = [
        pl.BlockSpec((rows, D_MODEL), lambda i: (i, 0)),
        _const_spec((2 * nb, CONV_DIM)), _const_spec((nb, N_STATE)), _const_spec((nb, N_STATE)),
    ]
    out_shape = [
        jax.ShapeDtypeStruct((steps_total * nb, D_MODEL), F32),
        jax.ShapeDtypeStruct((2 * nb, CONV_DIM), F32),
        jax.ShapeDtypeStruct((nb, N_STATE), F32),
        jax.ShapeDtypeStruct((nb, N_STATE), F32),
    ]
    scratch = [
        pltpu.VMEM((rows, IN_DIM), F32),
        pltpu.VMEM((rows + 2 * nb, CONV_DIM), F32),
        pltpu.VMEM((rows, 2 * N_STATE), F32),
        pltpu.VMEM((nb, N_STATE), F32),
        pltpu.VMEM((nb, N_STATE), F32),
        pltpu.VMEM((D_MODEL, IN_DIM), BF16),
        pltpu.VMEM((SSM_DIM, SSM_DIM), BF16),
        pltpu.VMEM((D_MODEL, D_MODEL), BF16),
        pltpu.VMEM((2, HALF_SSM, 2 * HALF_STATE), BF16),
        pltpu.VMEM((2, HALF_SSM, HALF_STATE), BF16),
        pltpu.VMEM((2, HALF_SSM, HALF_STATE), BF16),
    ] + tm_scratch
    return pl.pallas_call(
        functools.partial(_mixer_kernel, nb, chunk, batch_major),
        grid=(nchunks,),
        in_specs=in_specs, out_specs=out_specs, out_shape=out_shape,
        scratch_shapes=scratch,
        compiler_params=pltpu.CompilerParams(
            dimension_semantics=("arbitrary",), vmem_limit_bytes=VMEM_LIMIT),
        name="mixer",
    )(x, mod, mod, mod, h0r, h0i, cb0, *weights)


TOK_BLOCK = 512
SEG_ALIGN = 16
ROW_TILE = 512
CHUNKS_PER_TILE = ROW_TILE // SEG_ALIGN
SCATTER_GROUPS = 8
GATHER_BUFFERS = 3
TABLE_CHUNK_ROWS = 64
N_SLOTS = -(-(2 * TOK_BLOCK + N_EXPERTS * (SEG_ALIGN - 1)) // 256) * 256
SPARE_ROW = N_SLOTS - SEG_ALIGN
assert SPARE_ROW >= 2 * TOK_BLOCK + N_EXPERTS * (SEG_ALIGN - 1)
ROUTER_ROWS = 48
META_ROWS = SUBLANES

_NT = (((1,), (1,)), ((), ()))
_TN = (((0,), (0,)), ((), ()))


def _iota_f32(shape, dim):
    return lax.broadcasted_iota(jnp.int32, shape, dim).astype(F32)


def _split_bf16(v):
    hi = v.astype(BF16)
    return hi, (v - hi.astype(F32)).astype(BF16)


def _route_t(lg):
    row = _iota_f32(lg.shape, 0)
    neg = jnp.float32(-jnp.inf)
    none = float(ROUTER_ROWS)
    is_g = row < N_EXPERT_GROUPS
    gl = jnp.where(is_g, lg, neg)
    gmax = jnp.max(gl, axis=0, keepdims=True)
    gsum = jnp.sum(jnp.where(is_g, jnp.exp(gl - gmax), 0.0), axis=0, keepdims=True)
    g_prob = 1.0 / gsum
    g_idx = jnp.min(jnp.where(gl == gmax, row, none), axis=0, keepdims=True)
    lo = N_EXPERT_GROUPS + EXPERTS_PER_GROUP * g_idx
    el = jnp.where(row >= lo, jnp.where(row < lo + EXPERTS_PER_GROUP, lg, neg), neg)
    m1 = jnp.max(el, axis=0, keepdims=True)
    i1 = jnp.min(jnp.where(el == m1, row, none), axis=0, keepdims=True)
    el2 = jnp.where(row == i1, neg, el)
    m2 = jnp.max(el2, axis=0, keepdims=True)
    i2 = jnp.min(jnp.where(el2 == m2, row, none), axis=0, keepdims=True)
    e21 = jnp.exp(m2 - m1)
    return i1, i2, g_prob / (1.0 + e21), g_prob * e21 / (1.0 + e21)


def _dispatch_kernel(nbp, x1p_ref, x1s_ref, sc_ref, sh_ref, n2g_ref, wr_ref, br_ref,
                     xs_ref, meta_ref, cnt_ref):
    x1 = jnp.where(pl.program_id(0) < nbp, x1p_ref[...], x1s_ref[...])
    h = _rms(x1, n2g_ref[...]) * (1.0 + sc_ref[0, 0]) + sh_ref[0, 0]
    t = h.shape[0]
    hb, hl = _split_bf16(h)
    wh, wl = _split_bf16(wr_ref[...])
    lg = (lax.dot_general(wh, hb, _NT, preferred_element_type=F32)
          + lax.dot_general(wh, hl, _NT, preferred_element_type=F32)
          + lax.dot_general(wl, hb, _NT, preferred_element_type=F32)) + br_ref[...]
    i1, i2, w1, w2 = _route_t(lg)

    row = _iota_f32((ROUTER_ROWS, t), 0)
    hit1 = row == i1
    hit2 = row == i2
    onehot = jnp.where(hit1, 1.0, 0.0) + jnp.where(hit2, 1.0, 0.0)
    before = jnp.where(_iota_f32((t, t), 0) < _iota_f32((t, t), 1), 1.0, 0.0)
    rank = _bdot(onehot, before)
    cnt = jnp.sum(onehot, axis=1, keepdims=True)
    seg = jnp.floor((cnt + (SEG_ALIGN - 1)) * (1.0 / SEG_ALIGN)) * SEG_ALIGN
    earlier = jnp.where(_iota_f32((ROUTER_ROWS, ROUTER_ROWS), 1) < _iota_f32((ROUTER_ROWS, ROUTER_ROWS), 0),
                        1.0, 0.0)
    base = rank + _bdot(earlier, jnp.broadcast_to(seg, (ROUTER_ROWS, t)))
    slot1 = jnp.sum(jnp.where(hit1, base, 0.0), axis=0, keepdims=True)
    slot2 = jnp.sum(jnp.where(hit2, base, 0.0), axis=0, keepdims=True)
    j = _iota_f32((N_SLOTS, t), 0)
    place = jnp.where(j == slot1, 1.0, 0.0) + jnp.where(j == slot2, 1.0, 0.0)
    xs_ref[...] = jnp.dot(place.astype(BF16), hb, preferred_element_type=F32).astype(BF16)
    r = _iota_f32((META_ROWS, t), 0)
    meta_ref[0] = jnp.where(r == 0.0, slot1, jnp.where(r == 1.0, slot2,
                            jnp.where(r == 2.0, w1, jnp.where(r == 3.0, w2, 0.0))))
    cnt_ref[0] = jnp.broadcast_to(cnt, (ROUTER_ROWS, LANES))


def _combine_kernel(nbp, x1p_ref, x1s_ref, gt_ref, meta_ref, ys_ref, fg_ref, yp_ref, ysm_ref, slab):
    i = pl.program_id(0)
    x1 = jnp.where(i < nbp, x1p_ref[...], x1s_ref[...])
    meta = meta_ref[0]
    slot1, slot2, w1, w2 = (meta[k:k + 1, :] for k in range(4))
    j = _iota_f32((N_SLOTS, x1.shape[0]), 0)
    pick = jnp.where(j == slot1, w1, 0.0) + jnp.where(j == slot2, w2, 0.0)
    moe = lax.dot_general(pick.astype(BF16), ys_ref[...], _TN, preferred_element_type=F32)
    y = _rms(x1 + gt_ref[0, 0] * moe, fg_ref[...])

    @pl.when(i < nbp)
    def _():
        nb, steps, d = yp_ref.shape
        for j in range(d // LANES):
            slab[j] = y[:, j * LANES:(j + 1) * LANES]
        for b in range(nb):
            for j in range(d // LANES):
                yp_ref[b, :, j * LANES:(j + 1) * LANES] = slab[j, pl.ds(b, steps, stride=nb), :]

    @pl.when(i >= nbp)
    def _():
        ysm_ref[...] = y


def _chunk_table_kernel(start_ref, m_ref, src_ref, out_ref):
    start = start_ref[...]
    end = start + m_ref[...]
    src = src_ref[...]

    def body(b, _):
        rows = pl.ds(pl.multiple_of(b * TABLE_CHUNK_ROWS, TABLE_CHUNK_ROWS), TABLE_CHUNK_ROWS)
        first = lax.convert_element_type(b * TABLE_CHUNK_ROWS, F32)
        i = _iota_f32((TABLE_CHUNK_ROWS, start.shape[1]), 0) + first
        val = jnp.where(i >= start, jnp.where(i < end, src + SEG_ALIGN * (i - start), 0.0), 0.0)
        out_ref[rows, :] = jnp.sum(val, axis=1, keepdims=True).astype(jnp.int32)
        return 0

    lax.fori_loop(0, out_ref.shape[0] // TABLE_CHUNK_ROWS, body, 0)


def _expert_kernel(te_ref, tnv_ref, tc0_ref, ctab_ref, nt_ref, xs_hbm, wg_ref, wu_ref, wd_ref,
                   ys_hbm, xbuf, ybuf, wgub, wdb, in_sem, out_sem):
    t = pl.program_id(0)
    nt = nt_ref[0]
    slot = t % 2
    xslot = t % GATHER_BUFFERS

    def chunk_rows(j):
        return pl.ds(pl.multiple_of(j * SEG_ALIGN, SEG_ALIGN), SEG_ALIGN)

    def hbm_rows(ref, r):
        return ref.at[pl.ds(pl.multiple_of(r, SEG_ALIGN), SEG_ALIGN), :]

    def in_copy(r, j, s):
        return pltpu.make_async_copy(hbm_rows(xs_hbm, r), xbuf.at[s, chunk_rows(j), :], in_sem.at[s])

    def out_copy(r, j, s):
        return pltpu.make_async_copy(ybuf.at[s, chunk_rows(j), :], hbm_rows(ys_hbm, r), out_sem.at[s])

    def start_gather(tile, s):
        c0 = tc0_ref[tile]
        nv = tnv_ref[tile]
        for j in range(CHUNKS_PER_TILE):
            in_copy(jnp.where(j < nv, ctab_ref[c0 + j], SPARE_ROW), j, s).start()

    def wait_gather(s):
        pltpu.make_async_copy(xs_hbm.at[pl.ds(0, ROW_TILE), :], xbuf.at[s], in_sem.at[s]).wait()

    def wait_scatter(tile, s):
        nv = tnv_ref[tile]

        @pl.when(nv == CHUNKS_PER_TILE)
        def _():
            pltpu.make_async_copy(ybuf.at[s], ys_hbm.at[pl.ds(0, ROW_TILE), :], out_sem.at[s]).wait()

        @pl.when(nv < CHUNKS_PER_TILE)
        def _():
            def body(j, _):
                out_copy(0, j, s).wait()
                return 0

            lax.fori_loop(0, nv, body, 0)

    def run_tile(full):
        ahead = t + GATHER_BUFFERS - 1
        start_gather(ahead, ahead % GATHER_BUFFERS)
        wait_gather(xslot)
        ab = jnp.dot(xbuf[xslot], wgub[...], preferred_element_type=F32)
        a, b = ab[:, :D_EXPERT], ab[:, D_EXPERT:]
        o = jnp.dot((a * _sigmoid(a) * b).astype(BF16), wdb[...], preferred_element_type=F32)
        c0 = tc0_ref[t]
        if full:
            group = ROW_TILE // SCATTER_GROUPS
            for r in range(0, ROW_TILE, group):
                ybuf[slot, r:r + group, :] = o[r:r + group].astype(BF16)
                for j in range(r // SEG_ALIGN, (r + group) // SEG_ALIGN):
                    out_copy(ctab_ref[c0 + j], j, slot).start()
        else:
            ybuf[slot] = o.astype(BF16)

            def body(j, _):
                out_copy(ctab_ref[c0 + j], j, slot).start()
                return 0

            lax.fori_loop(0, tnv_ref[t], body, 0)

    @pl.when(t == 0)
    def _():
        for k in range(GATHER_BUFFERS - 1):
            start_gather(k, k)

    @pl.when(jnp.logical_or(t == 0, te_ref[t] != te_ref[jnp.maximum(t - 1, 0)]))
    def _():
        wgub[:, :D_EXPERT] = wg_ref[0].astype(BF16)
        wgub[:, D_EXPERT:] = wu_ref[0].astype(BF16)
        wdb[...] = wd_ref[0].astype(BF16)

    @pl.when(t < nt)
    def _():
        @pl.when(t >= 2)
        def _():
            wait_scatter(t - 2, slot)

        pl.when(tnv_ref[t] == CHUNKS_PER_TILE)(lambda: run_tile(True))
        pl.when(tnv_ref[t] < CHUNKS_PER_TILE)(lambda: run_tile(False))

        @pl.when(t == nt - 1)
        def _():
            for k in range(1, GATHER_BUFFERS):
                wait_gather((t + k) % GATHER_BUFFERS)
            wait_scatter(t, slot)

            @pl.when(t >= 1)
            def _():
                wait_scatter(t - 1, 1 - slot)


def _tile_plan(cnt):
    nblk = cnt.shape[0]
    max_chunks = nblk * (2 * TOK_BLOCK // SEG_ALIGN + N_EXPERTS)
    max_tiles = max_chunks // CHUNKS_PER_TILE + N_EXPERTS
    table_len = -(-(max_chunks + CHUNKS_PER_TILE) // LANES) * LANES
    m = (cnt + SEG_ALIGN - 1) // SEG_ALIGN
    seg_start = jnp.cumsum(m, axis=1) - m
    src0 = jnp.arange(nblk, dtype=jnp.int32)[:, None] * N_SLOTS + SEG_ALIGN * seg_start
    seg_m = m.T.reshape(1, -1)
    seg_first = jnp.cumsum(seg_m, axis=1) - seg_m
    assert table_len % TABLE_CHUNK_ROWS == 0

    def row(v):
        return jnp.pad(v.astype(F32), ((0, 0), (0, -v.shape[1] % LANES)))

    chunk_src = pl.pallas_call(
        _chunk_table_kernel,
        out_shape=jax.ShapeDtypeStruct((table_len, 1), jnp.int32),
        name="moe_chunk_table",
    )(row(seg_first), row(seg_m), row(src0.T.reshape(1, -1))).reshape(-1)
    ce = jnp.sum(m, axis=0)
    c_start = jnp.cumsum(ce) - ce
    tiles = (ce + CHUNKS_PER_TILE - 1) // CHUNKS_PER_TILE
    t_cum = jnp.cumsum(tiles)
    nt = t_cum[-1]
    k = jnp.arange(max_tiles + GATHER_BUFFERS - 1, dtype=jnp.int32)
    kk = jnp.minimum(k, nt - 1)
    ek = jnp.minimum(jnp.sum((t_cum[None, :] <= kk[:, None]).astype(jnp.int32), axis=1), N_EXPERTS - 1)
    is_e = ek[:, None] == jnp.arange(N_EXPERTS, dtype=jnp.int32)[None, :]
    pick = lambda v: jnp.sum(jnp.where(is_e, v[None, :], 0), axis=1)
    local = kk - pick(t_cum - tiles)
    tile_nv = jnp.where(k < nt, jnp.clip(pick(ce) - CHUNKS_PER_TILE * local, 0, CHUNKS_PER_TILE), 0)
    tile_c0 = jnp.where(k < nt, pick(c_start) + CHUNKS_PER_TILE * local, 0)
    return tuple(v.astype(jnp.int32) for v in (ek, tile_nv, tile_c0, chunk_src, nt.reshape(1)))


def _moe(x1p, x1s, bp, mod_rows, n2g, wr, br, wg, wu, wd, fg):
    nbp = x1p.shape[0] // TOK_BLOCK
    nblk = nbp + 1
    blk = (TOK_BLOCK, D_MODEL)
    p_spec = pl.BlockSpec(blk, lambda i: (jnp.minimum(i, nbp - 1), 0))
    s_spec = pl.BlockSpec(blk, lambda i: (0, 0))

    def mod_spec(k):
        return pl.BlockSpec((1, 1) + blk, lambda i: (k, jnp.minimum(i // nbp, 1), 0, 0))

    sh_spec, sc_spec, gt_spec = mod_spec(0), mod_spec(1), mod_spec(2)
    meta_spec = pl.BlockSpec((1, META_ROWS, TOK_BLOCK), lambda i: (i, 0, 0))
    slots_spec = pl.BlockSpec((N_SLOTS, D_MODEL), lambda i: (i, 0))
    params = pltpu.CompilerParams(dimension_semantics=("arbitrary",), vmem_limit_bytes=VMEM_LIMIT)

    def cs(shape):
        return pl.BlockSpec(shape, lambda i: (0,) * len(shape))

    xs, meta, cnt = pl.pallas_call(
        functools.partial(_dispatch_kernel, nbp),
        grid=(nblk,),
        in_specs=[p_spec, s_spec, sc_spec, sh_spec, cs((1, D_MODEL)),
                  cs((ROUTER_ROWS, D_MODEL)), cs((ROUTER_ROWS, TOK_BLOCK))],
        out_specs=[slots_spec, meta_spec,
                   pl.BlockSpec((1, ROUTER_ROWS, LANES), lambda i: (i, 0, 0))],
        out_shape=[jax.ShapeDtypeStruct((nblk * N_SLOTS, D_MODEL), BF16),
                   jax.ShapeDtypeStruct((nblk, META_ROWS, TOK_BLOCK), F32),
                   jax.ShapeDtypeStruct((nblk, ROUTER_ROWS, LANES), F32)],
        compiler_params=params,
        name="moe_dispatch",
    )(x1p, x1s, mod_rows, mod_rows, n2g, wr, br)

    plan = _tile_plan(cnt[:, N_EXPERT_GROUPS:N_EXPERT_GROUPS + N_EXPERTS, 0].astype(jnp.int32))
    max_tiles = plan[0].shape[0] - (GATHER_BUFFERS - 1)

    def w_spec(shape):
        return pl.BlockSpec((1,) + shape, lambda t, te, *_: (te[t], 0, 0))

    ys = pl.pallas_call(
        _expert_kernel,
        grid_spec=pltpu.PrefetchScalarGridSpec(
            num_scalar_prefetch=len(plan),
            grid=(max_tiles,),
            in_specs=[pl.BlockSpec(memory_space=pl.ANY),
                      w_spec((D_MODEL, D_EXPERT)), w_spec((D_MODEL, D_EXPERT)), w_spec((D_EXPERT, D_MODEL))],
            out_specs=pl.BlockSpec(memory_space=pl.ANY),
            scratch_shapes=[pltpu.VMEM((GATHER_BUFFERS, ROW_TILE, D_MODEL), BF16),
                            pltpu.VMEM((2, ROW_TILE, D_MODEL), BF16),
                            pltpu.VMEM((D_MODEL, 2 * D_EXPERT), BF16),
                            pltpu.VMEM((D_EXPERT, D_MODEL), BF16),
                            pltpu.SemaphoreType.DMA((GATHER_BUFFERS,)), pltpu.SemaphoreType.DMA((2,))]),
        out_shape=jax.ShapeDtypeStruct(xs.shape, BF16),
        input_output_aliases={len(plan): 0},
        compiler_params=params,
        name="moe_experts",
    )(*plan, xs, wg, wu, wd)

    return pl.pallas_call(
        functools.partial(_combine_kernel, nbp),
        grid=(nblk,),
        in_specs=[p_spec, s_spec, gt_spec, meta_spec, slots_spec, cs((1, D_MODEL))],
        out_specs=[pl.BlockSpec((bp, TOK_BLOCK // bp, D_MODEL), lambda i: (0, jnp.minimum(i, nbp - 1), 0)),
                   s_spec],
        out_shape=[jax.ShapeDtypeStruct((bp, x1p.shape[0] // bp, D_MODEL), F32),
                   jax.ShapeDtypeStruct(x1s.shape, F32)],
        scratch_shapes=[pltpu.VMEM((D_MODEL // LANES, TOK_BLOCK, LANES), F32)],
        compiler_params=params,
        name="moe_combine",
    )(x1p, x1s, mod_rows, meta, ys, fg)


def _to_time_major(x):
    b, s, d = x.shape
    return x.transpose(1, 0, 2).reshape(s * b, d)


def _from_time_major(x, b, s):
    return x.reshape(s, b, -1).transpose(1, 0, 2)


def kernel(x_prompt, x_sample, state_conv, state_ssm_re, state_ssm_im, c_prompt, c_sample, w_ada, b_ada, norm1_g, norm2_g, w_in, conv_w, lambda_re, lambda_im, log_dt, ssm_b_re, ssm_b_im, ssm_c_re, ssm_c_im, ssm_d, w_glu, b_glu, out_norm_conv_g, out_norm_ssm_g, w_out, w_router_group, b_router_group, w_router_expert, b_router_expert, w_expert_gate, w_expert_up, w_expert_down, final_norm_g):
    assert w_ada.shape[0] == 1, "single-layer step"
    bp, sp, _ = x_prompt.shape
    bs, ss, _ = x_sample.shape

    ar, ai, bbr, bbi = _discretise(lambda_re[0], lambda_im[0], log_dt[0], ssm_b_re[0], ssm_b_im[0])
    crt = ssm_c_re[0].transpose(1, 0, 2).reshape(SSM_GROUP, N_STATE)
    cit = ssm_c_im[0].transpose(1, 0, 2).reshape(SSM_GROUP, N_STATE)

    assert bs % bp == 0
    mod, mod_rows = _adaln(jnp.concatenate([c_sample, c_prompt], axis=0), bs, TOK_BLOCK, w_ada[0], b_ada[0])

    mixer_weights = (
        norm1_g[0].reshape(1, -1), w_in[0], conv_w[0], ar, ai, bbr, bbi, crt, cit,
        ssm_d[0].reshape(1, -1), w_glu[0], b_glu[0].reshape(1, -1),
        out_norm_conv_g[0].reshape(1, -1), out_norm_ssm_g[0].reshape(1, -1), w_out[0],
    )

    pad_rows = ROUTER_ROWS - N_EXPERT_GROUPS - N_EXPERTS
    w_router = jnp.concatenate(
        [w_router_group[0].T, w_router_expert[0].transpose(0, 2, 1).reshape(N_EXPERTS, D_MODEL),
         jnp.zeros((pad_rows, D_MODEL), F32)], axis=0)
    b_router = jnp.concatenate(
        [b_router_group[0], b_router_expert[0].reshape(-1), jnp.zeros((pad_rows,), F32)])
    b_router = jnp.broadcast_to(b_router[:, None], (ROUTER_ROWS, TOK_BLOCK))
    moe_weights = (norm2_g[0].reshape(1, -1), w_router, b_router,
                   w_expert_gate[0], w_expert_up[0], w_expert_down[0], final_norm_g.reshape(1, -1))

    def mix(x, mod_row0, h0r, h0i, cb0, chunk, in_kernel_transpose):
        nb, s, _ = x.shape
        x1, cbo, hro, hio = _mixer(
            x if in_kernel_transpose else _to_time_major(x), nb, s, chunk,
            mod, mod_row0, h0r, h0i, cb0, mixer_weights)
        new_conv = cbo.reshape(2, nb, CONV_DIM).transpose(1, 0, 2)[None]
        new_re = hro.reshape(1, nb, SSM_GROUPS, SSM_STATE)
        new_im = hio.reshape(1, nb, SSM_GROUPS, SSM_STATE)
        return x1, new_conv, new_re, new_im

    zero_state = jnp.zeros((bp, N_STATE), F32)
    zero_conv = jnp.zeros((2 * bp, CONV_DIM), F32)
    x1p, conv_p, re_p, im_p = mix(x_prompt, bs, zero_state, zero_state, zero_conv, TOK_BLOCK // bp, True)
    x1s, conv_s, re_s, im_s = mix(
        x_sample, 0,
        state_ssm_re[0].reshape(bs, N_STATE), state_ssm_im[0].reshape(bs, N_STATE),
        state_conv[0].transpose(1, 0, 2).reshape(2 * bs, CONV_DIM), ss, False)

    assert bs * ss == TOK_BLOCK and (bp * sp) % TOK_BLOCK == 0 and TOK_BLOCK % bp == 0

    y_p, y_s = _moe(x1p, x1s, bp, mod_rows, *moe_weights)
    return (y_p, _from_time_major(y_s, bs, ss), conv_p, re_p, im_p, conv_s, re_s, im_s)
```

```python
import functools

import jax
import jax.numpy as jnp
from jax import lax
from jax.experimental import pallas as pl
from jax.experimental.pallas import tpu as pltpu

D_MODEL = 1024
CONV_DIM = 512
SSM_DIM = 512
SSM_GROUP = 16
SSM_GROUPS = 32
SSM_STATE = 64
N_STATE = SSM_GROUPS * SSM_STATE
HALF_SSM = SSM_DIM // 2
HALF_STATE = N_STATE // 2
IN_DIM = 2048
N_EXPERT_GROUPS = 4
EXPERTS_PER_GROUP = 8
N_EXPERTS = 32
D_EXPERT = 256
N_MOD = 6
EPS = 1e-6

SUBLANES = 8
LANES = 128
SCAN_COLS = 512
STRIDE_PAD = 8
VMEM_LIMIT = 60 * 1024 * 1024

F32 = jnp.float32
BF16 = jnp.bfloat16


def _sigmoid(x):
    return 1.0 / (1.0 + jnp.exp(-x))


def _gelu_tanh(x):
    return 0.5 * x * (1.0 + jnp.tanh(0.7978845608028654 * (x + 0.044715 * (x * x * x))))


def _rms(x, g):
    return x * lax.rsqrt(jnp.mean(x * x, axis=-1, keepdims=True) + EPS) * g


def _bdot(a, b):
    return jnp.dot(a.astype(BF16), b.astype(BF16), preferred_element_type=F32)


def _per_seq(v, nb, scale, shift=None):
    rows, d = v.shape
    v3 = v.reshape(rows // nb, nb, d) * scale[None]
    if shift is not None:
        v3 = v3 + shift[None]
    return v3.reshape(rows, d)


def _disc_kernel(lr_ref, li_ref, ldt_ref, brt_ref, bit_ref, ar_ref, ai_ref, bbr_ref, bbi_ref):
    lr = lr_ref[...]
    li = li_ref[...]
    dt = jnp.exp(ldt_ref[...])
    mag = jnp.exp(lr * dt)
    ar = mag * jnp.cos(li * dt)
    ai = mag * jnp.sin(li * dt)
    den = lr * lr + li * li
    fr = ((ar - 1.0) * lr + ai * li) / den
    fi = (ai * lr - (ar - 1.0) * li) / den
    ar_ref[...] = ar
    ai_ref[...] = ai
    brt = brt_ref[...]
    bit = bit_ref[...]
    bbr_ref[...] = fr * brt - fi * bit
    bbi_ref[...] = fr * bit + fi * brt


def _discretise(lambda_re, lambda_im, log_dt, b_re, b_im):
    lr = lambda_re.reshape(1, N_STATE)
    li = lambda_im.reshape(1, N_STATE)
    ldt = jnp.repeat(log_dt, SSM_STATE).reshape(1, N_STATE)
    brt = b_re.transpose(2, 0, 1).reshape(SSM_GROUP, N_STATE)
    bit = b_im.transpose(2, 0, 1).reshape(SSM_GROUP, N_STATE)
    row = jax.ShapeDtypeStruct((1, N_STATE), F32)
    mat = jax.ShapeDtypeStruct((SSM_GROUP, N_STATE), F32)
    return pl.pallas_call(_disc_kernel, out_shape=(row, row, mat, mat), name="s5_discretise")(
        lr, li, ldt, brt, bit)


def _ada_kernel(n_first, block_rows, c_ref, w_ref, b_ref, o_ref, rows_ref):
    c = c_ref[...]
    o = _bdot(c * _sigmoid(c), w_ref[...]) + b_ref[...]
    o_ref[...] = o

    @pl.when(pl.program_id(0) >= N_MOD // 2)
    def _():
        n_rest = o.shape[0] - n_first
        rows_ref[0, 0] = jnp.tile(o[n_first:], (block_rows // n_rest, 1))
        rows_ref[0, 1] = jnp.tile(o[:n_first], (block_rows // n_first, 1))


def _adaln(c_all, n_first, block_rows, w_ada, b_ada):
    n = c_all.shape[0]
    half = N_MOD // 2
    return pl.pallas_call(
        functools.partial(_ada_kernel, n_first, block_rows),
        grid=(N_MOD,),
        in_specs=[
            pl.BlockSpec((n, D_MODEL), lambda j: (0, 0)),
            pl.BlockSpec((D_MODEL, D_MODEL), lambda j: (0, j)),
            pl.BlockSpec((1, D_MODEL), lambda j: (0, j)),
        ],
        out_specs=[pl.BlockSpec((n, D_MODEL), lambda j: (0, j)),
                   pl.BlockSpec((1, 2, block_rows, D_MODEL), lambda j: (jnp.maximum(j - half, 0), 0, 0, 0))],
        out_shape=[jax.ShapeDtypeStruct((n, N_MOD * D_MODEL), F32),
                   jax.ShapeDtypeStruct((half, 2, block_rows, D_MODEL), F32)],
        compiler_params=pltpu.CompilerParams(
            dimension_semantics=("arbitrary",), vmem_limit_bytes=VMEM_LIMIT),
        name="adaln",
    )(c_all, w_ada, b_ada.reshape(1, -1))


def _rows_to_time_major(x_ref, slab, xt_scr):
    nb, steps, d = x_ref.shape
    pitch = slab.shape[1] // nb
    for b in range(nb):
        for j in range(d // LANES):
            slab[j, b * pitch:b * pitch + steps, :] = x_ref[b, :, j * LANES:(j + 1) * LANES]
    for l in range(steps):
        for j in range(d // LANES):
            xt_scr[l * nb:(l + 1) * nb, j * LANES:(j + 1) * LANES] = slab[j, pl.ds(l, nb, stride=pitch), :]


def _group_diagonal(w_t, k):
    tiled = jnp.tile(w_t[:, k * HALF_STATE:(k + 1) * HALF_STATE], (SSM_GROUPS // 2, 1))
    row_group = lax.shift_right_logical(lax.broadcasted_iota(jnp.int32, tiled.shape, 0),
                                        SSM_GROUP.bit_length() - 1)
    lane_group = lax.shift_right_logical(lax.broadcasted_iota(jnp.int32, tiled.shape, 1),
                                         SSM_STATE.bit_length() - 1)
    return jnp.where(row_group == lane_group, tiled, 0.0).astype(BF16)


def _mixer_kernel(nb, steps, batch_major,
                  x_ref, sh_ref, sc_ref, gt_ref, h0r_ref, h0i_ref, cb0_ref,
                  n1g_ref, win_ref, cw_ref, ar_ref, ai_ref, bbr_ref, bbi_ref, crt_ref, cit_ref,
                  dsk_ref, wglu_ref, bglu_ref, gcv_ref, gsm_ref, wout_ref,
                  x1_ref, cbo_ref, hro_ref, hio_ref,
                  z_scr, cv_scr, s_scr, hr_scr, hi_scr, win_b, wglu_b, wout_b, wb_ref, wcr_ref, wci_ref,
                  *tm_scr):
    rows = nb * steps
    i = pl.program_id(0)

    @pl.when(i == 0)
    def _():
        hr_scr[...] = h0r_ref[...]
        hi_scr[...] = h0i_ref[...]
        cv_scr[0:2 * nb, :] = cb0_ref[...]
        win_b[...] = win_ref[...].astype(BF16)
        wglu_b[...] = wglu_ref[...].astype(BF16)
        wout_b[...] = wout_ref[...].astype(BF16)
        for k in range(2):
            wb_ref[k, :, 0:HALF_STATE] = _group_diagonal(bbr_ref[...], k)
            wb_ref[k, :, HALF_STATE:2 * HALF_STATE] = _group_diagonal(bbi_ref[...], k)
            wcr_ref[k] = _group_diagonal(crt_ref[...], k)
            wci_ref[k] = _group_diagonal(cit_ref[...], k)

    if batch_major:
        slab, x_tm = tm_scr
        _rows_to_time_major(x_ref, slab, x_tm)
    else:
        x_tm = x_ref
    h = _per_seq(_rms(x_tm[...], n1g_ref[...]), nb, 1.0 + sc_ref[...], sh_ref[...])
    z_scr[...] = _bdot(h, win_b[...])

    u_cols = 3 * CONV_DIM
    for k in range(2):
        s_scr[:, 2 * k * HALF_STATE:2 * (k + 1) * HALF_STATE] = _bdot(
            z_scr[:, u_cols + k * HALF_SSM:u_cols + (k + 1) * HALF_SSM], wb_ref[k])

    def scan_cols(cg):
        cols = slice(cg * SCAN_COLS, (cg + 1) * SCAN_COLS)
        half, off = divmod(cg * SCAN_COLS, HALF_STATE)
        re_cols = slice(2 * half * HALF_STATE + off, 2 * half * HALF_STATE + off + SCAN_COLS)
        im_cols = slice(re_cols.start + HALF_STATE, re_cols.stop + HALF_STATE)
        a_r = jnp.broadcast_to(ar_ref[:, cols], (SUBLANES, SCAN_COLS))
        a_i = jnp.broadcast_to(ai_ref[:, cols], (SUBLANES, SCAN_COLS))

        def advance(hr, hi, rows_l):
            nr = a_r * hr - a_i * hi + s_scr[rows_l, re_cols]
            ni = a_r * hi + a_i * hr + s_scr[rows_l, im_cols]
            s_scr[rows_l, re_cols] = nr
            s_scr[rows_l, im_cols] = ni
            return nr, ni

        for r0 in range(0, nb, SUBLANES):
            tile = slice(r0, r0 + SUBLANES)
            hr, hi = hr_scr[tile, cols], hi_scr[tile, cols]
            for l in range(steps):
                hr, hi = advance(hr, hi, slice(l * nb + r0, l * nb + r0 + SUBLANES))
            hr_scr[tile, cols] = hr
            hi_scr[tile, cols] = hi

    y_halves = []
    for k in range(2):
        for cg in range(k * HALF_STATE // SCAN_COLS, (k + 1) * HALF_STATE // SCAN_COLS):
            scan_cols(cg)
        s_re = s_scr[:, 2 * k * HALF_STATE:(2 * k + 1) * HALF_STATE].astype(BF16)
        s_im = s_scr[:, (2 * k + 1) * HALF_STATE:(2 * k + 2) * HALF_STATE].astype(BF16)
        y_halves.append(lax.dot_general(s_re, wcr_ref[k], _NT, preferred_element_type=F32)
                        - lax.dot_general(s_im, wci_ref[k], _NT, preferred_element_type=F32))

    hro_ref[...] = hr_scr[...]
    hio_ref[...] = hi_scr[...]

    y = jnp.concatenate(y_halves, axis=-1) + dsk_ref[...] * z_scr[:, u_cols:IN_DIM]
    g = _gelu_tanh(y)
    y_ssm = _rms(g * _sigmoid(_bdot(g, wglu_b[...]) + bglu_ref[...]), gsm_ref[...])

    cv_scr[2 * nb:2 * nb + rows, :] = z_scr[:, CONV_DIM:2 * CONV_DIM] * z_scr[:, 2 * CONV_DIM:3 * CONV_DIM]
    conv = (cv_scr[0:rows, :] * cw_ref[0:1, :]
            + cv_scr[nb:nb + rows, :] * cw_ref[1:2, :]
            + cv_scr[2 * nb:2 * nb + rows, :] * cw_ref[2:3, :])
    y_conv = _rms(z_scr[:, 0:CONV_DIM] * conv, gcv_ref[...])
    carry = cv_scr[rows:rows + 2 * nb, :]
    cv_scr[0:2 * nb, :] = carry
    cbo_ref[...] = carry

    mixed = _bdot(y_conv, wout_b[0:CONV_DIM, :]) + _bdot(y_ssm, wout_b[CONV_DIM:2 * CONV_DIM, :])
    x1_ref[...] = x_tm[...] + _per_seq(mixed, nb, gt_ref[...])


def _const_spec(shape):
    return pl.BlockSpec(shape, lambda i: (0,) * len(shape), pipeline_mode=pl.Buffered(1))


def _mixer(x, nb, steps_total, chunk, mod, mod_row0, h0r, h0i, cb0, weights):
    rows = nb * chunk
    nchunks = steps_total // chunk
    batch_major = x.ndim == 3
    if batch_major:
        assert nb == SUBLANES
        x_spec = pl.BlockSpec((nb, chunk, D_MODEL), lambda i: (0, i, 0))
        tm_scratch = [pltpu.VMEM((D_MODEL // LANES, nb * (chunk + STRIDE_PAD), LANES), F32),
                      pltpu.VMEM((rows, D_MODEL), F32)]
    else:
        x_spec = pl.BlockSpec((rows, D_MODEL), lambda i: (i, 0))
        tm_scratch = []
    assert mod_row0 % nb == 0

    def mod_spec(k):
        return pl.BlockSpec((nb, D_MODEL), lambda i: (mod_row0 // nb, k), pipeline_mode=pl.Buffered(1))

    in_specs = [
        x_spec, mod_spec(0), mod_spec(1), mod_spec(2),
        _const_spec((nb, N_STATE)), _const_spec((nb, N_STATE)), _const_spec((2 * nb, CONV_DIM)),
    ] + [_const_spec(w.shape) for w in weights]
    out_specs = [
        pl.BlockSpec((rows, D_MODEL), lambda i: (i, 0)),
        _const_spec((2 * nb, CONV_DIM)), _const_spec((nb, N_STATE)), _const_spec((nb, N_STATE)),
    ]
    out_shape = [
        jax.ShapeDtypeStruct((steps_total * nb, D_MODEL), F32),
        jax.ShapeDtypeStruct((2 * nb, CONV_DIM), F32),
        jax.ShapeDtypeStruct((nb, N_STATE), F32),
        jax.ShapeDtypeStruct((nb, N_STATE), F32),
    ]
    scratch = [
        pltpu.VMEM((rows, IN_DIM), F32),
        pltpu.VMEM((rows + 2 * nb, CONV_DIM), F32),
        pltpu.VMEM((rows, 2 * N_STATE), F32),
        pltpu.VMEM((nb, N_STATE), F32),
        pltpu.VMEM((nb, N_STATE), F32),
        pltpu.VMEM((D_MODEL, IN_DIM), BF16),
        pltpu.VMEM((SSM_DIM, SSM_DIM), BF16),
        pltpu.VMEM((D_MODEL, D_MODEL), BF16),
        pltpu.VMEM((2, HALF_SSM, 2 * HALF_STATE), BF16),
        pltpu.VMEM((2, HALF_SSM, HALF_STATE), BF16),
        pltpu.VMEM((2, HALF_SSM, HALF_STATE), BF16),
    ] + tm_scratch
    return pl.pallas_call(
        functools.partial(_mixer_kernel, nb, chunk, batch_major),
        grid=(nchunks,),
        in_specs=in_specs, out_specs=out_specs, out_shape=out_shape,
        scratch_shapes=scratch,
        compiler_params=pltpu.CompilerParams(
            dimension_semantics=("arbitrary",), vmem_limit_bytes=VMEM_LIMIT),
        name="mixer",
    )(x, mod, mod, mod, h0r, h0i, cb0, *weights)


TOK_BLOCK = 512
SEG_ALIGN = 16
ROW_TILE = 512
CHUNKS_PER_TILE = ROW_TILE // SEG_ALIGN
SCATTER_GROUPS = 8
GATHER_BUFFERS = 3
TABLE_CHUNK_ROWS = 64
N_SLOTS = -(-(2 * TOK_BLOCK + N_EXPERTS * (SEG_ALIGN - 1)) // 256) * 256
SPARE_ROW = N_SLOTS - SEG_ALIGN
assert SPARE_ROW >= 2 * TOK_BLOCK + N_EXPERTS * (SEG_ALIGN - 1)
ROUTER_ROWS = 48
META_ROWS = SUBLANES

_NT = (((1,), (1,)), ((), ()))
_TN = (((0,), (0,)), ((), ()))


def _iota_f32(shape, dim):
    return lax.broadcasted_iota(jnp.int32, shape, dim).astype(F32)


def _split_bf16(v):
    hi = v.astype(BF16)
    return hi, (v - hi.astype(F32)).astype(BF16)


def _route_t(lg):
    row = _iota_f32(lg.shape, 0)
    neg = jnp.float32(-jnp.inf)
    none = float(ROUTER_ROWS)
    is_g = row < N_EXPERT_GROUPS
    gl = jnp.where(is_g, lg, neg)
    gmax = jnp.max(gl, axis=0, keepdims=True)
    gsum = jnp.sum(jnp.where(is_g, jnp.exp(gl - gmax), 0.0), axis=0, keepdims=True)
    g_prob = 1.0 / gsum
    g_idx = jnp.min(jnp.where(gl == gmax, row, none), axis=0, keepdims=True)
    lo = N_EXPERT_GROUPS + EXPERTS_PER_GROUP * g_idx
    el = jnp.where(row >= lo, jnp.where(row < lo + EXPERTS_PER_GROUP, lg, neg), neg)
    m1 = jnp.max(el, axis=0, keepdims=True)
    i1 = jnp.min(jnp.where(el == m1, row, none), axis=0, keepdims=True)
    el2 = jnp.where(row == i1, neg, el)
    m2 = jnp.max(el2, axis=0, keepdims=True)
    i2 = jnp.min(jnp.where(el2 == m2, row, none), axis=0, keepdims=True)
    e21 = jnp.exp(m2 - m1)
    return i1, i2, g_prob / (1.0 + e21), g_prob * e21 / (1.0 + e21)


def _dispatch_kernel(nbp, x1p_ref, x1s_ref, sc_ref, sh_ref, n2g_ref, wr_ref, br_ref,
                     xs_ref, meta_ref, cnt_ref):
    x1 = jnp.where(pl.program_id(0) < nbp, x1p_ref[...], x1s_ref[...])
    h = _rms(x1, n2g_ref[...]) * (1.0 + sc_ref[0, 0]) + sh_ref[0, 0]
    t = h.shape[0]
    hb, hl = _split_bf16(h)
    wh, wl = _split_bf16(wr_ref[...])
    lg = (lax.dot_general(wh, hb, _NT, preferred_element_type=F32)
          + lax.dot_general(wh, hl, _NT, preferred_element_type=F32)
          + lax.dot_general(wl, hb, _NT, preferred_element_type=F32)) + br_ref[...]
    i1, i2, w1, w2 = _route_t(lg)

    row = _iota_f32((ROUTER_ROWS, t), 0)
    hit1 = row == i1
    hit2 = row == i2
    onehot = jnp.where(hit1, 1.0, 0.0) + jnp.where(hit2, 1.0, 0.0)
    before = jnp.where(_iota_f32((t, t), 0) < _iota_f32((t, t), 1), 1.0, 0.0)
    rank = _bdot(onehot, before)
    cnt = jnp.sum(onehot, axis=1, keepdims=True)
    seg = jnp.floor((cnt + (SEG_ALIGN - 1)) * (1.0 / SEG_ALIGN)) * SEG_ALIGN
    earlier = jnp.where(_iota_f32((ROUTER_ROWS, ROUTER_ROWS), 1) < _iota_f32((ROUTER_ROWS, ROUTER_ROWS), 0),
                        1.0, 0.0)
    base = rank + _bdot(earlier, jnp.broadcast_to(seg, (ROUTER_ROWS, t)))
    slot1 = jnp.sum(jnp.where(hit1, base, 0.0), axis=0, keepdims=True)
    slot2 = jnp.sum(jnp.where(hit2, base, 0.0), axis=0, keepdims=True)
    j = _iota_f32((N_SLOTS, t), 0)
    place = jnp.where(j == slot1, 1.0, 0.0) + jnp.where(j == slot2, 1.0, 0.0)
    xs_ref[...] = jnp.dot(place.astype(BF16), hb, preferred_element_type=F32).astype(BF16)
    r = _iota_f32((META_ROWS, t), 0)
    meta_ref[0] = jnp.where(r == 0.0, slot1, jnp.where(r == 1.0, slot2,
                            jnp.where(r == 2.0, w1, jnp.where(r == 3.0, w2, 0.0))))
    cnt_ref[0] = jnp.broadcast_to(cnt, (ROUTER_ROWS, LANES))


def _combine_kernel(nbp, x1p_ref, x1s_ref, gt_ref, meta_ref, ys_ref, fg_ref, yp_ref, ysm_ref, slab):
    i = pl.program_id(0)
    x1 = jnp.where(i < nbp, x1p_ref[...], x1s_ref[...])
    meta = meta_ref[0]
    slot1, slot2, w1, w2 = (meta[k:k + 1, :] for k in range(4))
    j = _iota_f32((N_SLOTS, x1.shape[0]), 0)
    pick = jnp.where(j == slot1, w1, 0.0) + jnp.where(j == slot2, w2, 0.0)
    moe = lax.dot_general(pick.astype(BF16), ys_ref[...], _TN, preferred_element_type=F32)
    y = _rms(x1 + gt_ref[0, 0] * moe, fg_ref[...])

    @pl.when(i < nbp)
    def _():
        nb, steps, d = yp_ref.shape
        for j in range(d // LANES):
            slab[j] = y[:, j * LANES:(j + 1) * LANES]
        for b in range(nb):
            for j in range(d // LANES):
                yp_ref[b, :, j * LANES:(j + 1) * LANES] = slab[j, pl.ds(b, steps, stride=nb), :]

    @pl.when(i >= nbp)
    def _():
        ysm_ref[...] = y


def _chunk_table_kernel(start_ref, m_ref, src_ref, out_ref):
    start = start_ref[...]
    end = start + m_ref[...]
    src = src_ref[...]

    def body(b, _):
        rows = pl.ds(pl.multiple_of(b * TABLE_CHUNK_ROWS, TABLE_CHUNK_ROWS), TABLE_CHUNK_ROWS)
        first = lax.convert_element_type(b * TABLE_CHUNK_ROWS, F32)
        i = _iota_f32((TABLE_CHUNK_ROWS, start.shape[1]), 0) + first
        val = jnp.where(i >= start, jnp.where(i < end, src + SEG_ALIGN * (i - start), 0.0), 0.0)
        out_ref[rows, :] = jnp.sum(val, axis=1, keepdims=True).astype(jnp.int32)
        return 0

    lax.fori_loop(0, out_ref.shape[0] // TABLE_CHUNK_ROWS, body, 0)


def _expert_kernel(te_ref, tnv_ref, tc0_ref, ctab_ref, nt_ref, xs_hbm, wg_ref, wu_ref, wd_ref,
                   ys_hbm, xbuf, ybuf, wgub, wdb, wsel, in_sem, out_sem):
    t = pl.program_id(0) - 1
    nt = nt_ref[0]
    slot = t % 2
    xslot = t % GATHER_BUFFERS

    def chunk_rows(j):
        return pl.ds(pl.multiple_of(j * SEG_ALIGN, SEG_ALIGN), SEG_ALIGN)

    def hbm_rows(ref, r):
        return ref.at[pl.ds(pl.multiple_of(r, SEG_ALIGN), SEG_ALIGN), :]

    def in_copy(r, j, s):
        return pltpu.make_async_copy(hbm_rows(xs_hbm, r), xbuf.at[s, chunk_rows(j), :], in_sem.at[s])

    def out_copy(r, j, s):
        return pltpu.make_async_copy(ybuf.at[s, chunk_rows(j), :], hbm_rows(ys_hbm, r), out_sem.at[s])

    def start_gather(tile, s):
        c0 = tc0_ref[tile]
        nv = tnv_ref[tile]
        for j in range(CHUNKS_PER_TILE):
            in_copy(jnp.where(j < nv, ctab_ref[c0 + j], SPARE_ROW), j, s).start()

    def wait_gather(s):
        pltpu.make_async_copy(xs_hbm.at[pl.ds(0, ROW_TILE), :], xbuf.at[s], in_sem.at[s]).wait()

    def wait_scatter(tile, s):
        nv = tnv_ref[tile]

        @pl.when(nv == CHUNKS_PER_TILE)
        def _():
            pltpu.make_async_copy(ybuf.at[s], ys_hbm.at[pl.ds(0, ROW_TILE), :], out_sem.at[s]).wait()

        @pl.when(nv < CHUNKS_PER_TILE)
        def _():
            def body(j, _):
                out_copy(0, j, s).wait()
                return 0

            lax.fori_loop(0, nv, body, 0)

    def run_tile(full):
        ahead = t + GATHER_BUFFERS - 1
        start_gather(ahead, ahead % GATHER_BUFFERS)
        wait_gather(xslot)
        cur = wsel[0]
        ab = jnp.dot(xbuf[xslot], wgub[cur], preferred_element_type=F32)
        a, b = ab[:, :D_EXPERT], ab[:, D_EXPERT:]
        o = jnp.dot((a * _sigmoid(a) * b).astype(BF16), wdb[cur], preferred_element_type=F32)
        c0 = tc0_ref[t]
        if full:
            group = ROW_TILE // SCATTER_GROUPS
            for r in range(0, ROW_TILE, group):
                ybuf[slot, r:r + group, :] = o[r:r + group].astype(BF16)
                for j in range(r // SEG_ALIGN, (r + group) // SEG_ALIGN):
                    out_copy(ctab_ref[c0 + j], j, slot).start()
        else:
            ybuf[slot] = o.astype(BF16)

            def body(j, _):
                out_copy(ctab_ref[c0 + j], j, slot).start()
                return 0

            lax.fori_loop(0, tnv_ref[t], body, 0)

    def stage_weights(w):
        wgub[w, :, :D_EXPERT] = wg_ref[0].astype(BF16)
        wgub[w, :, D_EXPERT:] = wu_ref[0].astype(BF16)
        wdb[w] = wd_ref[0].astype(BF16)

    @pl.when(t < 0)
    def _():
        wsel[0] = 0
        stage_weights(0)
        for k in range(GATHER_BUFFERS - 1):
            start_gather(k, k)

    @pl.when(jnp.logical_and(t >= 0, t < nt))
    def _():
        @pl.when(jnp.logical_and(t >= 1, te_ref[t] != te_ref[jnp.maximum(t - 1, 0)]))
        def _():
            wsel[0] = 1 - wsel[0]

        @pl.when(t >= 2)
        def _():
            wait_scatter(t - 2, slot)

        pl.when(tnv_ref[t] == CHUNKS_PER_TILE)(lambda: run_tile(True))
        pl.when(tnv_ref[t] < CHUNKS_PER_TILE)(lambda: run_tile(False))

        @pl.when(te_ref[t + 1] != te_ref[t])
        def _():
            stage_weights(1 - wsel[0])

        @pl.when(t == nt - 1)
        def _():
            for k in range(1, GATHER_BUFFERS):
                wait_gather((t + k) % GATHER_BUFFERS)
            wait_scatter(t, slot)

            @pl.when(t >= 1)
            def _():
                wait_scatter(t - 1, 1 - slot)


def _tile_plan(cnt):
    nblk = cnt.shape[0]
    max_chunks = nblk * (2 * TOK_BLOCK // SEG_ALIGN + N_EXPERTS)
    max_tiles = max_chunks // CHUNKS_PER_TILE + N_EXPERTS
    table_len = -(-(max_chunks + CHUNKS_PER_TILE) // LANES) * LANES
    m = (cnt + SEG_ALIGN - 1) // SEG_ALIGN
    seg_start = jnp.cumsum(m, axis=1) - m
    src0 = jnp.arange(nblk, dtype=jnp.int32)[:, None] * N_SLOTS + SEG_ALIGN * seg_start
    seg_m = m.T.reshape(1, -1)
    seg_first = jnp.cumsum(seg_m, axis=1) - seg_m
    assert table_len % TABLE_CHUNK_ROWS == 0

    def row(v):
        return jnp.pad(v.astype(F32), ((0, 0), (0, -v.shape[1] % LANES)))

    chunk_src = pl.pallas_call(
        _chunk_table_kernel,
        out_shape=jax.ShapeDtypeStruct((table_len, 1), jnp.int32),
        name="moe_chunk_table",
    )(row(seg_first), row(seg_m), row(src0.T.reshape(1, -1))).reshape(-1)
    ce = jnp.sum(m, axis=0)
    c_start = jnp.cumsum(ce) - ce
    tiles = (ce + CHUNKS_PER_TILE - 1) // CHUNKS_PER_TILE
    t_cum = jnp.cumsum(tiles)
    nt = t_cum[-1]
    k = jnp.arange(max_tiles + GATHER_BUFFERS - 1, dtype=jnp.int32)
    kk = jnp.minimum(k, nt - 1)
    ek = jnp.minimum(jnp.sum((t_cum[None, :] <= kk[:, None]).astype(jnp.int32), axis=1), N_EXPERTS - 1)
    is_e = ek[:, None] == jnp.arange(N_EXPERTS, dtype=jnp.int32)[None, :]
    pick = lambda v: jnp.sum(jnp.where(is_e, v[None, :], 0), axis=1)
    local = kk - pick(t_cum - tiles)
    tile_nv = jnp.where(k < nt, jnp.clip(pick(ce) - CHUNKS_PER_TILE * local, 0, CHUNKS_PER_TILE), 0)
    tile_c0 = jnp.where(k < nt, pick(c_start) + CHUNKS_PER_TILE * local, 0)
    return tuple(v.astype(jnp.int32) for v in (ek, tile_nv, tile_c0, chunk_src, nt.reshape(1)))


def _moe(x1p, x1s, bp, mod_rows, n2g, wr, br, wg, wu, wd, fg):
    nbp = x1p.shape[0] // TOK_BLOCK
    nblk = nbp + 1
    blk = (TOK_BLOCK, D_MODEL)
    p_spec = pl.BlockSpec(blk, lambda i: (jnp.minimum(i, nbp - 1), 0))
    s_spec = pl.BlockSpec(blk, lambda i: (0, 0))

    def mod_spec(k):
        return pl.BlockSpec((1, 1) + blk, lambda i: (k, jnp.minimum(i // nbp, 1), 0, 0))

    sh_spec, sc_spec, gt_spec = mod_spec(0), mod_spec(1), mod_spec(2)
    meta_spec = pl.BlockSpec((1, META_ROWS, TOK_BLOCK), lambda i: (i, 0, 0))
    slots_spec = pl.BlockSpec((N_SLOTS, D_MODEL), lambda i: (i, 0))
    params = pltpu.CompilerParams(dimension_semantics=("arbitrary",), vmem_limit_bytes=VMEM_LIMIT)

    def cs(shape):
        return pl.BlockSpec(shape, lambda i: (0,) * len(shape))

    xs, meta, cnt = pl.pallas_call(
        functools.partial(_dispatch_kernel, nbp),
        grid=(nblk,),
        in_specs=[p_spec, s_spec, sc_spec, sh_spec, cs((1, D_MODEL)),
                  cs((ROUTER_ROWS, D_MODEL)), cs((ROUTER_ROWS, TOK_BLOCK))],
        out_specs=[slots_spec, meta_spec,
                   pl.BlockSpec((1, ROUTER_ROWS, LANES), lambda i: (i, 0, 0))],
        out_shape=[jax.ShapeDtypeStruct((nblk * N_SLOTS, D_MODEL), BF16),
                   jax.ShapeDtypeStruct((nblk, META_ROWS, TOK_BLOCK), F32),
                   jax.ShapeDtypeStruct((nblk, ROUTER_ROWS, LANES), F32)],
        compiler_params=params,
        name="moe_dispatch",
    )(x1p, x1s, mod_rows, mod_rows, n2g, wr, br)

    plan = _tile_plan(cnt[:, N_EXPERT_GROUPS:N_EXPERT_GROUPS + N_EXPERTS, 0].astype(jnp.int32))
    max_tiles = plan[0].shape[0] - (GATHER_BUFFERS - 1)

    def w_spec(shape):
        return pl.BlockSpec((1,) + shape, lambda t, te, *_: (te[t], 0, 0))

    ys = pl.pallas_call(
        _expert_kernel,
        grid_spec=pltpu.PrefetchScalarGridSpec(
            num_scalar_prefetch=len(plan),
            grid=(max_tiles + 1,),
            in_specs=[pl.BlockSpec(memory_space=pl.ANY),
                      w_spec((D_MODEL, D_EXPERT)), w_spec((D_MODEL, D_EXPERT)), w_spec((D_EXPERT, D_MODEL))],
            out_specs=pl.BlockSpec(memory_space=pl.ANY),
            scratch_shapes=[pltpu.VMEM((GATHER_BUFFERS, ROW_TILE, D_MODEL), BF16),
                            pltpu.VMEM((2, ROW_TILE, D_MODEL), BF16),
                            pltpu.VMEM((2, D_MODEL, 2 * D_EXPERT), BF16),
                            pltpu.VMEM((2, D_EXPERT, D_MODEL), BF16),
                            pltpu.SMEM((1,), jnp.int32),
                            pltpu.SemaphoreType.DMA((GATHER_BUFFERS,)), pltpu.SemaphoreType.DMA((2,))]),
        out_shape=jax.ShapeDtypeStruct(xs.shape, BF16),
        input_output_aliases={len(plan): 0},
        compiler_params=params,
        name="moe_experts",
    )(*plan, xs, wg, wu, wd)

    return pl.pallas_call(
        functools.partial(_combine_kernel, nbp),
        grid=(nblk,),
        in_specs=[p_spec, s_spec, gt_spec, meta_spec, slots_spec, cs((1, D_MODEL))],
        out_specs=[pl.BlockSpec((bp, TOK_BLOCK // bp, D_MODEL), lambda i: (0, jnp.minimum(i, nbp - 1), 0)),
                   s_spec],
        out_shape=[jax.ShapeDtypeStruct((bp, x1p.shape[0] // bp, D_MODEL), F32),
                   jax.ShapeDtypeStruct(x1s.shape, F32)],
        scratch_shapes=[pltpu.VMEM((D_MODEL // LANES, TOK_BLOCK, LANES), F32)],
        compiler_params=params,
        name="moe_combine",
    )(x1p, x1s, mod_rows, meta, ys, fg)


def _to_time_major(x):
    b, s, d = x.shape
    return x.transpose(1, 0, 2).reshape(s * b, d)


def _from_time_major(x, b, s):
    return x.reshape(s, b, -1).transpose(1, 0, 2)


def kernel(x_prompt, x_sample, state_conv, state_ssm_re, state_ssm_im, c_prompt, c_sample, w_ada, b_ada, norm1_g, norm2_g, w_in, conv_w, lambda_re, lambda_im, log_dt, ssm_b_re, ssm_b_im, ssm_c_re, ssm_c_im, ssm_d, w_glu, b_glu, out_norm_conv_g, out_norm_ssm_g, w_out, w_router_group, b_router_group, w_router_expert, b_router_expert, w_expert_gate, w_expert_up, w_expert_down, final_norm_g):
    assert w_ada.shape[0] == 1, "single-layer step"
    bp, sp, _ = x_prompt.shape
    bs, ss, _ = x_sample.shape

    ar, ai, bbr, bbi = _discretise(lambda_re[0], lambda_im[0], log_dt[0], ssm_b_re[0], ssm_b_im[0])
    crt = ssm_c_re[0].transpose(1, 0, 2).reshape(SSM_GROUP, N_STATE)
    cit = ssm_c_im[0].transpose(1, 0, 2).reshape(SSM_GROUP, N_STATE)

    assert bs % bp == 0
    mod, mod_rows = _adaln(jnp.concatenate([c_sample, c_prompt], axis=0), bs, TOK_BLOCK, w_ada[0], b_ada[0])

    mixer_weights = (
        norm1_g[0].reshape(1, -1), w_in[0], conv_w[0], ar, ai, bbr, bbi, crt, cit,
        ssm_d[0].reshape(1, -1), w_glu[0], b_glu[0].reshape(1, -1),
        out_norm_conv_g[0].reshape(1, -1), out_norm_ssm_g[0].reshape(1, -1), w_out[0],
    )

    pad_rows = ROUTER_ROWS - N_EXPERT_GROUPS - N_EXPERTS
    w_router = jnp.concatenate(
        [w_router_group[0].T, w_router_expert[0].transpose(0, 2, 1).reshape(N_EXPERTS, D_MODEL),
         jnp.zeros((pad_rows, D_MODEL), F32)], axis=0)
    b_router = jnp.concatenate(
        [b_router_group[0], b_router_expert[0].reshape(-1), jnp.zeros((pad_rows,), F32)])
    b_router = jnp.broadcast_to(b_router[:, None], (ROUTER_ROWS, TOK_BLOCK))
    moe_weights = (norm2_g[0].reshape(1, -1), w_router, b_router,
                   w_expert_gate[0], w_expert_up[0], w_expert_down[0], final_norm_g.reshape(1, -1))

    def mix(x, mod_row0, h0r, h0i, cb0, chunk, in_kernel_transpose):
        nb, s, _ = x.shape
        x1, cbo, hro, hio = _mixer(
            x if in_kernel_transpose else _to_time_major(x), nb, s, chunk,
            mod, mod_row0, h0r, h0i, cb0, mixer_weights)
        new_conv = cbo.reshape(2, nb, CONV_DIM).transpose(1, 0, 2)[None]
        new_re = hro.reshape(1, nb, SSM_GROUPS, SSM_STATE)
        new_im = hio.reshape(1, nb, SSM_GROUPS, SSM_STATE)
        return x1, new_conv, new_re, new_im

    zero_state = jnp.zeros((bp, N_STATE), F32)
    zero_conv = jnp.zeros((2 * bp, CONV_DIM), F32)
    x1p, conv_p, re_p, im_p = mix(x_prompt, bs, zero_state, zero_state, zero_conv, TOK_BLOCK // bp, True)
    x1s, conv_s, re_s, im_s = mix(
        x_sample, 0,
        state_ssm_re[0].reshape(bs, N_STATE), state_ssm_im[0].reshape(bs, N_STATE),
        state_conv[0].transpose(1, 0, 2).reshape(2 * bs, CONV_DIM), ss, False)

    assert bs * ss == TOK_BLOCK and (bp * sp) % TOK_BLOCK == 0 and TOK_BLOCK % bp == 0

    y_p, y_s = _moe(x1p, x1s, bp, mod_rows, *moe_weights)
    return (y_p, _from_time_major(y_s, bs, ss), conv_p, re_p, im_p, conv_s, re_s, im_s)
```

```python
import functools

import jax
import jax.numpy as jnp
from jax import lax
from jax.experimental import pallas as pl
from jax.experimental.pallas import tpu as pltpu

D_MODEL = 1024
CONV_DIM = 512
SSM_DIM = 512
SSM_GROUP = 16
SSM_GROUPS = 32
SSM_STATE = 64
N_STATE = SSM_GROUPS * SSM_STATE
HALF_SSM = SSM_DIM // 2
HALF_STATE = N_STATE // 2
IN_DIM = 2048
N_EXPERT_GROUPS = 4
EXPERTS_PER_GROUP = 8
N_EXPERTS = 32
D_EXPERT = 256
N_MOD = 6
EPS = 1e-6

SUBLANES = 8
LANES = 128
SCAN_COLS = 512
STRIDE_PAD = 8
VMEM_LIMIT = 60 * 1024 * 1024

F32 = jnp.float32
BF16 = jnp.bfloat16


def _sigmoid(x):
    return 1.0 / (1.0 + jnp.exp(-x))


def _gelu_tanh(x):
    return 0.5 * x * (1.0 + jnp.tanh(0.7978845608028654 * (x + 0.044715 * (x * x * x))))


def _rms(x, g):
    return x * lax.rsqrt(jnp.mean(x * x, axis=-1, keepdims=True) + EPS) * g


def _bdot(a, b):
    return jnp.dot(a.astype(BF16), b.astype(BF16), preferred_element_type=F32)


def _per_seq(v, nb, scale, shift=None):
    rows, d = v.shape
    v3 = v.reshape(rows // nb, nb, d) * scale[None]
    if shift is not None:
        v3 = v3 + shift[None]
    return v3.reshape(rows, d)


def _disc_kernel(lr_ref, li_ref, ldt_ref, brt_ref, bit_ref, ar_ref, ai_ref, bbr_ref, bbi_ref):
    lr = lr_ref[...]
    li = li_ref[...]
    dt = jnp.exp(ldt_ref[...])
    mag = jnp.exp(lr * dt)
    ar = mag * jnp.cos(li * dt)
    ai = mag * jnp.sin(li * dt)
    den = lr * lr + li * li
    fr = ((ar - 1.0) * lr + ai * li) / den
    fi = (ai * lr - (ar - 1.0) * li) / den
    ar_ref[...] = ar
    ai_ref[...] = ai
    brt = brt_ref[...]
    bit = bit_ref[...]
    bbr_ref[...] = fr * brt - fi * bit
    bbi_ref[...] = fr * bit + fi * brt


def _discretise(lambda_re, lambda_im, log_dt, b_re, b_im):
    lr = lambda_re.reshape(1, N_STATE)
    li = lambda_im.reshape(1, N_STATE)
    ldt = jnp.repeat(log_dt, SSM_STATE).reshape(1, N_STATE)
    brt = b_re.transpose(2, 0, 1).reshape(SSM_GROUP, N_STATE)
    bit = b_im.transpose(2, 0, 1).reshape(SSM_GROUP, N_STATE)
    row = jax.ShapeDtypeStruct((1, N_STATE), F32)
    mat = jax.ShapeDtypeStruct((SSM_GROUP, N_STATE), F32)
    return pl.pallas_call(_disc_kernel, out_shape=(row, row, mat, mat), name="s5_discretise")(
        lr, li, ldt, brt, bit)


def _ada_kernel(n_first, block_rows, c_ref, w_ref, b_ref, o_ref, rows_ref):
    c = c_ref[...]
    o = _bdot(c * _sigmoid(c), w_ref[...]) + b_ref[...]
    o_ref[...] = o

    @pl.when(pl.program_id(0) >= N_MOD // 2)
    def _():
        n_rest = o.shape[0] - n_first
        rows_ref[0, 0] = jnp.tile(o[n_first:], (block_rows // n_rest, 1))
        rows_ref[0, 1] = jnp.tile(o[:n_first], (block_rows // n_first, 1))


def _adaln(c_all, n_first, block_rows, w_ada, b_ada):
    n = c_all.shape[0]
    half = N_MOD // 2
    return pl.pallas_call(
        functools.partial(_ada_kernel, n_first, block_rows),
        grid=(N_MOD,),
        in_specs=[
            pl.BlockSpec((n, D_MODEL), lambda j: (0, 0)),
            pl.BlockSpec((D_MODEL, D_MODEL), lambda j: (0, j)),
            pl.BlockSpec((1, D_MODEL), lambda j: (0, j)),
        ],
        out_specs=[pl.BlockSpec((n, D_MODEL), lambda j: (0, j)),
                   pl.BlockSpec((1, 2, block_rows, D_MODEL), lambda j: (jnp.maximum(j - half, 0), 0, 0, 0))],
        out_shape=[jax.ShapeDtypeStruct((n, N_MOD * D_MODEL), F32),
                   jax.ShapeDtypeStruct((half, 2, block_rows, D_MODEL), F32)],
        compiler_params=pltpu.CompilerParams(
            dimension_semantics=("arbitrary",), vmem_limit_bytes=VMEM_LIMIT),
        name="adaln",
    )(c_all, w_ada, b_ada.reshape(1, -1))


def _rows_to_time_major(x_ref, slab, xt_scr):
    nb, steps, d = x_ref.shape
    pitch = slab.shape[1] // nb
    for b in range(nb):
        for j in range(d // LANES):
            slab[j, b * pitch:b * pitch + steps, :] = x_ref[b, :, j * LANES:(j + 1) * LANES]
    for l in range(steps):
        for j in range(d // LANES):
            xt_scr[l * nb:(l + 1) * nb, j * LANES:(j + 1) * LANES] = slab[j, pl.ds(l, nb, stride=pitch), :]


def _group_diagonal(w_t, k):
    tiled = jnp.tile(w_t[:, k * HALF_STATE:(k + 1) * HALF_STATE], (SSM_GROUPS // 2, 1))
    row_group = lax.shift_right_logical(lax.broadcasted_iota(jnp.int32, tiled.shape, 0),
                                        SSM_GROUP.bit_length() - 1)
    lane_group = lax.shift_right_logical(lax.broadcasted_iota(jnp.int32, tiled.shape, 1),
                                         SSM_STATE.bit_length() - 1)
    return jnp.where(row_group == lane_group, tiled, 0.0).astype(BF16)


def _mixer_kernel(nb, steps, batch_major, has_dest, *refs):
    (x_ref, sh_ref, sc_ref, gt_ref, h0r_ref, h0i_ref, cb0_ref,
     n1g_ref, win_ref, cw_ref, ar_ref, ai_ref, bbr_ref, bbi_ref, crt_ref, cit_ref,
     dsk_ref, wglu_ref, bglu_ref, gcv_ref, gsm_ref, wout_ref,
     x1_ref, cbo_ref, hro_ref, hio_ref,
     z_scr, cv_scr, s_scr, hr_scr, hi_scr, win_b, wglu_b, wout_b, wb_ref, wcr_ref, wci_ref,
     *tm_scr) = refs[1:] if has_dest else refs
    rows = nb * steps
    i = pl.program_id(0)

    @pl.when(i == 0)
    def _():
        hr_scr[...] = h0r_ref[...]
        hi_scr[...] = h0i_ref[...]
        cv_scr[0:2 * nb, :] = cb0_ref[...]
        win_b[...] = win_ref[...].astype(BF16)
        wglu_b[...] = wglu_ref[...].astype(BF16)
        wout_b[...] = wout_ref[...].astype(BF16)
        for k in range(2):
            wb_ref[k, :, 0:HALF_STATE] = _group_diagonal(bbr_ref[...], k)
            wb_ref[k, :, HALF_STATE:2 * HALF_STATE] = _group_diagonal(bbi_ref[...], k)
            wcr_ref[k] = _group_diagonal(crt_ref[...], k)
            wci_ref[k] = _group_diagonal(cit_ref[...], k)

    if batch_major:
        slab, x_tm = tm_scr
        _rows_to_time_major(x_ref, slab, x_tm)
    else:
        x_tm = x_ref
    h = _per_seq(_rms(x_tm[...], n1g_ref[...]), nb, 1.0 + sc_ref[...], sh_ref[...])
    z_scr[...] = _bdot(h, win_b[...])

    u_cols = 3 * CONV_DIM
    for k in range(2):
        s_scr[:, 2 * k * HALF_STATE:2 * (k + 1) * HALF_STATE] = _bdot(
            z_scr[:, u_cols + k * HALF_SSM:u_cols + (k + 1) * HALF_SSM], wb_ref[k])

    def scan_cols(cg):
        cols = slice(cg * SCAN_COLS, (cg + 1) * SCAN_COLS)
        half, off = divmod(cg * SCAN_COLS, HALF_STATE)
        re_cols = slice(2 * half * HALF_STATE + off, 2 * half * HALF_STATE + off + SCAN_COLS)
        im_cols = slice(re_cols.start + HALF_STATE, re_cols.stop + HALF_STATE)
        a_r = jnp.broadcast_to(ar_ref[:, cols], (SUBLANES, SCAN_COLS))
        a_i = jnp.broadcast_to(ai_ref[:, cols], (SUBLANES, SCAN_COLS))

        def advance(hr, hi, rows_l):
            nr = a_r * hr - a_i * hi + s_scr[rows_l, re_cols]
            ni = a_r * hi + a_i * hr + s_scr[rows_l, im_cols]
            s_scr[rows_l, re_cols] = nr
            s_scr[rows_l, im_cols] = ni
            return nr, ni

        for r0 in range(0, nb, SUBLANES):
            tile = slice(r0, r0 + SUBLANES)
            hr, hi = hr_scr[tile, cols], hi_scr[tile, cols]
            for l in range(steps):
                hr, hi = advance(hr, hi, slice(l * nb + r0, l * nb + r0 + SUBLANES))
            hr_scr[tile, cols] = hr
            hi_scr[tile, cols] = hi

    y_halves = []
    for k in range(2):
        for cg in range(k * HALF_STATE // SCAN_COLS, (k + 1) * HALF_STATE // SCAN_COLS):
            scan_cols(cg)
        s_re = s_scr[:, 2 * k * HALF_STATE:(2 * k + 1) * HALF_STATE].astype(BF16)
        s_im = s_scr[:, (2 * k + 1) * HALF_STATE:(2 * k + 2) * HALF_STATE].astype(BF16)
        y_halves.append(lax.dot_general(s_re, wcr_ref[k], _NT, preferred_element_type=F32)
                        - lax.dot_general(s_im, wci_ref[k], _NT, preferred_element_type=F32))

    hro_ref[...] = hr_scr[...]
    hio_ref[...] = hi_scr[...]

    y = jnp.concatenate(y_halves, axis=-1) + dsk_ref[...] * z_scr[:, u_cols:IN_DIM]
    g = _gelu_tanh(y)
    y_ssm = _rms(g * _sigmoid(_bdot(g, wglu_b[...]) + bglu_ref[...]), gsm_ref[...])

    cv_scr[2 * nb:2 * nb + rows, :] = z_scr[:, CONV_DIM:2 * CONV_DIM] * z_scr[:, 2 * CONV_DIM:3 * CONV_DIM]
    conv = (cv_scr[0:rows, :] * cw_ref[0:1, :]
            + cv_scr[nb:nb + rows, :] * cw_ref[1:2, :]
            + cv_scr[2 * nb:2 * nb + rows, :] * cw_ref[2:3, :])
    y_conv = _rms(z_scr[:, 0:CONV_DIM] * conv, gcv_ref[...])
    carry = cv_scr[rows:rows + 2 * nb, :]
    cv_scr[0:2 * nb, :] = carry
    cbo_ref[...] = carry

    mixed = _bdot(y_conv, wout_b[0:CONV_DIM, :]) + _bdot(y_ssm, wout_b[CONV_DIM:2 * CONV_DIM, :])
    x1_ref[...] = x_tm[...] + _per_seq(mixed, nb, gt_ref[...])


def _const_spec(shape):
    return pl.BlockSpec(shape, lambda i: (0,) * len(shape), pipeline_mode=pl.Buffered(1))


def _mixer(x, nb, steps_total, chunk, mod, mod_row0, h0r, h0i, cb0, weights, x1_rows=None, x1_dest=None):
    rows = nb * chunk
    nchunks = steps_total // chunk
    batch_major = x.ndim == 3
    n_rows = steps_total * nb
    if x1_dest is not None:
        x1_rows = x1_dest.shape[0]
    elif x1_rows is None:
        x1_rows = n_rows
    first_block = (x1_rows - n_rows) // rows if x1_dest is not None else 0
    if batch_major:
        assert nb == SUBLANES
        x_spec = pl.BlockSpec((nb, chunk, D_MODEL), lambda i: (0, i, 0))
        tm_scratch = [pltpu.VMEM((D_MODEL // LANES, nb * (chunk + STRIDE_PAD), LANES), F32),
                      pltpu.VMEM((rows, D_MODEL), F32)]
    else:
        x_spec = pl.BlockSpec((rows, D_MODEL), lambda i: (i, 0))
        tm_scratch = []
    assert mod_row0 % nb == 0

    def mod_spec(k):
        return pl.BlockSpec((nb, D_MODEL), lambda i: (mod_row0 // nb, k), pipeline_mode=pl.Buffered(1))

    in_specs = [
        x_spec, mod_spec(0), mod_spec(1), mod_spec(2),
        _const_spec((nb, N_STATE)), _const_spec((nb, N_STATE)), _const_spec((2 * nb, CONV_DIM)),
    ] + [_const_spec(w.shape) for w in weights]
    out_specs = [
        pl.BlockSpec((rows, D_MODEL), lambda i: (first_block + i, 0)),
        _const_spec((2 * nb, CONV_DIM)), _const_spec((nb, N_STATE)), _const_spec((nb, N_STATE)),
    ]
    out_shape = [
        jax.ShapeDtypeStruct((x1_rows, D_MODEL), F32),
        jax.ShapeDtypeStruct((2 * nb, CONV_DIM), F32),
        jax.ShapeDtypeStruct((nb, N_STATE), F32),
        jax.ShapeDtypeStruct((nb, N_STATE), F32),
    ]
    scratch = [
        pltpu.VMEM((rows, IN_DIM), F32),
        pltpu.VMEM((rows + 2 * nb, CONV_DIM), F32),
        pltpu.VMEM((rows, 2 * N_STATE), F32),
        pltpu.VMEM((nb, N_STATE), F32),
        pltpu.VMEM((nb, N_STATE), F32),
        pltpu.VMEM((D_MODEL, IN_DIM), BF16),
        pltpu.VMEM((SSM_DIM, SSM_DIM), BF16),
        pltpu.VMEM((D_MODEL, D_MODEL), BF16),
        pltpu.VMEM((2, HALF_SSM, 2 * HALF_STATE), BF16),
        pltpu.VMEM((2, HALF_SSM, HALF_STATE), BF16),
        pltpu.VMEM((2, HALF_SSM, HALF_STATE), BF16),
    ] + tm_scratch
    operands = [x, mod, mod, mod, h0r, h0i, cb0, *weights]
    if x1_dest is not None:
        in_specs = [pl.BlockSpec(memory_space=pl.ANY)] + in_specs
        operands = [x1_dest] + operands
    return pl.pallas_call(
        functools.partial(_mixer_kernel, nb, chunk, batch_major, x1_dest is not None),
        grid=(nchunks,),
        in_specs=in_specs, out_specs=out_specs, out_shape=out_shape,
        scratch_shapes=scratch,
        input_output_aliases={0: 0} if x1_dest is not None else {},
        compiler_params=pltpu.CompilerParams(
            dimension_semantics=("arbitrary",), vmem_limit_bytes=VMEM_LIMIT),
        name="mixer",
    )(*operands)


TOK_BLOCK = 512
SEG_ALIGN = 16
ROW_TILE = 512
CHUNKS_PER_TILE = ROW_TILE // SEG_ALIGN
SCATTER_GROUPS = 8
GATHER_BUFFERS = 3
TABLE_CHUNK_ROWS = 64
N_SLOTS = -(-(2 * TOK_BLOCK + N_EXPERTS * (SEG_ALIGN - 1)) // 256) * 256
SPARE_ROW = N_SLOTS - SEG_ALIGN
assert SPARE_ROW >= 2 * TOK_BLOCK + N_EXPERTS * (SEG_ALIGN - 1)
ROUTER_ROWS = 48
META_ROWS = SUBLANES

_NT = (((1,), (1,)), ((), ()))
_TN = (((0,), (0,)), ((), ()))


def _iota_f32(shape, dim):
    return lax.broadcasted_iota(jnp.int32, shape, dim).astype(F32)


def _split_bf16(v):
    hi = v.astype(BF16)
    return hi, (v - hi.astype(F32)).astype(BF16)


def _route_t(lg):
    row = _iota_f32(lg.shape, 0)
    neg = jnp.float32(-jnp.inf)
    none = float(ROUTER_ROWS)
    is_g = row < N_EXPERT_GROUPS
    gl = jnp.where(is_g, lg, neg)
    gmax = jnp.max(gl, axis=0, keepdims=True)
    gsum = jnp.sum(jnp.where(is_g, jnp.exp(gl - gmax), 0.0), axis=0, keepdims=True)
    g_prob = 1.0 / gsum
    g_idx = jnp.min(jnp.where(gl == gmax, row, none), axis=0, keepdims=True)
    lo = N_EXPERT_GROUPS + EXPERTS_PER_GROUP * g_idx
    el = jnp.where(row >= lo, jnp.where(row < lo + EXPERTS_PER_GROUP, lg, neg), neg)
    m1 = jnp.max(el, axis=0, keepdims=True)
    i1 = jnp.min(jnp.where(el == m1, row, none), axis=0, keepdims=True)
    el2 = jnp.where(row == i1, neg, el)
    m2 = jnp.max(el2, axis=0, keepdims=True)
    i2 = jnp.min(jnp.where(el2 == m2, row, none), axis=0, keepdims=True)
    e21 = jnp.exp(m2 - m1)
    return i1, i2, g_prob / (1.0 + e21), g_prob * e21 / (1.0 + e21)


def _dispatch_kernel(x1_ref, sc_ref, sh_ref, n2g_ref, wr_ref, br_ref, xs_ref, meta_ref, cnt_ref):
    h = _rms(x1_ref[...], n2g_ref[...]) * (1.0 + sc_ref[0, 0]) + sh_ref[0, 0]
    t = h.shape[0]
    hb, hl = _split_bf16(h)
    wh, wl = _split_bf16(wr_ref[...])
    lg = (lax.dot_general(wh, hb, _NT, preferred_element_type=F32)
          + lax.dot_general(wh, hl, _NT, preferred_element_type=F32)
          + lax.dot_general(wl, hb, _NT, preferred_element_type=F32)) + br_ref[...]
    i1, i2, w1, w2 = _route_t(lg)

    row = _iota_f32((ROUTER_ROWS, t), 0)
    hit1 = row == i1
    hit2 = row == i2
    onehot = jnp.where(hit1, 1.0, 0.0) + jnp.where(hit2, 1.0, 0.0)
    before = jnp.where(_iota_f32((t, t), 0) < _iota_f32((t, t), 1), 1.0, 0.0)
    rank = _bdot(onehot, before)
    cnt = jnp.sum(onehot, axis=1, keepdims=True)
    seg = jnp.floor((cnt + (SEG_ALIGN - 1)) * (1.0 / SEG_ALIGN)) * SEG_ALIGN
    earlier = jnp.where(_iota_f32((ROUTER_ROWS, ROUTER_ROWS), 1) < _iota_f32((ROUTER_ROWS, ROUTER_ROWS), 0),
                        1.0, 0.0)
    base = rank + _bdot(earlier, jnp.broadcast_to(seg, (ROUTER_ROWS, t)))
    slot1 = jnp.sum(jnp.where(hit1, base, 0.0), axis=0, keepdims=True)
    slot2 = jnp.sum(jnp.where(hit2, base, 0.0), axis=0, keepdims=True)
    j = _iota_f32((N_SLOTS, t), 0)
    place = jnp.where(j == slot1, 1.0, 0.0) + jnp.where(j == slot2, 1.0, 0.0)
    xs_ref[...] = jnp.dot(place.astype(BF16), hb, preferred_element_type=F32).astype(BF16)
    r = _iota_f32((META_ROWS, t), 0)
    meta_ref[0] = jnp.where(r == 0.0, slot1, jnp.where(r == 1.0, slot2,
                            jnp.where(r == 2.0, w1, jnp.where(r == 3.0, w2, 0.0))))
    cnt_ref[0] = jnp.broadcast_to(cnt, (ROUTER_ROWS, LANES))


def _combine_kernel(nbp, x1_ref, gt_ref, meta_ref, ys_ref, fg_ref, yp_ref, ysm_ref, slab):
    i = pl.program_id(0)
    x1 = x1_ref[...]
    meta = meta_ref[0]
    slot1, slot2, w1, w2 = (meta[k:k + 1, :] for k in range(4))
    j = _iota_f32((N_SLOTS, x1.shape[0]), 0)
    pick = jnp.where(j == slot1, w1, 0.0) + jnp.where(j == slot2, w2, 0.0)
    moe = lax.dot_general(pick.astype(BF16), ys_ref[...], _TN, preferred_element_type=F32)
    y = _rms(x1 + gt_ref[0, 0] * moe, fg_ref[...])

    @pl.when(i < nbp)
    def _():
        nb, steps, d = yp_ref.shape
        for j in range(d // LANES):
            slab[j] = y[:, j * LANES:(j + 1) * LANES]
        for b in range(nb):
            for j in range(d // LANES):
                yp_ref[b, :, j * LANES:(j + 1) * LANES] = slab[j, pl.ds(b, steps, stride=nb), :]

    @pl.when(i >= nbp)
    def _():
        ysm_ref[...] = y


def _chunk_table_kernel(start_ref, m_ref, src_ref, out_ref):
    start = start_ref[...]
    end = start + m_ref[...]
    src = src_ref[...]

    def body(b, _):
        rows = pl.ds(pl.multiple_of(b * TABLE_CHUNK_ROWS, TABLE_CHUNK_ROWS), TABLE_CHUNK_ROWS)
        first = lax.convert_element_type(b * TABLE_CHUNK_ROWS, F32)
        i = _iota_f32((TABLE_CHUNK_ROWS, start.shape[1]), 0) + first
        val = jnp.where(i >= start, jnp.where(i < end, src + SEG_ALIGN * (i - start), 0.0), 0.0)
        out_ref[rows, :] = jnp.sum(val, axis=1, keepdims=True).astype(jnp.int32)
        return 0

    lax.fori_loop(0, out_ref.shape[0] // TABLE_CHUNK_ROWS, body, 0)


def _expert_kernel(te_ref, tnv_ref, tc0_ref, ctab_ref, nt_ref, xs_hbm, wg_ref, wu_ref, wd_ref,
                   ys_hbm, xbuf, ybuf, wgub, wdb, in_sem, out_sem):
    t = pl.program_id(0)
    nt = nt_ref[0]
    slot = t % 2
    xslot = t % GATHER_BUFFERS

    def chunk_rows(j):
        return pl.ds(pl.multiple_of(j * SEG_ALIGN, SEG_ALIGN), SEG_ALIGN)

    def hbm_rows(ref, r):
        return ref.at[pl.ds(pl.multiple_of(r, SEG_ALIGN), SEG_ALIGN), :]

    def in_copy(r, j, s):
        return pltpu.make_async_copy(hbm_rows(xs_hbm, r), xbuf.at[s, chunk_rows(j), :], in_sem.at[s])

    def out_copy(r, j, s):
        return pltpu.make_async_copy(ybuf.at[s, chunk_rows(j), :], hbm_rows(ys_hbm, r), out_sem.at[s])

    def start_gather(tile, s):
        c0 = tc0_ref[tile]
        nv = tnv_ref[tile]
        for j in range(CHUNKS_PER_TILE):
            in_copy(jnp.where(j < nv, ctab_ref[c0 + j], SPARE_ROW), j, s).start()

    def wait_gather(s):
        pltpu.make_async_copy(xs_hbm.at[pl.ds(0, ROW_TILE), :], xbuf.at[s], in_sem.at[s]).wait()

    def wait_scatter(tile, s):
        nv = tnv_ref[tile]

        @pl.when(nv == CHUNKS_PER_TILE)
        def _():
            pltpu.make_async_copy(ybuf.at[s], ys_hbm.at[pl.ds(0, ROW_TILE), :], out_sem.at[s]).wait()

        @pl.when(nv < CHUNKS_PER_TILE)
        def _():
            def body(j, _):
                out_copy(0, j, s).wait()
                return 0

            lax.fori_loop(0, nv, body, 0)

    def run_tile(full):
        ahead = t + GATHER_BUFFERS - 1
        start_gather(ahead, ahead % GATHER_BUFFERS)
        wait_gather(xslot)
        ab = jnp.dot(xbuf[xslot], wgub[...], preferred_element_type=F32)
        a, b = ab[:, :D_EXPERT], ab[:, D_EXPERT:]
        o = jnp.dot((a * _sigmoid(a) * b).astype(BF16), wdb[...], preferred_element_type=F32)
        c0 = tc0_ref[t]
        if full:
            group = ROW_TILE // SCATTER_GROUPS
            for r in range(0, ROW_TILE, group):
                ybuf[slot, r:r + group, :] = o[r:r + group].astype(BF16)
                for j in range(r // SEG_ALIGN, (r + group) // SEG_ALIGN):
                    out_copy(ctab_ref[c0 + j], j, slot).start()
        else:
            ybuf[slot] = o.astype(BF16)

            def body(j, _):
                out_copy(ctab_ref[c0 + j], j, slot).start()
                return 0

            lax.fori_loop(0, tnv_ref[t], body, 0)

    @pl.when(t == 0)
    def _():
        for k in range(GATHER_BUFFERS - 1):
            start_gather(k, k)

    @pl.when(jnp.logical_or(t == 0, te_ref[t] != te_ref[jnp.maximum(t - 1, 0)]))
    def _():
        wgub[:, :D_EXPERT] = wg_ref[0].astype(BF16)
        wgub[:, D_EXPERT:] = wu_ref[0].astype(BF16)
        wdb[...] = wd_ref[0].astype(BF16)

    @pl.when(t < nt)
    def _():
        @pl.when(t >= 2)
        def _():
            wait_scatter(t - 2, slot)

        pl.when(tnv_ref[t] == CHUNKS_PER_TILE)(lambda: run_tile(True))
        pl.when(tnv_ref[t] < CHUNKS_PER_TILE)(lambda: run_tile(False))

        @pl.when(t == nt - 1)
        def _():
            for k in range(1, GATHER_BUFFERS):
                wait_gather((t + k) % GATHER_BUFFERS)
            wait_scatter(t, slot)

            @pl.when(t >= 1)
            def _():
                wait_scatter(t - 1, 1 - slot)


def _tile_plan(cnt):
    nblk = cnt.shape[0]
    max_chunks = nblk * (2 * TOK_BLOCK // SEG_ALIGN + N_EXPERTS)
    max_tiles = max_chunks // CHUNKS_PER_TILE + N_EXPERTS
    table_len = -(-(max_chunks + CHUNKS_PER_TILE) // LANES) * LANES
    m = (cnt + SEG_ALIGN - 1) // SEG_ALIGN
    seg_start = jnp.cumsum(m, axis=1) - m
    src0 = jnp.arange(nblk, dtype=jnp.int32)[:, None] * N_SLOTS + SEG_ALIGN * seg_start
    seg_m = m.T.reshape(1, -1)
    seg_first = jnp.cumsum(seg_m, axis=1) - seg_m
    assert table_len % TABLE_CHUNK_ROWS == 0

    def row(v):
        return jnp.pad(v.astype(F32), ((0, 0), (0, -v.shape[1] % LANES)))

    chunk_src = pl.pallas_call(
        _chunk_table_kernel,
        out_shape=jax.ShapeDtypeStruct((table_len, 1), jnp.int32),
        name="moe_chunk_table",
    )(row(seg_first), row(seg_m), row(src0.T.reshape(1, -1))).reshape(-1)
    ce = jnp.sum(m, axis=0)
    c_start = jnp.cumsum(ce) - ce
    tiles = (ce + CHUNKS_PER_TILE - 1) // CHUNKS_PER_TILE
    t_cum = jnp.cumsum(tiles)
    nt = t_cum[-1]
    k = jnp.arange(max_tiles + GATHER_BUFFERS - 1, dtype=jnp.int32)
    kk = jnp.minimum(k, nt - 1)
    ek = jnp.minimum(jnp.sum((t_cum[None, :] <= kk[:, None]).astype(jnp.int32), axis=1), N_EXPERTS - 1)
    is_e = ek[:, None] == jnp.arange(N_EXPERTS, dtype=jnp.int32)[None, :]
    pick = lambda v: jnp.sum(jnp.where(is_e, v[None, :], 0), axis=1)
    local = kk - pick(t_cum - tiles)
    tile_nv = jnp.where(k < nt, jnp.clip(pick(ce) - CHUNKS_PER_TILE * local, 0, CHUNKS_PER_TILE), 0)
    tile_c0 = jnp.where(k < nt, pick(c_start) + CHUNKS_PER_TILE * local, 0)
    return tuple(v.astype(jnp.int32) for v in (ek, tile_nv, tile_c0, chunk_src, nt.reshape(1)))


def _moe(x1, bp, mod_rows, n2g, wr, br, wg, wu, wd, fg):
    nblk = x1.shape[0] // TOK_BLOCK
    nbp = nblk - 1
    blk = (TOK_BLOCK, D_MODEL)
    x1_spec = pl.BlockSpec(blk, lambda i: (i, 0))

    def mod_spec(k):
        return pl.BlockSpec((1, 1) + blk, lambda i: (k, jnp.minimum(i // nbp, 1), 0, 0))

    sh_spec, sc_spec, gt_spec = mod_spec(0), mod_spec(1), mod_spec(2)
    meta_spec = pl.BlockSpec((1, META_ROWS, TOK_BLOCK), lambda i: (i, 0, 0))
    slots_spec = pl.BlockSpec((N_SLOTS, D_MODEL), lambda i: (i, 0))
    params = pltpu.CompilerParams(dimension_semantics=("arbitrary",), vmem_limit_bytes=VMEM_LIMIT)

    def cs(shape):
        return pl.BlockSpec(shape, lambda i: (0,) * len(shape))

    xs, meta, cnt = pl.pallas_call(
        _dispatch_kernel,
        grid=(nblk,),
        in_specs=[x1_spec, sc_spec, sh_spec, cs((1, D_MODEL)),
                  cs((ROUTER_ROWS, D_MODEL)), cs((ROUTER_ROWS, TOK_BLOCK))],
        out_specs=[slots_spec, meta_spec,
                   pl.BlockSpec((1, ROUTER_ROWS, LANES), lambda i: (i, 0, 0))],
        out_shape=[jax.ShapeDtypeStruct((nblk * N_SLOTS, D_MODEL), BF16),
                   jax.ShapeDtypeStruct((nblk, META_ROWS, TOK_BLOCK), F32),
                   jax.ShapeDtypeStruct((nblk, ROUTER_ROWS, LANES), F32)],
        compiler_params=params,
        name="moe_dispatch",
    )(x1, mod_rows, mod_rows, n2g, wr, br)

    plan = _tile_plan(cnt[:, N_EXPERT_GROUPS:N_EXPERT_GROUPS + N_EXPERTS, 0].astype(jnp.int32))
    max_tiles = plan[0].shape[0] - (GATHER_BUFFERS - 1)

    def w_spec(shape):
        return pl.BlockSpec((1,) + shape, lambda t, te, *_: (te[t], 0, 0))

    ys = pl.pallas_call(
        _expert_kernel,
        grid_spec=pltpu.PrefetchScalarGridSpec(
            num_scalar_prefetch=len(plan),
            grid=(max_tiles,),
            in_specs=[pl.BlockSpec(memory_space=pl.ANY),
                      w_spec((D_MODEL, D_EXPERT)), w_spec((D_MODEL, D_EXPERT)), w_spec((D_EXPERT, D_MODEL))],
            out_specs=pl.BlockSpec(memory_space=pl.ANY),
            scratch_shapes=[pltpu.VMEM((GATHER_BUFFERS, ROW_TILE, D_MODEL), BF16),
                            pltpu.VMEM((2, ROW_TILE, D_MODEL), BF16),
                            pltpu.VMEM((D_MODEL, 2 * D_EXPERT), BF16),
                            pltpu.VMEM((D_EXPERT, D_MODEL), BF16),
                            pltpu.SemaphoreType.DMA((GATHER_BUFFERS,)), pltpu.SemaphoreType.DMA((2,))]),
        out_shape=jax.ShapeDtypeStruct(xs.shape, BF16),
        input_output_aliases={len(plan): 0},
        compiler_params=params,
        name="moe_experts",
    )(*plan, xs, wg, wu, wd)

    return pl.pallas_call(
        functools.partial(_combine_kernel, nbp),
        grid=(nblk,),
        in_specs=[x1_spec, gt_spec, meta_spec, slots_spec, cs((1, D_MODEL))],
        out_specs=[pl.BlockSpec((bp, TOK_BLOCK // bp, D_MODEL), lambda i: (0, jnp.minimum(i, nbp - 1), 0)),
                   pl.BlockSpec(blk, lambda i: (0, 0))],
        out_shape=[jax.ShapeDtypeStruct((bp, nbp * TOK_BLOCK // bp, D_MODEL), F32),
                   jax.ShapeDtypeStruct(blk, F32)],
        scratch_shapes=[pltpu.VMEM((D_MODEL // LANES, TOK_BLOCK, LANES), F32)],
        compiler_params=params,
        name="moe_combine",
    )(x1, mod_rows, meta, ys, fg)


def _to_time_major(x):
    b, s, d = x.shape
    return x.transpose(1, 0, 2).reshape(s * b, d)


def _from_time_major(x, b, s):
    return x.reshape(s, b, -1).transpose(1, 0, 2)


def kernel(x_prompt, x_sample, state_conv, state_ssm_re, state_ssm_im, c_prompt, c_sample, w_ada, b_ada, norm1_g, norm2_g, w_in, conv_w, lambda_re, lambda_im, log_dt, ssm_b_re, ssm_b_im, ssm_c_re, ssm_c_im, ssm_d, w_glu, b_glu, out_norm_conv_g, out_norm_ssm_g, w_out, w_router_group, b_router_group, w_router_expert, b_router_expert, w_expert_gate, w_expert_up, w_expert_down, final_norm_g):
    assert w_ada.shape[0] == 1, "single-layer step"
    bp, sp, _ = x_prompt.shape
    bs, ss, _ = x_sample.shape

    ar, ai, bbr, bbi = _discretise(lambda_re[0], lambda_im[0], log_dt[0], ssm_b_re[0], ssm_b_im[0])
    crt = ssm_c_re[0].transpose(1, 0, 2).reshape(SSM_GROUP, N_STATE)
    cit = ssm_c_im[0].transpose(1, 0, 2).reshape(SSM_GROUP, N_STATE)

    assert bs % bp == 0
    mod, mod_rows = _adaln(jnp.concatenate([c_sample, c_prompt], axis=0), bs, TOK_BLOCK, w_ada[0], b_ada[0])

    mixer_weights = (
        norm1_g[0].reshape(1, -1), w_in[0], conv_w[0], ar, ai, bbr, bbi, crt, cit,
        ssm_d[0].reshape(1, -1), w_glu[0], b_glu[0].reshape(1, -1),
        out_norm_conv_g[0].reshape(1, -1), out_norm_ssm_g[0].reshape(1, -1), w_out[0],
    )

    pad_rows = ROUTER_ROWS - N_EXPERT_GROUPS - N_EXPERTS
    w_router = jnp.concatenate(
        [w_router_group[0].T, w_router_expert[0].transpose(0, 2, 1).reshape(N_EXPERTS, D_MODEL),
         jnp.zeros((pad_rows, D_MODEL), F32)], axis=0)
    b_router = jnp.concatenate(
        [b_router_group[0], b_router_expert[0].reshape(-1), jnp.zeros((pad_rows,), F32)])
    b_router = jnp.broadcast_to(b_router[:, None], (ROUTER_ROWS, TOK_BLOCK))
    moe_weights = (norm2_g[0].reshape(1, -1), w_router, b_router,
                   w_expert_gate[0], w_expert_up[0], w_expert_down[0], final_norm_g.reshape(1, -1))

    assert bs * ss == TOK_BLOCK and (bp * sp) % TOK_BLOCK == 0 and TOK_BLOCK % bp == 0

    def mix(x, mod_row0, h0r, h0i, cb0, chunk, in_kernel_transpose, **x1_where):
        nb, s, _ = x.shape
        x1, cbo, hro, hio = _mixer(
            x if in_kernel_transpose else _to_time_major(x), nb, s, chunk,
            mod, mod_row0, h0r, h0i, cb0, mixer_weights, **x1_where)
        new_conv = cbo.reshape(2, nb, CONV_DIM).transpose(1, 0, 2)[None]
        new_re = hro.reshape(1, nb, SSM_GROUPS, SSM_STATE)
        new_im = hio.reshape(1, nb, SSM_GROUPS, SSM_STATE)
        return x1, new_conv, new_re, new_im

    zero_state = jnp.zeros((bp, N_STATE), F32)
    zero_conv = jnp.zeros((2 * bp, CONV_DIM), F32)
    x1, conv_p, re_p, im_p = mix(x_prompt, bs, zero_state, zero_state, zero_conv, TOK_BLOCK // bp, True,
                                 x1_rows=bp * sp + bs * ss)
    x1, conv_s, re_s, im_s = mix(
        x_sample, 0,
        state_ssm_re[0].reshape(bs, N_STATE), state_ssm_im[0].reshape(bs, N_STATE),
        state_conv[0].transpose(1, 0, 2).reshape(2 * bs, CONV_DIM), ss, False, x1_dest=x1)

    y_p, y_s = _moe(x1, bp, mod_rows, *moe_weights)
    return (y_p, _from_time_major(y_s, bs, ss), conv_p, re_p, im_p, conv_s, re_s, im_s)
```

```python
import functools

import jax
import jax.numpy as jnp
from jax import lax
from jax.experimental import pallas as pl
from jax.experimental.pallas import tpu as pltpu

D_MODEL = 1024
CONV_DIM = 512
SSM_DIM = 512
SSM_GROUP = 16
SSM_GROUPS = 32
SSM_STATE = 64
N_STATE = SSM_GROUPS * SSM_STATE
HALF_SSM = SSM_DIM // 2
HALF_STATE = N_STATE // 2
IN_DIM = 2048
N_EXPERT_GROUPS = 4
EXPERTS_PER_GROUP = 8
N_EXPERTS = 32
D_EXPERT = 256
N_MOD = 6
EPS = 1e-6

SUBLANES = 8
LANES = 128
SCAN_COLS = 512
STRIDE_PAD = 8
VMEM_LIMIT = 60 * 1024 * 1024

F32 = jnp.float32
BF16 = jnp.bfloat16


def _sigmoid(x):
    return 1.0 / (1.0 + jnp.exp(-x))


def _gelu_tanh(x):
    return 0.5 * x * (1.0 + jnp.tanh(0.7978845608028654 * (x + 0.044715 * (x * x * x))))


def _rms(x, g):
    return x * lax.rsqrt(jnp.mean(x * x, axis=-1, keepdims=True) + EPS) * g


def _bdot(a, b):
    return jnp.dot(a.astype(BF16), b.astype(BF16), preferred_element_type=F32)


def _per_seq(v, nb, scale, shift=None):
    rows, d = v.shape
    v3 = v.reshape(rows // nb, nb, d) * scale[None]
    if shift is not None:
        v3 = v3 + shift[None]
    return v3.reshape(rows, d)


def _disc_kernel(lr_ref, li_ref, ldt_ref, brt_ref, bit_ref, ar_ref, ai_ref, bbr_ref, bbi_ref):
    lr = lr_ref[...]
    li = li_ref[...]
    dt = jnp.exp(ldt_ref[...])
    mag = jnp.exp(lr * dt)
    ar = mag * jnp.cos(li * dt)
    ai = mag * jnp.sin(li * dt)
    den = lr * lr + li * li
    fr = ((ar - 1.0) * lr + ai * li) / den
    fi = (ai * lr - (ar - 1.0) * li) / den
    ar_ref[...] = ar
    ai_ref[...] = ai
    brt = brt_ref[...]
    bit = bit_ref[...]
    bbr_ref[...] = fr * brt - fi * bit
    bbi_ref[...] = fr * bit + fi * brt


def _discretise(lambda_re, lambda_im, log_dt, b_re, b_im):
    lr = lambda_re.reshape(1, N_STATE)
    li = lambda_im.reshape(1, N_STATE)
    ldt = jnp.repeat(log_dt, SSM_STATE).reshape(1, N_STATE)
    brt = b_re.transpose(2, 0, 1).reshape(SSM_GROUP, N_STATE)
    bit = b_im.transpose(2, 0, 1).reshape(SSM_GROUP, N_STATE)
    row = jax.ShapeDtypeStruct((1, N_STATE), F32)
    mat = jax.ShapeDtypeStruct((SSM_GROUP, N_STATE), F32)
    return pl.pallas_call(_disc_kernel, out_shape=(row, row, mat, mat), name="s5_discretise")(
        lr, li, ldt, brt, bit)


def _ada_kernel(n_first, block_rows, c_ref, w_ref, b_ref, o_ref, rows_ref):
    c = c_ref[...]
    o = _bdot(c * _sigmoid(c), w_ref[...]) + b_ref[...]
    o_ref[...] = o

    @pl.when(pl.program_id(0) >= N_MOD // 2)
    def _():
        n_rest = o.shape[0] - n_first
        rows_ref[0, 0] = jnp.tile(o[n_first:], (block_rows // n_rest, 1))
        rows_ref[0, 1] = jnp.tile(o[:n_first], (block_rows // n_first, 1))


def _adaln(c_all, n_first, block_rows, w_ada, b_ada):
    n = c_all.shape[0]
    half = N_MOD // 2
    return pl.pallas_call(
        functools.partial(_ada_kernel, n_first, block_rows),
        grid=(N_MOD,),
        in_specs=[
            pl.BlockSpec((n, D_MODEL), lambda j: (0, 0)),
            pl.BlockSpec((D_MODEL, D_MODEL), lambda j: (0, j)),
            pl.BlockSpec((1, D_MODEL), lambda j: (0, j)),
        ],
        out_specs=[pl.BlockSpec((n, D_MODEL), lambda j: (0, j)),
                   pl.BlockSpec((1, 2, block_rows, D_MODEL), lambda j: (jnp.maximum(j - half, 0), 0, 0, 0))],
        out_shape=[jax.ShapeDtypeStruct((n, N_MOD * D_MODEL), F32),
                   jax.ShapeDtypeStruct((half, 2, block_rows, D_MODEL), F32)],
        compiler_params=pltpu.CompilerParams(
            dimension_semantics=("arbitrary",), vmem_limit_bytes=VMEM_LIMIT),
        name="adaln",
    )(c_all, w_ada, b_ada.reshape(1, -1))


def _rows_to_time_major(x_ref, slab, xt_scr):
    nb, steps, d = x_ref.shape
    pitch = slab.shape[1] // nb
    for b in range(nb):
        for j in range(d // LANES):
            slab[j, b * pitch:b * pitch + steps, :] = x_ref[b, :, j * LANES:(j + 1) * LANES]
    for l in range(steps):
        for j in range(d // LANES):
            xt_scr[l * nb:(l + 1) * nb, j * LANES:(j + 1) * LANES] = slab[j, pl.ds(l, nb, stride=pitch), :]


def _group_diagonal(w_t, k):
    tiled = jnp.tile(w_t[:, k * HALF_STATE:(k + 1) * HALF_STATE], (SSM_GROUPS // 2, 1))
    row_group = lax.shift_right_logical(lax.broadcasted_iota(jnp.int32, tiled.shape, 0),
                                        SSM_GROUP.bit_length() - 1)
    lane_group = lax.shift_right_logical(lax.broadcasted_iota(jnp.int32, tiled.shape, 1),
                                         SSM_STATE.bit_length() - 1)
    return jnp.where(row_group == lane_group, tiled, 0.0).astype(BF16)


N_MIXER_INPUTS = 22


def _mixer_kernel(nb, steps, batch_major, has_dest, nchunks, *refs):
    refs = refs[1:] if has_dest else refs
    x1_ref = refs[N_MIXER_INPUTS]

    @pl.when(pl.program_id(0) < nchunks)
    def _():
        _mixer_step(nb, steps, batch_major, *refs)

    @pl.when(pl.program_id(0) >= nchunks)
    def _():
        x1_ref[...] = jnp.zeros_like(x1_ref)


def _mixer_step(nb, steps, batch_major, *refs):
    (x_ref, sh_ref, sc_ref, gt_ref, h0r_ref, h0i_ref, cb0_ref,
     n1g_ref, win_ref, cw_ref, ar_ref, ai_ref, bbr_ref, bbi_ref, crt_ref, cit_ref,
     dsk_ref, wglu_ref, bglu_ref, gcv_ref, gsm_ref, wout_ref,
     x1_ref, cbo_ref, hro_ref, hio_ref,
     z_scr, cv_scr, s_scr, hr_scr, hi_scr, win_b, wglu_b, wout_b, wb_ref, wcr_ref, wci_ref,
     *tm_scr) = refs
    assert refs[N_MIXER_INPUTS] is x1_ref
    rows = nb * steps
    i = pl.program_id(0)

    @pl.when(i == 0)
    def _():
        hr_scr[...] = h0r_ref[...]
        hi_scr[...] = h0i_ref[...]
        cv_scr[0:2 * nb, :] = cb0_ref[...]
        win_b[...] = win_ref[...].astype(BF16)
        wglu_b[...] = wglu_ref[...].astype(BF16)
        wout_b[...] = wout_ref[...].astype(BF16)
        for k in range(2):
            wb_ref[k, :, 0:HALF_STATE] = _group_diagonal(bbr_ref[...], k)
            wb_ref[k, :, HALF_STATE:2 * HALF_STATE] = _group_diagonal(bbi_ref[...], k)
            wcr_ref[k] = _group_diagonal(crt_ref[...], k)
            wci_ref[k] = _group_diagonal(cit_ref[...], k)

    if batch_major:
        slab, x_tm = tm_scr
        _rows_to_time_major(x_ref, slab, x_tm)
    else:
        x_tm = x_ref
    h = _per_seq(_rms(x_tm[...], n1g_ref[...]), nb, 1.0 + sc_ref[...], sh_ref[...])
    z_scr[...] = _bdot(h, win_b[...])

    u_cols = 3 * CONV_DIM
    for k in range(2):
        s_scr[:, 2 * k * HALF_STATE:2 * (k + 1) * HALF_STATE] = _bdot(
            z_scr[:, u_cols + k * HALF_SSM:u_cols + (k + 1) * HALF_SSM], wb_ref[k])

    def scan_cols(cg):
        cols = slice(cg * SCAN_COLS, (cg + 1) * SCAN_COLS)
        half, off = divmod(cg * SCAN_COLS, HALF_STATE)
        re_cols = slice(2 * half * HALF_STATE + off, 2 * half * HALF_STATE + off + SCAN_COLS)
        im_cols = slice(re_cols.start + HALF_STATE, re_cols.stop + HALF_STATE)
        a_r = jnp.broadcast_to(ar_ref[:, cols], (SUBLANES, SCAN_COLS))
        a_i = jnp.broadcast_to(ai_ref[:, cols], (SUBLANES, SCAN_COLS))

        def advance(hr, hi, rows_l):
            nr = a_r * hr - a_i * hi + s_scr[rows_l, re_cols]
            ni = a_r * hi + a_i * hr + s_scr[rows_l, im_cols]
            s_scr[rows_l, re_cols] = nr
            s_scr[rows_l, im_cols] = ni
            return nr, ni

        for r0 in range(0, nb, SUBLANES):
            tile = slice(r0, r0 + SUBLANES)
            hr, hi = hr_scr[tile, cols], hi_scr[tile, cols]
            for l in range(steps):
                hr, hi = advance(hr, hi, slice(l * nb + r0, l * nb + r0 + SUBLANES))
            hr_scr[tile, cols] = hr
            hi_scr[tile, cols] = hi

    y_halves = []
    for k in range(2):
        for cg in range(k * HALF_STATE // SCAN_COLS, (k + 1) * HALF_STATE // SCAN_COLS):
            scan_cols(cg)
        s_re = s_scr[:, 2 * k * HALF_STATE:(2 * k + 1) * HALF_STATE].astype(BF16)
        s_im = s_scr[:, (2 * k + 1) * HALF_STATE:(2 * k + 2) * HALF_STATE].astype(BF16)
        y_halves.append(lax.dot_general(s_re, wcr_ref[k], _NT, preferred_element_type=F32)
                        - lax.dot_general(s_im, wci_ref[k], _NT, preferred_element_type=F32))

    hro_ref[...] = hr_scr[...]
    hio_ref[...] = hi_scr[...]

    y = jnp.concatenate(y_halves, axis=-1) + dsk_ref[...] * z_scr[:, u_cols:IN_DIM]
    g = _gelu_tanh(y)
    y_ssm = _rms(g * _sigmoid(_bdot(g, wglu_b[...]) + bglu_ref[...]), gsm_ref[...])

    cv_scr[2 * nb:2 * nb + rows, :] = z_scr[:, CONV_DIM:2 * CONV_DIM] * z_scr[:, 2 * CONV_DIM:3 * CONV_DIM]
    conv = (cv_scr[0:rows, :] * cw_ref[0:1, :]
            + cv_scr[nb:nb + rows, :] * cw_ref[1:2, :]
            + cv_scr[2 * nb:2 * nb + rows, :] * cw_ref[2:3, :])
    y_conv = _rms(z_scr[:, 0:CONV_DIM] * conv, gcv_ref[...])
    carry = cv_scr[rows:rows + 2 * nb, :]
    cv_scr[0:2 * nb, :] = carry
    cbo_ref[...] = carry

    mixed = _bdot(y_conv, wout_b[0:CONV_DIM, :]) + _bdot(y_ssm, wout_b[CONV_DIM:2 * CONV_DIM, :])
    x1_ref[...] = x_tm[...] + _per_seq(mixed, nb, gt_ref[...])


def _const_spec(shape):
    return pl.BlockSpec(shape, lambda i: (0,) * len(shape), pipeline_mode=pl.Buffered(1))


def _mixer(x, nb, steps_total, chunk, mod, mod_row0, h0r, h0i, cb0, weights, x1_rows=None, x1_dest=None):
    rows = nb * chunk
    nchunks = steps_total // chunk
    batch_major = x.ndim == 3
    n_rows = steps_total * nb
    if x1_dest is not None:
        x1_rows = x1_dest.shape[0]
    elif x1_rows is None:
        x1_rows = n_rows
    first_block = (x1_rows - n_rows) // rows if x1_dest is not None else 0
    n_steps = nchunks if x1_dest is not None else x1_rows // rows
    last = nchunks - 1
    if batch_major:
        assert nb == SUBLANES
        x_spec = pl.BlockSpec((nb, chunk, D_MODEL), lambda i: (0, jnp.minimum(i, last), 0))
        tm_scratch = [pltpu.VMEM((D_MODEL // LANES, nb * (chunk + STRIDE_PAD), LANES), F32),
                      pltpu.VMEM((rows, D_MODEL), F32)]
    else:
        x_spec = pl.BlockSpec((rows, D_MODEL), lambda i: (jnp.minimum(i, last), 0))
        tm_scratch = []
    assert mod_row0 % nb == 0

    def mod_spec(k):
        return pl.BlockSpec((nb, D_MODEL), lambda i: (mod_row0 // nb, k), pipeline_mode=pl.Buffered(1))

    in_specs = [
        x_spec, mod_spec(0), mod_spec(1), mod_spec(2),
        _const_spec((nb, N_STATE)), _const_spec((nb, N_STATE)), _const_spec((2 * nb, CONV_DIM)),
    ] + [_const_spec(w.shape) for w in weights]
    out_specs = [
        pl.BlockSpec((rows, D_MODEL), lambda i: (first_block + i, 0)),
        _const_spec((2 * nb, CONV_DIM)), _const_spec((nb, N_STATE)), _const_spec((nb, N_STATE)),
    ]
    out_shape = [
        jax.ShapeDtypeStruct((x1_rows, D_MODEL), F32),
        jax.ShapeDtypeStruct((2 * nb, CONV_DIM), F32),
        jax.ShapeDtypeStruct((nb, N_STATE), F32),
        jax.ShapeDtypeStruct((nb, N_STATE), F32),
    ]
    scratch = [
        pltpu.VMEM((rows, IN_DIM), F32),
        pltpu.VMEM((rows + 2 * nb, CONV_DIM), F32),
        pltpu.VMEM((rows, 2 * N_STATE), F32),
        pltpu.VMEM((nb, N_STATE), F32),
        pltpu.VMEM((nb, N_STATE), F32),
        pltpu.VMEM((D_MODEL, IN_DIM), BF16),
        pltpu.VMEM((SSM_DIM, SSM_DIM), BF16),
        pltpu.VMEM((D_MODEL, D_MODEL), BF16),
        pltpu.VMEM((2, HALF_SSM, 2 * HALF_STATE), BF16),
        pltpu.VMEM((2, HALF_SSM, HALF_STATE), BF16),
        pltpu.VMEM((2, HALF_SSM, HALF_STATE), BF16),
    ] + tm_scratch
    operands = [x, mod, mod, mod, h0r, h0i, cb0, *weights]
    if x1_dest is not None:
        in_specs = [pl.BlockSpec(memory_space=pl.ANY)] + in_specs
        operands = [x1_dest] + operands
    return pl.pallas_call(
        functools.partial(_mixer_kernel, nb, chunk, batch_major, x1_dest is not None, nchunks),
        grid=(n_steps,),
        in_specs=in_specs, out_specs=out_specs, out_shape=out_shape,
        scratch_shapes=scratch,
        input_output_aliases={0: 0} if x1_dest is not None else {},
        compiler_params=pltpu.CompilerParams(
            dimension_semantics=("arbitrary",), vmem_limit_bytes=VMEM_LIMIT),
        name="mixer",
    )(*operands)


TOK_BLOCK = 512
SEG_ALIGN = 16
ROW_TILE = 512
CHUNKS_PER_TILE = ROW_TILE // SEG_ALIGN
SCATTER_GROUPS = 8
GATHER_BUFFERS = 3
TABLE_CHUNK_ROWS = 64
N_SLOTS = -(-(2 * TOK_BLOCK + N_EXPERTS * (SEG_ALIGN - 1)) // 256) * 256
SPARE_ROW = N_SLOTS - SEG_ALIGN
assert SPARE_ROW >= 2 * TOK_BLOCK + N_EXPERTS * (SEG_ALIGN - 1)
ROUTER_ROWS = 48
META_ROWS = SUBLANES

_NT = (((1,), (1,)), ((), ()))
_TN = (((0,), (0,)), ((), ()))


def _iota_f32(shape, dim):
    return lax.broadcasted_iota(jnp.int32, shape, dim).astype(F32)


def _split_bf16(v):
    hi = v.astype(BF16)
    return hi, (v - hi.astype(F32)).astype(BF16)


def _route_t(lg):
    row = _iota_f32(lg.shape, 0)
    neg = jnp.float32(-jnp.inf)
    none = float(ROUTER_ROWS)
    is_g = row < N_EXPERT_GROUPS
    gl = jnp.where(is_g, lg, neg)
    gmax = jnp.max(gl, axis=0, keepdims=True)
    gsum = jnp.sum(jnp.where(is_g, jnp.exp(gl - gmax), 0.0), axis=0, keepdims=True)
    g_prob = 1.0 / gsum
    g_idx = jnp.min(jnp.where(gl == gmax, row, none), axis=0, keepdims=True)
    lo = N_EXPERT_GROUPS + EXPERTS_PER_GROUP * g_idx
    el = jnp.where(row >= lo, jnp.where(row < lo + EXPERTS_PER_GROUP, lg, neg), neg)
    m1 = jnp.max(el, axis=0, keepdims=True)
    i1 = jnp.min(jnp.where(el == m1, row, none), axis=0, keepdims=True)
    el2 = jnp.where(row == i1, neg, el)
    m2 = jnp.max(el2, axis=0, keepdims=True)
    i2 = jnp.min(jnp.where(el2 == m2, row, none), axis=0, keepdims=True)
    e21 = jnp.exp(m2 - m1)
    return i1, i2, g_prob / (1.0 + e21), g_prob * e21 / (1.0 + e21)


def _dispatch_kernel(x1_ref, sc_ref, sh_ref, n2g_ref, wr_ref, br_ref, xs_ref, meta_ref, cnt_ref):
    h = _rms(x1_ref[...], n2g_ref[...]) * (1.0 + sc_ref[0, 0]) + sh_ref[0, 0]
    t = h.shape[0]
    hb, hl = _split_bf16(h)
    wh, wl = _split_bf16(wr_ref[...])
    lg = (lax.dot_general(wh, hb, _NT, preferred_element_type=F32)
          + lax.dot_general(wh, hl, _NT, preferred_element_type=F32)
          + lax.dot_general(wl, hb, _NT, preferred_element_type=F32)) + br_ref[...]
    i1, i2, w1, w2 = _route_t(lg)

    row = _iota_f32((ROUTER_ROWS, t), 0)
    hit1 = row == i1
    hit2 = row == i2
    onehot = jnp.where(hit1, 1.0, 0.0) + jnp.where(hit2, 1.0, 0.0)
    before = jnp.where(_iota_f32((t, t), 0) < _iota_f32((t, t), 1), 1.0, 0.0)
    rank = _bdot(onehot, before)
    cnt = jnp.sum(onehot, axis=1, keepdims=True)
    seg = jnp.floor((cnt + (SEG_ALIGN - 1)) * (1.0 / SEG_ALIGN)) * SEG_ALIGN
    earlier = jnp.where(_iota_f32((ROUTER_ROWS, ROUTER_ROWS), 1) < _iota_f32((ROUTER_ROWS, ROUTER_ROWS), 0),
                        1.0, 0.0)
    base = rank + _bdot(earlier, jnp.broadcast_to(seg, (ROUTER_ROWS, t)))
    slot1 = jnp.sum(jnp.where(hit1, base, 0.0), axis=0, keepdims=True)
    slot2 = jnp.sum(jnp.where(hit2, base, 0.0), axis=0, keepdims=True)
    j = _iota_f32((N_SLOTS, t), 0)
    place = jnp.where(j == slot1, 1.0, 0.0) + jnp.where(j == slot2, 1.0, 0.0)
    xs_ref[...] = jnp.dot(place.astype(BF16), hb, preferred_element_type=F32).astype(BF16)
    r = _iota_f32((META_ROWS, t), 0)
    meta_ref[0] = jnp.where(r == 0.0, slot1, jnp.where(r == 1.0, slot2,
                            jnp.where(r == 2.0, w1, jnp.where(r == 3.0, w2, 0.0))))
    cnt_ref[0] = jnp.broadcast_to(cnt, (ROUTER_ROWS, LANES))


def _combine_kernel(nbp, x1_ref, gt_ref, meta_ref, ys_ref, fg_ref, yp_ref, ysm_ref, slab):
    i = pl.program_id(0)
    x1 = x1_ref[...]
    meta = meta_ref[0]
    slot1, slot2, w1, w2 = (meta[k:k + 1, :] for k in range(4))
    j = _iota_f32((N_SLOTS, x1.shape[0]), 0)
    pick = jnp.where(j == slot1, w1, 0.0) + jnp.where(j == slot2, w2, 0.0)
    moe = lax.dot_general(pick.astype(BF16), ys_ref[...], _TN, preferred_element_type=F32)
    y = _rms(x1 + gt_ref[0, 0] * moe, fg_ref[...])

    @pl.when(i < nbp)
    def _():
        nb, steps, d = yp_ref.shape
        for j in range(d // LANES):
            slab[j] = y[:, j * LANES:(j + 1) * LANES]
        for b in range(nb):
            for j in range(d // LANES):
                yp_ref[b, :, j * LANES:(j + 1) * LANES] = slab[j, pl.ds(b, steps, stride=nb), :]

    @pl.when(i >= nbp)
    def _():
        ysm_ref[...] = y


def _chunk_table_kernel(start_ref, m_ref, src_ref, out_ref):
    start = start_ref[...]
    end = start + m_ref[...]
    src = src_ref[...]

    def body(b, _):
        rows = pl.ds(pl.multiple_of(b * TABLE_CHUNK_ROWS, TABLE_CHUNK_ROWS), TABLE_CHUNK_ROWS)
        first = lax.convert_element_type(b * TABLE_CHUNK_ROWS, F32)
        i = _iota_f32((TABLE_CHUNK_ROWS, start.shape[1]), 0) + first
        val = jnp.where(i >= start, jnp.where(i < end, src + SEG_ALIGN * (i - start), 0.0), 0.0)
        out_ref[rows, :] = jnp.sum(val, axis=1, keepdims=True).astype(jnp.int32)
        return 0

    lax.fori_loop(0, out_ref.shape[0] // TABLE_CHUNK_ROWS, body, 0)


def _expert_kernel(te_ref, tnv_ref, tc0_ref, ctab_ref, nt_ref, xs_hbm, wg_ref, wu_ref, wd_ref,
                   ys_hbm, xbuf, ybuf, wgub, wdb, in_sem, out_sem):
    t = pl.program_id(0)
    nt = nt_ref[0]
    slot = t % 2
    xslot = t % GATHER_BUFFERS

    def chunk_rows(j):
        return pl.ds(pl.multiple_of(j * SEG_ALIGN, SEG_ALIGN), SEG_ALIGN)

    def hbm_rows(ref, r):
        return ref.at[pl.ds(pl.multiple_of(r, SEG_ALIGN), SEG_ALIGN), :]

    def in_copy(r, j, s):
        return pltpu.make_async_copy(hbm_rows(xs_hbm, r), xbuf.at[s, chunk_rows(j), :], in_sem.at[s])

    def out_copy(r, j, s):
        return pltpu.make_async_copy(ybuf.at[s, chunk_rows(j), :], hbm_rows(ys_hbm, r), out_sem.at[s])

    def start_gather(tile, s):
        c0 = tc0_ref[tile]
        nv = tnv_ref[tile]
        for j in range(CHUNKS_PER_TILE):
            in_copy(jnp.where(j < nv, ctab_ref[c0 + j], SPARE_ROW), j, s).start()

    def wait_gather(s):
        pltpu.make_async_copy(xs_hbm.at[pl.ds(0, ROW_TILE), :], xbuf.at[s], in_sem.at[s]).wait()

    def wait_scatter(tile, s):
        nv = tnv_ref[tile]

        @pl.when(nv == CHUNKS_PER_TILE)
        def _():
            pltpu.make_async_copy(ybuf.at[s], ys_hbm.at[pl.ds(0, ROW_TILE), :], out_sem.at[s]).wait()

        @pl.when(nv < CHUNKS_PER_TILE)
        def _():
            def body(j, _):
                out_copy(0, j, s).wait()
                return 0

            lax.fori_loop(0, nv, body, 0)

    def run_tile(full):
        ahead = t + GATHER_BUFFERS - 1
        start_gather(ahead, ahead % GATHER_BUFFERS)
        wait_gather(xslot)
        ab = jnp.dot(xbuf[xslot], wgub[...], preferred_element_type=F32)
        a, b = ab[:, :D_EXPERT], ab[:, D_EXPERT:]
        o = jnp.dot((a * _sigmoid(a) * b).astype(BF16), wdb[...], preferred_element_type=F32)
        c0 = tc0_ref[t]
        if full:
            group = ROW_TILE // SCATTER_GROUPS
            for r in range(0, ROW_TILE, group):
                ybuf[slot, r:r + group, :] = o[r:r + group].astype(BF16)
                for j in range(r // SEG_ALIGN, (r + group) // SEG_ALIGN):
                    out_copy(ctab_ref[c0 + j], j, slot).start()
        else:
            ybuf[slot] = o.astype(BF16)

            def body(j, _):
                out_copy(ctab_ref[c0 + j], j, slot).start()
                return 0

            lax.fori_loop(0, tnv_ref[t], body, 0)

    @pl.when(t == 0)
    def _():
        for k in range(GATHER_BUFFERS - 1):
            start_gather(k, k)

    @pl.when(jnp.logical_or(t == 0, te_ref[t] != te_ref[jnp.maximum(t - 1, 0)]))
    def _():
        wgub[:, :D_EXPERT] = wg_ref[0].astype(BF16)
        wgub[:, D_EXPERT:] = wu_ref[0].astype(BF16)
        wdb[...] = wd_ref[0].astype(BF16)

    @pl.when(t < nt)
    def _():
        @pl.when(t >= 2)
        def _():
            wait_scatter(t - 2, slot)

        pl.when(tnv_ref[t] == CHUNKS_PER_TILE)(lambda: run_tile(True))
        pl.when(tnv_ref[t] < CHUNKS_PER_TILE)(lambda: run_tile(False))

        @pl.when(t == nt - 1)
        def _():
            for k in range(1, GATHER_BUFFERS):
                wait_gather((t + k) % GATHER_BUFFERS)
            wait_scatter(t, slot)

            @pl.when(t >= 1)
            def _():
                wait_scatter(t - 1, 1 - slot)


def _tile_plan(cnt):
    nblk = cnt.shape[0]
    max_chunks = nblk * (2 * TOK_BLOCK // SEG_ALIGN + N_EXPERTS)
    max_tiles = max_chunks // CHUNKS_PER_TILE + N_EXPERTS
    table_len = -(-(max_chunks + CHUNKS_PER_TILE) // LANES) * LANES
    m = (cnt + SEG_ALIGN - 1) // SEG_ALIGN
    seg_start = jnp.cumsum(m, axis=1) - m
    src0 = jnp.arange(nblk, dtype=jnp.int32)[:, None] * N_SLOTS + SEG_ALIGN * seg_start
    seg_m = m.T.reshape(1, -1)
    seg_first = jnp.cumsum(seg_m, axis=1) - seg_m
    assert table_len % TABLE_CHUNK_ROWS == 0

    def row(v):
        return jnp.pad(v.astype(F32), ((0, 0), (0, -v.shape[1] % LANES)))

    chunk_src = pl.pallas_call(
        _chunk_table_kernel,
        out_shape=jax.ShapeDtypeStruct((table_len, 1), jnp.int32),
        name="moe_chunk_table",
    )(row(seg_first), row(seg_m), row(src0.T.reshape(1, -1))).reshape(-1)
    ce = jnp.sum(m, axis=0)
    c_start = jnp.cumsum(ce) - ce
    tiles = (ce + CHUNKS_PER_TILE - 1) // CHUNKS_PER_TILE
    t_cum = jnp.cumsum(tiles)
    nt = t_cum[-1]
    k = jnp.arange(max_tiles + GATHER_BUFFERS - 1, dtype=jnp.int32)
    kk = jnp.minimum(k, nt - 1)
    ek = jnp.minimum(jnp.sum((t_cum[None, :] <= kk[:, None]).astype(jnp.int32), axis=1), N_EXPERTS - 1)
    is_e = ek[:, None] == jnp.arange(N_EXPERTS, dtype=jnp.int32)[None, :]
    pick = lambda v: jnp.sum(jnp.where(is_e, v[None, :], 0), axis=1)
    local = kk - pick(t_cum - tiles)
    tile_nv = jnp.where(k < nt, jnp.clip(pick(ce) - CHUNKS_PER_TILE * local, 0, CHUNKS_PER_TILE), 0)
    tile_c0 = jnp.where(k < nt, pick(c_start) + CHUNKS_PER_TILE * local, 0)
    return tuple(v.astype(jnp.int32) for v in (ek, tile_nv, tile_c0, chunk_src, nt.reshape(1)))


def _moe(x1, bp, mod_rows, n2g, wr, br, wg, wu, wd, fg):
    nblk = x1.shape[0] // TOK_BLOCK
    nbp = nblk - 1
    blk = (TOK_BLOCK, D_MODEL)
    x1_spec = pl.BlockSpec(blk, lambda i: (i, 0))

    def mod_spec(k):
        return pl.BlockSpec((1, 1) + blk, lambda i: (k, jnp.minimum(i // nbp, 1), 0, 0))

    sh_spec, sc_spec, gt_spec = mod_spec(0), mod_spec(1), mod_spec(2)
    meta_spec = pl.BlockSpec((1, META_ROWS, TOK_BLOCK), lambda i: (i, 0, 0))
    slots_spec = pl.BlockSpec((N_SLOTS, D_MODEL), lambda i: (i, 0))
    params = pltpu.CompilerParams(dimension_semantics=("arbitrary",), vmem_limit_bytes=VMEM_LIMIT)

    def cs(shape):
        return pl.BlockSpec(shape, lambda i: (0,) * len(shape))

    xs, meta, cnt = pl.pallas_call(
        _dispatch_kernel,
        grid=(nblk,),
        in_specs=[x1_spec, sc_spec, sh_spec, cs((1, D_MODEL)),
                  cs((ROUTER_ROWS, D_MODEL)), cs((ROUTER_ROWS, TOK_BLOCK))],
        out_specs=[slots_spec, meta_spec,
                   pl.BlockSpec((1, ROUTER_ROWS, LANES), lambda i: (i, 0, 0))],
        out_shape=[jax.ShapeDtypeStruct((nblk * N_SLOTS, D_MODEL), BF16),
                   jax.ShapeDtypeStruct((nblk, META_ROWS, TOK_BLOCK), F32),
                   jax.ShapeDtypeStruct((nblk, ROUTER_ROWS, LANES), F32)],
        compiler_params=params,
        name="moe_dispatch",
    )(x1, mod_rows, mod_rows, n2g, wr, br)

    plan = _tile_plan(cnt[:, N_EXPERT_GROUPS:N_EXPERT_GROUPS + N_EXPERTS, 0].astype(jnp.int32))
    max_tiles = plan[0].shape[0] - (GATHER_BUFFERS - 1)

    def w_spec(shape):
        return pl.BlockSpec((1,) + shape, lambda t, te, *_: (te[t], 0, 0))

    ys = pl.pallas_call(
        _expert_kernel,
        grid_spec=pltpu.PrefetchScalarGridSpec(
            num_scalar_prefetch=len(plan),
            grid=(max_tiles,),
            in_specs=[pl.BlockSpec(memory_space=pl.ANY),
                      w_spec((D_MODEL, D_EXPERT)), w_spec((D_MODEL, D_EXPERT)), w_spec((D_EXPERT, D_MODEL))],
            out_specs=pl.BlockSpec(memory_space=pl.ANY),
            scratch_shapes=[pltpu.VMEM((GATHER_BUFFERS, ROW_TILE, D_MODEL), BF16),
                            pltpu.VMEM((2, ROW_TILE, D_MODEL), BF16),
                            pltpu.VMEM((D_MODEL, 2 * D_EXPERT), BF16),
                            pltpu.VMEM((D_EXPERT, D_MODEL), BF16),
                            pltpu.SemaphoreType.DMA((GATHER_BUFFERS,)), pltpu.SemaphoreType.DMA((2,))]),
        out_shape=jax.ShapeDtypeStruct(xs.shape, BF16),
        input_output_aliases={len(plan): 0},
        compiler_params=params,
        name="moe_experts",
    )(*plan, xs, wg, wu, wd)

    return pl.pallas_call(
        functools.partial(_combine_kernel, nbp),
        grid=(nblk,),
        in_specs=[x1_spec, gt_spec, meta_spec, slots_spec, cs((1, D_MODEL))],
        out_specs=[pl.BlockSpec((bp, TOK_BLOCK // bp, D_MODEL), lambda i: (0, jnp.minimum(i, nbp - 1), 0)),
                   pl.BlockSpec(blk, lambda i: (0, 0))],
        out_shape=[jax.ShapeDtypeStruct((bp, nbp * TOK_BLOCK // bp, D_MODEL), F32),
                   jax.ShapeDtypeStruct(blk, F32)],
        scratch_shapes=[pltpu.VMEM((D_MODEL // LANES, TOK_BLOCK, LANES), F32)],
        compiler_params=params,
        name="moe_combine",
    )(x1, mod_rows, meta, ys, fg)


def _to_time_major(x):
    b, s, d = x.shape
    return x.transpose(1, 0, 2).reshape(s * b, d)


def _from_time_major(x, b, s):
    return x.reshape(s, b, -1).transpose(1, 0, 2)


def kernel(x_prompt, x_sample, state_conv, state_ssm_re, state_ssm_im, c_prompt, c_sample, w_ada, b_ada, norm1_g, norm2_g, w_in, conv_w, lambda_re, lambda_im, log_dt, ssm_b_re, ssm_b_im, ssm_c_re, ssm_c_im, ssm_d, w_glu, b_glu, out_norm_conv_g, out_norm_ssm_g, w_out, w_router_group, b_router_group, w_router_expert, b_router_expert, w_expert_gate, w_expert_up, w_expert_down, final_norm_g):
    assert w_ada.shape[0] == 1, "single-layer step"
    bp, sp, _ = x_prompt.shape
    bs, ss, _ = x_sample.shape

    ar, ai, bbr, bbi = _discretise(lambda_re[0], lambda_im[0], log_dt[0], ssm_b_re[0], ssm_b_im[0])
    crt = ssm_c_re[0].transpose(1, 0, 2).reshape(SSM_GROUP, N_STATE)
    cit = ssm_c_im[0].transpose(1, 0, 2).reshape(SSM_GROUP, N_STATE)

    assert bs % bp == 0
    mod, mod_rows = _adaln(jnp.concatenate([c_sample, c_prompt], axis=0), bs, TOK_BLOCK, w_ada[0], b_ada[0])

    mixer_weights = (
        norm1_g[0].reshape(1, -1), w_in[0], conv_w[0], ar, ai, bbr, bbi, crt, cit,
        ssm_d[0].reshape(1, -1), w_glu[0], b_glu[0].reshape(1, -1),
        out_norm_conv_g[0].reshape(1, -1), out_norm_ssm_g[0].reshape(1, -1), w_out[0],
    )

    pad_rows = ROUTER_ROWS - N_EXPERT_GROUPS - N_EXPERTS
    w_router = jnp.concatenate(
        [w_router_group[0].T, w_router_expert[0].transpose(0, 2, 1).reshape(N_EXPERTS, D_MODEL),
         jnp.zeros((pad_rows, D_MODEL), F32)], axis=0)
    b_router = jnp.concatenate(
        [b_router_group[0], b_router_expert[0].reshape(-1), jnp.zeros((pad_rows,), F32)])
    b_router = jnp.broadcast_to(b_router[:, None], (ROUTER_ROWS, TOK_BLOCK))
    moe_weights = (norm2_g[0].reshape(1, -1), w_router, b_router,
                   w_expert_gate[0], w_expert_up[0], w_expert_down[0], final_norm_g.reshape(1, -1))

    assert bs * ss == TOK_BLOCK and (bp * sp) % TOK_BLOCK == 0 and TOK_BLOCK % bp == 0

    def mix(x, mod_row0, h0r, h0i, cb0, chunk, in_kernel_transpose, **x1_where):
        nb, s, _ = x.shape
        x1, cbo, hro, hio = _mixer(
            x if in_kernel_transpose else _to_time_major(x), nb, s, chunk,
            mod, mod_row0, h0r, h0i, cb0, mixer_weights, **x1_where)
        new_conv = cbo.reshape(2, nb, CONV_DIM).transpose(1, 0, 2)[None]
        new_re = hro.reshape(1, nb, SSM_GROUPS, SSM_STATE)
        new_im = hio.reshape(1, nb, SSM_GROUPS, SSM_STATE)
        return x1, new_conv, new_re, new_im

    zero_state = jnp.zeros((bp, N_STATE), F32)
    zero_conv = jnp.zeros((2 * bp, CONV_DIM), F32)
    x1, conv_p, re_p, im_p = mix(x_prompt, bs, zero_state, zero_state, zero_conv, TOK_BLOCK // bp, True,
                                 x1_rows=bp * sp + bs * ss)
    x1, conv_s, re_s, im_s = mix(
        x_sample, 0,
        state_ssm_re[0].reshape(bs, N_STATE), state_ssm_im[0].reshape(bs, N_STATE),
        state_conv[0].transpose(1, 0, 2).reshape(2 * bs, CONV_DIM), ss, False, x1_dest=x1)

    y_p, y_s = _moe(x1, bp, mod_rows, *moe_weights)
    return (y_p, _from_time_major(y_s, bs, ss), conv_p, re_p, im_p, conv_s, re_s, im_s)
```

```python
import functools

import jax
import jax.numpy as jnp
from jax import lax
from jax.experimental import pallas as pl
from jax.experimental.pallas import tpu as pltpu

D_MODEL = 1024
CONV_DIM = 512
SSM_DIM = 512
SSM_GROUP = 16
SSM_GROUPS = 32
SSM_STATE = 64
N_STATE = SSM_GROUPS * SSM_STATE
HALF_SSM = SSM_DIM // 2
HALF_STATE = N_STATE // 2
IN_DIM = 2048
N_EXPERT_GROUPS = 4
EXPERTS_PER_GROUP = 8
N_EXPERTS = 32
D_EXPERT = 256
N_MOD = 6
EPS = 1e-6

SUBLANES = 8
LANES = 128
SCAN_COLS = 512
STRIDE_PAD = 8
VMEM_LIMIT = 60 * 1024 * 1024

F32 = jnp.float32
BF16 = jnp.bfloat16


def _sigmoid(x):
    return 1.0 / (1.0 + jnp.exp(-x))


def _gelu_tanh(x):
    return 0.5 * x * (1.0 + jnp.tanh(0.7978845608028654 * (x + 0.044715 * (x * x * x))))


def _rms(x, g):
    return x * lax.rsqrt(jnp.mean(x * x, axis=-1, keepdims=True) + EPS) * g


def _bdot(a, b):
    return jnp.dot(a.astype(BF16), b.astype(BF16), preferred_element_type=F32)


def _per_seq(v, nb, scale, shift=None):
    rows, d = v.shape
    v3 = v.reshape(rows // nb, nb, d) * scale[None]
    if shift is not None:
        v3 = v3 + shift[None]
    return v3.reshape(rows, d)


def _disc_kernel(lr_ref, li_ref, ldt_ref, brt_ref, bit_ref, ar_ref, ai_ref, bbr_ref, bbi_ref):
    lr = lr_ref[...]
    li = li_ref[...]
    dt = jnp.exp(ldt_ref[...])
    mag = jnp.exp(lr * dt)
    ar = mag * jnp.cos(li * dt)
    ai = mag * jnp.sin(li * dt)
    den = lr * lr + li * li
    fr = ((ar - 1.0) * lr + ai * li) / den
    fi = (ai * lr - (ar - 1.0) * li) / den
    ar_ref[...] = ar
    ai_ref[...] = ai
    brt = brt_ref[...]
    bit = bit_ref[...]
    bbr_ref[...] = fr * brt - fi * bit
    bbi_ref[...] = fr * bit + fi * brt


def _discretise(lambda_re, lambda_im, log_dt, b_re, b_im):
    lr = lambda_re.reshape(1, N_STATE)
    li = lambda_im.reshape(1, N_STATE)
    ldt = jnp.repeat(log_dt, SSM_STATE).reshape(1, N_STATE)
    brt = b_re.transpose(2, 0, 1).reshape(SSM_GROUP, N_STATE)
    bit = b_im.transpose(2, 0, 1).reshape(SSM_GROUP, N_STATE)
    row = jax.ShapeDtypeStruct((1, N_STATE), F32)
    mat = jax.ShapeDtypeStruct((SSM_GROUP, N_STATE), F32)
    return pl.pallas_call(_disc_kernel, out_shape=(row, row, mat, mat), name="s5_discretise")(
        lr, li, ldt, brt, bit)


def _ada_kernel(n_first, block_rows, c_ref, w_ref, b_ref, o_ref, rows_ref):
    c = c_ref[...]
    o = _bdot(c * _sigmoid(c), w_ref[...]) + b_ref[...]
    o_ref[...] = o

    @pl.when(pl.program_id(0) >= N_MOD // 2)
    def _():
        n_rest = o.shape[0] - n_first
        rows_ref[0, 0] = jnp.tile(o[n_first:], (block_rows // n_rest, 1))
        rows_ref[0, 1] = jnp.tile(o[:n_first], (block_rows // n_first, 1))


def _adaln(c_all, n_first, block_rows, w_ada, b_ada):
    n = c_all.shape[0]
    half = N_MOD // 2
    return pl.pallas_call(
        functools.partial(_ada_kernel, n_first, block_rows),
        grid=(N_MOD,),
        in_specs=[
            pl.BlockSpec((n, D_MODEL), lambda j: (0, 0)),
            pl.BlockSpec((D_MODEL, D_MODEL), lambda j: (0, j)),
            pl.BlockSpec((1, D_MODEL), lambda j: (0, j)),
        ],
        out_specs=[pl.BlockSpec((n, D_MODEL), lambda j: (0, j)),
                   pl.BlockSpec((1, 2, block_rows, D_MODEL), lambda j: (jnp.maximum(j - half, 0), 0, 0, 0))],
        out_shape=[jax.ShapeDtypeStruct((n, N_MOD * D_MODEL), F32),
                   jax.ShapeDtypeStruct((half, 2, block_rows, D_MODEL), F32)],
        compiler_params=pltpu.CompilerParams(
            dimension_semantics=("arbitrary",), vmem_limit_bytes=VMEM_LIMIT),
        name="adaln",
    )(c_all, w_ada, b_ada.reshape(1, -1))


def _rows_to_time_major(x_ref, slab, xt_scr):
    nb, steps, d = x_ref.shape
    pitch = slab.shape[1] // nb
    for b in range(nb):
        for j in range(d // LANES):
            slab[j, b * pitch:b * pitch + steps, :] = x_ref[b, :, j * LANES:(j + 1) * LANES]
    for l in range(steps):
        for j in range(d // LANES):
            xt_scr[l * nb:(l + 1) * nb, j * LANES:(j + 1) * LANES] = slab[j, pl.ds(l, nb, stride=pitch), :]


def _group_diagonal(w_t, k):
    tiled = jnp.tile(w_t[:, k * HALF_STATE:(k + 1) * HALF_STATE], (SSM_GROUPS // 2, 1))
    row_group = lax.shift_right_logical(lax.broadcasted_iota(jnp.int32, tiled.shape, 0),
                                        SSM_GROUP.bit_length() - 1)
    lane_group = lax.shift_right_logical(lax.broadcasted_iota(jnp.int32, tiled.shape, 1),
                                         SSM_STATE.bit_length() - 1)
    return jnp.where(row_group == lane_group, tiled, 0.0).astype(BF16)


N_MIXER_INPUTS = 22


def _mixer_kernel(nb, steps, batch_major, has_dest, nchunks, *refs):
    refs = refs[1:] if has_dest else refs
    x1_ref = refs[N_MIXER_INPUTS]

    @pl.when(pl.program_id(0) < nchunks)
    def _():
        _mixer_step(nb, steps, batch_major, *refs)

    @pl.when(pl.program_id(0) >= nchunks)
    def _():
        x1_ref[...] = jnp.zeros_like(x1_ref)


def _mixer_step(nb, steps, batch_major, *refs):
    (x_ref, sh_ref, sc_ref, gt_ref, h0r_ref, h0i_ref, cb0_ref,
     n1g_ref, win_ref, cw_ref, ar_ref, ai_ref, bbr_ref, bbi_ref, crt_ref, cit_ref,
     dsk_ref, wglu_ref, bglu_ref, gcv_ref, gsm_ref, wout_ref,
     x1_ref, cbo_ref, hro_ref, hio_ref,
     z_scr, cv_scr, s_scr, hr_scr, hi_scr, win_b, wglu_b, wout_b, wb_ref, wcr_ref, wci_ref,
     *tm_scr) = refs
    assert refs[N_MIXER_INPUTS] is x1_ref
    rows = nb * steps
    i = pl.program_id(0)

    @pl.when(i == 0)
    def _():
        hr_scr[...] = h0r_ref[...]
        hi_scr[...] = h0i_ref[...]
        cv_scr[0:2 * nb, :] = cb0_ref[...]
        win_b[...] = win_ref[...].astype(BF16)
        wglu_b[...] = wglu_ref[...].astype(BF16)
        wout_b[...] = wout_ref[...].astype(BF16)
        for k in range(2):
            wb_ref[k, :, 0:HALF_STATE] = _group_diagonal(bbr_ref[...], k)
            wb_ref[k, :, HALF_STATE:2 * HALF_STATE] = _group_diagonal(bbi_ref[...], k)
            wcr_ref[k] = _group_diagonal(crt_ref[...], k)
            wci_ref[k] = _group_diagonal(cit_ref[...], k)

    if batch_major:
        slab, x_tm = tm_scr
        _rows_to_time_major(x_ref, slab, x_tm)
    else:
        x_tm = x_ref
    h = _per_seq(_rms(x_tm[...], n1g_ref[...]), nb, 1.0 + sc_ref[...], sh_ref[...])
    z_scr[...] = _bdot(h, win_b[...])

    u_cols = 3 * CONV_DIM
    for k in range(2):
        s_scr[:, 2 * k * HALF_STATE:2 * (k + 1) * HALF_STATE] = _bdot(
            z_scr[:, u_cols + k * HALF_SSM:u_cols + (k + 1) * HALF_SSM], wb_ref[k])

    def scan_cols(cg):
        cols = slice(cg * SCAN_COLS, (cg + 1) * SCAN_COLS)
        half, off = divmod(cg * SCAN_COLS, HALF_STATE)
        re_cols = slice(2 * half * HALF_STATE + off, 2 * half * HALF_STATE + off + SCAN_COLS)
        im_cols = slice(re_cols.start + HALF_STATE, re_cols.stop + HALF_STATE)
        a_r = jnp.broadcast_to(ar_ref[:, cols], (SUBLANES, SCAN_COLS))
        a_i = jnp.broadcast_to(ai_ref[:, cols], (SUBLANES, SCAN_COLS))

        def advance(hr, hi, rows_l):
            nr = a_r * hr - a_i * hi + s_scr[rows_l, re_cols]
            ni = a_r * hi + a_i * hr + s_scr[rows_l, im_cols]
            s_scr[rows_l, re_cols] = nr
            s_scr[rows_l, im_cols] = ni
            return nr, ni

        for r0 in range(0, nb, SUBLANES):
            tile = slice(r0, r0 + SUBLANES)
            hr, hi = hr_scr[tile, cols], hi_scr[tile, cols]
            for l in range(steps):
                hr, hi = advance(hr, hi, slice(l * nb + r0, l * nb + r0 + SUBLANES))
            hr_scr[tile, cols] = hr
            hi_scr[tile, cols] = hi

    y_halves = []
    for k in range(2):
        for cg in range(k * HALF_STATE // SCAN_COLS, (k + 1) * HALF_STATE // SCAN_COLS):
            scan_cols(cg)
        s_re = s_scr[:, 2 * k * HALF_STATE:(2 * k + 1) * HALF_STATE].astype(BF16)
        s_im = s_scr[:, (2 * k + 1) * HALF_STATE:(2 * k + 2) * HALF_STATE].astype(BF16)
        y_halves.append(lax.dot_general(s_re, wcr_ref[k], _NT, preferred_element_type=F32)
                        - lax.dot_general(s_im, wci_ref[k], _NT, preferred_element_type=F32))

    hro_ref[...] = hr_scr[...]
    hio_ref[...] = hi_scr[...]

    y = jnp.concatenate(y_halves, axis=-1) + dsk_ref[...] * z_scr[:, u_cols:IN_DIM]
    g = _gelu_tanh(y)
    y_ssm = _rms(g * _sigmoid(_bdot(g, wglu_b[...]) + bglu_ref[...]), gsm_ref[...])

    cv_scr[2 * nb:2 * nb + rows, :] = z_scr[:, CONV_DIM:2 * CONV_DIM] * z_scr[:, 2 * CONV_DIM:3 * CONV_DIM]
    conv = (cv_scr[0:rows, :] * cw_ref[0:1, :]
            + cv_scr[nb:nb + rows, :] * cw_ref[1:2, :]
            + cv_scr[2 * nb:2 * nb + rows, :] * cw_ref[2:3, :])
    y_conv = _rms(z_scr[:, 0:CONV_DIM] * conv, gcv_ref[...])
    carry = cv_scr[rows:rows + 2 * nb, :]
    cv_scr[0:2 * nb, :] = carry
    cbo_ref[...] = carry

    mixed = _bdot(y_conv, wout_b[0:CONV_DIM, :]) + _bdot(y_ssm, wout_b[CONV_DIM:2 * CONV_DIM, :])
    x1_ref[...] = x_tm[...] + _per_seq(mixed, nb, gt_ref[...])


def _const_spec(shape):
    return pl.BlockSpec(shape, lambda i: (0,) * len(shape), pipeline_mode=pl.Buffered(1))


def _mixer(x, nb, steps_total, chunk, mod, mod_row0, h0r, h0i, cb0, weights, x1_rows=None, x1_dest=None):
    rows = nb * chunk
    nchunks = steps_total // chunk
    batch_major = x.ndim == 3
    n_rows = steps_total * nb
    if x1_dest is not None:
        x1_rows = x1_dest.shape[0]
    elif x1_rows is None:
        x1_rows = n_rows
    first_block = (x1_rows - n_rows) // rows if x1_dest is not None else 0
    n_steps = nchunks if x1_dest is not None else x1_rows // rows
    last = nchunks - 1
    if batch_major:
        assert nb == SUBLANES
        x_spec = pl.BlockSpec((nb, chunk, D_MODEL), lambda i: (0, jnp.minimum(i, last), 0))
        tm_scratch = [pltpu.VMEM((D_MODEL // LANES, nb * (chunk + STRIDE_PAD), LANES), F32),
                      pltpu.VMEM((rows, D_MODEL), F32)]
    else:
        x_spec = pl.BlockSpec((rows, D_MODEL), lambda i: (jnp.minimum(i, last), 0))
        tm_scratch = []
    assert mod_row0 % nb == 0

    def mod_spec(k):
        return pl.BlockSpec((nb, D_MODEL), lambda i: (mod_row0 // nb, k), pipeline_mode=pl.Buffered(1))

    in_specs = [
        x_spec, mod_spec(0), mod_spec(1), mod_spec(2),
        _const_spec((nb, N_STATE)), _const_spec((nb, N_STATE)), _const_spec((2 * nb, CONV_DIM)),
    ] + [_const_spec(w.shape) for w in weights]
    out_specs = [
        pl.BlockSpec((rows, D_MODEL), lambda i: (first_block + i, 0)),
        _const_spec((2 * nb, CONV_DIM)), _const_spec((nb, N_STATE)), _const_spec((nb, N_STATE)),
    ]
    out_shape = [
        jax.ShapeDtypeStruct((x1_rows, D_MODEL), F32),
        jax.ShapeDtypeStruct((2 * nb, CONV_DIM), F32),
        jax.ShapeDtypeStruct((nb, N_STATE), F32),
        jax.ShapeDtypeStruct((nb, N_STATE), F32),
    ]
    scratch = [
        pltpu.VMEM((rows, IN_DIM), F32),
        pltpu.VMEM((rows + 2 * nb, CONV_DIM), F32),
        pltpu.VMEM((rows, 2 * N_STATE), F32),
        pltpu.VMEM((nb, N_STATE), F32),
        pltpu.VMEM((nb, N_STATE), F32),
        pltpu.VMEM((D_MODEL, IN_DIM), BF16),
        pltpu.VMEM((SSM_DIM, SSM_DIM), BF16),
        pltpu.VMEM((D_MODEL, D_MODEL), BF16),
        pltpu.VMEM((2, HALF_SSM, 2 * HALF_STATE), BF16),
        pltpu.VMEM((2, HALF_SSM, HALF_STATE), BF16),
        pltpu.VMEM((2, HALF_SSM, HALF_STATE), BF16),
    ] + tm_scratch
    operands = [x, mod, mod, mod, h0r, h0i, cb0, *weights]
    if x1_dest is not None:
        in_specs = [pl.BlockSpec(memory_space=pl.ANY)] + in_specs
        operands = [x1_dest] + operands
    return pl.pallas_call(
        functools.partial(_mixer_kernel, nb, chunk, batch_major, x1_dest is not None, nchunks),
        grid=(n_steps,),
        in_specs=in_specs, out_specs=out_specs, out_shape=out_shape,
        scratch_shapes=scratch,
        input_output_aliases={0: 0} if x1_dest is not None else {},
        compiler_params=pltpu.CompilerParams(
            dimension_semantics=("arbitrary",), vmem_limit_bytes=VMEM_LIMIT),
        name="mixer",
    )(*operands)


TOK_BLOCK = 512
SEG_ALIGN = 16
ROW_TILE = 512
CHUNKS_PER_TILE = ROW_TILE // SEG_ALIGN
SCATTER_GROUPS = 8
GATHER_BUFFERS = 3
TABLE_CHUNK_ROWS = 64
N_SLOTS = -(-(2 * TOK_BLOCK + N_EXPERTS * (SEG_ALIGN - 1)) // 256) * 256
SPARE_ROW = N_SLOTS - SEG_ALIGN
assert SPARE_ROW >= 2 * TOK_BLOCK + N_EXPERTS * (SEG_ALIGN - 1)
SLOTS_ALWAYS = N_SLOTS - 12 * SEG_ALIGN
ROUTER_ROWS = 48
META_ROWS = SUBLANES

_NT = (((1,), (1,)), ((), ()))
_TN = (((0,), (0,)), ((), ()))


def _iota_f32(shape, dim):
    return lax.broadcasted_iota(jnp.int32, shape, dim).astype(F32)


def _split_bf16(v):
    hi = v.astype(BF16)
    return hi, (v - hi.astype(F32)).astype(BF16)


def _route_t(lg):
    row = _iota_f32(lg.shape, 0)
    neg = jnp.float32(-jnp.inf)
    none = float(ROUTER_ROWS)
    is_g = row < N_EXPERT_GROUPS
    gl = jnp.where(is_g, lg, neg)
    gmax = jnp.max(gl, axis=0, keepdims=True)
    gsum = jnp.sum(jnp.where(is_g, jnp.exp(gl - gmax), 0.0), axis=0, keepdims=True)
    g_prob = 1.0 / gsum
    g_idx = jnp.min(jnp.where(gl == gmax, row, none), axis=0, keepdims=True)
    lo = N_EXPERT_GROUPS + EXPERTS_PER_GROUP * g_idx
    el = jnp.where(row >= lo, jnp.where(row < lo + EXPERTS_PER_GROUP, lg, neg), neg)
    m1 = jnp.max(el, axis=0, keepdims=True)
    i1 = jnp.min(jnp.where(el == m1, row, none), axis=0, keepdims=True)
    el2 = jnp.where(row == i1, neg, el)
    m2 = jnp.max(el2, axis=0, keepdims=True)
    i2 = jnp.min(jnp.where(el2 == m2, row, none), axis=0, keepdims=True)
    e21 = jnp.exp(m2 - m1)
    return i1, i2, g_prob / (1.0 + e21), g_prob * e21 / (1.0 + e21)


def _dispatch_kernel(x1_ref, sc_ref, sh_ref, n2g_ref, wr_ref, br_ref, xs_ref, meta_ref, cnt_ref):
    h = _rms(x1_ref[...], n2g_ref[...]) * (1.0 + sc_ref[0, 0]) + sh_ref[0, 0]
    t = h.shape[0]
    hb, hl = _split_bf16(h)
    wh, wl = _split_bf16(wr_ref[...])
    lg = (lax.dot_general(wh, hb, _NT, preferred_element_type=F32)
          + lax.dot_general(wh, hl, _NT, preferred_element_type=F32)
          + lax.dot_general(wl, hb, _NT, preferred_element_type=F32)) + br_ref[...]
    i1, i2, w1, w2 = _route_t(lg)

    row = _iota_f32((ROUTER_ROWS, t), 0)
    hit1 = row == i1
    hit2 = row == i2
    onehot = jnp.where(hit1, 1.0, 0.0) + jnp.where(hit2, 1.0, 0.0)
    before = jnp.where(_iota_f32((t, t), 0) < _iota_f32((t, t), 1), 1.0, 0.0)
    rank = _bdot(onehot, before)
    cnt = jnp.sum(onehot, axis=1, keepdims=True)
    seg = jnp.floor((cnt + (SEG_ALIGN - 1)) * (1.0 / SEG_ALIGN)) * SEG_ALIGN
    earlier = jnp.where(_iota_f32((ROUTER_ROWS, ROUTER_ROWS), 1) < _iota_f32((ROUTER_ROWS, ROUTER_ROWS), 0),
                        1.0, 0.0)
    base = rank + _bdot(earlier, jnp.broadcast_to(seg, (ROUTER_ROWS, t)))
    slot1 = jnp.sum(jnp.where(hit1, base, 0.0), axis=0, keepdims=True)
    slot2 = jnp.sum(jnp.where(hit2, base, 0.0), axis=0, keepdims=True)
    def sorted_rows(first, n):
        j = _iota_f32((n, t), 0) + float(first)
        place = jnp.where(j == slot1, 1.0, 0.0) + jnp.where(j == slot2, 1.0, 0.0)
        return jnp.dot(place.astype(BF16), hb, preferred_element_type=F32).astype(BF16)

    xs_ref[0:SLOTS_ALWAYS, :] = sorted_rows(0, SLOTS_ALWAYS)
    used = jnp.sum(seg)

    @pl.when(used > SLOTS_ALWAYS)
    def _():
        xs_ref[SLOTS_ALWAYS:N_SLOTS, :] = sorted_rows(SLOTS_ALWAYS, N_SLOTS - SLOTS_ALWAYS)

    @pl.when(used <= SLOTS_ALWAYS)
    def _():
        xs_ref[SLOTS_ALWAYS:N_SLOTS, :] = jnp.zeros((N_SLOTS - SLOTS_ALWAYS, D_MODEL), BF16)

    r = _iota_f32((META_ROWS, t), 0)
    meta_ref[0] = jnp.where(r == 0.0, slot1, jnp.where(r == 1.0, slot2,
                            jnp.where(r == 2.0, w1, jnp.where(r == 3.0, w2, 0.0))))
    cnt_ref[0] = jnp.broadcast_to(cnt, (ROUTER_ROWS, LANES))


def _combine_kernel(nbp, x1_ref, gt_ref, meta_ref, ys_ref, fg_ref, yp_ref, ysm_ref, slab):
    i = pl.program_id(0)
    x1 = x1_ref[...]
    meta = meta_ref[0]
    slot1, slot2, w1, w2 = (meta[k:k + 1, :] for k in range(4))
    j = _iota_f32((N_SLOTS, x1.shape[0]), 0)
    pick = jnp.where(j == slot1, w1, 0.0) + jnp.where(j == slot2, w2, 0.0)
    moe = lax.dot_general(pick.astype(BF16), ys_ref[...], _TN, preferred_element_type=F32)
    y = _rms(x1 + gt_ref[0, 0] * moe, fg_ref[...])

    @pl.when(i < nbp)
    def _():
        nb, steps, d = yp_ref.shape
        for j in range(d // LANES):
            slab[j] = y[:, j * LANES:(j + 1) * LANES]
        for b in range(nb):
            for j in range(d // LANES):
                yp_ref[b, :, j * LANES:(j + 1) * LANES] = slab[j, pl.ds(b, steps, stride=nb), :]

    @pl.when(i >= nbp)
    def _():
        ysm_ref[...] = y


def _chunk_table_kernel(start_ref, m_ref, src_ref, out_ref):
    start = start_ref[...]
    end = start + m_ref[...]
    src = src_ref[...]

    def body(b, _):
        rows = pl.ds(pl.multiple_of(b * TABLE_CHUNK_ROWS, TABLE_CHUNK_ROWS), TABLE_CHUNK_ROWS)
        first = lax.convert_element_type(b * TABLE_CHUNK_ROWS, F32)
        i = _iota_f32((TABLE_CHUNK_ROWS, start.shape[1]), 0) + first
        val = jnp.where(i >= start, jnp.where(i < end, src + SEG_ALIGN * (i - start), 0.0), 0.0)
        out_ref[rows, :] = jnp.sum(val, axis=1, keepdims=True).astype(jnp.int32)
        return 0

    lax.fori_loop(0, out_ref.shape[0] // TABLE_CHUNK_ROWS, body, 0)


def _expert_kernel(te_ref, tnv_ref, tc0_ref, ctab_ref, nt_ref, xs_hbm, wg_ref, wu_ref, wd_ref,
                   ys_hbm, xbuf, ybuf, wgub, wdb, in_sem, out_sem):
    t = pl.program_id(0)
    nt = nt_ref[0]
    slot = t % 2
    xslot = t % GATHER_BUFFERS

    def chunk_rows(j):
        return pl.ds(pl.multiple_of(j * SEG_ALIGN, SEG_ALIGN), SEG_ALIGN)

    def hbm_rows(ref, r):
        return ref.at[pl.ds(pl.multiple_of(r, SEG_ALIGN), SEG_ALIGN), :]

    def in_copy(r, j, s):
        return pltpu.make_async_copy(hbm_rows(xs_hbm, r), xbuf.at[s, chunk_rows(j), :], in_sem.at[s])

    def out_copy(r, j, s):
        return pltpu.make_async_copy(ybuf.at[s, chunk_rows(j), :], hbm_rows(ys_hbm, r), out_sem.at[s])

    def start_gather(tile, s):
        c0 = tc0_ref[tile]
        nv = tnv_ref[tile]
        for j in range(CHUNKS_PER_TILE):
            in_copy(jnp.where(j < nv, ctab_ref[c0 + j], SPARE_ROW), j, s).start()

    def wait_gather(s):
        pltpu.make_async_copy(xs_hbm.at[pl.ds(0, ROW_TILE), :], xbuf.at[s], in_sem.at[s]).wait()

    def wait_scatter(tile, s):
        nv = tnv_ref[tile]

        @pl.when(nv == CHUNKS_PER_TILE)
        def _():
            pltpu.make_async_copy(ybuf.at[s], ys_hbm.at[pl.ds(0, ROW_TILE), :], out_sem.at[s]).wait()

        @pl.when(nv < CHUNKS_PER_TILE)
        def _():
            def body(j, _):
                out_copy(0, j, s).wait()
                return 0

            lax.fori_loop(0, nv, body, 0)

    def run_tile(full):
        ahead = t + GATHER_BUFFERS - 1
        start_gather(ahead, ahead % GATHER_BUFFERS)
        wait_gather(xslot)
        ab = jnp.dot(xbuf[xslot], wgub[...], preferred_element_type=F32)
        a, b = ab[:, :D_EXPERT], ab[:, D_EXPERT:]
        o = jnp.dot((a * _sigmoid(a) * b).astype(BF16), wdb[...], preferred_element_type=F32)
        c0 = tc0_ref[t]
        if full:
            group = ROW_TILE // SCATTER_GROUPS
            for r in range(0, ROW_TILE, group):
                ybuf[slot, r:r + group, :] = o[r:r + group].astype(BF16)
                for j in range(r // SEG_ALIGN, (r + group) // SEG_ALIGN):
                    out_copy(ctab_ref[c0 + j], j, slot).start()
        else:
            ybuf[slot] = o.astype(BF16)

            def body(j, _):
                out_copy(ctab_ref[c0 + j], j, slot).start()
                return 0

            lax.fori_loop(0, tnv_ref[t], body, 0)

    @pl.when(t == 0)
    def _():
        for k in range(GATHER_BUFFERS - 1):
            start_gather(k, k)

    @pl.when(jnp.logical_or(t == 0, te_ref[t] != te_ref[jnp.maximum(t - 1, 0)]))
    def _():
        wgub[:, :D_EXPERT] = wg_ref[0].astype(BF16)
        wgub[:, D_EXPERT:] = wu_ref[0].astype(BF16)
        wdb[...] = wd_ref[0].astype(BF16)

    @pl.when(t < nt)
    def _():
        @pl.when(t >= 2)
        def _():
            wait_scatter(t - 2, slot)

        pl.when(tnv_ref[t] == CHUNKS_PER_TILE)(lambda: run_tile(True))
        pl.when(tnv_ref[t] < CHUNKS_PER_TILE)(lambda: run_tile(False))

        @pl.when(t == nt - 1)
        def _():
            for k in range(1, GATHER_BUFFERS):
                wait_gather((t + k) % GATHER_BUFFERS)
            wait_scatter(t, slot)

            @pl.when(t >= 1)
            def _():
                wait_scatter(t - 1, 1 - slot)


def _tile_plan(cnt):
    nblk = cnt.shape[0]
    max_chunks = nblk * (2 * TOK_BLOCK // SEG_ALIGN + N_EXPERTS)
    max_tiles = max_chunks // CHUNKS_PER_TILE + N_EXPERTS
    table_len = -(-(max_chunks + CHUNKS_PER_TILE) // LANES) * LANES
    m = (cnt + SEG_ALIGN - 1) // SEG_ALIGN
    seg_start = jnp.cumsum(m, axis=1) - m
    src0 = jnp.arange(nblk, dtype=jnp.int32)[:, None] * N_SLOTS + SEG_ALIGN * seg_start
    seg_m = m.T.reshape(1, -1)
    seg_first = jnp.cumsum(seg_m, axis=1) - seg_m
    assert table_len % TABLE_CHUNK_ROWS == 0

    def row(v):
        return jnp.pad(v.astype(F32), ((0, 0), (0, -v.shape[1] % LANES)))

    chunk_src = pl.pallas_call(
        _chunk_table_kernel,
        out_shape=jax.ShapeDtypeStruct((table_len, 1), jnp.int32),
        name="moe_chunk_table",
    )(row(seg_first), row(seg_m), row(src0.T.reshape(1, -1))).reshape(-1)
    ce = jnp.sum(m, axis=0)
    c_start = jnp.cumsum(ce) - ce
    tiles = (ce + CHUNKS_PER_TILE - 1) // CHUNKS_PER_TILE
    t_cum = jnp.cumsum(tiles)
    nt = t_cum[-1]
    k = jnp.arange(max_tiles + GATHER_BUFFERS - 1, dtype=jnp.int32)
    kk = jnp.minimum(k, nt - 1)
    ek = jnp.minimum(jnp.sum((t_cum[None, :] <= kk[:, None]).astype(jnp.int32), axis=1), N_EXPERTS - 1)
    is_e = ek[:, None] == jnp.arange(N_EXPERTS, dtype=jnp.int32)[None, :]
    pick = lambda v: jnp.sum(jnp.where(is_e, v[None, :], 0), axis=1)
    local = kk - pick(t_cum - tiles)
    tile_nv = jnp.where(k < nt, jnp.clip(pick(ce) - CHUNKS_PER_TILE * local, 0, CHUNKS_PER_TILE), 0)
    tile_c0 = jnp.where(k < nt, pick(c_start) + CHUNKS_PER_TILE * local, 0)
    return tuple(v.astype(jnp.int32) for v in (ek, tile_nv, tile_c0, chunk_src, nt.reshape(1)))


def _moe(x1, bp, mod_rows, n2g, wr, br, wg, wu, wd, fg):
    nblk = x1.shape[0] // TOK_BLOCK
    nbp = nblk - 1
    blk = (TOK_BLOCK, D_MODEL)
    x1_spec = pl.BlockSpec(blk, lambda i: (i, 0))

    def mod_spec(k):
        return pl.BlockSpec((1, 1) + blk, lambda i: (k, jnp.minimum(i // nbp, 1), 0, 0))

    sh_spec, sc_spec, gt_spec = mod_spec(0), mod_spec(1), mod_spec(2)
    meta_spec = pl.BlockSpec((1, META_ROWS, TOK_BLOCK), lambda i: (i, 0, 0))
    slots_spec = pl.BlockSpec((N_SLOTS, D_MODEL), lambda i: (i, 0))
    params = pltpu.CompilerParams(dimension_semantics=("arbitrary",), vmem_limit_bytes=VMEM_LIMIT)

    def cs(shape):
        return pl.BlockSpec(shape, lambda i: (0,) * len(shape))

    xs, meta, cnt = pl.pallas_call(
        _dispatch_kernel,
        grid=(nblk,),
        in_specs=[x1_spec, sc_spec, sh_spec, cs((1, D_MODEL)),
                  cs((ROUTER_ROWS, D_MODEL)), cs((ROUTER_ROWS, TOK_BLOCK))],
        out_specs=[slots_spec, meta_spec,
                   pl.BlockSpec((1, ROUTER_ROWS, LANES), lambda i: (i, 0, 0))],
        out_shape=[jax.ShapeDtypeStruct((nblk * N_SLOTS, D_MODEL), BF16),
                   jax.ShapeDtypeStruct((nblk, META_ROWS, TOK_BLOCK), F32),
                   jax.ShapeDtypeStruct((nblk, ROUTER_ROWS, LANES), F32)],
        compiler_params=params,
        name="moe_dispatch",
    )(x1, mod_rows, mod_rows, n2g, wr, br)

    plan = _tile_plan(cnt[:, N_EXPERT_GROUPS:N_EXPERT_GROUPS + N_EXPERTS, 0].astype(jnp.int32))
    max_tiles = plan[0].shape[0] - (GATHER_BUFFERS - 1)

    def w_spec(shape):
        return pl.BlockSpec((1,) + shape, lambda t, te, *_: (te[t], 0, 0))

    ys = pl.pallas_call(
        _expert_kernel,
        grid_spec=pltpu.PrefetchScalarGridSpec(
            num_scalar_prefetch=len(plan),
            grid=(max_tiles,),
            in_specs=[pl.BlockSpec(memory_space=pl.ANY),
                      w_spec((D_MODEL, D_EXPERT)), w_spec((D_MODEL, D_EXPERT)), w_spec((D_EXPERT, D_MODEL))],
            out_specs=pl.BlockSpec(memory_space=pl.ANY),
            scratch_shapes=[pltpu.VMEM((GATHER_BUFFERS, ROW_TILE, D_MODEL), BF16),
                            pltpu.VMEM((2, ROW_TILE, D_MODEL), BF16),
                            pltpu.VMEM((D_MODEL, 2 * D_EXPERT), BF16),
                            pltpu.VMEM((D_EXPERT, D_MODEL), BF16),
                            pltpu.SemaphoreType.DMA((GATHER_BUFFERS,)), pltpu.SemaphoreType.DMA((2,))]),
        out_shape=jax.ShapeDtypeStruct(xs.shape, BF16),
        input_output_aliases={len(plan): 0},
        compiler_params=params,
        name="moe_experts",
    )(*plan, xs, wg, wu, wd)

    return pl.pallas_call(
        functools.partial(_combine_kernel, nbp),
        grid=(nblk,),
        in_specs=[x1_spec, gt_spec, meta_spec, slots_spec, cs((1, D_MODEL))],
        out_specs=[pl.BlockSpec((bp, TOK_BLOCK // bp, D_MODEL), lambda i: (0, jnp.minimum(i, nbp - 1), 0)),
                   pl.BlockSpec(blk, lambda i: (0, 0))],
        out_shape=[jax.ShapeDtypeStruct((bp, nbp * TOK_BLOCK // bp, D_MODEL), F32),
                   jax.ShapeDtypeStruct(blk, F32)],
        scratch_shapes=[pltpu.VMEM((D_MODEL // LANES, TOK_BLOCK, LANES), F32)],
        compiler_params=params,
        name="moe_combine",
    )(x1, mod_rows, meta, ys, fg)


def _to_time_major(x):
    b, s, d = x.shape
    return x.transpose(1, 0, 2).reshape(s * b, d)


def _from_time_major(x, b, s):
    return x.reshape(s, b, -1).transpose(1, 0, 2)


def kernel(x_prompt, x_sample, state_conv, state_ssm_re, state_ssm_im, c_prompt, c_sample, w_ada, b_ada, norm1_g, norm2_g, w_in, conv_w, lambda_re, lambda_im, log_dt, ssm_b_re, ssm_b_im, ssm_c_re, ssm_c_im, ssm_d, w_glu, b_glu, out_norm_conv_g, out_norm_ssm_g, w_out, w_router_group, b_router_group, w_router_expert, b_router_expert, w_expert_gate, w_expert_up, w_expert_down, final_norm_g):
    assert w_ada.shape[0] == 1, "single-layer step"
    bp, sp, _ = x_prompt.shape
    bs, ss, _ = x_sample.shape

    ar, ai, bbr, bbi = _discretise(lambda_re[0], lambda_im[0], log_dt[0], ssm_b_re[0], ssm_b_im[0])
    crt = ssm_c_re[0].transpose(1, 0, 2).reshape(SSM_GROUP, N_STATE)
    cit = ssm_c_im[0].transpose(1, 0, 2).reshape(SSM_GROUP, N_STATE)

    assert bs % bp == 0
    mod, mod_rows = _adaln(jnp.concatenate([c_sample, c_prompt], axis=0), bs, TOK_BLOCK, w_ada[0], b_ada[0])

    mixer_weights = (
        norm1_g[0].reshape(1, -1), w_in[0], conv_w[0], ar, ai, bbr, bbi, crt, cit,
        ssm_d[0].reshape(1, -1), w_glu[0], b_glu[0].reshape(1, -1),
        out_norm_conv_g[0].reshape(1, -1), out_norm_ssm_g[0].reshape(1, -1), w_out[0],
    )

    pad_rows = ROUTER_ROWS - N_EXPERT_GROUPS - N_EXPERTS
    w_router = jnp.concatenate(
        [w_router_group[0].T, w_router_expert[0].transpose(0, 2, 1).reshape(N_EXPERTS, D_MODEL),
         jnp.zeros((pad_rows, D_MODEL), F32)], axis=0)
    b_router = jnp.concatenate(
        [b_router_group[0], b_router_expert[0].reshape(-1), jnp.zeros((pad_rows,), F32)])
    b_router = jnp.broadcast_to(b_router[:, None], (ROUTER_ROWS, TOK_BLOCK))
    moe_weights = (norm2_g[0].reshape(1, -1), w_router, b_router,
                   w_expert_gate[0], w_expert_up[0], w_expert_down[0], final_norm_g.reshape(1, -1))

    assert bs * ss == TOK_BLOCK and (bp * sp) % TOK_BLOCK == 0 and TOK_BLOCK % bp == 0

    def mix(x, mod_row0, h0r, h0i, cb0, chunk, in_kernel_transpose, **x1_where):
        nb, s, _ = x.shape
        x1, cbo, hro, hio = _mixer(
            x if in_kernel_transpose else _to_time_major(x), nb, s, chunk,
            mod, mod_row0, h0r, h0i, cb0, mixer_weights, **x1_where)
        new_conv = cbo.reshape(2, nb, CONV_DIM).transpose(1, 0, 2)[None]
        new_re = hro.reshape(1, nb, SSM_GROUPS, SSM_STATE)
        new_im = hio.reshape(1, nb, SSM_GROUPS, SSM_STATE)
        return x1, new_conv, new_re, new_im

    zero_state = jnp.zeros((bp, N_STATE), F32)
    zero_conv = jnp.zeros((2 * bp, CONV_DIM), F32)
    x1, conv_p, re_p, im_p = mix(x_prompt, bs, zero_state, zero_state, zero_conv, TOK_BLOCK // bp, True,
                                 x1_rows=bp * sp + bs * ss)
    x1, conv_s, re_s, im_s = mix(
        x_sample, 0,
        state_ssm_re[0].reshape(bs, N_STATE), state_ssm_im[0].reshape(bs, N_STATE),
        state_conv[0].transpose(1, 0, 2).reshape(2 * bs, CONV_DIM), ss, False, x1_dest=x1)

    y_p, y_s = _moe(x1, bp, mod_rows, *moe_weights)
    return (y_p, _from_time_major(y_s, bs, ss), conv_p, re_p, im_p, conv_s, re_s, im_s)
```

```python
import functools

import jax
import jax.numpy as jnp
from jax import lax
from jax.experimental import pallas as pl
from jax.experimental.pallas import tpu as pltpu

D_MODEL = 1024
CONV_DIM = 512
SSM_DIM = 512
SSM_GROUP = 16
SSM_GROUPS = 32
SSM_STATE = 64
N_STATE = SSM_GROUPS * SSM_STATE
HALF_SSM = SSM_DIM // 2
HALF_STATE = N_STATE // 2
IN_DIM = 2048
N_EXPERT_GROUPS = 4
EXPERTS_PER_GROUP = 8
N_EXPERTS = 32
D_EXPERT = 256
N_MOD = 6
EPS = 1e-6

SUBLANES = 8
LANES = 128
SCAN_COLS = 512
STRIDE_PAD = 8
VMEM_LIMIT = 60 * 1024 * 1024

F32 = jnp.float32
BF16 = jnp.bfloat16


def _sigmoid(x):
    return 1.0 / (1.0 + jnp.exp(-x))


def _gelu_tanh(x):
    return 0.5 * x * (1.0 + jnp.tanh(0.7978845608028654 * (x + 0.044715 * (x * x * x))))


def _rms(x, g):
    return x * lax.rsqrt(jnp.mean(x * x, axis=-1, keepdims=True) + EPS) * g


def _bdot(a, b):
    return jnp.dot(a.astype(BF16), b.astype(BF16), preferred_element_type=F32)


def _per_seq(v, nb, scale, shift=None):
    rows, d = v.shape
    v3 = v.reshape(rows // nb, nb, d) * scale[None]
    if shift is not None:
        v3 = v3 + shift[None]
    return v3.reshape(rows, d)


def _disc_kernel(lr_ref, li_ref, ldt_ref, brt_ref, bit_ref, ar_ref, ai_ref, bbr_ref, bbi_ref):
    lr = lr_ref[...]
    li = li_ref[...]
    dt = jnp.exp(ldt_ref[...])
    mag = jnp.exp(lr * dt)
    ar = mag * jnp.cos(li * dt)
    ai = mag * jnp.sin(li * dt)
    den = lr * lr + li * li
    fr = ((ar - 1.0) * lr + ai * li) / den
    fi = (ai * lr - (ar - 1.0) * li) / den
    ar_ref[...] = ar
    ai_ref[...] = ai
    brt = brt_ref[...]
    bit = bit_ref[...]
    bbr_ref[...] = fr * brt - fi * bit
    bbi_ref[...] = fr * bit + fi * brt


def _discretise(lambda_re, lambda_im, log_dt, b_re, b_im):
    lr = lambda_re.reshape(1, N_STATE)
    li = lambda_im.reshape(1, N_STATE)
    ldt = jnp.repeat(log_dt, SSM_STATE).reshape(1, N_STATE)
    brt = b_re.transpose(2, 0, 1).reshape(SSM_GROUP, N_STATE)
    bit = b_im.transpose(2, 0, 1).reshape(SSM_GROUP, N_STATE)
    row = jax.ShapeDtypeStruct((1, N_STATE), F32)
    mat = jax.ShapeDtypeStruct((SSM_GROUP, N_STATE), F32)
    return pl.pallas_call(_disc_kernel, out_shape=(row, row, mat, mat), name="s5_discretise")(
        lr, li, ldt, brt, bit)


def _ada_kernel(n_first, block_rows, c_ref, w_ref, b_ref, o_ref, rows_ref):
    c = c_ref[...]
    o = _bdot(c * _sigmoid(c), w_ref[...]) + b_ref[...]
    o_ref[...] = o

    @pl.when(pl.program_id(0) >= N_MOD // 2)
    def _():
        n_rest = o.shape[0] - n_first
        rows_ref[0, 0] = jnp.tile(o[n_first:], (block_rows // n_rest, 1))
        rows_ref[0, 1] = jnp.tile(o[:n_first], (block_rows // n_first, 1))


def _adaln(c_all, n_first, block_rows, w_ada, b_ada):
    n = c_all.shape[0]
    half = N_MOD // 2
    return pl.pallas_call(
        functools.partial(_ada_kernel, n_first, block_rows),
        grid=(N_MOD,),
        in_specs=[
            pl.BlockSpec((n, D_MODEL), lambda j: (0, 0)),
            pl.BlockSpec((D_MODEL, D_MODEL), lambda j: (0, j)),
            pl.BlockSpec((1, D_MODEL), lambda j: (0, j)),
        ],
        out_specs=[pl.BlockSpec((n, D_MODEL), lambda j: (0, j)),
                   pl.BlockSpec((1, 2, block_rows, D_MODEL), lambda j: (jnp.maximum(j - half, 0), 0, 0, 0))],
        out_shape=[jax.ShapeDtypeStruct((n, N_MOD * D_MODEL), F32),
                   jax.ShapeDtypeStruct((half, 2, block_rows, D_MODEL), F32)],
        compiler_params=pltpu.CompilerParams(
            dimension_semantics=("arbitrary",), vmem_limit_bytes=VMEM_LIMIT),
        name="adaln",
    )(c_all, w_ada, b_ada.reshape(1, -1))


def _rows_to_time_major(x_ref, slab, xt_scr):
    nb, steps, d = x_ref.shape
    pitch = slab.shape[1] // nb
    for b in range(nb):
        for j in range(d // LANES):
            slab[j, b * pitch:b * pitch + steps, :] = x_ref[b, :, j * LANES:(j + 1) * LANES]
    for l in range(steps):
        for j in range(d // LANES):
            xt_scr[l * nb:(l + 1) * nb, j * LANES:(j + 1) * LANES] = slab[j, pl.ds(l, nb, stride=pitch), :]


def _group_diagonal(w_t, k):
    tiled = jnp.tile(w_t[:, k * HALF_STATE:(k + 1) * HALF_STATE], (SSM_GROUPS // 2, 1))
    row_group = lax.shift_right_logical(lax.broadcasted_iota(jnp.int32, tiled.shape, 0),
                                        SSM_GROUP.bit_length() - 1)
    lane_group = lax.shift_right_logical(lax.broadcasted_iota(jnp.int32, tiled.shape, 1),
                                         SSM_STATE.bit_length() - 1)
    return jnp.where(row_group == lane_group, tiled, 0.0).astype(BF16)


N_MIXER_INPUTS = 22


def _mixer_kernel(nb, steps, batch_major, has_dest, nchunks, *refs):
    refs = refs[1:] if has_dest else refs
    x1_ref = refs[N_MIXER_INPUTS]

    @pl.when(pl.program_id(0) < nchunks)
    def _():
        _mixer_step(nb, steps, batch_major, *refs)

    @pl.when(pl.program_id(0) >= nchunks)
    def _():
        x1_ref[...] = jnp.zeros_like(x1_ref)


def _mixer_step(nb, steps, batch_major, *refs):
    (x_ref, sh_ref, sc_ref, gt_ref, h0r_ref, h0i_ref, cb0_ref,
     n1g_ref, win_ref, cw_ref, ar_ref, ai_ref, bbr_ref, bbi_ref, crt_ref, cit_ref,
     dsk_ref, wglu_ref, bglu_ref, gcv_ref, gsm_ref, wout_ref,
     x1_ref, cbo_ref, hro_ref, hio_ref,
     z_scr, cv_scr, s_scr, hr_scr, hi_scr, win_b, wglu_b, wout_b, wb_ref, wcr_ref, wci_ref,
     *tm_scr) = refs
    assert refs[N_MIXER_INPUTS] is x1_ref
    rows = nb * steps
    i = pl.program_id(0)

    @pl.when(i == 0)
    def _():
        hr_scr[...] = h0r_ref[...]
        hi_scr[...] = h0i_ref[...]
        cv_scr[0:2 * nb, :] = cb0_ref[...]
        win_b[...] = win_ref[...].astype(BF16)
        wglu_b[...] = wglu_ref[...].astype(BF16)
        wout_b[...] = wout_ref[...].astype(BF16)
        for k in range(2):
            wb_ref[k, :, 0:HALF_STATE] = _group_diagonal(bbr_ref[...], k)
            wb_ref[k, :, HALF_STATE:2 * HALF_STATE] = _group_diagonal(bbi_ref[...], k)
            wcr_ref[k] = _group_diagonal(crt_ref[...], k)
            wci_ref[k] = _group_diagonal(cit_ref[...], k)

    if batch_major:
        slab, x_tm = tm_scr
        _rows_to_time_major(x_ref, slab, x_tm)
    else:
        x_tm = x_ref
    h = _per_seq(_rms(x_tm[...], n1g_ref[...]), nb, 1.0 + sc_ref[...], sh_ref[...])
    z_scr[...] = _bdot(h, win_b[...])

    u_cols = 3 * CONV_DIM
    for k in range(2):
        s_scr[:, 2 * k * HALF_STATE:2 * (k + 1) * HALF_STATE] = _bdot(
            z_scr[:, u_cols + k * HALF_SSM:u_cols + (k + 1) * HALF_SSM], wb_ref[k])

    def scan_cols(cg):
        cols = slice(cg * SCAN_COLS, (cg + 1) * SCAN_COLS)
        half, off = divmod(cg * SCAN_COLS, HALF_STATE)
        re_cols = slice(2 * half * HALF_STATE + off, 2 * half * HALF_STATE + off + SCAN_COLS)
        im_cols = slice(re_cols.start + HALF_STATE, re_cols.stop + HALF_STATE)
        a_r = jnp.broadcast_to(ar_ref[:, cols], (SUBLANES, SCAN_COLS))
        a_i = jnp.broadcast_to(ai_ref[:, cols], (SUBLANES, SCAN_COLS))

        def advance(hr, hi, rows_l):
            nr = a_r * hr - a_i * hi + s_scr[rows_l, re_cols]
            ni = a_r * hi + a_i * hr + s_scr[rows_l, im_cols]
            s_scr[rows_l, re_cols] = nr
            s_scr[rows_l, im_cols] = ni
            return nr, ni

        for r0 in range(0, nb, SUBLANES):
            tile = slice(r0, r0 + SUBLANES)
            hr, hi = hr_scr[tile, cols], hi_scr[tile, cols]
            for l in range(steps):
                hr, hi = advance(hr, hi, slice(l * nb + r0, l * nb + r0 + SUBLANES))
            hr_scr[tile, cols] = hr
            hi_scr[tile, cols] = hi

    y_halves = []
    for k in range(2):
        for cg in range(k * HALF_STATE // SCAN_COLS, (k + 1) * HALF_STATE // SCAN_COLS):
            scan_cols(cg)
        s_re = s_scr[:, 2 * k * HALF_STATE:(2 * k + 1) * HALF_STATE].astype(BF16)
        s_im = s_scr[:, (2 * k + 1) * HALF_STATE:(2 * k + 2) * HALF_STATE].astype(BF16)
        y_halves.append(lax.dot_general(s_re, wcr_ref[k], _NT, preferred_element_type=F32)
                        - lax.dot_general(s_im, wci_ref[k], _NT, preferred_element_type=F32))

    hro_ref[...] = hr_scr[...]
    hio_ref[...] = hi_scr[...]

    y = jnp.concatenate(y_halves, axis=-1) + dsk_ref[...] * z_scr[:, u_cols:IN_DIM]
    g = _gelu_tanh(y)
    y_ssm = _rms(g * _sigmoid(_bdot(g, wglu_b[...]) + bglu_ref[...]), gsm_ref[...])

    cv_scr[2 * nb:2 * nb + rows, :] = z_scr[:, CONV_DIM:2 * CONV_DIM] * z_scr[:, 2 * CONV_DIM:3 * CONV_DIM]
    conv = (cv_scr[0:rows, :] * cw_ref[0:1, :]
            + cv_scr[nb:nb + rows, :] * cw_ref[1:2, :]
            + cv_scr[2 * nb:2 * nb + rows, :] * cw_ref[2:3, :])
    y_conv = _rms(z_scr[:, 0:CONV_DIM] * conv, gcv_ref[...])
    carry = cv_scr[rows:rows + 2 * nb, :]
    cv_scr[0:2 * nb, :] = carry
    cbo_ref[...] = carry

    mixed = _bdot(y_conv, wout_b[0:CONV_DIM, :]) + _bdot(y_ssm, wout_b[CONV_DIM:2 * CONV_DIM, :])
    x1_ref[...] = x_tm[...] + _per_seq(mixed, nb, gt_ref[...])


def _const_spec(shape):
    return pl.BlockSpec(shape, lambda i: (0,) * len(shape), pipeline_mode=pl.Buffered(1))


def _mixer(x, nb, steps_total, chunk, mod, mod_row0, h0r, h0i, cb0, weights, x1_rows=None, x1_dest=None):
    rows = nb * chunk
    nchunks = steps_total // chunk
    batch_major = x.ndim == 3
    n_rows = steps_total * nb
    if x1_dest is not None:
        x1_rows = x1_dest.shape[0]
    elif x1_rows is None:
        x1_rows = n_rows
    first_block = (x1_rows - n_rows) // rows if x1_dest is not None else 0
    n_steps = nchunks if x1_dest is not None else x1_rows // rows
    last = nchunks - 1
    if batch_major:
        assert nb == SUBLANES
        x_spec = pl.BlockSpec((nb, chunk, D_MODEL), lambda i: (0, jnp.minimum(i, last), 0))
        tm_scratch = [pltpu.VMEM((D_MODEL // LANES, nb * (chunk + STRIDE_PAD), LANES), F32),
                      pltpu.VMEM((rows, D_MODEL), F32)]
    else:
        x_spec = pl.BlockSpec((rows, D_MODEL), lambda i: (jnp.minimum(i, last), 0))
        tm_scratch = []
    assert mod_row0 % nb == 0

    def mod_spec(k):
        return pl.BlockSpec((nb, D_MODEL), lambda i: (mod_row0 // nb, k), pipeline_mode=pl.Buffered(1))

    in_specs = [
        x_spec, mod_spec(0), mod_spec(1), mod_spec(2),
        _const_spec((nb, N_STATE)), _const_spec((nb, N_STATE)), _const_spec((2 * nb, CONV_DIM)),
    ] + [_const_spec(w.shape) for w in weights]
    out_specs = [
        pl.BlockSpec((rows, D_MODEL), lambda i: (first_block + i, 0)),
        _const_spec((2 * nb, CONV_DIM)), _const_spec((nb, N_STATE)), _const_spec((nb, N_STATE)),
    ]
    out_shape = [
        jax.ShapeDtypeStruct((x1_rows, D_MODEL), F32),
        jax.ShapeDtypeStruct((2 * nb, CONV_DIM), F32),
        jax.ShapeDtypeStruct((nb, N_STATE), F32),
        jax.ShapeDtypeStruct((nb, N_STATE), F32),
    ]
    scratch = [
        pltpu.VMEM((rows, IN_DIM), F32),
        pltpu.VMEM((rows + 2 * nb, CONV_DIM), F32),
        pltpu.VMEM((rows, 2 * N_STATE), F32),
        pltpu.VMEM((nb, N_STATE), F32),
        pltpu.VMEM((nb, N_STATE), F32),
        pltpu.VMEM((D_MODEL, IN_DIM), BF16),
        pltpu.VMEM((SSM_DIM, SSM_DIM), BF16),
        pltpu.VMEM((D_MODEL, D_MODEL), BF16),
        pltpu.VMEM((2, HALF_SSM, 2 * HALF_STATE), BF16),
        pltpu.VMEM((2, HALF_SSM, HALF_STATE), BF16),
        pltpu.VMEM((2, HALF_SSM, HALF_STATE), BF16),
    ] + tm_scratch
    operands = [x, mod, mod, mod, h0r, h0i, cb0, *weights]
    if x1_dest is not None:
        in_specs = [pl.BlockSpec(memory_space=pl.ANY)] + in_specs
        operands = [x1_dest] + operands
    return pl.pallas_call(
        functools.partial(_mixer_kernel, nb, chunk, batch_major, x1_dest is not None, nchunks),
        grid=(n_steps,),
        in_specs=in_specs, out_specs=out_specs, out_shape=out_shape,
        scratch_shapes=scratch,
        input_output_aliases={0: 0} if x1_dest is not None else {},
        compiler_params=pltpu.CompilerParams(
            dimension_semantics=("arbitrary",), vmem_limit_bytes=VMEM_LIMIT),
        name="mixer",
    )(*operands)


TOK_BLOCK = 512
SEG_ALIGN = 16
ROW_TILE = 512
CHUNKS_PER_TILE = ROW_TILE // SEG_ALIGN
SCATTER_GROUPS = 8
GATHER_BUFFERS = 3
TABLE_CHUNK_ROWS = 64
N_SLOTS = -(-(2 * TOK_BLOCK + N_EXPERTS * (SEG_ALIGN - 1)) // 256) * 256
SPARE_ROW = N_SLOTS - SEG_ALIGN
assert SPARE_ROW >= 2 * TOK_BLOCK + N_EXPERTS * (SEG_ALIGN - 1)
SLOTS_ALWAYS = N_SLOTS - 12 * SEG_ALIGN
COMBINE_SLOTS = N_SLOTS - 256
ROUTER_ROWS = 48
META_ROWS = SUBLANES

_NT = (((1,), (1,)), ((), ()))
_TN = (((0,), (0,)), ((), ()))


def _iota_f32(shape, dim):
    return lax.broadcasted_iota(jnp.int32, shape, dim).astype(F32)


def _split_bf16(v):
    hi = v.astype(BF16)
    return hi, (v - hi.astype(F32)).astype(BF16)


def _route_t(lg):
    row = _iota_f32(lg.shape, 0)
    neg = jnp.float32(-jnp.inf)
    none = float(ROUTER_ROWS)
    is_g = row < N_EXPERT_GROUPS
    gl = jnp.where(is_g, lg, neg)
    gmax = jnp.max(gl, axis=0, keepdims=True)
    gsum = jnp.sum(jnp.where(is_g, jnp.exp(gl - gmax), 0.0), axis=0, keepdims=True)
    g_prob = 1.0 / gsum
    g_idx = jnp.min(jnp.where(gl == gmax, row, none), axis=0, keepdims=True)
    lo = N_EXPERT_GROUPS + EXPERTS_PER_GROUP * g_idx
    el = jnp.where(row >= lo, jnp.where(row < lo + EXPERTS_PER_GROUP, lg, neg), neg)
    m1 = jnp.max(el, axis=0, keepdims=True)
    i1 = jnp.min(jnp.where(el == m1, row, none), axis=0, keepdims=True)
    el2 = jnp.where(row == i1, neg, el)
    m2 = jnp.max(el2, axis=0, keepdims=True)
    i2 = jnp.min(jnp.where(el2 == m2, row, none), axis=0, keepdims=True)
    e21 = jnp.exp(m2 - m1)
    return i1, i2, g_prob / (1.0 + e21), g_prob * e21 / (1.0 + e21)


def _dispatch_kernel(x1_ref, sc_ref, sh_ref, n2g_ref, wr_ref, br_ref, xs_ref, meta_ref, cnt_ref):
    h = _rms(x1_ref[...], n2g_ref[...]) * (1.0 + sc_ref[0, 0]) + sh_ref[0, 0]
    t = h.shape[0]
    hb, hl = _split_bf16(h)
    wh, wl = _split_bf16(wr_ref[...])
    lg = (lax.dot_general(wh, hb, _NT, preferred_element_type=F32)
          + lax.dot_general(wh, hl, _NT, preferred_element_type=F32)
          + lax.dot_general(wl, hb, _NT, preferred_element_type=F32)) + br_ref[...]
    i1, i2, w1, w2 = _route_t(lg)

    row = _iota_f32((ROUTER_ROWS, t), 0)
    hit1 = row == i1
    hit2 = row == i2
    onehot = jnp.where(hit1, 1.0, 0.0) + jnp.where(hit2, 1.0, 0.0)
    before = jnp.where(_iota_f32((t, t), 0) < _iota_f32((t, t), 1), 1.0, 0.0)
    rank = _bdot(onehot, before)
    cnt = jnp.sum(onehot, axis=1, keepdims=True)
    seg = jnp.floor((cnt + (SEG_ALIGN - 1)) * (1.0 / SEG_ALIGN)) * SEG_ALIGN
    earlier = jnp.where(_iota_f32((ROUTER_ROWS, ROUTER_ROWS), 1) < _iota_f32((ROUTER_ROWS, ROUTER_ROWS), 0),
                        1.0, 0.0)
    base = rank + _bdot(earlier, jnp.broadcast_to(seg, (ROUTER_ROWS, t)))
    slot1 = jnp.sum(jnp.where(hit1, base, 0.0), axis=0, keepdims=True)
    slot2 = jnp.sum(jnp.where(hit2, base, 0.0), axis=0, keepdims=True)
    def sorted_rows(first, n):
        j = _iota_f32((n, t), 0) + float(first)
        place = jnp.where(j == slot1, 1.0, 0.0) + jnp.where(j == slot2, 1.0, 0.0)
        return jnp.dot(place.astype(BF16), hb, preferred_element_type=F32).astype(BF16)

    xs_ref[0:SLOTS_ALWAYS, :] = sorted_rows(0, SLOTS_ALWAYS)
    used = jnp.sum(seg)

    @pl.when(used > SLOTS_ALWAYS)
    def _():
        xs_ref[SLOTS_ALWAYS:N_SLOTS, :] = sorted_rows(SLOTS_ALWAYS, N_SLOTS - SLOTS_ALWAYS)

    @pl.when(used <= SLOTS_ALWAYS)
    def _():
        xs_ref[SLOTS_ALWAYS:N_SLOTS, :] = jnp.zeros((N_SLOTS - SLOTS_ALWAYS, D_MODEL), BF16)

    r = _iota_f32((META_ROWS, t), 0)
    meta_ref[0] = jnp.where(r == 0.0, slot1, jnp.where(r == 1.0, slot2, jnp.where(
        r == 2.0, w1, jnp.where(r == 3.0, w2, jnp.where(r == 4.0, used, 0.0)))))
    cnt_ref[0] = jnp.broadcast_to(cnt, (ROUTER_ROWS, LANES))


def _combine_kernel(nbp, x1_ref, gt_ref, meta_ref, ys_ref, fg_ref, yp_ref, ysm_ref, slab):
    i = pl.program_id(0)
    meta = meta_ref[0]
    slot1, slot2, w1, w2 = (meta[k:k + 1, :] for k in range(4))
    used = jnp.max(meta[4:5, :])

    def combine(n_slots):
        j = _iota_f32((n_slots, TOK_BLOCK), 0)
        pick = jnp.where(j == slot1, w1, 0.0) + jnp.where(j == slot2, w2, 0.0)
        moe = lax.dot_general(pick.astype(BF16), ys_ref[0:n_slots, :], _TN, preferred_element_type=F32)
        y = _rms(x1_ref[...] + gt_ref[0, 0] * moe, fg_ref[...])

        @pl.when(i < nbp)
        def _():
            nb, steps, d = yp_ref.shape
            for j in range(d // LANES):
                slab[j] = y[:, j * LANES:(j + 1) * LANES]
            for b in range(nb):
                for j in range(d // LANES):
                    yp_ref[b, :, j * LANES:(j + 1) * LANES] = slab[j, pl.ds(b, steps, stride=nb), :]

        @pl.when(i >= nbp)
        def _():
            ysm_ref[...] = y

    pl.when(used <= COMBINE_SLOTS)(lambda: combine(COMBINE_SLOTS))
    pl.when(used > COMBINE_SLOTS)(lambda: combine(N_SLOTS))


def _chunk_table_kernel(start_ref, m_ref, src_ref, out_ref):
    start = start_ref[...]
    end = start + m_ref[...]
    src = src_ref[...]

    def body(b, _):
        rows = pl.ds(pl.multiple_of(b * TABLE_CHUNK_ROWS, TABLE_CHUNK_ROWS), TABLE_CHUNK_ROWS)
        first = lax.convert_element_type(b * TABLE_CHUNK_ROWS, F32)
        i = _iota_f32((TABLE_CHUNK_ROWS, start.shape[1]), 0) + first
        val = jnp.where(i >= start, jnp.where(i < end, src + SEG_ALIGN * (i - start), 0.0), 0.0)
        out_ref[rows, :] = jnp.sum(val, axis=1, keepdims=True).astype(jnp.int32)
        return 0

    lax.fori_loop(0, out_ref.shape[0] // TABLE_CHUNK_ROWS, body, 0)


def _expert_kernel(te_ref, tnv_ref, tc0_ref, ctab_ref, nt_ref, xs_hbm, wg_ref, wu_ref, wd_ref,
                   ys_hbm, xbuf, ybuf, wgub, wdb, in_sem, out_sem):
    t = pl.program_id(0)
    nt = nt_ref[0]
    slot = t % 2
    xslot = t % GATHER_BUFFERS

    def chunk_rows(j):
        return pl.ds(pl.multiple_of(j * SEG_ALIGN, SEG_ALIGN), SEG_ALIGN)

    def hbm_rows(ref, r):
        return ref.at[pl.ds(pl.multiple_of(r, SEG_ALIGN), SEG_ALIGN), :]

    def in_copy(r, j, s):
        return pltpu.make_async_copy(hbm_rows(xs_hbm, r), xbuf.at[s, chunk_rows(j), :], in_sem.at[s])

    def out_copy(r, j, s):
        return pltpu.make_async_copy(ybuf.at[s, chunk_rows(j), :], hbm_rows(ys_hbm, r), out_sem.at[s])

    def start_gather(tile, s):
        c0 = tc0_ref[tile]
        nv = tnv_ref[tile]
        for j in range(CHUNKS_PER_TILE):
            in_copy(jnp.where(j < nv, ctab_ref[c0 + j], SPARE_ROW), j, s).start()

    def wait_gather(s):
        pltpu.make_async_copy(xs_hbm.at[pl.ds(0, ROW_TILE), :], xbuf.at[s], in_sem.at[s]).wait()

    def wait_scatter(tile, s):
        nv = tnv_ref[tile]

        @pl.when(nv == CHUNKS_PER_TILE)
        def _():
            pltpu.make_async_copy(ybuf.at[s], ys_hbm.at[pl.ds(0, ROW_TILE), :], out_sem.at[s]).wait()

        @pl.when(nv < CHUNKS_PER_TILE)
        def _():
            def body(j, _):
                out_copy(0, j, s).wait()
                return 0

            lax.fori_loop(0, nv, body, 0)

    def run_tile(full):
        ahead = t + GATHER_BUFFERS - 1
        start_gather(ahead, ahead % GATHER_BUFFERS)
        wait_gather(xslot)
        ab = jnp.dot(xbuf[xslot], wgub[...], preferred_element_type=F32)
        a, b = ab[:, :D_EXPERT], ab[:, D_EXPERT:]
        o = jnp.dot((a * _sigmoid(a) * b).astype(BF16), wdb[...], preferred_element_type=F32)
        c0 = tc0_ref[t]
        if full:
            group = ROW_TILE // SCATTER_GROUPS
            for r in range(0, ROW_TILE, group):
                ybuf[slot, r:r + group, :] = o[r:r + group].astype(BF16)
                for j in range(r // SEG_ALIGN, (r + group) // SEG_ALIGN):
                    out_copy(ctab_ref[c0 + j], j, slot).start()
        else:
            ybuf[slot] = o.astype(BF16)

            def body(j, _):
                out_copy(ctab_ref[c0 + j], j, slot).start()
                return 0

            lax.fori_loop(0, tnv_ref[t], body, 0)

    @pl.when(t == 0)
    def _():
        for k in range(GATHER_BUFFERS - 1):
            start_gather(k, k)

    @pl.when(jnp.logical_or(t == 0, te_ref[t] != te_ref[jnp.maximum(t - 1, 0)]))
    def _():
        wgub[:, :D_EXPERT] = wg_ref[0].astype(BF16)
        wgub[:, D_EXPERT:] = wu_ref[0].astype(BF16)
        wdb[...] = wd_ref[0].astype(BF16)

    @pl.when(t < nt)
    def _():
        @pl.when(t >= 2)
        def _():
            wait_scatter(t - 2, slot)

        pl.when(tnv_ref[t] == CHUNKS_PER_TILE)(lambda: run_tile(True))
        pl.when(tnv_ref[t] < CHUNKS_PER_TILE)(lambda: run_tile(False))

        @pl.when(t == nt - 1)
        def _():
            for k in range(1, GATHER_BUFFERS):
                wait_gather((t + k) % GATHER_BUFFERS)
            wait_scatter(t, slot)

            @pl.when(t >= 1)
            def _():
                wait_scatter(t - 1, 1 - slot)


def _tile_plan(cnt):
    nblk = cnt.shape[0]
    max_chunks = nblk * (2 * TOK_BLOCK // SEG_ALIGN + N_EXPERTS)
    max_tiles = max_chunks // CHUNKS_PER_TILE + N_EXPERTS
    table_len = -(-(max_chunks + CHUNKS_PER_TILE) // LANES) * LANES
    m = (cnt + SEG_ALIGN - 1) // SEG_ALIGN
    seg_start = jnp.cumsum(m, axis=1) - m
    src0 = jnp.arange(nblk, dtype=jnp.int32)[:, None] * N_SLOTS + SEG_ALIGN * seg_start
    seg_m = m.T.reshape(1, -1)
    seg_first = jnp.cumsum(seg_m, axis=1) - seg_m
    assert table_len % TABLE_CHUNK_ROWS == 0

    def row(v):
        return jnp.pad(v.astype(F32), ((0, 0), (0, -v.shape[1] % LANES)))

    chunk_src = pl.pallas_call(
        _chunk_table_kernel,
        out_shape=jax.ShapeDtypeStruct((table_len, 1), jnp.int32),
        name="moe_chunk_table",
    )(row(seg_first), row(seg_m), row(src0.T.reshape(1, -1))).reshape(-1)
    ce = jnp.sum(m, axis=0)
    c_start = jnp.cumsum(ce) - ce
    tiles = (ce + CHUNKS_PER_TILE - 1) // CHUNKS_PER_TILE
    t_cum = jnp.cumsum(tiles)
    nt = t_cum[-1]
    k = jnp.arange(max_tiles + GATHER_BUFFERS - 1, dtype=jnp.int32)
    kk = jnp.minimum(k, nt - 1)
    ek = jnp.minimum(jnp.sum((t_cum[None, :] <= kk[:, None]).astype(jnp.int32), axis=1), N_EXPERTS - 1)
    is_e = ek[:, None] == jnp.arange(N_EXPERTS, dtype=jnp.int32)[None, :]
    pick = lambda v: jnp.sum(jnp.where(is_e, v[None, :], 0), axis=1)
    local = kk - pick(t_cum - tiles)
    tile_nv = jnp.where(k < nt, jnp.clip(pick(ce) - CHUNKS_PER_TILE * local, 0, CHUNKS_PER_TILE), 0)
    tile_c0 = jnp.where(k < nt, pick(c_start) + CHUNKS_PER_TILE * local, 0)
    return tuple(v.astype(jnp.int32) for v in (ek, tile_nv, tile_c0, chunk_src, nt.reshape(1)))


def _moe(x1, bp, mod_rows, n2g, wr, br, wg, wu, wd, fg):
    nblk = x1.shape[0] // TOK_BLOCK
    nbp = nblk - 1
    blk = (TOK_BLOCK, D_MODEL)
    x1_spec = pl.BlockSpec(blk, lambda i: (i, 0))

    def mod_spec(k):
        return pl.BlockSpec((1, 1) + blk, lambda i: (k, jnp.minimum(i // nbp, 1), 0, 0))

    sh_spec, sc_spec, gt_spec = mod_spec(0), mod_spec(1), mod_spec(2)
    meta_spec = pl.BlockSpec((1, META_ROWS, TOK_BLOCK), lambda i: (i, 0, 0))
    slots_spec = pl.BlockSpec((N_SLOTS, D_MODEL), lambda i: (i, 0))
    params = pltpu.CompilerParams(dimension_semantics=("arbitrary",), vmem_limit_bytes=VMEM_LIMIT)

    def cs(shape):
        return pl.BlockSpec(shape, lambda i: (0,) * len(shape))

    xs, meta, cnt = pl.pallas_call(
        _dispatch_kernel,
        grid=(nblk,),
        in_specs=[x1_spec, sc_spec, sh_spec, cs((1, D_MODEL)),
                  cs((ROUTER_ROWS, D_MODEL)), cs((ROUTER_ROWS, TOK_BLOCK))],
        out_specs=[slots_spec, meta_spec,
                   pl.BlockSpec((1, ROUTER_ROWS, LANES), lambda i: (i, 0, 0))],
        out_shape=[jax.ShapeDtypeStruct((nblk * N_SLOTS, D_MODEL), BF16),
                   jax.ShapeDtypeStruct((nblk, META_ROWS, TOK_BLOCK), F32),
                   jax.ShapeDtypeStruct((nblk, ROUTER_ROWS, LANES), F32)],
        compiler_params=params,
        name="moe_dispatch",
    )(x1, mod_rows, mod_rows, n2g, wr, br)

    plan = _tile_plan(cnt[:, N_EXPERT_GROUPS:N_EXPERT_GROUPS + N_EXPERTS, 0].astype(jnp.int32))
    max_tiles = plan[0].shape[0] - (GATHER_BUFFERS - 1)

    def w_spec(shape):
        return pl.BlockSpec((1,) + shape, lambda t, te, *_: (te[t], 0, 0))

    ys = pl.pallas_call(
        _expert_kernel,
        grid_spec=pltpu.PrefetchScalarGridSpec(
            num_scalar_prefetch=len(plan),
            grid=(max_tiles,),
            in_specs=[pl.BlockSpec(memory_space=pl.ANY),
                      w_spec((D_MODEL, D_EXPERT)), w_spec((D_MODEL, D_EXPERT)), w_spec((D_EXPERT, D_MODEL))],
            out_specs=pl.BlockSpec(memory_space=pl.ANY),
            scratch_shapes=[pltpu.VMEM((GATHER_BUFFERS, ROW_TILE, D_MODEL), BF16),
                            pltpu.VMEM((2, ROW_TILE, D_MODEL), BF16),
                            pltpu.VMEM((D_MODEL, 2 * D_EXPERT), BF16),
                            pltpu.VMEM((D_EXPERT, D_MODEL), BF16),
                            pltpu.SemaphoreType.DMA((GATHER_BUFFERS,)), pltpu.SemaphoreType.DMA((2,))]),
        out_shape=jax.ShapeDtypeStruct(xs.shape, BF16),
        input_output_aliases={len(plan): 0},
        compiler_params=params,
        name="moe_experts",
    )(*plan, xs, wg, wu, wd)

    return pl.pallas_call(
        functools.partial(_combine_kernel, nbp),
        grid=(nblk,),
        in_specs=[x1_spec, gt_spec, meta_spec, slots_spec, cs((1, D_MODEL))],
        out_specs=[pl.BlockSpec((bp, TOK_BLOCK // bp, D_MODEL), lambda i: (0, jnp.minimum(i, nbp - 1), 0)),
                   pl.BlockSpec(blk, lambda i: (0, 0))],
        out_shape=[jax.ShapeDtypeStruct((bp, nbp * TOK_BLOCK // bp, D_MODEL), F32),
                   jax.ShapeDtypeStruct(blk, F32)],
        scratch_shapes=[pltpu.VMEM((D_MODEL // LANES, TOK_BLOCK, LANES), F32)],
        compiler_params=params,
        name="moe_combine",
    )(x1, mod_rows, meta, ys, fg)


def _to_time_major(x):
    b, s, d = x.shape
    return x.transpose(1, 0, 2).reshape(s * b, d)


def _from_time_major(x, b, s):
    return x.reshape(s, b, -1).transpose(1, 0, 2)


def kernel(x_prompt, x_sample, state_conv, state_ssm_re, state_ssm_im, c_prompt, c_sample, w_ada, b_ada, norm1_g, norm2_g, w_in, conv_w, lambda_re, lambda_im, log_dt, ssm_b_re, ssm_b_im, ssm_c_re, ssm_c_im, ssm_d, w_glu, b_glu, out_norm_conv_g, out_norm_ssm_g, w_out, w_router_group, b_router_group, w_router_expert, b_router_expert, w_expert_gate, w_expert_up, w_expert_down, final_norm_g):
    assert w_ada.shape[0] == 1, "single-layer step"
    bp, sp, _ = x_prompt.shape
    bs, ss, _ = x_sample.shape

    ar, ai, bbr, bbi = _discretise(lambda_re[0], lambda_im[0], log_dt[0], ssm_b_re[0], ssm_b_im[0])
    crt = ssm_c_re[0].transpose(1, 0, 2).reshape(SSM_GROUP, N_STATE)
    cit = ssm_c_im[0].transpose(1, 0, 2).reshape(SSM_GROUP, N_STATE)

    assert bs % bp == 0
    mod, mod_rows = _adaln(jnp.concatenate([c_sample, c_prompt], axis=0), bs, TOK_BLOCK, w_ada[0], b_ada[0])

    mixer_weights = (
        norm1_g[0].reshape(1, -1), w_in[0], conv_w[0], ar, ai, bbr, bbi, crt, cit,
        ssm_d[0].reshape(1, -1), w_glu[0], b_glu[0].reshape(1, -1),
        out_norm_conv_g[0].reshape(1, -1), out_norm_ssm_g[0].reshape(1, -1), w_out[0],
    )

    pad_rows = ROUTER_ROWS - N_EXPERT_GROUPS - N_EXPERTS
    w_router = jnp.concatenate(
        [w_router_group[0].T, w_router_expert[0].transpose(0, 2, 1).reshape(N_EXPERTS, D_MODEL),
         jnp.zeros((pad_rows, D_MODEL), F32)], axis=0)
    b_router = jnp.concatenate(
        [b_router_group[0], b_router_expert[0].reshape(-1), jnp.zeros((pad_rows,), F32)])
    b_router = jnp.broadcast_to(b_router[:, None], (ROUTER_ROWS, TOK_BLOCK))
    moe_weights = (norm2_g[0].reshape(1, -1), w_router, b_router,
                   w_expert_gate[0], w_expert_up[0], w_expert_down[0], final_norm_g.reshape(1, -1))

    assert bs * ss == TOK_BLOCK and (bp * sp) % TOK_BLOCK == 0 and TOK_BLOCK % bp == 0

    def mix(x, mod_row0, h0r, h0i, cb0, chunk, in_kernel_transpose, **x1_where):
        nb, s, _ = x.shape
        x1, cbo, hro, hio = _mixer(
            x if in_kernel_transpose else _to_time_major(x), nb, s, chunk,
            mod, mod_row0, h0r, h0i, cb0, mixer_weights, **x1_where)
        new_conv = cbo.reshape(2, nb, CONV_DIM).transpose(1, 0, 2)[None]
        new_re = hro.reshape(1, nb, SSM_GROUPS, SSM_STATE)
        new_im = hio.reshape(1, nb, SSM_GROUPS, SSM_STATE)
        return x1, new_conv, new_re, new_im

    zero_state = jnp.zeros((bp, N_STATE), F32)
    zero_conv = jnp.zeros((2 * bp, CONV_DIM), F32)
    x1, conv_p, re_p, im_p = mix(x_prompt, bs, zero_state, zero_state, zero_conv, TOK_BLOCK // bp, True,
                                 x1_rows=bp * sp + bs * ss)
    x1, conv_s, re_s, im_s = mix(
        x_sample, 0,
        state_ssm_re[0].reshape(bs, N_STATE), state_ssm_im[0].reshape(bs, N_STATE),
        state_conv[0].transpose(1, 0, 2).reshape(2 * bs, CONV_DIM), ss, False, x1_dest=x1)

    y_p, y_s = _moe(x1, bp, mod_rows, *moe_weights)
    return (y_p, _from_time_major(y_s, bs, ss), conv_p, re_p, im_p, conv_s, re_s, im_s)
```

```python
import functools

import jax
import jax.numpy as jnp
from jax import lax
from jax.experimental import pallas as pl
from jax.experimental.pallas import tpu as pltpu

D_MODEL = 1024
CONV_DIM = 512
SSM_DIM = 512
SSM_GROUP = 16
SSM_GROUPS = 32
SSM_STATE = 64
N_STATE = SSM_GROUPS * SSM_STATE
HALF_SSM = SSM_DIM // 2
HALF_STATE = N_STATE // 2
IN_DIM = 2048
N_EXPERT_GROUPS = 4
EXPERTS_PER_GROUP = 8
N_EXPERTS = 32
D_EXPERT = 256
N_MOD = 6
EPS = 1e-6

SUBLANES = 8
LANES = 128
SCAN_COLS = 512
STRIDE_PAD = 8
VMEM_LIMIT = 60 * 1024 * 1024

F32 = jnp.float32
BF16 = jnp.bfloat16


def _sigmoid(x):
    return 1.0 / (1.0 + jnp.exp(-x))


def _gelu_tanh(x):
    return 0.5 * x * (1.0 + jnp.tanh(0.7978845608028654 * (x + 0.044715 * (x * x * x))))


def _rms(x, g):
    return x * lax.rsqrt(jnp.mean(x * x, axis=-1, keepdims=True) + EPS) * g


def _bdot(a, b):
    return jnp.dot(a.astype(BF16), b.astype(BF16), preferred_element_type=F32)


def _per_seq(v, nb, scale, shift=None):
    rows, d = v.shape
    v3 = v.reshape(rows // nb, nb, d) * scale[None]
    if shift is not None:
        v3 = v3 + shift[None]
    return v3.reshape(rows, d)


def _disc_kernel(lr_ref, li_ref, ldt_ref, brt_ref, bit_ref, ar_ref, ai_ref, bbr_ref, bbi_ref):
    lr = lr_ref[...]
    li = li_ref[...]
    dt = jnp.exp(ldt_ref[...])
    mag = jnp.exp(lr * dt)
    ar = mag * jnp.cos(li * dt)
    ai = mag * jnp.sin(li * dt)
    den = lr * lr + li * li
    fr = ((ar - 1.0) * lr + ai * li) / den
    fi = (ai * lr - (ar - 1.0) * li) / den
    ar_ref[...] = ar
    ai_ref[...] = ai
    brt = brt_ref[...]
    bit = bit_ref[...]
    bbr_ref[...] = fr * brt - fi * bit
    bbi_ref[...] = fr * bit + fi * brt


def _discretise(lambda_re, lambda_im, log_dt, b_re, b_im):
    lr = lambda_re.reshape(1, N_STATE)
    li = lambda_im.reshape(1, N_STATE)
    ldt = jnp.repeat(log_dt, SSM_STATE).reshape(1, N_STATE)
    brt = b_re.transpose(2, 0, 1).reshape(SSM_GROUP, N_STATE)
    bit = b_im.transpose(2, 0, 1).reshape(SSM_GROUP, N_STATE)
    row = jax.ShapeDtypeStruct((1, N_STATE), F32)
    mat = jax.ShapeDtypeStruct((SSM_GROUP, N_STATE), F32)
    return pl.pallas_call(_disc_kernel, out_shape=(row, row, mat, mat), name="s5_discretise")(
        lr, li, ldt, brt, bit)


def _ada_kernel(n_first, block_rows, c_ref, w_ref, b_ref, o_ref, rows_ref):
    c = c_ref[...]
    o = _bdot(c * _sigmoid(c), w_ref[...]) + b_ref[...]
    o_ref[...] = o

    @pl.when(pl.program_id(0) >= N_MOD // 2)
    def _():
        n_rest = o.shape[0] - n_first
        rows_ref[0, 0] = jnp.tile(o[n_first:], (block_rows // n_rest, 1))
        rows_ref[0, 1] = jnp.tile(o[:n_first], (block_rows // n_first, 1))


def _adaln(c_all, n_first, block_rows, w_ada, b_ada):
    n = c_all.shape[0]
    half = N_MOD // 2
    return pl.pallas_call(
        functools.partial(_ada_kernel, n_first, block_rows),
        grid=(N_MOD,),
        in_specs=[
            pl.BlockSpec((n, D_MODEL), lambda j: (0, 0)),
            pl.BlockSpec((D_MODEL, D_MODEL), lambda j: (0, j)),
            pl.BlockSpec((1, D_MODEL), lambda j: (0, j)),
        ],
        out_specs=[pl.BlockSpec((n, D_MODEL), lambda j: (0, j)),
                   pl.BlockSpec((1, 2, block_rows, D_MODEL), lambda j: (jnp.maximum(j - half, 0), 0, 0, 0))],
        out_shape=[jax.ShapeDtypeStruct((n, N_MOD * D_MODEL), F32),
                   jax.ShapeDtypeStruct((half, 2, block_rows, D_MODEL), F32)],
        compiler_params=pltpu.CompilerParams(
            dimension_semantics=("arbitrary",), vmem_limit_bytes=VMEM_LIMIT),
        name="adaln",
    )(c_all, w_ada, b_ada.reshape(1, -1))


def _rows_to_time_major(x_ref, slab, xt_scr):
    nb, steps, d = x_ref.shape
    pitch = slab.shape[1] // nb
    for b in range(nb):
        for j in range(d // LANES):
            slab[j, b * pitch:b * pitch + steps, :] = x_ref[b, :, j * LANES:(j + 1) * LANES]
    for l in range(steps):
        for j in range(d // LANES):
            xt_scr[l * nb:(l + 1) * nb, j * LANES:(j + 1) * LANES] = slab[j, pl.ds(l, nb, stride=pitch), :]


def _group_diagonal(w_t, k):
    tiled = jnp.tile(w_t[:, k * HALF_STATE:(k + 1) * HALF_STATE], (SSM_GROUPS // 2, 1))
    row_group = lax.shift_right_logical(lax.broadcasted_iota(jnp.int32, tiled.shape, 0),
                                        SSM_GROUP.bit_length() - 1)
    lane_group = lax.shift_right_logical(lax.broadcasted_iota(jnp.int32, tiled.shape, 1),
                                         SSM_STATE.bit_length() - 1)
    return jnp.where(row_group == lane_group, tiled, 0.0).astype(BF16)


N_MIXER_INPUTS = 22


def _mixer_kernel(nb, steps, batch_major, has_dest, nchunks, *refs):
    refs = refs[1:] if has_dest else refs
    x1_ref = refs[N_MIXER_INPUTS]

    @pl.when(pl.program_id(0) < nchunks)
    def _():
        _mixer_step(nb, steps, batch_major, *refs)

    @pl.when(pl.program_id(0) >= nchunks)
    def _():
        x1_ref[...] = jnp.zeros_like(x1_ref)


def _mixer_step(nb, steps, batch_major, *refs):
    (x_ref, sh_ref, sc_ref, gt_ref, h0r_ref, h0i_ref, cb0_ref,
     n1g_ref, win_ref, cw_ref, ar_ref, ai_ref, bbr_ref, bbi_ref, crt_ref, cit_ref,
     dsk_ref, wglu_ref, bglu_ref, gcv_ref, gsm_ref, wout_ref,
     x1_ref, cbo_ref, hro_ref, hio_ref,
     z_scr, cv_scr, s_scr, hr_scr, hi_scr, win_b, wglu_b, wout_b, wb_ref, wcr_ref, wci_ref,
     *tm_scr) = refs
    assert refs[N_MIXER_INPUTS] is x1_ref
    rows = nb * steps
    i = pl.program_id(0)

    @pl.when(i == 0)
    def _():
        hr_scr[...] = h0r_ref[...]
        hi_scr[...] = h0i_ref[...]
        cv_scr[0:2 * nb, :] = cb0_ref[...]
        win_b[...] = win_ref[...].astype(BF16)
        wglu_b[...] = wglu_ref[...].astype(BF16)
        wout_b[...] = wout_ref[...].astype(BF16)
        for k in range(2):
            wb_ref[k, :, 0:HALF_STATE] = _group_diagonal(bbr_ref[...], k)
            wb_ref[k, :, HALF_STATE:2 * HALF_STATE] = _group_diagonal(bbi_ref[...], k)
            wcr_ref[k] = _group_diagonal(crt_ref[...], k)
            wci_ref[k] = _group_diagonal(cit_ref[...], k)

    if batch_major:
        slab, x_tm = tm_scr
        _rows_to_time_major(x_ref, slab, x_tm)
    else:
        x_tm = x_ref
    h = _per_seq(_rms(x_tm[...], n1g_ref[...]), nb, 1.0 + sc_ref[...], sh_ref[...])
    z_scr[...] = _bdot(h, win_b[...])

    u_cols = 3 * CONV_DIM
    for k in range(2):
        s_scr[:, 2 * k * HALF_STATE:2 * (k + 1) * HALF_STATE] = _bdot(
            z_scr[:, u_cols + k * HALF_SSM:u_cols + (k + 1) * HALF_SSM], wb_ref[k])

    def scan_cols(cg):
        cols = slice(cg * SCAN_COLS, (cg + 1) * SCAN_COLS)
        half, off = divmod(cg * SCAN_COLS, HALF_STATE)
        re_cols = slice(2 * half * HALF_STATE + off, 2 * half * HALF_STATE + off + SCAN_COLS)
        im_cols = slice(re_cols.start + HALF_STATE, re_cols.stop + HALF_STATE)
        a_r = jnp.broadcast_to(ar_ref[:, cols], (SUBLANES, SCAN_COLS))
        a_i = jnp.broadcast_to(ai_ref[:, cols], (SUBLANES, SCAN_COLS))

        def advance(hr, hi, rows_l):
            nr = a_r * hr - a_i * hi + s_scr[rows_l, re_cols]
            ni = a_r * hi + a_i * hr + s_scr[rows_l, im_cols]
            s_scr[rows_l, re_cols] = nr
            s_scr[rows_l, im_cols] = ni
            return nr, ni

        for r0 in range(0, nb, SUBLANES):
            tile = slice(r0, r0 + SUBLANES)
            hr, hi = hr_scr[tile, cols], hi_scr[tile, cols]
            for l in range(steps):
                hr, hi = advance(hr, hi, slice(l * nb + r0, l * nb + r0 + SUBLANES))
            hr_scr[tile, cols] = hr
            hi_scr[tile, cols] = hi

    y_halves = []
    for k in range(2):
        for cg in range(k * HALF_STATE // SCAN_COLS, (k + 1) * HALF_STATE // SCAN_COLS):
            scan_cols(cg)
        s_re = s_scr[:, 2 * k * HALF_STATE:(2 * k + 1) * HALF_STATE].astype(BF16)
        s_im = s_scr[:, (2 * k + 1) * HALF_STATE:(2 * k + 2) * HALF_STATE].astype(BF16)
        y_halves.append(lax.dot_general(s_re, wcr_ref[k], _NT, preferred_element_type=F32)
                        - lax.dot_general(s_im, wci_ref[k], _NT, preferred_element_type=F32))

    hro_ref[...] = hr_scr[...]
    hio_ref[...] = hi_scr[...]

    y = jnp.concatenate(y_halves, axis=-1) + dsk_ref[...] * z_scr[:, u_cols:IN_DIM]
    g = _gelu_tanh(y)
    y_ssm = _rms(g * _sigmoid(_bdot(g, wglu_b[...]) + bglu_ref[...]), gsm_ref[...])

    cv_scr[2 * nb:2 * nb + rows, :] = z_scr[:, CONV_DIM:2 * CONV_DIM] * z_scr[:, 2 * CONV_DIM:3 * CONV_DIM]
    conv = (cv_scr[0:rows, :] * cw_ref[0:1, :]
            + cv_scr[nb:nb + rows, :] * cw_ref[1:2, :]
            + cv_scr[2 * nb:2 * nb + rows, :] * cw_ref[2:3, :])
    y_conv = _rms(z_scr[:, 0:CONV_DIM] * conv, gcv_ref[...])
    carry = cv_scr[rows:rows + 2 * nb, :]
    cv_scr[0:2 * nb, :] = carry
    cbo_ref[...] = carry

    mixed = _bdot(y_conv, wout_b[0:CONV_DIM, :]) + _bdot(y_ssm, wout_b[CONV_DIM:2 * CONV_DIM, :])
    x1_ref[...] = x_tm[...] + _per_seq(mixed, nb, gt_ref[...])


def _const_spec(shape):
    return pl.BlockSpec(shape, lambda i: (0,) * len(shape), pipeline_mode=pl.Buffered(1))


def _mixer(x, nb, steps_total, chunk, mod, mod_row0, h0r, h0i, cb0, weights, x1_rows=None, x1_dest=None):
    rows = nb * chunk
    nchunks = steps_total // chunk
    batch_major = x.ndim == 3
    n_rows = steps_total * nb
    if x1_dest is not None:
        x1_rows = x1_dest.shape[0]
    elif x1_rows is None:
        x1_rows = n_rows
    first_block = (x1_rows - n_rows) // rows if x1_dest is not None else 0
    n_steps = nchunks if x1_dest is not None else x1_rows // rows
    last = nchunks - 1
    if batch_major:
        assert nb == SUBLANES
        x_spec = pl.BlockSpec((nb, chunk, D_MODEL), lambda i: (0, jnp.minimum(i, last), 0))
        tm_scratch = [pltpu.VMEM((D_MODEL // LANES, nb * (chunk + STRIDE_PAD), LANES), F32),
                      pltpu.VMEM((rows, D_MODEL), F32)]
    else:
        x_spec = pl.BlockSpec((rows, D_MODEL), lambda i: (jnp.minimum(i, last), 0))
        tm_scratch = []
    assert mod_row0 % nb == 0

    def mod_spec(k):
        return pl.BlockSpec((nb, D_MODEL), lambda i: (mod_row0 // nb, k), pipeline_mode=pl.Buffered(1))

    in_specs = [
        x_spec, mod_spec(0), mod_spec(1), mod_spec(2),
        _const_spec((nb, N_STATE)), _const_spec((nb, N_STATE)), _const_spec((2 * nb, CONV_DIM)),
    ] + [_const_spec(w.shape) for w in weights]
    out_specs = [
        pl.BlockSpec((rows, D_MODEL), lambda i: (first_block + i, 0)),
        _const_spec((2 * nb, CONV_DIM)), _const_spec((nb, N_STATE)), _const_spec((nb, N_STATE)),
    ]
    out_shape = [
        jax.ShapeDtypeStruct((x1_rows, D_MODEL), F32),
        jax.ShapeDtypeStruct((2 * nb, CONV_DIM), F32),
        jax.ShapeDtypeStruct((nb, N_STATE), F32),
        jax.ShapeDtypeStruct((nb, N_STATE), F32),
    ]
    scratch = [
        pltpu.VMEM((rows, IN_DIM), F32),
        pltpu.VMEM((rows + 2 * nb, CONV_DIM), F32),
        pltpu.VMEM((rows, 2 * N_STATE), F32),
        pltpu.VMEM((nb, N_STATE), F32),
        pltpu.VMEM((nb, N_STATE), F32),
        pltpu.VMEM((D_MODEL, IN_DIM), BF16),
        pltpu.VMEM((SSM_DIM, SSM_DIM), BF16),
        pltpu.VMEM((D_MODEL, D_MODEL), BF16),
        pltpu.VMEM((2, HALF_SSM, 2 * HALF_STATE), BF16),
        pltpu.VMEM((2, HALF_SSM, HALF_STATE), BF16),
        pltpu.VMEM((2, HALF_SSM, HALF_STATE), BF16),
    ] + tm_scratch
    operands = [x, mod, mod, mod, h0r, h0i, cb0, *weights]
    if x1_dest is not None:
        in_specs = [pl.BlockSpec(memory_space=pl.ANY)] + in_specs
        operands = [x1_dest] + operands
    return pl.pallas_call(
        functools.partial(_mixer_kernel, nb, chunk, batch_major, x1_dest is not None, nchunks),
        grid=(n_steps,),
        in_specs=in_specs, out_specs=out_specs, out_shape=out_shape,
        scratch_shapes=scratch,
        input_output_aliases={0: 0} if x1_dest is not None else {},
        compiler_params=pltpu.CompilerParams(
            dimension_semantics=("arbitrary",), vmem_limit_bytes=VMEM_LIMIT),
        name="mixer",
    )(*operands)


TOK_BLOCK = 512
SEG_ALIGN = 16
ROW_TILE = 512
CHUNKS_PER_TILE = ROW_TILE // SEG_ALIGN
SCATTER_GROUPS = 8
GATHER_BUFFERS = 3
TABLE_CHUNK_ROWS = 64
N_SLOTS = -(-(2 * TOK_BLOCK + N_EXPERTS * (SEG_ALIGN - 1)) // 256) * 256
SPARE_ROW = N_SLOTS - SEG_ALIGN
assert SPARE_ROW >= 2 * TOK_BLOCK + N_EXPERTS * (SEG_ALIGN - 1)
SLOTS_ALWAYS = N_SLOTS - 12 * SEG_ALIGN
ROUTER_ROWS = 48
META_ROWS = SUBLANES

_NT = (((1,), (1,)), ((), ()))
_TN = (((0,), (0,)), ((), ()))


def _iota_f32(shape, dim):
    return lax.broadcasted_iota(jnp.int32, shape, dim).astype(F32)


def _split_bf16(v):
    hi = v.astype(BF16)
    return hi, (v - hi.astype(F32)).astype(BF16)


def _route_t(lg):
    row = _iota_f32(lg.shape, 0)
    neg = jnp.float32(-jnp.inf)
    none = float(ROUTER_ROWS)
    is_g = row < N_EXPERT_GROUPS
    gl = jnp.where(is_g, lg, neg)
    gmax = jnp.max(gl, axis=0, keepdims=True)
    gsum = jnp.sum(jnp.where(is_g, jnp.exp(gl - gmax), 0.0), axis=0, keepdims=True)
    g_prob = 1.0 / gsum
    g_idx = jnp.min(jnp.where(gl == gmax, row, none), axis=0, keepdims=True)
    lo = N_EXPERT_GROUPS + EXPERTS_PER_GROUP * g_idx
    el = jnp.where(row >= lo, jnp.where(row < lo + EXPERTS_PER_GROUP, lg, neg), neg)
    m1 = jnp.max(el, axis=0, keepdims=True)
    i1 = jnp.min(jnp.where(el == m1, row, none), axis=0, keepdims=True)
    el2 = jnp.where(row == i1, neg, el)
    m2 = jnp.max(el2, axis=0, keepdims=True)
    i2 = jnp.min(jnp.where(el2 == m2, row, none), axis=0, keepdims=True)
    e21 = jnp.exp(m2 - m1)
    return i1, i2, g_prob / (1.0 + e21), g_prob * e21 / (1.0 + e21)


def _dispatch_kernel(x1_ref, sc_ref, sh_ref, n2g_ref, wr_ref, br_ref, xs_ref, meta_ref, cnt_ref):
    h = _rms(x1_ref[...], n2g_ref[...]) * (1.0 + sc_ref[0, 0]) + sh_ref[0, 0]
    t = h.shape[0]
    hb, hl = _split_bf16(h)
    wh, wl = _split_bf16(wr_ref[...])
    lg = (lax.dot_general(wh, hb, _NT, preferred_element_type=F32)
          + lax.dot_general(wh, hl, _NT, preferred_element_type=F32)
          + lax.dot_general(wl, hb, _NT, preferred_element_type=F32)) + br_ref[...]
    i1, i2, w1, w2 = _route_t(lg)

    row = _iota_f32((ROUTER_ROWS, t), 0)
    hit1 = row == i1
    hit2 = row == i2
    onehot = jnp.where(hit1, 1.0, 0.0) + jnp.where(hit2, 1.0, 0.0)
    before = jnp.where(_iota_f32((t, t), 0) < _iota_f32((t, t), 1), 1.0, 0.0)
    rank = _bdot(onehot, before)
    cnt = jnp.sum(onehot, axis=1, keepdims=True)
    seg = jnp.floor((cnt + (SEG_ALIGN - 1)) * (1.0 / SEG_ALIGN)) * SEG_ALIGN
    earlier = jnp.where(_iota_f32((ROUTER_ROWS, ROUTER_ROWS), 1) < _iota_f32((ROUTER_ROWS, ROUTER_ROWS), 0),
                        1.0, 0.0)
    base = rank + _bdot(earlier, jnp.broadcast_to(seg, (ROUTER_ROWS, t)))
    slot1 = jnp.sum(jnp.where(hit1, base, 0.0), axis=0, keepdims=True)
    slot2 = jnp.sum(jnp.where(hit2, base, 0.0), axis=0, keepdims=True)
    def sorted_rows(first, n):
        j = _iota_f32((n, t), 0) + float(first)
        place = jnp.where(j == slot1, 1.0, 0.0) + jnp.where(j == slot2, 1.0, 0.0)
        return jnp.dot(place.astype(BF16), hb, preferred_element_type=F32).astype(BF16)

    xs_ref[0:SLOTS_ALWAYS, :] = sorted_rows(0, SLOTS_ALWAYS)
    used = jnp.sum(seg)

    @pl.when(used > SLOTS_ALWAYS)
    def _():
        xs_ref[SLOTS_ALWAYS:N_SLOTS, :] = sorted_rows(SLOTS_ALWAYS, N_SLOTS - SLOTS_ALWAYS)

    @pl.when(used <= SLOTS_ALWAYS)
    def _():
        xs_ref[SLOTS_ALWAYS:N_SLOTS, :] = jnp.zeros((N_SLOTS - SLOTS_ALWAYS, D_MODEL), BF16)

    r = _iota_f32((META_ROWS, t), 0)
    meta_ref[0] = jnp.where(r == 0.0, slot1, jnp.where(r == 1.0, slot2,
                            jnp.where(r == 2.0, w1, jnp.where(r == 3.0, w2, 0.0))))
    cnt_ref[0] = jnp.broadcast_to(cnt, (ROUTER_ROWS, LANES))


def _combine_kernel(nbp, x1_ref, gt_ref, meta_ref, ys_ref, fg_ref, yp_ref, ysm_ref, slab):
    i = pl.program_id(0)
    x1 = x1_ref[...]
    meta = meta_ref[0]
    slot1, slot2, w1, w2 = (meta[k:k + 1, :] for k in range(4))
    j = _iota_f32((N_SLOTS, x1.shape[0]), 0)
    pick = jnp.where(j == slot1, w1, 0.0) + jnp.where(j == slot2, w2, 0.0)
    moe = lax.dot_general(pick.astype(BF16), ys_ref[...], _TN, preferred_element_type=F32)
    y = _rms(x1 + gt_ref[0, 0] * moe, fg_ref[...])

    @pl.when(i < nbp)
    def _():
        nb, steps, d = yp_ref.shape
        for j in range(d // LANES):
            slab[j] = y[:, j * LANES:(j + 1) * LANES]
        for b in range(nb):
            for j in range(d // LANES):
                yp_ref[b, :, j * LANES:(j + 1) * LANES] = slab[j, pl.ds(b, steps, stride=nb), :]

    @pl.when(i >= nbp)
    def _():
        ysm_ref[...] = y


def _chunk_table_kernel(start_ref, m_ref, src_ref, out_ref):
    start = start_ref[...]
    end = start + m_ref[...]
    src = src_ref[...]

    def body(b, _):
        rows = pl.ds(pl.multiple_of(b * TABLE_CHUNK_ROWS, TABLE_CHUNK_ROWS), TABLE_CHUNK_ROWS)
        first = lax.convert_element_type(b * TABLE_CHUNK_ROWS, F32)
        i = _iota_f32((TABLE_CHUNK_ROWS, start.shape[1]), 0) + first
        val = jnp.where(i >= start, jnp.where(i < end, src + SEG_ALIGN * (i - start), 0.0), 0.0)
        out_ref[rows, :] = jnp.sum(val, axis=1, keepdims=True).astype(jnp.int32)
        return 0

    lax.fori_loop(0, out_ref.shape[0] // TABLE_CHUNK_ROWS, body, 0)


def _expert_kernel(te_ref, tnv_ref, tc0_ref, ctab_ref, nt_ref, xs_hbm, wg_ref, wu_ref, wd_ref,
                   ys_hbm, xbuf, ybuf, wgub, wdb, in_sem, out_sem):
    t = pl.program_id(0)
    nt = nt_ref[0]
    slot = t % 2
    xslot = t % GATHER_BUFFERS

    def chunk_rows(j):
        return pl.ds(pl.multiple_of(j * SEG_ALIGN, SEG_ALIGN), SEG_ALIGN)

    def hbm_rows(ref, r):
        return ref.at[pl.ds(pl.multiple_of(r, SEG_ALIGN), SEG_ALIGN), :]

    def in_copy(r, j, s):
        return pltpu.make_async_copy(hbm_rows(xs_hbm, r), xbuf.at[s, chunk_rows(j), :], in_sem.at[s])

    def out_copy(r, j, s):
        return pltpu.make_async_copy(ybuf.at[s, chunk_rows(j), :], hbm_rows(ys_hbm, r), out_sem.at[s])

    def start_gather(tile, s):
        c0 = tc0_ref[tile]
        nv = tnv_ref[tile]
        for j in range(CHUNKS_PER_TILE):
            in_copy(jnp.where(j < nv, ctab_ref[c0 + j], SPARE_ROW), j, s).start()

    def wait_gather(s):
        pltpu.make_async_copy(xs_hbm.at[pl.ds(0, ROW_TILE), :], xbuf.at[s], in_sem.at[s]).wait()

    def wait_scatter(tile, s):
        nv = tnv_ref[tile]

        @pl.when(nv == CHUNKS_PER_TILE)
        def _():
            pltpu.make_async_copy(ybuf.at[s], ys_hbm.at[pl.ds(0, ROW_TILE), :], out_sem.at[s]).wait()

        @pl.when(nv < CHUNKS_PER_TILE)
        def _():
            def body(j, _):
                out_copy(0, j, s).wait()
                return 0

            lax.fori_loop(0, nv, body, 0)

    def run_tile(full, rows=ROW_TILE):
        ahead = t + GATHER_BUFFERS - 1
        start_gather(ahead, ahead % GATHER_BUFFERS)
        wait_gather(xslot)
        ab = jnp.dot(xbuf[xslot, 0:rows, :], wgub[...], preferred_element_type=F32)
        a, b = ab[:, :D_EXPERT], ab[:, D_EXPERT:]
        o = jnp.dot((a * _sigmoid(a) * b).astype(BF16), wdb[...], preferred_element_type=F32)
        c0 = tc0_ref[t]
        if full:
            group = ROW_TILE // SCATTER_GROUPS
            for r in range(0, ROW_TILE, group):
                ybuf[slot, r:r + group, :] = o[r:r + group].astype(BF16)
                for j in range(r // SEG_ALIGN, (r + group) // SEG_ALIGN):
                    out_copy(ctab_ref[c0 + j], j, slot).start()
        else:
            ybuf[slot, 0:rows, :] = o.astype(BF16)

            def body(j, _):
                out_copy(ctab_ref[c0 + j], j, slot).start()
                return 0

            lax.fori_loop(0, tnv_ref[t], body, 0)

    @pl.when(t == 0)
    def _():
        for k in range(GATHER_BUFFERS - 1):
            start_gather(k, k)

    @pl.when(jnp.logical_or(t == 0, te_ref[t] != te_ref[jnp.maximum(t - 1, 0)]))
    def _():
        wgub[:, :D_EXPERT] = wg_ref[0].astype(BF16)
        wgub[:, D_EXPERT:] = wu_ref[0].astype(BF16)
        wdb[...] = wd_ref[0].astype(BF16)

    @pl.when(t < nt)
    def _():
        @pl.when(t >= 2)
        def _():
            wait_scatter(t - 2, slot)

        nv = tnv_ref[t]
        half = CHUNKS_PER_TILE // 2
        pl.when(nv == CHUNKS_PER_TILE)(lambda: run_tile(True))
        pl.when(jnp.logical_and(nv > half, nv < CHUNKS_PER_TILE))(lambda: run_tile(False))
        pl.when(nv <= half)(lambda: run_tile(False, ROW_TILE // 2))

        @pl.when(t == nt - 1)
        def _():
            for k in range(1, GATHER_BUFFERS):
                wait_gather((t + k) % GATHER_BUFFERS)
            wait_scatter(t, slot)

            @pl.when(t >= 1)
            def _():
                wait_scatter(t - 1, 1 - slot)


def _tile_plan(cnt):
    nblk = cnt.shape[0]
    max_chunks = nblk * (2 * TOK_BLOCK // SEG_ALIGN + N_EXPERTS)
    max_tiles = max_chunks // CHUNKS_PER_TILE + N_EXPERTS
    table_len = -(-(max_chunks + CHUNKS_PER_TILE) // LANES) * LANES
    m = (cnt + SEG_ALIGN - 1) // SEG_ALIGN
    seg_start = jnp.cumsum(m, axis=1) - m
    src0 = jnp.arange(nblk, dtype=jnp.int32)[:, None] * N_SLOTS + SEG_ALIGN * seg_start
    seg_m = m.T.reshape(1, -1)
    seg_first = jnp.cumsum(seg_m, axis=1) - seg_m
    assert table_len % TABLE_CHUNK_ROWS == 0

    def row(v):
        return jnp.pad(v.astype(F32), ((0, 0), (0, -v.shape[1] % LANES)))

    chunk_src = pl.pallas_call(
        _chunk_table_kernel,
        out_shape=jax.ShapeDtypeStruct((table_len, 1), jnp.int32),
        name="moe_chunk_table",
    )(row(seg_first), row(seg_m), row(src0.T.reshape(1, -1))).reshape(-1)
    ce = jnp.sum(m, axis=0)
    c_start = jnp.cumsum(ce) - ce
    tiles = (ce + CHUNKS_PER_TILE - 1) // CHUNKS_PER_TILE
    t_cum = jnp.cumsum(tiles)
    nt = t_cum[-1]
    k = jnp.arange(max_tiles + GATHER_BUFFERS - 1, dtype=jnp.int32)
    kk = jnp.minimum(k, nt - 1)
    ek = jnp.minimum(jnp.sum((t_cum[None, :] <= kk[:, None]).astype(jnp.int32), axis=1), N_EXPERTS - 1)
    is_e = ek[:, None] == jnp.arange(N_EXPERTS, dtype=jnp.int32)[None, :]
    pick = lambda v: jnp.sum(jnp.where(is_e, v[None, :], 0), axis=1)
    local = kk - pick(t_cum - tiles)
    tile_nv = jnp.where(k < nt, jnp.clip(pick(ce) - CHUNKS_PER_TILE * local, 0, CHUNKS_PER_TILE), 0)
    tile_c0 = jnp.where(k < nt, pick(c_start) + CHUNKS_PER_TILE * local, 0)
    return tuple(v.astype(jnp.int32) for v in (ek, tile_nv, tile_c0, chunk_src, nt.reshape(1)))


def _moe(x1, bp, mod_rows, n2g, wr, br, wg, wu, wd, fg):
    nblk = x1.shape[0] // TOK_BLOCK
    nbp = nblk - 1
    blk = (TOK_BLOCK, D_MODEL)
    x1_spec = pl.BlockSpec(blk, lambda i: (i, 0))

    def mod_spec(k):
        return pl.BlockSpec((1, 1) + blk, lambda i: (k, jnp.minimum(i // nbp, 1), 0, 0))

    sh_spec, sc_spec, gt_spec = mod_spec(0), mod_spec(1), mod_spec(2)
    meta_spec = pl.BlockSpec((1, META_ROWS, TOK_BLOCK), lambda i: (i, 0, 0))
    slots_spec = pl.BlockSpec((N_SLOTS, D_MODEL), lambda i: (i, 0))
    params = pltpu.CompilerParams(dimension_semantics=("arbitrary",), vmem_limit_bytes=VMEM_LIMIT)

    def cs(shape):
        return pl.BlockSpec(shape, lambda i: (0,) * len(shape))

    xs, meta, cnt = pl.pallas_call(
        _dispatch_kernel,
        grid=(nblk,),
        in_specs=[x1_spec, sc_spec, sh_spec, cs((1, D_MODEL)),
                  cs((ROUTER_ROWS, D_MODEL)), cs((ROUTER_ROWS, TOK_BLOCK))],
        out_specs=[slots_spec, meta_spec,
                   pl.BlockSpec((1, ROUTER_ROWS, LANES), lambda i: (i, 0, 0))],
        out_shape=[jax.ShapeDtypeStruct((nblk * N_SLOTS, D_MODEL), BF16),
                   jax.ShapeDtypeStruct((nblk, META_ROWS, TOK_BLOCK), F32),
                   jax.ShapeDtypeStruct((nblk, ROUTER_ROWS, LANES), F32)],
        compiler_params=params,
        name="moe_dispatch",
    )(x1, mod_rows, mod_rows, n2g, wr, br)

    plan = _tile_plan(cnt[:, N_EXPERT_GROUPS:N_EXPERT_GROUPS + N_EXPERTS, 0].astype(jnp.int32))
    max_tiles = plan[0].shape[0] - (GATHER_BUFFERS - 1)

    def w_spec(shape):
        return pl.BlockSpec((1,) + shape, lambda t, te, *_: (te[t], 0, 0))

    ys = pl.pallas_call(
        _expert_kernel,
        grid_spec=pltpu.PrefetchScalarGridSpec(
            num_scalar_prefetch=len(plan),
            grid=(max_tiles,),
            in_specs=[pl.BlockSpec(memory_space=pl.ANY),
                      w_spec((D_MODEL, D_EXPERT)), w_spec((D_MODEL, D_EXPERT)), w_spec((D_EXPERT, D_MODEL))],
            out_specs=pl.BlockSpec(memory_space=pl.ANY),
            scratch_shapes=[pltpu.VMEM((GATHER_BUFFERS, ROW_TILE, D_MODEL), BF16),
                            pltpu.VMEM((2, ROW_TILE, D_MODEL), BF16),
                            pltpu.VMEM((D_MODEL, 2 * D_EXPERT), BF16),
                            pltpu.VMEM((D_EXPERT, D_MODEL), BF16),
                            pltpu.SemaphoreType.DMA((GATHER_BUFFERS,)), pltpu.SemaphoreType.DMA((2,))]),
        out_shape=jax.ShapeDtypeStruct(xs.shape, BF16),
        input_output_aliases={len(plan): 0},
        compiler_params=params,
        name="moe_experts",
    )(*plan, xs, wg, wu, wd)

    return pl.pallas_call(
        functools.partial(_combine_kernel, nbp),
        grid=(nblk,),
        in_specs=[x1_spec, gt_spec, meta_spec, slots_spec, cs((1, D_MODEL))],
        out_specs=[pl.BlockSpec((bp, TOK_BLOCK // bp, D_MODEL), lambda i: (0, jnp.minimum(i, nbp - 1), 0)),
                   pl.BlockSpec(blk, lambda i: (0, 0))],
        out_shape=[jax.ShapeDtypeStruct((bp, nbp * TOK_BLOCK // bp, D_MODEL), F32),
                   jax.ShapeDtypeStruct(blk, F32)],
        scratch_shapes=[pltpu.VMEM((D_MODEL // LANES, TOK_BLOCK, LANES), F32)],
        compiler_params=params,
        name="moe_combine",
    )(x1, mod_rows, meta, ys, fg)


def _to_time_major(x):
    b, s, d = x.shape
    return x.transpose(1, 0, 2).reshape(s * b, d)


def _from_time_major(x, b, s):
    return x.reshape(s, b, -1).transpose(1, 0, 2)


def kernel(x_prompt, x_sample, state_conv, state_ssm_re, state_ssm_im, c_prompt, c_sample, w_ada, b_ada, norm1_g, norm2_g, w_in, conv_w, lambda_re, lambda_im, log_dt, ssm_b_re, ssm_b_im, ssm_c_re, ssm_c_im, ssm_d, w_glu, b_glu, out_norm_conv_g, out_norm_ssm_g, w_out, w_router_group, b_router_group, w_router_expert, b_router_expert, w_expert_gate, w_expert_up, w_expert_down, final_norm_g):
    assert w_ada.shape[0] == 1, "single-layer step"
    bp, sp, _ = x_prompt.shape
    bs, ss, _ = x_sample.shape

    ar, ai, bbr, bbi = _discretise(lambda_re[0], lambda_im[0], log_dt[0], ssm_b_re[0], ssm_b_im[0])
    crt = ssm_c_re[0].transpose(1, 0, 2).reshape(SSM_GROUP, N_STATE)
    cit = ssm_c_im[0].transpose(1, 0, 2).reshape(SSM_GROUP, N_STATE)

    assert bs % bp == 0
    mod, mod_rows = _adaln(jnp.concatenate([c_sample, c_prompt], axis=0), bs, TOK_BLOCK, w_ada[0], b_ada[0])

    mixer_weights = (
        norm1_g[0].reshape(1, -1), w_in[0], conv_w[0], ar, ai, bbr, bbi, crt, cit,
        ssm_d[0].reshape(1, -1), w_glu[0], b_glu[0].reshape(1, -1),
        out_norm_conv_g[0].reshape(1, -1), out_norm_ssm_g[0].reshape(1, -1), w_out[0],
    )

    pad_rows = ROUTER_ROWS - N_EXPERT_GROUPS - N_EXPERTS
    w_router = jnp.concatenate(
        [w_router_group[0].T, w_router_expert[0].transpose(0, 2, 1).reshape(N_EXPERTS, D_MODEL),
         jnp.zeros((pad_rows, D_MODEL), F32)], axis=0)
    b_router = jnp.concatenate(
        [b_router_group[0], b_router_expert[0].reshape(-1), jnp.zeros((pad_rows,), F32)])
    b_router = jnp.broadcast_to(b_router[:, None], (ROUTER_ROWS, TOK_BLOCK))
    moe_weights = (norm2_g[0].reshape(1, -1), w_router, b_router,
                   w_expert_gate[0], w_expert_up[0], w_expert_down[0], final_norm_g.reshape(1, -1))

    assert bs * ss == TOK_BLOCK and (bp * sp) % TOK_BLOCK == 0 and TOK_BLOCK % bp == 0

    def mix(x, mod_row0, h0r, h0i, cb0, chunk, in_kernel_transpose, **x1_where):
        nb, s, _ = x.shape
        x1, cbo, hro, hio = _mixer(
            x if in_kernel_transpose else _to_time_major(x), nb, s, chunk,
            mod, mod_row0, h0r, h0i, cb0, mixer_weights, **x1_where)
        new_conv = cbo.reshape(2, nb, CONV_DIM).transpose(1, 0, 2)[None]
        new_re = hro.reshape(1, nb, SSM_GROUPS, SSM_STATE)
        new_im = hio.reshape(1, nb, SSM_GROUPS, SSM_STATE)
        return x1, new_conv, new_re, new_im

    zero_state = jnp.zeros((bp, N_STATE), F32)
    zero_conv = jnp.zeros((2 * bp, CONV_DIM), F32)
    x1, conv_p, re_p, im_p = mix(x_prompt, bs, zero_state, zero_state, zero_conv, TOK_BLOCK // bp, True,
                                 x1_rows=bp * sp + bs * ss)
    x1, conv_s, re_s, im_s = mix(
        x_sample, 0,
        state_ssm_re[0].reshape(bs, N_STATE), state_ssm_im[0].reshape(bs, N_STATE),
        state_conv[0].transpose(1, 0, 2).reshape(2 * bs, CONV_DIM), ss, False, x1_dest=x1)

    y_p, y_s = _moe(x1, bp, mod_rows, *moe_weights)
    return (y_p, _from_time_major(y_s, bs, ss), conv_p, re_p, im_p, conv_s, re_s, im_s)
```

```python
import functools

import jax
import jax.numpy as jnp
from jax import lax
from jax.experimental import pallas as pl
from jax.experimental.pallas import tpu as pltpu

D_MODEL = 1024
CONV_DIM = 512
SSM_DIM = 512
SSM_GROUP = 16
SSM_GROUPS = 32
SSM_STATE = 64
N_STATE = SSM_GROUPS * SSM_STATE
HALF_SSM = SSM_DIM // 2
HALF_STATE = N_STATE // 2
IN_DIM = 2048
N_EXPERT_GROUPS = 4
EXPERTS_PER_GROUP = 8
N_EXPERTS = 32
D_EXPERT = 256
N_MOD = 6
EPS = 1e-6

SUBLANES = 8
LANES = 128
SCAN_COLS = 512
STRIDE_PAD = 4
VMEM_LIMIT = 60 * 1024 * 1024

F32 = jnp.float32
BF16 = jnp.bfloat16


def _sigmoid(x):
    return 1.0 / (1.0 + jnp.exp(-x))


def _gelu_tanh(x):
    return 0.5 * x * (1.0 + jnp.tanh(0.7978845608028654 * (x + 0.044715 * (x * x * x))))


def _rms(x, g):
    return x * lax.rsqrt(jnp.mean(x * x, axis=-1, keepdims=True) + EPS) * g


def _bdot(a, b):
    return jnp.dot(a.astype(BF16), b.astype(BF16), preferred_element_type=F32)


def _per_seq(v, nb, scale, shift=None):
    rows, d = v.shape
    v3 = v.reshape(rows // nb, nb, d) * scale[None]
    if shift is not None:
        v3 = v3 + shift[None]
    return v3.reshape(rows, d)


def _disc_kernel(lr_ref, li_ref, ldt_ref, brt_ref, bit_ref, ar_ref, ai_ref, bbr_ref, bbi_ref):
    lr = lr_ref[...]
    li = li_ref[...]
    dt = jnp.exp(ldt_ref[...])
    mag = jnp.exp(lr * dt)
    ar = mag * jnp.cos(li * dt)
    ai = mag * jnp.sin(li * dt)
    den = lr * lr + li * li
    fr = ((ar - 1.0) * lr + ai * li) / den
    fi = (ai * lr - (ar - 1.0) * li) / den
    ar_ref[...] = ar
    ai_ref[...] = ai
    brt = brt_ref[...]
    bit = bit_ref[...]
    bbr_ref[...] = fr * brt - fi * bit
    bbi_ref[...] = fr * bit + fi * brt


def _discretise(lambda_re, lambda_im, log_dt, b_re, b_im):
    lr = lambda_re.reshape(1, N_STATE)
    li = lambda_im.reshape(1, N_STATE)
    ldt = jnp.repeat(log_dt, SSM_STATE).reshape(1, N_STATE)
    brt = b_re.transpose(2, 0, 1).reshape(SSM_GROUP, N_STATE)
    bit = b_im.transpose(2, 0, 1).reshape(SSM_GROUP, N_STATE)
    row = jax.ShapeDtypeStruct((1, N_STATE), F32)
    mat = jax.ShapeDtypeStruct((SSM_GROUP, N_STATE), F32)
    return pl.pallas_call(_disc_kernel, out_shape=(row, row, mat, mat), name="s5_discretise")(
        lr, li, ldt, brt, bit)


def _ada_kernel(n_first, block_rows, c_ref, w_ref, b_ref, o_ref, rows_ref):
    c = c_ref[...]
    o = _bdot(c * _sigmoid(c), w_ref[...]) + b_ref[...]
    o_ref[...] = o

    @pl.when(pl.program_id(0) >= N_MOD // 2)
    def _():
        n_rest = o.shape[0] - n_first
        rows_ref[0, 0] = jnp.tile(o[n_first:], (block_rows // n_rest, 1))
        rows_ref[0, 1] = jnp.tile(o[:n_first], (block_rows // n_first, 1))


def _adaln(c_all, n_first, block_rows, w_ada, b_ada):
    n = c_all.shape[0]
    half = N_MOD // 2
    return pl.pallas_call(
        functools.partial(_ada_kernel, n_first, block_rows),
        grid=(N_MOD,),
        in_specs=[
            pl.BlockSpec((n, D_MODEL), lambda j: (0, 0)),
            pl.BlockSpec((D_MODEL, D_MODEL), lambda j: (0, j)),
            pl.BlockSpec((1, D_MODEL), lambda j: (0, j)),
        ],
        out_specs=[pl.BlockSpec((n, D_MODEL), lambda j: (0, j)),
                   pl.BlockSpec((1, 2, block_rows, D_MODEL), lambda j: (jnp.maximum(j - half, 0), 0, 0, 0))],
        out_shape=[jax.ShapeDtypeStruct((n, N_MOD * D_MODEL), F32),
                   jax.ShapeDtypeStruct((half, 2, block_rows, D_MODEL), F32)],
        compiler_params=pltpu.CompilerParams(
            dimension_semantics=("arbitrary",), vmem_limit_bytes=VMEM_LIMIT),
        name="adaln",
    )(c_all, w_ada, b_ada.reshape(1, -1))


def _rows_to_time_major(x_ref, slab, xt_scr):
    nb, steps, d = x_ref.shape
    pitch = slab.shape[1] // nb
    for b in range(nb):
        for j in range(d // LANES):
            slab[j, b * pitch:b * pitch + steps, :] = x_ref[b, :, j * LANES:(j + 1) * LANES]
    for l in range(steps):
        for j in range(d // LANES):
            xt_scr[l * nb:(l + 1) * nb, j * LANES:(j + 1) * LANES] = slab[j, pl.ds(l, nb, stride=pitch), :]


def _group_diagonal(w_t, k):
    tiled = jnp.tile(w_t[:, k * HALF_STATE:(k + 1) * HALF_STATE], (SSM_GROUPS // 2, 1))
    row_group = lax.shift_right_logical(lax.broadcasted_iota(jnp.int32, tiled.shape, 0),
                                        SSM_GROUP.bit_length() - 1)
    lane_group = lax.shift_right_logical(lax.broadcasted_iota(jnp.int32, tiled.shape, 1),
                                         SSM_STATE.bit_length() - 1)
    return jnp.where(row_group == lane_group, tiled, 0.0).astype(BF16)


N_MIXER_INPUTS = 22


def _mixer_kernel(nb, steps, batch_major, has_dest, nchunks, *refs):
    refs = refs[1:] if has_dest else refs
    x1_ref = refs[N_MIXER_INPUTS]

    @pl.when(pl.program_id(0) < nchunks)
    def _():
        _mixer_step(nb, steps, batch_major, *refs)

    @pl.when(pl.program_id(0) >= nchunks)
    def _():
        x1_ref[...] = jnp.zeros_like(x1_ref)


def _mixer_step(nb, steps, batch_major, *refs):
    (x_ref, sh_ref, sc_ref, gt_ref, h0r_ref, h0i_ref, cb0_ref,
     n1g_ref, win_ref, cw_ref, ar_ref, ai_ref, bbr_ref, bbi_ref, crt_ref, cit_ref,
     dsk_ref, wglu_ref, bglu_ref, gcv_ref, gsm_ref, wout_ref,
     x1_ref, cbo_ref, hro_ref, hio_ref,
     z_scr, cv_scr, s_scr, hr_scr, hi_scr, win_b, wglu_b, wout_b, wb_ref, wcr_ref, wci_ref,
     *tm_scr) = refs
    assert refs[N_MIXER_INPUTS] is x1_ref
    rows = nb * steps
    i = pl.program_id(0)

    @pl.when(i == 0)
    def _():
        hr_scr[...] = h0r_ref[...]
        hi_scr[...] = h0i_ref[...]
        cv_scr[0:2 * nb, :] = cb0_ref[...]
        win_b[...] = win_ref[...].astype(BF16)
        wglu_b[...] = wglu_ref[...].astype(BF16)
        wout_b[...] = wout_ref[...].astype(BF16)
        for k in range(2):
            wb_ref[k, :, 0:HALF_STATE] = _group_diagonal(bbr_ref[...], k)
            wb_ref[k, :, HALF_STATE:2 * HALF_STATE] = _group_diagonal(bbi_ref[...], k)
            wcr_ref[k] = _group_diagonal(crt_ref[...], k)
            wci_ref[k] = _group_diagonal(cit_ref[...], k)

    if batch_major:
        slab, x_tm = tm_scr
        _rows_to_time_major(x_ref, slab, x_tm)
    else:
        x_tm = x_ref
    h = _per_seq(_rms(x_tm[...], n1g_ref[...]), nb, 1.0 + sc_ref[...], sh_ref[...])
    z_scr[...] = _bdot(h, win_b[...])

    u_cols = 3 * CONV_DIM
    for k in range(2):
        s_scr[:, 2 * k * HALF_STATE:2 * (k + 1) * HALF_STATE] = _bdot(
            z_scr[:, u_cols + k * HALF_SSM:u_cols + (k + 1) * HALF_SSM], wb_ref[k])

    def scan_cols(cg):
        cols = slice(cg * SCAN_COLS, (cg + 1) * SCAN_COLS)
        half, off = divmod(cg * SCAN_COLS, HALF_STATE)
        re_cols = slice(2 * half * HALF_STATE + off, 2 * half * HALF_STATE + off + SCAN_COLS)
        im_cols = slice(re_cols.start + HALF_STATE, re_cols.stop + HALF_STATE)
        a_r = jnp.broadcast_to(ar_ref[:, cols], (SUBLANES, SCAN_COLS))
        a_i = jnp.broadcast_to(ai_ref[:, cols], (SUBLANES, SCAN_COLS))

        def advance(hr, hi, rows_l):
            nr = a_r * hr - a_i * hi + s_scr[rows_l, re_cols]
            ni = a_r * hi + a_i * hr + s_scr[rows_l, im_cols]
            s_scr[rows_l, re_cols] = nr
            s_scr[rows_l, im_cols] = ni
            return nr, ni

        for r0 in range(0, nb, SUBLANES):
            tile = slice(r0, r0 + SUBLANES)
            hr, hi = hr_scr[tile, cols], hi_scr[tile, cols]
            for l in range(steps):
                hr, hi = advance(hr, hi, slice(l * nb + r0, l * nb + r0 + SUBLANES))
            hr_scr[tile, cols] = hr
            hi_scr[tile, cols] = hi

    y_halves = []
    for k in range(2):
        for cg in range(k * HALF_STATE // SCAN_COLS, (k + 1) * HALF_STATE // SCAN_COLS):
            scan_cols(cg)
        s_re = s_scr[:, 2 * k * HALF_STATE:(2 * k + 1) * HALF_STATE].astype(BF16)
        s_im = s_scr[:, (2 * k + 1) * HALF_STATE:(2 * k + 2) * HALF_STATE].astype(BF16)
        y_halves.append(lax.dot_general(s_re, wcr_ref[k], _NT, preferred_element_type=F32)
                        - lax.dot_general(s_im, wci_ref[k], _NT, preferred_element_type=F32))

    hro_ref[...] = hr_scr[...]
    hio_ref[...] = hi_scr[...]

    y = jnp.concatenate(y_halves, axis=-1) + dsk_ref[...] * z_scr[:, u_cols:IN_DIM]
    g = _gelu_tanh(y)
    y_ssm = _rms(g * _sigmoid(_bdot(g, wglu_b[...]) + bglu_ref[...]), gsm_ref[...])

    cv_scr[2 * nb:2 * nb + rows, :] = z_scr[:, CONV_DIM:2 * CONV_DIM] * z_scr[:, 2 * CONV_DIM:3 * CONV_DIM]
    conv = (cv_scr[0:rows, :] * cw_ref[0:1, :]
            + cv_scr[nb:nb + rows, :] * cw_ref[1:2, :]
            + cv_scr[2 * nb:2 * nb + rows, :] * cw_ref[2:3, :])
    y_conv = _rms(z_scr[:, 0:CONV_DIM] * conv, gcv_ref[...])
    carry = cv_scr[rows:rows + 2 * nb, :]
    cv_scr[0:2 * nb, :] = carry
    cbo_ref[...] = carry

    mixed = _bdot(y_conv, wout_b[0:CONV_DIM, :]) + _bdot(y_ssm, wout_b[CONV_DIM:2 * CONV_DIM, :])
    x1_ref[...] = x_tm[...] + _per_seq(mixed, nb, gt_ref[...])


def _const_spec(shape):
    return pl.BlockSpec(shape, lambda i: (0,) * len(shape), pipeline_mode=pl.Buffered(1))


def _mixer(x, nb, steps_total, chunk, mod, mod_row0, h0r, h0i, cb0, weights, x1_rows=None, x1_dest=None):
    rows = nb * chunk
    nchunks = steps_total // chunk
    batch_major = x.ndim == 3
    n_rows = steps_total * nb
    if x1_dest is not None:
        x1_rows = x1_dest.shape[0]
    elif x1_rows is None:
        x1_rows = n_rows
    first_block = (x1_rows - n_rows) // rows if x1_dest is not None else 0
    n_steps = nchunks if x1_dest is not None else x1_rows // rows
    last = nchunks - 1
    if batch_major:
        assert nb == SUBLANES
        x_spec = pl.BlockSpec((nb, chunk, D_MODEL), lambda i: (0, jnp.minimum(i, last), 0))
        tm_scratch = [pltpu.VMEM((D_MODEL // LANES, nb * (chunk + STRIDE_PAD), LANES), F32),
                      pltpu.VMEM((rows, D_MODEL), F32)]
    else:
        x_spec = pl.BlockSpec((rows, D_MODEL), lambda i: (jnp.minimum(i, last), 0))
        tm_scratch = []
    assert mod_row0 % nb == 0

    def mod_spec(k):
        return pl.BlockSpec((nb, D_MODEL), lambda i: (mod_row0 // nb, k), pipeline_mode=pl.Buffered(1))

    in_specs = [
        x_spec, mod_spec(0), mod_spec(1), mod_spec(2),
        _const_spec((nb, N_STATE)), _const_spec((nb, N_STATE)), _const_spec((2 * nb, CONV_DIM)),
    ] + [_const_spec(w.shape) for w in weights]
    out_specs = [
        pl.BlockSpec((rows, D_MODEL), lambda i: (first_block + i, 0)),
        _const_spec((2 * nb, CONV_DIM)), _const_spec((nb, N_STATE)), _const_spec((nb, N_STATE)),
    ]
    out_shape = [
        jax.ShapeDtypeStruct((x1_rows, D_MODEL), F32),
        jax.ShapeDtypeStruct((2 * nb, CONV_DIM), F32),
        jax.ShapeDtypeStruct((nb, N_STATE), F32),
        jax.ShapeDtypeStruct((nb, N_STATE), F32),
    ]
    scratch = [
        pltpu.VMEM((rows, IN_DIM), F32),
        pltpu.VMEM((rows + 2 * nb, CONV_DIM), F32),
        pltpu.VMEM((rows, 2 * N_STATE), F32),
        pltpu.VMEM((nb, N_STATE), F32),
        pltpu.VMEM((nb, N_STATE), F32),
        pltpu.VMEM((D_MODEL, IN_DIM), BF16),
        pltpu.VMEM((SSM_DIM, SSM_DIM), BF16),
        pltpu.VMEM((D_MODEL, D_MODEL), BF16),
        pltpu.VMEM((2, HALF_SSM, 2 * HALF_STATE), BF16),
        pltpu.VMEM((2, HALF_SSM, HALF_STATE), BF16),
        pltpu.VMEM((2, HALF_SSM, HALF_STATE), BF16),
    ] + tm_scratch
    operands = [x, mod, mod, mod, h0r, h0i, cb0, *weights]
    if x1_dest is not None:
        in_specs = [pl.BlockSpec(memory_space=pl.ANY)] + in_specs
        operands = [x1_dest] + operands
    return pl.pallas_call(
        functools.partial(_mixer_kernel, nb, chunk, batch_major, x1_dest is not None, nchunks),
        grid=(n_steps,),
        in_specs=in_specs, out_specs=out_specs, out_shape=out_shape,
        scratch_shapes=scratch,
        input_output_aliases={0: 0} if x1_dest is not None else {},
        compiler_params=pltpu.CompilerParams(
            dimension_semantics=("arbitrary",), vmem_limit_bytes=VMEM_LIMIT),
        name="mixer",
    )(*operands)


TOK_BLOCK = 512
SEG_ALIGN = 16
ROW_TILE = 512
CHUNKS_PER_TILE = ROW_TILE // SEG_ALIGN
SCATTER_GROUPS = 8
GATHER_BUFFERS = 3
TABLE_CHUNK_ROWS = 64
N_SLOTS = -(-(2 * TOK_BLOCK + N_EXPERTS * (SEG_ALIGN - 1)) // 256) * 256
SPARE_ROW = N_SLOTS - SEG_ALIGN
assert SPARE_ROW >= 2 * TOK_BLOCK + N_EXPERTS * (SEG_ALIGN - 1)
SLOTS_ALWAYS = N_SLOTS - 12 * SEG_ALIGN
ROUTER_ROWS = 48
META_ROWS = SUBLANES

_NT = (((1,), (1,)), ((), ()))
_TN = (((0,), (0,)), ((), ()))


def _iota_f32(shape, dim):
    return lax.broadcasted_iota(jnp.int32, shape, dim).astype(F32)


def _split_bf16(v):
    hi = v.astype(BF16)
    return hi, (v - hi.astype(F32)).astype(BF16)


def _route_t(lg):
    row = _iota_f32(lg.shape, 0)
    neg = jnp.float32(-jnp.inf)
    none = float(ROUTER_ROWS)
    is_g = row < N_EXPERT_GROUPS
    gl = jnp.where(is_g, lg, neg)
    gmax = jnp.max(gl, axis=0, keepdims=True)
    gsum = jnp.sum(jnp.where(is_g, jnp.exp(gl - gmax), 0.0), axis=0, keepdims=True)
    g_prob = 1.0 / gsum
    g_idx = jnp.min(jnp.where(gl == gmax, row, none), axis=0, keepdims=True)
    lo = N_EXPERT_GROUPS + EXPERTS_PER_GROUP * g_idx
    el = jnp.where(row >= lo, jnp.where(row < lo + EXPERTS_PER_GROUP, lg, neg), neg)
    m1 = jnp.max(el, axis=0, keepdims=True)
    i1 = jnp.min(jnp.where(el == m1, row, none), axis=0, keepdims=True)
    el2 = jnp.where(row == i1, neg, el)
    m2 = jnp.max(el2, axis=0, keepdims=True)
    i2 = jnp.min(jnp.where(el2 == m2, row, none), axis=0, keepdims=True)
    e21 = jnp.exp(m2 - m1)
    return i1, i2, g_prob / (1.0 + e21), g_prob * e21 / (1.0 + e21)


def _dispatch_kernel(x1_ref, sc_ref, sh_ref, n2g_ref, wr_ref, br_ref, xs_ref, meta_ref, cnt_ref):
    h = _rms(x1_ref[...], n2g_ref[...]) * (1.0 + sc_ref[0, 0]) + sh_ref[0, 0]
    t = h.shape[0]
    hb, hl = _split_bf16(h)
    wh, wl = _split_bf16(wr_ref[...])
    lg = (lax.dot_general(wh, hb, _NT, preferred_element_type=F32)
          + lax.dot_general(wh, hl, _NT, preferred_element_type=F32)
          + lax.dot_general(wl, hb, _NT, preferred_element_type=F32)) + br_ref[...]
    i1, i2, w1, w2 = _route_t(lg)

    row = _iota_f32((ROUTER_ROWS, t), 0)
    hit1 = row == i1
    hit2 = row == i2
    onehot = jnp.where(hit1, 1.0, 0.0) + jnp.where(hit2, 1.0, 0.0)
    before = jnp.where(_iota_f32((t, t), 0) < _iota_f32((t, t), 1), 1.0, 0.0)
    rank = _bdot(onehot, before)
    cnt = jnp.sum(onehot, axis=1, keepdims=True)
    seg = jnp.floor((cnt + (SEG_ALIGN - 1)) * (1.0 / SEG_ALIGN)) * SEG_ALIGN
    earlier = jnp.where(_iota_f32((ROUTER_ROWS, ROUTER_ROWS), 1) < _iota_f32((ROUTER_ROWS, ROUTER_ROWS), 0),
                        1.0, 0.0)
    base = rank + _bdot(earlier, jnp.broadcast_to(seg, (ROUTER_ROWS, t)))
    slot1 = jnp.sum(jnp.where(hit1, base, 0.0), axis=0, keepdims=True)
    slot2 = jnp.sum(jnp.where(hit2, base, 0.0), axis=0, keepdims=True)
    def sorted_rows(first, n):
        j = _iota_f32((n, t), 0) + float(first)
        place = jnp.where(j == slot1, 1.0, 0.0) + jnp.where(j == slot2, 1.0, 0.0)
        return jnp.dot(place.astype(BF16), hb, preferred_element_type=F32).astype(BF16)

    xs_ref[0:SLOTS_ALWAYS, :] = sorted_rows(0, SLOTS_ALWAYS)
    used = jnp.sum(seg)

    @pl.when(used > SLOTS_ALWAYS)
    def _():
        xs_ref[SLOTS_ALWAYS:N_SLOTS, :] = sorted_rows(SLOTS_ALWAYS, N_SLOTS - SLOTS_ALWAYS)

    @pl.when(used <= SLOTS_ALWAYS)
    def _():
        xs_ref[SLOTS_ALWAYS:N_SLOTS, :] = jnp.zeros((N_SLOTS - SLOTS_ALWAYS, D_MODEL), BF16)

    r = _iota_f32((META_ROWS, t), 0)
    meta_ref[0] = jnp.where(r == 0.0, slot1, jnp.where(r == 1.0, slot2,
                            jnp.where(r == 2.0, w1, jnp.where(r == 3.0, w2, 0.0))))
    cnt_ref[0] = jnp.broadcast_to(cnt, (ROUTER_ROWS, LANES))


def _combine_kernel(nbp, x1_ref, gt_ref, meta_ref, ys_ref, fg_ref, yp_ref, ysm_ref, slab):
    i = pl.program_id(0)
    x1 = x1_ref[...]
    meta = meta_ref[0]
    slot1, slot2, w1, w2 = (meta[k:k + 1, :] for k in range(4))
    j = _iota_f32((N_SLOTS, x1.shape[0]), 0)
    pick = jnp.where(j == slot1, w1, 0.0) + jnp.where(j == slot2, w2, 0.0)
    moe = lax.dot_general(pick.astype(BF16), ys_ref[...], _TN, preferred_element_type=F32)
    y = _rms(x1 + gt_ref[0, 0] * moe, fg_ref[...])

    @pl.when(i < nbp)
    def _():
        nb, steps, d = yp_ref.shape
        for j in range(d // LANES):
            slab[j] = y[:, j * LANES:(j + 1) * LANES]
        for b in range(nb):
            for j in range(d // LANES):
                yp_ref[b, :, j * LANES:(j + 1) * LANES] = slab[j, pl.ds(b, steps, stride=nb), :]

    @pl.when(i >= nbp)
    def _():
        ysm_ref[...] = y


def _chunk_table_kernel(start_ref, m_ref, src_ref, out_ref):
    start = start_ref[...]
    end = start + m_ref[...]
    src = src_ref[...]

    def body(b, _):
        rows = pl.ds(pl.multiple_of(b * TABLE_CHUNK_ROWS, TABLE_CHUNK_ROWS), TABLE_CHUNK_ROWS)
        first = lax.convert_element_type(b * TABLE_CHUNK_ROWS, F32)
        i = _iota_f32((TABLE_CHUNK_ROWS, start.shape[1]), 0) + first
        val = jnp.where(i >= start, jnp.where(i < end, src + SEG_ALIGN * (i - start), 0.0), 0.0)
        out_ref[rows, :] = jnp.sum(val, axis=1, keepdims=True).astype(jnp.int32)
        return 0

    lax.fori_loop(0, out_ref.shape[0] // TABLE_CHUNK_ROWS, body, 0)


def _expert_kernel(te_ref, tnv_ref, tc0_ref, ctab_ref, nt_ref, xs_hbm, wg_ref, wu_ref, wd_ref,
                   ys_hbm, xbuf, ybuf, wgub, wdb, in_sem, out_sem):
    t = pl.program_id(0)
    nt = nt_ref[0]
    slot = t % 2
    xslot = t % GATHER_BUFFERS

    def chunk_rows(j):
        return pl.ds(pl.multiple_of(j * SEG_ALIGN, SEG_ALIGN), SEG_ALIGN)

    def hbm_rows(ref, r):
        return ref.at[pl.ds(pl.multiple_of(r, SEG_ALIGN), SEG_ALIGN), :]

    def in_copy(r, j, s):
        return pltpu.make_async_copy(hbm_rows(xs_hbm, r), xbuf.at[s, chunk_rows(j), :], in_sem.at[s])

    def out_copy(r, j, s):
        return pltpu.make_async_copy(ybuf.at[s, chunk_rows(j), :], hbm_rows(ys_hbm, r), out_sem.at[s])

    def start_gather(tile, s):
        c0 = tc0_ref[tile]
        nv = tnv_ref[tile]
        for j in range(CHUNKS_PER_TILE):
            in_copy(jnp.where(j < nv, ctab_ref[c0 + j], SPARE_ROW), j, s).start()

    def wait_gather(s):
        pltpu.make_async_copy(xs_hbm.at[pl.ds(0, ROW_TILE), :], xbuf.at[s], in_sem.at[s]).wait()

    def wait_scatter(tile, s):
        nv = tnv_ref[tile]

        @pl.when(nv == CHUNKS_PER_TILE)
        def _():
            pltpu.make_async_copy(ybuf.at[s], ys_hbm.at[pl.ds(0, ROW_TILE), :], out_sem.at[s]).wait()

        @pl.when(nv < CHUNKS_PER_TILE)
        def _():
            def body(j, _):
                out_copy(0, j, s).wait()
                return 0

            lax.fori_loop(0, nv, body, 0)

    def run_tile(full):
        ahead = t + GATHER_BUFFERS - 1
        start_gather(ahead, ahead % GATHER_BUFFERS)
        wait_gather(xslot)
        ab = jnp.dot(xbuf[xslot], wgub[...], preferred_element_type=F32)
        a, b = ab[:, :D_EXPERT], ab[:, D_EXPERT:]
        o = jnp.dot((a * _sigmoid(a) * b).astype(BF16), wdb[...], preferred_element_type=F32)
        c0 = tc0_ref[t]
        if full:
            group = ROW_TILE // SCATTER_GROUPS
            for r in range(0, ROW_TILE, group):
                ybuf[slot, r:r + group, :] = o[r:r + group].astype(BF16)
                for j in range(r // SEG_ALIGN, (r + group) // SEG_ALIGN):
                    out_copy(ctab_ref[c0 + j], j, slot).start()
        else:
            ybuf[slot] = o.astype(BF16)

            def body(j, _):
                out_copy(ctab_ref[c0 + j], j, slot).start()
                return 0

            lax.fori_loop(0, tnv_ref[t], body, 0)

    @pl.when(t == 0)
    def _():
        for k in range(GATHER_BUFFERS - 1):
            start_gather(k, k)

    @pl.when(jnp.logical_or(t == 0, te_ref[t] != te_ref[jnp.maximum(t - 1, 0)]))
    def _():
        wgub[:, :D_EXPERT] = wg_ref[0].astype(BF16)
        wgub[:, D_EXPERT:] = wu_ref[0].astype(BF16)
        wdb[...] = wd_ref[0].astype(BF16)

    @pl.when(t < nt)
    def _():
        @pl.when(t >= 2)
        def _():
            wait_scatter(t - 2, slot)

        pl.when(tnv_ref[t] == CHUNKS_PER_TILE)(lambda: run_tile(True))
        pl.when(tnv_ref[t] < CHUNKS_PER_TILE)(lambda: run_tile(False))

        @pl.when(t == nt - 1)
        def _():
            for k in range(1, GATHER_BUFFERS):
                wait_gather((t + k) % GATHER_BUFFERS)
            wait_scatter(t, slot)

            @pl.when(t >= 1)
            def _():
                wait_scatter(t - 1, 1 - slot)


def _tile_plan(cnt):
    nblk = cnt.shape[0]
    max_chunks = nblk * (2 * TOK_BLOCK // SEG_ALIGN + N_EXPERTS)
    max_tiles = max_chunks // CHUNKS_PER_TILE + N_EXPERTS
    table_len = -(-(max_chunks + CHUNKS_PER_TILE) // LANES) * LANES
    m = (cnt + SEG_ALIGN - 1) // SEG_ALIGN
    seg_start = jnp.cumsum(m, axis=1) - m
    src0 = jnp.arange(nblk, dtype=jnp.int32)[:, None] * N_SLOTS + SEG_ALIGN * seg_start
    seg_m = m.T.reshape(1, -1)
    seg_first = jnp.cumsum(seg_m, axis=1) - seg_m
    assert table_len % TABLE_CHUNK_ROWS == 0

    def row(v):
        return jnp.pad(v.astype(F32), ((0, 0), (0, -v.shape[1] % LANES)))

    chunk_src = pl.pallas_call(
        _chunk_table_kernel,
        out_shape=jax.ShapeDtypeStruct((table_len, 1), jnp.int32),
        name="moe_chunk_table",
    )(row(seg_first), row(seg_m), row(src0.T.reshape(1, -1))).reshape(-1)
    ce = jnp.sum(m, axis=0)
    c_start = jnp.cumsum(ce) - ce
    tiles = (ce + CHUNKS_PER_TILE - 1) // CHUNKS_PER_TILE
    t_cum = jnp.cumsum(tiles)
    nt = t_cum[-1]
    k = jnp.arange(max_tiles + GATHER_BUFFERS - 1, dtype=jnp.int32)
    kk = jnp.minimum(k, nt - 1)
    ek = jnp.minimum(jnp.sum((t_cum[None, :] <= kk[:, None]).astype(jnp.int32), axis=1), N_EXPERTS - 1)
    is_e = ek[:, None] == jnp.arange(N_EXPERTS, dtype=jnp.int32)[None, :]
    pick = lambda v: jnp.sum(jnp.where(is_e, v[None, :], 0), axis=1)
    local = kk - pick(t_cum - tiles)
    tile_nv = jnp.where(k < nt, jnp.clip(pick(ce) - CHUNKS_PER_TILE * local, 0, CHUNKS_PER_TILE), 0)
    tile_c0 = jnp.where(k < nt, pick(c_start) + CHUNKS_PER_TILE * local, 0)
    return tuple(v.astype(jnp.int32) for v in (ek, tile_nv, tile_c0, chunk_src, nt.reshape(1)))


def _moe(x1, bp, mod_rows, n2g, wr, br, wg, wu, wd, fg):
    nblk = x1.shape[0] // TOK_BLOCK
    nbp = nblk - 1
    blk = (TOK_BLOCK, D_MODEL)
    x1_spec = pl.BlockSpec(blk, lambda i: (i, 0))

    def mod_spec(k):
        return pl.BlockSpec((1, 1) + blk, lambda i: (k, jnp.minimum(i // nbp, 1), 0, 0))

    sh_spec, sc_spec, gt_spec = mod_spec(0), mod_spec(1), mod_spec(2)
    meta_spec = pl.BlockSpec((1, META_ROWS, TOK_BLOCK), lambda i: (i, 0, 0))
    slots_spec = pl.BlockSpec((N_SLOTS, D_MODEL), lambda i: (i, 0))
    params = pltpu.CompilerParams(dimension_semantics=("arbitrary",), vmem_limit_bytes=VMEM_LIMIT)

    def cs(shape):
        return pl.BlockSpec(shape, lambda i: (0,) * len(shape))

    xs, meta, cnt = pl.pallas_call(
        _dispatch_kernel,
        grid=(nblk,),
        in_specs=[x1_spec, sc_spec, sh_spec, cs((1, D_MODEL)),
                  cs((ROUTER_ROWS, D_MODEL)), cs((ROUTER_ROWS, TOK_BLOCK))],
        out_specs=[slots_spec, meta_spec,
                   pl.BlockSpec((1, ROUTER_ROWS, LANES), lambda i: (i, 0, 0))],
        out_shape=[jax.ShapeDtypeStruct((nblk * N_SLOTS, D_MODEL), BF16),
                   jax.ShapeDtypeStruct((nblk, META_ROWS, TOK_BLOCK), F32),
                   jax.ShapeDtypeStruct((nblk, ROUTER_ROWS, LANES), F32)],
        compiler_params=params,
        name="moe_dispatch",
    )(x1, mod_rows, mod_rows, n2g, wr, br)

    plan = _tile_plan(cnt[:, N_EXPERT_GROUPS:N_EXPERT_GROUPS + N_EXPERTS, 0].astype(jnp.int32))
    max_tiles = plan[0].shape[0] - (GATHER_BUFFERS - 1)

    def w_spec(shape):
        return pl.BlockSpec((1,) + shape, lambda t, te, *_: (te[t], 0, 0))

    ys = pl.pallas_call(
        _expert_kernel,
        grid_spec=pltpu.PrefetchScalarGridSpec(
            num_scalar_prefetch=len(plan),
            grid=(max_tiles,),
            in_specs=[pl.BlockSpec(memory_space=pl.ANY),
                      w_spec((D_MODEL, D_EXPERT)), w_spec((D_MODEL, D_EXPERT)), w_spec((D_EXPERT, D_MODEL))],
            out_specs=pl.BlockSpec(memory_space=pl.ANY),
            scratch_shapes=[pltpu.VMEM((GATHER_BUFFERS, ROW_TILE, D_MODEL), BF16),
                            pltpu.VMEM((2, ROW_TILE, D_MODEL), BF16),
                            pltpu.VMEM((D_MODEL, 2 * D_EXPERT), BF16),
                            pltpu.VMEM((D_EXPERT, D_MODEL), BF16),
                            pltpu.SemaphoreType.DMA((GATHER_BUFFERS,)), pltpu.SemaphoreType.DMA((2,))]),
        out_shape=jax.ShapeDtypeStruct(xs.shape, BF16),
        input_output_aliases={len(plan): 0},
        compiler_params=params,
        name="moe_experts",
    )(*plan, xs, wg, wu, wd)

    return pl.pallas_call(
        functools.partial(_combine_kernel, nbp),
        grid=(nblk,),
        in_specs=[x1_spec, gt_spec, meta_spec, slots_spec, cs((1, D_MODEL))],
        out_specs=[pl.BlockSpec((bp, TOK_BLOCK // bp, D_MODEL), lambda i: (0, jnp.minimum(i, nbp - 1), 0)),
                   pl.BlockSpec(blk, lambda i: (0, 0))],
        out_shape=[jax.ShapeDtypeStruct((bp, nbp * TOK_BLOCK // bp, D_MODEL), F32),
                   jax.ShapeDtypeStruct(blk, F32)],
        scratch_shapes=[pltpu.VMEM((D_MODEL // LANES, TOK_BLOCK, LANES), F32)],
        compiler_params=params,
        name="moe_combine",
    )(x1, mod_rows, meta, ys, fg)


def _to_time_major(x):
    b, s, d = x.shape
    return x.transpose(1, 0, 2).reshape(s * b, d)


def _from_time_major(x, b, s):
    return x.reshape(s, b, -1).transpose(1, 0, 2)


def kernel(x_prompt, x_sample, state_conv, state_ssm_re, state_ssm_im, c_prompt, c_sample, w_ada, b_ada, norm1_g, norm2_g, w_in, conv_w, lambda_re, lambda_im, log_dt, ssm_b_re, ssm_b_im, ssm_c_re, ssm_c_im, ssm_d, w_glu, b_glu, out_norm_conv_g, out_norm_ssm_g, w_out, w_router_group, b_router_group, w_router_expert, b_router_expert, w_expert_gate, w_expert_up, w_expert_down, final_norm_g):
    assert w_ada.shape[0] == 1, "single-layer step"
    bp, sp, _ = x_prompt.shape
    bs, ss, _ = x_sample.shape

    ar, ai, bbr, bbi = _discretise(lambda_re[0], lambda_im[0], log_dt[0], ssm_b_re[0], ssm_b_im[0])
    crt = ssm_c_re[0].transpose(1, 0, 2).reshape(SSM_GROUP, N_STATE)
    cit = ssm_c_im[0].transpose(1, 0, 2).reshape(SSM_GROUP, N_STATE)

    assert bs % bp == 0
    mod, mod_rows = _adaln(jnp.concatenate([c_sample, c_prompt], axis=0), bs, TOK_BLOCK, w_ada[0], b_ada[0])

    mixer_weights = (
        norm1_g[0].reshape(1, -1), w_in[0], conv_w[0], ar, ai, bbr, bbi, crt, cit,
        ssm_d[0].reshape(1, -1), w_glu[0], b_glu[0].reshape(1, -1),
        out_norm_conv_g[0].reshape(1, -1), out_norm_ssm_g[0].reshape(1, -1), w_out[0],
    )

    pad_rows = ROUTER_ROWS - N_EXPERT_GROUPS - N_EXPERTS
    w_router = jnp.concatenate(
        [w_router_group[0].T, w_router_expert[0].transpose(0, 2, 1).reshape(N_EXPERTS, D_MODEL),
         jnp.zeros((pad_rows, D_MODEL), F32)], axis=0)
    b_router = jnp.concatenate(
        [b_router_group[0], b_router_expert[0].reshape(-1), jnp.zeros((pad_rows,), F32)])
    b_router = jnp.broadcast_to(b_router[:, None], (ROUTER_ROWS, TOK_BLOCK))
    moe_weights = (norm2_g[0].reshape(1, -1), w_router, b_router,
                   w_expert_gate[0], w_expert_up[0], w_expert_down[0], final_norm_g.reshape(1, -1))

    assert bs * ss == TOK_BLOCK and (bp * sp) % TOK_BLOCK == 0 and TOK_BLOCK % bp == 0

    def mix(x, mod_row0, h0r, h0i, cb0, chunk, in_kernel_transpose, **x1_where):
        nb, s, _ = x.shape
        x1, cbo, hro, hio = _mixer(
            x if in_kernel_transpose else _to_time_major(x), nb, s, chunk,
            mod, mod_row0, h0r, h0i, cb0, mixer_weights, **x1_where)
        new_conv = cbo.reshape(2, nb, CONV_DIM).transpose(1, 0, 2)[None]
        new_re = hro.reshape(1, nb, SSM_GROUPS, SSM_STATE)
        new_im = hio.reshape(1, nb, SSM_GROUPS, SSM_STATE)
        return x1, new_conv, new_re, new_im

    zero_state = jnp.zeros((bp, N_STATE), F32)
    zero_conv = jnp.zeros((2 * bp, CONV_DIM), F32)
    x1, conv_p, re_p, im_p = mix(x_prompt, bs, zero_state, zero_state, zero_conv, TOK_BLOCK // bp, True,
                                 x1_rows=bp * sp + bs * ss)
    x1, conv_s, re_s, im_s = mix(
        x_sample, 0,
        state_ssm_re[0].reshape(bs, N_STATE), state_ssm_im[0].reshape(bs, N_STATE),
        state_conv[0].transpose(1, 0, 2).reshape(2 * bs, CONV_DIM), ss, False, x1_dest=x1)

    y_p, y_s = _moe(x1, bp, mod_rows, *moe_weights)
    return (y_p, _from_time_major(y_s, bs, ss), conv_p, re_p, im_p, conv_s, re_s, im_s)
```

```python
import functools

import jax
import jax.numpy as jnp
from jax import lax
from jax.experimental import pallas as pl
from jax.experimental.pallas import tpu as pltpu

D_MODEL = 1024
CONV_DIM = 512
SSM_DIM = 512
SSM_GROUP = 16
SSM_GROUPS = 32
SSM_STATE = 64
N_STATE = SSM_GROUPS * SSM_STATE
HALF_SSM = SSM_DIM // 2
HALF_STATE = N_STATE // 2
IN_DIM = 2048
N_EXPERT_GROUPS = 4
EXPERTS_PER_GROUP = 8
N_EXPERTS = 32
D_EXPERT = 256
N_MOD = 6
EPS = 1e-6

SUBLANES = 8
LANES = 128
SCAN_COLS = 512
STRIDE_PAD = 4
VMEM_LIMIT = 60 * 1024 * 1024

F32 = jnp.float32
BF16 = jnp.bfloat16


def _sigmoid(x):
    return 1.0 / (1.0 + jnp.exp(-x))


def _gelu_tanh(x):
    return 0.5 * x * (1.0 + jnp.tanh(0.7978845608028654 * (x + 0.044715 * (x * x * x))))


def _rms(x, g):
    return x * lax.rsqrt(jnp.mean(x * x, axis=-1, keepdims=True) + EPS) * g


def _bdot(a, b):
    return jnp.dot(a.astype(BF16), b.astype(BF16), preferred_element_type=F32)


def _per_seq(v, nb, scale, shift=None):
    rows, d = v.shape
    v3 = v.reshape(rows // nb, nb, d) * scale[None]
    if shift is not None:
        v3 = v3 + shift[None]
    return v3.reshape(rows, d)


def _disc_kernel(lr_ref, li_ref, ldt_ref, brt_ref, bit_ref, ar_ref, ai_ref, bbr_ref, bbi_ref):
    lr = lr_ref[...]
    li = li_ref[...]
    dt = jnp.exp(ldt_ref[...])
    mag = jnp.exp(lr * dt)
    ar = mag * jnp.cos(li * dt)
    ai = mag * jnp.sin(li * dt)
    den = lr * lr + li * li
    fr = ((ar - 1.0) * lr + ai * li) / den
    fi = (ai * lr - (ar - 1.0) * li) / den
    ar_ref[...] = ar
    ai_ref[...] = ai
    brt = brt_ref[...]
    bit = bit_ref[...]
    bbr_ref[...] = fr * brt - fi * bit
    bbi_ref[...] = fr * bit + fi * brt


def _discretise(lambda_re, lambda_im, log_dt, b_re, b_im):
    lr = lambda_re.reshape(1, N_STATE)
    li = lambda_im.reshape(1, N_STATE)
    ldt = jnp.repeat(log_dt, SSM_STATE).reshape(1, N_STATE)
    brt = b_re.transpose(2, 0, 1).reshape(SSM_GROUP, N_STATE)
    bit = b_im.transpose(2, 0, 1).reshape(SSM_GROUP, N_STATE)
    row = jax.ShapeDtypeStruct((1, N_STATE), F32)
    mat = jax.ShapeDtypeStruct((SSM_GROUP, N_STATE), F32)
    return pl.pallas_call(_disc_kernel, out_shape=(row, row, mat, mat), name="s5_discretise")(
        lr, li, ldt, brt, bit)


def _ada_kernel(n_first, block_rows, c_ref, w_ref, b_ref, o_ref, rows_ref):
    c = c_ref[...]
    o = _bdot(c * _sigmoid(c), w_ref[...]) + b_ref[...]
    o_ref[...] = o

    @pl.when(pl.program_id(0) >= N_MOD // 2)
    def _():
        n_rest = o.shape[0] - n_first
        rows_ref[0, 0] = jnp.tile(o[n_first:], (block_rows // n_rest, 1))
        rows_ref[0, 1] = jnp.tile(o[:n_first], (block_rows // n_first, 1))


def _adaln(c_all, n_first, block_rows, w_ada, b_ada):
    n = c_all.shape[0]
    half = N_MOD // 2
    return pl.pallas_call(
        functools.partial(_ada_kernel, n_first, block_rows),
        grid=(N_MOD,),
        in_specs=[
            pl.BlockSpec((n, D_MODEL), lambda j: (0, 0)),
            pl.BlockSpec((D_MODEL, D_MODEL), lambda j: (0, j)),
            pl.BlockSpec((1, D_MODEL), lambda j: (0, j)),
        ],
        out_specs=[pl.BlockSpec((n, D_MODEL), lambda j: (0, j)),
                   pl.BlockSpec((1, 2, block_rows, D_MODEL), lambda j: (jnp.maximum(j - half, 0), 0, 0, 0))],
        out_shape=[jax.ShapeDtypeStruct((n, N_MOD * D_MODEL), F32),
                   jax.ShapeDtypeStruct((half, 2, block_rows, D_MODEL), F32)],
        compiler_params=pltpu.CompilerParams(
            dimension_semantics=("arbitrary",), vmem_limit_bytes=VMEM_LIMIT),
        name="adaln",
    )(c_all, w_ada, b_ada.reshape(1, -1))


def _rows_to_time_major(x_ref, slab, xt_scr):
    nb, steps, d = x_ref.shape
    pitch = slab.shape[1] // nb
    for b in range(nb):
        for j in range(d // LANES):
            slab[j, b * pitch:b * pitch + steps, :] = x_ref[b, :, j * LANES:(j + 1) * LANES]
    for l in range(steps):
        for j in range(d // LANES):
            xt_scr[l * nb:(l + 1) * nb, j * LANES:(j + 1) * LANES] = slab[j, pl.ds(l, nb, stride=pitch), :]


def _group_diagonal(w_t, k):
    tiled = jnp.tile(w_t[:, k * HALF_STATE:(k + 1) * HALF_STATE], (SSM_GROUPS // 2, 1))
    row_group = lax.shift_right_logical(lax.broadcasted_iota(jnp.int32, tiled.shape, 0),
                                        SSM_GROUP.bit_length() - 1)
    lane_group = lax.shift_right_logical(lax.broadcasted_iota(jnp.int32, tiled.shape, 1),
                                         SSM_STATE.bit_length() - 1)
    return jnp.where(row_group == lane_group, tiled, 0.0).astype(BF16)


N_MIXER_INPUTS = 22


def _mixer_kernel(nb, steps, batch_major, has_dest, nchunks, *refs):
    refs = refs[1:] if has_dest else refs
    x1_ref = refs[N_MIXER_INPUTS]

    @pl.when(pl.program_id(0) < nchunks)
    def _():
        _mixer_step(nb, steps, batch_major, *refs)

    @pl.when(pl.program_id(0) >= nchunks)
    def _():
        x1_ref[...] = jnp.zeros_like(x1_ref)


def _mixer_step(nb, steps, batch_major, *refs):
    (x_ref, sh_ref, sc_ref, gt_ref, h0r_ref, h0i_ref, cb0_ref,
     n1g_ref, win_ref, cw_ref, ar_ref, ai_ref, bbr_ref, bbi_ref, crt_ref, cit_ref,
     dsk_ref, wglu_ref, bglu_ref, gcv_ref, gsm_ref, wout_ref,
     x1_ref, cbo_ref, hro_ref, hio_ref,
     z_scr, cv_scr, s_scr, hr_scr, hi_scr, win_b, wglu_b, wout_b, wb_ref, wcr_ref, wci_ref,
     *tm_scr) = refs
    assert refs[N_MIXER_INPUTS] is x1_ref
    rows = nb * steps
    i = pl.program_id(0)

    @pl.when(i == 0)
    def _():
        hr_scr[...] = h0r_ref[...]
        hi_scr[...] = h0i_ref[...]
        cv_scr[0:2 * nb, :] = cb0_ref[...]
        win_b[...] = win_ref[...].astype(BF16)
        wglu_b[...] = wglu_ref[...].astype(BF16)
        wout_b[...] = wout_ref[...].astype(BF16)
        for k in range(2):
            wb_ref[k, :, 0:HALF_STATE] = _group_diagonal(bbr_ref[...], k)
            wb_ref[k, :, HALF_STATE:2 * HALF_STATE] = _group_diagonal(bbi_ref[...], k)
            wcr_ref[k] = _group_diagonal(crt_ref[...], k)
            wci_ref[k] = _group_diagonal(cit_ref[...], k)

    if batch_major:
        slab, x_tm = tm_scr
        _rows_to_time_major(x_ref, slab, x_tm)
    else:
        x_tm = x_ref
    h = _per_seq(_rms(x_tm[...], n1g_ref[...]), nb, 1.0 + sc_ref[...], sh_ref[...])
    z_scr[...] = _bdot(h, win_b[...])

    u_cols = 3 * CONV_DIM
    for k in range(2):
        s_scr[:, 2 * k * HALF_STATE:2 * (k + 1) * HALF_STATE] = _bdot(
            z_scr[:, u_cols + k * HALF_SSM:u_cols + (k + 1) * HALF_SSM], wb_ref[k])

    def scan_cols(cg):
        cols = slice(cg * SCAN_COLS, (cg + 1) * SCAN_COLS)
        half, off = divmod(cg * SCAN_COLS, HALF_STATE)
        re_cols = slice(2 * half * HALF_STATE + off, 2 * half * HALF_STATE + off + SCAN_COLS)
        im_cols = slice(re_cols.start + HALF_STATE, re_cols.stop + HALF_STATE)
        a_r = jnp.broadcast_to(ar_ref[:, cols], (SUBLANES, SCAN_COLS))
        a_i = jnp.broadcast_to(ai_ref[:, cols], (SUBLANES, SCAN_COLS))

        def advance(hr, hi, rows_l):
            nr = a_r * hr - a_i * hi + s_scr[rows_l, re_cols]
            ni = a_r * hi + a_i * hr + s_scr[rows_l, im_cols]
            s_scr[rows_l, re_cols] = nr
            s_scr[rows_l, im_cols] = ni
            return nr, ni

        for r0 in range(0, nb, SUBLANES):
            tile = slice(r0, r0 + SUBLANES)
            hr, hi = hr_scr[tile, cols], hi_scr[tile, cols]
            for l in range(steps):
                hr, hi = advance(hr, hi, slice(l * nb + r0, l * nb + r0 + SUBLANES))
            hr_scr[tile, cols] = hr
            hi_scr[tile, cols] = hi

    y_halves = []
    for k in range(2):
        for cg in range(k * HALF_STATE // SCAN_COLS, (k + 1) * HALF_STATE // SCAN_COLS):
            scan_cols(cg)
        s_re = s_scr[:, 2 * k * HALF_STATE:(2 * k + 1) * HALF_STATE].astype(BF16)
        s_im = s_scr[:, (2 * k + 1) * HALF_STATE:(2 * k + 2) * HALF_STATE].astype(BF16)
        y_halves.append(lax.dot_general(s_re, wcr_ref[k], _NT, preferred_element_type=F32)
                        - lax.dot_general(s_im, wci_ref[k], _NT, preferred_element_type=F32))

    hro_ref[...] = hr_scr[...]
    hio_ref[...] = hi_scr[...]

    y = jnp.concatenate(y_halves, axis=-1) + dsk_ref[...] * z_scr[:, u_cols:IN_DIM]
    g = _gelu_tanh(y)
    y_ssm = _rms(g * _sigmoid(_bdot(g, wglu_b[...]) + bglu_ref[...]), gsm_ref[...])

    cv_scr[2 * nb:2 * nb + rows, :] = z_scr[:, CONV_DIM:2 * CONV_DIM] * z_scr[:, 2 * CONV_DIM:3 * CONV_DIM]
    conv = (cv_scr[0:rows, :] * cw_ref[0:1, :]
            + cv_scr[nb:nb + rows, :] * cw_ref[1:2, :]
            + cv_scr[2 * nb:2 * nb + rows, :] * cw_ref[2:3, :])
    y_conv = _rms(z_scr[:, 0:CONV_DIM] * conv, gcv_ref[...])
    carry = cv_scr[rows:rows + 2 * nb, :]
    cv_scr[0:2 * nb, :] = carry
    cbo_ref[...] = carry

    mixed = _bdot(y_conv, wout_b[0:CONV_DIM, :]) + _bdot(y_ssm, wout_b[CONV_DIM:2 * CONV_DIM, :])
    x1_ref[...] = x_tm[...] + _per_seq(mixed, nb, gt_ref[...])


def _const_spec(shape):
    return pl.BlockSpec(shape, lambda i: (0,) * len(shape), pipeline_mode=pl.Buffered(1))


def _mixer(x, nb, steps_total, chunk, mod, mod_row0, h0r, h0i, cb0, weights, x1_rows=None, x1_dest=None):
    rows = nb * chunk
    nchunks = steps_total // chunk
    batch_major = x.ndim == 3
    n_rows = steps_total * nb
    if x1_dest is not None:
        x1_rows = x1_dest.shape[0]
    elif x1_rows is None:
        x1_rows = n_rows
    first_block = (x1_rows - n_rows) // rows if x1_dest is not None else 0
    n_steps = nchunks if x1_dest is not None else x1_rows // rows
    last = nchunks - 1
    if batch_major:
        assert nb == SUBLANES
        x_spec = pl.BlockSpec((nb, chunk, D_MODEL), lambda i: (0, jnp.minimum(i, last), 0))
        tm_scratch = [pltpu.VMEM((D_MODEL // LANES, nb * (chunk + STRIDE_PAD), LANES), F32),
                      pltpu.VMEM((rows, D_MODEL), F32)]
    else:
        x_spec = pl.BlockSpec((rows, D_MODEL), lambda i: (jnp.minimum(i, last), 0))
        tm_scratch = []
    assert mod_row0 % nb == 0

    def mod_spec(k):
        return pl.BlockSpec((nb, D_MODEL), lambda i: (mod_row0 // nb, k), pipeline_mode=pl.Buffered(1))

    in_specs = [
        x_spec, mod_spec(0), mod_spec(1), mod_spec(2),
        _const_spec((nb, N_STATE)), _const_spec((nb, N_STATE)), _const_spec((2 * nb, CONV_DIM)),
    ] + [_const_spec(w.shape) for w in weights]
    out_specs = [
        pl.BlockSpec((rows, D_MODEL), lambda i: (first_block + i, 0)),
        _const_spec((2 * nb, CONV_DIM)), _const_spec((nb, N_STATE)), _const_spec((nb, N_STATE)),
    ]
    out_shape = [
        jax.ShapeDtypeStruct((x1_rows, D_MODEL), F32),
        jax.ShapeDtypeStruct((2 * nb, CONV_DIM), F32),
        jax.ShapeDtypeStruct((nb, N_STATE), F32),
        jax.ShapeDtypeStruct((nb, N_STATE), F32),
    ]
    scratch = [
        pltpu.VMEM((rows, IN_DIM), F32),
        pltpu.VMEM((rows + 2 * nb, CONV_DIM), F32),
        pltpu.VMEM((rows, 2 * N_STATE), F32),
        pltpu.VMEM((nb, N_STATE), F32),
        pltpu.VMEM((nb, N_STATE), F32),
        pltpu.VMEM((D_MODEL, IN_DIM), BF16),
        pltpu.VMEM((SSM_DIM, SSM_DIM), BF16),
        pltpu.VMEM((D_MODEL, D_MODEL), BF16),
        pltpu.VMEM((2, HALF_SSM, 2 * HALF_STATE), BF16),
        pltpu.VMEM((2, HALF_SSM, HALF_STATE), BF16),
        pltpu.VMEM((2, HALF_SSM, HALF_STATE), BF16),
    ] + tm_scratch
    operands = [x, mod, mod, mod, h0r, h0i, cb0, *weights]
    if x1_dest is not None:
        in_specs = [pl.BlockSpec(memory_space=pl.ANY)] + in_specs
        operands = [x1_dest] + operands
    return pl.pallas_call(
        functools.partial(_mixer_kernel, nb, chunk, batch_major, x1_dest is not None, nchunks),
        grid=(n_steps,),
        in_specs=in_specs, out_specs=out_specs, out_shape=out_shape,
        scratch_shapes=scratch,
        input_output_aliases={0: 0} if x1_dest is not None else {},
        compiler_params=pltpu.CompilerParams(
            dimension_semantics=("arbitrary",), vmem_limit_bytes=VMEM_LIMIT),
        name="mixer",
    )(*operands)


TOK_BLOCK = 512
SEG_ALIGN = 16
ROW_TILE = 512
CHUNKS_PER_TILE = ROW_TILE // SEG_ALIGN
SCATTER_GROUPS = 8
GATHER_BUFFERS = 3
TABLE_CHUNK_ROWS = 64
N_SLOTS = -(-(2 * TOK_BLOCK + N_EXPERTS * (SEG_ALIGN - 1)) // 256) * 256
SPARE_ROW = N_SLOTS - SEG_ALIGN
assert SPARE_ROW >= 2 * TOK_BLOCK + N_EXPERTS * (SEG_ALIGN - 1)
SLOTS_ALWAYS = N_SLOTS - 12 * SEG_ALIGN
ROUTER_ROWS = 48
META_ROWS = SUBLANES

_NT = (((1,), (1,)), ((), ()))
_TN = (((0,), (0,)), ((), ()))


def _iota_f32(shape, dim):
    return lax.broadcasted_iota(jnp.int32, shape, dim).astype(F32)


def _split_bf16(v):
    hi = v.astype(BF16)
    return hi, (v - hi.astype(F32)).astype(BF16)


def _route_t(lg):
    row = _iota_f32(lg.shape, 0)
    neg = jnp.float32(-jnp.inf)
    none = float(ROUTER_ROWS)
    is_g = row < N_EXPERT_GROUPS
    gl = jnp.where(is_g, lg, neg)
    gmax = jnp.max(gl, axis=0, keepdims=True)
    gsum = jnp.sum(jnp.where(is_g, jnp.exp(gl - gmax), 0.0), axis=0, keepdims=True)
    g_prob = 1.0 / gsum
    g_idx = jnp.min(jnp.where(gl == gmax, row, none), axis=0, keepdims=True)
    lo = N_EXPERT_GROUPS + EXPERTS_PER_GROUP * g_idx
    el = jnp.where(row >= lo, jnp.where(row < lo + EXPERTS_PER_GROUP, lg, neg), neg)
    m1 = jnp.max(el, axis=0, keepdims=True)
    i1 = jnp.min(jnp.where(el == m1, row, none), axis=0, keepdims=True)
    el2 = jnp.where(row == i1, neg, el)
    m2 = jnp.max(el2, axis=0, keepdims=True)
    i2 = jnp.min(jnp.where(el2 == m2, row, none), axis=0, keepdims=True)
    e21 = jnp.exp(m2 - m1)
    return i1, i2, g_prob / (1.0 + e21), g_prob * e21 / (1.0 + e21)


def _dispatch_kernel(x1_ref, sc_ref, sh_ref, n2g_ref, wr_ref, br_ref, xs_ref, meta_ref, cnt_ref):
    h = _rms(x1_ref[...], n2g_ref[...]) * (1.0 + sc_ref[0, 0]) + sh_ref[0, 0]
    t = h.shape[0]
    hb, hl = _split_bf16(h)
    wh, wl = _split_bf16(wr_ref[...])
    lg = (lax.dot_general(wh, hb, _NT, preferred_element_type=F32)
          + lax.dot_general(wh, hl, _NT, preferred_element_type=F32)
          + lax.dot_general(wl, hb, _NT, preferred_element_type=F32)) + br_ref[...]
    i1, i2, w1, w2 = _route_t(lg)

    row = _iota_f32((ROUTER_ROWS, t), 0)
    hit1 = row == i1
    hit2 = row == i2
    onehot = jnp.where(hit1, 1.0, 0.0) + jnp.where(hit2, 1.0, 0.0)
    before = jnp.where(_iota_f32((t, t), 0) < _iota_f32((t, t), 1), 1.0, 0.0)
    rank = _bdot(onehot, before)
    cnt = jnp.sum(onehot, axis=1, keepdims=True)
    seg = jnp.floor((cnt + (SEG_ALIGN - 1)) * (1.0 / SEG_ALIGN)) * SEG_ALIGN
    earlier = jnp.where(_iota_f32((ROUTER_ROWS, ROUTER_ROWS), 1) < _iota_f32((ROUTER_ROWS, ROUTER_ROWS), 0),
                        1.0, 0.0)
    base = rank + _bdot(earlier, jnp.broadcast_to(seg, (ROUTER_ROWS, t)))
    slot1 = jnp.sum(jnp.where(hit1, base, 0.0), axis=0, keepdims=True)
    slot2 = jnp.sum(jnp.where(hit2, base, 0.0), axis=0, keepdims=True)
    def sorted_rows(first, n):
        j = _iota_f32((n, t), 0) + float(first)
        place = jnp.where(j == slot1, 1.0, 0.0) + jnp.where(j == slot2, 1.0, 0.0)
        return jnp.dot(place.astype(BF16), hb, preferred_element_type=F32).astype(BF16)

    xs_ref[0:SLOTS_ALWAYS, :] = sorted_rows(0, SLOTS_ALWAYS)
    used = jnp.sum(seg)

    @pl.when(used > SLOTS_ALWAYS)
    def _():
        xs_ref[SLOTS_ALWAYS:N_SLOTS, :] = sorted_rows(SLOTS_ALWAYS, N_SLOTS - SLOTS_ALWAYS)

    @pl.when(used <= SLOTS_ALWAYS)
    def _():
        xs_ref[SLOTS_ALWAYS:N_SLOTS, :] = jnp.zeros((N_SLOTS - SLOTS_ALWAYS, D_MODEL), BF16)

    r = _iota_f32((META_ROWS, t), 0)
    meta_ref[0] = jnp.where(r == 0.0, slot1, jnp.where(r == 1.0, slot2,
                            jnp.where(r == 2.0, w1, jnp.where(r == 3.0, w2, 0.0))))
    cnt_ref[0] = jnp.broadcast_to(cnt, (ROUTER_ROWS, LANES))


def _combine_kernel(nbp, x1_ref, gt_ref, meta_ref, ys_ref, fg_ref, yp_ref, ysm_ref, slab):
    i = pl.program_id(0)
    x1 = x1_ref[...]
    meta = meta_ref[0]
    slot1, slot2, w1, w2 = (meta[k:k + 1, :] for k in range(4))
    j = _iota_f32((N_SLOTS, x1.shape[0]), 0)
    pick = jnp.where(j == slot1, w1, 0.0) + jnp.where(j == slot2, w2, 0.0)
    moe = lax.dot_general(pick.astype(BF16), ys_ref[...], _TN, preferred_element_type=F32)
    y = _rms(x1 + gt_ref[0, 0] * moe, fg_ref[...])

    @pl.when(i < nbp)
    def _():
        nb, steps, d = yp_ref.shape
        pitch = nb + STRIDE_PAD
        for j in range(d // LANES):
            for l in range(steps):
                slab[j, l * pitch:l * pitch + nb, :] = y[l * nb:(l + 1) * nb, j * LANES:(j + 1) * LANES]
        for b in range(nb):
            for j in range(d // LANES):
                yp_ref[b, :, j * LANES:(j + 1) * LANES] = slab[j, pl.ds(b, steps, stride=pitch), :]

    @pl.when(i >= nbp)
    def _():
        ysm_ref[...] = y


def _chunk_table_kernel(start_ref, m_ref, src_ref, out_ref):
    start = start_ref[...]
    end = start + m_ref[...]
    src = src_ref[...]

    def body(b, _):
        rows = pl.ds(pl.multiple_of(b * TABLE_CHUNK_ROWS, TABLE_CHUNK_ROWS), TABLE_CHUNK_ROWS)
        first = lax.convert_element_type(b * TABLE_CHUNK_ROWS, F32)
        i = _iota_f32((TABLE_CHUNK_ROWS, start.shape[1]), 0) + first
        val = jnp.where(i >= start, jnp.where(i < end, src + SEG_ALIGN * (i - start), 0.0), 0.0)
        out_ref[rows, :] = jnp.sum(val, axis=1, keepdims=True).astype(jnp.int32)
        return 0

    lax.fori_loop(0, out_ref.shape[0] // TABLE_CHUNK_ROWS, body, 0)


def _expert_kernel(te_ref, tnv_ref, tc0_ref, ctab_ref, nt_ref, xs_hbm, wg_ref, wu_ref, wd_ref,
                   ys_hbm, xbuf, ybuf, wgub, wdb, in_sem, out_sem):
    t = pl.program_id(0)
    nt = nt_ref[0]
    slot = t % 2
    xslot = t % GATHER_BUFFERS

    def chunk_rows(j):
        return pl.ds(pl.multiple_of(j * SEG_ALIGN, SEG_ALIGN), SEG_ALIGN)

    def hbm_rows(ref, r):
        return ref.at[pl.ds(pl.multiple_of(r, SEG_ALIGN), SEG_ALIGN), :]

    def in_copy(r, j, s):
        return pltpu.make_async_copy(hbm_rows(xs_hbm, r), xbuf.at[s, chunk_rows(j), :], in_sem.at[s])

    def out_copy(r, j, s):
        return pltpu.make_async_copy(ybuf.at[s, chunk_rows(j), :], hbm_rows(ys_hbm, r), out_sem.at[s])

    def start_gather(tile, s):
        c0 = tc0_ref[tile]
        nv = tnv_ref[tile]
        for j in range(CHUNKS_PER_TILE):
            in_copy(jnp.where(j < nv, ctab_ref[c0 + j], SPARE_ROW), j, s).start()

    def wait_gather(s):
        pltpu.make_async_copy(xs_hbm.at[pl.ds(0, ROW_TILE), :], xbuf.at[s], in_sem.at[s]).wait()

    def wait_scatter(tile, s):
        nv = tnv_ref[tile]

        @pl.when(nv == CHUNKS_PER_TILE)
        def _():
            pltpu.make_async_copy(ybuf.at[s], ys_hbm.at[pl.ds(0, ROW_TILE), :], out_sem.at[s]).wait()

        @pl.when(nv < CHUNKS_PER_TILE)
        def _():
            def body(j, _):
                out_copy(0, j, s).wait()
                return 0

            lax.fori_loop(0, nv, body, 0)

    def run_tile(full):
        ahead = t + GATHER_BUFFERS - 1
        start_gather(ahead, ahead % GATHER_BUFFERS)
        wait_gather(xslot)
        ab = jnp.dot(xbuf[xslot], wgub[...], preferred_element_type=F32)
        a, b = ab[:, :D_EXPERT], ab[:, D_EXPERT:]
        o = jnp.dot((a * _sigmoid(a) * b).astype(BF16), wdb[...], preferred_element_type=F32)
        c0 = tc0_ref[t]
        if full:
            group = ROW_TILE // SCATTER_GROUPS
            for r in range(0, ROW_TILE, group):
                ybuf[slot, r:r + group, :] = o[r:r + group].astype(BF16)
                for j in range(r // SEG_ALIGN, (r + group) // SEG_ALIGN):
                    out_copy(ctab_ref[c0 + j], j, slot).start()
        else:
            ybuf[slot] = o.astype(BF16)

            def body(j, _):
                out_copy(ctab_ref[c0 + j], j, slot).start()
                return 0

            lax.fori_loop(0, tnv_ref[t], body, 0)

    @pl.when(t == 0)
    def _():
        for k in range(GATHER_BUFFERS - 1):
            start_gather(k, k)

    @pl.when(jnp.logical_or(t == 0, te_ref[t] != te_ref[jnp.maximum(t - 1, 0)]))
    def _():
        wgub[:, :D_EXPERT] = wg_ref[0].astype(BF16)
        wgub[:, D_EXPERT:] = wu_ref[0].astype(BF16)
        wdb[...] = wd_ref[0].astype(BF16)

    @pl.when(t < nt)
    def _():
        @pl.when(t >= 2)
        def _():
            wait_scatter(t - 2, slot)

        pl.when(tnv_ref[t] == CHUNKS_PER_TILE)(lambda: run_tile(True))
        pl.when(tnv_ref[t] < CHUNKS_PER_TILE)(lambda: run_tile(False))

        @pl.when(t == nt - 1)
        def _():
            for k in range(1, GATHER_BUFFERS):
                wait_gather((t + k) % GATHER_BUFFERS)
            wait_scatter(t, slot)

            @pl.when(t >= 1)
            def _():
                wait_scatter(t - 1, 1 - slot)


def _tile_plan(cnt):
    nblk = cnt.shape[0]
    max_chunks = nblk * (2 * TOK_BLOCK // SEG_ALIGN + N_EXPERTS)
    max_tiles = max_chunks // CHUNKS_PER_TILE + N_EXPERTS
    table_len = -(-(max_chunks + CHUNKS_PER_TILE) // LANES) * LANES
    m = (cnt + SEG_ALIGN - 1) // SEG_ALIGN
    seg_start = jnp.cumsum(m, axis=1) - m
    src0 = jnp.arange(nblk, dtype=jnp.int32)[:, None] * N_SLOTS + SEG_ALIGN * seg_start
    seg_m = m.T.reshape(1, -1)
    seg_first = jnp.cumsum(seg_m, axis=1) - seg_m
    assert table_len % TABLE_CHUNK_ROWS == 0

    def row(v):
        return jnp.pad(v.astype(F32), ((0, 0), (0, -v.shape[1] % LANES)))

    chunk_src = pl.pallas_call(
        _chunk_table_kernel,
        out_shape=jax.ShapeDtypeStruct((table_len, 1), jnp.int32),
        name="moe_chunk_table",
    )(row(seg_first), row(seg_m), row(src0.T.reshape(1, -1))).reshape(-1)
    ce = jnp.sum(m, axis=0)
    c_start = jnp.cumsum(ce) - ce
    tiles = (ce + CHUNKS_PER_TILE - 1) // CHUNKS_PER_TILE
    t_cum = jnp.cumsum(tiles)
    nt = t_cum[-1]
    k = jnp.arange(max_tiles + GATHER_BUFFERS - 1, dtype=jnp.int32)
    kk = jnp.minimum(k, nt - 1)
    ek = jnp.minimum(jnp.sum((t_cum[None, :] <= kk[:, None]).astype(jnp.int32), axis=1), N_EXPERTS - 1)
    is_e = ek[:, None] == jnp.arange(N_EXPERTS, dtype=jnp.int32)[None, :]
    pick = lambda v: jnp.sum(jnp.where(is_e, v[None, :], 0), axis=1)
    local = kk - pick(t_cum - tiles)
    tile_nv = jnp.where(k < nt, jnp.clip(pick(ce) - CHUNKS_PER_TILE * local, 0, CHUNKS_PER_TILE), 0)
    tile_c0 = jnp.where(k < nt, pick(c_start) + CHUNKS_PER_TILE * local, 0)
    return tuple(v.astype(jnp.int32) for v in (ek, tile_nv, tile_c0, chunk_src, nt.reshape(1)))


def _moe(x1, bp, mod_rows, n2g, wr, br, wg, wu, wd, fg):
    nblk = x1.shape[0] // TOK_BLOCK
    nbp = nblk - 1
    blk = (TOK_BLOCK, D_MODEL)
    x1_spec = pl.BlockSpec(blk, lambda i: (i, 0))

    def mod_spec(k):
        return pl.BlockSpec((1, 1) + blk, lambda i: (k, jnp.minimum(i // nbp, 1), 0, 0))

    sh_spec, sc_spec, gt_spec = mod_spec(0), mod_spec(1), mod_spec(2)
    meta_spec = pl.BlockSpec((1, META_ROWS, TOK_BLOCK), lambda i: (i, 0, 0))
    slots_spec = pl.BlockSpec((N_SLOTS, D_MODEL), lambda i: (i, 0))
    params = pltpu.CompilerParams(dimension_semantics=("arbitrary",), vmem_limit_bytes=VMEM_LIMIT)

    def cs(shape):
        return pl.BlockSpec(shape, lambda i: (0,) * len(shape))

    xs, meta, cnt = pl.pallas_call(
        _dispatch_kernel,
        grid=(nblk,),
        in_specs=[x1_spec, sc_spec, sh_spec, cs((1, D_MODEL)),
                  cs((ROUTER_ROWS, D_MODEL)), cs((ROUTER_ROWS, TOK_BLOCK))],
        out_specs=[slots_spec, meta_spec,
                   pl.BlockSpec((1, ROUTER_ROWS, LANES), lambda i: (i, 0, 0))],
        out_shape=[jax.ShapeDtypeStruct((nblk * N_SLOTS, D_MODEL), BF16),
                   jax.ShapeDtypeStruct((nblk, META_ROWS, TOK_BLOCK), F32),
                   jax.ShapeDtypeStruct((nblk, ROUTER_ROWS, LANES), F32)],
        compiler_params=params,
        name="moe_dispatch",
    )(x1, mod_rows, mod_rows, n2g, wr, br)

    plan = _tile_plan(cnt[:, N_EXPERT_GROUPS:N_EXPERT_GROUPS + N_EXPERTS, 0].astype(jnp.int32))
    max_tiles = plan[0].shape[0] - (GATHER_BUFFERS - 1)

    def w_spec(shape):
        return pl.BlockSpec((1,) + shape, lambda t, te, *_: (te[t], 0, 0))

    ys = pl.pallas_call(
        _expert_kernel,
        grid_spec=pltpu.PrefetchScalarGridSpec(
            num_scalar_prefetch=len(plan),
            grid=(max_tiles,),
            in_specs=[pl.BlockSpec(memory_space=pl.ANY),
                      w_spec((D_MODEL, D_EXPERT)), w_spec((D_MODEL, D_EXPERT)), w_spec((D_EXPERT, D_MODEL))],
            out_specs=pl.BlockSpec(memory_space=pl.ANY),
            scratch_shapes=[pltpu.VMEM((GATHER_BUFFERS, ROW_TILE, D_MODEL), BF16),
                            pltpu.VMEM((2, ROW_TILE, D_MODEL), BF16),
                            pltpu.VMEM((D_MODEL, 2 * D_EXPERT), BF16),
                            pltpu.VMEM((D_EXPERT, D_MODEL), BF16),
                            pltpu.SemaphoreType.DMA((GATHER_BUFFERS,)), pltpu.SemaphoreType.DMA((2,))]),
        out_shape=jax.ShapeDtypeStruct(xs.shape, BF16),
        input_output_aliases={len(plan): 0},
        compiler_params=params,
        name="moe_experts",
    )(*plan, xs, wg, wu, wd)

    return pl.pallas_call(
        functools.partial(_combine_kernel, nbp),
        grid=(nblk,),
        in_specs=[x1_spec, gt_spec, meta_spec, slots_spec, cs((1, D_MODEL))],
        out_specs=[pl.BlockSpec((bp, TOK_BLOCK // bp, D_MODEL), lambda i: (0, jnp.minimum(i, nbp - 1), 0)),
                   pl.BlockSpec(blk, lambda i: (0, 0))],
        out_shape=[jax.ShapeDtypeStruct((bp, nbp * TOK_BLOCK // bp, D_MODEL), F32),
                   jax.ShapeDtypeStruct(blk, F32)],
        scratch_shapes=[pltpu.VMEM((D_MODEL // LANES, TOK_BLOCK // bp * (bp + STRIDE_PAD), LANES), F32)],
        compiler_params=params,
        name="moe_combine",
    )(x1, mod_rows, meta, ys, fg)


def _to_time_major(x):
    b, s, d = x.shape
    return x.transpose(1, 0, 2).reshape(s * b, d)


def _from_time_major(x, b, s):
    return x.reshape(s, b, -1).transpose(1, 0, 2)


def kernel(x_prompt, x_sample, state_conv, state_ssm_re, state_ssm_im, c_prompt, c_sample, w_ada, b_ada, norm1_g, norm2_g, w_in, conv_w, lambda_re, lambda_im, log_dt, ssm_b_re, ssm_b_im, ssm_c_re, ssm_c_im, ssm_d, w_glu, b_glu, out_norm_conv_g, out_norm_ssm_g, w_out, w_router_group, b_router_group, w_router_expert, b_router_expert, w_expert_gate, w_expert_up, w_expert_down, final_norm_g):
    assert w_ada.shape[0] == 1, "single-layer step"
    bp, sp, _ = x_prompt.shape
    bs, ss, _ = x_sample.shape

    ar, ai, bbr, bbi = _discretise(lambda_re[0], lambda_im[0], log_dt[0], ssm_b_re[0], ssm_b_im[0])
    crt = ssm_c_re[0].transpose(1, 0, 2).reshape(SSM_GROUP, N_STATE)
    cit = ssm_c_im[0].transpose(1, 0, 2).reshape(SSM_GROUP, N_STATE)

    assert bs % bp == 0
    mod, mod_rows = _adaln(jnp.concatenate([c_sample, c_prompt], axis=0), bs, TOK_BLOCK, w_ada[0], b_ada[0])

    mixer_weights = (
        norm1_g[0].reshape(1, -1), w_in[0], conv_w[0], ar, ai, bbr, bbi, crt, cit,
        ssm_d[0].reshape(1, -1), w_glu[0], b_glu[0].reshape(1, -1),
        out_norm_conv_g[0].reshape(1, -1), out_norm_ssm_g[0].reshape(1, -1), w_out[0],
    )

    pad_rows = ROUTER_ROWS - N_EXPERT_GROUPS - N_EXPERTS
    w_router = jnp.concatenate(
        [w_router_group[0].T, w_router_expert[0].transpose(0, 2, 1).reshape(N_EXPERTS, D_MODEL),
         jnp.zeros((pad_rows, D_MODEL), F32)], axis=0)
    b_router = jnp.concatenate(
        [b_router_group[0], b_router_expert[0].reshape(-1), jnp.zeros((pad_rows,), F32)])
    b_router = jnp.broadcast_to(b_router[:, None], (ROUTER_ROWS, TOK_BLOCK))
    moe_weights = (norm2_g[0].reshape(1, -1), w_router, b_router,
                   w_expert_gate[0], w_expert_up[0], w_expert_down[0], final_norm_g.reshape(1, -1))

    assert bs * ss == TOK_BLOCK and (bp * sp) % TOK_BLOCK == 0 and TOK_BLOCK % bp == 0

    def mix(x, mod_row0, h0r, h0i, cb0, chunk, in_kernel_transpose, **x1_where):
        nb, s, _ = x.shape
        x1, cbo, hro, hio = _mixer(
            x if in_kernel_transpose else _to_time_major(x), nb, s, chunk,
            mod, mod_row0, h0r, h0i, cb0, mixer_weights, **x1_where)
        new_conv = cbo.reshape(2, nb, CONV_DIM).transpose(1, 0, 2)[None]
        new_re = hro.reshape(1, nb, SSM_GROUPS, SSM_STATE)
        new_im = hio.reshape(1, nb, SSM_GROUPS, SSM_STATE)
        return x1, new_conv, new_re, new_im

    zero_state = jnp.zeros((bp, N_STATE), F32)
    zero_conv = jnp.zeros((2 * bp, CONV_DIM), F32)
    x1, conv_p, re_p, im_p = mix(x_prompt, bs, zero_state, zero_state, zero_conv, TOK_BLOCK // bp, True,
                                 x1_rows=bp * sp + bs * ss)
    x1, conv_s, re_s, im_s = mix(
        x_sample, 0,
        state_ssm_re[0].reshape(bs, N_STATE), state_ssm_im[0].reshape(bs, N_STATE),
        state_conv[0].transpose(1, 0, 2).reshape(2 * bs, CONV_DIM), ss, False, x1_dest=x1)

    y_p, y_s = _moe(x1, bp, mod_rows, *moe_weights)
    return (y_p, _from_time_major(y_s, bs, ss), conv_p, re_p, im_p, conv_s, re_s, im_s)
```

```python
import functools

import jax
import jax.numpy as jnp
from jax import lax
from jax.experimental import pallas as pl
from jax.experimental.pallas import tpu as pltpu

D_MODEL = 1024
CONV_DIM = 512
SSM_DIM = 512
SSM_GROUP = 16
SSM_GROUPS = 32
SSM_STATE = 64
N_STATE = SSM_GROUPS * SSM_STATE
HALF_SSM = SSM_DIM // 2
HALF_STATE = N_STATE // 2
IN_DIM = 2048
N_EXPERT_GROUPS = 4
EXPERTS_PER_GROUP = 8
N_EXPERTS = 32
D_EXPERT = 256
N_MOD = 6
EPS = 1e-6

SUBLANES = 8
LANES = 128
SCAN_COLS = 512
STRIDE_PAD = 4
VMEM_LIMIT = 60 * 1024 * 1024

F32 = jnp.float32
BF16 = jnp.bfloat16


def _sigmoid(x):
    return 1.0 / (1.0 + jnp.exp(-x))


def _gelu_tanh(x):
    return 0.5 * x * (1.0 + jnp.tanh(0.7978845608028654 * (x + 0.044715 * (x * x * x))))


def _rms(x, g):
    return x * lax.rsqrt(jnp.mean(x * x, axis=-1, keepdims=True) + EPS) * g


def _bdot(a, b):
    return jnp.dot(a.astype(BF16), b.astype(BF16), preferred_element_type=F32)


def _per_seq(v, nb, scale, shift=None):
    rows, d = v.shape
    v3 = v.reshape(rows // nb, nb, d) * scale[None]
    if shift is not None:
        v3 = v3 + shift[None]
    return v3.reshape(rows, d)


def _disc_kernel(lr_ref, li_ref, ldt_ref, brt_ref, bit_ref, ar_ref, ai_ref, bbr_ref, bbi_ref):
    lr = lr_ref[...]
    li = li_ref[...]
    dt = jnp.exp(ldt_ref[...])
    mag = jnp.exp(lr * dt)
    ar = mag * jnp.cos(li * dt)
    ai = mag * jnp.sin(li * dt)
    den = lr * lr + li * li
    fr = ((ar - 1.0) * lr + ai * li) / den
    fi = (ai * lr - (ar - 1.0) * li) / den
    ar_ref[...] = ar
    ai_ref[...] = ai
    brt = brt_ref[...]
    bit = bit_ref[...]
    bbr_ref[...] = fr * brt - fi * bit
    bbi_ref[...] = fr * bit + fi * brt


def _discretise(lambda_re, lambda_im, log_dt, b_re, b_im):
    lr = lambda_re.reshape(1, N_STATE)
    li = lambda_im.reshape(1, N_STATE)
    ldt = jnp.repeat(log_dt, SSM_STATE).reshape(1, N_STATE)
    brt = b_re.transpose(2, 0, 1).reshape(SSM_GROUP, N_STATE)
    bit = b_im.transpose(2, 0, 1).reshape(SSM_GROUP, N_STATE)
    row = jax.ShapeDtypeStruct((1, N_STATE), F32)
    mat = jax.ShapeDtypeStruct((SSM_GROUP, N_STATE), F32)
    return pl.pallas_call(_disc_kernel, out_shape=(row, row, mat, mat), name="s5_discretise")(
        lr, li, ldt, brt, bit)


def _ada_kernel(n_first, block_rows, c_ref, w_ref, b_ref, o_ref, rows_ref):
    c = c_ref[...]
    o = _bdot(c * _sigmoid(c), w_ref[...]) + b_ref[...]
    o_ref[...] = o

    @pl.when(pl.program_id(0) >= N_MOD // 2)
    def _():
        n_rest = o.shape[0] - n_first
        rows_ref[0, 0] = jnp.tile(o[n_first:], (block_rows // n_rest, 1))
        rows_ref[0, 1] = jnp.tile(o[:n_first], (block_rows // n_first, 1))


def _adaln(c_all, n_first, block_rows, w_ada, b_ada):
    n = c_all.shape[0]
    half = N_MOD // 2
    return pl.pallas_call(
        functools.partial(_ada_kernel, n_first, block_rows),
        grid=(N_MOD,),
        in_specs=[
            pl.BlockSpec((n, D_MODEL), lambda j: (0, 0)),
            pl.BlockSpec((D_MODEL, D_MODEL), lambda j: (0, j)),
            pl.BlockSpec((1, D_MODEL), lambda j: (0, j)),
        ],
        out_specs=[pl.BlockSpec((n, D_MODEL), lambda j: (0, j)),
                   pl.BlockSpec((1, 2, block_rows, D_MODEL), lambda j: (jnp.maximum(j - half, 0), 0, 0, 0))],
        out_shape=[jax.ShapeDtypeStruct((n, N_MOD * D_MODEL), F32),
                   jax.ShapeDtypeStruct((half, 2, block_rows, D_MODEL), F32)],
        compiler_params=pltpu.CompilerParams(
            dimension_semantics=("arbitrary",), vmem_limit_bytes=VMEM_LIMIT),
        name="adaln",
    )(c_all, w_ada, b_ada.reshape(1, -1))


def _rows_to_time_major(x_ref, slab, xt_scr):
    nb, steps, d = x_ref.shape
    pitch = slab.shape[1] // nb
    for b in range(nb):
        for j in range(d // LANES):
            slab[j, b * pitch:b * pitch + steps, :] = x_ref[b, :, j * LANES:(j + 1) * LANES]
    for l in range(steps):
        for j in range(d // LANES):
            xt_scr[l * nb:(l + 1) * nb, j * LANES:(j + 1) * LANES] = slab[j, pl.ds(l, nb, stride=pitch), :]


def _group_diagonal(w_t, k):
    tiled = jnp.tile(w_t[:, k * HALF_STATE:(k + 1) * HALF_STATE], (SSM_GROUPS // 2, 1))
    row_group = lax.shift_right_logical(lax.broadcasted_iota(jnp.int32, tiled.shape, 0),
                                        SSM_GROUP.bit_length() - 1)
    lane_group = lax.shift_right_logical(lax.broadcasted_iota(jnp.int32, tiled.shape, 1),
                                         SSM_STATE.bit_length() - 1)
    return jnp.where(row_group == lane_group, tiled, 0.0).astype(BF16)


N_MIXER_INPUTS = 22


def _mixer_kernel(nb, steps, batch_major, has_dest, nchunks, *refs):
    refs = refs[1:] if has_dest else refs
    x1_ref = refs[N_MIXER_INPUTS]

    @pl.when(pl.program_id(0) < nchunks)
    def _():
        _mixer_step(nb, steps, batch_major, *refs)

    @pl.when(pl.program_id(0) >= nchunks)
    def _():
        x1_ref[...] = jnp.zeros_like(x1_ref)


def _mixer_step(nb, steps, batch_major, *refs):
    (x_ref, sh_ref, sc_ref, gt_ref, h0r_ref, h0i_ref, cb0_ref,
     n1g_ref, win_ref, cw_ref, ar_ref, ai_ref, bbr_ref, bbi_ref, crt_ref, cit_ref,
     dsk_ref, wglu_ref, bglu_ref, gcv_ref, gsm_ref, wout_ref,
     x1_ref, cbo_ref, hro_ref, hio_ref,
     z_scr, cv_scr, s_scr, hr_scr, hi_scr, win_b, wglu_b, wout_b, wb_ref, wcr_ref, wci_ref,
     *tm_scr) = refs
    assert refs[N_MIXER_INPUTS] is x1_ref
    rows = nb * steps
    i = pl.program_id(0)

    @pl.when(i == 0)
    def _():
        hr_scr[...] = h0r_ref[...]
        hi_scr[...] = h0i_ref[...]
        cv_scr[0:2 * nb, :] = cb0_ref[...]
        win_b[...] = win_ref[...].astype(BF16)
        wglu_b[...] = wglu_ref[...].astype(BF16)
        wout_b[...] = wout_ref[...].astype(BF16)
        for k in range(2):
            wb_ref[k, :, 0:HALF_STATE] = _group_diagonal(bbr_ref[...], k)
            wb_ref[k, :, HALF_STATE:2 * HALF_STATE] = _group_diagonal(bbi_ref[...], k)
            wcr_ref[k] = _group_diagonal(crt_ref[...], k)
            wci_ref[k] = _group_diagonal(cit_ref[...], k)

    if batch_major:
        slab, x_tm = tm_scr
        _rows_to_time_major(x_ref, slab, x_tm)
    else:
        x_tm = x_ref
    h = _per_seq(_rms(x_tm[...], n1g_ref[...]), nb, 1.0 + sc_ref[...], sh_ref[...])
    z_scr[...] = _bdot(h, win_b[...])

    u_cols = 3 * CONV_DIM
    for k in range(2):
        s_scr[:, 2 * k * HALF_STATE:2 * (k + 1) * HALF_STATE] = _bdot(
            z_scr[:, u_cols + k * HALF_SSM:u_cols + (k + 1) * HALF_SSM], wb_ref[k])

    def scan_cols(cg):
        cols = slice(cg * SCAN_COLS, (cg + 1) * SCAN_COLS)
        half, off = divmod(cg * SCAN_COLS, HALF_STATE)
        re_cols = slice(2 * half * HALF_STATE + off, 2 * half * HALF_STATE + off + SCAN_COLS)
        im_cols = slice(re_cols.start + HALF_STATE, re_cols.stop + HALF_STATE)
        a_r = jnp.broadcast_to(ar_ref[:, cols], (SUBLANES, SCAN_COLS))
        a_i = jnp.broadcast_to(ai_ref[:, cols], (SUBLANES, SCAN_COLS))

        def advance(hr, hi, rows_l):
            nr = a_r * hr - a_i * hi + s_scr[rows_l, re_cols]
            ni = a_r * hi + a_i * hr + s_scr[rows_l, im_cols]
            s_scr[rows_l, re_cols] = nr
            s_scr[rows_l, im_cols] = ni
            return nr, ni

        for r0 in range(0, nb, SUBLANES):
            tile = slice(r0, r0 + SUBLANES)
            hr, hi = hr_scr[tile, cols], hi_scr[tile, cols]
            for l in range(steps):
                hr, hi = advance(hr, hi, slice(l * nb + r0, l * nb + r0 + SUBLANES))
            hr_scr[tile, cols] = hr
            hi_scr[tile, cols] = hi

    y_halves = []
    for k in range(2):
        for cg in range(k * HALF_STATE // SCAN_COLS, (k + 1) * HALF_STATE // SCAN_COLS):
            scan_cols(cg)
        s_re = s_scr[:, 2 * k * HALF_STATE:(2 * k + 1) * HALF_STATE].astype(BF16)
        s_im = s_scr[:, (2 * k + 1) * HALF_STATE:(2 * k + 2) * HALF_STATE].astype(BF16)
        y_halves.append(lax.dot_general(s_re, wcr_ref[k], _NT, preferred_element_type=F32)
                        - lax.dot_general(s_im, wci_ref[k], _NT, preferred_element_type=F32))

    hro_ref[...] = hr_scr[...]
    hio_ref[...] = hi_scr[...]

    y = jnp.concatenate(y_halves, axis=-1) + dsk_ref[...] * z_scr[:, u_cols:IN_DIM]
    g = _gelu_tanh(y)
    y_ssm = _rms(g * _sigmoid(_bdot(g, wglu_b[...]) + bglu_ref[...]), gsm_ref[...])

    cv_scr[2 * nb:2 * nb + rows, :] = z_scr[:, CONV_DIM:2 * CONV_DIM] * z_scr[:, 2 * CONV_DIM:3 * CONV_DIM]
    conv = (cv_scr[0:rows, :] * cw_ref[0:1, :]
            + cv_scr[nb:nb + rows, :] * cw_ref[1:2, :]
            + cv_scr[2 * nb:2 * nb + rows, :] * cw_ref[2:3, :])
    y_conv = _rms(z_scr[:, 0:CONV_DIM] * conv, gcv_ref[...])
    carry = cv_scr[rows:rows + 2 * nb, :]
    cv_scr[0:2 * nb, :] = carry
    cbo_ref[...] = carry

    mixed = _bdot(y_conv, wout_b[0:CONV_DIM, :]) + _bdot(y_ssm, wout_b[CONV_DIM:2 * CONV_DIM, :])
    x1_ref[...] = x_tm[...] + _per_seq(mixed, nb, gt_ref[...])


def _const_spec(shape):
    return pl.BlockSpec(shape, lambda i: (0,) * len(shape), pipeline_mode=pl.Buffered(1))


def _mixer(x, nb, steps_total, chunk, mod, mod_row0, h0r, h0i, cb0, weights, x1_rows=None, x1_dest=None):
    rows = nb * chunk
    nchunks = steps_total // chunk
    batch_major = x.ndim == 3
    n_rows = steps_total * nb
    if x1_dest is not None:
        x1_rows = x1_dest.shape[0]
    elif x1_rows is None:
        x1_rows = n_rows
    first_block = (x1_rows - n_rows) // rows if x1_dest is not None else 0
    n_steps = nchunks if x1_dest is not None else x1_rows // rows
    last = nchunks - 1
    if batch_major:
        assert nb == SUBLANES
        x_spec = pl.BlockSpec((nb, chunk, D_MODEL), lambda i: (0, jnp.minimum(i, last), 0))
        tm_scratch = [pltpu.VMEM((D_MODEL // LANES, nb * (chunk + STRIDE_PAD), LANES), F32),
                      pltpu.VMEM((rows, D_MODEL), F32)]
    else:
        x_spec = pl.BlockSpec((rows, D_MODEL), lambda i: (jnp.minimum(i, last), 0))
        tm_scratch = []
    assert mod_row0 % nb == 0

    def mod_spec(k):
        return pl.BlockSpec((nb, D_MODEL), lambda i: (mod_row0 // nb, k), pipeline_mode=pl.Buffered(1))

    in_specs = [
        x_spec, mod_spec(0), mod_spec(1), mod_spec(2),
        _const_spec((nb, N_STATE)), _const_spec((nb, N_STATE)), _const_spec((2 * nb, CONV_DIM)),
    ] + [_const_spec(w.shape) for w in weights]
    out_specs = [
        pl.BlockSpec((rows, D_MODEL), lambda i: (first_block + i, 0)),
        _const_spec((2 * nb, CONV_DIM)), _const_spec((nb, N_STATE)), _const_spec((nb, N_STATE)),
    ]
    out_shape = [
        jax.ShapeDtypeStruct((x1_rows, D_MODEL), F32),
        jax.ShapeDtypeStruct((2 * nb, CONV_DIM), F32),
        jax.ShapeDtypeStruct((nb, N_STATE), F32),
        jax.ShapeDtypeStruct((nb, N_STATE), F32),
    ]
    scratch = [
        pltpu.VMEM((rows, IN_DIM), F32),
        pltpu.VMEM((rows + 2 * nb, CONV_DIM), F32),
        pltpu.VMEM((rows, 2 * N_STATE), F32),
        pltpu.VMEM((nb, N_STATE), F32),
        pltpu.VMEM((nb, N_STATE), F32),
        pltpu.VMEM((D_MODEL, IN_DIM), BF16),
        pltpu.VMEM((SSM_DIM, SSM_DIM), BF16),
        pltpu.VMEM((D_MODEL, D_MODEL), BF16),
        pltpu.VMEM((2, HALF_SSM, 2 * HALF_STATE), BF16),
        pltpu.VMEM((2, HALF_SSM, HALF_STATE), BF16),
        pltpu.VMEM((2, HALF_SSM, HALF_STATE), BF16),
    ] + tm_scratch
    operands = [x, mod, mod, mod, h0r, h0i, cb0, *weights]
    if x1_dest is not None:
        in_specs = [pl.BlockSpec(memory_space=pl.ANY)] + in_specs
        operands = [x1_dest] + operands
    return pl.pallas_call(
        functools.partial(_mixer_kernel, nb, chunk, batch_major, x1_dest is not None, nchunks),
        grid=(n_steps,),
        in_specs=in_specs, out_specs=out_specs, out_shape=out_shape,
        scratch_shapes=scratch,
        input_output_aliases={0: 0} if x1_dest is not None else {},
        compiler_params=pltpu.CompilerParams(
            dimension_semantics=("arbitrary",), vmem_limit_bytes=VMEM_LIMIT),
        name="mixer",
    )(*operands)


TOK_BLOCK = 512
SEG_ALIGN = 16
ROW_TILE = 512
CHUNKS_PER_TILE = ROW_TILE // SEG_ALIGN
SCATTER_GROUPS = 16
GATHER_BUFFERS = 3
TABLE_CHUNK_ROWS = 64
N_SLOTS = -(-(2 * TOK_BLOCK + N_EXPERTS * (SEG_ALIGN - 1)) // 256) * 256
SPARE_ROW = N_SLOTS - SEG_ALIGN
assert SPARE_ROW >= 2 * TOK_BLOCK + N_EXPERTS * (SEG_ALIGN - 1)
SLOTS_ALWAYS = N_SLOTS - 12 * SEG_ALIGN
ROUTER_ROWS = 48
META_ROWS = SUBLANES

_NT = (((1,), (1,)), ((), ()))
_TN = (((0,), (0,)), ((), ()))


def _iota_f32(shape, dim):
    return lax.broadcasted_iota(jnp.int32, shape, dim).astype(F32)


def _split_bf16(v):
    hi = v.astype(BF16)
    return hi, (v - hi.astype(F32)).astype(BF16)


def _route_t(lg):
    row = _iota_f32(lg.shape, 0)
    neg = jnp.float32(-jnp.inf)
    none = float(ROUTER_ROWS)
    is_g = row < N_EXPERT_GROUPS
    gl = jnp.where(is_g, lg, neg)
    gmax = jnp.max(gl, axis=0, keepdims=True)
    gsum = jnp.sum(jnp.where(is_g, jnp.exp(gl - gmax), 0.0), axis=0, keepdims=True)
    g_prob = 1.0 / gsum
    g_idx = jnp.min(jnp.where(gl == gmax, row, none), axis=0, keepdims=True)
    lo = N_EXPERT_GROUPS + EXPERTS_PER_GROUP * g_idx
    el = jnp.where(row >= lo, jnp.where(row < lo + EXPERTS_PER_GROUP, lg, neg), neg)
    m1 = jnp.max(el, axis=0, keepdims=True)
    i1 = jnp.min(jnp.where(el == m1, row, none), axis=0, keepdims=True)
    el2 = jnp.where(row == i1, neg, el)
    m2 = jnp.max(el2, axis=0, keepdims=True)
    i2 = jnp.min(jnp.where(el2 == m2, row, none), axis=0, keepdims=True)
    e21 = jnp.exp(m2 - m1)
    return i1, i2, g_prob / (1.0 + e21), g_prob * e21 / (1.0 + e21)


def _dispatch_kernel(x1_ref, sc_ref, sh_ref, n2g_ref, wr_ref, br_ref, xs_ref, meta_ref, cnt_ref):
    h = _rms(x1_ref[...], n2g_ref[...]) * (1.0 + sc_ref[0, 0]) + sh_ref[0, 0]
    t = h.shape[0]
    hb, hl = _split_bf16(h)
    wh, wl = _split_bf16(wr_ref[...])
    lg = (lax.dot_general(wh, hb, _NT, preferred_element_type=F32)
          + lax.dot_general(wh, hl, _NT, preferred_element_type=F32)
          + lax.dot_general(wl, hb, _NT, preferred_element_type=F32)) + br_ref[...]
    i1, i2, w1, w2 = _route_t(lg)

    row = _iota_f32((ROUTER_ROWS, t), 0)
    hit1 = row == i1
    hit2 = row == i2
    onehot = jnp.where(hit1, 1.0, 0.0) + jnp.where(hit2, 1.0, 0.0)
    before = jnp.where(_iota_f32((t, t), 0) < _iota_f32((t, t), 1), 1.0, 0.0)
    rank = _bdot(onehot, before)
    cnt = jnp.sum(onehot, axis=1, keepdims=True)
    seg = jnp.floor((cnt + (SEG_ALIGN - 1)) * (1.0 / SEG_ALIGN)) * SEG_ALIGN
    earlier = jnp.where(_iota_f32((ROUTER_ROWS, ROUTER_ROWS), 1) < _iota_f32((ROUTER_ROWS, ROUTER_ROWS), 0),
                        1.0, 0.0)
    base = rank + _bdot(earlier, jnp.broadcast_to(seg, (ROUTER_ROWS, t)))
    slot1 = jnp.sum(jnp.where(hit1, base, 0.0), axis=0, keepdims=True)
    slot2 = jnp.sum(jnp.where(hit2, base, 0.0), axis=0, keepdims=True)
    def sorted_rows(first, n):
        j = _iota_f32((n, t), 0) + float(first)
        place = jnp.where(j == slot1, 1.0, 0.0) + jnp.where(j == slot2, 1.0, 0.0)
        return jnp.dot(place.astype(BF16), hb, preferred_element_type=F32).astype(BF16)

    xs_ref[0:SLOTS_ALWAYS, :] = sorted_rows(0, SLOTS_ALWAYS)
    used = jnp.sum(seg)

    @pl.when(used > SLOTS_ALWAYS)
    def _():
        xs_ref[SLOTS_ALWAYS:N_SLOTS, :] = sorted_rows(SLOTS_ALWAYS, N_SLOTS - SLOTS_ALWAYS)

    @pl.when(used <= SLOTS_ALWAYS)
    def _():
        xs_ref[SLOTS_ALWAYS:N_SLOTS, :] = jnp.zeros((N_SLOTS - SLOTS_ALWAYS, D_MODEL), BF16)

    r = _iota_f32((META_ROWS, t), 0)
    meta_ref[0] = jnp.where(r == 0.0, slot1, jnp.where(r == 1.0, slot2,
                            jnp.where(r == 2.0, w1, jnp.where(r == 3.0, w2, 0.0))))
    cnt_ref[0] = jnp.broadcast_to(cnt, (ROUTER_ROWS, LANES))


def _combine_kernel(nbp, x1_ref, gt_ref, meta_ref, ys_ref, fg_ref, yp_ref, ysm_ref, slab):
    i = pl.program_id(0)
    x1 = x1_ref[...]
    meta = meta_ref[0]
    slot1, slot2, w1, w2 = (meta[k:k + 1, :] for k in range(4))
    j = _iota_f32((N_SLOTS, x1.shape[0]), 0)
    pick = jnp.where(j == slot1, w1, 0.0) + jnp.where(j == slot2, w2, 0.0)
    moe = lax.dot_general(pick.astype(BF16), ys_ref[...], _TN, preferred_element_type=F32)
    y = _rms(x1 + gt_ref[0, 0] * moe, fg_ref[...])

    @pl.when(i < nbp)
    def _():
        nb, steps, d = yp_ref.shape
        pitch = nb + STRIDE_PAD
        for j in range(d // LANES):
            for l in range(steps):
                slab[j, l * pitch:l * pitch + nb, :] = y[l * nb:(l + 1) * nb, j * LANES:(j + 1) * LANES]
        for b in range(nb):
            for j in range(d // LANES):
                yp_ref[b, :, j * LANES:(j + 1) * LANES] = slab[j, pl.ds(b, steps, stride=pitch), :]

    @pl.when(i >= nbp)
    def _():
        ysm_ref[...] = y


def _chunk_table_kernel(start_ref, m_ref, src_ref, out_ref):
    start = start_ref[...]
    end = start + m_ref[...]
    offset = src_ref[...] - SEG_ALIGN * start
    total = jnp.max(end, axis=1, keepdims=True)

    def body(b, _):
        rows = pl.ds(pl.multiple_of(b * TABLE_CHUNK_ROWS, TABLE_CHUNK_ROWS), TABLE_CHUNK_ROWS)
        first = lax.convert_element_type(b * TABLE_CHUNK_ROWS, F32)
        i = _iota_f32((TABLE_CHUNK_ROWS, start.shape[1]), 0) + first
        found = jnp.sum(jnp.where(i >= start, jnp.where(i < end, offset, 0.0), 0.0), axis=1, keepdims=True)
        i_col = _iota_f32((TABLE_CHUNK_ROWS, 1), 0) + first
        out_ref[rows, :] = (found + jnp.where(i_col < total, SEG_ALIGN * i_col, 0.0)).astype(jnp.int32)
        return 0

    lax.fori_loop(0, out_ref.shape[0] // TABLE_CHUNK_ROWS, body, 0)


def _expert_kernel(te_ref, tnv_ref, tc0_ref, ctab_ref, nt_ref, xs_hbm, wg_ref, wu_ref, wd_ref,
                   ys_hbm, xbuf, ybuf, wgub, wdb, in_sem, out_sem):
    t = pl.program_id(0)
    nt = nt_ref[0]
    slot = t % 2
    xslot = t % GATHER_BUFFERS

    def chunk_rows(j):
        return pl.ds(pl.multiple_of(j * SEG_ALIGN, SEG_ALIGN), SEG_ALIGN)

    def hbm_rows(ref, r):
        return ref.at[pl.ds(pl.multiple_of(r, SEG_ALIGN), SEG_ALIGN), :]

    def in_copy(r, j, s):
        return pltpu.make_async_copy(hbm_rows(xs_hbm, r), xbuf.at[s, chunk_rows(j), :], in_sem.at[s])

    def out_copy(r, j, s):
        return pltpu.make_async_copy(ybuf.at[s, chunk_rows(j), :], hbm_rows(ys_hbm, r), out_sem.at[s])

    def start_gather(tile, s):
        c0 = tc0_ref[tile]
        nv = tnv_ref[tile]
        for j in range(CHUNKS_PER_TILE):
            in_copy(jnp.where(j < nv, ctab_ref[c0 + j], SPARE_ROW), j, s).start()

    def wait_gather(s):
        pltpu.make_async_copy(xs_hbm.at[pl.ds(0, ROW_TILE), :], xbuf.at[s], in_sem.at[s]).wait()

    def wait_scatter(tile, s):
        nv = tnv_ref[tile]

        @pl.when(nv == CHUNKS_PER_TILE)
        def _():
            pltpu.make_async_copy(ybuf.at[s], ys_hbm.at[pl.ds(0, ROW_TILE), :], out_sem.at[s]).wait()

        @pl.when(nv < CHUNKS_PER_TILE)
        def _():
            def body(j, _):
                out_copy(0, j, s).wait()
                return 0

            lax.fori_loop(0, nv, body, 0)

    def run_tile(full):
        ahead = t + GATHER_BUFFERS - 1
        start_gather(ahead, ahead % GATHER_BUFFERS)
        wait_gather(xslot)
        ab = jnp.dot(xbuf[xslot], wgub[...], preferred_element_type=F32)
        a, b = ab[:, :D_EXPERT], ab[:, D_EXPERT:]
        o = jnp.dot((a * _sigmoid(a) * b).astype(BF16), wdb[...], preferred_element_type=F32)
        c0 = tc0_ref[t]
        if full:
            group = ROW_TILE // SCATTER_GROUPS
            for r in range(0, ROW_TILE, group):
                ybuf[slot, r:r + group, :] = o[r:r + group].astype(BF16)
                for j in range(r // SEG_ALIGN, (r + group) // SEG_ALIGN):
                    out_copy(ctab_ref[c0 + j], j, slot).start()
        else:
            ybuf[slot] = o.astype(BF16)

            def body(j, _):
                out_copy(ctab_ref[c0 + j], j, slot).start()
                return 0

            lax.fori_loop(0, tnv_ref[t], body, 0)

    @pl.when(t == 0)
    def _():
        for k in range(GATHER_BUFFERS - 1):
            start_gather(k, k)

    @pl.when(jnp.logical_or(t == 0, te_ref[t] != te_ref[jnp.maximum(t - 1, 0)]))
    def _():
        wgub[:, :D_EXPERT] = wg_ref[0].astype(BF16)
        wgub[:, D_EXPERT:] = wu_ref[0].astype(BF16)
        wdb[...] = wd_ref[0].astype(BF16)

    @pl.when(t < nt)
    def _():
        @pl.when(t >= 2)
        def _():
            wait_scatter(t - 2, slot)

        pl.when(tnv_ref[t] == CHUNKS_PER_TILE)(lambda: run_tile(True))
        pl.when(tnv_ref[t] < CHUNKS_PER_TILE)(lambda: run_tile(False))

        @pl.when(t == nt - 1)
        def _():
            for k in range(1, GATHER_BUFFERS):
                wait_gather((t + k) % GATHER_BUFFERS)
            wait_scatter(t, slot)

            @pl.when(t >= 1)
            def _():
                wait_scatter(t - 1, 1 - slot)


def _tile_plan(cnt):
    nblk = cnt.shape[0]
    max_chunks = nblk * (2 * TOK_BLOCK // SEG_ALIGN + N_EXPERTS)
    max_tiles = max_chunks // CHUNKS_PER_TILE + N_EXPERTS
    table_len = -(-(max_chunks + CHUNKS_PER_TILE) // LANES) * LANES
    m = (cnt + SEG_ALIGN - 1) // SEG_ALIGN
    seg_start = jnp.cumsum(m, axis=1) - m
    src0 = jnp.arange(nblk, dtype=jnp.int32)[:, None] * N_SLOTS + SEG_ALIGN * seg_start
    seg_m = m.T.reshape(1, -1)
    seg_first = jnp.cumsum(seg_m, axis=1) - seg_m
    assert table_len % TABLE_CHUNK_ROWS == 0

    def row(v):
        return jnp.pad(v.astype(F32), ((0, 0), (0, -v.shape[1] % LANES)))

    chunk_src = pl.pallas_call(
        _chunk_table_kernel,
        out_shape=jax.ShapeDtypeStruct((table_len, 1), jnp.int32),
        name="moe_chunk_table",
    )(row(seg_first), row(seg_m), row(src0.T.reshape(1, -1))).reshape(-1)
    ce = jnp.sum(m, axis=0)
    c_start = jnp.cumsum(ce) - ce
    tiles = (ce + CHUNKS_PER_TILE - 1) // CHUNKS_PER_TILE
    t_cum = jnp.cumsum(tiles)
    nt = t_cum[-1]
    k = jnp.arange(max_tiles + GATHER_BUFFERS - 1, dtype=jnp.int32)
    kk = jnp.minimum(k, nt - 1)
    ek = jnp.minimum(jnp.sum((t_cum[None, :] <= kk[:, None]).astype(jnp.int32), axis=1), N_EXPERTS - 1)
    is_e = ek[:, None] == jnp.arange(N_EXPERTS, dtype=jnp.int32)[None, :]
    pick = lambda v: jnp.sum(jnp.where(is_e, v[None, :], 0), axis=1)
    local = kk - pick(t_cum - tiles)
    tile_nv = jnp.where(k < nt, jnp.clip(pick(ce) - CHUNKS_PER_TILE * local, 0, CHUNKS_PER_TILE), 0)
    tile_c0 = jnp.where(k < nt, pick(c_start) + CHUNKS_PER_TILE * local, 0)
    return tuple(v.astype(jnp.int32) for v in (ek, tile_nv, tile_c0, chunk_src, nt.reshape(1)))


def _moe(x1, bp, mod_rows, n2g, wr, br, wg, wu, wd, fg):
    nblk = x1.shape[0] // TOK_BLOCK
    nbp = nblk - 1
    blk = (TOK_BLOCK, D_MODEL)
    x1_spec = pl.BlockSpec(blk, lambda i: (i, 0))

    def mod_spec(k):
        return pl.BlockSpec((1, 1) + blk, lambda i: (k, jnp.minimum(i // nbp, 1), 0, 0))

    sh_spec, sc_spec, gt_spec = mod_spec(0), mod_spec(1), mod_spec(2)
    meta_spec = pl.BlockSpec((1, META_ROWS, TOK_BLOCK), lambda i: (i, 0, 0))
    slots_spec = pl.BlockSpec((N_SLOTS, D_MODEL), lambda i: (i, 0))
    params = pltpu.CompilerParams(dimension_semantics=("arbitrary",), vmem_limit_bytes=VMEM_LIMIT)

    def cs(shape):
        return pl.BlockSpec(shape, lambda i: (0,) * len(shape))

    xs, meta, cnt = pl.pallas_call(
        _dispatch_kernel,
        grid=(nblk,),
        in_specs=[x1_spec, sc_spec, sh_spec, cs((1, D_MODEL)),
                  cs((ROUTER_ROWS, D_MODEL)), cs((ROUTER_ROWS, TOK_BLOCK))],
        out_specs=[slots_spec, meta_spec,
                   pl.BlockSpec((1, ROUTER_ROWS, LANES), lambda i: (i, 0, 0))],
        out_shape=[jax.ShapeDtypeStruct((nblk * N_SLOTS, D_MODEL), BF16),
                   jax.ShapeDtypeStruct((nblk, META_ROWS, TOK_BLOCK), F32),
                   jax.ShapeDtypeStruct((nblk, ROUTER_ROWS, LANES), F32)],
        compiler_params=params,
        name="moe_dispatch",
    )(x1, mod_rows, mod_rows, n2g, wr, br)

    plan = _tile_plan(cnt[:, N_EXPERT_GROUPS:N_EXPERT_GROUPS + N_EXPERTS, 0].astype(jnp.int32))
    max_tiles = plan[0].shape[0] - (GATHER_BUFFERS - 1)

    def w_spec(shape):
        return pl.BlockSpec((1,) + shape, lambda t, te, *_: (te[t], 0, 0))

    ys = pl.pallas_call(
        _expert_kernel,
        grid_spec=pltpu.PrefetchScalarGridSpec(
            num_scalar_prefetch=len(plan),
            grid=(max_tiles,),
            in_specs=[pl.BlockSpec(memory_space=pl.ANY),
                      w_spec((D_MODEL, D_EXPERT)), w_spec((D_MODEL, D_EXPERT)), w_spec((D_EXPERT, D_MODEL))],
            out_specs=pl.BlockSpec(memory_space=pl.ANY),
            scratch_shapes=[pltpu.VMEM((GATHER_BUFFERS, ROW_TILE, D_MODEL), BF16),
                            pltpu.VMEM((2, ROW_TILE, D_MODEL), BF16),
                            pltpu.VMEM((D_MODEL, 2 * D_EXPERT), BF16),
                            pltpu.VMEM((D_EXPERT, D_MODEL), BF16),
                            pltpu.SemaphoreType.DMA((GATHER_BUFFERS,)), pltpu.SemaphoreType.DMA((2,))]),
        out_shape=jax.ShapeDtypeStruct(xs.shape, BF16),
        input_output_aliases={len(plan): 0},
        compiler_params=params,
        name="moe_experts",
    )(*plan, xs, wg, wu, wd)

    return pl.pallas_call(
        functools.partial(_combine_kernel, nbp),
        grid=(nblk,),
        in_specs=[x1_spec, gt_spec, meta_spec, slots_spec, cs((1, D_MODEL))],
        out_specs=[pl.BlockSpec((bp, TOK_BLOCK // bp, D_MODEL), lambda i: (0, jnp.minimum(i, nbp - 1), 0)),
                   pl.BlockSpec(blk, lambda i: (0, 0))],
        out_shape=[jax.ShapeDtypeStruct((bp, nbp * TOK_BLOCK // bp, D_MODEL), F32),
                   jax.ShapeDtypeStruct(blk, F32)],
        scratch_shapes=[pltpu.VMEM((D_MODEL // LANES, TOK_BLOCK // bp * (bp + STRIDE_PAD), LANES), F32)],
        compiler_params=params,
        name="moe_combine",
    )(x1, mod_rows, meta, ys, fg)


def _to_time_major(x):
    b, s, d = x.shape
    return x.transpose(1, 0, 2).reshape(s * b, d)


def _from_time_major(x, b, s):
    return x.reshape(s, b, -1).transpose(1, 0, 2)


def kernel(x_prompt, x_sample, state_conv, state_ssm_re, state_ssm_im, c_prompt, c_sample, w_ada, b_ada, norm1_g, norm2_g, w_in, conv_w, lambda_re, lambda_im, log_dt, ssm_b_re, ssm_b_im, ssm_c_re, ssm_c_im, ssm_d, w_glu, b_glu, out_norm_conv_g, out_norm_ssm_g, w_out, w_router_group, b_router_group, w_router_expert, b_router_expert, w_expert_gate, w_expert_up, w_expert_down, final_norm_g):
    assert w_ada.shape[0] == 1, "single-layer step"
    bp, sp, _ = x_prompt.shape
    bs, ss, _ = x_sample.shape

    ar, ai, bbr, bbi = _discretise(lambda_re[0], lambda_im[0], log_dt[0], ssm_b_re[0], ssm_b_im[0])
    crt = ssm_c_re[0].transpose(1, 0, 2).reshape(SSM_GROUP, N_STATE)
    cit = ssm_c_im[0].transpose(1, 0, 2).reshape(SSM_GROUP, N_STATE)

    assert bs % bp == 0
    mod, mod_rows = _adaln(jnp.concatenate([c_sample, c_prompt], axis=0), bs, TOK_BLOCK, w_ada[0], b_ada[0])

    mixer_weights = (
        norm1_g[0].reshape(1, -1), w_in[0], conv_w[0], ar, ai, bbr, bbi, crt, cit,
        ssm_d[0].reshape(1, -1), w_glu[0], b_glu[0].reshape(1, -1),
        out_norm_conv_g[0].reshape(1, -1), out_norm_ssm_g[0].reshape(1, -1), w_out[0],
    )

    pad_rows = ROUTER_ROWS - N_EXPERT_GROUPS - N_EXPERTS
    w_router = jnp.concatenate(
        [w_router_group[0].T, w_router_expert[0].transpose(0, 2, 1).reshape(N_EXPERTS, D_MODEL),
         jnp.zeros((pad_rows, D_MODEL), F32)], axis=0)
    b_router = jnp.concatenate(
        [b_router_group[0], b_router_expert[0].reshape(-1), jnp.zeros((pad_rows,), F32)])
    b_router = jnp.broadcast_to(b_router[:, None], (ROUTER_ROWS, TOK_BLOCK))
    moe_weights = (norm2_g[0].reshape(1, -1), w_router, b_router,
                   w_expert_gate[0], w_expert_up[0], w_expert_down[0], final_norm_g.reshape(1, -1))

    assert bs * ss == TOK_BLOCK and (bp * sp) % TOK_BLOCK == 0 and TOK_BLOCK % bp == 0

    def mix(x, mod_row0, h0r, h0i, cb0, chunk, in_kernel_transpose, **x1_where):
        nb, s, _ = x.shape
        x1, cbo, hro, hio = _mixer(
            x if in_kernel_transpose else _to_time_major(x), nb, s, chunk,
            mod, mod_row0, h0r, h0i, cb0, mixer_weights, **x1_where)
        new_conv = cbo.reshape(2, nb, CONV_DIM).transpose(1, 0, 2)[None]
        new_re = hro.reshape(1, nb, SSM_GROUPS, SSM_STATE)
        new_im = hio.reshape(1, nb, SSM_GROUPS, SSM_STATE)
        return x1, new_conv, new_re, new_im

    zero_state = jnp.zeros((bp, N_STATE), F32)
    zero_conv = jnp.zeros((2 * bp, CONV_DIM), F32)
    x1, conv_p, re_p, im_p = mix(x_prompt, bs, zero_state, zero_state, zero_conv, TOK_BLOCK // bp, True,
                                 x1_rows=bp * sp + bs * ss)
    x1, conv_s, re_s, im_s = mix(
        x_sample, 0,
        state_ssm_re[0].reshape(bs, N_STATE), state_ssm_im[0].reshape(bs, N_STATE),
        state_conv[0].transpose(1, 0, 2).reshape(2 * bs, CONV_DIM), ss, False, x1_dest=x1)

    y_p, y_s = _moe(x1, bp, mod_rows, *moe_weights)
    return (y_p, _from_time_major(y_s, bs, ss), conv_p, re_p, im_p, conv_s, re_s, im_s)
```

```python
import functools

import jax
import jax.numpy as jnp
from jax import lax
from jax.experimental import pallas as pl
from jax.experimental.pallas import tpu as pltpu

D_MODEL = 1024
CONV_DIM = 512
SSM_DIM = 512
SSM_GROUP = 16
SSM_GROUPS = 32
SSM_STATE = 64
N_STATE = SSM_GROUPS * SSM_STATE
HALF_SSM = SSM_DIM // 2
HALF_STATE = N_STATE // 2
IN_DIM = 2048
N_EXPERT_GROUPS = 4
EXPERTS_PER_GROUP = 8
N_EXPERTS = 32
D_EXPERT = 256
N_MOD = 6
EPS = 1e-6

SUBLANES = 8
LANES = 128
SCAN_COLS = 512
STRIDE_PAD = 4
VMEM_LIMIT = 60 * 1024 * 1024

F32 = jnp.float32
BF16 = jnp.bfloat16


def _sigmoid(x):
    return 1.0 / (1.0 + jnp.exp(-x))


def _gelu_tanh(x):
    return 0.5 * x * (1.0 + jnp.tanh(0.7978845608028654 * (x + 0.044715 * (x * x * x))))


def _rms(x, g):
    return x * lax.rsqrt(jnp.mean(x * x, axis=-1, keepdims=True) + EPS) * g


def _bdot(a, b):
    return jnp.dot(a.astype(BF16), b.astype(BF16), preferred_element_type=F32)


def _per_seq(v, nb, scale, shift=None):
    rows, d = v.shape
    v3 = v.reshape(rows // nb, nb, d) * scale[None]
    if shift is not None:
        v3 = v3 + shift[None]
    return v3.reshape(rows, d)


def _disc_kernel(lr_ref, li_ref, ldt_ref, brt_ref, bit_ref, ar_ref, ai_ref, bbr_ref, bbi_ref):
    lr = lr_ref[...]
    li = li_ref[...]
    dt = jnp.exp(ldt_ref[...])
    mag = jnp.exp(lr * dt)
    ar = mag * jnp.cos(li * dt)
    ai = mag * jnp.sin(li * dt)
    den = lr * lr + li * li
    fr = ((ar - 1.0) * lr + ai * li) / den
    fi = (ai * lr - (ar - 1.0) * li) / den
    ar_ref[...] = ar
    ai_ref[...] = ai
    brt = brt_ref[...]
    bit = bit_ref[...]
    bbr_ref[...] = fr * brt - fi * bit
    bbi_ref[...] = fr * bit + fi * brt


def _discretise(lambda_re, lambda_im, log_dt, b_re, b_im):
    lr = lambda_re.reshape(1, N_STATE)
    li = lambda_im.reshape(1, N_STATE)
    ldt = jnp.repeat(log_dt, SSM_STATE).reshape(1, N_STATE)
    brt = b_re.transpose(2, 0, 1).reshape(SSM_GROUP, N_STATE)
    bit = b_im.transpose(2, 0, 1).reshape(SSM_GROUP, N_STATE)
    row = jax.ShapeDtypeStruct((1, N_STATE), F32)
    mat = jax.ShapeDtypeStruct((SSM_GROUP, N_STATE), F32)
    return pl.pallas_call(_disc_kernel, out_shape=(row, row, mat, mat), name="s5_discretise")(
        lr, li, ldt, brt, bit)


def _ada_kernel(n_first, block_rows, c_ref, w_ref, b_ref, o_ref, rows_ref):
    c = c_ref[...]
    o = _bdot(c * _sigmoid(c), w_ref[...]) + b_ref[...]
    o_ref[...] = o

    @pl.when(pl.program_id(0) >= N_MOD // 2)
    def _():
        n_rest = o.shape[0] - n_first
        rows_ref[0, 0] = jnp.tile(o[n_first:], (block_rows // n_rest, 1))
        rows_ref[0, 1] = jnp.tile(o[:n_first], (block_rows // n_first, 1))


def _adaln(c_all, n_first, block_rows, w_ada, b_ada):
    n = c_all.shape[0]
    half = N_MOD // 2
    return pl.pallas_call(
        functools.partial(_ada_kernel, n_first, block_rows),
        grid=(N_MOD,),
        in_specs=[
            pl.BlockSpec((n, D_MODEL), lambda j: (0, 0)),
            pl.BlockSpec((D_MODEL, D_MODEL), lambda j: (0, j)),
            pl.BlockSpec((1, D_MODEL), lambda j: (0, j)),
        ],
        out_specs=[pl.BlockSpec((n, D_MODEL), lambda j: (0, j)),
                   pl.BlockSpec((1, 2, block_rows, D_MODEL), lambda j: (jnp.maximum(j - half, 0), 0, 0, 0))],
        out_shape=[jax.ShapeDtypeStruct((n, N_MOD * D_MODEL), F32),
                   jax.ShapeDtypeStruct((half, 2, block_rows, D_MODEL), F32)],
        compiler_params=pltpu.CompilerParams(
            dimension_semantics=("arbitrary",), vmem_limit_bytes=VMEM_LIMIT),
        name="adaln",
    )(c_all, w_ada, b_ada.reshape(1, -1))


def _rows_to_time_major(x_ref, slab, xt_scr):
    nb, steps, d = x_ref.shape
    pitch = slab.shape[1] // nb
    for b in range(nb):
        for j in range(d // LANES):
            slab[j, b * pitch:b * pitch + steps, :] = x_ref[b, :, j * LANES:(j + 1) * LANES]
    for l in range(steps):
        for j in range(d // LANES):
            xt_scr[l * nb:(l + 1) * nb, j * LANES:(j + 1) * LANES] = slab[j, pl.ds(l, nb, stride=pitch), :]


def _group_diagonal(w_t, k):
    tiled = jnp.tile(w_t[:, k * HALF_STATE:(k + 1) * HALF_STATE], (SSM_GROUPS // 2, 1))
    row_group = lax.shift_right_logical(lax.broadcasted_iota(jnp.int32, tiled.shape, 0),
                                        SSM_GROUP.bit_length() - 1)
    lane_group = lax.shift_right_logical(lax.broadcasted_iota(jnp.int32, tiled.shape, 1),
                                         SSM_STATE.bit_length() - 1)
    return jnp.where(row_group == lane_group, tiled, 0.0).astype(BF16)


N_MIXER_INPUTS = 22


def _mixer_kernel(nb, steps, batch_major, has_dest, nchunks, *refs):
    refs = refs[1:] if has_dest else refs
    x1_ref = refs[N_MIXER_INPUTS]

    @pl.when(pl.program_id(0) < nchunks)
    def _():
        _mixer_step(nb, steps, batch_major, *refs)

    @pl.when(pl.program_id(0) >= nchunks)
    def _():
        x1_ref[...] = jnp.zeros_like(x1_ref)


def _mixer_step(nb, steps, batch_major, *refs):
    (x_ref, sh_ref, sc_ref, gt_ref, h0r_ref, h0i_ref, cb0_ref,
     n1g_ref, win_ref, cw_ref, ar_ref, ai_ref, bbr_ref, bbi_ref, crt_ref, cit_ref,
     dsk_ref, wglu_ref, bglu_ref, gcv_ref, gsm_ref, wout_ref,
     x1_ref, cbo_ref, hro_ref, hio_ref,
     z_scr, cv_scr, s_scr, hr_scr, hi_scr, win_b, wglu_b, wout_b, wb_ref, wcr_ref, wci_ref,
     *tm_scr) = refs
    assert refs[N_MIXER_INPUTS] is x1_ref
    rows = nb * steps
    i = pl.program_id(0)

    @pl.when(i == 0)
    def _():
        hr_scr[...] = h0r_ref[...]
        hi_scr[...] = h0i_ref[...]
        cv_scr[0:2 * nb, :] = cb0_ref[...]
        win_b[...] = win_ref[...].astype(BF16)
        wglu_b[...] = wglu_ref[...].astype(BF16)
        wout_b[...] = wout_ref[...].astype(BF16)
        for k in range(2):
            wb_ref[k, :, 0:HALF_STATE] = _group_diagonal(bbr_ref[...], k)
            wb_ref[k, :, HALF_STATE:2 * HALF_STATE] = _group_diagonal(bbi_ref[...], k)
            wcr_ref[k] = _group_diagonal(crt_ref[...], k)
            wci_ref[k] = _group_diagonal(cit_ref[...], k)

    if batch_major:
        slab, x_tm = tm_scr
        _rows_to_time_major(x_ref, slab, x_tm)
    else:
        x_tm = x_ref
    h = _per_seq(_rms(x_tm[...], n1g_ref[...]), nb, 1.0 + sc_ref[...], sh_ref[...])
    z_scr[...] = _bdot(h, win_b[...])

    u_cols = 3 * CONV_DIM
    for k in range(2):
        s_scr[:, 2 * k * HALF_STATE:2 * (k + 1) * HALF_STATE] = _bdot(
            z_scr[:, u_cols + k * HALF_SSM:u_cols + (k + 1) * HALF_SSM], wb_ref[k])

    def scan_cols(cg):
        cols = slice(cg * SCAN_COLS, (cg + 1) * SCAN_COLS)
        half, off = divmod(cg * SCAN_COLS, HALF_STATE)
        re_cols = slice(2 * half * HALF_STATE + off, 2 * half * HALF_STATE + off + SCAN_COLS)
        im_cols = slice(re_cols.start + HALF_STATE, re_cols.stop + HALF_STATE)
        a_r = jnp.broadcast_to(ar_ref[:, cols], (SUBLANES, SCAN_COLS))
        a_i = jnp.broadcast_to(ai_ref[:, cols], (SUBLANES, SCAN_COLS))

        def advance(hr, hi, rows_l):
            nr = a_r * hr - a_i * hi + s_scr[rows_l, re_cols]
            ni = a_r * hi + a_i * hr + s_scr[rows_l, im_cols]
            s_scr[rows_l, re_cols] = nr
            s_scr[rows_l, im_cols] = ni
            return nr, ni

        for r0 in range(0, nb, SUBLANES):
            tile = slice(r0, r0 + SUBLANES)
            hr, hi = hr_scr[tile, cols], hi_scr[tile, cols]
            for l in range(steps):
                hr, hi = advance(hr, hi, slice(l * nb + r0, l * nb + r0 + SUBLANES))
            hr_scr[tile, cols] = hr
            hi_scr[tile, cols] = hi

    y_halves = []
    for k in range(2):
        for cg in range(k * HALF_STATE // SCAN_COLS, (k + 1) * HALF_STATE // SCAN_COLS):
            scan_cols(cg)
        s_re = s_scr[:, 2 * k * HALF_STATE:(2 * k + 1) * HALF_STATE].astype(BF16)
        s_im = s_scr[:, (2 * k + 1) * HALF_STATE:(2 * k + 2) * HALF_STATE].astype(BF16)
        y_halves.append(lax.dot_general(s_re, wcr_ref[k], _NT, preferred_element_type=F32)
                        - lax.dot_general(s_im, wci_ref[k], _NT, preferred_element_type=F32))

    hro_ref[...] = hr_scr[...]
    hio_ref[...] = hi_scr[...]

    y = jnp.concatenate(y_halves, axis=-1) + dsk_ref[...] * z_scr[:, u_cols:IN_DIM]
    g = _gelu_tanh(y)
    y_ssm = _rms(g * _sigmoid(_bdot(g, wglu_b[...]) + bglu_ref[...]), gsm_ref[...])

    cv_scr[2 * nb:2 * nb + rows, :] = z_scr[:, CONV_DIM:2 * CONV_DIM] * z_scr[:, 2 * CONV_DIM:3 * CONV_DIM]
    conv = (cv_scr[0:rows, :] * cw_ref[0:1, :]
            + cv_scr[nb:nb + rows, :] * cw_ref[1:2, :]
            + cv_scr[2 * nb:2 * nb + rows, :] * cw_ref[2:3, :])
    y_conv = _rms(z_scr[:, 0:CONV_DIM] * conv, gcv_ref[...])
    carry = cv_scr[rows:rows + 2 * nb, :]
    cv_scr[0:2 * nb, :] = carry
    cbo_ref[...] = carry

    mixed = _bdot(y_conv, wout_b[0:CONV_DIM, :]) + _bdot(y_ssm, wout_b[CONV_DIM:2 * CONV_DIM, :])
    x1_ref[...] = x_tm[...] + _per_seq(mixed, nb, gt_ref[...])


def _const_spec(shape):
    return pl.BlockSpec(shape, lambda i: (0,) * len(shape), pipeline_mode=pl.Buffered(1))


def _mixer(x, nb, steps_total, chunk, mod, mod_row0, h0r, h0i, cb0, weights, x1_rows=None, x1_dest=None):
    rows = nb * chunk
    nchunks = steps_total // chunk
    batch_major = x.ndim == 3
    n_rows = steps_total * nb
    if x1_dest is not None:
        x1_rows = x1_dest.shape[0]
    elif x1_rows is None:
        x1_rows = n_rows
    first_block = (x1_rows - n_rows) // rows if x1_dest is not None else 0
    n_steps = nchunks if x1_dest is not None else x1_rows // rows
    last = nchunks - 1
    if batch_major:
        assert nb == SUBLANES
        x_spec = pl.BlockSpec((nb, chunk, D_MODEL), lambda i: (0, jnp.minimum(i, last), 0))
        tm_scratch = [pltpu.VMEM((D_MODEL // LANES, nb * (chunk + STRIDE_PAD), LANES), F32),
                      pltpu.VMEM((rows, D_MODEL), F32)]
    else:
        x_spec = pl.BlockSpec((rows, D_MODEL), lambda i: (jnp.minimum(i, last), 0))
        tm_scratch = []
    assert mod_row0 % nb == 0

    def mod_spec(k):
        return pl.BlockSpec((nb, D_MODEL), lambda i: (mod_row0 // nb, k), pipeline_mode=pl.Buffered(1))

    in_specs = [
        x_spec, mod_spec(0), mod_spec(1), mod_spec(2),
        _const_spec((nb, N_STATE)), _const_spec((nb, N_STATE)), _const_spec((2 * nb, CONV_DIM)),
    ] + [_const_spec(w.shape) for w in weights]
    out_specs = [
        pl.BlockSpec((rows, D_MODEL), lambda i: (first_block + i, 0)),
        _const_spec((2 * nb, CONV_DIM)), _const_spec((nb, N_STATE)), _const_spec((nb, N_STATE)),
    ]
    out_shape = [
        jax.ShapeDtypeStruct((x1_rows, D_MODEL), F32),
        jax.ShapeDtypeStruct((2 * nb, CONV_DIM), F32),
        jax.ShapeDtypeStruct((nb, N_STATE), F32),
        jax.ShapeDtypeStruct((nb, N_STATE), F32),
    ]
    scratch = [
        pltpu.VMEM((rows, IN_DIM), F32),
        pltpu.VMEM((rows + 2 * nb, CONV_DIM), F32),
        pltpu.VMEM((rows, 2 * N_STATE), F32),
        pltpu.VMEM((nb, N_STATE), F32),
        pltpu.VMEM((nb, N_STATE), F32),
        pltpu.VMEM((D_MODEL, IN_DIM), BF16),
        pltpu.VMEM((SSM_DIM, SSM_DIM), BF16),
        pltpu.VMEM((D_MODEL, D_MODEL), BF16),
        pltpu.VMEM((2, HALF_SSM, 2 * HALF_STATE), BF16),
        pltpu.VMEM((2, HALF_SSM, HALF_STATE), BF16),
        pltpu.VMEM((2, HALF_SSM, HALF_STATE), BF16),
    ] + tm_scratch
    operands = [x, mod, mod, mod, h0r, h0i, cb0, *weights]
    if x1_dest is not None:
        in_specs = [pl.BlockSpec(memory_space=pl.ANY)] + in_specs
        operands = [x1_dest] + operands
    return pl.pallas_call(
        functools.partial(_mixer_kernel, nb, chunk, batch_major, x1_dest is not None, nchunks),
        grid=(n_steps,),
        in_specs=in_specs, out_specs=out_specs, out_shape=out_shape,
        scratch_shapes=scratch,
        input_output_aliases={0: 0} if x1_dest is not None else {},
        compiler_params=pltpu.CompilerParams(
            dimension_semantics=("arbitrary",), vmem_limit_bytes=VMEM_LIMIT),
        name="mixer",
    )(*operands)


TOK_BLOCK = 512
SEG_ALIGN = 16
ROW_TILE = 512
CHUNKS_PER_TILE = ROW_TILE // SEG_ALIGN
SCATTER_GROUPS = 16
GATHER_BUFFERS = 3
TABLE_CHUNK_ROWS = 64
N_SLOTS = -(-(2 * TOK_BLOCK + N_EXPERTS * (SEG_ALIGN - 1)) // 256) * 256
SPARE_ROW = N_SLOTS - SEG_ALIGN
assert SPARE_ROW >= 2 * TOK_BLOCK + N_EXPERTS * (SEG_ALIGN - 1)
SLOTS_ALWAYS = N_SLOTS - 12 * SEG_ALIGN
ROUTER_ROWS = 48
META_ROWS = SUBLANES

_NT = (((1,), (1,)), ((), ()))
_TN = (((0,), (0,)), ((), ()))


def _iota_f32(shape, dim):
    return lax.broadcasted_iota(jnp.int32, shape, dim).astype(F32)


def _split_bf16(v):
    hi = v.astype(BF16)
    return hi, (v - hi.astype(F32)).astype(BF16)


def _route_t(lg):
    row = _iota_f32(lg.shape, 0)
    neg = jnp.float32(-jnp.inf)
    none = float(ROUTER_ROWS)
    is_g = row < N_EXPERT_GROUPS
    gl = jnp.where(is_g, lg, neg)
    gmax = jnp.max(gl, axis=0, keepdims=True)
    gsum = jnp.sum(jnp.where(is_g, jnp.exp(gl - gmax), 0.0), axis=0, keepdims=True)
    g_prob = 1.0 / gsum
    g_idx = jnp.min(jnp.where(gl == gmax, row, none), axis=0, keepdims=True)
    lo = N_EXPERT_GROUPS + EXPERTS_PER_GROUP * g_idx
    el = jnp.where(row >= lo, jnp.where(row < lo + EXPERTS_PER_GROUP, lg, neg), neg)
    m1 = jnp.max(el, axis=0, keepdims=True)
    i1 = jnp.min(jnp.where(el == m1, row, none), axis=0, keepdims=True)
    el2 = jnp.where(row == i1, neg, el)
    m2 = jnp.max(el2, axis=0, keepdims=True)
    i2 = jnp.min(jnp.where(el2 == m2, row, none), axis=0, keepdims=True)
    e21 = jnp.exp(m2 - m1)
    return i1, i2, g_prob / (1.0 + e21), g_prob * e21 / (1.0 + e21)


def _dispatch_kernel(x1_ref, sc_ref, sh_ref, n2g_ref, wr_ref, br_ref, xs_ref, meta_ref, cnt_ref):
    h = _rms(x1_ref[...], n2g_ref[...]) * (1.0 + sc_ref[0, 0]) + sh_ref[0, 0]
    t = h.shape[0]
    hb, hl = _split_bf16(h)
    wh, wl = _split_bf16(wr_ref[...])
    lg = (lax.dot_general(wh, hb, _NT, preferred_element_type=F32)
          + lax.dot_general(wh, hl, _NT, preferred_element_type=F32)
          + lax.dot_general(wl, hb, _NT, preferred_element_type=F32)) + br_ref[...]
    i1, i2, w1, w2 = _route_t(lg)

    row = _iota_f32((ROUTER_ROWS, t), 0)
    hit1 = row == i1
    hit2 = row == i2
    onehot = jnp.where(hit1, 1.0, 0.0) + jnp.where(hit2, 1.0, 0.0)
    before = jnp.where(_iota_f32((t, t), 0) < _iota_f32((t, t), 1), 1.0, 0.0)
    rank = _bdot(onehot, before)
    cnt = jnp.sum(onehot, axis=1, keepdims=True)
    seg = jnp.floor((cnt + (SEG_ALIGN - 1)) * (1.0 / SEG_ALIGN)) * SEG_ALIGN
    earlier = jnp.where(_iota_f32((ROUTER_ROWS, ROUTER_ROWS), 1) < _iota_f32((ROUTER_ROWS, ROUTER_ROWS), 0),
                        1.0, 0.0)
    base = rank + _bdot(earlier, jnp.broadcast_to(seg, (ROUTER_ROWS, t)))
    slot1 = jnp.sum(jnp.where(hit1, base, 0.0), axis=0, keepdims=True)
    slot2 = jnp.sum(jnp.where(hit2, base, 0.0), axis=0, keepdims=True)
    def sorted_rows(first, n):
        j = _iota_f32((n, t), 0) + float(first)
        place = jnp.where(j == slot1, 1.0, 0.0) + jnp.where(j == slot2, 1.0, 0.0)
        return jnp.dot(place.astype(BF16), hb, preferred_element_type=F32).astype(BF16)

    xs_ref[0:SLOTS_ALWAYS, :] = sorted_rows(0, SLOTS_ALWAYS)
    used = jnp.sum(seg)

    @pl.when(used > SLOTS_ALWAYS)
    def _():
        xs_ref[SLOTS_ALWAYS:N_SLOTS, :] = sorted_rows(SLOTS_ALWAYS, N_SLOTS - SLOTS_ALWAYS)

    @pl.when(used <= SLOTS_ALWAYS)
    def _():
        xs_ref[SLOTS_ALWAYS:N_SLOTS, :] = jnp.zeros((N_SLOTS - SLOTS_ALWAYS, D_MODEL), BF16)

    r = _iota_f32((META_ROWS, t), 0)
    meta_ref[0] = jnp.where(r == 0.0, slot1, jnp.where(r == 1.0, slot2,
                            jnp.where(r == 2.0, w1, jnp.where(r == 3.0, w2, 0.0))))
    cnt_ref[0] = jnp.broadcast_to(cnt, (ROUTER_ROWS, LANES))


def _combine_kernel(nbp, x1_ref, gt_ref, meta_ref, ys_ref, fg_ref, yp_ref, ysm_ref, slab):
    i = pl.program_id(0)
    x1 = x1_ref[...]
    meta = meta_ref[0]
    slot1, slot2, w1, w2 = (meta[k:k + 1, :] for k in range(4))
    j = _iota_f32((N_SLOTS, x1.shape[0]), 0)
    pick = jnp.where(j == slot1, w1, 0.0) + jnp.where(j == slot2, w2, 0.0)
    moe = lax.dot_general(pick.astype(BF16), ys_ref[...], _TN, preferred_element_type=F32)
    y = _rms(x1 + gt_ref[0, 0] * moe, fg_ref[...])

    @pl.when(i < nbp)
    def _():
        nb, steps, d = yp_ref.shape
        pitch = nb + STRIDE_PAD
        for j in range(d // LANES):
            for l in range(steps):
                slab[j, l * pitch:l * pitch + nb, :] = y[l * nb:(l + 1) * nb, j * LANES:(j + 1) * LANES]
        for b in range(nb):
            for j in range(d // LANES):
                yp_ref[b, :, j * LANES:(j + 1) * LANES] = slab[j, pl.ds(b, steps, stride=pitch), :]

    @pl.when(i >= nbp)
    def _():
        ysm_ref[...] = y


def _chunk_table_kernel(start_ref, m_ref, src_ref, out_ref):
    start = start_ref[...]
    end = start + m_ref[...]
    offset = src_ref[...] - SEG_ALIGN * start
    total = jnp.max(end, axis=1, keepdims=True)

    def body(b, _):
        rows = pl.ds(pl.multiple_of(b * TABLE_CHUNK_ROWS, TABLE_CHUNK_ROWS), TABLE_CHUNK_ROWS)
        first = lax.convert_element_type(b * TABLE_CHUNK_ROWS, F32)
        i = _iota_f32((TABLE_CHUNK_ROWS, start.shape[1]), 0) + first
        found = jnp.sum(jnp.where(i >= start, jnp.where(i < end, offset, 0.0), 0.0), axis=1, keepdims=True)
        i_col = _iota_f32((TABLE_CHUNK_ROWS, 1), 0) + first
        out_ref[rows, :] = (found + jnp.where(i_col < total, SEG_ALIGN * i_col, 0.0)).astype(jnp.int32)
        return 0

    lax.fori_loop(0, out_ref.shape[0] // TABLE_CHUNK_ROWS, body, 0)


def _expert_kernel(te_ref, tnv_ref, tc0_ref, ctab_ref, nt_ref, xs_hbm, wg_ref, wu_ref, wd_ref,
                   ys_hbm, xbuf, ybuf, wgub, wdb, in_sem, out_sem):
    t = pl.program_id(0)
    nt = nt_ref[0]
    slot = t % 2
    xslot = t % GATHER_BUFFERS

    def chunk_rows(j):
        return pl.ds(pl.multiple_of(j * SEG_ALIGN, SEG_ALIGN), SEG_ALIGN)

    def hbm_rows(ref, r):
        return ref.at[pl.ds(pl.multiple_of(r, SEG_ALIGN), SEG_ALIGN), :]

    def in_copy(r, j, s):
        return pltpu.make_async_copy(hbm_rows(xs_hbm, r), xbuf.at[s, chunk_rows(j), :], in_sem.at[s])

    def out_copy(r, j, s):
        return pltpu.make_async_copy(ybuf.at[s, chunk_rows(j), :], hbm_rows(ys_hbm, r), out_sem.at[s])

    def start_gather(tile, s):
        c0 = tc0_ref[tile]
        nv = tnv_ref[tile]
        for j in range(CHUNKS_PER_TILE):
            in_copy(jnp.where(j < nv, ctab_ref[c0 + j], SPARE_ROW), j, s).start(priority=j % 2)

    def wait_gather(s):
        pltpu.make_async_copy(xs_hbm.at[pl.ds(0, ROW_TILE), :], xbuf.at[s], in_sem.at[s]).wait()

    def wait_scatter(tile, s):
        nv = tnv_ref[tile]

        @pl.when(nv == CHUNKS_PER_TILE)
        def _():
            pltpu.make_async_copy(ybuf.at[s], ys_hbm.at[pl.ds(0, ROW_TILE), :], out_sem.at[s]).wait()

        @pl.when(nv < CHUNKS_PER_TILE)
        def _():
            def body(j, _):
                out_copy(0, j, s).wait()
                return 0

            lax.fori_loop(0, nv, body, 0)

    def run_tile(full):
        ahead = t + GATHER_BUFFERS - 1
        start_gather(ahead, ahead % GATHER_BUFFERS)
        wait_gather(xslot)
        ab = jnp.dot(xbuf[xslot], wgub[...], preferred_element_type=F32)
        a, b = ab[:, :D_EXPERT], ab[:, D_EXPERT:]
        o = jnp.dot((a * _sigmoid(a) * b).astype(BF16), wdb[...], preferred_element_type=F32)
        c0 = tc0_ref[t]
        if full:
            group = ROW_TILE // SCATTER_GROUPS
            for r in range(0, ROW_TILE, group):
                ybuf[slot, r:r + group, :] = o[r:r + group].astype(BF16)
                for j in range(r // SEG_ALIGN, (r + group) // SEG_ALIGN):
                    out_copy(ctab_ref[c0 + j], j, slot).start(priority=j % 2)
        else:
            ybuf[slot] = o.astype(BF16)

            def body(j, _):
                out_copy(ctab_ref[c0 + j], j, slot).start()
                return 0

            lax.fori_loop(0, tnv_ref[t], body, 0)

    @pl.when(t == 0)
    def _():
        for k in range(GATHER_BUFFERS - 1):
            start_gather(k, k)

    @pl.when(jnp.logical_or(t == 0, te_ref[t] != te_ref[jnp.maximum(t - 1, 0)]))
    def _():
        wgub[:, :D_EXPERT] = wg_ref[0].astype(BF16)
        wgub[:, D_EXPERT:] = wu_ref[0].astype(BF16)
        wdb[...] = wd_ref[0].astype(BF16)

    @pl.when(t < nt)
    def _():
        @pl.when(t >= 2)
        def _():
            wait_scatter(t - 2, slot)

        pl.when(tnv_ref[t] == CHUNKS_PER_TILE)(lambda: run_tile(True))
        pl.when(tnv_ref[t] < CHUNKS_PER_TILE)(lambda: run_tile(False))

        @pl.when(t == nt - 1)
        def _():
            for k in range(1, GATHER_BUFFERS):
                wait_gather((t + k) % GATHER_BUFFERS)
            wait_scatter(t, slot)

            @pl.when(t >= 1)
            def _():
                wait_scatter(t - 1, 1 - slot)


def _tile_plan(cnt):
    nblk = cnt.shape[0]
    max_chunks = nblk * (2 * TOK_BLOCK // SEG_ALIGN + N_EXPERTS)
    max_tiles = max_chunks // CHUNKS_PER_TILE + N_EXPERTS
    table_len = -(-(max_chunks + CHUNKS_PER_TILE) // LANES) * LANES
    m = (cnt + SEG_ALIGN - 1) // SEG_ALIGN
    seg_start = jnp.cumsum(m, axis=1) - m
    src0 = jnp.arange(nblk, dtype=jnp.int32)[:, None] * N_SLOTS + SEG_ALIGN * seg_start
    seg_m = m.T.reshape(1, -1)
    seg_first = jnp.cumsum(seg_m, axis=1) - seg_m
    assert table_len % TABLE_CHUNK_ROWS == 0

    def row(v):
        return jnp.pad(v.astype(F32), ((0, 0), (0, -v.shape[1] % LANES)))

    chunk_src = pl.pallas_call(
        _chunk_table_kernel,
        out_shape=jax.ShapeDtypeStruct((table_len, 1), jnp.int32),
        name="moe_chunk_table",
    )(row(seg_first), row(seg_m), row(src0.T.reshape(1, -1))).reshape(-1)
    ce = jnp.sum(m, axis=0)
    c_start = jnp.cumsum(ce) - ce
    tiles = (ce + CHUNKS_PER_TILE - 1) // CHUNKS_PER_TILE
    t_cum = jnp.cumsum(tiles)
    nt = t_cum[-1]
    k = jnp.arange(max_tiles + GATHER_BUFFERS - 1, dtype=jnp.int32)
    kk = jnp.minimum(k, nt - 1)
    ek = jnp.minimum(jnp.sum((t_cum[None, :] <= kk[:, None]).astype(jnp.int32), axis=1), N_EXPERTS - 1)
    is_e = ek[:, None] == jnp.arange(N_EXPERTS, dtype=jnp.int32)[None, :]
    pick = lambda v: jnp.sum(jnp.where(is_e, v[None, :], 0), axis=1)
    local = kk - pick(t_cum - tiles)
    tile_nv = jnp.where(k < nt, jnp.clip(pick(ce) - CHUNKS_PER_TILE * local, 0, CHUNKS_PER_TILE), 0)
    tile_c0 = jnp.where(k < nt, pick(c_start) + CHUNKS_PER_TILE * local, 0)
    return tuple(v.astype(jnp.int32) for v in (ek, tile_nv, tile_c0, chunk_src, nt.reshape(1)))


def _moe(x1, bp, mod_rows, n2g, wr, br, wg, wu, wd, fg):
    nblk = x1.shape[0] // TOK_BLOCK
    nbp = nblk - 1
    blk = (TOK_BLOCK, D_MODEL)
    x1_spec = pl.BlockSpec(blk, lambda i: (i, 0))

    def mod_spec(k):
        return pl.BlockSpec((1, 1) + blk, lambda i: (k, jnp.minimum(i // nbp, 1), 0, 0))

    sh_spec, sc_spec, gt_spec = mod_spec(0), mod_spec(1), mod_spec(2)
    meta_spec = pl.BlockSpec((1, META_ROWS, TOK_BLOCK), lambda i: (i, 0, 0))
    slots_spec = pl.BlockSpec((N_SLOTS, D_MODEL), lambda i: (i, 0))
    params = pltpu.CompilerParams(dimension_semantics=("arbitrary",), vmem_limit_bytes=VMEM_LIMIT)

    def cs(shape):
        return pl.BlockSpec(shape, lambda i: (0,) * len(shape))

    xs, meta, cnt = pl.pallas_call(
        _dispatch_kernel,
        grid=(nblk,),
        in_specs=[x1_spec, sc_spec, sh_spec, cs((1, D_MODEL)),
                  cs((ROUTER_ROWS, D_MODEL)), cs((ROUTER_ROWS, TOK_BLOCK))],
        out_specs=[slots_spec, meta_spec,
                   pl.BlockSpec((1, ROUTER_ROWS, LANES), lambda i: (i, 0, 0))],
        out_shape=[jax.ShapeDtypeStruct((nblk * N_SLOTS, D_MODEL), BF16),
                   jax.ShapeDtypeStruct((nblk, META_ROWS, TOK_BLOCK), F32),
                   jax.ShapeDtypeStruct((nblk, ROUTER_ROWS, LANES), F32)],
        compiler_params=params,
        name="moe_dispatch",
    )(x1, mod_rows, mod_rows, n2g, wr, br)

    plan = _tile_plan(cnt[:, N_EXPERT_GROUPS:N_EXPERT_GROUPS + N_EXPERTS, 0].astype(jnp.int32))
    max_tiles = plan[0].shape[0] - (GATHER_BUFFERS - 1)

    def w_spec(shape):
        return pl.BlockSpec((1,) + shape, lambda t, te, *_: (te[t], 0, 0))

    ys = pl.pallas_call(
        _expert_kernel,
        grid_spec=pltpu.PrefetchScalarGridSpec(
            num_scalar_prefetch=len(plan),
            grid=(max_tiles,),
            in_specs=[pl.BlockSpec(memory_space=pl.ANY),
                      w_spec((D_MODEL, D_EXPERT)), w_spec((D_MODEL, D_EXPERT)), w_spec((D_EXPERT, D_MODEL))],
            out_specs=pl.BlockSpec(memory_space=pl.ANY),
            scratch_shapes=[pltpu.VMEM((GATHER_BUFFERS, ROW_TILE, D_MODEL), BF16),
                            pltpu.VMEM((2, ROW_TILE, D_MODEL), BF16),
                            pltpu.VMEM((D_MODEL, 2 * D_EXPERT), BF16),
                            pltpu.VMEM((D_EXPERT, D_MODEL), BF16),
                            pltpu.SemaphoreType.DMA((GATHER_BUFFERS,)), pltpu.SemaphoreType.DMA((2,))]),
        out_shape=jax.ShapeDtypeStruct(xs.shape, BF16),
        input_output_aliases={len(plan): 0},
        compiler_params=params,
        name="moe_experts",
    )(*plan, xs, wg, wu, wd)

    return pl.pallas_call(
        functools.partial(_combine_kernel, nbp),
        grid=(nblk,),
        in_specs=[x1_spec, gt_spec, meta_spec, slots_spec, cs((1, D_MODEL))],
        out_specs=[pl.BlockSpec((bp, TOK_BLOCK // bp, D_MODEL), lambda i: (0, jnp.minimum(i, nbp - 1), 0)),
                   pl.BlockSpec(blk, lambda i: (0, 0))],
        out_shape=[jax.ShapeDtypeStruct((bp, nbp * TOK_BLOCK // bp, D_MODEL), F32),
                   jax.ShapeDtypeStruct(blk, F32)],
        scratch_shapes=[pltpu.VMEM((D_MODEL // LANES, TOK_BLOCK // bp * (bp + STRIDE_PAD), LANES), F32)],
        compiler_params=params,
        name="moe_combine",
    )(x1, mod_rows, meta, ys, fg)


def _to_time_major(x):
    b, s, d = x.shape
    return x.transpose(1, 0, 2).reshape(s * b, d)


def _from_time_major(x, b, s):
    return x.reshape(s, b, -1).transpose(1, 0, 2)


def kernel(x_prompt, x_sample, state_conv, state_ssm_re, state_ssm_im, c_prompt, c_sample, w_ada, b_ada, norm1_g, norm2_g, w_in, conv_w, lambda_re, lambda_im, log_dt, ssm_b_re, ssm_b_im, ssm_c_re, ssm_c_im, ssm_d, w_glu, b_glu, out_norm_conv_g, out_norm_ssm_g, w_out, w_router_group, b_router_group, w_router_expert, b_router_expert, w_expert_gate, w_expert_up, w_expert_down, final_norm_g):
    assert w_ada.shape[0] == 1, "single-layer step"
    bp, sp, _ = x_prompt.shape
    bs, ss, _ = x_sample.shape

    ar, ai, bbr, bbi = _discretise(lambda_re[0], lambda_im[0], log_dt[0], ssm_b_re[0], ssm_b_im[0])
    crt = ssm_c_re[0].transpose(1, 0, 2).reshape(SSM_GROUP, N_STATE)
    cit = ssm_c_im[0].transpose(1, 0, 2).reshape(SSM_GROUP, N_STATE)

    assert bs % bp == 0
    mod, mod_rows = _adaln(jnp.concatenate([c_sample, c_prompt], axis=0), bs, TOK_BLOCK, w_ada[0], b_ada[0])

    mixer_weights = (
        norm1_g[0].reshape(1, -1), w_in[0], conv_w[0], ar, ai, bbr, bbi, crt, cit,
        ssm_d[0].reshape(1, -1), w_glu[0], b_glu[0].reshape(1, -1),
        out_norm_conv_g[0].reshape(1, -1), out_norm_ssm_g[0].reshape(1, -1), w_out[0],
    )

    pad_rows = ROUTER_ROWS - N_EXPERT_GROUPS - N_EXPERTS
    w_router = jnp.concatenate(
        [w_router_group[0].T, w_router_expert[0].transpose(0, 2, 1).reshape(N_EXPERTS, D_MODEL),
         jnp.zeros((pad_rows, D_MODEL), F32)], axis=0)
    b_router = jnp.concatenate(
        [b_router_group[0], b_router_expert[0].reshape(-1), jnp.zeros((pad_rows,), F32)])
    b_router = jnp.broadcast_to(b_router[:, None], (ROUTER_ROWS, TOK_BLOCK))
    moe_weights = (norm2_g[0].reshape(1, -1), w_router, b_router,
                   w_expert_gate[0], w_expert_up[0], w_expert_down[0], final_norm_g.reshape(1, -1))

    assert bs * ss == TOK_BLOCK and (bp * sp) % TOK_BLOCK == 0 and TOK_BLOCK % bp == 0

    def mix(x, mod_row0, h0r, h0i, cb0, chunk, in_kernel_transpose, **x1_where):
        nb, s, _ = x.shape
        x1, cbo, hro, hio = _mixer(
            x if in_kernel_transpose else _to_time_major(x), nb, s, chunk,
            mod, mod_row0, h0r, h0i, cb0, mixer_weights, **x1_where)
        new_conv = cbo.reshape(2, nb, CONV_DIM).transpose(1, 0, 2)[None]
        new_re = hro.reshape(1, nb, SSM_GROUPS, SSM_STATE)
        new_im = hio.reshape(1, nb, SSM_GROUPS, SSM_STATE)
        return x1, new_conv, new_re, new_im

    zero_state = jnp.zeros((bp, N_STATE), F32)
    zero_conv = jnp.zeros((2 * bp, CONV_DIM), F32)
    x1, conv_p, re_p, im_p = mix(x_prompt, bs, zero_state, zero_state, zero_conv, TOK_BLOCK // bp, True,
                                 x1_rows=bp * sp + bs * ss)
    x1, conv_s, re_s, im_s = mix(
        x_sample, 0,
        state_ssm_re[0].reshape(bs, N_STATE), state_ssm_im[0].reshape(bs, N_STATE),
        state_conv[0].transpose(1, 0, 2).reshape(2 * bs, CONV_DIM), ss, False, x1_dest=x1)

    y_p, y_s = _moe(x1, bp, mod_rows, *moe_weights)
    return (y_p, _from_time_major(y_s, bs, ss), conv_p, re_p, im_p, conv_s, re_s, im_s)
```
